```python
import jax, jax.numpy as jnp
from jax import lax
import numpy as np

D_MODEL = 2048
BATCH = 4
SEQ = 4096
DEPTH = 1

CHUNK = 64
FOX_HEADS = 8
FOX_HEAD_DIM = 128
FOX_WIDTH = FOX_HEADS * FOX_HEAD_DIM
QUERY_BLOCK = 128
RWKV_HEADS = 16
RWKV_HEAD_DIM = 64
RWKV_WIDTH = RWKV_HEADS * RWKV_HEAD_DIM
W_LORA = 96
A_LORA = 96
G_LORA = 256
D_MIX = FOX_WIDTH + RWKV_WIDTH
FOX_COLS = 4 * FOX_WIDTH + FOX_HEADS
RWKV_COLS = 3 * RWKV_WIDTH + W_LORA + A_LORA + G_LORA
IN_COLS = FOX_COLS + RWKV_COLS
PEER_HEADS = 8
PEER_NKEYS = 128
PEER_N_EXPERTS = PEER_NKEYS * PEER_NKEYS
PEER_DK = 256
PEER_DK_HALF = PEER_DK // 2
PEER_TOPK = 16
PEER_TOKEN_BLOCK = 128
NORM_EPS = 1e-6
GN_EPS = 64e-5

kernel_name = "hybrid_fox_rwkv7_peer_block"


def rms_norm(x, w):
    xf = x.astype(jnp.float32)
    y = xf * lax.rsqrt(jnp.mean(xf * xf, axis=-1, keepdims=True) + NORM_EPS)
    return (y * w.astype(jnp.float32)).astype(x.dtype)


def fox_mixer(cols, q_norm_w, k_norm_w, f_bias):
    B, S, _ = cols.shape
    q, k, v, gate = [cols[..., i * FOX_WIDTH:(i + 1) * FOX_WIDTH] for i in range(4)]
    f_logit = cols[..., 4 * FOX_WIDTH:] + f_bias
    heads = lambda t: t.reshape(B, S, FOX_HEADS, FOX_HEAD_DIM).transpose(0, 2, 1, 3)
    q = rms_norm(heads(q), q_norm_w)
    k = rms_norm(heads(k), k_norm_w)
    v = heads(v)
    log_f = jax.nn.log_sigmoid(f_logit.astype(jnp.float32)).transpose(0, 2, 1)
    cum = jnp.cumsum(log_f, axis=-1)
    scale = FOX_HEAD_DIM ** -0.5
    outs = []
    for i in range(S // QUERY_BLOCK):
        lo, hi = i * QUERY_BLOCK, (i + 1) * QUERY_BLOCK
        s = jnp.einsum('bhqd,bhkd->bhqk', q[:, :, lo:hi], k[:, :, :hi]).astype(jnp.float32) * scale
        s = s + (cum[:, :, lo:hi, None] - cum[:, :, None, :hi])
        causal = jnp.arange(lo, hi)[:, None] >= jnp.arange(hi)[None, :]
        p = jax.nn.softmax(jnp.where(causal, s, -jnp.inf), axis=-1)
        outs.append(jnp.einsum('bhqk,bhkd->bhqd', p.astype(v.dtype), v[:, :, :hi]))
    o = jnp.concatenate(outs, axis=2).transpose(0, 2, 1, 3).reshape(B, S, FOX_WIDTH)
    return o * jax.nn.sigmoid(gate)


def rwkv7_recurrence(r, w, k, v, a, b):
    B, S, H, N = r.shape
    to_chunks = lambda t: t.astype(jnp.float32).transpose(1, 0, 2, 3).reshape(S // CHUNK, CHUNK, B, H, N)

    def step(state, inp):
        r_t, w_t, k_t, v_t, a_t, b_t = inp
        sa = jnp.einsum('bhij,bhj->bhi', state, a_t)
        state = state * w_t[:, :, None, :] + sa[..., None] * b_t[:, :, None, :] + v_t[..., None] * k_t[:, :, None, :]
        return state, jnp.einsum('bhij,bhj->bhi', state, r_t)

    def chunk_step(state, chunk_inp):
        return lax.scan(step, state, chunk_inp)

    state0 = jnp.zeros((B, H, N, N), jnp.float32)
    _, y = lax.scan(chunk_step, state0, tuple(to_chunks(t) for t in (r, w, k, v, a, b)))
    return y.reshape(S, B, H, N).transpose(1, 0, 2, 3)


def rwkv7_mixer(cols, mu, w0, w_up, a0, a_up, g_up, k_k, k_a, r_k, ln_w, ln_b):
    B, S, _ = cols.shape
    prev = jnp.pad(cols, ((0, 0), (1, 0), (0, 0)))[:, :-1]
    cols = cols + (prev - cols) * mu
    W = RWKV_WIDTH
    r, k, v = cols[..., :W], cols[..., W:2 * W], cols[..., 2 * W:3 * W]
    wd = cols[..., 3 * W:3 * W + W_LORA]
    ad = cols[..., 3 * W + W_LORA:3 * W + W_LORA + A_LORA]
    gd = cols[..., 3 * W + W_LORA + A_LORA:]
    w_raw = -jax.nn.softplus(-(w0 + jnp.tanh(wd) @ w_up).astype(jnp.float32)) - 0.5
    decay = jnp.exp(-jnp.exp(w_raw))
    a = jax.nn.sigmoid((a0 + ad @ a_up).astype(jnp.float32))
    g = jax.nn.sigmoid(gd) @ g_up
    heads = lambda t: t.reshape(B, S, RWKV_HEADS, RWKV_HEAD_DIM)
    kk = heads((k * k_k).astype(jnp.float32))
    kk = kk / jnp.maximum(jnp.sqrt(jnp.sum(kk * kk, axis=-1, keepdims=True)), 1e-12)
    k = k * (1 + (a - 1) * k_a).astype(k.dtype)
    r_h, k_h, v_h, a_h = heads(r), heads(k), heads(v), heads(a)
    y = rwkv7_recurrence(r_h, heads(decay), k_h, v_h, -kk, kk * a_h)
    mean = jnp.mean(y, axis=-1, keepdims=True)
    var = jnp.mean(jnp.square(y - mean), axis=-1, keepdims=True)
    y = ((y - mean) * lax.rsqrt(var + GN_EPS)).reshape(B, S, W)
    y = (y * ln_w.astype(jnp.float32) + ln_b.astype(jnp.float32)).astype(cols.dtype)
    bonus = jnp.sum(r_h * k_h * r_k, axis=-1, keepdims=True) * v_h
    y = y + bonus.reshape(B, S, W)
    return y * g


def peer_route(h, w_query, sub_keys):
    T = h.shape[0]
    q = (h @ w_query).reshape(T, PEER_HEADS, 2, PEER_DK_HALF).astype(jnp.float32)
    s = jnp.einsum('thpd,hpkd->thpk', q, sub_keys.astype(jnp.float32))
    top_s, top_i = lax.top_k(s, PEER_TOPK)
    cand_s = (top_s[:, :, 0, :, None] + top_s[:, :, 1, None, :]).reshape(T, PEER_HEADS, PEER_TOPK * PEER_TOPK)
    cand_i = (top_i[:, :, 0, :, None] * PEER_NKEYS + top_i[:, :, 1, None, :]).reshape(T, PEER_HEADS, PEER_TOPK * PEER_TOPK)
    best_s, best_pos = lax.top_k(cand_s, PEER_TOPK)
    experts = jnp.take_along_axis(cand_i, best_pos, axis=-1)
    gates = jax.nn.softmax(best_s, axis=-1)
    return experts.reshape(T, PEER_HEADS * PEER_TOPK), gates.reshape(T, PEER_HEADS * PEER_TOPK)


def peer_experts(h, experts, gates, u, v):
    T, D = h.shape
    E = experts.shape[-1]
    nb = T // PEER_TOKEN_BLOCK

    def block(args):
        hb, eb, gb = args
        act = jax.nn.gelu(jnp.einsum('ted,td->te', u[eb], hb), approximate=False)
        return jnp.einsum('te,ted->td', (gb * act).astype(v.dtype), v[eb])

    out = lax.map(block, (h.reshape(nb, PEER_TOKEN_BLOCK, D),
                          experts.reshape(nb, PEER_TOKEN_BLOCK, E),
                          gates.reshape(nb, PEER_TOKEN_BLOCK, E)))
    return out.reshape(T, D)


def hybrid_layer(x, c, w_ada, b_ada, norm_mix_w, w_in, fox_q_norm_w, fox_k_norm_w, fox_f_bias,
                 rwkv_mu, rwkv_w0, rwkv_w_up, rwkv_a0, rwkv_a_up, rwkv_g_up, rwkv_k_k, rwkv_k_a,
                 rwkv_r_k, rwkv_ln_w, rwkv_ln_b, w_out, norm_ffn_w, peer_w_query, peer_sub_keys,
                 peer_u, peer_v):
    B, S, D = x.shape
    mod = jax.nn.silu(c) @ w_ada + b_ada
    sh1, sc1, g1, sh2, sc2, g2 = jnp.split(mod, 6, axis=-1)
    h = rms_norm(x, norm_mix_w) * (1 + sc1[:, None]) + sh1[:, None]
    proj = h @ w_in
    y_fox = fox_mixer(proj[..., :FOX_COLS], fox_q_norm_w, fox_k_norm_w, fox_f_bias)
    y_rwkv = rwkv7_mixer(proj[..., FOX_COLS:], rwkv_mu, rwkv_w0, rwkv_w_up, rwkv_a0, rwkv_a_up,
                         rwkv_g_up, rwkv_k_k, rwkv_k_a, rwkv_r_k, rwkv_ln_w, rwkv_ln_b)
    mix = jnp.concatenate([y_fox, y_rwkv], axis=-1) @ w_out
    x = x + g1[:, None] * mix
    h2 = rms_norm(x, norm_ffn_w) * (1 + sc2[:, None]) + sh2[:, None]
    tok = h2.reshape(B * S, D)
    experts, gates = peer_route(tok, peer_w_query, peer_sub_keys)
    ffn = peer_experts(tok, experts, gates, peer_u, peer_v).reshape(B, S, D)
    return x + g2[:, None] * ffn


def setup_inputs(seed: int = 0) -> dict:
    key = jax.random.key(seed)
    ks = jax.random.split(key, 28)
    n = lambda k, shape, s: jax.random.normal(k, shape, jnp.float32) * s
    L = DEPTH
    return {
        "x": n(ks[0], (BATCH, SEQ, D_MODEL), 1.0),
        "c": n(ks[1], (BATCH, D_MODEL), 1.0),
        "w_ada": n(ks[2], (L, D_MODEL, 6 * D_MODEL), D_MODEL ** -0.5),
        "b_ada": n(ks[3], (L, 6 * D_MODEL), 0.02),
        "norm_mix_w": 1.0 + n(ks[4], (L, D_MODEL), 0.02),
        "w_in": n(ks[5], (L, D_MODEL, IN_COLS), D_MODEL ** -0.5),
        "fox_q_norm_w": 1.0 + n(ks[6], (L, FOX_HEAD_DIM), 0.02),
        "fox_k_norm_w": 1.0 + n(ks[7], (L, FOX_HEAD_DIM), 0.02),
        "fox_f_bias": jax.random.uniform(ks[8], (L, FOX_HEADS), jnp.float32, 2.0, 5.0),
        "rwkv_mu": jax.random.uniform(ks[9], (L, RWKV_COLS), jnp.float32, 0.0, 1.0),
        "rwkv_w0": jax.random.uniform(ks[10], (L, RWKV_WIDTH), jnp.float32, -5.0, 1.0),
        "rwkv_w_up": n(ks[11], (L, W_LORA, RWKV_WIDTH), W_LORA ** -0.5),
        "rwkv_a0": n(ks[12], (L, RWKV_WIDTH), 0.1),
        "rwkv_a_up": n(ks[13], (L, A_LORA, RWKV_WIDTH), A_LORA ** -0.5),
        "rwkv_g_up": n(ks[14], (L, G_LORA, RWKV_WIDTH), G_LORA ** -0.5),
        "rwkv_k_k": 0.85 + n(ks[15], (L, RWKV_WIDTH), 0.05),
        "rwkv_k_a": 1.0 + n(ks[16], (L, RWKV_WIDTH), 0.05),
        "rwkv_r_k": n(ks[17], (L, RWKV_HEADS, RWKV_HEAD_DIM), 0.1),
        "rwkv_ln_w": 1.0 + n(ks[18], (L, RWKV_WIDTH), 0.02),
        "rwkv_ln_b": n(ks[19], (L, RWKV_WIDTH), 0.02),
        "w_out": n(ks[20], (L, D_MIX, D_MODEL), D_MIX ** -0.5),
        "norm_ffn_w": 1.0 + n(ks[21], (L, D_MODEL), 0.02),
        "peer_w_query": n(ks[22], (L, D_MODEL, PEER_HEADS * PEER_DK), D_MODEL ** -0.5),
        "peer_sub_keys": n(ks[23], (L, PEER_HEADS, 2, PEER_NKEYS, PEER_DK_HALF), PEER_DK_HALF ** -0.5),
        "peer_u": n(ks[24], (L, PEER_N_EXPERTS, D_MODEL), D_MODEL ** -0.5),
        "peer_v": n(ks[25], (L, PEER_N_EXPERTS, D_MODEL), PEER_HEADS ** -0.5),
    }


def reference(x, c, w_ada, b_ada, norm_mix_w, w_in, fox_q_norm_w, fox_k_norm_w, fox_f_bias,
              rwkv_mu, rwkv_w0, rwkv_w_up, rwkv_a0, rwkv_a_up, rwkv_g_up, rwkv_k_k, rwkv_k_a,
              rwkv_r_k, rwkv_ln_w, rwkv_ln_b, w_out, norm_ffn_w, peer_w_query, peer_sub_keys,
              peer_u, peer_v):
    for l in range(DEPTH):
        x = hybrid_layer(x, c, w_ada[l], b_ada[l], norm_mix_w[l], w_in[l], fox_q_norm_w[l],
                         fox_k_norm_w[l], fox_f_bias[l], rwkv_mu[l], rwkv_w0[l], rwkv_w_up[l],
                         rwkv_a0[l], rwkv_a_up[l], rwkv_g_up[l], rwkv_k_k[l], rwkv_k_a[l],
                         rwkv_r_k[l], rwkv_ln_w[l], rwkv_ln_b[l], w_out[l], norm_ffn_w[l],
                         peer_w_query[l], peer_sub_keys[l], peer_u[l], peer_v[l])
    return x
```

```python
import functools

import jax
import jax.numpy as jnp
from jax import lax
from jax.experimental import pallas as pl
from jax.experimental.pallas import tpu as pltpu

F32 = jnp.float32
BF16 = jnp.bfloat16
HIGHEST = lax.Precision.HIGHEST

LANES = 128
NORM_EPS = 1e-6
GN_EPS = 64e-5
CHUNK = 64
FOX_HEADS = 8
FOX_HEAD_DIM = 128
RWKV_HEADS = 16
RWKV_HEAD_DIM = 64
PEER_HEADS = 8
PEER_NKEYS = 128
PEER_TOPK = 16
NEG_BIG = -1e30
VMEM_LIMIT = 56 * 1024 * 1024


def _cparams(sem):
    return pltpu.CompilerParams(dimension_semantics=sem, vmem_limit_bytes=VMEM_LIMIT)


def _dot(a, b, precision=None):
    return jnp.dot(a, b, preferred_element_type=F32, precision=precision)


def _dot_nt(a, b, precision=None):
    return lax.dot_general(a, b, (((1,), (1,)), ((), ())), preferred_element_type=F32,
                           precision=precision)


def _dot_tn(a, b, precision=None):
    return lax.dot_general(a, b, (((0,), (0,)), ((), ())), preferred_element_type=F32,
                           precision=precision)


def _ada_kernel(c_ref, w_ref, b_ref, o_ref):
    c = c_ref[...]
    s = c * jax.nn.sigmoid(c)
    o_ref[...] = _dot(s, w_ref[...], HIGHEST) + b_ref[...]


def _ada_mod(c_pad, w_ada, b_ada):
    rows, d = c_pad.shape
    n = w_ada.shape[1]
    bn = 1024
    return pl.pallas_call(
        _ada_kernel,
        grid=(n // bn,),
        in_specs=[pl.BlockSpec((rows, d), lambda j: (0, 0)),
                  pl.BlockSpec((d, bn), lambda j: (0, j)),
                  pl.BlockSpec((1, bn), lambda j: (0, j))],
        out_specs=pl.BlockSpec((rows, bn), lambda j: (0, j)),
        out_shape=jax.ShapeDtypeStruct((rows, n), F32),
        compiler_params=_cparams(("arbitrary",)),
        name="ada_mod",
    )(c_pad, w_ada, b_ada.reshape(1, n))


def _modulated_norm(x, nw, sc, sh):
    y = x * lax.rsqrt(jnp.mean(x * x, axis=-1, keepdims=True) + NORM_EPS)
    return y * nw * (1.0 + sc) + sh


def _in_proj_kernel(x_ref, nw_ref, sc_ref, sh_ref, w_ref, hw_ref, o_ref, h_scr, *, n_qk_blocks):
    j = pl.program_id(1)

    @pl.when(j == 0)
    def _():
        h = _modulated_norm(x_ref[...], nw_ref[...], sc_ref[0], sh_ref[0])
        h_scr[...] = h.astype(BF16)

    acc = _dot(h_scr[...], w_ref[...])

    @pl.when(j < n_qk_blocks)
    def _():
        for hh in range(acc.shape[1] // FOX_HEAD_DIM):
            sl = slice(hh * FOX_HEAD_DIM, (hh + 1) * FOX_HEAD_DIM)
            a = acc[:, sl]
            rs = lax.rsqrt(jnp.mean(a * a, axis=-1, keepdims=True) + NORM_EPS)
            o_ref[:, sl] = a * rs * hw_ref[:, sl]

    @pl.when(j >= n_qk_blocks)
    def _():
        o_ref[...] = acc


def _in_proj(x2, norm_w, sc, sh, w_bf, head_w, tokens_per_batch, n_qk_cols):
    t, d = x2.shape
    n = w_bf.shape[1]
    tm = min(1024, tokens_per_batch)
    bn = 512
    tpb = tokens_per_batch // tm
    kern = functools.partial(_in_proj_kernel, n_qk_blocks=n_qk_cols // bn)
    return pl.pallas_call(
        kern,
        grid=(t // tm, n // bn),
        in_specs=[pl.BlockSpec((tm, d), lambda i, j: (i, 0)),
                  pl.BlockSpec((1, d), lambda i, j: (0, 0)),
                  pl.BlockSpec((1, 1, d), lambda i, j: (i // tpb, 0, 0)),
                  pl.BlockSpec((1, 1, d), lambda i, j: (i // tpb, 0, 0)),
                  pl.BlockSpec((d, bn), lambda i, j: (0, j)),
                  pl.BlockSpec((1, bn), lambda i, j: (0, j))],
        out_specs=pl.BlockSpec((tm, bn), lambda i, j: (i, j)),
        out_shape=jax.ShapeDtypeStruct((t, n), F32),
        scratch_shapes=[pltpu.VMEM((tm, d), BF16)],
        compiler_params=_cparams(("parallel", "arbitrary")),
        name="in_proj",
    )(x2, norm_w, sc, sh, w_bf, head_w)


def _log_sigmoid(x):
    return jnp.minimum(x, 0.0) - jnp.log(1.0 + jnp.exp(-jnp.abs(x)))


def _fox_cum_kernel(f_ref, b_ref, o_ref, *, blk):
    s = f_ref.shape[0]
    row = lax.broadcasted_iota(jnp.int32, (blk, blk), 0)
    col = lax.broadcasted_iota(jnp.int32, (blk, blk), 1)
    tri = (row >= col).astype(F32)
    carry = jnp.zeros((1, f_ref.shape[1]), F32)
    for i in range(s // blk):
        lf = _log_sigmoid(f_ref[i * blk:(i + 1) * blk, :] + b_ref[...])
        cs = _dot(tri, lf, HIGHEST) + carry
        o_ref[i * blk:(i + 1) * blk, :] = cs
        carry = cs[blk - 1:blk, :]


def _fox_cum(proj, f_bias_row, batch, seq, col_block):
    blk = min(256, seq)
    return pl.pallas_call(
        functools.partial(_fox_cum_kernel, blk=blk),
        grid=(batch,),
        in_specs=[pl.BlockSpec((seq, LANES), lambda b: (b, col_block)),
                  pl.BlockSpec((1, LANES), lambda b: (0, 0))],
        out_specs=pl.BlockSpec((seq, LANES), lambda b: (b, 0)),
        out_shape=jax.ShapeDtypeStruct((batch * seq, LANES), F32),
        compiler_params=_cparams(("parallel",)),
        name="fox_cum",
    )(proj, f_bias_row)


def _fox_attn_kernel(q_ref, k_ref, v_ref, c_ref, o_ref, m_scr, l_scr, acc_scr):
    qi = pl.program_id(2)
    ki = pl.program_id(3)

    @pl.when(ki == 0)
    def _():
        m_scr[...] = jnp.full(m_scr.shape, NEG_BIG, F32)
        l_scr[...] = jnp.zeros(l_scr.shape, F32)
        acc_scr[...] = jnp.zeros(acc_scr.shape, F32)

    def step(masked):
        q = q_ref[...].astype(BF16)
        k = k_ref[...].astype(BF16)
        s = _dot_nt(q, k) - c_ref[0, 0]
        if masked:
            row = lax.broadcasted_iota(jnp.int32, s.shape, 0)
            col = lax.broadcasted_iota(jnp.int32, s.shape, 1)
            s = jnp.where(row >= col, s, NEG_BIG)
        m_prev = m_scr[...]
        m_new = jnp.maximum(m_prev, jnp.max(s, axis=-1, keepdims=True))
        alpha = jnp.exp(m_prev - m_new)
        p = jnp.exp(s - m_new)
        l_scr[...] = alpha * l_scr[...] + jnp.sum(p, axis=-1, keepdims=True)
        acc_scr[...] = alpha * acc_scr[...] + _dot(p.astype(BF16), v_ref[...].astype(BF16))
        m_scr[...] = m_new

    @pl.when(ki < qi)
    def _():
        step(False)

    @pl.when(ki == qi)
    def _():
        step(True)
        o_ref[...] = acc_scr[...] / l_scr[...]


def _fox_attn(proj, cum_rows, batch, seq):
    tq = min(512, seq)
    nq = seq // tq
    hd = FOX_HEAD_DIM
    return pl.pallas_call(
        _fox_attn_kernel,
        grid=(batch, FOX_HEADS, nq, nq),
        in_specs=[
            pl.BlockSpec((tq, hd), lambda b, h, qi, ki: (b * nq + qi, h)),
            pl.BlockSpec((tq, hd), lambda b, h, qi, ki: (b * nq + jnp.minimum(ki, qi), FOX_HEADS + h)),
            pl.BlockSpec((tq, hd), lambda b, h, qi, ki: (b * nq + jnp.minimum(ki, qi), 2 * FOX_HEADS + h)),
            pl.BlockSpec((1, 1, 1, tq), lambda b, h, qi, ki: (b, h, 0, jnp.minimum(ki, qi))),
        ],
        out_specs=pl.BlockSpec((tq, hd), lambda b, h, qi, ki: (b * nq + qi, h)),
        out_shape=jax.ShapeDtypeStruct((batch * seq, FOX_HEADS * hd), F32),
        scratch_shapes=[pltpu.VMEM((tq, 1), F32), pltpu.VMEM((tq, 1), F32), pltpu.VMEM((tq, hd), F32)],
        compiler_params=_cparams(("parallel", "parallel", "parallel", "arbitrary")),
        name="fox_attn",
    )(proj, proj, proj, cum_rows)


def _head_sum(x, bd):
    parts = [_dot(x[:, j * LANES:(j + 1) * LANES], bd, HIGHEST) for j in range(x.shape[1] // LANES)]
    return jnp.concatenate(parts, axis=1)


def _head_block_diag():
    r = lax.broadcasted_iota(jnp.int32, (LANES, LANES), 0) // RWKV_HEAD_DIM
    c = lax.broadcasted_iota(jnp.int32, (LANES, LANES), 1) // RWKV_HEAD_DIM
    return (r == c).astype(F32)


def _rwkv_prep_kernel(r_ref, k_ref, v_ref, gd_ref, lo_ref, pr_ref, pk_ref, pv_ref, pgd_ref, plo_ref,
                      mu_r, mu_k, mu_v, mu_gd, mu_lo, w0_ref, wup_ref, a0_ref, aup_ref, gup_ref,
                      kk_ref, ka_ref, rk_ref,
                      or_ref, olw_ref, ok_ref, ov_ref, oa_ref, ob_ref, og_ref, obonus_ref, *, tiles_per_batch):
    i = pl.program_id(0)
    first = (i % tiles_per_batch) == 0

    def shifted(cur_ref, prv_ref, mu_ref):
        cur = cur_ref[...]
        last = jnp.where(first, 0.0, prv_ref[7:8, :])
        row = lax.broadcasted_iota(jnp.int32, cur.shape, 0)
        prev = jnp.where(row == 0, last, pltpu.roll(cur, 1, 0))
        return cur + (prev - cur) * mu_ref[...]

    r = shifted(r_ref, pr_ref, mu_r)
    k = shifted(k_ref, pk_ref, mu_k)
    v = shifted(v_ref, pv_ref, mu_v)
    gd = shifted(gd_ref, pgd_ref, mu_gd)
    lo = shifted(lo_ref, plo_ref, mu_lo)
    wd = lo[:, :LANES]
    ad = lo[:, LANES:]

    w_pre = w0_ref[...] + _dot(jnp.tanh(wd), wup_ref[...], HIGHEST)
    w_raw = _log_sigmoid(w_pre) - 0.5
    log_decay = -jnp.exp(w_raw)
    a = jax.nn.sigmoid(a0_ref[...] + _dot(ad, aup_ref[...], HIGHEST))
    g = _dot(jax.nn.sigmoid(gd), gup_ref[...], HIGHEST)

    bd = _head_block_diag()
    kk = k * kk_ref[...]
    nrm = jnp.maximum(jnp.sqrt(_head_sum(kk * kk, bd)), 1e-12)
    kk = kk / nrm
    k_mod = k * (1.0 + (a - 1.0) * ka_ref[...])
    bonus = _head_sum(r * k_mod * rk_ref[...], bd) * v

    or_ref[...] = r
    olw_ref[...] = log_decay
    ok_ref[...] = k_mod
    ov_ref[...] = v
    oa_ref[...] = -kk
    ob_ref[...] = kk * a
    og_ref[...] = g
    obonus_ref[...] = bonus


def _rwkv_prep(proj, cols, mus, w0, w_up, a0, a_up, g_up, k_k, k_a, r_k, tokens_per_batch):
    t = proj.shape[0]
    w = RWKV_HEADS * RWKV_HEAD_DIM
    tm = min(256, tokens_per_batch)
    tpb = tokens_per_batch // tm
    widths = [w, w, w, 256, 256]
    offs = [cols["rr"], cols["rk"], cols["rv"], cols["gd"], cols["lo"]]
    cur_specs = [pl.BlockSpec((tm, wd), functools.partial(lambda i, cb: (i, cb), cb=o // wd))
                 for wd, o in zip(widths, offs)]
    prv_specs = [pl.BlockSpec((8, wd), functools.partial(
        lambda i, cb: (jnp.maximum(i * (tm // 8) - 1, 0), cb), cb=o // wd))
        for wd, o in zip(widths, offs)]
    full = lambda a: pl.BlockSpec(a.shape, lambda i: (0,) * a.ndim)
    params = list(mus) + [w0, w_up, a0, a_up, g_up, k_k, k_a, r_k]
    out_spec = pl.BlockSpec((tm, w), lambda i: (i, 0))
    return pl.pallas_call(
        functools.partial(_rwkv_prep_kernel, tiles_per_batch=tpb),
        grid=(t // tm,),
        in_specs=cur_specs + prv_specs + [full(p) for p in params],
        out_specs=[out_spec] * 8,
        out_shape=[jax.ShapeDtypeStruct((t, w), F32)] * 8,
        compiler_params=_cparams(("parallel",)),
        name="rwkv_prep",
    )(*([proj] * 10), *params)


def _rwkv_scan_kernel(r_ref, lw_ref, k_ref, v_ref, a_ref, b_ref, o_ref, h_scr):
    c = pl.program_id(1)

    @pl.when(c == 0)
    def _():
        h_scr[...] = jnp.zeros(h_scr.shape, F32)

    C = r_ref.shape[0]
    n_pairs = r_ref.shape[1] // LANES
    P = HIGHEST

    row = lax.broadcasted_iota(jnp.int32, (C, C), 0)
    col = lax.broadcasted_iota(jnp.int32, (C, C), 1)
    tri = (row >= col).astype(F32)
    lw = lw_ref[...]
    cw = _dot(tri, lw, P)
    cw_end = cw[C - 1:C, :]
    e_pos = jnp.exp(cw)
    e_prev = jnp.exp(cw - lw)
    e_neg = jnp.exp(-cw)
    e_end = jnp.exp(cw_end - cw)
    w_end = jnp.exp(cw_end)

    a = a_ref[...]
    b = b_ref[...]
    k = k_ref[...]
    r = r_ref[...]
    at = a * e_prev
    bt = b * e_neg
    kt = k * e_neg
    rt = r * e_pos
    bh = b * e_end
    kh = k * e_end
    v = v_ref[...]

    lane = lax.broadcasted_iota(jnp.int32, (C, LANES), 1)
    head0 = lane < RWKV_HEAD_DIM
    r2 = lax.broadcasted_iota(jnp.int32, (2 * C, 2 * C), 0)
    c2 = lax.broadcasted_iota(jnp.int32, (2 * C, 2 * C), 1)
    same = (r2 // C) == (c2 // C)
    strict = same & (r2 > c2)
    incl = same & (r2 >= c2)
    eye = (r2 == c2).astype(F32)

    def two(x, p):
        xp = x[:, p * LANES:(p + 1) * LANES]
        return jnp.concatenate([jnp.where(head0, xp, 0.0), jnp.where(head0, 0.0, xp)], axis=0)

    for p in range(n_pairs):
        at2, bt2, kt2, rt2 = two(at, p), two(bt, p), two(kt, p), two(rt, p)
        bh2, kh2, v2 = two(bh, p), two(kh, p), two(v, p)
        a_ab = jnp.where(strict, _dot_nt(at2, bt2, P), 0.0)
        a_ak = jnp.where(strict, _dot_nt(at2, kt2, P), 0.0)
        m_rb = jnp.where(incl, _dot_nt(rt2, bt2, P), 0.0)
        m_rk = jnp.where(incl, _dot_nt(rt2, kt2, P), 0.0)
        tinv = eye + a_ab
        pw = a_ab
        n_sq = max(1, (C - 1).bit_length() - 1)
        for _ in range(n_sq):
            pw = _dot(pw, pw, P)
            tinv = tinv + _dot(tinv, pw, P)
        akv = _dot(a_ak, v2, P)
        pm = _dot(tinv, at2, P)
        qm = _dot(tinv, akv, P)
        rr = rt2 + _dot(m_rb, pm, P)
        y0 = _dot(m_rb, qm, P) + _dot(m_rk, v2, P)
        wdiag = eye * w_end[:, p * LANES:(p + 1) * LANES]
        gm = wdiag + _dot_tn(bh2, pm, P)
        em = _dot_tn(bh2, qm, P) + _dot_tn(kh2, v2, P)
        h0 = h_scr[p]
        y2 = _dot(rr, h0, P) + y0
        h_scr[p] = _dot(gm, h0, P) + em
        o_ref[:, p * LANES:(p + 1) * LANES] = y2[:C, :] + y2[C:, :]


def _rwkv_scan(r, lw, k, v, a, b, batch, seq):
    w = r.shape[1]
    nc = seq // CHUNK
    spec = pl.BlockSpec((CHUNK, w), lambda bi, ci: (bi * nc + ci, 0))
    return pl.pallas_call(
        _rwkv_scan_kernel,
        grid=(batch, nc),
        in_specs=[spec] * 6,
        out_specs=spec,
        out_shape=jax.ShapeDtypeStruct((batch * seq, w), F32),
        scratch_shapes=[pltpu.VMEM((w // LANES, LANES, LANES), F32)],
        compiler_params=_cparams(("parallel", "arbitrary")),
        name="rwkv_scan",
    )(r, lw, k, v, a, b)


def _mix_out_kernel(o_ref, gate_ref, y_ref, bonus_ref, g_ref, x_ref, g1_ref, lnw_ref, lnb_ref, w_ref, out_ref):
    fox = o_ref[...] * jax.nn.sigmoid(gate_ref[...])
    bd = _head_block_diag()
    y = y_ref[...]
    inv_n = 1.0 / RWKV_HEAD_DIM
    mean = _head_sum(y, bd) * inv_n
    d = y - mean
    var = _head_sum(d * d, bd) * inv_n
    yn = d * lax.rsqrt(var + GN_EPS) * lnw_ref[...] + lnb_ref[...]
    rw = (yn + bonus_ref[...]) * g_ref[...]
    wf = fox.shape[1]
    mix = _dot(fox.astype(BF16), w_ref[:wf, :]) + _dot(rw.astype(BF16), w_ref[wf:, :])
    out_ref[...] = x_ref[...] + g1_ref[0] * mix


def _mix_out(o_fox, proj, gate_col, y, bonus, g, x2, g1, ln_w, ln_b, w_out_bf, tokens_per_batch):
    t, d = x2.shape
    wf = o_fox.shape[1]
    wr = y.shape[1]
    tm = min(256, tokens_per_batch)
    tpb = tokens_per_batch // tm
    return pl.pallas_call(
        _mix_out_kernel,
        grid=(t // tm,),
        in_specs=[pl.BlockSpec((tm, wf), lambda i: (i, 0)),
                  pl.BlockSpec((tm, wf), lambda i: (i, gate_col // wf)),
                  pl.BlockSpec((tm, wr), lambda i: (i, 0)),
                  pl.BlockSpec((tm, wr), lambda i: (i, 0)),
                  pl.BlockSpec((tm, wr), lambda i: (i, 0)),
                  pl.BlockSpec((tm, d), lambda i: (i, 0)),
                  pl.BlockSpec((1, 1, d), lambda i: (i // tpb, 0, 0)),
                  pl.BlockSpec((1, wr), lambda i: (0, 0)),
                  pl.BlockSpec((1, wr), lambda i: (0, 0)),
                  pl.BlockSpec(w_out_bf.shape, lambda i: (0, 0))],
        out_specs=pl.BlockSpec((tm, d), lambda i: (i, 0)),
        out_shape=jax.ShapeDtypeStruct((t, d), F32),
        compiler_params=_cparams(("parallel",)),
        name="mix_out",
    )(o_fox, proj, y, bonus, g, x2, g1, ln_w, ln_b, w_out_bf)


def _top_k_rows(s, ids, k):
    n = s.shape[0]
    row = lax.broadcasted_iota(jnp.int32, s.shape, 0).astype(F32)
    vals, picks = [], []
    for _ in range(k):
        m = jnp.max(s, axis=0, keepdims=True)
        pos = jnp.min(jnp.where(s == m, row, float(n)), axis=0, keepdims=True)
        hit = row == pos
        vals.append(m)
        picks.append(jnp.max(jnp.where(hit, ids, -1.0), axis=0, keepdims=True))
        s = jnp.where(hit, -jnp.inf, s)
    return jnp.concatenate(vals, axis=0), jnp.concatenate(picks, axis=0)


def _peer_route_kernel(x_ref, nw_ref, sc_ref, sh_ref, wq_ref, keys_ref, h_ref, e_ref, g_ref):
    h = _modulated_norm(x_ref[...], nw_ref[...], sc_ref[0], sh_ref[0]).astype(BF16)
    h_ref[...] = h
    q = _dot(h, wq_ref[...])
    tm = q.shape[0]
    half = keys_ref.shape[-1]
    key_row = lax.broadcasted_iota(jnp.int32, (PEER_NKEYS, tm), 0).astype(F32)
    for hd in range(PEER_HEADS):
        tops, topi = [], []
        for p in range(2):
            qp = q[:, (hd * 2 + p) * half:(hd * 2 + p + 1) * half]
            st = _dot_nt(keys_ref[hd, p], qp, HIGHEST)
            ts, ti = _top_k_rows(st, key_row, PEER_TOPK)
            tops.append(ts)
            topi.append(ti)
        cand_s = jnp.concatenate([tops[0][i:i + 1, :] + tops[1] for i in range(PEER_TOPK)], axis=0)
        cand_i = jnp.concatenate([topi[0][i:i + 1, :] * float(PEER_NKEYS) + topi[1]
                                  for i in range(PEER_TOPK)], axis=0)
        best_s, best_e = _top_k_rows(cand_s, cand_i, PEER_TOPK)
        z = jnp.exp(best_s - best_s[0:1, :])
        gates = z / jnp.sum(z, axis=0, keepdims=True)
        e_ref[hd * PEER_TOPK:(hd + 1) * PEER_TOPK, :] = best_e.astype(jnp.int32)
        g_ref[hd * PEER_TOPK:(hd + 1) * PEER_TOPK, :] = gates


def _peer_route(x1, norm_w, sc, sh, wq_bf, sub_keys, tokens_per_batch):
    t, d = x1.shape
    tm = min(256, tokens_per_batch)
    tpb = tokens_per_batch // tm
    slots = PEER_HEADS * PEER_TOPK
    return pl.pallas_call(
        _peer_route_kernel,
        grid=(t // tm,),
        in_specs=[pl.BlockSpec((tm, d), lambda i: (i, 0)),
                  pl.BlockSpec((1, d), lambda i: (0, 0)),
                  pl.BlockSpec((1, 1, d), lambda i: (i // tpb, 0, 0)),
                  pl.BlockSpec((1, 1, d), lambda i: (i // tpb, 0, 0)),
                  pl.BlockSpec(wq_bf.shape, lambda i: (0, 0)),
                  pl.BlockSpec(sub_keys.shape, lambda i: (0, 0, 0, 0))],
        out_specs=[pl.BlockSpec((tm, d), lambda i: (i, 0)),
                   pl.BlockSpec((slots, tm), lambda i: (0, i)),
                   pl.BlockSpec((slots, tm), lambda i: (0, i))],
        out_shape=[jax.ShapeDtypeStruct((t, d), BF16),
                   jax.ShapeDtypeStruct((slots, t), jnp.int32),
                   jax.ShapeDtypeStruct((slots, t), F32)],
        compiler_params=_cparams(("parallel",)),
        name="peer_route",
    )(x1, norm_w, sc, sh, wq_bf, sub_keys)


def _peer_gate_kernel(e_ref, g_ref, o_ref):
    tg, slots = e_ref.shape
    sub = lax.broadcasted_iota(jnp.int32, (PEER_NKEYS, slots), 0)

    def body(t, carry):
        e = e_ref[pl.ds(t, 1), :]
        g = g_ref[pl.ds(t, 1), :]
        hot_i = jnp.where(sub == (e >> 7), 1.0, 0.0).astype(BF16)
        hot_j = jnp.where(sub == (e & 127), g, 0.0).astype(BF16)
        o_ref[t] = _dot_nt(hot_i, hot_j).astype(BF16)
        return carry

    lax.fori_loop(0, tg, body, 0)


def _peer_gate(experts, gates):
    t, slots = experts.shape
    tg = min(128, t)
    return pl.pallas_call(
        _peer_gate_kernel,
        grid=(t // tg,),
        in_specs=[pl.BlockSpec((tg, slots), lambda i: (i, 0)),
                  pl.BlockSpec((tg, slots), lambda i: (i, 0))],
        out_specs=pl.BlockSpec((tg, PEER_NKEYS, PEER_NKEYS), lambda i: (i, 0, 0)),
        out_shape=jax.ShapeDtypeStruct((t, PEER_NKEYS, PEER_NKEYS), BF16),
        compiler_params=_cparams(("parallel",)),
        name="peer_gate",
    )(experts, gates)


def _gelu_exact(x):
    return 0.5 * x * (1.0 + lax.erf(x * 0.7071067811865476))


def _peer_expert_kernel(h_ref, u_ref, v_ref, gm_ref, x_ref, g2_ref, o_ref, acc_scr):
    j = pl.program_id(1)

    @pl.when(j == 0)
    def _():
        acc_scr[...] = jnp.zeros(acc_scr.shape, F32)

    act = _gelu_exact(_dot_nt(h_ref[...], u_ref[...]))
    p = (gm_ref[...].astype(F32) * act).astype(BF16)
    acc_scr[...] += _dot(p, v_ref[...])

    @pl.when(j == pl.num_programs(1) - 1)
    def _():
        o_ref[...] = x_ref[...] + g2_ref[0] * acc_scr[...]


def _peer_expert(h_bf, u_bf, v_bf, gate_mat, x1, g2, tokens_per_batch):
    t, d = x1.shape
    ne = u_bf.shape[0]
    tm = min(512, tokens_per_batch)
    tpb = tokens_per_batch // tm
    be = 512
    return pl.pallas_call(
        _peer_expert_kernel,
        grid=(t // tm, ne // be),
        in_specs=[pl.BlockSpec((tm, d), lambda i, j: (i, 0)),
                  pl.BlockSpec((be, d), lambda i, j: (j, 0)),
                  pl.BlockSpec((be, d), lambda i, j: (j, 0)),
                  pl.BlockSpec((tm, be), lambda i, j: (i, j)),
                  pl.BlockSpec((tm, d), lambda i, j: (i, 0)),
                  pl.BlockSpec((1, 1, d), lambda i, j: (i // tpb, 0, 0))],
        out_specs=pl.BlockSpec((tm, d), lambda i, j: (i, 0)),
        out_shape=jax.ShapeDtypeStruct((t, d), F32),
        scratch_shapes=[pltpu.VMEM((tm, d), F32)],
        compiler_params=_cparams(("parallel", "arbitrary")),
        name="peer_expert",
    )(h_bf, u_bf, v_bf, gate_mat, x1, g2)


def _pad_cols(a, n):
    return jnp.pad(a, ((0, 0), (0, n - a.shape[1])))


def _pad_rows(a, n):
    return jnp.pad(a, ((0, n - a.shape[0]), (0, 0)))


def _layer(x, c, w_ada, b_ada, norm_mix_w, w_in, fox_q_norm_w, fox_k_norm_w, fox_f_bias,
           rwkv_mu, rwkv_w0, rwkv_w_up, rwkv_a0, rwkv_a_up, rwkv_g_up, rwkv_k_k, rwkv_k_a,
           rwkv_r_k, rwkv_ln_w, rwkv_ln_b, w_out, norm_ffn_w, peer_w_query, peer_sub_keys,
           peer_u, peer_v):
    B, S, D = x.shape
    T = B * S
    fw = FOX_HEADS * FOX_HEAD_DIM
    rw = RWKV_HEADS * RWKV_HEAD_DIM
    w_lora = rwkv_w_up.shape[0]
    a_lora = rwkv_a_up.shape[0]
    g_lora = rwkv_g_up.shape[0]
    assert w_lora <= LANES and a_lora + FOX_HEADS <= LANES and g_lora == 256

    c_pad = _pad_rows(c, 8)
    mod = _ada_mod(c_pad, w_ada, b_ada)[:B]
    sh1, sc1, g1, sh2, sc2, g2 = [m.reshape(B, 1, D) for m in jnp.split(mod, 6, axis=-1)]

    fox_cols = 4 * fw + FOX_HEADS
    wi_fox, wi_rw = w_in[:, :fox_cols], w_in[:, fox_cols:]
    mu = rwkv_mu.reshape(1, -1)
    seg = lambda a, lo, n: a[:, lo:lo + n]
    w_perm = jnp.concatenate([
        seg(wi_fox, 0, 4 * fw),
        seg(wi_rw, 0, 3 * rw),
        seg(wi_rw, 3 * rw + w_lora + a_lora, g_lora),
        _pad_cols(seg(wi_rw, 3 * rw, w_lora), LANES),
        _pad_cols(jnp.concatenate([seg(wi_rw, 3 * rw + w_lora, a_lora), seg(wi_fox, 4 * fw, FOX_HEADS)], 1), LANES),
    ], axis=1).astype(BF16)
    cols = {"gate": 3 * fw, "rr": 4 * fw, "rk": 4 * fw + rw, "rv": 4 * fw + 2 * rw,
            "gd": 4 * fw + 3 * rw, "lo": 4 * fw + 3 * rw + g_lora}
    f_lane = a_lora
    f_block = (cols["lo"] + LANES) // LANES
    mus = [seg(mu, 0, rw), seg(mu, rw, rw), seg(mu, 2 * rw, rw),
           seg(mu, 3 * rw + w_lora + a_lora, g_lora),
           jnp.concatenate([_pad_cols(seg(mu, 3 * rw, w_lora), LANES),
                            _pad_cols(seg(mu, 3 * rw + w_lora, a_lora), LANES)], 1)]
    scale = FOX_HEAD_DIM ** -0.5
    head_w = _pad_cols(jnp.concatenate([jnp.tile(fox_q_norm_w * scale, FOX_HEADS),
                                        jnp.tile(fox_k_norm_w, FOX_HEADS)]).reshape(1, -1), w_perm.shape[1])

    x2 = x.reshape(T, D)
    proj = _in_proj(x2, norm_mix_w.reshape(1, D), sc1, sh1, w_perm, head_w, S, 2 * fw)

    f_bias_row = jnp.zeros((1, LANES), F32).at[0, f_lane:f_lane + FOX_HEADS].set(fox_f_bias)
    cum = _fox_cum(proj, f_bias_row, B, S, f_block)
    cum_rows = cum.reshape(B, S, LANES)[:, :, f_lane:f_lane + FOX_HEADS].transpose(0, 2, 1).reshape(B, FOX_HEADS, 1, S)
    o_fox = _fox_attn(proj, cum_rows, B, S)

    row = lambda a: a.reshape(1, -1)
    r, lw, k, v, a_vec, b_vec, g, bonus = _rwkv_prep(
        proj, cols, mus, row(rwkv_w0), _pad_rows(rwkv_w_up, LANES), row(rwkv_a0), _pad_rows(rwkv_a_up, LANES),
        rwkv_g_up, row(rwkv_k_k), row(rwkv_k_a), row(rwkv_r_k), S)
    y = _rwkv_scan(r, lw, k, v, a_vec, b_vec, B, S)

    x1 = _mix_out(o_fox, proj, cols["gate"], y, bonus, g, x2, g1, row(rwkv_ln_w), row(rwkv_ln_b),
                  w_out.astype(BF16), S)

    h2, experts_t, gates_t = _peer_route(x1, norm_ffn_w.reshape(1, D), sc2, sh2,
                                         peer_w_query.astype(BF16), peer_sub_keys, S)
    gate_mat = _peer_gate(experts_t.T, gates_t.T).reshape(T, PEER_NKEYS * PEER_NKEYS)
    out = _peer_expert(h2, peer_u.astype(BF16), peer_v.astype(BF16), gate_mat, x1, g2, S)
    return out.reshape(B, S, D)


def kernel(x, c, w_ada, b_ada, norm_mix_w, w_in, fox_q_norm_w, fox_k_norm_w, fox_f_bias, rwkv_mu, rwkv_w0,
           rwkv_w_up, rwkv_a0, rwkv_a_up, rwkv_g_up, rwkv_k_k, rwkv_k_a, rwkv_r_k, rwkv_ln_w, rwkv_ln_b,
           w_out, norm_ffn_w, peer_w_query, peer_sub_keys, peer_u, peer_v):
    params = (w_ada, b_ada, norm_mix_w, w_in, fox_q_norm_w, fox_k_norm_w, fox_f_bias, rwkv_mu, rwkv_w0,
              rwkv_w_up, rwkv_a0, rwkv_a_up, rwkv_g_up, rwkv_k_k, rwkv_k_a, rwkv_r_k, rwkv_ln_w, rwkv_ln_b,
              w_out, norm_ffn_w, peer_w_query, peer_sub_keys, peer_u, peer_v)
    for l in range(w_ada.shape[0]):
        x = _layer(x, c, *[p[l] for p in params])
    return x
```

```python
import functools

import jax
import jax.numpy as jnp
from jax import lax
from jax.experimental import pallas as pl
from jax.experimental.pallas import tpu as pltpu

F32 = jnp.float32
BF16 = jnp.bfloat16
HIGHEST = lax.Precision.HIGHEST

LANES = 128
NORM_EPS = 1e-6
GN_EPS = 64e-5
CHUNK = 64
FOX_HEADS = 8
FOX_HEAD_DIM = 128
RWKV_HEADS = 16
RWKV_HEAD_DIM = 64
PEER_HEADS = 8
PEER_NKEYS = 128
PEER_TOPK = 16
NEG_BIG = -1e30
LOG2_E = 1.4426950408889634
VMEM_LIMIT = 56 * 1024 * 1024


def _cparams(sem):
    return pltpu.CompilerParams(dimension_semantics=sem, vmem_limit_bytes=VMEM_LIMIT)


def _dot(a, b, precision=None):
    return jnp.dot(a, b, preferred_element_type=F32, precision=precision)


def _dot_nt(a, b, precision=None):
    return lax.dot_general(a, b, (((1,), (1,)), ((), ())), preferred_element_type=F32,
                           precision=precision)


def _dot_tn(a, b, precision=None):
    return lax.dot_general(a, b, (((0,), (0,)), ((), ())), preferred_element_type=F32,
                           precision=precision)


def _ada_kernel(c_ref, w_ref, b_ref, o_ref):
    c = c_ref[...]
    s = c * jax.nn.sigmoid(c)
    o_ref[...] = _dot(s, w_ref[...], HIGHEST) + b_ref[...]


def _ada_mod(c_pad, w_ada, b_ada):
    rows, d = c_pad.shape
    n = w_ada.shape[1]
    bn = 1024
    return pl.pallas_call(
        _ada_kernel,
        grid=(n // bn,),
        in_specs=[pl.BlockSpec((rows, d), lambda j: (0, 0)),
                  pl.BlockSpec((d, bn), lambda j: (0, j)),
                  pl.BlockSpec((1, bn), lambda j: (0, j))],
        out_specs=pl.BlockSpec((rows, bn), lambda j: (0, j)),
        out_shape=jax.ShapeDtypeStruct((rows, n), F32),
        compiler_params=_cparams(("arbitrary",)),
        name="ada_mod",
    )(c_pad, w_ada, b_ada.reshape(1, n))


def _modulated_norm(x, nw, sc, sh):
    y = x * lax.rsqrt(jnp.mean(x * x, axis=-1, keepdims=True) + NORM_EPS)
    return y * nw * (1.0 + sc) + sh


def _in_proj_kernel(x_ref, nw_ref, sc_ref, sh_ref, w_ref, hw_ref, o_ref, h_scr, *, n_qk_blocks):
    j = pl.program_id(1)

    @pl.when(j == 0)
    def _():
        h = _modulated_norm(x_ref[...], nw_ref[...], sc_ref[0], sh_ref[0])
        h_scr[...] = h.astype(BF16)

    acc = _dot(h_scr[...], w_ref[...])

    @pl.when(j < n_qk_blocks)
    def _():
        for hh in range(acc.shape[1] // FOX_HEAD_DIM):
            sl = slice(hh * FOX_HEAD_DIM, (hh + 1) * FOX_HEAD_DIM)
            a = acc[:, sl]
            rs = lax.rsqrt(jnp.mean(a * a, axis=-1, keepdims=True) + NORM_EPS)
            o_ref[:, sl] = a * rs * hw_ref[:, sl]

    @pl.when(j >= n_qk_blocks)
    def _():
        o_ref[...] = acc


def _in_proj(x2, norm_w, sc, sh, w_bf, head_w, tokens_per_batch, n_qk_cols):
    t, d = x2.shape
    n = w_bf.shape[1]
    tm = min(1024, tokens_per_batch)
    bn = 512
    tpb = tokens_per_batch // tm
    kern = functools.partial(_in_proj_kernel, n_qk_blocks=n_qk_cols // bn)
    return pl.pallas_call(
        kern,
        grid=(t // tm, n // bn),
        in_specs=[pl.BlockSpec((tm, d), lambda i, j: (i, 0)),
                  pl.BlockSpec((1, d), lambda i, j: (0, 0)),
                  pl.BlockSpec((1, 1, d), lambda i, j: (i // tpb, 0, 0)),
                  pl.BlockSpec((1, 1, d), lambda i, j: (i // tpb, 0, 0)),
                  pl.BlockSpec((d, bn), lambda i, j: (0, j)),
                  pl.BlockSpec((1, bn), lambda i, j: (0, j))],
        out_specs=pl.BlockSpec((tm, bn), lambda i, j: (i, j)),
        out_shape=jax.ShapeDtypeStruct((t, n), F32),
        scratch_shapes=[pltpu.VMEM((tm, d), BF16)],
        compiler_params=_cparams(("parallel", "arbitrary")),
        name="in_proj",
    )(x2, norm_w, sc, sh, w_bf, head_w)


def _log_sigmoid(x):
    return jnp.minimum(x, 0.0) - jnp.log(1.0 + jnp.exp(-jnp.abs(x)))


def _fox_cum_kernel(f_ref, b_ref, o_ref, *, blk):
    s = f_ref.shape[0]
    row = lax.broadcasted_iota(jnp.int32, (blk, blk), 0)
    col = lax.broadcasted_iota(jnp.int32, (blk, blk), 1)
    tri = (row >= col).astype(F32)
    carry = jnp.zeros((1, f_ref.shape[1]), F32)
    for i in range(s // blk):
        lf = _log_sigmoid(f_ref[i * blk:(i + 1) * blk, :] + b_ref[...])
        cs = _dot(tri, lf, HIGHEST) + carry
        o_ref[i * blk:(i + 1) * blk, :] = cs * LOG2_E
        carry = cs[blk - 1:blk, :]


def _fox_cum(proj, f_bias_row, batch, seq, col_block):
    blk = min(256, seq)
    return pl.pallas_call(
        functools.partial(_fox_cum_kernel, blk=blk),
        grid=(batch,),
        in_specs=[pl.BlockSpec((seq, LANES), lambda b: (b, col_block)),
                  pl.BlockSpec((1, LANES), lambda b: (0, 0))],
        out_specs=pl.BlockSpec((seq, LANES), lambda b: (b, 0)),
        out_shape=jax.ShapeDtypeStruct((batch * seq, LANES), F32),
        compiler_params=_cparams(("parallel",)),
        name="fox_cum",
    )(proj, f_bias_row)


def _fox_attn_kernel(qt_ref, kt_ref, q_ref, k_ref, v_ref, c_ref, o_ref, m_scr, l_scr, acc_scr):
    t = pl.program_id(2)
    qi = qt_ref[t]
    ki = kt_ref[t]
    hd = FOX_HEAD_DIM
    n_heads = q_ref.shape[1] // hd

    @pl.when(ki == 0)
    def _():
        m_scr[...] = jnp.full(m_scr.shape, NEG_BIG, F32)
        l_scr[...] = jnp.zeros(l_scr.shape, F32)
        acc_scr[...] = jnp.zeros(acc_scr.shape, F32)

    def step(masked):
        for h in range(n_heads):
            sl = slice(h * hd, (h + 1) * hd)
            q = q_ref[:, sl].astype(BF16)
            k = k_ref[:, sl].astype(BF16)
            s = _dot_nt(q, k) - c_ref[0, h]
            if masked:
                row = lax.broadcasted_iota(jnp.int32, s.shape, 0)
                col = lax.broadcasted_iota(jnp.int32, s.shape, 1)
                s = jnp.where(row >= col, s, NEG_BIG)
            m_prev = m_scr[h]
            m_new = jnp.maximum(m_prev, jnp.max(s, axis=-1, keepdims=True))
            alpha = jnp.exp2(m_prev - m_new)
            p = jnp.exp2(s - m_new)
            l_scr[h] = alpha * l_scr[h] + jnp.sum(p, axis=-1, keepdims=True)
            acc_scr[:, sl] = alpha * acc_scr[:, sl] + _dot(p.astype(BF16), v_ref[:, sl].astype(BF16))
            m_scr[h] = m_new

    @pl.when(ki < qi)
    def _():
        step(False)

    @pl.when(ki == qi)
    def _():
        step(True)
        for h in range(n_heads):
            sl = slice(h * hd, (h + 1) * hd)
            o_ref[:, sl] = acc_scr[:, sl] / l_scr[h]


def _fox_attn(proj, cum_rows, batch, seq):
    tq = min(512, seq)
    nq = seq // tq
    hps = 2
    w = hps * FOX_HEAD_DIM
    kcol = FOX_HEADS // hps
    tri = [(q, k) for q in range(nq) for k in range(q + 1)]
    qt = jnp.asarray([q for q, _ in tri], jnp.int32)
    kt = jnp.asarray([k for _, k in tri], jnp.int32)
    grid_spec = pltpu.PrefetchScalarGridSpec(
        num_scalar_prefetch=2,
        grid=(batch, kcol, len(tri)),
        in_specs=[
            pl.BlockSpec((tq, w), lambda b, h, t, qt, kt: (b * nq + qt[t], h)),
            pl.BlockSpec((tq, w), lambda b, h, t, qt, kt: (b * nq + kt[t], kcol + h)),
            pl.BlockSpec((tq, w), lambda b, h, t, qt, kt: (b * nq + kt[t], 2 * kcol + h)),
            pl.BlockSpec((1, hps, 1, tq), lambda b, h, t, qt, kt: (b, h, 0, kt[t])),
        ],
        out_specs=pl.BlockSpec((tq, w), lambda b, h, t, qt, kt: (b * nq + qt[t], h)),
        scratch_shapes=[pltpu.VMEM((hps, tq, 1), F32), pltpu.VMEM((hps, tq, 1), F32),
                        pltpu.VMEM((tq, w), F32)],
    )
    return pl.pallas_call(
        _fox_attn_kernel,
        grid_spec=grid_spec,
        out_shape=jax.ShapeDtypeStruct((batch * seq, FOX_HEADS * FOX_HEAD_DIM), F32),
        compiler_params=_cparams(("parallel", "parallel", "arbitrary")),
        name="fox_attn",
    )(qt, kt, proj, proj, proj, cum_rows)


def _head_sum(x, bd):
    parts = [_dot(x[:, j * LANES:(j + 1) * LANES], bd, HIGHEST) for j in range(x.shape[1] // LANES)]
    return jnp.concatenate(parts, axis=1)


def _head_block_diag():
    r = lax.broadcasted_iota(jnp.int32, (LANES, LANES), 0) // RWKV_HEAD_DIM
    c = lax.broadcasted_iota(jnp.int32, (LANES, LANES), 1) // RWKV_HEAD_DIM
    return (r == c).astype(F32)


def _rwkv_prep_kernel(r_ref, k_ref, v_ref, gd_ref, lo_ref, pr_ref, pk_ref, pv_ref, pgd_ref, plo_ref,
                      mu_r, mu_k, mu_v, mu_gd, mu_lo, w0_ref, wup_ref, a0_ref, aup_ref, gup_ref,
                      kk_ref, ka_ref, rk_ref,
                      or_ref, olw_ref, ok_ref, ov_ref, oa_ref, ob_ref, og_ref, obonus_ref, *, tiles_per_batch):
    i = pl.program_id(0)
    first = (i % tiles_per_batch) == 0

    def shifted(cur_ref, prv_ref, mu_ref):
        cur = cur_ref[...]
        last = jnp.where(first, 0.0, prv_ref[7:8, :])
        row = lax.broadcasted_iota(jnp.int32, cur.shape, 0)
        prev = jnp.where(row == 0, last, pltpu.roll(cur, 1, 0))
        return cur + (prev - cur) * mu_ref[...]

    r = shifted(r_ref, pr_ref, mu_r)
    k = shifted(k_ref, pk_ref, mu_k)
    v = shifted(v_ref, pv_ref, mu_v)
    gd = shifted(gd_ref, pgd_ref, mu_gd)
    lo = shifted(lo_ref, plo_ref, mu_lo)
    wd = lo[:, :LANES]
    ad = lo[:, LANES:]

    w_pre = w0_ref[...] + _dot(jnp.tanh(wd), wup_ref[...], HIGHEST)
    w_raw = _log_sigmoid(w_pre) - 0.5
    log_decay = -jnp.exp(w_raw)
    a = jax.nn.sigmoid(a0_ref[...] + _dot(ad, aup_ref[...], HIGHEST))
    g = _dot(jax.nn.sigmoid(gd), gup_ref[...], HIGHEST)

    bd = _head_block_diag()
    kk = k * kk_ref[...]
    nrm = jnp.maximum(jnp.sqrt(_head_sum(kk * kk, bd)), 1e-12)
    kk = kk / nrm
    k_mod = k * (1.0 + (a - 1.0) * ka_ref[...])
    bonus = _head_sum(r * k_mod * rk_ref[...], bd) * v

    or_ref[...] = r
    olw_ref[...] = log_decay
    ok_ref[...] = k_mod
    ov_ref[...] = v
    oa_ref[...] = -kk
    ob_ref[...] = kk * a
    og_ref[...] = g
    obonus_ref[...] = bonus


def _rwkv_prep(proj, cols, mus, w0, w_up, a0, a_up, g_up, k_k, k_a, r_k, tokens_per_batch):
    t = proj.shape[0]
    w = RWKV_HEADS * RWKV_HEAD_DIM
    tm = min(256, tokens_per_batch)
    tpb = tokens_per_batch // tm
    widths = [w, w, w, 256, 256]
    offs = [cols["rr"], cols["rk"], cols["rv"], cols["gd"], cols["lo"]]
    cur_specs = [pl.BlockSpec((tm, wd), functools.partial(lambda i, cb: (i, cb), cb=o // wd))
                 for wd, o in zip(widths, offs)]
    prv_specs = [pl.BlockSpec((8, wd), functools.partial(
        lambda i, cb: (jnp.maximum(i * (tm // 8) - 1, 0), cb), cb=o // wd))
        for wd, o in zip(widths, offs)]
    full = lambda a: pl.BlockSpec(a.shape, lambda i: (0,) * a.ndim)
    params = list(mus) + [w0, w_up, a0, a_up, g_up, k_k, k_a, r_k]
    out_spec = pl.BlockSpec((tm, w), lambda i: (i, 0))
    return pl.pallas_call(
        functools.partial(_rwkv_prep_kernel, tiles_per_batch=tpb),
        grid=(t // tm,),
        in_specs=cur_specs + prv_specs + [full(p) for p in params],
        out_specs=[out_spec] * 8,
        out_shape=[jax.ShapeDtypeStruct((t, w), F32)] * 8,
        compiler_params=_cparams(("parallel",)),
        name="rwkv_prep",
    )(*([proj] * 10), *params)


def _rwkv_scan_kernel(r_ref, lw_ref, k_ref, v_ref, a_ref, b_ref, o_ref, h_scr):
    c = pl.program_id(1)

    @pl.when(c == 0)
    def _():
        h_scr[...] = jnp.zeros(h_scr.shape, F32)

    C = r_ref.shape[0]
    n_pairs = r_ref.shape[1] // LANES
    P = HIGHEST

    row = lax.broadcasted_iota(jnp.int32, (C, C), 0)
    col = lax.broadcasted_iota(jnp.int32, (C, C), 1)
    tri = (row >= col).astype(F32)
    lw = lw_ref[...]
    cw = _dot(tri, lw, P)
    cw_end = cw[C - 1:C, :]
    e_pos = jnp.exp(cw)
    e_prev = jnp.exp(cw - lw)
    e_neg = jnp.exp(-cw)
    e_end = jnp.exp(cw_end - cw)
    w_end = jnp.exp(cw_end)

    a = a_ref[...]
    b = b_ref[...]
    k = k_ref[...]
    r = r_ref[...]
    at = a * e_prev
    bt = b * e_neg
    kt = k * e_neg
    rt = r * e_pos
    bh = b * e_end
    kh = k * e_end
    v = v_ref[...]

    lane = lax.broadcasted_iota(jnp.int32, (C, LANES), 1)
    head0 = lane < RWKV_HEAD_DIM
    r2 = lax.broadcasted_iota(jnp.int32, (2 * C, 2 * C), 0)
    c2 = lax.broadcasted_iota(jnp.int32, (2 * C, 2 * C), 1)
    same = (r2 // C) == (c2 // C)
    strict = same & (r2 > c2)
    incl = same & (r2 >= c2)
    eye = (r2 == c2).astype(F32)

    def two(x, p):
        xp = x[:, p * LANES:(p + 1) * LANES]
        return jnp.concatenate([jnp.where(head0, xp, 0.0), jnp.where(head0, 0.0, xp)], axis=0)

    def mm(x, y):
        return _dot(x.astype(BF16), y.astype(BF16))

    def split(x):
        hi = x.astype(BF16)
        return hi, (x - hi.astype(F32)).astype(BF16)

    def mm3(x, y):
        xh, xl = split(x)
        yh, yl = split(y)
        return _dot(xh, yh) + (_dot(xh, yl) + _dot(xl, yh))

    G = 2 * C
    pairs = range(n_pairs)
    at2 = [two(at, p) for p in pairs]
    rt2 = [two(rt, p) for p in pairs]
    v2 = [two(v, p).astype(BF16) for p in pairs]
    gram = [_dot_nt(jnp.concatenate([at2[p], rt2[p]], axis=0).astype(BF16),
                    jnp.concatenate([two(bt, p), two(kt, p)], axis=0).astype(BF16)) for p in pairs]
    a_ab = [jnp.where(strict, gram[p][:G, :G], 0.0) for p in pairs]
    nmat = list(a_ab)
    pw = list(a_ab)
    for _ in range(max(1, (C - 1).bit_length() - 1)):
        pw = [mm3(pw[p], pw[p]) for p in pairs]
        nmat = [nmat[p] + pw[p] + mm3(nmat[p], pw[p]) for p in pairs]
    akv = [mm(jnp.where(strict, gram[p][:G, G:], 0.0), v2[p]) for p in pairs]
    rhs = [jnp.concatenate([at2[p], akv[p]], axis=1) for p in pairs]
    pq = [(rhs[p] + mm(nmat[p], rhs[p])).astype(BF16) for p in pairs]
    ry = [mm(jnp.where(incl, gram[p][G:, :G], 0.0), pq[p]) for p in pairs]
    mv = [mm(jnp.where(incl, gram[p][G:, G:], 0.0), v2[p]) for p in pairs]
    ge = [_dot_tn(two(bh, p).astype(BF16), pq[p]) for p in pairs]
    kv = [_dot_tn(two(kh, p).astype(BF16), v2[p]) for p in pairs]
    for p in pairs:
        rr = rt2[p] + ry[p][:, :LANES]
        gm = eye * w_end[:, p * LANES:(p + 1) * LANES] + ge[p][:, :LANES]
        yh = mm(jnp.concatenate([rr, gm], axis=0), h_scr[p])
        y2 = yh[:G] + ry[p][:, LANES:] + mv[p]
        h_scr[p] = yh[G:] + ge[p][:, LANES:] + kv[p]
        o_ref[:, p * LANES:(p + 1) * LANES] = y2[:C, :] + y2[C:, :]


def _rwkv_scan(r, lw, k, v, a, b, batch, seq):
    w = r.shape[1]
    nc = seq // CHUNK
    spec = pl.BlockSpec((CHUNK, w), lambda bi, ci: (bi * nc + ci, 0))
    return pl.pallas_call(
        _rwkv_scan_kernel,
        grid=(batch, nc),
        in_specs=[spec] * 6,
        out_specs=spec,
        out_shape=jax.ShapeDtypeStruct((batch * seq, w), F32),
        scratch_shapes=[pltpu.VMEM((w // LANES, LANES, LANES), F32)],
        compiler_params=_cparams(("parallel", "arbitrary")),
        name="rwkv_scan",
    )(r, lw, k, v, a, b)


def _mix_out_kernel(o_ref, gate_ref, y_ref, bonus_ref, g_ref, x_ref, g1_ref, lnw_ref, lnb_ref, w_ref, out_ref):
    fox = o_ref[...] * jax.nn.sigmoid(gate_ref[...])
    bd = _head_block_diag()
    y = y_ref[...]
    inv_n = 1.0 / RWKV_HEAD_DIM
    mean = _head_sum(y, bd) * inv_n
    d = y - mean
    var = _head_sum(d * d, bd) * inv_n
    yn = d * lax.rsqrt(var + GN_EPS) * lnw_ref[...] + lnb_ref[...]
    rw = (yn + bonus_ref[...]) * g_ref[...]
    wf = fox.shape[1]
    mix = _dot(fox.astype(BF16), w_ref[:wf, :]) + _dot(rw.astype(BF16), w_ref[wf:, :])
    out_ref[...] = x_ref[...] + g1_ref[0] * mix


def _mix_out(o_fox, proj, gate_col, y, bonus, g, x2, g1, ln_w, ln_b, w_out_bf, tokens_per_batch):
    t, d = x2.shape
    wf = o_fox.shape[1]
    wr = y.shape[1]
    tm = min(256, tokens_per_batch)
    tpb = tokens_per_batch // tm
    return pl.pallas_call(
        _mix_out_kernel,
        grid=(t // tm,),
        in_specs=[pl.BlockSpec((tm, wf), lambda i: (i, 0)),
                  pl.BlockSpec((tm, wf), lambda i: (i, gate_col // wf)),
                  pl.BlockSpec((tm, wr), lambda i: (i, 0)),
                  pl.BlockSpec((tm, wr), lambda i: (i, 0)),
                  pl.BlockSpec((tm, wr), lambda i: (i, 0)),
                  pl.BlockSpec((tm, d), lambda i: (i, 0)),
                  pl.BlockSpec((1, 1, d), lambda i: (i // tpb, 0, 0)),
                  pl.BlockSpec((1, wr), lambda i: (0, 0)),
                  pl.BlockSpec((1, wr), lambda i: (0, 0)),
                  pl.BlockSpec(w_out_bf.shape, lambda i: (0, 0))],
        out_specs=pl.BlockSpec((tm, d), lambda i: (i, 0)),
        out_shape=jax.ShapeDtypeStruct((t, d), F32),
        compiler_params=_cparams(("parallel",)),
        name="mix_out",
    )(o_fox, proj, y, bonus, g, x2, g1, ln_w, ln_b, w_out_bf)


def _top_k_rows(s, ids, k):
    n = s.shape[0]
    row = lax.broadcasted_iota(jnp.int32, s.shape, 0).astype(F32)
    vals, picks = [], []
    for _ in range(k):
        m = jnp.max(s, axis=0, keepdims=True)
        pos = jnp.min(jnp.where(s == m, row, float(n)), axis=0, keepdims=True)
        hit = row == pos
        vals.append(m)
        picks.append(jnp.max(jnp.where(hit, ids, -1.0), axis=0, keepdims=True))
        s = jnp.where(hit, -jnp.inf, s)
    return jnp.concatenate(vals, axis=0), jnp.concatenate(picks, axis=0)


def _peer_route_kernel(x_ref, nw_ref, sc_ref, sh_ref, wq_ref, keys_ref, h_ref, e_ref, g_ref):
    h = _modulated_norm(x_ref[...], nw_ref[...], sc_ref[0], sh_ref[0]).astype(BF16)
    h_ref[...] = h
    q = _dot(h, wq_ref[...])
    tm = q.shape[0]
    half = keys_ref.shape[-1]
    key_row = lax.broadcasted_iota(jnp.int32, (PEER_NKEYS, tm), 0).astype(F32)
    for hd in range(PEER_HEADS):
        tops, topi = [], []
        for p in range(2):
            qp = q[:, (hd * 2 + p) * half:(hd * 2 + p + 1) * half]
            st = _dot_nt(keys_ref[hd, p], qp, HIGHEST)
            ts, ti = _top_k_rows(st, key_row, PEER_TOPK)
            tops.append(ts)
            topi.append(ti)
        cand_s = jnp.concatenate([tops[0][i:i + 1, :] + tops[1] for i in range(PEER_TOPK)], axis=0)
        cand_i = jnp.concatenate([topi[0][i:i + 1, :] * float(PEER_NKEYS) + topi[1]
                                  for i in range(PEER_TOPK)], axis=0)
        best_s, best_e = _top_k_rows(cand_s, cand_i, PEER_TOPK)
        z = jnp.exp(best_s - best_s[0:1, :])
        gates = z / jnp.sum(z, axis=0, keepdims=True)
        e_ref[hd * PEER_TOPK:(hd + 1) * PEER_TOPK, :] = best_e.astype(jnp.int32)
        g_ref[hd * PEER_TOPK:(hd + 1) * PEER_TOPK, :] = gates


def _peer_route(x1, norm_w, sc, sh, wq_bf, sub_keys, tokens_per_batch):
    t, d = x1.shape
    tm = min(256, tokens_per_batch)
    tpb = tokens_per_batch // tm
    slots = PEER_HEADS * PEER_TOPK
    return pl.pallas_call(
        _peer_route_kernel,
        grid=(t // tm,),
        in_specs=[pl.BlockSpec((tm, d), lambda i: (i, 0)),
                  pl.BlockSpec((1, d), lambda i: (0, 0)),
                  pl.BlockSpec((1, 1, d), lambda i: (i // tpb, 0, 0)),
                  pl.BlockSpec((1, 1, d), lambda i: (i // tpb, 0, 0)),
                  pl.BlockSpec(wq_bf.shape, lambda i: (0, 0)),
                  pl.BlockSpec(sub_keys.shape, lambda i: (0, 0, 0, 0))],
        out_specs=[pl.BlockSpec((tm, d), lambda i: (i, 0)),
                   pl.BlockSpec((slots, tm), lambda i: (0, i)),
                   pl.BlockSpec((slots, tm), lambda i: (0, i))],
        out_shape=[jax.ShapeDtypeStruct((t, d), BF16),
                   jax.ShapeDtypeStruct((slots, t), jnp.int32),
                   jax.ShapeDtypeStruct((slots, t), F32)],
        compiler_params=_cparams(("parallel",)),
        name="peer_route",
    )(x1, norm_w, sc, sh, wq_bf, sub_keys)


def _peer_gate_kernel(e_ref, g_ref, o_ref):
    tg, slots = e_ref.shape
    sub = lax.broadcasted_iota(jnp.int32, (PEER_NKEYS, slots), 0)

    def body(t, carry):
        e = e_ref[pl.ds(t, 1), :]
        g = g_ref[pl.ds(t, 1), :]
        hot_i = jnp.where(sub == (e >> 7), 1.0, 0.0).astype(BF16)
        hot_j = jnp.where(sub == (e & 127), g, 0.0).astype(BF16)
        o_ref[t] = _dot_nt(hot_i, hot_j).astype(BF16)
        return carry

    lax.fori_loop(0, tg, body, 0, unroll=8)


def _peer_gate(experts, gates):
    t, slots = experts.shape
    tg = min(128, t)
    return pl.pallas_call(
        _peer_gate_kernel,
        grid=(t // tg,),
        in_specs=[pl.BlockSpec((tg, slots), lambda i: (i, 0)),
                  pl.BlockSpec((tg, slots), lambda i: (i, 0))],
        out_specs=pl.BlockSpec((tg, PEER_NKEYS, PEER_NKEYS), lambda i: (i, 0, 0)),
        out_shape=jax.ShapeDtypeStruct((t, PEER_NKEYS, PEER_NKEYS), BF16),
        compiler_params=_cparams(("parallel",)),
        name="peer_gate",
    )(experts, gates)


def _gelu_exact(x):
    return 0.5 * x * (1.0 + lax.erf(x * 0.7071067811865476))


def _peer_expert_kernel(h_ref, u_ref, v_ref, gm_ref, x_ref, g2_ref, o_ref, acc_scr):
    j = pl.program_id(1)

    @pl.when(j == 0)
    def _():
        acc_scr[...] = jnp.zeros(acc_scr.shape, F32)

    act = _gelu_exact(_dot_nt(h_ref[...], u_ref[...]))
    p = (gm_ref[...].astype(F32) * act).astype(BF16)
    acc_scr[...] += _dot(p, v_ref[...])

    @pl.when(j == pl.num_programs(1) - 1)
    def _():
        o_ref[...] = x_ref[...] + g2_ref[0] * acc_scr[...]


def _peer_expert(h_bf, u_bf, v_bf, gate_mat, x1, g2, tokens_per_batch):
    t, d = x1.shape
    ne = u_bf.shape[0]
    tm = min(512, tokens_per_batch)
    tpb = tokens_per_batch // tm
    be = 512
    return pl.pallas_call(
        _peer_expert_kernel,
        grid=(t // tm, ne // be),
        in_specs=[pl.BlockSpec((tm, d), lambda i, j: (i, 0)),
                  pl.BlockSpec((be, d), lambda i, j: (j, 0)),
                  pl.BlockSpec((be, d), lambda i, j: (j, 0)),
                  pl.BlockSpec((tm, be), lambda i, j: (i, j)),
                  pl.BlockSpec((tm, d), lambda i, j: (i, 0)),
                  pl.BlockSpec((1, 1, d), lambda i, j: (i // tpb, 0, 0))],
        out_specs=pl.BlockSpec((tm, d), lambda i, j: (i, 0)),
        out_shape=jax.ShapeDtypeStruct((t, d), F32),
        scratch_shapes=[pltpu.VMEM((tm, d), F32)],
        compiler_params=_cparams(("parallel", "arbitrary")),
        name="peer_expert",
    )(h_bf, u_bf, v_bf, gate_mat, x1, g2)


def _pad_cols(a, n):
    return jnp.pad(a, ((0, 0), (0, n - a.shape[1])))


def _pad_rows(a, n):
    return jnp.pad(a, ((0, n - a.shape[0]), (0, 0)))


def _layer(x, c, w_ada, b_ada, norm_mix_w, w_in, fox_q_norm_w, fox_k_norm_w, fox_f_bias,
           rwkv_mu, rwkv_w0, rwkv_w_up, rwkv_a0, rwkv_a_up, rwkv_g_up, rwkv_k_k, rwkv_k_a,
           rwkv_r_k, rwkv_ln_w, rwkv_ln_b, w_out, norm_ffn_w, peer_w_query, peer_sub_keys,
           peer_u, peer_v):
    B, S, D = x.shape
    T = B * S
    fw = FOX_HEADS * FOX_HEAD_DIM
    rw = RWKV_HEADS * RWKV_HEAD_DIM
    w_lora = rwkv_w_up.shape[0]
    a_lora = rwkv_a_up.shape[0]
    g_lora = rwkv_g_up.shape[0]
    assert w_lora <= LANES and a_lora + FOX_HEADS <= LANES and g_lora == 256

    c_pad = _pad_rows(c, 8)
    mod = _ada_mod(c_pad, w_ada, b_ada)[:B]
    sh1, sc1, g1, sh2, sc2, g2 = [m.reshape(B, 1, D) for m in jnp.split(mod, 6, axis=-1)]

    fox_cols = 4 * fw + FOX_HEADS
    wi_fox, wi_rw = w_in[:, :fox_cols], w_in[:, fox_cols:]
    mu = rwkv_mu.reshape(1, -1)
    seg = lambda a, lo, n: a[:, lo:lo + n]
    w_perm = jnp.concatenate([
        seg(wi_fox, 0, 4 * fw),
        seg(wi_rw, 0, 3 * rw),
        seg(wi_rw, 3 * rw + w_lora + a_lora, g_lora),
        _pad_cols(seg(wi_rw, 3 * rw, w_lora), LANES),
        _pad_cols(jnp.concatenate([seg(wi_rw, 3 * rw + w_lora, a_lora), seg(wi_fox, 4 * fw, FOX_HEADS)], 1), LANES),
    ], axis=1).astype(BF16)
    cols = {"gate": 3 * fw, "rr": 4 * fw, "rk": 4 * fw + rw, "rv": 4 * fw + 2 * rw,
            "gd": 4 * fw + 3 * rw, "lo": 4 * fw + 3 * rw + g_lora}
    f_lane = a_lora
    f_block = (cols["lo"] + LANES) // LANES
    mus = [seg(mu, 0, rw), seg(mu, rw, rw), seg(mu, 2 * rw, rw),
           seg(mu, 3 * rw + w_lora + a_lora, g_lora),
           jnp.concatenate([_pad_cols(seg(mu, 3 * rw, w_lora), LANES),
                            _pad_cols(seg(mu, 3 * rw + w_lora, a_lora), LANES)], 1)]
    scale = FOX_HEAD_DIM ** -0.5 * LOG2_E
    head_w = _pad_cols(jnp.concatenate([jnp.tile(fox_q_norm_w * scale, FOX_HEADS),
                                        jnp.tile(fox_k_norm_w, FOX_HEADS)]).reshape(1, -1), w_perm.shape[1])

    x2 = x.reshape(T, D)
    proj = _in_proj(x2, norm_mix_w.reshape(1, D), sc1, sh1, w_perm, head_w, S, 2 * fw)

    f_bias_row = jnp.zeros((1, LANES), F32).at[0, f_lane:f_lane + FOX_HEADS].set(fox_f_bias)
    cum = _fox_cum(proj, f_bias_row, B, S, f_block)
    cum_rows = cum.reshape(B, S, LANES)[:, :, f_lane:f_lane + FOX_HEADS].transpose(0, 2, 1).reshape(B, FOX_HEADS, 1, S)
    o_fox = _fox_attn(proj, cum_rows, B, S)

    row = lambda a: a.reshape(1, -1)
    r, lw, k, v, a_vec, b_vec, g, bonus = _rwkv_prep(
        proj, cols, mus, row(rwkv_w0), _pad_rows(rwkv_w_up, LANES), row(rwkv_a0), _pad_rows(rwkv_a_up, LANES),
        rwkv_g_up, row(rwkv_k_k), row(rwkv_k_a), row(rwkv_r_k), S)
    y = _rwkv_scan(r, lw, k, v, a_vec, b_vec, B, S)

    x1 = _mix_out(o_fox, proj, cols["gate"], y, bonus, g, x2, g1, row(rwkv_ln_w), row(rwkv_ln_b),
                  w_out.astype(BF16), S)

    h2, experts_t, gates_t = _peer_route(x1, norm_ffn_w.reshape(1, D), sc2, sh2,
                                         peer_w_query.astype(BF16), peer_sub_keys, S)
    gate_mat = _peer_gate(experts_t.T, gates_t.T).reshape(T, PEER_NKEYS * PEER_NKEYS)
    out = _peer_expert(h2, peer_u.astype(BF16), peer_v.astype(BF16), gate_mat, x1, g2, S)
    return out.reshape(B, S, D)


def kernel(x, c, w_ada, b_ada, norm_mix_w, w_in, fox_q_norm_w, fox_k_norm_w, fox_f_bias, rwkv_mu, rwkv_w0,
           rwkv_w_up, rwkv_a0, rwkv_a_up, rwkv_g_up, rwkv_k_k, rwkv_k_a, rwkv_r_k, rwkv_ln_w, rwkv_ln_b,
           w_out, norm_ffn_w, peer_w_query, peer_sub_keys, peer_u, peer_v):
    params = (w_ada, b_ada, norm_mix_w, w_in, fox_q_norm_w, fox_k_norm_w, fox_f_bias, rwkv_mu, rwkv_w0,
              rwkv_w_up, rwkv_a0, rwkv_a_up, rwkv_g_up, rwkv_k_k, rwkv_k_a, rwkv_r_k, rwkv_ln_w, rwkv_ln_b,
              w_out, norm_ffn_w, peer_w_query, peer_sub_keys, peer_u, peer_v)
    for l in range(w_ada.shape[0]):
        x = _layer(x, c, *[p[l] for p in params])
    return x
```

```python
import functools

import jax
import jax.numpy as jnp
from jax import lax
from jax.experimental import pallas as pl
from jax.experimental.pallas import tpu as pltpu

F32 = jnp.float32
BF16 = jnp.bfloat16
HIGHEST = lax.Precision.HIGHEST

LANES = 128
NORM_EPS = 1e-6
GN_EPS = 64e-5
CHUNK = 64
FOX_HEADS = 8
FOX_HEAD_DIM = 128
RWKV_HEADS = 16
RWKV_HEAD_DIM = 64
PEER_HEADS = 8
PEER_NKEYS = 128
PEER_TOPK = 16
NEG_BIG = -1e30
LOG2_E = 1.4426950408889634
VMEM_LIMIT = 56 * 1024 * 1024


def _cparams(sem):
    return pltpu.CompilerParams(dimension_semantics=sem, vmem_limit_bytes=VMEM_LIMIT)


def _dot(a, b, precision=None):
    return jnp.dot(a, b, preferred_element_type=F32, precision=precision)


def _dot_nt(a, b, precision=None):
    return lax.dot_general(a, b, (((1,), (1,)), ((), ())), preferred_element_type=F32,
                           precision=precision)


def _dot_tn(a, b, precision=None):
    return lax.dot_general(a, b, (((0,), (0,)), ((), ())), preferred_element_type=F32,
                           precision=precision)


def _ada_kernel(c_ref, w_ref, b_ref, o_ref):
    c = c_ref[...]
    s = c * jax.nn.sigmoid(c)
    o_ref[...] = _dot(s, w_ref[...], HIGHEST) + b_ref[...]


def _ada_mod(c_pad, w_ada, b_ada):
    rows, d = c_pad.shape
    n = w_ada.shape[1]
    bn = 1024
    return pl.pallas_call(
        _ada_kernel,
        grid=(n // bn,),
        in_specs=[pl.BlockSpec((rows, d), lambda j: (0, 0)),
                  pl.BlockSpec((d, bn), lambda j: (0, j)),
                  pl.BlockSpec((1, bn), lambda j: (0, j))],
        out_specs=pl.BlockSpec((rows, bn), lambda j: (0, j)),
        out_shape=jax.ShapeDtypeStruct((rows, n), F32),
        compiler_params=_cparams(("arbitrary",)),
        name="ada_mod",
    )(c_pad, w_ada, b_ada.reshape(1, n))


def _modulated_norm(x, nw, sc, sh):
    y = x * lax.rsqrt(jnp.mean(x * x, axis=-1, keepdims=True) + NORM_EPS)
    return y * nw * (1.0 + sc) + sh


def _in_proj_kernel(x_ref, nw_ref, sc_ref, sh_ref, w_ref, hw_ref, o_ref, h_scr, *, n_qk_blocks):
    j = pl.program_id(1)

    @pl.when(j == 0)
    def _():
        h = _modulated_norm(x_ref[...], nw_ref[...], sc_ref[0], sh_ref[0])
        h_scr[...] = h.astype(BF16)

    acc = _dot(h_scr[...], w_ref[...])

    @pl.when(j < n_qk_blocks)
    def _():
        for hh in range(acc.shape[1] // FOX_HEAD_DIM):
            sl = slice(hh * FOX_HEAD_DIM, (hh + 1) * FOX_HEAD_DIM)
            a = acc[:, sl]
            rs = lax.rsqrt(jnp.mean(a * a, axis=-1, keepdims=True) + NORM_EPS)
            o_ref[:, sl] = a * rs * hw_ref[:, sl]

    @pl.when(j >= n_qk_blocks)
    def _():
        o_ref[...] = acc


def _in_proj(x2, norm_w, sc, sh, w_bf, head_w, tokens_per_batch, n_qk_cols):
    t, d = x2.shape
    n = w_bf.shape[1]
    tm = min(1024, tokens_per_batch)
    bn = 512
    tpb = tokens_per_batch // tm
    kern = functools.partial(_in_proj_kernel, n_qk_blocks=n_qk_cols // bn)
    return pl.pallas_call(
        kern,
        grid=(t // tm, n // bn),
        in_specs=[pl.BlockSpec((tm, d), lambda i, j: (i, 0)),
                  pl.BlockSpec((1, d), lambda i, j: (0, 0)),
                  pl.BlockSpec((1, 1, d), lambda i, j: (i // tpb, 0, 0)),
                  pl.BlockSpec((1, 1, d), lambda i, j: (i // tpb, 0, 0)),
                  pl.BlockSpec((d, bn), lambda i, j: (0, j)),
                  pl.BlockSpec((1, bn), lambda i, j: (0, j))],
        out_specs=pl.BlockSpec((tm, bn), lambda i, j: (i, j)),
        out_shape=jax.ShapeDtypeStruct((t, n), F32),
        scratch_shapes=[pltpu.VMEM((tm, d), BF16)],
        compiler_params=_cparams(("parallel", "arbitrary")),
        name="in_proj",
    )(x2, norm_w, sc, sh, w_bf, head_w)


def _log_sigmoid(x):
    return jnp.minimum(x, 0.0) - jnp.log(1.0 + jnp.exp(-jnp.abs(x)))


def _fox_cum_kernel(f_ref, b_ref, o_ref, *, blk):
    s = f_ref.shape[0]
    row = lax.broadcasted_iota(jnp.int32, (blk, blk), 0)
    col = lax.broadcasted_iota(jnp.int32, (blk, blk), 1)
    tri = (row >= col).astype(F32)
    carry = jnp.zeros((1, f_ref.shape[1]), F32)
    for i in range(s // blk):
        lf = _log_sigmoid(f_ref[i * blk:(i + 1) * blk, :] + b_ref[...])
        cs = _dot(tri, lf, HIGHEST) + carry
        o_ref[i * blk:(i + 1) * blk, :] = cs * LOG2_E
        carry = cs[blk - 1:blk, :]


def _fox_cum(proj, f_bias_row, batch, seq, col_block):
    blk = min(256, seq)
    return pl.pallas_call(
        functools.partial(_fox_cum_kernel, blk=blk),
        grid=(batch,),
        in_specs=[pl.BlockSpec((seq, LANES), lambda b: (b, col_block)),
                  pl.BlockSpec((1, LANES), lambda b: (0, 0))],
        out_specs=pl.BlockSpec((seq, LANES), lambda b: (b, 0)),
        out_shape=jax.ShapeDtypeStruct((batch * seq, LANES), F32),
        compiler_params=_cparams(("parallel",)),
        name="fox_cum",
    )(proj, f_bias_row)


def _fox_attn_kernel(qt_ref, kt_ref, q_ref, k_ref, v_ref, c_ref, o_ref, m_scr, l_scr, acc_scr):
    t = pl.program_id(2)
    qi = qt_ref[t]
    ki = kt_ref[t]
    hd = FOX_HEAD_DIM
    n_heads = q_ref.shape[1] // hd

    @pl.when(ki == 0)
    def _():
        m_scr[...] = jnp.full(m_scr.shape, NEG_BIG, F32)
        l_scr[...] = jnp.zeros(l_scr.shape, F32)
        acc_scr[...] = jnp.zeros(acc_scr.shape, F32)

    def step(masked):
        for h in range(n_heads):
            sl = slice(h * hd, (h + 1) * hd)
            q = q_ref[:, sl].astype(BF16)
            k = k_ref[:, sl].astype(BF16)
            s = _dot_nt(q, k) - c_ref[0, h]
            if masked:
                row = lax.broadcasted_iota(jnp.int32, s.shape, 0)
                col = lax.broadcasted_iota(jnp.int32, s.shape, 1)
                s = jnp.where(row >= col, s, NEG_BIG)
            m_prev = m_scr[h]
            m_new = jnp.maximum(m_prev, jnp.max(s, axis=-1, keepdims=True))
            alpha = jnp.exp2(m_prev - m_new)
            p = jnp.exp2(s - m_new)
            l_scr[h] = alpha * l_scr[h] + jnp.sum(p, axis=-1, keepdims=True)
            acc_scr[:, sl] = alpha * acc_scr[:, sl] + _dot(p.astype(BF16), v_ref[:, sl].astype(BF16))
            m_scr[h] = m_new

    @pl.when(ki < qi)
    def _():
        step(False)

    @pl.when(ki == qi)
    def _():
        step(True)
        for h in range(n_heads):
            sl = slice(h * hd, (h + 1) * hd)
            o_ref[:, sl] = acc_scr[:, sl] / l_scr[h]


def _fox_attn(proj, cum_rows, batch, seq):
    tq = min(512, seq)
    nq = seq // tq
    hps = 2
    w = hps * FOX_HEAD_DIM
    kcol = FOX_HEADS // hps
    tri = [(q, k) for q in range(nq) for k in range(q + 1)]
    qt = jnp.asarray([q for q, _ in tri], jnp.int32)
    kt = jnp.asarray([k for _, k in tri], jnp.int32)
    grid_spec = pltpu.PrefetchScalarGridSpec(
        num_scalar_prefetch=2,
        grid=(batch, kcol, len(tri)),
        in_specs=[
            pl.BlockSpec((tq, w), lambda b, h, t, qt, kt: (b * nq + qt[t], h)),
            pl.BlockSpec((tq, w), lambda b, h, t, qt, kt: (b * nq + kt[t], kcol + h)),
            pl.BlockSpec((tq, w), lambda b, h, t, qt, kt: (b * nq + kt[t], 2 * kcol + h)),
            pl.BlockSpec((1, hps, 1, tq), lambda b, h, t, qt, kt: (b, h, 0, kt[t])),
        ],
        out_specs=pl.BlockSpec((tq, w), lambda b, h, t, qt, kt: (b * nq + qt[t], h)),
        scratch_shapes=[pltpu.VMEM((hps, tq, 1), F32), pltpu.VMEM((hps, tq, 1), F32),
                        pltpu.VMEM((tq, w), F32)],
    )
    return pl.pallas_call(
        _fox_attn_kernel,
        grid_spec=grid_spec,
        out_shape=jax.ShapeDtypeStruct((batch * seq, FOX_HEADS * FOX_HEAD_DIM), F32),
        compiler_params=_cparams(("parallel", "parallel", "arbitrary")),
        name="fox_attn",
    )(qt, kt, proj, proj, proj, cum_rows)


def _head_sum(x, bd):
    parts = [_dot(x[:, j * LANES:(j + 1) * LANES], bd, HIGHEST) for j in range(x.shape[1] // LANES)]
    return jnp.concatenate(parts, axis=1)


def _head_block_diag():
    r = lax.broadcasted_iota(jnp.int32, (LANES, LANES), 0) // RWKV_HEAD_DIM
    c = lax.broadcasted_iota(jnp.int32, (LANES, LANES), 1) // RWKV_HEAD_DIM
    return (r == c).astype(F32)


def _rwkv_prep_kernel(r_ref, k_ref, v_ref, gd_ref, lo_ref, pr_ref, pk_ref, pv_ref, pgd_ref, plo_ref,
                      mu_r, mu_k, mu_v, mu_gd, mu_lo, w0_ref, wup_ref, a0_ref, aup_ref, gup_ref,
                      kk_ref, ka_ref, rk_ref,
                      or_ref, olw_ref, ok_ref, ov_ref, oa_ref, ob_ref, og_ref, obonus_ref, *, tiles_per_batch):
    i = pl.program_id(0)
    first = (i % tiles_per_batch) == 0

    def shifted(cur_ref, prv_ref, mu_ref):
        cur = cur_ref[...]
        last = jnp.where(first, 0.0, prv_ref[7:8, :])
        row = lax.broadcasted_iota(jnp.int32, cur.shape, 0)
        prev = jnp.where(row == 0, last, pltpu.roll(cur, 1, 0))
        return cur + (prev - cur) * mu_ref[...]

    r = shifted(r_ref, pr_ref, mu_r)
    k = shifted(k_ref, pk_ref, mu_k)
    v = shifted(v_ref, pv_ref, mu_v)
    gd = shifted(gd_ref, pgd_ref, mu_gd)
    lo = shifted(lo_ref, plo_ref, mu_lo)
    wd = lo[:, :LANES]
    ad = lo[:, LANES:]

    w_pre = w0_ref[...] + _dot(jnp.tanh(wd), wup_ref[...], HIGHEST)
    w_raw = _log_sigmoid(w_pre) - 0.5
    log_decay = -jnp.exp(w_raw)
    a = jax.nn.sigmoid(a0_ref[...] + _dot(ad, aup_ref[...], HIGHEST))
    g = _dot(jax.nn.sigmoid(gd), gup_ref[...], HIGHEST)

    bd = _head_block_diag()
    kk = k * kk_ref[...]
    nrm = jnp.maximum(jnp.sqrt(_head_sum(kk * kk, bd)), 1e-12)
    kk = kk / nrm
    k_mod = k * (1.0 + (a - 1.0) * ka_ref[...])
    bonus = _head_sum(r * k_mod * rk_ref[...], bd) * v

    or_ref[...] = r
    olw_ref[...] = log_decay
    ok_ref[...] = k_mod
    ov_ref[...] = v
    oa_ref[...] = -kk
    ob_ref[...] = kk * a
    og_ref[...] = g
    obonus_ref[...] = bonus


def _rwkv_prep(proj, cols, mus, w0, w_up, a0, a_up, g_up, k_k, k_a, r_k, tokens_per_batch):
    t = proj.shape[0]
    w = RWKV_HEADS * RWKV_HEAD_DIM
    tm = min(256, tokens_per_batch)
    tpb = tokens_per_batch // tm
    widths = [w, w, w, 256, 256]
    offs = [cols["rr"], cols["rk"], cols["rv"], cols["gd"], cols["lo"]]
    cur_specs = [pl.BlockSpec((tm, wd), functools.partial(lambda i, cb: (i, cb), cb=o // wd))
                 for wd, o in zip(widths, offs)]
    prv_specs = [pl.BlockSpec((8, wd), functools.partial(
        lambda i, cb: (jnp.maximum(i * (tm // 8) - 1, 0), cb), cb=o // wd))
        for wd, o in zip(widths, offs)]
    full = lambda a: pl.BlockSpec(a.shape, lambda i: (0,) * a.ndim)
    params = list(mus) + [w0, w_up, a0, a_up, g_up, k_k, k_a, r_k]
    out_spec = pl.BlockSpec((tm, w), lambda i: (i, 0))
    return pl.pallas_call(
        functools.partial(_rwkv_prep_kernel, tiles_per_batch=tpb),
        grid=(t // tm,),
        in_specs=cur_specs + prv_specs + [full(p) for p in params],
        out_specs=[out_spec] * 8,
        out_shape=[jax.ShapeDtypeStruct((t, w), F32)] * 8,
        compiler_params=_cparams(("parallel",)),
        name="rwkv_prep",
    )(*([proj] * 10), *params)


def _rwkv_scan_kernel(r_ref, lw_ref, k_ref, v_ref, a_ref, b_ref, o_ref, h_scr):
    c = pl.program_id(1)

    @pl.when(c == 0)
    def _():
        h_scr[...] = jnp.zeros(h_scr.shape, F32)

    C = r_ref.shape[0]
    n_pairs = r_ref.shape[1] // LANES
    P = HIGHEST

    row = lax.broadcasted_iota(jnp.int32, (C, C), 0)
    col = lax.broadcasted_iota(jnp.int32, (C, C), 1)
    tri = (row >= col).astype(F32)
    lw = lw_ref[...]
    cw = _dot(tri, lw, P)
    cw_end = cw[C - 1:C, :]
    e_pos = jnp.exp(cw)
    e_prev = jnp.exp(cw - lw)
    e_neg = jnp.exp(-cw)
    e_end = jnp.exp(cw_end - cw)
    w_end = jnp.exp(cw_end)

    a = a_ref[...]
    b = b_ref[...]
    k = k_ref[...]
    r = r_ref[...]
    at = a * e_prev
    bt = b * e_neg
    kt = k * e_neg
    rt = r * e_pos
    bh = b * e_end
    kh = k * e_end
    v = v_ref[...]

    lane = lax.broadcasted_iota(jnp.int32, (C, LANES), 1)
    head0 = lane < RWKV_HEAD_DIM
    r2 = lax.broadcasted_iota(jnp.int32, (2 * C, 2 * C), 0)
    c2 = lax.broadcasted_iota(jnp.int32, (2 * C, 2 * C), 1)
    same = (r2 // C) == (c2 // C)
    strict = same & (r2 > c2)
    incl = same & (r2 >= c2)
    eye = (r2 == c2).astype(F32)

    def two(x, p):
        xp = x[:, p * LANES:(p + 1) * LANES]
        return jnp.concatenate([jnp.where(head0, xp, 0.0), jnp.where(head0, 0.0, xp)], axis=0)

    def mm(x, y):
        return _dot(x.astype(BF16), y.astype(BF16))

    def split(x):
        hi = x.astype(BF16)
        return hi, (x - hi.astype(F32)).astype(BF16)

    def mm3(x, y):
        xh, xl = split(x)
        yh, yl = split(y)
        return _dot(xh, yh) + (_dot(xh, yl) + _dot(xl, yh))

    G = 2 * C
    pairs = range(n_pairs)
    at2 = [two(at, p) for p in pairs]
    rt2 = [two(rt, p) for p in pairs]
    v2 = [two(v, p).astype(BF16) for p in pairs]
    gram = [_dot_nt(jnp.concatenate([at2[p], rt2[p]], axis=0).astype(BF16),
                    jnp.concatenate([two(bt, p), two(kt, p)], axis=0).astype(BF16)) for p in pairs]
    a_ab = [jnp.where(strict, gram[p][:G, :G], 0.0) for p in pairs]
    nmat = list(a_ab)
    pw = list(a_ab)
    for _ in range(max(1, (C - 1).bit_length() - 1)):
        pw = [mm3(pw[p], pw[p]) for p in pairs]
        nmat = [nmat[p] + pw[p] + mm3(nmat[p], pw[p]) for p in pairs]
    akv = [mm(jnp.where(strict, gram[p][:G, G:], 0.0), v2[p]) for p in pairs]
    rhs = [jnp.concatenate([at2[p], akv[p]], axis=1) for p in pairs]
    pq = [(rhs[p] + mm(nmat[p], rhs[p])).astype(BF16) for p in pairs]
    ry = [mm(jnp.where(incl, gram[p][G:, :G], 0.0), pq[p]) for p in pairs]
    mv = [mm(jnp.where(incl, gram[p][G:, G:], 0.0), v2[p]) for p in pairs]
    ge = [_dot_tn(two(bh, p).astype(BF16), pq[p]) for p in pairs]
    kv = [_dot_tn(two(kh, p).astype(BF16), v2[p]) for p in pairs]
    for p in pairs:
        rr = rt2[p] + ry[p][:, :LANES]
        gm = eye * w_end[:, p * LANES:(p + 1) * LANES] + ge[p][:, :LANES]
        yh = mm(jnp.concatenate([rr, gm], axis=0), h_scr[p])
        y2 = yh[:G] + ry[p][:, LANES:] + mv[p]
        h_scr[p] = yh[G:] + ge[p][:, LANES:] + kv[p]
        o_ref[:, p * LANES:(p + 1) * LANES] = y2[:C, :] + y2[C:, :]


def _rwkv_scan(r, lw, k, v, a, b, batch, seq):
    w = r.shape[1]
    nc = seq // CHUNK
    spec = pl.BlockSpec((CHUNK, w), lambda bi, ci: (bi * nc + ci, 0))
    return pl.pallas_call(
        _rwkv_scan_kernel,
        grid=(batch, nc),
        in_specs=[spec] * 6,
        out_specs=spec,
        out_shape=jax.ShapeDtypeStruct((batch * seq, w), F32),
        scratch_shapes=[pltpu.VMEM((w // LANES, LANES, LANES), F32)],
        compiler_params=_cparams(("parallel", "arbitrary")),
        name="rwkv_scan",
    )(r, lw, k, v, a, b)


def _mix_out_kernel(o_ref, gate_ref, y_ref, bonus_ref, g_ref, x_ref, g1_ref, lnw_ref, lnb_ref, w_ref, out_ref):
    fox = o_ref[...] * jax.nn.sigmoid(gate_ref[...])
    bd = _head_block_diag()
    y = y_ref[...]
    inv_n = 1.0 / RWKV_HEAD_DIM
    mean = _head_sum(y, bd) * inv_n
    d = y - mean
    var = _head_sum(d * d, bd) * inv_n
    yn = d * lax.rsqrt(var + GN_EPS) * lnw_ref[...] + lnb_ref[...]
    rw = (yn + bonus_ref[...]) * g_ref[...]
    wf = fox.shape[1]
    mix = _dot(fox.astype(BF16), w_ref[:wf, :]) + _dot(rw.astype(BF16), w_ref[wf:, :])
    out_ref[...] = x_ref[...] + g1_ref[0] * mix


def _mix_out(o_fox, proj, gate_col, y, bonus, g, x2, g1, ln_w, ln_b, w_out_bf, tokens_per_batch):
    t, d = x2.shape
    wf = o_fox.shape[1]
    wr = y.shape[1]
    tm = min(256, tokens_per_batch)
    tpb = tokens_per_batch // tm
    return pl.pallas_call(
        _mix_out_kernel,
        grid=(t // tm,),
        in_specs=[pl.BlockSpec((tm, wf), lambda i: (i, 0)),
                  pl.BlockSpec((tm, wf), lambda i: (i, gate_col // wf)),
                  pl.BlockSpec((tm, wr), lambda i: (i, 0)),
                  pl.BlockSpec((tm, wr), lambda i: (i, 0)),
                  pl.BlockSpec((tm, wr), lambda i: (i, 0)),
                  pl.BlockSpec((tm, d), lambda i: (i, 0)),
                  pl.BlockSpec((1, 1, d), lambda i: (i // tpb, 0, 0)),
                  pl.BlockSpec((1, wr), lambda i: (0, 0)),
                  pl.BlockSpec((1, wr), lambda i: (0, 0)),
                  pl.BlockSpec(w_out_bf.shape, lambda i: (0, 0))],
        out_specs=pl.BlockSpec((tm, d), lambda i: (i, 0)),
        out_shape=jax.ShapeDtypeStruct((t, d), F32),
        compiler_params=_cparams(("parallel",)),
        name="mix_out",
    )(o_fox, proj, y, bonus, g, x2, g1, ln_w, ln_b, w_out_bf)


def _top_k_mask_rows(s, k):
    n = s.shape[0]
    row = lax.broadcasted_iota(jnp.int32, s.shape, 0).astype(F32)
    rank = jnp.full(s.shape, float(k), F32)
    vals = []
    for r in range(k):
        m = jnp.max(s, axis=0, keepdims=True)
        pos = jnp.min(jnp.where(s == m, row, float(n)), axis=0, keepdims=True)
        hit = row == pos
        vals.append(m)
        rank = jnp.where(hit, float(r), rank)
        s = jnp.where(hit, -jnp.inf, s)
    return jnp.concatenate(vals, axis=0), rank


def _candidate_rows():
    k = PEER_TOPK
    groups, valid = [], []
    for r0 in range(k // 2):
        n1 = k // (r0 + 1)
        for g in range(-(-n1 // 8)):
            groups.append((r0, g * 8))
            valid.append([g * 8 + i < n1 for i in range(8)])
    groups.append((None, k // 2))
    valid.append([True] * 8)
    return groups, valid


def _peer_route_kernel(x_ref, nw_ref, sc_ref, sh_ref, wq_ref, keys_ref,
                       h_ref, pa_ref, len_ref, qb_ref, rnk_ref):
    h = _modulated_norm(x_ref[...], nw_ref[...], sc_ref[0], sh_ref[0]).astype(BF16)
    h_ref[...] = h
    q = _dot(h, wq_ref[...])
    tm = q.shape[0]
    half = keys_ref.shape[-1]
    k = PEER_TOPK
    groups, valid = _candidate_rows()
    sub8 = lax.broadcasted_iota(jnp.int32, (8, tm), 0)
    for hd in range(PEER_HEADS):
        qa = q[:, (hd * 2) * half:(hd * 2 + 1) * half]
        qb = q[:, (hd * 2 + 1) * half:(hd * 2 + 2) * half]
        sa = _dot_nt(keys_ref[hd, 0], qa, HIGHEST)
        sb = _dot_nt(keys_ref[hd, 1], qb, HIGHEST)
        top_a, rank_a = _top_k_mask_rows(sa, k)
        top_b, rank_b = _top_k_mask_rows(sb, k)
        parts = []
        for (r0, off), ok in zip(groups, valid):
            if r0 is None:
                part = top_a[off:off + 8, :] + top_b[0:1, :]
            else:
                part = top_a[r0:r0 + 1, :] + top_b[off:off + 8, :]
            if not all(ok):
                n_ok = sum(ok)
                part = jnp.where(sub8 < n_ok, part, -jnp.inf)
            parts.append(part)
        cand = jnp.concatenate(parts, axis=0)
        _, crank = _top_k_mask_rows(cand, k)
        sel = crank < float(k)
        z = jnp.sum(jnp.where(sel, jnp.exp(cand - cand[0:1, :]), 0.0), axis=0, keepdims=True)
        self32 = jnp.where(sel, 1.0, 0.0)
        length = jnp.zeros_like(sa)
        tail = None
        for gi, (r0, off) in enumerate(groups):
            cnt = self32[gi * 8:(gi + 1) * 8, :]
            if r0 is None:
                tail = cnt
            else:
                length = length + jnp.where(rank_a == float(r0), jnp.sum(cnt, axis=0, keepdims=True), 0.0)
        for r in range(8):
            length = length + jnp.where(rank_a == float(k // 2 + r), tail[r:r + 1, :], 0.0)
        pa_ref[hd] = jnp.exp(sa - top_a[0:1, :])
        len_ref[hd] = length
        qb_ref[hd] = jnp.exp(sb - top_b[0:1, :]) / z
        rnk_ref[hd] = rank_b


def _peer_route(x1, norm_w, sc, sh, wq_bf, sub_keys, tokens_per_batch):
    t, d = x1.shape
    tm = min(256, tokens_per_batch)
    tpb = tokens_per_batch // tm
    dense = jax.ShapeDtypeStruct((PEER_HEADS, PEER_NKEYS, t), F32)
    dense_spec = pl.BlockSpec((PEER_HEADS, PEER_NKEYS, tm), lambda i: (0, 0, i))
    return pl.pallas_call(
        _peer_route_kernel,
        grid=(t // tm,),
        in_specs=[pl.BlockSpec((tm, d), lambda i: (i, 0)),
                  pl.BlockSpec((1, d), lambda i: (0, 0)),
                  pl.BlockSpec((1, 1, d), lambda i: (i // tpb, 0, 0)),
                  pl.BlockSpec((1, 1, d), lambda i: (i // tpb, 0, 0)),
                  pl.BlockSpec(wq_bf.shape, lambda i: (0, 0)),
                  pl.BlockSpec(sub_keys.shape, lambda i: (0, 0, 0, 0))],
        out_specs=[pl.BlockSpec((tm, d), lambda i: (i, 0))] + [dense_spec] * 4,
        out_shape=[jax.ShapeDtypeStruct((t, d), BF16)] + [dense] * 4,
        compiler_params=_cparams(("parallel",)),
        name="peer_route",
    )(x1, norm_w, sc, sh, wq_bf, sub_keys)


def _gelu_exact(x):
    return 0.5 * x * (1.0 + lax.erf(x * 0.7071067811865476))


def _peer_expert_kernel(h_ref, u_ref, v_ref, pa_ref, len_ref, qb_ref, rnk_ref, x_ref, g2_ref, o_ref):
    j = pl.program_id(1)
    be = u_ref.shape[0]
    n_slabs = be // PEER_NKEYS
    act = _gelu_exact(_dot_nt(u_ref[...], h_ref[...]))
    slabs = []
    for s in range(n_slabs):
        i = j * n_slabs + s
        g = None
        for hd in range(PEER_HEADS):
            ln = len_ref[hd, pl.ds(i, 1), :]
            pa = pa_ref[hd, pl.ds(i, 1), :]
            term = jnp.where(rnk_ref[hd] < ln, qb_ref[hd], 0.0) * pa
            g = term if g is None else g + term
        slabs.append(g * act[s * PEER_NKEYS:(s + 1) * PEER_NKEYS, :])
    p = jnp.concatenate(slabs, axis=0).astype(BF16)
    contrib = _dot_tn(p, v_ref[...])

    @pl.when(j == 0)
    def _():
        o_ref[...] = contrib

    @pl.when(j > 0)
    def _():
        o_ref[...] += contrib

    @pl.when(j == pl.num_programs(1) - 1)
    def _():
        o_ref[...] = x_ref[...] + g2_ref[0] * o_ref[...]


def _peer_expert(h_bf, u_bf, v_bf, pa, ln, qb, rnk, x1, g2, tokens_per_batch):
    t, d = x1.shape
    ne = u_bf.shape[0]
    tm = min(512, tokens_per_batch)
    tpb = tokens_per_batch // tm
    be = 512
    dense_spec = pl.BlockSpec((PEER_HEADS, PEER_NKEYS, tm), lambda i, j: (0, 0, i))
    return pl.pallas_call(
        _peer_expert_kernel,
        grid=(t // tm, ne // be),
        in_specs=[pl.BlockSpec((tm, d), lambda i, j: (i, 0)),
                  pl.BlockSpec((be, d), lambda i, j: (j, 0)),
                  pl.BlockSpec((be, d), lambda i, j: (j, 0)),
                  dense_spec, dense_spec, dense_spec, dense_spec,
                  pl.BlockSpec((tm, d), lambda i, j: (i, 0)),
                  pl.BlockSpec((1, 1, d), lambda i, j: (i // tpb, 0, 0))],
        out_specs=pl.BlockSpec((tm, d), lambda i, j: (i, 0)),
        out_shape=jax.ShapeDtypeStruct((t, d), F32),
        compiler_params=_cparams(("parallel", "arbitrary")),
        name="peer_expert",
    )(h_bf, u_bf, v_bf, pa, ln, qb, rnk, x1, g2)


def _pad_cols(a, n):
    return jnp.pad(a, ((0, 0), (0, n - a.shape[1])))


def _pad_rows(a, n):
    return jnp.pad(a, ((0, n - a.shape[0]), (0, 0)))


def _layer(x, c, w_ada, b_ada, norm_mix_w, w_in, fox_q_norm_w, fox_k_norm_w, fox_f_bias,
           rwkv_mu, rwkv_w0, rwkv_w_up, rwkv_a0, rwkv_a_up, rwkv_g_up, rwkv_k_k, rwkv_k_a,
           rwkv_r_k, rwkv_ln_w, rwkv_ln_b, w_out, norm_ffn_w, peer_w_query, peer_sub_keys,
           peer_u, peer_v):
    B, S, D = x.shape
    T = B * S
    fw = FOX_HEADS * FOX_HEAD_DIM
    rw = RWKV_HEADS * RWKV_HEAD_DIM
    w_lora = rwkv_w_up.shape[0]
    a_lora = rwkv_a_up.shape[0]
    g_lora = rwkv_g_up.shape[0]
    assert w_lora <= LANES and a_lora + FOX_HEADS <= LANES and g_lora == 256

    c_pad = _pad_rows(c, 8)
    mod = _ada_mod(c_pad, w_ada, b_ada)[:B]
    sh1, sc1, g1, sh2, sc2, g2 = [m.reshape(B, 1, D) for m in jnp.split(mod, 6, axis=-1)]

    fox_cols = 4 * fw + FOX_HEADS
    wi_fox, wi_rw = w_in[:, :fox_cols], w_in[:, fox_cols:]
    mu = rwkv_mu.reshape(1, -1)
    seg = lambda a, lo, n: a[:, lo:lo + n]
    w_perm = jnp.concatenate([
        seg(wi_fox, 0, 4 * fw),
        seg(wi_rw, 0, 3 * rw),
        seg(wi_rw, 3 * rw + w_lora + a_lora, g_lora),
        _pad_cols(seg(wi_rw, 3 * rw, w_lora), LANES),
        _pad_cols(jnp.concatenate([seg(wi_rw, 3 * rw + w_lora, a_lora), seg(wi_fox, 4 * fw, FOX_HEADS)], 1), LANES),
    ], axis=1).astype(BF16)
    cols = {"gate": 3 * fw, "rr": 4 * fw, "rk": 4 * fw + rw, "rv": 4 * fw + 2 * rw,
            "gd": 4 * fw + 3 * rw, "lo": 4 * fw + 3 * rw + g_lora}
    f_lane = a_lora
    f_block = (cols["lo"] + LANES) // LANES
    mus = [seg(mu, 0, rw), seg(mu, rw, rw), seg(mu, 2 * rw, rw),
           seg(mu, 3 * rw + w_lora + a_lora, g_lora),
           jnp.concatenate([_pad_cols(seg(mu, 3 * rw, w_lora), LANES),
                            _pad_cols(seg(mu, 3 * rw + w_lora, a_lora), LANES)], 1)]
    scale = FOX_HEAD_DIM ** -0.5 * LOG2_E
    head_w = _pad_cols(jnp.concatenate([jnp.tile(fox_q_norm_w * scale, FOX_HEADS),
                                        jnp.tile(fox_k_norm_w, FOX_HEADS)]).reshape(1, -1), w_perm.shape[1])

    x2 = x.reshape(T, D)
    proj = _in_proj(x2, norm_mix_w.reshape(1, D), sc1, sh1, w_perm, head_w, S, 2 * fw)

    f_bias_row = jnp.zeros((1, LANES), F32).at[0, f_lane:f_lane + FOX_HEADS].set(fox_f_bias)
    cum = _fox_cum(proj, f_bias_row, B, S, f_block)
    cum_rows = cum.reshape(B, S, LANES)[:, :, f_lane:f_lane + FOX_HEADS].transpose(0, 2, 1).reshape(B, FOX_HEADS, 1, S)
    o_fox = _fox_attn(proj, cum_rows, B, S)

    row = lambda a: a.reshape(1, -1)
    r, lw, k, v, a_vec, b_vec, g, bonus = _rwkv_prep(
        proj, cols, mus, row(rwkv_w0), _pad_rows(rwkv_w_up, LANES), row(rwkv_a0), _pad_rows(rwkv_a_up, LANES),
        rwkv_g_up, row(rwkv_k_k), row(rwkv_k_a), row(rwkv_r_k), S)
    y = _rwkv_scan(r, lw, k, v, a_vec, b_vec, B, S)

    x1 = _mix_out(o_fox, proj, cols["gate"], y, bonus, g, x2, g1, row(rwkv_ln_w), row(rwkv_ln_b),
                  w_out.astype(BF16), S)

    h2, pa, ln, qb, rnk = _peer_route(x1, norm_ffn_w.reshape(1, D), sc2, sh2,
                                      peer_w_query.astype(BF16), peer_sub_keys, S)
    out = _peer_expert(h2, peer_u.astype(BF16), peer_v.astype(BF16), pa, ln, qb, rnk, x1, g2, S)
    return out.reshape(B, S, D)


def kernel(x, c, w_ada, b_ada, norm_mix_w, w_in, fox_q_norm_w, fox_k_norm_w, fox_f_bias, rwkv_mu, rwkv_w0,
           rwkv_w_up, rwkv_a0, rwkv_a_up, rwkv_g_up, rwkv_k_k, rwkv_k_a, rwkv_r_k, rwkv_ln_w, rwkv_ln_b,
           w_out, norm_ffn_w, peer_w_query, peer_sub_keys, peer_u, peer_v):
    params = (w_ada, b_ada, norm_mix_w, w_in, fox_q_norm_w, fox_k_norm_w, fox_f_bias, rwkv_mu, rwkv_w0,
              rwkv_w_up, rwkv_a0, rwkv_a_up, rwkv_g_up, rwkv_k_k, rwkv_k_a, rwkv_r_k, rwkv_ln_w, rwkv_ln_b,
              w_out, norm_ffn_w, peer_w_query, peer_sub_keys, peer_u, peer_v)
    for l in range(w_ada.shape[0]):
        x = _layer(x, c, *[p[l] for p in params])
    return x
```

```python
import functools

import jax
import jax.numpy as jnp
from jax import lax
from jax.experimental import pallas as pl
from jax.experimental.pallas import tpu as pltpu

F32 = jnp.float32
BF16 = jnp.bfloat16
HIGHEST = lax.Precision.HIGHEST

LANES = 128
NORM_EPS = 1e-6
GN_EPS = 64e-5
CHUNK = 64
FOX_HEADS = 8
FOX_HEAD_DIM = 128
RWKV_HEADS = 16
RWKV_HEAD_DIM = 64
PEER_HEADS = 8
PEER_NKEYS = 128
PEER_TOPK = 16
NEG_BIG = -1e30
LOG2_E = 1.4426950408889634
VMEM_LIMIT = 56 * 1024 * 1024


def _cparams(sem):
    return pltpu.CompilerParams(dimension_semantics=sem, vmem_limit_bytes=VMEM_LIMIT)


def _dot(a, b, precision=None):
    return jnp.dot(a, b, preferred_element_type=F32, precision=precision)


def _dot_nt(a, b, precision=None):
    return lax.dot_general(a, b, (((1,), (1,)), ((), ())), preferred_element_type=F32,
                           precision=precision)


def _dot_tn(a, b, precision=None):
    return lax.dot_general(a, b, (((0,), (0,)), ((), ())), preferred_element_type=F32,
                           precision=precision)


def _ada_kernel(c_ref, w_ref, b_ref, o_ref):
    c = c_ref[...]
    s = c * jax.nn.sigmoid(c)
    o_ref[...] = _dot(s, w_ref[...], HIGHEST) + b_ref[...]


def _ada_mod(c_pad, w_ada, b_ada):
    rows, d = c_pad.shape
    n = w_ada.shape[1]
    bn = 1024
    return pl.pallas_call(
        _ada_kernel,
        grid=(n // bn,),
        in_specs=[pl.BlockSpec((rows, d), lambda j: (0, 0)),
                  pl.BlockSpec((d, bn), lambda j: (0, j)),
                  pl.BlockSpec((1, bn), lambda j: (0, j))],
        out_specs=pl.BlockSpec((rows, bn), lambda j: (0, j)),
        out_shape=jax.ShapeDtypeStruct((rows, n), F32),
        compiler_params=_cparams(("arbitrary",)),
        name="ada_mod",
    )(c_pad, w_ada, b_ada.reshape(1, n))


def _modulated_norm(x, nw, sc, sh):
    y = x * lax.rsqrt(jnp.mean(x * x, axis=-1, keepdims=True) + NORM_EPS)
    return y * nw * (1.0 + sc) + sh


def _in_proj_kernel(x_ref, nw_ref, sc_ref, sh_ref, w_ref, hw_ref, o_ref, h_scr, *, n_qk_blocks):
    j = pl.program_id(1)

    @pl.when(j == 0)
    def _():
        h = _modulated_norm(x_ref[...], nw_ref[...], sc_ref[0], sh_ref[0])
        h_scr[...] = h.astype(BF16)

    acc = _dot(h_scr[...], w_ref[...])

    @pl.when(j < n_qk_blocks)
    def _():
        for hh in range(acc.shape[1] // FOX_HEAD_DIM):
            sl = slice(hh * FOX_HEAD_DIM, (hh + 1) * FOX_HEAD_DIM)
            a = acc[:, sl]
            rs = lax.rsqrt(jnp.mean(a * a, axis=-1, keepdims=True) + NORM_EPS)
            o_ref[:, sl] = a * rs * hw_ref[:, sl]

    @pl.when(j >= n_qk_blocks)
    def _():
        o_ref[...] = acc


def _in_proj(x2, norm_w, sc, sh, w_bf, head_w, tokens_per_batch, n_qk_cols):
    t, d = x2.shape
    n = w_bf.shape[1]
    tm = min(1024, tokens_per_batch)
    bn = 512
    tpb = tokens_per_batch // tm
    kern = functools.partial(_in_proj_kernel, n_qk_blocks=n_qk_cols // bn)
    return pl.pallas_call(
        kern,
        grid=(t // tm, n // bn),
        in_specs=[pl.BlockSpec((tm, d), lambda i, j: (i, 0)),
                  pl.BlockSpec((1, d), lambda i, j: (0, 0)),
                  pl.BlockSpec((1, 1, d), lambda i, j: (i // tpb, 0, 0)),
                  pl.BlockSpec((1, 1, d), lambda i, j: (i // tpb, 0, 0)),
                  pl.BlockSpec((d, bn), lambda i, j: (0, j)),
                  pl.BlockSpec((1, bn), lambda i, j: (0, j))],
        out_specs=pl.BlockSpec((tm, bn), lambda i, j: (i, j)),
        out_shape=jax.ShapeDtypeStruct((t, n), F32),
        scratch_shapes=[pltpu.VMEM((tm, d), BF16)],
        compiler_params=_cparams(("parallel", "arbitrary")),
        name="in_proj",
    )(x2, norm_w, sc, sh, w_bf, head_w)


def _log_sigmoid(x):
    return jnp.minimum(x, 0.0) - jnp.log(1.0 + jnp.exp(-jnp.abs(x)))


def _fox_cum_kernel(f_ref, b_ref, o_ref, *, blk):
    s = f_ref.shape[0]
    row = lax.broadcasted_iota(jnp.int32, (blk, blk), 0)
    col = lax.broadcasted_iota(jnp.int32, (blk, blk), 1)
    tri = (row >= col).astype(F32)
    carry = jnp.zeros((1, f_ref.shape[1]), F32)
    for i in range(s // blk):
        lf = _log_sigmoid(f_ref[i * blk:(i + 1) * blk, :] + b_ref[...])
        cs = _dot(tri, lf, HIGHEST) + carry
        o_ref[i * blk:(i + 1) * blk, :] = cs * LOG2_E
        carry = cs[blk - 1:blk, :]


def _fox_cum(proj, f_bias_row, batch, seq, col_block):
    blk = min(256, seq)
    return pl.pallas_call(
        functools.partial(_fox_cum_kernel, blk=blk),
        grid=(batch,),
        in_specs=[pl.BlockSpec((seq, LANES), lambda b: (b, col_block)),
                  pl.BlockSpec((1, LANES), lambda b: (0, 0))],
        out_specs=pl.BlockSpec((seq, LANES), lambda b: (b, 0)),
        out_shape=jax.ShapeDtypeStruct((batch * seq, LANES), F32),
        compiler_params=_cparams(("parallel",)),
        name="fox_cum",
    )(proj, f_bias_row)


def _fox_attn_kernel(qt_ref, kt_ref, q_ref, k_ref, v_ref, c_ref, o_ref, m_scr, l_scr, acc_scr):
    t = pl.program_id(2)
    qi = qt_ref[t]
    ki = kt_ref[t]
    hd = FOX_HEAD_DIM
    n_heads = q_ref.shape[1] // hd

    @pl.when(ki == 0)
    def _():
        m_scr[...] = jnp.full(m_scr.shape, NEG_BIG, F32)
        l_scr[...] = jnp.zeros(l_scr.shape, F32)
        acc_scr[...] = jnp.zeros(acc_scr.shape, F32)

    def step(masked):
        for h in range(n_heads):
            sl = slice(h * hd, (h + 1) * hd)
            q = q_ref[:, sl].astype(BF16)
            k = k_ref[:, sl].astype(BF16)
            s = _dot_nt(q, k) - c_ref[0, h]
            if masked:
                row = lax.broadcasted_iota(jnp.int32, s.shape, 0)
                col = lax.broadcasted_iota(jnp.int32, s.shape, 1)
                s = jnp.where(row >= col, s, NEG_BIG)
            m_prev = m_scr[h]
            m_new = jnp.maximum(m_prev, jnp.max(s, axis=-1, keepdims=True))
            alpha = jnp.exp2(m_prev - m_new)
            p = jnp.exp2(s - m_new)
            l_scr[h] = alpha * l_scr[h] + jnp.sum(p, axis=-1, keepdims=True)
            acc_scr[:, sl] = alpha * acc_scr[:, sl] + _dot(p.astype(BF16), v_ref[:, sl].astype(BF16))
            m_scr[h] = m_new

    @pl.when(ki < qi)
    def _():
        step(False)

    @pl.when(ki == qi)
    def _():
        step(True)
        for h in range(n_heads):
            sl = slice(h * hd, (h + 1) * hd)
            o_ref[:, sl] = acc_scr[:, sl] / l_scr[h]


def _fox_attn(proj, cum_rows, batch, seq):
    tq = min(512, seq)
    nq = seq // tq
    hps = 2
    w = hps * FOX_HEAD_DIM
    kcol = FOX_HEADS // hps
    tri = [(q, k) for q in range(nq) for k in range(q + 1)]
    qt = jnp.asarray([q for q, _ in tri], jnp.int32)
    kt = jnp.asarray([k for _, k in tri], jnp.int32)
    grid_spec = pltpu.PrefetchScalarGridSpec(
        num_scalar_prefetch=2,
        grid=(batch, kcol, len(tri)),
        in_specs=[
            pl.BlockSpec((tq, w), lambda b, h, t, qt, kt: (b * nq + qt[t], h)),
            pl.BlockSpec((tq, w), lambda b, h, t, qt, kt: (b * nq + kt[t], kcol + h)),
            pl.BlockSpec((tq, w), lambda b, h, t, qt, kt: (b * nq + kt[t], 2 * kcol + h)),
            pl.BlockSpec((1, hps, 1, tq), lambda b, h, t, qt, kt: (b, h, 0, kt[t])),
        ],
        out_specs=pl.BlockSpec((tq, w), lambda b, h, t, qt, kt: (b * nq + qt[t], h)),
        scratch_shapes=[pltpu.VMEM((hps, tq, 1), F32), pltpu.VMEM((hps, tq, 1), F32),
                        pltpu.VMEM((tq, w), F32)],
    )
    return pl.pallas_call(
        _fox_attn_kernel,
        grid_spec=grid_spec,
        out_shape=jax.ShapeDtypeStruct((batch * seq, FOX_HEADS * FOX_HEAD_DIM), F32),
        compiler_params=_cparams(("parallel", "parallel", "arbitrary")),
        name="fox_attn",
    )(qt, kt, proj, proj, proj, cum_rows)


def _head_sum(x, bd):
    parts = [_dot(x[:, j * LANES:(j + 1) * LANES], bd, HIGHEST) for j in range(x.shape[1] // LANES)]
    return jnp.concatenate(parts, axis=1)


def _head_block_diag():
    r = lax.broadcasted_iota(jnp.int32, (LANES, LANES), 0) // RWKV_HEAD_DIM
    c = lax.broadcasted_iota(jnp.int32, (LANES, LANES), 1) // RWKV_HEAD_DIM
    return (r == c).astype(F32)


def _rwkv_prep_kernel(r_ref, k_ref, v_ref, gd_ref, lo_ref, pr_ref, pk_ref, pv_ref, pgd_ref, plo_ref,
                      mu_r, mu_k, mu_v, mu_gd, mu_lo, w0_ref, wup_ref, a0_ref, aup_ref, gup_ref,
                      kk_ref, ka_ref, rk_ref,
                      or_ref, olw_ref, ok_ref, ov_ref, oa_ref, ob_ref, og_ref, obonus_ref, *, tiles_per_batch):
    i = pl.program_id(0)
    first = (i % tiles_per_batch) == 0

    def shifted(cur_ref, prv_ref, mu_ref):
        cur = cur_ref[...]
        last = jnp.where(first, 0.0, prv_ref[7:8, :])
        row = lax.broadcasted_iota(jnp.int32, cur.shape, 0)
        prev = jnp.where(row == 0, last, pltpu.roll(cur, 1, 0))
        return cur + (prev - cur) * mu_ref[...]

    r = shifted(r_ref, pr_ref, mu_r)
    k = shifted(k_ref, pk_ref, mu_k)
    v = shifted(v_ref, pv_ref, mu_v)
    gd = shifted(gd_ref, pgd_ref, mu_gd)
    lo = shifted(lo_ref, plo_ref, mu_lo)
    wd = lo[:, :LANES]
    ad = lo[:, LANES:]

    w_pre = w0_ref[...] + _dot(jnp.tanh(wd), wup_ref[...], HIGHEST)
    w_raw = _log_sigmoid(w_pre) - 0.5
    log_decay = -jnp.exp(w_raw)
    a = jax.nn.sigmoid(a0_ref[...] + _dot(ad, aup_ref[...], HIGHEST))
    g = _dot(jax.nn.sigmoid(gd), gup_ref[...], HIGHEST)

    bd = _head_block_diag()
    kk = k * kk_ref[...]
    nrm = jnp.maximum(jnp.sqrt(_head_sum(kk * kk, bd)), 1e-12)
    kk = kk / nrm
    k_mod = k * (1.0 + (a - 1.0) * ka_ref[...])
    bonus = _head_sum(r * k_mod * rk_ref[...], bd) * v

    or_ref[...] = r
    olw_ref[...] = log_decay
    ok_ref[...] = k_mod
    ov_ref[...] = v
    oa_ref[...] = -kk
    ob_ref[...] = kk * a
    og_ref[...] = g
    obonus_ref[...] = bonus


def _rwkv_prep(proj, cols, mus, w0, w_up, a0, a_up, g_up, k_k, k_a, r_k, tokens_per_batch):
    t = proj.shape[0]
    w = RWKV_HEADS * RWKV_HEAD_DIM
    tm = min(256, tokens_per_batch)
    tpb = tokens_per_batch // tm
    widths = [w, w, w, 256, 256]
    offs = [cols["rr"], cols["rk"], cols["rv"], cols["gd"], cols["lo"]]
    cur_specs = [pl.BlockSpec((tm, wd), functools.partial(lambda i, cb: (i, cb), cb=o // wd))
                 for wd, o in zip(widths, offs)]
    prv_specs = [pl.BlockSpec((8, wd), functools.partial(
        lambda i, cb: (jnp.maximum(i * (tm // 8) - 1, 0), cb), cb=o // wd))
        for wd, o in zip(widths, offs)]
    full = lambda a: pl.BlockSpec(a.shape, lambda i: (0,) * a.ndim)
    params = list(mus) + [w0, w_up, a0, a_up, g_up, k_k, k_a, r_k]
    out_spec = pl.BlockSpec((tm, w), lambda i: (i, 0))
    return pl.pallas_call(
        functools.partial(_rwkv_prep_kernel, tiles_per_batch=tpb),
        grid=(t // tm,),
        in_specs=cur_specs + prv_specs + [full(p) for p in params],
        out_specs=[out_spec] * 8,
        out_shape=[jax.ShapeDtypeStruct((t, w), F32)] * 8,
        compiler_params=_cparams(("parallel",)),
        name="rwkv_prep",
    )(*([proj] * 10), *params)


def _rwkv_scan_kernel(r_ref, lw_ref, k_ref, v_ref, a_ref, b_ref, o_ref, h_scr):
    c = pl.program_id(1)

    @pl.when(c == 0)
    def _():
        h_scr[...] = jnp.zeros(h_scr.shape, F32)

    C = r_ref.shape[0]
    n_pairs = r_ref.shape[1] // LANES
    P = HIGHEST

    row = lax.broadcasted_iota(jnp.int32, (C, C), 0)
    col = lax.broadcasted_iota(jnp.int32, (C, C), 1)
    tri = (row >= col).astype(F32)
    lw = lw_ref[...]
    cw = _dot(tri, lw, P)
    cw_end = cw[C - 1:C, :]
    e_pos = jnp.exp(cw)
    e_prev = jnp.exp(cw - lw)
    e_neg = jnp.exp(-cw)
    e_end = jnp.exp(cw_end - cw)
    w_end = jnp.exp(cw_end)

    a = a_ref[...]
    b = b_ref[...]
    k = k_ref[...]
    r = r_ref[...]
    at = a * e_prev
    bt = b * e_neg
    kt = k * e_neg
    rt = r * e_pos
    bh = b * e_end
    kh = k * e_end
    v = v_ref[...]

    lane = lax.broadcasted_iota(jnp.int32, (C, LANES), 1)
    head0 = lane < RWKV_HEAD_DIM
    r2 = lax.broadcasted_iota(jnp.int32, (2 * C, 2 * C), 0)
    c2 = lax.broadcasted_iota(jnp.int32, (2 * C, 2 * C), 1)
    same = (r2 // C) == (c2 // C)
    strict = same & (r2 > c2)
    incl = same & (r2 >= c2)
    eye = (r2 == c2).astype(F32)

    def two(x, p):
        xp = x[:, p * LANES:(p + 1) * LANES]
        return jnp.concatenate([jnp.where(head0, xp, 0.0), jnp.where(head0, 0.0, xp)], axis=0)

    def mm(x, y):
        return _dot(x.astype(BF16), y.astype(BF16))

    def split(x):
        hi = x.astype(BF16)
        return hi, (x - hi.astype(F32)).astype(BF16)

    def mm3(x, y):
        xh, xl = split(x)
        yh, yl = split(y)
        return _dot(xh, yh) + (_dot(xh, yl) + _dot(xl, yh))

    G = 2 * C
    pairs = range(n_pairs)
    at2 = [two(at, p) for p in pairs]
    rt2 = [two(rt, p) for p in pairs]
    v2 = [two(v, p).astype(BF16) for p in pairs]
    gram = [_dot_nt(jnp.concatenate([at2[p], rt2[p]], axis=0).astype(BF16),
                    jnp.concatenate([two(bt, p), two(kt, p)], axis=0).astype(BF16)) for p in pairs]
    a_ab = [jnp.where(strict, gram[p][:G, :G], 0.0) for p in pairs]
    nmat = list(a_ab)
    pw = list(a_ab)
    for _ in range(max(1, (C - 1).bit_length() - 1)):
        pw = [mm3(pw[p], pw[p]) for p in pairs]
        nmat = [nmat[p] + pw[p] + mm3(nmat[p], pw[p]) for p in pairs]
    akv = [mm(jnp.where(strict, gram[p][:G, G:], 0.0), v2[p]) for p in pairs]
    rhs = [jnp.concatenate([at2[p], akv[p]], axis=1) for p in pairs]
    pq = [(rhs[p] + mm(nmat[p], rhs[p])).astype(BF16) for p in pairs]
    ry = [mm(jnp.where(incl, gram[p][G:, :G], 0.0), pq[p]) for p in pairs]
    mv = [mm(jnp.where(incl, gram[p][G:, G:], 0.0), v2[p]) for p in pairs]
    ge = [_dot_tn(two(bh, p).astype(BF16), pq[p]) for p in pairs]
    kv = [_dot_tn(two(kh, p).astype(BF16), v2[p]) for p in pairs]
    for p in pairs:
        rr = rt2[p] + ry[p][:, :LANES]
        gm = eye * w_end[:, p * LANES:(p + 1) * LANES] + ge[p][:, :LANES]
        yh = mm(jnp.concatenate([rr, gm], axis=0), h_scr[p])
        y2 = yh[:G] + ry[p][:, LANES:] + mv[p]
        h_scr[p] = yh[G:] + ge[p][:, LANES:] + kv[p]
        o_ref[:, p * LANES:(p + 1) * LANES] = y2[:C, :] + y2[C:, :]


def _rwkv_scan(r, lw, k, v, a, b, batch, seq):
    w = r.shape[1]
    nc = seq // CHUNK
    spec = pl.BlockSpec((CHUNK, w), lambda bi, ci: (bi * nc + ci, 0))
    return pl.pallas_call(
        _rwkv_scan_kernel,
        grid=(batch, nc),
        in_specs=[spec] * 6,
        out_specs=spec,
        out_shape=jax.ShapeDtypeStruct((batch * seq, w), F32),
        scratch_shapes=[pltpu.VMEM((w // LANES, LANES, LANES), F32)],
        compiler_params=_cparams(("parallel", "arbitrary")),
        name="rwkv_scan",
    )(r, lw, k, v, a, b)


def _mix_out_kernel(o_ref, gate_ref, y_ref, bonus_ref, g_ref, x_ref, g1_ref, lnw_ref, lnb_ref, w_ref, out_ref):
    fox = o_ref[...] * jax.nn.sigmoid(gate_ref[...])
    bd = _head_block_diag()
    y = y_ref[...]
    inv_n = 1.0 / RWKV_HEAD_DIM
    mean = _head_sum(y, bd) * inv_n
    d = y - mean
    var = _head_sum(d * d, bd) * inv_n
    yn = d * lax.rsqrt(var + GN_EPS) * lnw_ref[...] + lnb_ref[...]
    rw = (yn + bonus_ref[...]) * g_ref[...]
    wf = fox.shape[1]
    mix = _dot(fox.astype(BF16), w_ref[:wf, :]) + _dot(rw.astype(BF16), w_ref[wf:, :])
    out_ref[...] = x_ref[...] + g1_ref[0] * mix


def _mix_out(o_fox, proj, gate_col, y, bonus, g, x2, g1, ln_w, ln_b, w_out_bf, tokens_per_batch):
    t, d = x2.shape
    wf = o_fox.shape[1]
    wr = y.shape[1]
    tm = min(256, tokens_per_batch)
    tpb = tokens_per_batch // tm
    return pl.pallas_call(
        _mix_out_kernel,
        grid=(t // tm,),
        in_specs=[pl.BlockSpec((tm, wf), lambda i: (i, 0)),
                  pl.BlockSpec((tm, wf), lambda i: (i, gate_col // wf)),
                  pl.BlockSpec((tm, wr), lambda i: (i, 0)),
                  pl.BlockSpec((tm, wr), lambda i: (i, 0)),
                  pl.BlockSpec((tm, wr), lambda i: (i, 0)),
                  pl.BlockSpec((tm, d), lambda i: (i, 0)),
                  pl.BlockSpec((1, 1, d), lambda i: (i // tpb, 0, 0)),
                  pl.BlockSpec((1, wr), lambda i: (0, 0)),
                  pl.BlockSpec((1, wr), lambda i: (0, 0)),
                  pl.BlockSpec(w_out_bf.shape, lambda i: (0, 0))],
        out_specs=pl.BlockSpec((tm, d), lambda i: (i, 0)),
        out_shape=jax.ShapeDtypeStruct((t, d), F32),
        compiler_params=_cparams(("parallel",)),
        name="mix_out",
    )(o_fox, proj, y, bonus, g, x2, g1, ln_w, ln_b, w_out_bf)


def _top_k_mask_rows(s, k):
    n = s.shape[0]
    row = lax.broadcasted_iota(jnp.int32, s.shape, 0).astype(F32)
    rank = jnp.full(s.shape, float(k), F32)
    vals = []
    for r in range(k):
        m = jnp.max(s, axis=0, keepdims=True)
        pos = jnp.min(jnp.where(s == m, row, float(n)), axis=0, keepdims=True)
        hit = row == pos
        vals.append(m)
        rank = jnp.where(hit, float(r), rank)
        s = jnp.where(hit, -jnp.inf, s)
    return jnp.concatenate(vals, axis=0), rank


def _candidate_rows():
    k = PEER_TOPK
    groups, valid = [], []
    for r0 in range(k // 2):
        n1 = k // (r0 + 1)
        for g in range(-(-n1 // 8)):
            groups.append((r0, g * 8))
            valid.append([g * 8 + i < n1 for i in range(8)])
    groups.append((None, k // 2))
    valid.append([True] * 8)
    return groups, valid


def _peer_route_kernel(x_ref, nw_ref, sc_ref, sh_ref, wq_ref, keys_ref,
                       h_ref, pa_ref, len_ref, qb_ref, rnk_ref):
    h = _modulated_norm(x_ref[...], nw_ref[...], sc_ref[0], sh_ref[0]).astype(BF16)
    h_ref[...] = h
    q = _dot(h, wq_ref[...])
    tm = q.shape[0]
    half = keys_ref.shape[-1]
    k = PEER_TOPK
    groups, valid = _candidate_rows()
    sub8 = lax.broadcasted_iota(jnp.int32, (8, tm), 0)
    for hd in range(PEER_HEADS):
        qa = q[:, (hd * 2) * half:(hd * 2 + 1) * half]
        qb = q[:, (hd * 2 + 1) * half:(hd * 2 + 2) * half]
        sa = _dot_nt(keys_ref[hd, 0], qa, HIGHEST)
        sb = _dot_nt(keys_ref[hd, 1], qb, HIGHEST)
        top_a, rank_a = _top_k_mask_rows(sa, k)
        top_b, rank_b = _top_k_mask_rows(sb, k)
        parts = []
        for (r0, off), ok in zip(groups, valid):
            if r0 is None:
                part = top_a[off:off + 8, :] + top_b[0:1, :]
            else:
                part = top_a[r0:r0 + 1, :] + top_b[off:off + 8, :]
            if not all(ok):
                n_ok = sum(ok)
                part = jnp.where(sub8 < n_ok, part, -jnp.inf)
            parts.append(part)
        cand = jnp.concatenate(parts, axis=0)
        _, crank = _top_k_mask_rows(cand, k)
        sel = crank < float(k)
        z = jnp.sum(jnp.where(sel, jnp.exp(cand - cand[0:1, :]), 0.0), axis=0, keepdims=True)
        self32 = jnp.where(sel, 1.0, 0.0)
        length = jnp.zeros_like(sa)
        tail = None
        for gi, (r0, off) in enumerate(groups):
            cnt = self32[gi * 8:(gi + 1) * 8, :]
            if r0 is None:
                tail = cnt
            else:
                length = length + jnp.where(rank_a == float(r0), jnp.sum(cnt, axis=0, keepdims=True), 0.0)
        for r in range(8):
            length = length + jnp.where(rank_a == float(k // 2 + r), tail[r:r + 1, :], 0.0)
        pa_ref[hd] = jnp.exp(sa - top_a[0:1, :])
        len_ref[hd] = length
        qb_ref[hd] = (jnp.exp(sb - top_b[0:1, :]) / z).astype(BF16)
        rnk_ref[hd] = rank_b.astype(BF16)


def _peer_route(x1, norm_w, sc, sh, wq_bf, sub_keys, tokens_per_batch):
    t, d = x1.shape
    tm = min(256, tokens_per_batch)
    tpb = tokens_per_batch // tm
    dense = lambda dt: jax.ShapeDtypeStruct((PEER_HEADS, PEER_NKEYS, t), dt)
    dense_spec = pl.BlockSpec((PEER_HEADS, PEER_NKEYS, tm), lambda i: (0, 0, i))
    return pl.pallas_call(
        _peer_route_kernel,
        grid=(t // tm,),
        in_specs=[pl.BlockSpec((tm, d), lambda i: (i, 0)),
                  pl.BlockSpec((1, d), lambda i: (0, 0)),
                  pl.BlockSpec((1, 1, d), lambda i: (i // tpb, 0, 0)),
                  pl.BlockSpec((1, 1, d), lambda i: (i // tpb, 0, 0)),
                  pl.BlockSpec(wq_bf.shape, lambda i: (0, 0)),
                  pl.BlockSpec(sub_keys.shape, lambda i: (0, 0, 0, 0))],
        out_specs=[pl.BlockSpec((tm, d), lambda i: (i, 0))] + [dense_spec] * 4,
        out_shape=[jax.ShapeDtypeStruct((t, d), BF16), dense(F32), dense(F32), dense(BF16), dense(BF16)],
        compiler_params=_cparams(("parallel",)),
        name="peer_route",
    )(x1, norm_w, sc, sh, wq_bf, sub_keys)


def _gelu_exact(x):
    return 0.5 * x * (1.0 + lax.erf(x * 0.7071067811865476))


def _peer_expert_kernel(h_ref, u_ref, v_ref, pa_ref, len_ref, qb_ref, rnk_ref, x_ref, g2_ref, o_ref):
    j = pl.program_id(1)
    be = u_ref.shape[0]
    tm = h_ref.shape[0]
    n_slabs = be // PEER_NKEYS
    rows = 16
    n_groups = PEER_NKEYS // rows
    act = _gelu_exact(_dot_nt(u_ref[...], h_ref[...])).astype(BF16)

    def row_tile(ref, hd, s):
        row = ref[hd, pl.ds(j * n_slabs + s, 1), :]
        return jnp.broadcast_to(row, (rows, tm)).astype(BF16)

    ln = [[row_tile(len_ref, hd, s) for s in range(n_slabs)] for hd in range(PEER_HEADS)]
    pa = [[row_tile(pa_ref, hd, s) for s in range(n_slabs)] for hd in range(PEER_HEADS)]
    zero = jnp.zeros((rows, tm), BF16)
    parts = [[None] * n_groups for _ in range(n_slabs)]
    for g in range(n_groups):
        sl = slice(g * rows, (g + 1) * rows)
        acc = [None] * n_slabs
        for hd in range(PEER_HEADS):
            rk = rnk_ref[hd, sl, :]
            qv = qb_ref[hd, sl, :]
            for s in range(n_slabs):
                term = jnp.where(rk < ln[hd][s], qv, zero) * pa[hd][s]
                acc[s] = term if acc[s] is None else acc[s] + term
        for s in range(n_slabs):
            lo = s * PEER_NKEYS + g * rows
            parts[s][g] = acc[s] * act[lo:lo + rows, :]
    p = jnp.concatenate([parts[s][g] for s in range(n_slabs) for g in range(n_groups)], axis=0)
    contrib = _dot_tn(p, v_ref[...])

    @pl.when(j == 0)
    def _():
        o_ref[...] = contrib

    @pl.when(j > 0)
    def _():
        o_ref[...] += contrib

    @pl.when(j == pl.num_programs(1) - 1)
    def _():
        o_ref[...] = x_ref[...] + g2_ref[0] * o_ref[...]


def _peer_expert(h_bf, u_bf, v_bf, pa, ln, qb, rnk, x1, g2, tokens_per_batch):
    t, d = x1.shape
    ne = u_bf.shape[0]
    tm = min(512, tokens_per_batch)
    tpb = tokens_per_batch // tm
    be = 512
    dense_spec = pl.BlockSpec((PEER_HEADS, PEER_NKEYS, tm), lambda i, j: (0, 0, i))
    return pl.pallas_call(
        _peer_expert_kernel,
        grid=(t // tm, ne // be),
        in_specs=[pl.BlockSpec((tm, d), lambda i, j: (i, 0)),
                  pl.BlockSpec((be, d), lambda i, j: (j, 0)),
                  pl.BlockSpec((be, d), lambda i, j: (j, 0)),
                  dense_spec, dense_spec, dense_spec, dense_spec,
                  pl.BlockSpec((tm, d), lambda i, j: (i, 0)),
                  pl.BlockSpec((1, 1, d), lambda i, j: (i // tpb, 0, 0))],
        out_specs=pl.BlockSpec((tm, d), lambda i, j: (i, 0)),
        out_shape=jax.ShapeDtypeStruct((t, d), F32),
        compiler_params=_cparams(("parallel", "arbitrary")),
        name="peer_expert",
    )(h_bf, u_bf, v_bf, pa, ln, qb, rnk, x1, g2)


def _pad_cols(a, n):
    return jnp.pad(a, ((0, 0), (0, n - a.shape[1])))


def _pad_rows(a, n):
    return jnp.pad(a, ((0, n - a.shape[0]), (0, 0)))


def _layer(x, c, w_ada, b_ada, norm_mix_w, w_in, fox_q_norm_w, fox_k_norm_w, fox_f_bias,
           rwkv_mu, rwkv_w0, rwkv_w_up, rwkv_a0, rwkv_a_up, rwkv_g_up, rwkv_k_k, rwkv_k_a,
           rwkv_r_k, rwkv_ln_w, rwkv_ln_b, w_out, norm_ffn_w, peer_w_query, peer_sub_keys,
           peer_u, peer_v):
    B, S, D = x.shape
    T = B * S
    fw = FOX_HEADS * FOX_HEAD_DIM
    rw = RWKV_HEADS * RWKV_HEAD_DIM
    w_lora = rwkv_w_up.shape[0]
    a_lora = rwkv_a_up.shape[0]
    g_lora = rwkv_g_up.shape[0]
    assert w_lora <= LANES and a_lora + FOX_HEADS <= LANES and g_lora == 256

    c_pad = _pad_rows(c, 8)
    mod = _ada_mod(c_pad, w_ada, b_ada)[:B]
    sh1, sc1, g1, sh2, sc2, g2 = [m.reshape(B, 1, D) for m in jnp.split(mod, 6, axis=-1)]

    fox_cols = 4 * fw + FOX_HEADS
    wi_fox, wi_rw = w_in[:, :fox_cols], w_in[:, fox_cols:]
    mu = rwkv_mu.reshape(1, -1)
    seg = lambda a, lo, n: a[:, lo:lo + n]
    w_perm = jnp.concatenate([
        seg(wi_fox, 0, 4 * fw),
        seg(wi_rw, 0, 3 * rw),
        seg(wi_rw, 3 * rw + w_lora + a_lora, g_lora),
        _pad_cols(seg(wi_rw, 3 * rw, w_lora), LANES),
        _pad_cols(jnp.concatenate([seg(wi_rw, 3 * rw + w_lora, a_lora), seg(wi_fox, 4 * fw, FOX_HEADS)], 1), LANES),
    ], axis=1).astype(BF16)
    cols = {"gate": 3 * fw, "rr": 4 * fw, "rk": 4 * fw + rw, "rv": 4 * fw + 2 * rw,
            "gd": 4 * fw + 3 * rw, "lo": 4 * fw + 3 * rw + g_lora}
    f_lane = a_lora
    f_block = (cols["lo"] + LANES) // LANES
    mus = [seg(mu, 0, rw), seg(mu, rw, rw), seg(mu, 2 * rw, rw),
           seg(mu, 3 * rw + w_lora + a_lora, g_lora),
           jnp.concatenate([_pad_cols(seg(mu, 3 * rw, w_lora), LANES),
                            _pad_cols(seg(mu, 3 * rw + w_lora, a_lora), LANES)], 1)]
    scale = FOX_HEAD_DIM ** -0.5 * LOG2_E
    head_w = _pad_cols(jnp.concatenate([jnp.tile(fox_q_norm_w * scale, FOX_HEADS),
                                        jnp.tile(fox_k_norm_w, FOX_HEADS)]).reshape(1, -1), w_perm.shape[1])

    x2 = x.reshape(T, D)
    proj = _in_proj(x2, norm_mix_w.reshape(1, D), sc1, sh1, w_perm, head_w, S, 2 * fw)

    f_bias_row = jnp.zeros((1, LANES), F32).at[0, f_lane:f_lane + FOX_HEADS].set(fox_f_bias)
    cum = _fox_cum(proj, f_bias_row, B, S, f_block)
    cum_rows = cum.reshape(B, S, LANES)[:, :, f_lane:f_lane + FOX_HEADS].transpose(0, 2, 1).reshape(B, FOX_HEADS, 1, S)
    o_fox = _fox_attn(proj, cum_rows, B, S)

    row = lambda a: a.reshape(1, -1)
    r, lw, k, v, a_vec, b_vec, g, bonus = _rwkv_prep(
        proj, cols, mus, row(rwkv_w0), _pad_rows(rwkv_w_up, LANES), row(rwkv_a0), _pad_rows(rwkv_a_up, LANES),
        rwkv_g_up, row(rwkv_k_k), row(rwkv_k_a), row(rwkv_r_k), S)
    y = _rwkv_scan(r, lw, k, v, a_vec, b_vec, B, S)

    x1 = _mix_out(o_fox, proj, cols["gate"], y, bonus, g, x2, g1, row(rwkv_ln_w), row(rwkv_ln_b),
                  w_out.astype(BF16), S)

    h2, pa, ln, qb, rnk = _peer_route(x1, norm_ffn_w.reshape(1, D), sc2, sh2,
                                      peer_w_query.astype(BF16), peer_sub_keys, S)
    out = _peer_expert(h2, peer_u.astype(BF16), peer_v.astype(BF16), pa, ln, qb, rnk, x1, g2, S)
    return out.reshape(B, S, D)


def kernel(x, c, w_ada, b_ada, norm_mix_w, w_in, fox_q_norm_w, fox_k_norm_w, fox_f_bias, rwkv_mu, rwkv_w0,
           rwkv_w_up, rwkv_a0, rwkv_a_up, rwkv_g_up, rwkv_k_k, rwkv_k_a, rwkv_r_k, rwkv_ln_w, rwkv_ln_b,
           w_out, norm_ffn_w, peer_w_query, peer_sub_keys, peer_u, peer_v):
    params = (w_ada, b_ada, norm_mix_w, w_in, fox_q_norm_w, fox_k_norm_w, fox_f_bias, rwkv_mu, rwkv_w0,
              rwkv_w_up, rwkv_a0, rwkv_a_up, rwkv_g_up, rwkv_k_k, rwkv_k_a, rwkv_r_k, rwkv_ln_w, rwkv_ln_b,
              w_out, norm_ffn_w, peer_w_query, peer_sub_keys, peer_u, peer_v)
    for l in range(w_ada.shape[0]):
        x = _layer(x, c, *[p[l] for p in params])
    return x
```

```python
import functools

import jax
import jax.numpy as jnp
from jax import lax
from jax.experimental import pallas as pl
from jax.experimental.pallas import tpu as pltpu

F32 = jnp.float32
BF16 = jnp.bfloat16
HIGHEST = lax.Precision.HIGHEST

LANES = 128
NORM_EPS = 1e-6
GN_EPS = 64e-5
CHUNK = 64
FOX_HEADS = 8
FOX_HEAD_DIM = 128
RWKV_HEADS = 16
RWKV_HEAD_DIM = 64
PEER_HEADS = 8
PEER_NKEYS = 128
PEER_TOPK = 16
NEG_BIG = -1e30
LOG2_E = 1.4426950408889634
VMEM_LIMIT = 56 * 1024 * 1024


def _cparams(sem):
    return pltpu.CompilerParams(dimension_semantics=sem, vmem_limit_bytes=VMEM_LIMIT)


def _dot(a, b, precision=None):
    return jnp.dot(a, b, preferred_element_type=F32, precision=precision)


def _dot_nt(a, b, precision=None):
    return lax.dot_general(a, b, (((1,), (1,)), ((), ())), preferred_element_type=F32,
                           precision=precision)


def _dot_tn(a, b, precision=None):
    return lax.dot_general(a, b, (((0,), (0,)), ((), ())), preferred_element_type=F32,
                           precision=precision)


def _ada_kernel(c_ref, w_ref, b_ref, o_ref):
    c = c_ref[...]
    s = c * jax.nn.sigmoid(c)
    o_ref[...] = _dot(s, w_ref[...], HIGHEST) + b_ref[...]


def _ada_mod(c_pad, w_ada, b_ada):
    rows, d = c_pad.shape
    n = w_ada.shape[1]
    bn = 1024
    return pl.pallas_call(
        _ada_kernel,
        grid=(n // bn,),
        in_specs=[pl.BlockSpec((rows, d), lambda j: (0, 0)),
                  pl.BlockSpec((d, bn), lambda j: (0, j)),
                  pl.BlockSpec((1, bn), lambda j: (0, j))],
        out_specs=pl.BlockSpec((rows, bn), lambda j: (0, j)),
        out_shape=jax.ShapeDtypeStruct((rows, n), F32),
        compiler_params=_cparams(("arbitrary",)),
        name="ada_mod",
    )(c_pad, w_ada, b_ada.reshape(1, n))


def _modulated_norm(x, nw, sc, sh):
    y = x * lax.rsqrt(jnp.mean(x * x, axis=-1, keepdims=True) + NORM_EPS)
    return y * nw * (1.0 + sc) + sh


def _in_proj_kernel(x_ref, nw_ref, sc_ref, sh_ref, w_ref, hw_ref, o_ref, h_scr, *, n_qk_blocks):
    j = pl.program_id(1)

    @pl.when(j == 0)
    def _():
        h = _modulated_norm(x_ref[...], nw_ref[...], sc_ref[0], sh_ref[0])
        h_scr[...] = h.astype(BF16)

    acc = _dot(h_scr[...], w_ref[...])

    @pl.when(j < n_qk_blocks)
    def _():
        for hh in range(acc.shape[1] // FOX_HEAD_DIM):
            sl = slice(hh * FOX_HEAD_DIM, (hh + 1) * FOX_HEAD_DIM)
            a = acc[:, sl]
            rs = lax.rsqrt(jnp.mean(a * a, axis=-1, keepdims=True) + NORM_EPS)
            o_ref[:, sl] = a * rs * hw_ref[:, sl]

    @pl.when(j >= n_qk_blocks)
    def _():
        o_ref[...] = acc


def _in_proj(x2, norm_w, sc, sh, w_bf, head_w, tokens_per_batch, n_qk_cols):
    t, d = x2.shape
    n = w_bf.shape[1]
    tm = min(1024, tokens_per_batch)
    bn = 512
    tpb = tokens_per_batch // tm
    kern = functools.partial(_in_proj_kernel, n_qk_blocks=n_qk_cols // bn)
    return pl.pallas_call(
        kern,
        grid=(t // tm, n // bn),
        in_specs=[pl.BlockSpec((tm, d), lambda i, j: (i, 0)),
                  pl.BlockSpec((1, d), lambda i, j: (0, 0)),
                  pl.BlockSpec((1, 1, d), lambda i, j: (i // tpb, 0, 0)),
                  pl.BlockSpec((1, 1, d), lambda i, j: (i // tpb, 0, 0)),
                  pl.BlockSpec((d, bn), lambda i, j: (0, j)),
                  pl.BlockSpec((1, bn), lambda i, j: (0, j))],
        out_specs=pl.BlockSpec((tm, bn), lambda i, j: (i, j)),
        out_shape=jax.ShapeDtypeStruct((t, n), F32),
        scratch_shapes=[pltpu.VMEM((tm, d), BF16)],
        compiler_params=_cparams(("parallel", "arbitrary")),
        name="in_proj",
    )(x2, norm_w, sc, sh, w_bf, head_w)


def _log_sigmoid(x):
    return jnp.minimum(x, 0.0) - jnp.log(1.0 + jnp.exp(-jnp.abs(x)))


def _fox_cum_kernel(f_ref, b_ref, o_ref, *, blk):
    s = f_ref.shape[0]
    row = lax.broadcasted_iota(jnp.int32, (blk, blk), 0)
    col = lax.broadcasted_iota(jnp.int32, (blk, blk), 1)
    tri = (row >= col).astype(F32)
    carry = jnp.zeros((1, f_ref.shape[1]), F32)
    for i in range(s // blk):
        lf = _log_sigmoid(f_ref[i * blk:(i + 1) * blk, :] + b_ref[...])
        cs = _dot(tri, lf, HIGHEST) + carry
        o_ref[i * blk:(i + 1) * blk, :] = cs * LOG2_E
        carry = cs[blk - 1:blk, :]


def _fox_cum(proj, f_bias_row, batch, seq, col_block):
    blk = min(256, seq)
    return pl.pallas_call(
        functools.partial(_fox_cum_kernel, blk=blk),
        grid=(batch,),
        in_specs=[pl.BlockSpec((seq, LANES), lambda b: (b, col_block)),
                  pl.BlockSpec((1, LANES), lambda b: (0, 0))],
        out_specs=pl.BlockSpec((seq, LANES), lambda b: (b, 0)),
        out_shape=jax.ShapeDtypeStruct((batch * seq, LANES), F32),
        compiler_params=_cparams(("parallel",)),
        name="fox_cum",
    )(proj, f_bias_row)


def _fox_attn_kernel(qt_ref, kt_ref, q_ref, k_ref, v_ref, c_ref, o_ref, m_scr, l_scr, acc_scr):
    t = pl.program_id(2)
    qi = qt_ref[t]
    ki = kt_ref[t]
    hd = FOX_HEAD_DIM
    n_heads = q_ref.shape[1] // hd

    @pl.when(ki == 0)
    def _():
        m_scr[...] = jnp.full(m_scr.shape, NEG_BIG, F32)
        l_scr[...] = jnp.zeros(l_scr.shape, F32)
        acc_scr[...] = jnp.zeros(acc_scr.shape, F32)

    def step(masked):
        hs = range(n_heads)
        sl = [slice(h * hd, (h + 1) * hd) for h in hs]
        s = [_dot_nt(q_ref[:, sl[h]].astype(BF16), k_ref[:, sl[h]].astype(BF16)) - c_ref[0, h] for h in hs]
        if masked:
            row = lax.broadcasted_iota(jnp.int32, s[0].shape, 0)
            col = lax.broadcasted_iota(jnp.int32, s[0].shape, 1)
            s = [jnp.where(row >= col, x, NEG_BIG) for x in s]
        m_prev = [m_scr[h] for h in hs]
        m_new = [jnp.maximum(m_prev[h], jnp.max(s[h], axis=-1, keepdims=True)) for h in hs]
        p = [jnp.exp2(s[h] - m_new[h]) for h in hs]
        alpha = [jnp.exp2(m_prev[h] - m_new[h]) for h in hs]
        p_sum = [jnp.sum(p[h], axis=-1, keepdims=True) for h in hs]
        pv = [_dot(p[h].astype(BF16), v_ref[:, sl[h]].astype(BF16)) for h in hs]
        for h in hs:
            l_scr[h] = alpha[h] * l_scr[h] + p_sum[h]
            acc_scr[:, sl[h]] = alpha[h] * acc_scr[:, sl[h]] + pv[h]
            m_scr[h] = m_new[h]

    @pl.when(ki < qi)
    def _():
        step(False)

    @pl.when(ki == qi)
    def _():
        step(True)
        for h in range(n_heads):
            sl = slice(h * hd, (h + 1) * hd)
            o_ref[:, sl] = acc_scr[:, sl] / l_scr[h]


def _fox_attn(proj, cum_rows, batch, seq):
    tq = min(512, seq)
    nq = seq // tq
    hps = 4
    w = hps * FOX_HEAD_DIM
    kcol = FOX_HEADS // hps
    tri = [(q, k) for q in range(nq) for k in range(q + 1)]
    qt = jnp.asarray([q for q, _ in tri], jnp.int32)
    kt = jnp.asarray([k for _, k in tri], jnp.int32)
    grid_spec = pltpu.PrefetchScalarGridSpec(
        num_scalar_prefetch=2,
        grid=(batch, kcol, len(tri)),
        in_specs=[
            pl.BlockSpec((tq, w), lambda b, h, t, qt, kt: (b * nq + qt[t], h)),
            pl.BlockSpec((tq, w), lambda b, h, t, qt, kt: (b * nq + kt[t], kcol + h)),
            pl.BlockSpec((tq, w), lambda b, h, t, qt, kt: (b * nq + kt[t], 2 * kcol + h)),
            pl.BlockSpec((1, hps, 1, tq), lambda b, h, t, qt, kt: (b, h, 0, kt[t])),
        ],
        out_specs=pl.BlockSpec((tq, w), lambda b, h, t, qt, kt: (b * nq + qt[t], h)),
        scratch_shapes=[pltpu.VMEM((hps, tq, 1), F32), pltpu.VMEM((hps, tq, 1), F32),
                        pltpu.VMEM((tq, w), F32)],
    )
    return pl.pallas_call(
        _fox_attn_kernel,
        grid_spec=grid_spec,
        out_shape=jax.ShapeDtypeStruct((batch * seq, FOX_HEADS * FOX_HEAD_DIM), F32),
        compiler_params=_cparams(("parallel", "parallel", "arbitrary")),
        name="fox_attn",
    )(qt, kt, proj, proj, proj, cum_rows)


def _head_sum(x, bd):
    parts = [_dot(x[:, j * LANES:(j + 1) * LANES], bd, HIGHEST) for j in range(x.shape[1] // LANES)]
    return jnp.concatenate(parts, axis=1)


def _head_block_diag():
    r = lax.broadcasted_iota(jnp.int32, (LANES, LANES), 0) // RWKV_HEAD_DIM
    c = lax.broadcasted_iota(jnp.int32, (LANES, LANES), 1) // RWKV_HEAD_DIM
    return (r == c).astype(F32)


def _rwkv_prep_kernel(r_ref, k_ref, v_ref, gd_ref, lo_ref, pr_ref, pk_ref, pv_ref, pgd_ref, plo_ref,
                      mu_r, mu_k, mu_v, mu_gd, mu_lo, w0_ref, wup_ref, a0_ref, aup_ref, gup_ref,
                      kk_ref, ka_ref, rk_ref,
                      or_ref, olw_ref, ok_ref, ov_ref, oa_ref, ob_ref, og_ref, obonus_ref, *, tiles_per_batch):
    i = pl.program_id(0)
    first = (i % tiles_per_batch) == 0

    def shifted(cur_ref, prv_ref, mu_ref):
        cur = cur_ref[...]
        last = jnp.where(first, 0.0, prv_ref[7:8, :])
        row = lax.broadcasted_iota(jnp.int32, cur.shape, 0)
        prev = jnp.where(row == 0, last, pltpu.roll(cur, 1, 0))
        return cur + (prev - cur) * mu_ref[...]

    r = shifted(r_ref, pr_ref, mu_r)
    k = shifted(k_ref, pk_ref, mu_k)
    v = shifted(v_ref, pv_ref, mu_v)
    gd = shifted(gd_ref, pgd_ref, mu_gd)
    lo = shifted(lo_ref, plo_ref, mu_lo)
    wd = lo[:, :LANES]
    ad = lo[:, LANES:]

    w_pre = w0_ref[...] + _dot(jnp.tanh(wd), wup_ref[...], HIGHEST)
    w_raw = _log_sigmoid(w_pre) - 0.5
    log_decay = -jnp.exp(w_raw)
    a = jax.nn.sigmoid(a0_ref[...] + _dot(ad, aup_ref[...], HIGHEST))
    g = _dot(jax.nn.sigmoid(gd), gup_ref[...], HIGHEST)

    bd = _head_block_diag()
    kk = k * kk_ref[...]
    nrm = jnp.maximum(jnp.sqrt(_head_sum(kk * kk, bd)), 1e-12)
    kk = kk / nrm
    k_mod = k * (1.0 + (a - 1.0) * ka_ref[...])
    bonus = _head_sum(r * k_mod * rk_ref[...], bd) * v

    or_ref[...] = r
    olw_ref[...] = log_decay
    ok_ref[...] = k_mod
    ov_ref[...] = v
    oa_ref[...] = -kk
    ob_ref[...] = kk * a
    og_ref[...] = g
    obonus_ref[...] = bonus


def _rwkv_prep(proj, cols, mus, w0, w_up, a0, a_up, g_up, k_k, k_a, r_k, tokens_per_batch):
    t = proj.shape[0]
    w = RWKV_HEADS * RWKV_HEAD_DIM
    tm = min(256, tokens_per_batch)
    tpb = tokens_per_batch // tm
    widths = [w, w, w, 256, 256]
    offs = [cols["rr"], cols["rk"], cols["rv"], cols["gd"], cols["lo"]]
    cur_specs = [pl.BlockSpec((tm, wd), functools.partial(lambda i, cb: (i, cb), cb=o // wd))
                 for wd, o in zip(widths, offs)]
    prv_specs = [pl.BlockSpec((8, wd), functools.partial(
        lambda i, cb: (jnp.maximum(i * (tm // 8) - 1, 0), cb), cb=o // wd))
        for wd, o in zip(widths, offs)]
    full = lambda a: pl.BlockSpec(a.shape, lambda i: (0,) * a.ndim)
    params = list(mus) + [w0, w_up, a0, a_up, g_up, k_k, k_a, r_k]
    out_spec = pl.BlockSpec((tm, w), lambda i: (i, 0))
    return pl.pallas_call(
        functools.partial(_rwkv_prep_kernel, tiles_per_batch=tpb),
        grid=(t // tm,),
        in_specs=cur_specs + prv_specs + [full(p) for p in params],
        out_specs=[out_spec] * 8,
        out_shape=[jax.ShapeDtypeStruct((t, w), F32)] * 8,
        compiler_params=_cparams(("parallel",)),
        name="rwkv_prep",
    )(*([proj] * 10), *params)


def _rwkv_scan_kernel(r_ref, lw_ref, k_ref, v_ref, a_ref, b_ref, o_ref, h_scr):
    c = pl.program_id(1)

    @pl.when(c == 0)
    def _():
        h_scr[...] = jnp.zeros(h_scr.shape, F32)

    C = r_ref.shape[0]
    n_pairs = r_ref.shape[1] // LANES
    P = HIGHEST

    row = lax.broadcasted_iota(jnp.int32, (C, C), 0)
    col = lax.broadcasted_iota(jnp.int32, (C, C), 1)
    tri = (row >= col).astype(F32)
    lw = lw_ref[...]
    cw = _dot(tri, lw, P)
    cw_end = cw[C - 1:C, :]
    e_pos = jnp.exp(cw)
    e_prev = jnp.exp(cw - lw)
    e_neg = jnp.exp(-cw)
    e_end = jnp.exp(cw_end - cw)
    w_end = jnp.exp(cw_end)

    a = a_ref[...]
    b = b_ref[...]
    k = k_ref[...]
    r = r_ref[...]
    at = a * e_prev
    bt = b * e_neg
    kt = k * e_neg
    rt = r * e_pos
    bh = b * e_end
    kh = k * e_end
    v = v_ref[...]

    lane = lax.broadcasted_iota(jnp.int32, (C, LANES), 1)
    head0 = lane < RWKV_HEAD_DIM
    r2 = lax.broadcasted_iota(jnp.int32, (2 * C, 2 * C), 0)
    c2 = lax.broadcasted_iota(jnp.int32, (2 * C, 2 * C), 1)
    same = (r2 // C) == (c2 // C)
    strict = same & (r2 > c2)
    incl = same & (r2 >= c2)
    eye = (r2 == c2).astype(F32)

    def two(x, p):
        xp = x[:, p * LANES:(p + 1) * LANES]
        return jnp.concatenate([jnp.where(head0, xp, 0.0), jnp.where(head0, 0.0, xp)], axis=0)

    def mm(x, y):
        return _dot(x.astype(BF16), y.astype(BF16))

    def split(x):
        hi = x.astype(BF16)
        return hi, (x - hi.astype(F32)).astype(BF16)

    def mm3(x, y):
        xh, xl = split(x)
        yh, yl = split(y)
        return _dot(xh, yh) + (_dot(xh, yl) + _dot(xl, yh))

    G = 2 * C
    pairs = range(n_pairs)
    at2 = [two(at, p) for p in pairs]
    rt2 = [two(rt, p) for p in pairs]
    v2 = [two(v, p).astype(BF16) for p in pairs]
    gram = [_dot_nt(jnp.concatenate([at2[p], rt2[p]], axis=0).astype(BF16),
                    jnp.concatenate([two(bt, p), two(kt, p)], axis=0).astype(BF16)) for p in pairs]
    a_ab = [jnp.where(strict, gram[p][:G, :G], 0.0) for p in pairs]
    nmat = list(a_ab)
    pw = list(a_ab)
    for _ in range(max(1, (C - 1).bit_length() - 1)):
        pw = [mm3(pw[p], pw[p]) for p in pairs]
        nmat = [nmat[p] + pw[p] + mm3(nmat[p], pw[p]) for p in pairs]
    akv = [mm(jnp.where(strict, gram[p][:G, G:], 0.0), v2[p]) for p in pairs]
    rhs = [jnp.concatenate([at2[p], akv[p]], axis=1) for p in pairs]
    pq = [(rhs[p] + mm(nmat[p], rhs[p])).astype(BF16) for p in pairs]
    ry = [mm(jnp.where(incl, gram[p][G:, :G], 0.0), pq[p]) for p in pairs]
    mv = [mm(jnp.where(incl, gram[p][G:, G:], 0.0), v2[p]) for p in pairs]
    ge = [_dot_tn(two(bh, p).astype(BF16), pq[p]) for p in pairs]
    kv = [_dot_tn(two(kh, p).astype(BF16), v2[p]) for p in pairs]
    for p in pairs:
        rr = rt2[p] + ry[p][:, :LANES]
        gm = eye * w_end[:, p * LANES:(p + 1) * LANES] + ge[p][:, :LANES]
        yh = mm(jnp.concatenate([rr, gm], axis=0), h_scr[p])
        y2 = yh[:G] + ry[p][:, LANES:] + mv[p]
        h_scr[p] = yh[G:] + ge[p][:, LANES:] + kv[p]
        o_ref[:, p * LANES:(p + 1) * LANES] = y2[:C, :] + y2[C:, :]


def _rwkv_scan(r, lw, k, v, a, b, batch, seq):
    w = r.shape[1]
    nc = seq // CHUNK
    spec = pl.BlockSpec((CHUNK, w), lambda bi, ci: (bi * nc + ci, 0))
    return pl.pallas_call(
        _rwkv_scan_kernel,
        grid=(batch, nc),
        in_specs=[spec] * 6,
        out_specs=spec,
        out_shape=jax.ShapeDtypeStruct((batch * seq, w), F32),
        scratch_shapes=[pltpu.VMEM((w // LANES, LANES, LANES), F32)],
        compiler_params=_cparams(("parallel", "arbitrary")),
        name="rwkv_scan",
    )(r, lw, k, v, a, b)


def _mix_out_kernel(o_ref, gate_ref, y_ref, bonus_ref, g_ref, x_ref, g1_ref, lnw_ref, lnb_ref, w_ref, out_ref):
    fox = o_ref[...] * jax.nn.sigmoid(gate_ref[...])
    bd = _head_block_diag()
    y = y_ref[...]
    inv_n = 1.0 / RWKV_HEAD_DIM
    mean = _head_sum(y, bd) * inv_n
    d = y - mean
    var = _head_sum(d * d, bd) * inv_n
    yn = d * lax.rsqrt(var + GN_EPS) * lnw_ref[...] + lnb_ref[...]
    rw = (yn + bonus_ref[...]) * g_ref[...]
    wf = fox.shape[1]
    mix = _dot(fox.astype(BF16), w_ref[:wf, :]) + _dot(rw.astype(BF16), w_ref[wf:, :])
    out_ref[...] = x_ref[...] + g1_ref[0] * mix


def _mix_out(o_fox, proj, gate_col, y, bonus, g, x2, g1, ln_w, ln_b, w_out_bf, tokens_per_batch):
    t, d = x2.shape
    wf = o_fox.shape[1]
    wr = y.shape[1]
    tm = min(256, tokens_per_batch)
    tpb = tokens_per_batch // tm
    return pl.pallas_call(
        _mix_out_kernel,
        grid=(t // tm,),
        in_specs=[pl.BlockSpec((tm, wf), lambda i: (i, 0)),
                  pl.BlockSpec((tm, wf), lambda i: (i, gate_col // wf)),
                  pl.BlockSpec((tm, wr), lambda i: (i, 0)),
                  pl.BlockSpec((tm, wr), lambda i: (i, 0)),
                  pl.BlockSpec((tm, wr), lambda i: (i, 0)),
                  pl.BlockSpec((tm, d), lambda i: (i, 0)),
                  pl.BlockSpec((1, 1, d), lambda i: (i // tpb, 0, 0)),
                  pl.BlockSpec((1, wr), lambda i: (0, 0)),
                  pl.BlockSpec((1, wr), lambda i: (0, 0)),
                  pl.BlockSpec(w_out_bf.shape, lambda i: (0, 0))],
        out_specs=pl.BlockSpec((tm, d), lambda i: (i, 0)),
        out_shape=jax.ShapeDtypeStruct((t, d), F32),
        compiler_params=_cparams(("parallel",)),
        name="mix_out",
    )(o_fox, proj, y, bonus, g, x2, g1, ln_w, ln_b, w_out_bf)


def _top_k_mask_rows(s, k):
    n = s.shape[0]
    row = lax.broadcasted_iota(jnp.int32, s.shape, 0).astype(F32)
    rank = jnp.full(s.shape, float(k), F32)
    vals = []
    for r in range(k):
        m = jnp.max(s, axis=0, keepdims=True)
        pos = jnp.min(jnp.where(s == m, row, float(n)), axis=0, keepdims=True)
        hit = row == pos
        vals.append(m)
        rank = jnp.where(hit, float(r), rank)
        s = jnp.where(hit, -jnp.inf, s)
    return jnp.concatenate(vals, axis=0), rank


def _candidate_rows():
    k = PEER_TOPK
    groups, valid = [], []
    for r0 in range(k // 2):
        n1 = k // (r0 + 1)
        for g in range(-(-n1 // 8)):
            groups.append((r0, g * 8))
            valid.append([g * 8 + i < n1 for i in range(8)])
    groups.append((None, k // 2))
    valid.append([True] * 8)
    return groups, valid


def _peer_route_kernel(x_ref, nw_ref, sc_ref, sh_ref, wq_ref, keys_ref,
                       h_ref, pa_ref, len_ref, qb_ref, rnk_ref):
    h = _modulated_norm(x_ref[...], nw_ref[...], sc_ref[0], sh_ref[0]).astype(BF16)
    h_ref[...] = h
    q = _dot(h, wq_ref[...])
    tm = q.shape[0]
    half = keys_ref.shape[-1]
    k = PEER_TOPK
    groups, valid = _candidate_rows()
    sub8 = lax.broadcasted_iota(jnp.int32, (8, tm), 0)
    for hd in range(PEER_HEADS):
        qa = q[:, (hd * 2) * half:(hd * 2 + 1) * half]
        qb = q[:, (hd * 2 + 1) * half:(hd * 2 + 2) * half]
        sa = _dot_nt(keys_ref[hd, 0], qa, HIGHEST)
        sb = _dot_nt(keys_ref[hd, 1], qb, HIGHEST)
        top_a, rank_a = _top_k_mask_rows(sa, k)
        top_b, rank_b = _top_k_mask_rows(sb, k)
        parts = []
        for (r0, off), ok in zip(groups, valid):
            if r0 is None:
                part = top_a[off:off + 8, :] + top_b[0:1, :]
            else:
                part = top_a[r0:r0 + 1, :] + top_b[off:off + 8, :]
            if not all(ok):
                n_ok = sum(ok)
                part = jnp.where(sub8 < n_ok, part, -jnp.inf)
            parts.append(part)
        cand = jnp.concatenate(parts, axis=0)
        _, crank = _top_k_mask_rows(cand, k)
        sel = crank < float(k)
        z = jnp.sum(jnp.where(sel, jnp.exp(cand - cand[0:1, :]), 0.0), axis=0, keepdims=True)
        self32 = jnp.where(sel, 1.0, 0.0)
        length = jnp.zeros_like(sa)
        tail = None
        for gi, (r0, off) in enumerate(groups):
            cnt = self32[gi * 8:(gi + 1) * 8, :]
            if r0 is None:
                tail = cnt
            else:
                length = length + jnp.where(rank_a == float(r0), jnp.sum(cnt, axis=0, keepdims=True), 0.0)
        for r in range(8):
            length = length + jnp.where(rank_a == float(k // 2 + r), tail[r:r + 1, :], 0.0)
        pa_ref[hd] = jnp.exp(sa - top_a[0:1, :])
        len_ref[hd] = length
        qb_ref[hd] = (jnp.exp(sb - top_b[0:1, :]) / z).astype(BF16)
        rnk_ref[hd] = rank_b.astype(BF16)


def _peer_route(x1, norm_w, sc, sh, wq_bf, sub_keys, tokens_per_batch):
    t, d = x1.shape
    tm = min(256, tokens_per_batch)
    tpb = tokens_per_batch // tm
    dense = lambda dt: jax.ShapeDtypeStruct((PEER_HEADS, PEER_NKEYS, t), dt)
    dense_spec = pl.BlockSpec((PEER_HEADS, PEER_NKEYS, tm), lambda i: (0, 0, i))
    return pl.pallas_call(
        _peer_route_kernel,
        grid=(t // tm,),
        in_specs=[pl.BlockSpec((tm, d), lambda i: (i, 0)),
                  pl.BlockSpec((1, d), lambda i: (0, 0)),
                  pl.BlockSpec((1, 1, d), lambda i: (i // tpb, 0, 0)),
                  pl.BlockSpec((1, 1, d), lambda i: (i // tpb, 0, 0)),
                  pl.BlockSpec(wq_bf.shape, lambda i: (0, 0)),
                  pl.BlockSpec(sub_keys.shape, lambda i: (0, 0, 0, 0))],
        out_specs=[pl.BlockSpec((tm, d), lambda i: (i, 0))] + [dense_spec] * 4,
        out_shape=[jax.ShapeDtypeStruct((t, d), BF16), dense(F32), dense(F32), dense(BF16), dense(BF16)],
        compiler_params=_cparams(("parallel",)),
        name="peer_route",
    )(x1, norm_w, sc, sh, wq_bf, sub_keys)


def _gelu_exact(x):
    return 0.5 * x * (1.0 + lax.erf(x * 0.7071067811865476))


def _peer_expert_kernel(ht_ref, u_ref, vt_ref, pa_ref, len_ref, qb_ref, rnk_ref, x_ref, g2_ref, o_ref, acc_scr):
    j = pl.program_id(1)
    be = u_ref.shape[0]
    tm = ht_ref.shape[1]

    @pl.when(j == 0)
    def _():
        acc_scr[...] = jnp.zeros(acc_scr.shape, F32)

    n_slabs = be // PEER_NKEYS
    rows = 16
    n_groups = PEER_NKEYS // rows
    act = _gelu_exact(_dot(u_ref[...], ht_ref[...])).astype(BF16)

    def row_tile(ref, hd, s):
        row = ref[hd, pl.ds(j * n_slabs + s, 1), :]
        return jnp.broadcast_to(row, (rows, tm)).astype(BF16)

    ln = [[row_tile(len_ref, hd, s) for s in range(n_slabs)] for hd in range(PEER_HEADS)]
    pa = [[row_tile(pa_ref, hd, s) for s in range(n_slabs)] for hd in range(PEER_HEADS)]
    zero = jnp.zeros((rows, tm), BF16)
    parts = [[None] * n_groups for _ in range(n_slabs)]
    for g in range(n_groups):
        sl = slice(g * rows, (g + 1) * rows)
        acc = [None] * n_slabs
        for hd in range(PEER_HEADS):
            rk = rnk_ref[hd, sl, :]
            qv = qb_ref[hd, sl, :]
            for s in range(n_slabs):
                term = jnp.where(rk < ln[hd][s], qv, zero) * pa[hd][s]
                acc[s] = term if acc[s] is None else acc[s] + term
        for s in range(n_slabs):
            lo = s * PEER_NKEYS + g * rows
            parts[s][g] = acc[s] * act[lo:lo + rows, :]
    p = jnp.concatenate([parts[s][g] for s in range(n_slabs) for g in range(n_groups)], axis=0)
    acc_scr[...] += _dot(vt_ref[...], p)

    @pl.when(j == pl.num_programs(1) - 1)
    def _():
        o_ref[...] = x_ref[...] + g2_ref[0] * acc_scr[...].T


def _peer_expert(ht_bf, u_bf, vt_bf, pa, ln, qb, rnk, x1, g2, tokens_per_batch):
    t, d = x1.shape
    ne = u_bf.shape[0]
    tm = min(512, tokens_per_batch)
    tpb = tokens_per_batch // tm
    be = 512
    dense_spec = pl.BlockSpec((PEER_HEADS, PEER_NKEYS, tm), lambda i, j: (0, 0, i))
    return pl.pallas_call(
        _peer_expert_kernel,
        grid=(t // tm, ne // be),
        in_specs=[pl.BlockSpec((d, tm), lambda i, j: (0, i)),
                  pl.BlockSpec((be, d), lambda i, j: (j, 0)),
                  pl.BlockSpec((d, be), lambda i, j: (0, j)),
                  dense_spec, dense_spec, dense_spec, dense_spec,
                  pl.BlockSpec((tm, d), lambda i, j: (i, 0)),
                  pl.BlockSpec((1, 1, d), lambda i, j: (i // tpb, 0, 0))],
        out_specs=pl.BlockSpec((tm, d), lambda i, j: (i, 0)),
        out_shape=jax.ShapeDtypeStruct((t, d), F32),
        scratch_shapes=[pltpu.VMEM((d, tm), F32)],
        compiler_params=_cparams(("parallel", "arbitrary")),
        name="peer_expert",
    )(ht_bf, u_bf, vt_bf, pa, ln, qb, rnk, x1, g2)


def _pad_cols(a, n):
    return jnp.pad(a, ((0, 0), (0, n - a.shape[1])))


def _pad_rows(a, n):
    return jnp.pad(a, ((0, n - a.shape[0]), (0, 0)))


def _layer(x, c, w_ada, b_ada, norm_mix_w, w_in, fox_q_norm_w, fox_k_norm_w, fox_f_bias,
           rwkv_mu, rwkv_w0, rwkv_w_up, rwkv_a0, rwkv_a_up, rwkv_g_up, rwkv_k_k, rwkv_k_a,
           rwkv_r_k, rwkv_ln_w, rwkv_ln_b, w_out, norm_ffn_w, peer_w_query, peer_sub_keys,
           peer_u, peer_v):
    B, S, D = x.shape
    T = B * S
    fw = FOX_HEADS * FOX_HEAD_DIM
    rw = RWKV_HEADS * RWKV_HEAD_DIM
    w_lora = rwkv_w_up.shape[0]
    a_lora = rwkv_a_up.shape[0]
    g_lora = rwkv_g_up.shape[0]
    assert w_lora <= LANES and a_lora + FOX_HEADS <= LANES and g_lora == 256

    c_pad = _pad_rows(c, 8)
    mod = _ada_mod(c_pad, w_ada, b_ada)[:B]
    sh1, sc1, g1, sh2, sc2, g2 = [m.reshape(B, 1, D) for m in jnp.split(mod, 6, axis=-1)]

    fox_cols = 4 * fw + FOX_HEADS
    wi_fox, wi_rw = w_in[:, :fox_cols], w_in[:, fox_cols:]
    mu = rwkv_mu.reshape(1, -1)
    seg = lambda a, lo, n: a[:, lo:lo + n]
    w_perm = jnp.concatenate([
        seg(wi_fox, 0, 4 * fw),
        seg(wi_rw, 0, 3 * rw),
        seg(wi_rw, 3 * rw + w_lora + a_lora, g_lora),
        _pad_cols(seg(wi_rw, 3 * rw, w_lora), LANES),
        _pad_cols(jnp.concatenate([seg(wi_rw, 3 * rw + w_lora, a_lora), seg(wi_fox, 4 * fw, FOX_HEADS)], 1), LANES),
    ], axis=1).astype(BF16)
    cols = {"gate": 3 * fw, "rr": 4 * fw, "rk": 4 * fw + rw, "rv": 4 * fw + 2 * rw,
            "gd": 4 * fw + 3 * rw, "lo": 4 * fw + 3 * rw + g_lora}
    f_lane = a_lora
    f_block = (cols["lo"] + LANES) // LANES
    mus = [seg(mu, 0, rw), seg(mu, rw, rw), seg(mu, 2 * rw, rw),
           seg(mu, 3 * rw + w_lora + a_lora, g_lora),
           jnp.concatenate([_pad_cols(seg(mu, 3 * rw, w_lora), LANES),
                            _pad_cols(seg(mu, 3 * rw + w_lora, a_lora), LANES)], 1)]
    scale = FOX_HEAD_DIM ** -0.5 * LOG2_E
    head_w = _pad_cols(jnp.concatenate([jnp.tile(fox_q_norm_w * scale, FOX_HEADS),
                                        jnp.tile(fox_k_norm_w, FOX_HEADS)]).reshape(1, -1), w_perm.shape[1])

    x2 = x.reshape(T, D)
    proj = _in_proj(x2, norm_mix_w.reshape(1, D), sc1, sh1, w_perm, head_w, S, 2 * fw)

    f_bias_row = jnp.zeros((1, LANES), F32).at[0, f_lane:f_lane + FOX_HEADS].set(fox_f_bias)
    cum = _fox_cum(proj, f_bias_row, B, S, f_block)
    cum_rows = cum.reshape(B, S, LANES)[:, :, f_lane:f_lane + FOX_HEADS].transpose(0, 2, 1).reshape(B, FOX_HEADS, 1, S)
    o_fox = _fox_attn(proj, cum_rows, B, S)

    row = lambda a: a.reshape(1, -1)
    r, lw, k, v, a_vec, b_vec, g, bonus = _rwkv_prep(
        proj, cols, mus, row(rwkv_w0), _pad_rows(rwkv_w_up, LANES), row(rwkv_a0), _pad_rows(rwkv_a_up, LANES),
        rwkv_g_up, row(rwkv_k_k), row(rwkv_k_a), row(rwkv_r_k), S)
    y = _rwkv_scan(r, lw, k, v, a_vec, b_vec, B, S)

    x1 = _mix_out(o_fox, proj, cols["gate"], y, bonus, g, x2, g1, row(rwkv_ln_w), row(rwkv_ln_b),
                  w_out.astype(BF16), S)

    h2, pa, ln, qb, rnk = _peer_route(x1, norm_ffn_w.reshape(1, D), sc2, sh2,
                                      peer_w_query.astype(BF16), peer_sub_keys, S)
    out = _peer_expert(h2.T, peer_u.astype(BF16), peer_v.astype(BF16).T, pa, ln, qb, rnk, x1, g2, S)
    return out.reshape(B, S, D)


def kernel(x, c, w_ada, b_ada, norm_mix_w, w_in, fox_q_norm_w, fox_k_norm_w, fox_f_bias, rwkv_mu, rwkv_w0,
           rwkv_w_up, rwkv_a0, rwkv_a_up, rwkv_g_up, rwkv_k_k, rwkv_k_a, rwkv_r_k, rwkv_ln_w, rwkv_ln_b,
           w_out, norm_ffn_w, peer_w_query, peer_sub_keys, peer_u, peer_v):
    params = (w_ada, b_ada, norm_mix_w, w_in, fox_q_norm_w, fox_k_norm_w, fox_f_bias, rwkv_mu, rwkv_w0,
              rwkv_w_up, rwkv_a0, rwkv_a_up, rwkv_g_up, rwkv_k_k, rwkv_k_a, rwkv_r_k, rwkv_ln_w, rwkv_ln_b,
              w_out, norm_ffn_w, peer_w_query, peer_sub_keys, peer_u, peer_v)
    for l in range(w_ada.shape[0]):
        x = _layer(x, c, *[p[l] for p in params])
    return x
```

```python
import functools

import jax
import jax.numpy as jnp
from jax import lax
from jax.experimental import pallas as pl
from jax.experimental.pallas import tpu as pltpu

F32 = jnp.float32
BF16 = jnp.bfloat16
HIGHEST = lax.Precision.HIGHEST

LANES = 128
NORM_EPS = 1e-6
GN_EPS = 64e-5
CHUNK = 64
FOX_HEADS = 8
FOX_HEAD_DIM = 128
RWKV_HEADS = 16
RWKV_HEAD_DIM = 64
PEER_HEADS = 8
PEER_NKEYS = 128
PEER_TOPK = 16
PEER_EXPERT_BLOCK = 1024
PEER_SLABS = PEER_EXPERT_BLOCK // PEER_NKEYS
NEG_BIG = -1e30
LOG2_E = 1.4426950408889634
VMEM_LIMIT = 56 * 1024 * 1024


def _cparams(sem):
    return pltpu.CompilerParams(dimension_semantics=sem, vmem_limit_bytes=VMEM_LIMIT)


def _dot(a, b, precision=None):
    return jnp.dot(a, b, preferred_element_type=F32, precision=precision)


def _dot_nt(a, b, precision=None):
    return lax.dot_general(a, b, (((1,), (1,)), ((), ())), preferred_element_type=F32,
                           precision=precision)


def _dot_tn(a, b, precision=None):
    return lax.dot_general(a, b, (((0,), (0,)), ((), ())), preferred_element_type=F32,
                           precision=precision)


def _ada_kernel(c_ref, w_ref, b_ref, o_ref):
    c = c_ref[...]
    s = c * jax.nn.sigmoid(c)
    o_ref[...] = _dot(s, w_ref[...], HIGHEST) + b_ref[...]


def _ada_mod(c_pad, w_ada, b_ada):
    rows, d = c_pad.shape
    n = w_ada.shape[1]
    bn = 1024
    return pl.pallas_call(
        _ada_kernel,
        grid=(n // bn,),
        in_specs=[pl.BlockSpec((rows, d), lambda j: (0, 0)),
                  pl.BlockSpec((d, bn), lambda j: (0, j)),
                  pl.BlockSpec((1, bn), lambda j: (0, j))],
        out_specs=pl.BlockSpec((rows, bn), lambda j: (0, j)),
        out_shape=jax.ShapeDtypeStruct((rows, n), F32),
        compiler_params=_cparams(("arbitrary",)),
        name="ada_mod",
    )(c_pad, w_ada, b_ada.reshape(1, n))


def _modulated_norm(x, nw, sc, sh):
    y = x * lax.rsqrt(jnp.mean(x * x, axis=-1, keepdims=True) + NORM_EPS)
    return y * nw * (1.0 + sc) + sh


def _in_proj_kernel(x_ref, nw_ref, sc_ref, sh_ref, w_ref, hw_ref, o_ref, h_scr, *, n_qk_blocks):
    j = pl.program_id(1)

    @pl.when(j == 0)
    def _():
        h = _modulated_norm(x_ref[...], nw_ref[...], sc_ref[0], sh_ref[0])
        h_scr[...] = h.astype(BF16)

    acc = _dot(h_scr[...], w_ref[...])

    @pl.when(j < n_qk_blocks)
    def _():
        for hh in range(acc.shape[1] // FOX_HEAD_DIM):
            sl = slice(hh * FOX_HEAD_DIM, (hh + 1) * FOX_HEAD_DIM)
            a = acc[:, sl]
            rs = lax.rsqrt(jnp.mean(a * a, axis=-1, keepdims=True) + NORM_EPS)
            o_ref[:, sl] = a * rs * hw_ref[:, sl]

    @pl.when(j >= n_qk_blocks)
    def _():
        o_ref[...] = acc


def _in_proj(x2, norm_w, sc, sh, w_bf, head_w, tokens_per_batch, n_qk_cols):
    t, d = x2.shape
    n = w_bf.shape[1]
    tm = min(1024, tokens_per_batch)
    bn = 512
    tpb = tokens_per_batch // tm
    kern = functools.partial(_in_proj_kernel, n_qk_blocks=n_qk_cols // bn)
    return pl.pallas_call(
        kern,
        grid=(t // tm, n // bn),
        in_specs=[pl.BlockSpec((tm, d), lambda i, j: (i, 0)),
                  pl.BlockSpec((1, d), lambda i, j: (0, 0)),
                  pl.BlockSpec((1, 1, d), lambda i, j: (i // tpb, 0, 0)),
                  pl.BlockSpec((1, 1, d), lambda i, j: (i // tpb, 0, 0)),
                  pl.BlockSpec((d, bn), lambda i, j: (0, j)),
                  pl.BlockSpec((1, bn), lambda i, j: (0, j))],
        out_specs=pl.BlockSpec((tm, bn), lambda i, j: (i, j)),
        out_shape=jax.ShapeDtypeStruct((t, n), F32),
        scratch_shapes=[pltpu.VMEM((tm, d), BF16)],
        compiler_params=_cparams(("parallel", "arbitrary")),
        name="in_proj",
    )(x2, norm_w, sc, sh, w_bf, head_w)


def _log_sigmoid(x):
    return jnp.minimum(x, 0.0) - jnp.log(1.0 + jnp.exp(-jnp.abs(x)))


def _fox_cum_kernel(f_ref, b_ref, o_ref, *, blk):
    s = f_ref.shape[0]
    row = lax.broadcasted_iota(jnp.int32, (blk, blk), 0)
    col = lax.broadcasted_iota(jnp.int32, (blk, blk), 1)
    tri = (row >= col).astype(F32)
    carry = jnp.zeros((1, f_ref.shape[1]), F32)
    for i in range(s // blk):
        lf = _log_sigmoid(f_ref[i * blk:(i + 1) * blk, :] + b_ref[...])
        cs = _dot(tri, lf, HIGHEST) + carry
        o_ref[i * blk:(i + 1) * blk, :] = cs * LOG2_E
        carry = cs[blk - 1:blk, :]


def _fox_cum(proj, f_bias_row, batch, seq, col_block):
    blk = min(256, seq)
    return pl.pallas_call(
        functools.partial(_fox_cum_kernel, blk=blk),
        grid=(batch,),
        in_specs=[pl.BlockSpec((seq, LANES), lambda b: (b, col_block)),
                  pl.BlockSpec((1, LANES), lambda b: (0, 0))],
        out_specs=pl.BlockSpec((seq, LANES), lambda b: (b, 0)),
        out_shape=jax.ShapeDtypeStruct((batch * seq, LANES), F32),
        compiler_params=_cparams(("parallel",)),
        name="fox_cum",
    )(proj, f_bias_row)


def _fox_attn_kernel(qt_ref, kt_ref, q_ref, k_ref, v_ref, c_ref, o_ref, m_scr, l_scr, acc_scr):
    t = pl.program_id(2)
    qi = qt_ref[t]
    ki = kt_ref[t]
    hd = FOX_HEAD_DIM
    n_heads = q_ref.shape[1] // hd

    @pl.when(ki == 0)
    def _():
        m_scr[...] = jnp.full(m_scr.shape, NEG_BIG, F32)
        l_scr[...] = jnp.zeros(l_scr.shape, F32)
        acc_scr[...] = jnp.zeros(acc_scr.shape, F32)

    def step(masked):
        hs = range(n_heads)
        sl = [slice(h * hd, (h + 1) * hd) for h in hs]
        s = [_dot_nt(q_ref[:, sl[h]].astype(BF16), k_ref[:, sl[h]].astype(BF16)) - c_ref[0, h] for h in hs]
        if masked:
            row = lax.broadcasted_iota(jnp.int32, s[0].shape, 0)
            col = lax.broadcasted_iota(jnp.int32, s[0].shape, 1)
            s = [jnp.where(row >= col, x, NEG_BIG) for x in s]
        m_prev = [m_scr[h] for h in hs]
        m_new = [jnp.maximum(m_prev[h], jnp.max(s[h], axis=-1, keepdims=True)) for h in hs]
        p = [jnp.exp2(s[h] - m_new[h]) for h in hs]
        alpha = [jnp.exp2(m_prev[h] - m_new[h]) for h in hs]
        p_sum = [jnp.sum(p[h], axis=-1, keepdims=True) for h in hs]
        pv = [_dot(p[h].astype(BF16), v_ref[:, sl[h]].astype(BF16)) for h in hs]
        for h in hs:
            l_scr[h] = alpha[h] * l_scr[h] + p_sum[h]
            acc_scr[:, sl[h]] = alpha[h] * acc_scr[:, sl[h]] + pv[h]
            m_scr[h] = m_new[h]

    @pl.when(ki < qi)
    def _():
        step(False)

    @pl.when(ki == qi)
    def _():
        step(True)
        for h in range(n_heads):
            sl = slice(h * hd, (h + 1) * hd)
            o_ref[:, sl] = acc_scr[:, sl] / l_scr[h]


def _fox_attn(proj, cum_rows, batch, seq):
    tq = min(512, seq)
    nq = seq // tq
    hps = 4
    w = hps * FOX_HEAD_DIM
    kcol = FOX_HEADS // hps
    tri = [(q, k) for q in range(nq) for k in range(q + 1)]
    qt = jnp.asarray([q for q, _ in tri], jnp.int32)
    kt = jnp.asarray([k for _, k in tri], jnp.int32)
    grid_spec = pltpu.PrefetchScalarGridSpec(
        num_scalar_prefetch=2,
        grid=(batch, kcol, len(tri)),
        in_specs=[
            pl.BlockSpec((tq, w), lambda b, h, t, qt, kt: (b * nq + qt[t], h)),
            pl.BlockSpec((tq, w), lambda b, h, t, qt, kt: (b * nq + kt[t], kcol + h)),
            pl.BlockSpec((tq, w), lambda b, h, t, qt, kt: (b * nq + kt[t], 2 * kcol + h)),
            pl.BlockSpec((1, hps, 1, tq), lambda b, h, t, qt, kt: (b, h, 0, kt[t])),
        ],
        out_specs=pl.BlockSpec((tq, w), lambda b, h, t, qt, kt: (b * nq + qt[t], h)),
        scratch_shapes=[pltpu.VMEM((hps, tq, 1), F32), pltpu.VMEM((hps, tq, 1), F32),
                        pltpu.VMEM((tq, w), F32)],
    )
    return pl.pallas_call(
        _fox_attn_kernel,
        grid_spec=grid_spec,
        out_shape=jax.ShapeDtypeStruct((batch * seq, FOX_HEADS * FOX_HEAD_DIM), F32),
        compiler_params=_cparams(("parallel", "parallel", "arbitrary")),
        name="fox_attn",
    )(qt, kt, proj, proj, proj, cum_rows)


def _head_sum(x, bd):
    parts = [_dot(x[:, j * LANES:(j + 1) * LANES], bd, HIGHEST) for j in range(x.shape[1] // LANES)]
    return jnp.concatenate(parts, axis=1)


def _head_block_diag():
    r = lax.broadcasted_iota(jnp.int32, (LANES, LANES), 0) // RWKV_HEAD_DIM
    c = lax.broadcasted_iota(jnp.int32, (LANES, LANES), 1) // RWKV_HEAD_DIM
    return (r == c).astype(F32)


def _rwkv_prep_kernel(r_ref, k_ref, v_ref, gd_ref, lo_ref, pr_ref, pk_ref, pv_ref, pgd_ref, plo_ref,
                      mu_r, mu_k, mu_v, mu_gd, mu_lo, w0_ref, wup_ref, a0_ref, aup_ref, gup_ref,
                      kk_ref, ka_ref, rk_ref,
                      or_ref, olw_ref, ok_ref, ov_ref, oa_ref, ob_ref, og_ref, obonus_ref, *, tiles_per_batch):
    i = pl.program_id(0)
    first = (i % tiles_per_batch) == 0

    def shifted(cur_ref, prv_ref, mu_ref):
        cur = cur_ref[...]
        last = jnp.where(first, 0.0, prv_ref[7:8, :])
        row = lax.broadcasted_iota(jnp.int32, cur.shape, 0)
        prev = jnp.where(row == 0, last, pltpu.roll(cur, 1, 0))
        return cur + (prev - cur) * mu_ref[...]

    r = shifted(r_ref, pr_ref, mu_r)
    k = shifted(k_ref, pk_ref, mu_k)
    v = shifted(v_ref, pv_ref, mu_v)
    gd = shifted(gd_ref, pgd_ref, mu_gd)
    lo = shifted(lo_ref, plo_ref, mu_lo)
    wd = lo[:, :LANES]
    ad = lo[:, LANES:]

    w_pre = w0_ref[...] + _dot(jnp.tanh(wd), wup_ref[...], HIGHEST)
    w_raw = _log_sigmoid(w_pre) - 0.5
    log_decay = -jnp.exp(w_raw)
    a = jax.nn.sigmoid(a0_ref[...] + _dot(ad, aup_ref[...], HIGHEST))
    g = _dot(jax.nn.sigmoid(gd), gup_ref[...], HIGHEST)

    bd = _head_block_diag()
    kk = k * kk_ref[...]
    nrm = jnp.maximum(jnp.sqrt(_head_sum(kk * kk, bd)), 1e-12)
    kk = kk / nrm
    k_mod = k * (1.0 + (a - 1.0) * ka_ref[...])
    bonus = _head_sum(r * k_mod * rk_ref[...], bd) * v

    or_ref[...] = r
    olw_ref[...] = log_decay
    ok_ref[...] = k_mod
    ov_ref[...] = v
    oa_ref[...] = -kk
    ob_ref[...] = kk * a
    og_ref[...] = g
    obonus_ref[...] = bonus


def _rwkv_prep(proj, cols, mus, w0, w_up, a0, a_up, g_up, k_k, k_a, r_k, tokens_per_batch):
    t = proj.shape[0]
    w = RWKV_HEADS * RWKV_HEAD_DIM
    tm = min(256, tokens_per_batch)
    tpb = tokens_per_batch // tm
    widths = [w, w, w, 256, 256]
    offs = [cols["rr"], cols["rk"], cols["rv"], cols["gd"], cols["lo"]]
    cur_specs = [pl.BlockSpec((tm, wd), functools.partial(lambda i, cb: (i, cb), cb=o // wd))
                 for wd, o in zip(widths, offs)]
    prv_specs = [pl.BlockSpec((8, wd), functools.partial(
        lambda i, cb: (jnp.maximum(i * (tm // 8) - 1, 0), cb), cb=o // wd))
        for wd, o in zip(widths, offs)]
    full = lambda a: pl.BlockSpec(a.shape, lambda i: (0,) * a.ndim)
    params = list(mus) + [w0, w_up, a0, a_up, g_up, k_k, k_a, r_k]
    out_spec = pl.BlockSpec((tm, w), lambda i: (i, 0))
    return pl.pallas_call(
        functools.partial(_rwkv_prep_kernel, tiles_per_batch=tpb),
        grid=(t // tm,),
        in_specs=cur_specs + prv_specs + [full(p) for p in params],
        out_specs=[out_spec] * 8,
        out_shape=[jax.ShapeDtypeStruct((t, w), F32)] * 8,
        compiler_params=_cparams(("parallel",)),
        name="rwkv_prep",
    )(*([proj] * 10), *params)


def _rwkv_scan_kernel(r_ref, lw_ref, k_ref, v_ref, a_ref, b_ref, o_ref, h_scr):
    c = pl.program_id(1)

    @pl.when(c == 0)
    def _():
        h_scr[...] = jnp.zeros(h_scr.shape, F32)

    C = r_ref.shape[0]
    n_pairs = r_ref.shape[1] // LANES
    P = HIGHEST

    row = lax.broadcasted_iota(jnp.int32, (C, C), 0)
    col = lax.broadcasted_iota(jnp.int32, (C, C), 1)
    tri = (row >= col).astype(F32)
    lw = lw_ref[...]
    cw = _dot(tri, lw, P)
    cw_end = cw[C - 1:C, :]
    e_pos = jnp.exp(cw)
    e_prev = jnp.exp(cw - lw)
    e_neg = jnp.exp(-cw)
    e_end = jnp.exp(cw_end - cw)
    w_end = jnp.exp(cw_end)

    a = a_ref[...]
    b = b_ref[...]
    k = k_ref[...]
    r = r_ref[...]
    at = a * e_prev
    bt = b * e_neg
    kt = k * e_neg
    rt = r * e_pos
    bh = b * e_end
    kh = k * e_end
    v = v_ref[...]

    lane = lax.broadcasted_iota(jnp.int32, (C, LANES), 1)
    head0 = lane < RWKV_HEAD_DIM
    r2 = lax.broadcasted_iota(jnp.int32, (2 * C, 2 * C), 0)
    c2 = lax.broadcasted_iota(jnp.int32, (2 * C, 2 * C), 1)
    same = (r2 // C) == (c2 // C)
    strict = same & (r2 > c2)
    incl = same & (r2 >= c2)
    eye = (r2 == c2).astype(F32)

    def two(x, p):
        xp = x[:, p * LANES:(p + 1) * LANES]
        return jnp.concatenate([jnp.where(head0, xp, 0.0), jnp.where(head0, 0.0, xp)], axis=0)

    def mm(x, y):
        return _dot(x.astype(BF16), y.astype(BF16))

    def split(x):
        hi = x.astype(BF16)
        return hi, (x - hi.astype(F32)).astype(BF16)

    def mm3(x, y):
        xh, xl = split(x)
        yh, yl = split(y)
        return _dot(xh, yh) + (_dot(xh, yl) + _dot(xl, yh))

    G = 2 * C
    pairs = range(n_pairs)
    at2 = [two(at, p) for p in pairs]
    rt2 = [two(rt, p) for p in pairs]
    v2 = [two(v, p).astype(BF16) for p in pairs]
    gram = [_dot_nt(jnp.concatenate([at2[p], rt2[p]], axis=0).astype(BF16),
                    jnp.concatenate([two(bt, p), two(kt, p)], axis=0).astype(BF16)) for p in pairs]
    a_ab = [jnp.where(strict, gram[p][:G, :G], 0.0) for p in pairs]
    nmat = list(a_ab)
    pw = list(a_ab)
    for _ in range(max(1, (C - 1).bit_length() - 1)):
        pw = [mm3(pw[p], pw[p]) for p in pairs]
        nmat = [nmat[p] + pw[p] + mm3(nmat[p], pw[p]) for p in pairs]
    akv = [mm(jnp.where(strict, gram[p][:G, G:], 0.0), v2[p]) for p in pairs]
    rhs = [jnp.concatenate([at2[p], akv[p]], axis=1) for p in pairs]
    pq = [(rhs[p] + mm(nmat[p], rhs[p])).astype(BF16) for p in pairs]
    ry = [mm(jnp.where(incl, gram[p][G:, :G], 0.0), pq[p]) for p in pairs]
    mv = [mm(jnp.where(incl, gram[p][G:, G:], 0.0), v2[p]) for p in pairs]
    ge = [_dot_tn(two(bh, p).astype(BF16), pq[p]) for p in pairs]
    kv = [_dot_tn(two(kh, p).astype(BF16), v2[p]) for p in pairs]
    for p in pairs:
        rr = rt2[p] + ry[p][:, :LANES]
        gm = eye * w_end[:, p * LANES:(p + 1) * LANES] + ge[p][:, :LANES]
        yh = mm(jnp.concatenate([rr, gm], axis=0), h_scr[p])
        y2 = yh[:G] + ry[p][:, LANES:] + mv[p]
        h_scr[p] = yh[G:] + ge[p][:, LANES:] + kv[p]
        o_ref[:, p * LANES:(p + 1) * LANES] = y2[:C, :] + y2[C:, :]


def _rwkv_scan(r, lw, k, v, a, b, batch, seq):
    w = r.shape[1]
    nc = seq // CHUNK
    spec = pl.BlockSpec((CHUNK, w), lambda bi, ci: (bi * nc + ci, 0))
    return pl.pallas_call(
        _rwkv_scan_kernel,
        grid=(batch, nc),
        in_specs=[spec] * 6,
        out_specs=spec,
        out_shape=jax.ShapeDtypeStruct((batch * seq, w), F32),
        scratch_shapes=[pltpu.VMEM((w // LANES, LANES, LANES), F32)],
        compiler_params=_cparams(("parallel", "arbitrary")),
        name="rwkv_scan",
    )(r, lw, k, v, a, b)


def _mix_out_kernel(o_ref, gate_ref, y_ref, bonus_ref, g_ref, x_ref, g1_ref, lnw_ref, lnb_ref, w_ref, out_ref):
    fox = o_ref[...] * jax.nn.sigmoid(gate_ref[...])
    bd = _head_block_diag()
    y = y_ref[...]
    inv_n = 1.0 / RWKV_HEAD_DIM
    mean = _head_sum(y, bd) * inv_n
    d = y - mean
    var = _head_sum(d * d, bd) * inv_n
    yn = d * lax.rsqrt(var + GN_EPS) * lnw_ref[...] + lnb_ref[...]
    rw = (yn + bonus_ref[...]) * g_ref[...]
    wf = fox.shape[1]
    mix = _dot(fox.astype(BF16), w_ref[:wf, :]) + _dot(rw.astype(BF16), w_ref[wf:, :])
    out_ref[...] = x_ref[...] + g1_ref[0] * mix


def _mix_out(o_fox, proj, gate_col, y, bonus, g, x2, g1, ln_w, ln_b, w_out_bf, tokens_per_batch):
    t, d = x2.shape
    wf = o_fox.shape[1]
    wr = y.shape[1]
    tm = min(256, tokens_per_batch)
    tpb = tokens_per_batch // tm
    return pl.pallas_call(
        _mix_out_kernel,
        grid=(t // tm,),
        in_specs=[pl.BlockSpec((tm, wf), lambda i: (i, 0)),
                  pl.BlockSpec((tm, wf), lambda i: (i, gate_col // wf)),
                  pl.BlockSpec((tm, wr), lambda i: (i, 0)),
                  pl.BlockSpec((tm, wr), lambda i: (i, 0)),
                  pl.BlockSpec((tm, wr), lambda i: (i, 0)),
                  pl.BlockSpec((tm, d), lambda i: (i, 0)),
                  pl.BlockSpec((1, 1, d), lambda i: (i // tpb, 0, 0)),
                  pl.BlockSpec((1, wr), lambda i: (0, 0)),
                  pl.BlockSpec((1, wr), lambda i: (0, 0)),
                  pl.BlockSpec(w_out_bf.shape, lambda i: (0, 0))],
        out_specs=pl.BlockSpec((tm, d), lambda i: (i, 0)),
        out_shape=jax.ShapeDtypeStruct((t, d), F32),
        compiler_params=_cparams(("parallel",)),
        name="mix_out",
    )(o_fox, proj, y, bonus, g, x2, g1, ln_w, ln_b, w_out_bf)


def _top_k_mask_rows(s, k):
    n = s.shape[0]
    row = lax.broadcasted_iota(jnp.int32, s.shape, 0).astype(F32)
    rank = jnp.full(s.shape, float(k), F32)
    vals = []
    for r in range(k):
        m = jnp.max(s, axis=0, keepdims=True)
        pos = jnp.min(jnp.where(s == m, row, float(n)), axis=0, keepdims=True)
        hit = row == pos
        vals.append(m)
        rank = jnp.where(hit, float(r), rank)
        s = jnp.where(hit, -jnp.inf, s)
    return jnp.concatenate(vals, axis=0), rank


def _candidate_rows():
    k = PEER_TOPK
    groups, valid = [], []
    for r0 in range(k // 2):
        n1 = k // (r0 + 1)
        for g in range(-(-n1 // 8)):
            groups.append((r0, g * 8))
            valid.append([g * 8 + i < n1 for i in range(8)])
    groups.append((None, k // 2))
    valid.append([True] * 8)
    return groups, valid


def _peer_route_kernel(x_ref, nw_ref, sc_ref, sh_ref, wq_ref, keys_ref,
                       h_ref, pa_ref, len_ref, qb_ref, rnk_ref):
    h = _modulated_norm(x_ref[...], nw_ref[...], sc_ref[0], sh_ref[0]).astype(BF16)
    h_ref[...] = h
    q = _dot(h, wq_ref[...])
    tm = q.shape[0]
    half = keys_ref.shape[-1]
    k = PEER_TOPK
    groups, valid = _candidate_rows()
    sub8 = lax.broadcasted_iota(jnp.int32, (8, tm), 0)
    for hd in range(PEER_HEADS):
        qa = q[:, (hd * 2) * half:(hd * 2 + 1) * half]
        qb = q[:, (hd * 2 + 1) * half:(hd * 2 + 2) * half]
        sa = _dot_nt(keys_ref[hd, 0], qa, HIGHEST)
        sb = _dot_nt(keys_ref[hd, 1], qb, HIGHEST)
        top_a, rank_a = _top_k_mask_rows(sa, k)
        top_b, rank_b = _top_k_mask_rows(sb, k)
        parts = []
        for (r0, off), ok in zip(groups, valid):
            if r0 is None:
                part = top_a[off:off + 8, :] + top_b[0:1, :]
            else:
                part = top_a[r0:r0 + 1, :] + top_b[off:off + 8, :]
            if not all(ok):
                n_ok = sum(ok)
                part = jnp.where(sub8 < n_ok, part, -jnp.inf)
            parts.append(part)
        cand = jnp.concatenate(parts, axis=0)
        _, crank = _top_k_mask_rows(cand, k)
        sel = crank < float(k)
        z = jnp.sum(jnp.where(sel, jnp.exp(cand - cand[0:1, :]), 0.0), axis=0, keepdims=True)
        self32 = jnp.where(sel, 1.0, 0.0)
        length = jnp.zeros_like(sa)
        tail = None
        for gi, (r0, off) in enumerate(groups):
            cnt = self32[gi * 8:(gi + 1) * 8, :]
            if r0 is None:
                tail = cnt
            else:
                length = length + jnp.where(rank_a == float(r0), jnp.sum(cnt, axis=0, keepdims=True), 0.0)
        for r in range(8):
            length = length + jnp.where(rank_a == float(k // 2 + r), tail[r:r + 1, :], 0.0)
        blocked = (PEER_NKEYS // PEER_SLABS, PEER_SLABS, tm)
        pa_ref[:, hd] = jnp.exp(sa - top_a[0:1, :]).reshape(blocked)
        len_ref[:, hd] = length.reshape(blocked)
        qb_ref[hd] = (jnp.exp(sb - top_b[0:1, :]) / z).astype(BF16)
        rnk_ref[hd] = rank_b.astype(BF16)


def _peer_route(x1, norm_w, sc, sh, wq_bf, sub_keys, tokens_per_batch):
    t, d = x1.shape
    tm = min(256, tokens_per_batch)
    tpb = tokens_per_batch // tm
    n_blocks = PEER_NKEYS // PEER_SLABS
    dense = lambda dt: jax.ShapeDtypeStruct((PEER_HEADS, PEER_NKEYS, t), dt)
    dense_spec = pl.BlockSpec((PEER_HEADS, PEER_NKEYS, tm), lambda i: (0, 0, i))
    blocked = jax.ShapeDtypeStruct((n_blocks, PEER_HEADS, PEER_SLABS, t), F32)
    blocked_spec = pl.BlockSpec((n_blocks, PEER_HEADS, PEER_SLABS, tm), lambda i: (0, 0, 0, i))
    return pl.pallas_call(
        _peer_route_kernel,
        grid=(t // tm,),
        in_specs=[pl.BlockSpec((tm, d), lambda i: (i, 0)),
                  pl.BlockSpec((1, d), lambda i: (0, 0)),
                  pl.BlockSpec((1, 1, d), lambda i: (i // tpb, 0, 0)),
                  pl.BlockSpec((1, 1, d), lambda i: (i // tpb, 0, 0)),
                  pl.BlockSpec(wq_bf.shape, lambda i: (0, 0)),
                  pl.BlockSpec(sub_keys.shape, lambda i: (0, 0, 0, 0))],
        out_specs=[pl.BlockSpec((tm, d), lambda i: (i, 0)), blocked_spec, blocked_spec, dense_spec, dense_spec],
        out_shape=[jax.ShapeDtypeStruct((t, d), BF16), blocked, blocked, dense(BF16), dense(BF16)],
        compiler_params=_cparams(("parallel",)),
        name="peer_route",
    )(x1, norm_w, sc, sh, wq_bf, sub_keys)


def _gelu_exact(x):
    return 0.5 * x * (1.0 + lax.erf(x * 0.7071067811865476))


def _peer_expert_kernel(ht_ref, u_ref, vt_ref, pa_ref, len_ref, qb_ref, rnk_ref, x_ref, g2_ref, o_ref, acc_scr):
    j = pl.program_id(1)
    be = u_ref.shape[0]
    tm = ht_ref.shape[1]

    @pl.when(j == 0)
    def _():
        acc_scr[...] = jnp.zeros(acc_scr.shape, F32)

    n_slabs = be // PEER_NKEYS
    rows = 16
    n_groups = PEER_NKEYS // rows
    act = _gelu_exact(_dot(u_ref[...], ht_ref[...])).astype(BF16)

    def row_tile(ref, hd, s):
        return jnp.broadcast_to(ref[0, hd, s:s + 1, :], (rows, tm)).astype(BF16)

    zero = jnp.zeros((rows, tm), BF16)
    parts = [[None] * n_groups for _ in range(n_slabs)]
    slab_group = 4
    for s0 in range(0, n_slabs, slab_group):
        ss = range(s0, min(s0 + slab_group, n_slabs))
        ln = {(hd, s): row_tile(len_ref, hd, s) for hd in range(PEER_HEADS) for s in ss}
        pa = {(hd, s): row_tile(pa_ref, hd, s) for hd in range(PEER_HEADS) for s in ss}
        for g in range(n_groups):
            sl = slice(g * rows, (g + 1) * rows)
            acc = {}
            for hd in range(PEER_HEADS):
                rk = rnk_ref[hd, sl, :]
                qv = qb_ref[hd, sl, :]
                for s in ss:
                    term = jnp.where(rk < ln[hd, s], qv, zero) * pa[hd, s]
                    acc[s] = term if s not in acc else acc[s] + term
            for s in ss:
                lo = s * PEER_NKEYS + g * rows
                parts[s][g] = acc[s] * act[lo:lo + rows, :]
    p = jnp.concatenate([parts[s][g] for s in range(n_slabs) for g in range(n_groups)], axis=0)
    acc_scr[...] += _dot(vt_ref[0], p)

    @pl.when(j == pl.num_programs(1) - 1)
    def _():
        o_ref[...] = x_ref[...] + g2_ref[0] * acc_scr[...].T


def _peer_expert(ht_bf, u_bf, vt_bf, pa, ln, qb, rnk, x1, g2, tokens_per_batch):
    t, d = x1.shape
    ne = u_bf.shape[0]
    tm = min(512, tokens_per_batch)
    tpb = tokens_per_batch // tm
    be = PEER_EXPERT_BLOCK
    dense_spec = pl.BlockSpec((PEER_HEADS, PEER_NKEYS, tm), lambda i, j: (0, 0, i))
    blocked_spec = pl.BlockSpec((1, PEER_HEADS, PEER_SLABS, tm), lambda i, j: (j, 0, 0, i))
    return pl.pallas_call(
        _peer_expert_kernel,
        grid=(t // tm, ne // be),
        in_specs=[pl.BlockSpec((d, tm), lambda i, j: (0, i)),
                  pl.BlockSpec((be, d), lambda i, j: (j, 0)),
                  pl.BlockSpec((1, d, be), lambda i, j: (j, 0, 0)),
                  blocked_spec, blocked_spec, dense_spec, dense_spec,
                  pl.BlockSpec((tm, d), lambda i, j: (i, 0)),
                  pl.BlockSpec((1, 1, d), lambda i, j: (i // tpb, 0, 0))],
        out_specs=pl.BlockSpec((tm, d), lambda i, j: (i, 0)),
        out_shape=jax.ShapeDtypeStruct((t, d), F32),
        scratch_shapes=[pltpu.VMEM((d, tm), F32)],
        compiler_params=_cparams(("parallel", "arbitrary")),
        name="peer_expert",
    )(ht_bf, u_bf, vt_bf, pa, ln, qb, rnk, x1, g2)


def _pad_cols(a, n):
    return jnp.pad(a, ((0, 0), (0, n - a.shape[1])))


def _pad_rows(a, n):
    return jnp.pad(a, ((0, n - a.shape[0]), (0, 0)))


def _layer(x, c, w_ada, b_ada, norm_mix_w, w_in, fox_q_norm_w, fox_k_norm_w, fox_f_bias,
           rwkv_mu, rwkv_w0, rwkv_w_up, rwkv_a0, rwkv_a_up, rwkv_g_up, rwkv_k_k, rwkv_k_a,
           rwkv_r_k, rwkv_ln_w, rwkv_ln_b, w_out, norm_ffn_w, peer_w_query, peer_sub_keys,
           peer_u, peer_v):
    B, S, D = x.shape
    T = B * S
    fw = FOX_HEADS * FOX_HEAD_DIM
    rw = RWKV_HEADS * RWKV_HEAD_DIM
    w_lora = rwkv_w_up.shape[0]
    a_lora = rwkv_a_up.shape[0]
    g_lora = rwkv_g_up.shape[0]
    assert w_lora <= LANES and a_lora + FOX_HEADS <= LANES and g_lora == 256

    c_pad = _pad_rows(c, 8)
    mod = _ada_mod(c_pad, w_ada, b_ada)[:B]
    sh1, sc1, g1, sh2, sc2, g2 = [m.reshape(B, 1, D) for m in jnp.split(mod, 6, axis=-1)]

    fox_cols = 4 * fw + FOX_HEADS
    wi_fox, wi_rw = w_in[:, :fox_cols], w_in[:, fox_cols:]
    mu = rwkv_mu.reshape(1, -1)
    seg = lambda a, lo, n: a[:, lo:lo + n]
    w_perm = jnp.concatenate([
        seg(wi_fox, 0, 4 * fw),
        seg(wi_rw, 0, 3 * rw),
        seg(wi_rw, 3 * rw + w_lora + a_lora, g_lora),
        _pad_cols(seg(wi_rw, 3 * rw, w_lora), LANES),
        _pad_cols(jnp.concatenate([seg(wi_rw, 3 * rw + w_lora, a_lora), seg(wi_fox, 4 * fw, FOX_HEADS)], 1), LANES),
    ], axis=1).astype(BF16)
    cols = {"gate": 3 * fw, "rr": 4 * fw, "rk": 4 * fw + rw, "rv": 4 * fw + 2 * rw,
            "gd": 4 * fw + 3 * rw, "lo": 4 * fw + 3 * rw + g_lora}
    f_lane = a_lora
    f_block = (cols["lo"] + LANES) // LANES
    mus = [seg(mu, 0, rw), seg(mu, rw, rw), seg(mu, 2 * rw, rw),
           seg(mu, 3 * rw + w_lora + a_lora, g_lora),
           jnp.concatenate([_pad_cols(seg(mu, 3 * rw, w_lora), LANES),
                            _pad_cols(seg(mu, 3 * rw + w_lora, a_lora), LANES)], 1)]
    scale = FOX_HEAD_DIM ** -0.5 * LOG2_E
    head_w = _pad_cols(jnp.concatenate([jnp.tile(fox_q_norm_w * scale, FOX_HEADS),
                                        jnp.tile(fox_k_norm_w, FOX_HEADS)]).reshape(1, -1), w_perm.shape[1])

    x2 = x.reshape(T, D)
    proj = _in_proj(x2, norm_mix_w.reshape(1, D), sc1, sh1, w_perm, head_w, S, 2 * fw)

    f_bias_row = jnp.zeros((1, LANES), F32).at[0, f_lane:f_lane + FOX_HEADS].set(fox_f_bias)
    cum = _fox_cum(proj, f_bias_row, B, S, f_block)
    cum_rows = cum.reshape(B, S, LANES)[:, :, f_lane:f_lane + FOX_HEADS].transpose(0, 2, 1).reshape(B, FOX_HEADS, 1, S)
    o_fox = _fox_attn(proj, cum_rows, B, S)

    row = lambda a: a.reshape(1, -1)
    r, lw, k, v, a_vec, b_vec, g, bonus = _rwkv_prep(
        proj, cols, mus, row(rwkv_w0), _pad_rows(rwkv_w_up, LANES), row(rwkv_a0), _pad_rows(rwkv_a_up, LANES),
        rwkv_g_up, row(rwkv_k_k), row(rwkv_k_a), row(rwkv_r_k), S)
    y = _rwkv_scan(r, lw, k, v, a_vec, b_vec, B, S)

    x1 = _mix_out(o_fox, proj, cols["gate"], y, bonus, g, x2, g1, row(rwkv_ln_w), row(rwkv_ln_b),
                  w_out.astype(BF16), S)

    h2, pa, ln, qb, rnk = _peer_route(x1, norm_ffn_w.reshape(1, D), sc2, sh2,
                                      peer_w_query.astype(BF16), peer_sub_keys, S)
    vt = peer_v.astype(BF16).reshape(-1, PEER_EXPERT_BLOCK, D).transpose(0, 2, 1)
    out = _peer_expert(h2.T, peer_u.astype(BF16), vt, pa, ln, qb, rnk, x1, g2, S)
    return out.reshape(B, S, D)


def kernel(x, c, w_ada, b_ada, norm_mix_w, w_in, fox_q_norm_w, fox_k_norm_w, fox_f_bias, rwkv_mu, rwkv_w0,
           rwkv_w_up, rwkv_a0, rwkv_a_up, rwkv_g_up, rwkv_k_k, rwkv_k_a, rwkv_r_k, rwkv_ln_w, rwkv_ln_b,
           w_out, norm_ffn_w, peer_w_query, peer_sub_keys, peer_u, peer_v):
    params = (w_ada, b_ada, norm_mix_w, w_in, fox_q_norm_w, fox_k_norm_w, fox_f_bias, rwkv_mu, rwkv_w0,
              rwkv_w_up, rwkv_a0, rwkv_a_up, rwkv_g_up, rwkv_k_k, rwkv_k_a, rwkv_r_k, rwkv_ln_w, rwkv_ln_b,
              w_out, norm_ffn_w, peer_w_query, peer_sub_keys, peer_u, peer_v)
    for l in range(w_ada.shape[0]):
        x = _layer(x, c, *[p[l] for p in params])
    return x
```

```python
import functools

import jax
import jax.numpy as jnp
from jax import lax
from jax.experimental import pallas as pl
from jax.experimental.pallas import tpu as pltpu

F32 = jnp.float32
BF16 = jnp.bfloat16
HIGHEST = lax.Precision.HIGHEST

LANES = 128
NORM_EPS = 1e-6
GN_EPS = 64e-5
CHUNK = 64
FOX_HEADS = 8
FOX_HEAD_DIM = 128
RWKV_HEADS = 16
RWKV_HEAD_DIM = 64
PEER_HEADS = 8
PEER_NKEYS = 128
PEER_TOPK = 16
PEER_EXPERT_BLOCK = 1024
PEER_SLABS = PEER_EXPERT_BLOCK // PEER_NKEYS
NEG_BIG = -1e30
LOG2_E = 1.4426950408889634
VMEM_LIMIT = 56 * 1024 * 1024


def _cparams(sem):
    return pltpu.CompilerParams(dimension_semantics=sem, vmem_limit_bytes=VMEM_LIMIT)


def _dot(a, b, precision=None):
    return jnp.dot(a, b, preferred_element_type=F32, precision=precision)


def _dot_nt(a, b, precision=None):
    return lax.dot_general(a, b, (((1,), (1,)), ((), ())), preferred_element_type=F32,
                           precision=precision)


def _dot_tn(a, b, precision=None):
    return lax.dot_general(a, b, (((0,), (0,)), ((), ())), preferred_element_type=F32,
                           precision=precision)


def _ada_kernel(c_ref, w_ref, b_ref, o_ref):
    c = c_ref[...]
    s = c * jax.nn.sigmoid(c)
    o_ref[...] = _dot(s, w_ref[...], HIGHEST) + b_ref[...]


def _ada_mod(c_pad, w_ada, b_ada):
    rows, d = c_pad.shape
    n = w_ada.shape[1]
    bn = 1024
    return pl.pallas_call(
        _ada_kernel,
        grid=(n // bn,),
        in_specs=[pl.BlockSpec((rows, d), lambda j: (0, 0)),
                  pl.BlockSpec((d, bn), lambda j: (0, j)),
                  pl.BlockSpec((1, bn), lambda j: (0, j))],
        out_specs=pl.BlockSpec((rows, bn), lambda j: (0, j)),
        out_shape=jax.ShapeDtypeStruct((rows, n), F32),
        compiler_params=_cparams(("arbitrary",)),
        name="ada_mod",
    )(c_pad, w_ada, b_ada.reshape(1, n))


def _modulated_norm(x, nw, sc, sh):
    y = x * lax.rsqrt(jnp.mean(x * x, axis=-1, keepdims=True) + NORM_EPS)
    return y * nw * (1.0 + sc) + sh


def _in_proj_kernel(x_ref, nw_ref, sc_ref, sh_ref, w_ref, hw_ref, o_ref, h_scr, *, n_qk_blocks):
    j = pl.program_id(1)

    @pl.when(j == 0)
    def _():
        h = _modulated_norm(x_ref[...], nw_ref[...], sc_ref[0], sh_ref[0])
        h_scr[...] = h.astype(BF16)

    acc = _dot(h_scr[...], w_ref[...])

    @pl.when(j < n_qk_blocks)
    def _():
        for hh in range(acc.shape[1] // FOX_HEAD_DIM):
            sl = slice(hh * FOX_HEAD_DIM, (hh + 1) * FOX_HEAD_DIM)
            a = acc[:, sl]
            rs = lax.rsqrt(jnp.mean(a * a, axis=-1, keepdims=True) + NORM_EPS)
            o_ref[:, sl] = a * rs * hw_ref[:, sl]

    @pl.when(j >= n_qk_blocks)
    def _():
        o_ref[...] = acc


def _in_proj(x2, norm_w, sc, sh, w_bf, head_w, tokens_per_batch, n_qk_cols):
    t, d = x2.shape
    n = w_bf.shape[1]
    tm = min(1024, tokens_per_batch)
    bn = 512
    tpb = tokens_per_batch // tm
    kern = functools.partial(_in_proj_kernel, n_qk_blocks=n_qk_cols // bn)
    return pl.pallas_call(
        kern,
        grid=(t // tm, n // bn),
        in_specs=[pl.BlockSpec((tm, d), lambda i, j: (i, 0)),
                  pl.BlockSpec((1, d), lambda i, j: (0, 0)),
                  pl.BlockSpec((1, 1, d), lambda i, j: (i // tpb, 0, 0)),
                  pl.BlockSpec((1, 1, d), lambda i, j: (i // tpb, 0, 0)),
                  pl.BlockSpec((d, bn), lambda i, j: (0, j)),
                  pl.BlockSpec((1, bn), lambda i, j: (0, j))],
        out_specs=pl.BlockSpec((tm, bn), lambda i, j: (i, j)),
        out_shape=jax.ShapeDtypeStruct((t, n), F32),
        scratch_shapes=[pltpu.VMEM((tm, d), BF16)],
        compiler_params=_cparams(("parallel", "arbitrary")),
        name="in_proj",
    )(x2, norm_w, sc, sh, w_bf, head_w)


def _log_sigmoid(x):
    return jnp.minimum(x, 0.0) - jnp.log(1.0 + jnp.exp(-jnp.abs(x)))


def _fox_cum_kernel(f_ref, b_ref, o_ref, *, blk):
    s = f_ref.shape[0]
    row = lax.broadcasted_iota(jnp.int32, (blk, blk), 0)
    col = lax.broadcasted_iota(jnp.int32, (blk, blk), 1)
    tri = (row >= col).astype(F32)
    carry = jnp.zeros((1, f_ref.shape[1]), F32)
    for i in range(s // blk):
        lf = _log_sigmoid(f_ref[i * blk:(i + 1) * blk, :] + b_ref[...])
        cs = _dot(tri, lf, HIGHEST) + carry
        o_ref[i * blk:(i + 1) * blk, :] = cs * LOG2_E
        carry = cs[blk - 1:blk, :]


def _fox_cum(proj, f_bias_row, batch, seq, col_block):
    blk = min(256, seq)
    return pl.pallas_call(
        functools.partial(_fox_cum_kernel, blk=blk),
        grid=(batch,),
        in_specs=[pl.BlockSpec((seq, LANES), lambda b: (b, col_block)),
                  pl.BlockSpec((1, LANES), lambda b: (0, 0))],
        out_specs=pl.BlockSpec((seq, LANES), lambda b: (b, 0)),
        out_shape=jax.ShapeDtypeStruct((batch * seq, LANES), F32),
        compiler_params=_cparams(("parallel",)),
        name="fox_cum",
    )(proj, f_bias_row)


def _fox_attn_kernel(qt_ref, kt_ref, q_ref, k_ref, v_ref, c_ref, o_ref, m_scr, l_scr, acc_scr):
    t = pl.program_id(2)
    qi = qt_ref[t]
    ki = kt_ref[t]
    hd = FOX_HEAD_DIM
    n_heads = q_ref.shape[1] // hd

    @pl.when(ki == 0)
    def _():
        m_scr[...] = jnp.full(m_scr.shape, NEG_BIG, F32)
        l_scr[...] = jnp.zeros(l_scr.shape, F32)
        acc_scr[...] = jnp.zeros(acc_scr.shape, F32)

    def step(masked):
        hs = range(n_heads)
        sl = [slice(h * hd, (h + 1) * hd) for h in hs]
        s = [_dot_nt(q_ref[:, sl[h]].astype(BF16), k_ref[:, sl[h]].astype(BF16)) - c_ref[0, h] for h in hs]
        if masked:
            row = lax.broadcasted_iota(jnp.int32, s[0].shape, 0)
            col = lax.broadcasted_iota(jnp.int32, s[0].shape, 1)
            s = [jnp.where(row >= col, x, NEG_BIG) for x in s]
        m_prev = [m_scr[h] for h in hs]
        m_new = [jnp.maximum(m_prev[h], jnp.max(s[h], axis=-1, keepdims=True)) for h in hs]
        p = [jnp.exp2(s[h] - m_new[h]) for h in hs]
        alpha = [jnp.exp2(m_prev[h] - m_new[h]) for h in hs]
        p_sum = [jnp.sum(p[h], axis=-1, keepdims=True) for h in hs]
        pv = [_dot(p[h].astype(BF16), v_ref[:, sl[h]].astype(BF16)) for h in hs]
        for h in hs:
            l_scr[h] = alpha[h] * l_scr[h] + p_sum[h]
            acc_scr[:, sl[h]] = alpha[h] * acc_scr[:, sl[h]] + pv[h]
            m_scr[h] = m_new[h]

    @pl.when(ki < qi)
    def _():
        step(False)

    @pl.when(ki == qi)
    def _():
        step(True)
        for h in range(n_heads):
            sl = slice(h * hd, (h + 1) * hd)
            o_ref[:, sl] = acc_scr[:, sl] / l_scr[h]


def _fox_attn(proj, cum_rows, batch, seq):
    tq = min(512, seq)
    nq = seq // tq
    hps = 4
    w = hps * FOX_HEAD_DIM
    kcol = FOX_HEADS // hps
    tri = [(q, k) for q in range(nq) for k in range(q + 1)]
    qt = jnp.asarray([q for q, _ in tri], jnp.int32)
    kt = jnp.asarray([k for _, k in tri], jnp.int32)
    grid_spec = pltpu.PrefetchScalarGridSpec(
        num_scalar_prefetch=2,
        grid=(batch, kcol, len(tri)),
        in_specs=[
            pl.BlockSpec((tq, w), lambda b, h, t, qt, kt: (b * nq + qt[t], h)),
            pl.BlockSpec((tq, w), lambda b, h, t, qt, kt: (b * nq + kt[t], kcol + h)),
            pl.BlockSpec((tq, w), lambda b, h, t, qt, kt: (b * nq + kt[t], 2 * kcol + h)),
            pl.BlockSpec((1, hps, 1, tq), lambda b, h, t, qt, kt: (b, h, 0, kt[t])),
        ],
        out_specs=pl.BlockSpec((tq, w), lambda b, h, t, qt, kt: (b * nq + qt[t], h)),
        scratch_shapes=[pltpu.VMEM((hps, tq, 1), F32), pltpu.VMEM((hps, tq, 1), F32),
                        pltpu.VMEM((tq, w), F32)],
    )
    return pl.pallas_call(
        _fox_attn_kernel,
        grid_spec=grid_spec,
        out_shape=jax.ShapeDtypeStruct((batch * seq, FOX_HEADS * FOX_HEAD_DIM), F32),
        compiler_params=_cparams(("parallel", "parallel", "arbitrary")),
        name="fox_attn",
    )(qt, kt, proj, proj, proj, cum_rows)


def _head_sum(x, bd):
    parts = [_dot(x[:, j * LANES:(j + 1) * LANES], bd, HIGHEST) for j in range(x.shape[1] // LANES)]
    return jnp.concatenate(parts, axis=1)


def _head_block_diag():
    r = lax.broadcasted_iota(jnp.int32, (LANES, LANES), 0) // RWKV_HEAD_DIM
    c = lax.broadcasted_iota(jnp.int32, (LANES, LANES), 1) // RWKV_HEAD_DIM
    return (r == c).astype(F32)


def _rwkv_prep_kernel(r_ref, k_ref, v_ref, gd_ref, lo_ref, pr_ref, pk_ref, pv_ref, pgd_ref, plo_ref,
                      mu_r, mu_k, mu_v, mu_gd, mu_lo, w0_ref, wup_ref, a0_ref, aup_ref, gup_ref,
                      kk_ref, ka_ref, rk_ref,
                      or_ref, olw_ref, ok_ref, ov_ref, oa_ref, ob_ref, og_ref, obonus_ref, *, tiles_per_batch):
    i = pl.program_id(0)
    first = (i % tiles_per_batch) == 0

    def shifted(cur_ref, prv_ref, mu_ref):
        cur = cur_ref[...]
        last = jnp.where(first, 0.0, prv_ref[7:8, :])
        row = lax.broadcasted_iota(jnp.int32, cur.shape, 0)
        prev = jnp.where(row == 0, last, pltpu.roll(cur, 1, 0))
        return cur + (prev - cur) * mu_ref[...]

    r = shifted(r_ref, pr_ref, mu_r)
    k = shifted(k_ref, pk_ref, mu_k)
    v = shifted(v_ref, pv_ref, mu_v)
    gd = shifted(gd_ref, pgd_ref, mu_gd)
    lo = shifted(lo_ref, plo_ref, mu_lo)
    wd = lo[:, :LANES]
    ad = lo[:, LANES:]

    w_pre = w0_ref[...] + _dot(jnp.tanh(wd), wup_ref[...], HIGHEST)
    w_raw = _log_sigmoid(w_pre) - 0.5
    log_decay = -jnp.exp(w_raw)
    a = jax.nn.sigmoid(a0_ref[...] + _dot(ad, aup_ref[...], HIGHEST))
    g = _dot(jax.nn.sigmoid(gd), gup_ref[...], HIGHEST)

    bd = _head_block_diag()
    kk = k * kk_ref[...]
    nrm = jnp.maximum(jnp.sqrt(_head_sum(kk * kk, bd)), 1e-12)
    kk = kk / nrm
    k_mod = k * (1.0 + (a - 1.0) * ka_ref[...])
    bonus = _head_sum(r * k_mod * rk_ref[...], bd) * v

    or_ref[...] = r
    olw_ref[...] = log_decay
    ok_ref[...] = k_mod
    ov_ref[...] = v
    oa_ref[...] = -kk
    ob_ref[...] = kk * a
    og_ref[...] = g
    obonus_ref[...] = bonus


def _rwkv_prep(proj, cols, mus, w0, w_up, a0, a_up, g_up, k_k, k_a, r_k, tokens_per_batch):
    t = proj.shape[0]
    w = RWKV_HEADS * RWKV_HEAD_DIM
    tm = min(256, tokens_per_batch)
    tpb = tokens_per_batch // tm
    widths = [w, w, w, 256, 256]
    offs = [cols["rr"], cols["rk"], cols["rv"], cols["gd"], cols["lo"]]
    cur_specs = [pl.BlockSpec((tm, wd), functools.partial(lambda i, cb: (i, cb), cb=o // wd))
                 for wd, o in zip(widths, offs)]
    prv_specs = [pl.BlockSpec((8, wd), functools.partial(
        lambda i, cb: (jnp.maximum(i * (tm // 8) - 1, 0), cb), cb=o // wd))
        for wd, o in zip(widths, offs)]
    full = lambda a: pl.BlockSpec(a.shape, lambda i: (0,) * a.ndim)
    params = list(mus) + [w0, w_up, a0, a_up, g_up, k_k, k_a, r_k]
    out_spec = pl.BlockSpec((tm, w), lambda i: (i, 0))
    return pl.pallas_call(
        functools.partial(_rwkv_prep_kernel, tiles_per_batch=tpb),
        grid=(t // tm,),
        in_specs=cur_specs + prv_specs + [full(p) for p in params],
        out_specs=[out_spec] * 8,
        out_shape=[jax.ShapeDtypeStruct((t, w), F32)] * 8,
        compiler_params=_cparams(("parallel",)),
        name="rwkv_prep",
    )(*([proj] * 10), *params)


def _rwkv_scan_kernel(r_ref, lw_ref, k_ref, v_ref, a_ref, b_ref, o_ref, h_scr):
    c = pl.program_id(1)

    @pl.when(c == 0)
    def _():
        h_scr[...] = jnp.zeros(h_scr.shape, F32)

    C = r_ref.shape[0]
    n_pairs = r_ref.shape[1] // LANES
    P = HIGHEST

    row = lax.broadcasted_iota(jnp.int32, (C, C), 0)
    col = lax.broadcasted_iota(jnp.int32, (C, C), 1)
    tri = (row >= col).astype(F32)
    lw = lw_ref[...]
    cw = _dot(tri, lw, P)
    cw_end = cw[C - 1:C, :]
    e_pos = jnp.exp(cw)
    e_prev = jnp.exp(cw - lw)
    e_neg = jnp.exp(-cw)
    e_end = jnp.exp(cw_end - cw)
    w_end = jnp.exp(cw_end)

    a = a_ref[...]
    b = b_ref[...]
    k = k_ref[...]
    r = r_ref[...]
    at = a * e_prev
    bt = b * e_neg
    kt = k * e_neg
    rt = r * e_pos
    bh = b * e_end
    kh = k * e_end
    v = v_ref[...]

    lane = lax.broadcasted_iota(jnp.int32, (C, LANES), 1)
    head0 = lane < RWKV_HEAD_DIM
    r2 = lax.broadcasted_iota(jnp.int32, (2 * C, 2 * C), 0)
    c2 = lax.broadcasted_iota(jnp.int32, (2 * C, 2 * C), 1)
    same = (r2 // C) == (c2 // C)
    strict = same & (r2 > c2)
    incl = same & (r2 >= c2)
    eye = (r2 == c2).astype(F32)

    def two(x, p):
        xp = x[:, p * LANES:(p + 1) * LANES]
        return jnp.concatenate([jnp.where(head0, xp, 0.0), jnp.where(head0, 0.0, xp)], axis=0)

    def mm(x, y):
        return _dot(x.astype(BF16), y.astype(BF16))

    def split(x):
        hi = x.astype(BF16)
        return hi, (x - hi.astype(F32)).astype(BF16)

    def mm3(x, y):
        xh, xl = split(x)
        yh, yl = split(y)
        return _dot(xh, yh) + (_dot(xh, yl) + _dot(xl, yh))

    G = 2 * C
    pairs = range(n_pairs)
    at2 = [two(at, p) for p in pairs]
    rt2 = [two(rt, p) for p in pairs]
    v2 = [two(v, p).astype(BF16) for p in pairs]
    gram = [_dot_nt(jnp.concatenate([at2[p], rt2[p]], axis=0).astype(BF16),
                    jnp.concatenate([two(bt, p), two(kt, p)], axis=0).astype(BF16)) for p in pairs]
    a_ab = [jnp.where(strict, gram[p][:G, :G], 0.0) for p in pairs]
    nmat = list(a_ab)
    pw = list(a_ab)
    for _ in range(max(1, (C - 1).bit_length() - 1)):
        pw = [mm3(pw[p], pw[p]) for p in pairs]
        nmat = [nmat[p] + pw[p] + mm3(nmat[p], pw[p]) for p in pairs]
    akv = [mm(jnp.where(strict, gram[p][:G, G:], 0.0), v2[p]) for p in pairs]
    rhs = [jnp.concatenate([at2[p], akv[p]], axis=1) for p in pairs]
    pq = [(rhs[p] + mm(nmat[p], rhs[p])).astype(BF16) for p in pairs]
    ry = [mm(jnp.where(incl, gram[p][G:, :G], 0.0), pq[p]) for p in pairs]
    mv = [mm(jnp.where(incl, gram[p][G:, G:], 0.0), v2[p]) for p in pairs]
    ge = [_dot_tn(two(bh, p).astype(BF16), pq[p]) for p in pairs]
    kv = [_dot_tn(two(kh, p).astype(BF16), v2[p]) for p in pairs]
    for p in pairs:
        rr = rt2[p] + ry[p][:, :LANES]
        gm = eye * w_end[:, p * LANES:(p + 1) * LANES] + ge[p][:, :LANES]
        yh = mm(jnp.concatenate([rr, gm], axis=0), h_scr[p])
        y2 = yh[:G] + ry[p][:, LANES:] + mv[p]
        h_scr[p] = yh[G:] + ge[p][:, LANES:] + kv[p]
        o_ref[:, p * LANES:(p + 1) * LANES] = y2[:C, :] + y2[C:, :]


def _rwkv_scan(r, lw, k, v, a, b, batch, seq):
    w = r.shape[1]
    nc = seq // CHUNK
    spec = pl.BlockSpec((CHUNK, w), lambda bi, ci: (bi * nc + ci, 0))
    return pl.pallas_call(
        _rwkv_scan_kernel,
        grid=(batch, nc),
        in_specs=[spec] * 6,
        out_specs=spec,
        out_shape=jax.ShapeDtypeStruct((batch * seq, w), F32),
        scratch_shapes=[pltpu.VMEM((w // LANES, LANES, LANES), F32)],
        compiler_params=_cparams(("parallel", "arbitrary")),
        name="rwkv_scan",
    )(r, lw, k, v, a, b)


def _mix_out_kernel(o_ref, gate_ref, y_ref, bonus_ref, g_ref, x_ref, g1_ref, lnw_ref, lnb_ref, w_ref, out_ref):
    fox = o_ref[...] * jax.nn.sigmoid(gate_ref[...])
    bd = _head_block_diag()
    y = y_ref[...]
    inv_n = 1.0 / RWKV_HEAD_DIM
    mean = _head_sum(y, bd) * inv_n
    d = y - mean
    var = _head_sum(d * d, bd) * inv_n
    yn = d * lax.rsqrt(var + GN_EPS) * lnw_ref[...] + lnb_ref[...]
    rw = (yn + bonus_ref[...]) * g_ref[...]
    wf = fox.shape[1]
    mix = _dot(fox.astype(BF16), w_ref[:wf, :]) + _dot(rw.astype(BF16), w_ref[wf:, :])
    out_ref[...] = x_ref[...] + g1_ref[0] * mix


def _mix_out(o_fox, proj, gate_col, y, bonus, g, x2, g1, ln_w, ln_b, w_out_bf, tokens_per_batch):
    t, d = x2.shape
    wf = o_fox.shape[1]
    wr = y.shape[1]
    tm = min(256, tokens_per_batch)
    tpb = tokens_per_batch // tm
    return pl.pallas_call(
        _mix_out_kernel,
        grid=(t // tm,),
        in_specs=[pl.BlockSpec((tm, wf), lambda i: (i, 0)),
                  pl.BlockSpec((tm, wf), lambda i: (i, gate_col // wf)),
                  pl.BlockSpec((tm, wr), lambda i: (i, 0)),
                  pl.BlockSpec((tm, wr), lambda i: (i, 0)),
                  pl.BlockSpec((tm, wr), lambda i: (i, 0)),
                  pl.BlockSpec((tm, d), lambda i: (i, 0)),
                  pl.BlockSpec((1, 1, d), lambda i: (i // tpb, 0, 0)),
                  pl.BlockSpec((1, wr), lambda i: (0, 0)),
                  pl.BlockSpec((1, wr), lambda i: (0, 0)),
                  pl.BlockSpec(w_out_bf.shape, lambda i: (0, 0))],
        out_specs=pl.BlockSpec((tm, d), lambda i: (i, 0)),
        out_shape=jax.ShapeDtypeStruct((t, d), F32),
        compiler_params=_cparams(("parallel",)),
        name="mix_out",
    )(o_fox, proj, y, bonus, g, x2, g1, ln_w, ln_b, w_out_bf)


def _top_k_mask_rows(s, k, tie_safe):
    n = s.shape[0]
    row = lax.broadcasted_iota(jnp.int32, s.shape, 0).astype(F32)
    rank = jnp.full(s.shape, float(k), F32)
    vals = []
    for r in range(k):
        m = jnp.max(s, axis=0, keepdims=True)
        hit = s == m
        if tie_safe:
            pos = jnp.min(jnp.where(hit, row, float(n)), axis=0, keepdims=True)
            hit = row == pos
        vals.append(m)
        rank = jnp.where(hit, float(r), rank)
        s = jnp.where(hit, -jnp.inf, s)
    picked = jnp.sum(jnp.where(rank < float(k), 1.0, 0.0), axis=0, keepdims=True)
    return jnp.concatenate(vals, axis=0), rank, jnp.where(picked == float(k), 1.0, 0.0)


def _candidate_rows():
    k = PEER_TOPK
    groups, valid = [], []
    for r0 in range(k // 2):
        n1 = k // (r0 + 1)
        for g in range(-(-n1 // 8)):
            groups.append((r0, g * 8))
            valid.append([g * 8 + i < n1 for i in range(8)])
    groups.append((None, k // 2))
    valid.append([True] * 8)
    return groups, valid


def _route_head(sa, sb, tie_safe):
    k = PEER_TOPK
    tm = sa.shape[1]
    groups, valid = _candidate_rows()
    sub8 = lax.broadcasted_iota(jnp.int32, (8, tm), 0)
    top_a, rank_a, clean_a = _top_k_mask_rows(sa, k, tie_safe)
    top_b, rank_b, clean_b = _top_k_mask_rows(sb, k, tie_safe)
    parts = []
    for (r0, off), ok in zip(groups, valid):
        if r0 is None:
            part = top_a[off:off + 8, :] + top_b[0:1, :]
        else:
            part = top_a[r0:r0 + 1, :] + top_b[off:off + 8, :]
        if not all(ok):
            part = jnp.where(sub8 < sum(ok), part, -jnp.inf)
        parts.append(part)
    cand = jnp.concatenate(parts, axis=0)
    _, crank, clean_c = _top_k_mask_rows(cand, k, tie_safe)
    sel = crank < float(k)
    z = jnp.sum(jnp.where(sel, jnp.exp(cand - cand[0:1, :]), 0.0), axis=0, keepdims=True)
    self32 = jnp.where(sel, 1.0, 0.0)
    length = jnp.zeros_like(sa)
    tail = None
    for gi, (r0, off) in enumerate(groups):
        cnt = self32[gi * 8:(gi + 1) * 8, :]
        if r0 is None:
            tail = cnt
        else:
            length = length + jnp.where(rank_a == float(r0), jnp.sum(cnt, axis=0, keepdims=True), 0.0)
    for r in range(8):
        length = length + jnp.where(rank_a == float(k // 2 + r), tail[r:r + 1, :], 0.0)
    pa = jnp.exp(sa - top_a[0:1, :])
    qb = jnp.exp(sb - top_b[0:1, :]) / z
    return pa, length, qb, rank_b, clean_a * clean_b * clean_c


def _peer_route_kernel(x_ref, nw_ref, sc_ref, sh_ref, wq_ref, keys_ref,
                       h_ref, pa_ref, len_ref, qb_ref, rnk_ref):
    h = _modulated_norm(x_ref[...], nw_ref[...], sc_ref[0], sh_ref[0]).astype(BF16)
    h_ref[...] = h
    q = _dot(h, wq_ref[...])
    tm = q.shape[0]
    half = keys_ref.shape[-1]
    blocked = (PEER_NKEYS // PEER_SLABS, PEER_SLABS, tm)
    for hd in range(PEER_HEADS):
        qa = q[:, (hd * 2) * half:(hd * 2 + 1) * half]
        qb = q[:, (hd * 2 + 1) * half:(hd * 2 + 2) * half]
        sa = _dot_nt(keys_ref[hd, 0], qa, HIGHEST)
        sb = _dot_nt(keys_ref[hd, 1], qb, HIGHEST)

        def emit(tables):
            pa, length, qbt, rank_b, _ = tables
            pa_ref[:, hd] = pa.reshape(blocked)
            len_ref[:, hd] = length.reshape(blocked)
            qb_ref[hd] = qbt.astype(BF16)
            rnk_ref[hd] = rank_b.astype(BF16)

        fast = _route_head(sa, sb, tie_safe=False)
        emit(fast)
        all_clean = jnp.min(fast[4], axis=1, keepdims=True)[0, 0]

        @pl.when(all_clean < 0.5)
        def _():
            emit(_route_head(sa, sb, tie_safe=True))


def _peer_route(x1, norm_w, sc, sh, wq_bf, sub_keys, tokens_per_batch):
    t, d = x1.shape
    tm = min(256, tokens_per_batch)
    tpb = tokens_per_batch // tm
    n_blocks = PEER_NKEYS // PEER_SLABS
    dense = lambda dt: jax.ShapeDtypeStruct((PEER_HEADS, PEER_NKEYS, t), dt)
    dense_spec = pl.BlockSpec((PEER_HEADS, PEER_NKEYS, tm), lambda i: (0, 0, i))
    blocked = jax.ShapeDtypeStruct((n_blocks, PEER_HEADS, PEER_SLABS, t), F32)
    blocked_spec = pl.BlockSpec((n_blocks, PEER_HEADS, PEER_SLABS, tm), lambda i: (0, 0, 0, i))
    return pl.pallas_call(
        _peer_route_kernel,
        grid=(t // tm,),
        in_specs=[pl.BlockSpec((tm, d), lambda i: (i, 0)),
                  pl.BlockSpec((1, d), lambda i: (0, 0)),
                  pl.BlockSpec((1, 1, d), lambda i: (i // tpb, 0, 0)),
                  pl.BlockSpec((1, 1, d), lambda i: (i // tpb, 0, 0)),
                  pl.BlockSpec(wq_bf.shape, lambda i: (0, 0)),
                  pl.BlockSpec(sub_keys.shape, lambda i: (0, 0, 0, 0))],
        out_specs=[pl.BlockSpec((tm, d), lambda i: (i, 0)), blocked_spec, blocked_spec, dense_spec, dense_spec],
        out_shape=[jax.ShapeDtypeStruct((t, d), BF16), blocked, blocked, dense(BF16), dense(BF16)],
        compiler_params=_cparams(("parallel",)),
        name="peer_route",
    )(x1, norm_w, sc, sh, wq_bf, sub_keys)


def _gelu_exact(x):
    return 0.5 * x * (1.0 + lax.erf(x * 0.7071067811865476))


def _peer_expert_kernel(ht_ref, u_ref, vt_ref, pa_ref, len_ref, qb_ref, rnk_ref, x_ref, g2_ref, o_ref, acc_scr):
    j = pl.program_id(1)
    be = u_ref.shape[0]
    tm = ht_ref.shape[1]

    @pl.when(j == 0)
    def _():
        acc_scr[...] = jnp.zeros(acc_scr.shape, F32)

    n_slabs = be // PEER_NKEYS
    rows = 16
    n_groups = PEER_NKEYS // rows
    act = _gelu_exact(_dot(u_ref[...], ht_ref[...])).astype(BF16)

    def row_tile(ref, hd, s):
        return jnp.broadcast_to(ref[0, hd, s:s + 1, :], (rows, tm)).astype(BF16)

    zero = jnp.zeros((rows, tm), BF16)
    parts = [[None] * n_groups for _ in range(n_slabs)]
    slab_group = 4
    for s0 in range(0, n_slabs, slab_group):
        ss = range(s0, min(s0 + slab_group, n_slabs))
        ln = {(hd, s): row_tile(len_ref, hd, s) for hd in range(PEER_HEADS) for s in ss}
        pa = {(hd, s): row_tile(pa_ref, hd, s) for hd in range(PEER_HEADS) for s in ss}
        for g in range(n_groups):
            sl = slice(g * rows, (g + 1) * rows)
            acc = {}
            for hd in range(PEER_HEADS):
                rk = rnk_ref[hd, sl, :]
                qv = qb_ref[hd, sl, :]
                for s in ss:
                    term = jnp.where(rk < ln[hd, s], qv, zero) * pa[hd, s]
                    acc[s] = term if s not in acc else acc[s] + term
            for s in ss:
                lo = s * PEER_NKEYS + g * rows
                parts[s][g] = acc[s] * act[lo:lo + rows, :]
    p = jnp.concatenate([parts[s][g] for s in range(n_slabs) for g in range(n_groups)], axis=0)
    acc_scr[...] += _dot(vt_ref[0], p)

    @pl.when(j == pl.num_programs(1) - 1)
    def _():
        o_ref[...] = x_ref[...] + g2_ref[0] * acc_scr[...].T


def _peer_expert(ht_bf, u_bf, vt_bf, pa, ln, qb, rnk, x1, g2, tokens_per_batch):
    t, d = x1.shape
    ne = u_bf.shape[0]
    tm = min(512, tokens_per_batch)
    tpb = tokens_per_batch // tm
    be = PEER_EXPERT_BLOCK
    dense_spec = pl.BlockSpec((PEER_HEADS, PEER_NKEYS, tm), lambda i, j: (0, 0, i))
    blocked_spec = pl.BlockSpec((1, PEER_HEADS, PEER_SLABS, tm), lambda i, j: (j, 0, 0, i))
    return pl.pallas_call(
        _peer_expert_kernel,
        grid=(t // tm, ne // be),
        in_specs=[pl.BlockSpec((d, tm), lambda i, j: (0, i)),
                  pl.BlockSpec((be, d), lambda i, j: (j, 0)),
                  pl.BlockSpec((1, d, be), lambda i, j: (j, 0, 0)),
                  blocked_spec, blocked_spec, dense_spec, dense_spec,
                  pl.BlockSpec((tm, d), lambda i, j: (i, 0)),
                  pl.BlockSpec((1, 1, d), lambda i, j: (i // tpb, 0, 0))],
        out_specs=pl.BlockSpec((tm, d), lambda i, j: (i, 0)),
        out_shape=jax.ShapeDtypeStruct((t, d), F32),
        scratch_shapes=[pltpu.VMEM((d, tm), F32)],
        compiler_params=_cparams(("parallel", "arbitrary")),
        name="peer_expert",
    )(ht_bf, u_bf, vt_bf, pa, ln, qb, rnk, x1, g2)


def _pad_cols(a, n):
    return jnp.pad(a, ((0, 0), (0, n - a.shape[1])))


def _pad_rows(a, n):
    return jnp.pad(a, ((0, n - a.shape[0]), (0, 0)))


def _layer(x, c, w_ada, b_ada, norm_mix_w, w_in, fox_q_norm_w, fox_k_norm_w, fox_f_bias,
           rwkv_mu, rwkv_w0, rwkv_w_up, rwkv_a0, rwkv_a_up, rwkv_g_up, rwkv_k_k, rwkv_k_a,
           rwkv_r_k, rwkv_ln_w, rwkv_ln_b, w_out, norm_ffn_w, peer_w_query, peer_sub_keys,
           peer_u, peer_v):
    B, S, D = x.shape
    T = B * S
    fw = FOX_HEADS * FOX_HEAD_DIM
    rw = RWKV_HEADS * RWKV_HEAD_DIM
    w_lora = rwkv_w_up.shape[0]
    a_lora = rwkv_a_up.shape[0]
    g_lora = rwkv_g_up.shape[0]
    assert w_lora <= LANES and a_lora + FOX_HEADS <= LANES and g_lora == 256

    c_pad = _pad_rows(c, 8)
    mod = _ada_mod(c_pad, w_ada, b_ada)[:B]
    sh1, sc1, g1, sh2, sc2, g2 = [m.reshape(B, 1, D) for m in jnp.split(mod, 6, axis=-1)]

    fox_cols = 4 * fw + FOX_HEADS
    wi_fox, wi_rw = w_in[:, :fox_cols], w_in[:, fox_cols:]
    mu = rwkv_mu.reshape(1, -1)
    seg = lambda a, lo, n: a[:, lo:lo + n]
    w_perm = jnp.concatenate([
        seg(wi_fox, 0, 4 * fw),
        seg(wi_rw, 0, 3 * rw),
        seg(wi_rw, 3 * rw + w_lora + a_lora, g_lora),
        _pad_cols(seg(wi_rw, 3 * rw, w_lora), LANES),
        _pad_cols(jnp.concatenate([seg(wi_rw, 3 * rw + w_lora, a_lora), seg(wi_fox, 4 * fw, FOX_HEADS)], 1), LANES),
    ], axis=1).astype(BF16)
    cols = {"gate": 3 * fw, "rr": 4 * fw, "rk": 4 * fw + rw, "rv": 4 * fw + 2 * rw,
            "gd": 4 * fw + 3 * rw, "lo": 4 * fw + 3 * rw + g_lora}
    f_lane = a_lora
    f_block = (cols["lo"] + LANES) // LANES
    mus = [seg(mu, 0, rw), seg(mu, rw, rw), seg(mu, 2 * rw, rw),
           seg(mu, 3 * rw + w_lora + a_lora, g_lora),
           jnp.concatenate([_pad_cols(seg(mu, 3 * rw, w_lora), LANES),
                            _pad_cols(seg(mu, 3 * rw + w_lora, a_lora), LANES)], 1)]
    scale = FOX_HEAD_DIM ** -0.5 * LOG2_E
    head_w = _pad_cols(jnp.concatenate([jnp.tile(fox_q_norm_w * scale, FOX_HEADS),
                                        jnp.tile(fox_k_norm_w, FOX_HEADS)]).reshape(1, -1), w_perm.shape[1])

    x2 = x.reshape(T, D)
    proj = _in_proj(x2, norm_mix_w.reshape(1, D), sc1, sh1, w_perm, head_w, S, 2 * fw)

    f_bias_row = jnp.zeros((1, LANES), F32).at[0, f_lane:f_lane + FOX_HEADS].set(fox_f_bias)
    cum = _fox_cum(proj, f_bias_row, B, S, f_block)
    cum_rows = cum.reshape(B, S, LANES)[:, :, f_lane:f_lane + FOX_HEADS].transpose(0, 2, 1).reshape(B, FOX_HEADS, 1, S)
    o_fox = _fox_attn(proj, cum_rows, B, S)

    row = lambda a: a.reshape(1, -1)
    r, lw, k, v, a_vec, b_vec, g, bonus = _rwkv_prep(
        proj, cols, mus, row(rwkv_w0), _pad_rows(rwkv_w_up, LANES), row(rwkv_a0), _pad_rows(rwkv_a_up, LANES),
        rwkv_g_up, row(rwkv_k_k), row(rwkv_k_a), row(rwkv_r_k), S)
    y = _rwkv_scan(r, lw, k, v, a_vec, b_vec, B, S)

    x1 = _mix_out(o_fox, proj, cols["gate"], y, bonus, g, x2, g1, row(rwkv_ln_w), row(rwkv_ln_b),
                  w_out.astype(BF16), S)

    h2, pa, ln, qb, rnk = _peer_route(x1, norm_ffn_w.reshape(1, D), sc2, sh2,
                                      peer_w_query.astype(BF16), peer_sub_keys, S)
    vt = peer_v.astype(BF16).reshape(-1, PEER_EXPERT_BLOCK, D).transpose(0, 2, 1)
    out = _peer_expert(h2.T, peer_u.astype(BF16), vt, pa, ln, qb, rnk, x1, g2, S)
    return out.reshape(B, S, D)


def kernel(x, c, w_ada, b_ada, norm_mix_w, w_in, fox_q_norm_w, fox_k_norm_w, fox_f_bias, rwkv_mu, rwkv_w0,
           rwkv_w_up, rwkv_a0, rwkv_a_up, rwkv_g_up, rwkv_k_k, rwkv_k_a, rwkv_r_k, rwkv_ln_w, rwkv_ln_b,
           w_out, norm_ffn_w, peer_w_query, peer_sub_keys, peer_u, peer_v):
    params = (w_ada, b_ada, norm_mix_w, w_in, fox_q_norm_w, fox_k_norm_w, fox_f_bias, rwkv_mu, rwkv_w0,
              rwkv_w_up, rwkv_a0, rwkv_a_up, rwkv_g_up, rwkv_k_k, rwkv_k_a, rwkv_r_k, rwkv_ln_w, rwkv_ln_b,
              w_out, norm_ffn_w, peer_w_query, peer_sub_keys, peer_u, peer_v)
    for l in range(w_ada.shape[0]):
        x = _layer(x, c, *[p[l] for p in params])
    return x
```

```python
import functools

import jax
import jax.numpy as jnp
from jax import lax
from jax.experimental import pallas as pl
from jax.experimental.pallas import tpu as pltpu

F32 = jnp.float32
BF16 = jnp.bfloat16
HIGHEST = lax.Precision.HIGHEST

LANES = 128
NORM_EPS = 1e-6
GN_EPS = 64e-5
CHUNK = 64
FOX_HEADS = 8
FOX_HEAD_DIM = 128
RWKV_HEADS = 16
RWKV_HEAD_DIM = 64
PEER_HEADS = 8
PEER_NKEYS = 128
PEER_TOPK = 16
PEER_EXPERT_BLOCK = 1024
PEER_SLABS = PEER_EXPERT_BLOCK // PEER_NKEYS
NEG_BIG = -1e30
LOG2_E = 1.4426950408889634
VMEM_LIMIT = 56 * 1024 * 1024


def _cparams(sem):
    return pltpu.CompilerParams(dimension_semantics=sem, vmem_limit_bytes=VMEM_LIMIT)


def _dot(a, b, precision=None):
    return jnp.dot(a, b, preferred_element_type=F32, precision=precision)


def _dot_nt(a, b, precision=None):
    return lax.dot_general(a, b, (((1,), (1,)), ((), ())), preferred_element_type=F32,
                           precision=precision)


def _dot_tn(a, b, precision=None):
    return lax.dot_general(a, b, (((0,), (0,)), ((), ())), preferred_element_type=F32,
                           precision=precision)


def _ada_kernel(c_ref, w_ref, b_ref, o_ref):
    c = c_ref[...]
    s = c * jax.nn.sigmoid(c)
    o_ref[...] = _dot(s, w_ref[...], HIGHEST) + b_ref[...]


def _ada_mod(c_pad, w_ada, b_ada):
    rows, d = c_pad.shape
    n = w_ada.shape[1]
    bn = 1024
    return pl.pallas_call(
        _ada_kernel,
        grid=(n // bn,),
        in_specs=[pl.BlockSpec((rows, d), lambda j: (0, 0)),
                  pl.BlockSpec((d, bn), lambda j: (0, j)),
                  pl.BlockSpec((1, bn), lambda j: (0, j))],
        out_specs=pl.BlockSpec((rows, bn), lambda j: (0, j)),
        out_shape=jax.ShapeDtypeStruct((rows, n), F32),
        compiler_params=_cparams(("arbitrary",)),
        name="ada_mod",
    )(c_pad, w_ada, b_ada.reshape(1, n))


def _modulated_norm(x, nw, sc, sh):
    y = x * lax.rsqrt(jnp.mean(x * x, axis=-1, keepdims=True) + NORM_EPS)
    return y * nw * (1.0 + sc) + sh


def _in_proj_kernel(x_ref, nw_ref, sc_ref, sh_ref, w_ref, hw_ref, o_ref, h_scr, *, n_qk_blocks):
    j = pl.program_id(1)

    @pl.when(j == 0)
    def _():
        h = _modulated_norm(x_ref[...], nw_ref[...], sc_ref[0], sh_ref[0])
        h_scr[...] = h.astype(BF16)

    acc = _dot(h_scr[...], w_ref[...])

    @pl.when(j < n_qk_blocks)
    def _():
        for hh in range(acc.shape[1] // FOX_HEAD_DIM):
            sl = slice(hh * FOX_HEAD_DIM, (hh + 1) * FOX_HEAD_DIM)
            a = acc[:, sl]
            rs = lax.rsqrt(jnp.mean(a * a, axis=-1, keepdims=True) + NORM_EPS)
            o_ref[:, sl] = a * rs * hw_ref[:, sl]

    @pl.when(j >= n_qk_blocks)
    def _():
        o_ref[...] = acc


def _in_proj(x2, norm_w, sc, sh, w_bf, head_w, tokens_per_batch, n_qk_cols):
    t, d = x2.shape
    n = w_bf.shape[1]
    tm = min(1024, tokens_per_batch)
    bn = 512
    tpb = tokens_per_batch // tm
    kern = functools.partial(_in_proj_kernel, n_qk_blocks=n_qk_cols // bn)
    return pl.pallas_call(
        kern,
        grid=(t // tm, n // bn),
        in_specs=[pl.BlockSpec((tm, d), lambda i, j: (i, 0)),
                  pl.BlockSpec((1, d), lambda i, j: (0, 0)),
                  pl.BlockSpec((1, 1, d), lambda i, j: (i // tpb, 0, 0)),
                  pl.BlockSpec((1, 1, d), lambda i, j: (i // tpb, 0, 0)),
                  pl.BlockSpec((d, bn), lambda i, j: (0, j)),
                  pl.BlockSpec((1, bn), lambda i, j: (0, j))],
        out_specs=pl.BlockSpec((tm, bn), lambda i, j: (i, j)),
        out_shape=jax.ShapeDtypeStruct((t, n), F32),
        scratch_shapes=[pltpu.VMEM((tm, d), BF16)],
        compiler_params=_cparams(("parallel", "arbitrary")),
        name="in_proj",
    )(x2, norm_w, sc, sh, w_bf, head_w)


def _log_sigmoid(x):
    return jnp.minimum(x, 0.0) - jnp.log(1.0 + jnp.exp(-jnp.abs(x)))


def _fox_cum_kernel(f_ref, b_ref, o_ref, *, blk):
    s = f_ref.shape[0]
    row = lax.broadcasted_iota(jnp.int32, (blk, blk), 0)
    col = lax.broadcasted_iota(jnp.int32, (blk, blk), 1)
    tri = (row >= col).astype(F32)
    carry = jnp.zeros((1, f_ref.shape[1]), F32)
    for i in range(s // blk):
        lf = _log_sigmoid(f_ref[i * blk:(i + 1) * blk, :] + b_ref[...])
        cs = _dot(tri, lf, HIGHEST) + carry
        o_ref[i * blk:(i + 1) * blk, :] = cs * LOG2_E
        carry = cs[blk - 1:blk, :]


def _fox_cum(proj, f_bias_row, batch, seq, col_block):
    blk = min(256, seq)
    return pl.pallas_call(
        functools.partial(_fox_cum_kernel, blk=blk),
        grid=(batch,),
        in_specs=[pl.BlockSpec((seq, LANES), lambda b: (b, col_block)),
                  pl.BlockSpec((1, LANES), lambda b: (0, 0))],
        out_specs=pl.BlockSpec((seq, LANES), lambda b: (b, 0)),
        out_shape=jax.ShapeDtypeStruct((batch * seq, LANES), F32),
        compiler_params=_cparams(("parallel",)),
        name="fox_cum",
    )(proj, f_bias_row)


def _fox_attn_kernel(qt_ref, kt_ref, q_ref, k_ref, v_ref, c_ref, o_ref, m_scr, l_scr, acc_scr):
    t = pl.program_id(2)
    qi = qt_ref[t]
    ki = kt_ref[t]
    hd = FOX_HEAD_DIM
    n_heads = q_ref.shape[1] // hd

    @pl.when(ki == 0)
    def _():
        m_scr[...] = jnp.full(m_scr.shape, NEG_BIG, F32)
        l_scr[...] = jnp.zeros(l_scr.shape, F32)
        acc_scr[...] = jnp.zeros(acc_scr.shape, F32)

    def step(masked):
        hs = range(n_heads)
        sl = [slice(h * hd, (h + 1) * hd) for h in hs]
        s = [_dot_nt(q_ref[:, sl[h]].astype(BF16), k_ref[:, sl[h]].astype(BF16)) - c_ref[0, h] for h in hs]
        if masked:
            row = lax.broadcasted_iota(jnp.int32, s[0].shape, 0)
            col = lax.broadcasted_iota(jnp.int32, s[0].shape, 1)
            s = [jnp.where(row >= col, x, NEG_BIG) for x in s]
        m_prev = [m_scr[h] for h in hs]
        m_new = [jnp.maximum(m_prev[h], jnp.max(s[h], axis=-1, keepdims=True)) for h in hs]
        n_rep = s[0].shape[1] // hd
        p = [jnp.exp2(s[h] - jnp.concatenate([m_new[h]] * n_rep, axis=1)) for h in hs]
        alpha = [jnp.exp2(m_prev[h] - m_new[h]) for h in hs]
        ones = jnp.ones((k_ref.shape[0], hd), BF16)
        pv = [_dot(p[h].astype(BF16), jnp.concatenate([v_ref[:, sl[h]].astype(BF16), ones], axis=1))
              for h in hs]
        for h in hs:
            l_scr[h] = alpha[h] * l_scr[h] + pv[h][:, hd:]
            acc_scr[:, sl[h]] = alpha[h] * acc_scr[:, sl[h]] + pv[h][:, :hd]
            m_scr[h] = m_new[h]

    @pl.when(ki < qi)
    def _():
        step(False)

    @pl.when(ki == qi)
    def _():
        step(True)
        for h in range(n_heads):
            sl = slice(h * hd, (h + 1) * hd)
            o_ref[:, sl] = acc_scr[:, sl] / l_scr[h]


def _fox_attn(proj, cum_rows, batch, seq):
    tq = min(512, seq)
    nq = seq // tq
    hps = 4
    w = hps * FOX_HEAD_DIM
    kcol = FOX_HEADS // hps
    tri = [(q, k) for q in range(nq) for k in range(q + 1)]
    qt = jnp.asarray([q for q, _ in tri], jnp.int32)
    kt = jnp.asarray([k for _, k in tri], jnp.int32)
    grid_spec = pltpu.PrefetchScalarGridSpec(
        num_scalar_prefetch=2,
        grid=(batch, kcol, len(tri)),
        in_specs=[
            pl.BlockSpec((tq, w), lambda b, h, t, qt, kt: (b * nq + qt[t], h)),
            pl.BlockSpec((tq, w), lambda b, h, t, qt, kt: (b * nq + kt[t], kcol + h)),
            pl.BlockSpec((tq, w), lambda b, h, t, qt, kt: (b * nq + kt[t], 2 * kcol + h)),
            pl.BlockSpec((1, hps, 1, tq), lambda b, h, t, qt, kt: (b, h, 0, kt[t])),
        ],
        out_specs=pl.BlockSpec((tq, w), lambda b, h, t, qt, kt: (b * nq + qt[t], h)),
        scratch_shapes=[pltpu.VMEM((hps, tq, FOX_HEAD_DIM), F32), pltpu.VMEM((hps, tq, FOX_HEAD_DIM), F32),
                        pltpu.VMEM((tq, w), F32)],
    )
    return pl.pallas_call(
        _fox_attn_kernel,
        grid_spec=grid_spec,
        out_shape=jax.ShapeDtypeStruct((batch * seq, FOX_HEADS * FOX_HEAD_DIM), F32),
        compiler_params=_cparams(("parallel", "parallel", "arbitrary")),
        name="fox_attn",
    )(qt, kt, proj, proj, proj, cum_rows)


def _head_sum(x, bd):
    parts = [_dot(x[:, j * LANES:(j + 1) * LANES], bd, HIGHEST) for j in range(x.shape[1] // LANES)]
    return jnp.concatenate(parts, axis=1)


def _head_block_diag():
    r = lax.broadcasted_iota(jnp.int32, (LANES, LANES), 0) // RWKV_HEAD_DIM
    c = lax.broadcasted_iota(jnp.int32, (LANES, LANES), 1) // RWKV_HEAD_DIM
    return (r == c).astype(F32)


def _rwkv_prep_kernel(r_ref, k_ref, v_ref, gd_ref, lo_ref, pr_ref, pk_ref, pv_ref, pgd_ref, plo_ref,
                      mu_r, mu_k, mu_v, mu_gd, mu_lo, w0_ref, wup_ref, a0_ref, aup_ref, gup_ref,
                      kk_ref, ka_ref, rk_ref,
                      or_ref, olw_ref, ok_ref, ov_ref, oa_ref, ob_ref, og_ref, obonus_ref, *, tiles_per_batch):
    i = pl.program_id(0)
    first = (i % tiles_per_batch) == 0

    def shifted(cur_ref, prv_ref, mu_ref):
        cur = cur_ref[...]
        last = jnp.where(first, 0.0, prv_ref[7:8, :])
        row = lax.broadcasted_iota(jnp.int32, cur.shape, 0)
        prev = jnp.where(row == 0, last, pltpu.roll(cur, 1, 0))
        return cur + (prev - cur) * mu_ref[...]

    r = shifted(r_ref, pr_ref, mu_r)
    k = shifted(k_ref, pk_ref, mu_k)
    v = shifted(v_ref, pv_ref, mu_v)
    gd = shifted(gd_ref, pgd_ref, mu_gd)
    lo = shifted(lo_ref, plo_ref, mu_lo)
    wd = lo[:, :LANES]
    ad = lo[:, LANES:]

    w_pre = w0_ref[...] + _dot(jnp.tanh(wd), wup_ref[...], HIGHEST)
    w_raw = _log_sigmoid(w_pre) - 0.5
    log_decay = -jnp.exp(w_raw)
    a = jax.nn.sigmoid(a0_ref[...] + _dot(ad, aup_ref[...], HIGHEST))
    g = _dot(jax.nn.sigmoid(gd), gup_ref[...], HIGHEST)

    bd = _head_block_diag()
    kk = k * kk_ref[...]
    nrm = jnp.maximum(jnp.sqrt(_head_sum(kk * kk, bd)), 1e-12)
    kk = kk / nrm
    k_mod = k * (1.0 + (a - 1.0) * ka_ref[...])
    bonus = _head_sum(r * k_mod * rk_ref[...], bd) * v

    or_ref[...] = r.astype(BF16)
    olw_ref[...] = log_decay
    ok_ref[...] = k_mod.astype(BF16)
    ov_ref[...] = v.astype(BF16)
    oa_ref[...] = (-kk).astype(BF16)
    ob_ref[...] = (kk * a).astype(BF16)
    og_ref[...] = g.astype(BF16)
    obonus_ref[...] = bonus.astype(BF16)


def _rwkv_prep(proj, cols, mus, w0, w_up, a0, a_up, g_up, k_k, k_a, r_k, tokens_per_batch):
    t = proj.shape[0]
    w = RWKV_HEADS * RWKV_HEAD_DIM
    tm = min(256, tokens_per_batch)
    tpb = tokens_per_batch // tm
    widths = [w, w, w, 256, 256]
    offs = [cols["rr"], cols["rk"], cols["rv"], cols["gd"], cols["lo"]]
    cur_specs = [pl.BlockSpec((tm, wd), functools.partial(lambda i, cb: (i, cb), cb=o // wd))
                 for wd, o in zip(widths, offs)]
    prv_specs = [pl.BlockSpec((8, wd), functools.partial(
        lambda i, cb: (jnp.maximum(i * (tm // 8) - 1, 0), cb), cb=o // wd))
        for wd, o in zip(widths, offs)]
    full = lambda a: pl.BlockSpec(a.shape, lambda i: (0,) * a.ndim)
    params = list(mus) + [w0, w_up, a0, a_up, g_up, k_k, k_a, r_k]
    out_spec = pl.BlockSpec((tm, w), lambda i: (i, 0))
    return pl.pallas_call(
        functools.partial(_rwkv_prep_kernel, tiles_per_batch=tpb),
        grid=(t // tm,),
        in_specs=cur_specs + prv_specs + [full(p) for p in params],
        out_specs=[out_spec] * 8,
        out_shape=[jax.ShapeDtypeStruct((t, w), dt) for dt in (BF16, F32, BF16, BF16, BF16, BF16, BF16, BF16)],
        compiler_params=_cparams(("parallel",)),
        name="rwkv_prep",
    )(*([proj] * 10), *params)


def _rwkv_scan_kernel(r_ref, lw_ref, k_ref, v_ref, a_ref, b_ref, o_ref, h_scr):
    c = pl.program_id(1)

    @pl.when(c == 0)
    def _():
        h_scr[...] = jnp.zeros(h_scr.shape, F32)

    C = r_ref.shape[0]
    n_pairs = r_ref.shape[1] // LANES
    P = HIGHEST

    row = lax.broadcasted_iota(jnp.int32, (C, C), 0)
    col = lax.broadcasted_iota(jnp.int32, (C, C), 1)
    tri = (row >= col).astype(F32)
    lw = lw_ref[...]
    cw = _dot(tri, lw, P)
    cw_end = cw[C - 1:C, :]
    e_pos = jnp.exp(cw)
    e_prev = jnp.exp(cw - lw)
    e_neg = jnp.exp(-cw)
    e_end = jnp.exp(cw_end - cw)
    w_end = jnp.exp(cw_end)

    a = a_ref[...].astype(F32)
    b = b_ref[...].astype(F32)
    k = k_ref[...].astype(F32)
    r = r_ref[...].astype(F32)
    at = a * e_prev
    bt = b * e_neg
    kt = k * e_neg
    rt = r * e_pos
    bh = b * e_end
    kh = k * e_end
    v = v_ref[...].astype(F32)

    lane = lax.broadcasted_iota(jnp.int32, (C, LANES), 1)
    head0 = lane < RWKV_HEAD_DIM
    r2 = lax.broadcasted_iota(jnp.int32, (2 * C, 2 * C), 0)
    c2 = lax.broadcasted_iota(jnp.int32, (2 * C, 2 * C), 1)
    same = (r2 // C) == (c2 // C)
    strict = same & (r2 > c2)
    incl = same & (r2 >= c2)
    eye = (r2 == c2).astype(F32)

    def two(x, p):
        xp = x[:, p * LANES:(p + 1) * LANES]
        return jnp.concatenate([jnp.where(head0, xp, 0.0), jnp.where(head0, 0.0, xp)], axis=0)

    def mm(x, y):
        return _dot(x.astype(BF16), y.astype(BF16))

    def split(x):
        hi = x.astype(BF16)
        return hi, (x - hi.astype(F32)).astype(BF16)

    def mm3(x, y):
        xh, xl = split(x)
        yh, yl = split(y)
        return _dot(xh, yh) + (_dot(xh, yl) + _dot(xl, yh))

    G = 2 * C
    pairs = range(n_pairs)
    at2 = [two(at, p) for p in pairs]
    rt2 = [two(rt, p) for p in pairs]
    v2 = [two(v, p).astype(BF16) for p in pairs]
    gram = [_dot_nt(jnp.concatenate([at2[p], rt2[p]], axis=0).astype(BF16),
                    jnp.concatenate([two(bt, p), two(kt, p)], axis=0).astype(BF16)) for p in pairs]
    a_ab = [jnp.where(strict, gram[p][:G, :G], 0.0) for p in pairs]
    nmat = list(a_ab)
    pw = list(a_ab)
    for _ in range(max(1, (C - 1).bit_length() - 1)):
        pw = [mm3(pw[p], pw[p]) for p in pairs]
        nmat = [nmat[p] + pw[p] + mm3(nmat[p], pw[p]) for p in pairs]
    akv = [mm(jnp.where(strict, gram[p][:G, G:], 0.0), v2[p]) for p in pairs]
    rhs = [jnp.concatenate([at2[p], akv[p]], axis=1) for p in pairs]
    pq = [(rhs[p] + mm(nmat[p], rhs[p])).astype(BF16) for p in pairs]
    ry = [mm(jnp.where(incl, gram[p][G:, :G], 0.0), pq[p]) for p in pairs]
    mv = [mm(jnp.where(incl, gram[p][G:, G:], 0.0), v2[p]) for p in pairs]
    ge = [_dot_tn(two(bh, p).astype(BF16), pq[p]) for p in pairs]
    kv = [_dot_tn(two(kh, p).astype(BF16), v2[p]) for p in pairs]
    for p in pairs:
        rr = rt2[p] + ry[p][:, :LANES]
        gm = eye * w_end[:, p * LANES:(p + 1) * LANES] + ge[p][:, :LANES]
        yh = mm(jnp.concatenate([rr, gm], axis=0), h_scr[p])
        y2 = yh[:G] + ry[p][:, LANES:] + mv[p]
        h_scr[p] = yh[G:] + ge[p][:, LANES:] + kv[p]
        o_ref[:, p * LANES:(p + 1) * LANES] = y2[:C, :] + y2[C:, :]


def _rwkv_scan(r, lw, k, v, a, b, batch, seq):
    w = r.shape[1]
    nc = seq // CHUNK
    spec = pl.BlockSpec((CHUNK, w), lambda bi, ci: (bi * nc + ci, 0))
    return pl.pallas_call(
        _rwkv_scan_kernel,
        grid=(batch, nc),
        in_specs=[spec] * 6,
        out_specs=spec,
        out_shape=jax.ShapeDtypeStruct((batch * seq, w), F32),
        scratch_shapes=[pltpu.VMEM((w // LANES, LANES, LANES), F32)],
        compiler_params=_cparams(("parallel", "arbitrary")),
        name="rwkv_scan",
    )(r, lw, k, v, a, b)


def _mix_out_kernel(o_ref, gate_ref, y_ref, bonus_ref, g_ref, x_ref, g1_ref, lnw_ref, lnb_ref, w_ref, out_ref):
    fox = o_ref[...] * jax.nn.sigmoid(gate_ref[...])
    bd = _head_block_diag()
    y = y_ref[...]
    inv_n = 1.0 / RWKV_HEAD_DIM
    mean = _head_sum(y, bd) * inv_n
    d = y - mean
    var = _head_sum(d * d, bd) * inv_n
    yn = d * lax.rsqrt(var + GN_EPS) * lnw_ref[...] + lnb_ref[...]
    rw = (yn + bonus_ref[...].astype(F32)) * g_ref[...].astype(F32)
    wf = fox.shape[1]
    mix = _dot(fox.astype(BF16), w_ref[:wf, :]) + _dot(rw.astype(BF16), w_ref[wf:, :])
    out_ref[...] = x_ref[...] + g1_ref[0] * mix


def _mix_out(o_fox, proj, gate_col, y, bonus, g, x2, g1, ln_w, ln_b, w_out_bf, tokens_per_batch):
    t, d = x2.shape
    wf = o_fox.shape[1]
    wr = y.shape[1]
    tm = min(256, tokens_per_batch)
    tpb = tokens_per_batch // tm
    return pl.pallas_call(
        _mix_out_kernel,
        grid=(t // tm,),
        in_specs=[pl.BlockSpec((tm, wf), lambda i: (i, 0)),
                  pl.BlockSpec((tm, wf), lambda i: (i, gate_col // wf)),
                  pl.BlockSpec((tm, wr), lambda i: (i, 0)),
                  pl.BlockSpec((tm, wr), lambda i: (i, 0)),
                  pl.BlockSpec((tm, wr), lambda i: (i, 0)),
                  pl.BlockSpec((tm, d), lambda i: (i, 0)),
                  pl.BlockSpec((1, 1, d), lambda i: (i // tpb, 0, 0)),
                  pl.BlockSpec((1, wr), lambda i: (0, 0)),
                  pl.BlockSpec((1, wr), lambda i: (0, 0)),
                  pl.BlockSpec(w_out_bf.shape, lambda i: (0, 0))],
        out_specs=pl.BlockSpec((tm, d), lambda i: (i, 0)),
        out_shape=jax.ShapeDtypeStruct((t, d), F32),
        compiler_params=_cparams(("parallel",)),
        name="mix_out",
    )(o_fox, proj, y, bonus, g, x2, g1, ln_w, ln_b, w_out_bf)


def _top_k_mask_rows(s, k, tie_safe):
    n = s.shape[0]
    row = lax.broadcasted_iota(jnp.int32, s.shape, 0).astype(F32)
    rank = jnp.full(s.shape, float(k), F32)
    vals = []
    for r in range(k):
        m = jnp.max(s, axis=0, keepdims=True)
        hit = s == m
        if tie_safe:
            pos = jnp.min(jnp.where(hit, row, float(n)), axis=0, keepdims=True)
            hit = row == pos
        vals.append(m)
        rank = jnp.where(hit, float(r), rank)
        s = jnp.where(hit, -jnp.inf, s)
    picked = jnp.sum(jnp.where(rank < float(k), 1.0, 0.0), axis=0, keepdims=True)
    return jnp.concatenate(vals, axis=0), rank, jnp.where(picked == float(k), 1.0, 0.0)


def _candidate_rows():
    k = PEER_TOPK
    groups, valid = [], []
    for r0 in range(k // 2):
        n1 = k // (r0 + 1)
        for g in range(-(-n1 // 8)):
            groups.append((r0, g * 8))
            valid.append([g * 8 + i < n1 for i in range(8)])
    groups.append((None, k // 2))
    valid.append([True] * 8)
    return groups, valid


def _route_head(sa, sb, tie_safe):
    k = PEER_TOPK
    tm = sa.shape[1]
    groups, valid = _candidate_rows()
    sub8 = lax.broadcasted_iota(jnp.int32, (8, tm), 0)
    top_a, rank_a, clean_a = _top_k_mask_rows(sa, k, tie_safe)
    top_b, rank_b, clean_b = _top_k_mask_rows(sb, k, tie_safe)
    parts = []
    for (r0, off), ok in zip(groups, valid):
        if r0 is None:
            part = top_a[off:off + 8, :] + top_b[0:1, :]
        else:
            part = top_a[r0:r0 + 1, :] + top_b[off:off + 8, :]
        if not all(ok):
            part = jnp.where(sub8 < sum(ok), part, -jnp.inf)
        parts.append(part)
    cand = jnp.concatenate(parts, axis=0)
    _, crank, clean_c = _top_k_mask_rows(cand, k, tie_safe)
    sel = crank < float(k)
    z = jnp.sum(jnp.where(sel, jnp.exp(cand - cand[0:1, :]), 0.0), axis=0, keepdims=True)
    self32 = jnp.where(sel, 1.0, 0.0)
    length = jnp.zeros_like(sa)
    tail = None
    for gi, (r0, off) in enumerate(groups):
        cnt = self32[gi * 8:(gi + 1) * 8, :]
        if r0 is None:
            tail = cnt
        else:
            length = length + jnp.where(rank_a == float(r0), jnp.sum(cnt, axis=0, keepdims=True), 0.0)
    for r in range(8):
        length = length + jnp.where(rank_a == float(k // 2 + r), tail[r:r + 1, :], 0.0)
    pa = jnp.exp(sa - top_a[0:1, :])
    qb = jnp.exp(sb - top_b[0:1, :]) / z
    return pa, length, qb, rank_b, clean_a * clean_b * clean_c


def _peer_route_kernel(x_ref, nw_ref, sc_ref, sh_ref, wq_ref, keys_ref,
                       h_ref, pa_ref, len_ref, qb_ref, rnk_ref):
    h = _modulated_norm(x_ref[...], nw_ref[...], sc_ref[0], sh_ref[0]).astype(BF16)
    h_ref[...] = h
    q = _dot(h, wq_ref[...])
    tm = q.shape[0]
    half = keys_ref.shape[-1]
    blocked = (PEER_NKEYS // PEER_SLABS, PEER_SLABS, tm)
    for hd in range(PEER_HEADS):
        qa = q[:, (hd * 2) * half:(hd * 2 + 1) * half]
        qb = q[:, (hd * 2 + 1) * half:(hd * 2 + 2) * half]
        sa = _dot_nt(keys_ref[hd, 0], qa, HIGHEST)
        sb = _dot_nt(keys_ref[hd, 1], qb, HIGHEST)

        def emit(tables):
            pa, length, qbt, rank_b, _ = tables
            pa_ref[:, hd] = pa.reshape(blocked)
            len_ref[:, hd] = length.reshape(blocked)
            qb_ref[hd] = qbt.astype(BF16)
            rnk_ref[hd] = rank_b.astype(BF16)

        fast = _route_head(sa, sb, tie_safe=False)
        emit(fast)
        all_clean = jnp.min(fast[4], axis=1, keepdims=True)[0, 0]

        @pl.when(all_clean < 0.5)
        def _():
            emit(_route_head(sa, sb, tie_safe=True))


def _peer_route(x1, norm_w, sc, sh, wq_bf, sub_keys, tokens_per_batch):
    t, d = x1.shape
    tm = min(256, tokens_per_batch)
    tpb = tokens_per_batch // tm
    n_blocks = PEER_NKEYS // PEER_SLABS
    dense = lambda dt: jax.ShapeDtypeStruct((PEER_HEADS, PEER_NKEYS, t), dt)
    dense_spec = pl.BlockSpec((PEER_HEADS, PEER_NKEYS, tm), lambda i: (0, 0, i))
    blocked = jax.ShapeDtypeStruct((n_blocks, PEER_HEADS, PEER_SLABS, t), F32)
    blocked_spec = pl.BlockSpec((n_blocks, PEER_HEADS, PEER_SLABS, tm), lambda i: (0, 0, 0, i))
    return pl.pallas_call(
        _peer_route_kernel,
        grid=(t // tm,),
        in_specs=[pl.BlockSpec((tm, d), lambda i: (i, 0)),
                  pl.BlockSpec((1, d), lambda i: (0, 0)),
                  pl.BlockSpec((1, 1, d), lambda i: (i // tpb, 0, 0)),
                  pl.BlockSpec((1, 1, d), lambda i: (i // tpb, 0, 0)),
                  pl.BlockSpec(wq_bf.shape, lambda i: (0, 0)),
                  pl.BlockSpec(sub_keys.shape, lambda i: (0, 0, 0, 0))],
        out_specs=[pl.BlockSpec((tm, d), lambda i: (i, 0)), blocked_spec, blocked_spec, dense_spec, dense_spec],
        out_shape=[jax.ShapeDtypeStruct((t, d), BF16), blocked, blocked, dense(BF16), dense(BF16)],
        compiler_params=_cparams(("parallel",)),
        name="peer_route",
    )(x1, norm_w, sc, sh, wq_bf, sub_keys)


def _gelu_exact(x):
    return 0.5 * x * (1.0 + lax.erf(x * 0.7071067811865476))


def _peer_expert_kernel(ht_ref, u_ref, vt_ref, pa_ref, len_ref, qb_ref, rnk_ref, x_ref, g2_ref, o_ref, acc_scr):
    j = pl.program_id(1)
    be = u_ref.shape[0]
    tm = ht_ref.shape[1]

    @pl.when(j == 0)
    def _():
        acc_scr[...] = jnp.zeros(acc_scr.shape, F32)

    n_slabs = be // PEER_NKEYS
    rows = 16
    n_groups = PEER_NKEYS // rows
    act = _gelu_exact(_dot(u_ref[...], ht_ref[...])).astype(BF16)

    def row_tile(ref, hd, s):
        return jnp.broadcast_to(ref[0, hd, s:s + 1, :], (rows, tm)).astype(BF16)

    zero = jnp.zeros((rows, tm), BF16)
    parts = [[None] * n_groups for _ in range(n_slabs)]
    slab_group = 4
    for s0 in range(0, n_slabs, slab_group):
        ss = range(s0, min(s0 + slab_group, n_slabs))
        ln = {(hd, s): row_tile(len_ref, hd, s) for hd in range(PEER_HEADS) for s in ss}
        pa = {(hd, s): row_tile(pa_ref, hd, s) for hd in range(PEER_HEADS) for s in ss}
        for g in range(n_groups):
            sl = slice(g * rows, (g + 1) * rows)
            acc = {}
            for hd in range(PEER_HEADS):
                rk = rnk_ref[hd, sl, :]
                qv = qb_ref[hd, sl, :]
                for s in ss:
                    term = jnp.where(rk < ln[hd, s], qv, zero) * pa[hd, s]
                    acc[s] = term if s not in acc else acc[s] + term
            for s in ss:
                lo = s * PEER_NKEYS + g * rows
                parts[s][g] = acc[s] * act[lo:lo + rows, :]
    p = jnp.concatenate([parts[s][g] for s in range(n_slabs) for g in range(n_groups)], axis=0)
    acc_scr[...] += _dot(vt_ref[0], p)

    @pl.when(j == pl.num_programs(1) - 1)
    def _():
        o_ref[...] = x_ref[...] + g2_ref[0] * acc_scr[...].T


def _peer_expert(ht_bf, u_bf, vt_bf, pa, ln, qb, rnk, x1, g2, tokens_per_batch):
    t, d = x1.shape
    ne = u_bf.shape[0]
    tm = min(512, tokens_per_batch)
    tpb = tokens_per_batch // tm
    be = PEER_EXPERT_BLOCK
    dense_spec = pl.BlockSpec((PEER_HEADS, PEER_NKEYS, tm), lambda i, j: (0, 0, i))
    blocked_spec = pl.BlockSpec((1, PEER_HEADS, PEER_SLABS, tm), lambda i, j: (j, 0, 0, i))
    return pl.pallas_call(
        _peer_expert_kernel,
        grid=(t // tm, ne // be),
        in_specs=[pl.BlockSpec((d, tm), lambda i, j: (0, i)),
                  pl.BlockSpec((be, d), lambda i, j: (j, 0)),
                  pl.BlockSpec((1, d, be), lambda i, j: (j, 0, 0)),
                  blocked_spec, blocked_spec, dense_spec, dense_spec,
                  pl.BlockSpec((tm, d), lambda i, j: (i, 0)),
                  pl.BlockSpec((1, 1, d), lambda i, j: (i // tpb, 0, 0))],
        out_specs=pl.BlockSpec((tm, d), lambda i, j: (i, 0)),
        out_shape=jax.ShapeDtypeStruct((t, d), F32),
        scratch_shapes=[pltpu.VMEM((d, tm), F32)],
        compiler_params=_cparams(("parallel", "arbitrary")),
        name="peer_expert",
    )(ht_bf, u_bf, vt_bf, pa, ln, qb, rnk, x1, g2)


def _pad_cols(a, n):
    return jnp.pad(a, ((0, 0), (0, n - a.shape[1])))


def _pad_rows(a, n):
    return jnp.pad(a, ((0, n - a.shape[0]), (0, 0)))


def _layer(x, c, w_ada, b_ada, norm_mix_w, w_in, fox_q_norm_w, fox_k_norm_w, fox_f_bias,
           rwkv_mu, rwkv_w0, rwkv_w_up, rwkv_a0, rwkv_a_up, rwkv_g_up, rwkv_k_k, rwkv_k_a,
           rwkv_r_k, rwkv_ln_w, rwkv_ln_b, w_out, norm_ffn_w, peer_w_query, peer_sub_keys,
           peer_u, peer_v):
    B, S, D = x.shape
    T = B * S
    fw = FOX_HEADS * FOX_HEAD_DIM
    rw = RWKV_HEADS * RWKV_HEAD_DIM
    w_lora = rwkv_w_up.shape[0]
    a_lora = rwkv_a_up.shape[0]
    g_lora = rwkv_g_up.shape[0]
    assert w_lora <= LANES and a_lora + FOX_HEADS <= LANES and g_lora == 256

    c_pad = _pad_rows(c, 8)
    mod = _ada_mod(c_pad, w_ada, b_ada)[:B]
    sh1, sc1, g1, sh2, sc2, g2 = [m.reshape(B, 1, D) for m in jnp.split(mod, 6, axis=-1)]

    fox_cols = 4 * fw + FOX_HEADS
    wi_fox, wi_rw = w_in[:, :fox_cols], w_in[:, fox_cols:]
    mu = rwkv_mu.reshape(1, -1)
    seg = lambda a, lo, n: a[:, lo:lo + n]
    w_perm = jnp.concatenate([
        seg(wi_fox, 0, 4 * fw),
        seg(wi_rw, 0, 3 * rw),
        seg(wi_rw, 3 * rw + w_lora + a_lora, g_lora),
        _pad_cols(seg(wi_rw, 3 * rw, w_lora), LANES),
        _pad_cols(jnp.concatenate([seg(wi_rw, 3 * rw + w_lora, a_lora), seg(wi_fox, 4 * fw, FOX_HEADS)], 1), LANES),
    ], axis=1).astype(BF16)
    cols = {"gate": 3 * fw, "rr": 4 * fw, "rk": 4 * fw + rw, "rv": 4 * fw + 2 * rw,
            "gd": 4 * fw + 3 * rw, "lo": 4 * fw + 3 * rw + g_lora}
    f_lane = a_lora
    f_block = (cols["lo"] + LANES) // LANES
    mus = [seg(mu, 0, rw), seg(mu, rw, rw), seg(mu, 2 * rw, rw),
           seg(mu, 3 * rw + w_lora + a_lora, g_lora),
           jnp.concatenate([_pad_cols(seg(mu, 3 * rw, w_lora), LANES),
                            _pad_cols(seg(mu, 3 * rw + w_lora, a_lora), LANES)], 1)]
    scale = FOX_HEAD_DIM ** -0.5 * LOG2_E
    head_w = _pad_cols(jnp.concatenate([jnp.tile(fox_q_norm_w * scale, FOX_HEADS),
                                        jnp.tile(fox_k_norm_w, FOX_HEADS)]).reshape(1, -1), w_perm.shape[1])

    x2 = x.reshape(T, D)
    proj = _in_proj(x2, norm_mix_w.reshape(1, D), sc1, sh1, w_perm, head_w, S, 2 * fw)

    f_bias_row = jnp.zeros((1, LANES), F32).at[0, f_lane:f_lane + FOX_HEADS].set(fox_f_bias)
    cum = _fox_cum(proj, f_bias_row, B, S, f_block)
    cum_rows = cum.reshape(B, S, LANES)[:, :, f_lane:f_lane + FOX_HEADS].transpose(0, 2, 1).reshape(B, FOX_HEADS, 1, S)
    o_fox = _fox_attn(proj, cum_rows, B, S)

    row = lambda a: a.reshape(1, -1)
    r, lw, k, v, a_vec, b_vec, g, bonus = _rwkv_prep(
        proj, cols, mus, row(rwkv_w0), _pad_rows(rwkv_w_up, LANES), row(rwkv_a0), _pad_rows(rwkv_a_up, LANES),
        rwkv_g_up, row(rwkv_k_k), row(rwkv_k_a), row(rwkv_r_k), S)
    y = _rwkv_scan(r, lw, k, v, a_vec, b_vec, B, S)

    x1 = _mix_out(o_fox, proj, cols["gate"], y, bonus, g, x2, g1, row(rwkv_ln_w), row(rwkv_ln_b),
                  w_out.astype(BF16), S)

    h2, pa, ln, qb, rnk = _peer_route(x1, norm_ffn_w.reshape(1, D), sc2, sh2,
                                      peer_w_query.astype(BF16), peer_sub_keys, S)
    vt = peer_v.astype(BF16).reshape(-1, PEER_EXPERT_BLOCK, D).transpose(0, 2, 1)
    out = _peer_expert(h2.T, peer_u.astype(BF16), vt, pa, ln, qb, rnk, x1, g2, S)
    return out.reshape(B, S, D)


def kernel(x, c, w_ada, b_ada, norm_mix_w, w_in, fox_q_norm_w, fox_k_norm_w, fox_f_bias, rwkv_mu, rwkv_w0,
           rwkv_w_up, rwkv_a0, rwkv_a_up, rwkv_g_up, rwkv_k_k, rwkv_k_a, rwkv_r_k, rwkv_ln_w, rwkv_ln_b,
           w_out, norm_ffn_w, peer_w_query, peer_sub_keys, peer_u, peer_v):
    params = (w_ada, b_ada, norm_mix_w, w_in, fox_q_norm_w, fox_k_norm_w, fox_f_bias, rwkv_mu, rwkv_w0,
              rwkv_w_up, rwkv_a0, rwkv_a_up, rwkv_g_up, rwkv_k_k, rwkv_k_a, rwkv_r_k, rwkv_ln_w, rwkv_ln_b,
              w_out, norm_ffn_w, peer_w_query, peer_sub_keys, peer_u, peer_v)
    for l in range(w_ada.shape[0]):
        x = _layer(x, c, *[p[l] for p in params])
    return x
```

```python
import functools

import jax
import jax.numpy as jnp
from jax import lax
from jax.experimental import pallas as pl
from jax.experimental.pallas import tpu as pltpu

F32 = jnp.float32
BF16 = jnp.bfloat16
HIGHEST = lax.Precision.HIGHEST

LANES = 128
NORM_EPS = 1e-6
GN_EPS = 64e-5
CHUNK = 64
FOX_HEADS = 8
FOX_HEAD_DIM = 128
RWKV_HEADS = 16
RWKV_HEAD_DIM = 64
PEER_HEADS = 8
PEER_NKEYS = 128
PEER_TOPK = 16
PEER_EXPERT_BLOCK = 1024
PEER_SLABS = PEER_EXPERT_BLOCK // PEER_NKEYS
NEG_BIG = -1e30
LOG2_E = 1.4426950408889634
VMEM_LIMIT = 56 * 1024 * 1024


def _cparams(sem):
    return pltpu.CompilerParams(dimension_semantics=sem, vmem_limit_bytes=VMEM_LIMIT)


def _dot(a, b, precision=None):
    return jnp.dot(a, b, preferred_element_type=F32, precision=precision)


def _dot_nt(a, b, precision=None):
    return lax.dot_general(a, b, (((1,), (1,)), ((), ())), preferred_element_type=F32,
                           precision=precision)


def _split_bf16(x):
    hi = x.astype(BF16)
    return hi, (x - hi.astype(F32)).astype(BF16)


def _dot_bf16x3(a, b):
    ah, al = _split_bf16(a)
    bh, bl = _split_bf16(b)
    return _dot(ah, bh) + (_dot(ah, bl) + _dot(al, bh))


def _dot_tn(a, b, precision=None):
    return lax.dot_general(a, b, (((0,), (0,)), ((), ())), preferred_element_type=F32,
                           precision=precision)


def _ada_kernel(c_ref, w_ref, b_ref, o_ref):
    c = c_ref[...]
    s = c * jax.nn.sigmoid(c)
    o_ref[...] = _dot(s, w_ref[...], HIGHEST) + b_ref[...]


def _ada_mod(c_pad, w_ada, b_ada):
    rows, d = c_pad.shape
    n = w_ada.shape[1]
    bn = 1024
    return pl.pallas_call(
        _ada_kernel,
        grid=(n // bn,),
        in_specs=[pl.BlockSpec((rows, d), lambda j: (0, 0)),
                  pl.BlockSpec((d, bn), lambda j: (0, j)),
                  pl.BlockSpec((1, bn), lambda j: (0, j))],
        out_specs=pl.BlockSpec((rows, bn), lambda j: (0, j)),
        out_shape=jax.ShapeDtypeStruct((rows, n), F32),
        compiler_params=_cparams(("arbitrary",)),
        name="ada_mod",
    )(c_pad, w_ada, b_ada.reshape(1, n))


def _modulated_norm(x, nw, sc, sh):
    y = x * lax.rsqrt(jnp.mean(x * x, axis=-1, keepdims=True) + NORM_EPS)
    return y * nw * (1.0 + sc) + sh


def _in_proj_kernel(x_ref, nw_ref, sc_ref, sh_ref, w_ref, hw_ref, o_ref, h_scr, *, n_qk_blocks):
    j = pl.program_id(1)

    @pl.when(j == 0)
    def _():
        h = _modulated_norm(x_ref[...], nw_ref[...], sc_ref[0], sh_ref[0])
        h_scr[...] = h.astype(BF16)

    acc = _dot(h_scr[...], w_ref[...])

    @pl.when(j < n_qk_blocks)
    def _():
        for hh in range(acc.shape[1] // FOX_HEAD_DIM):
            sl = slice(hh * FOX_HEAD_DIM, (hh + 1) * FOX_HEAD_DIM)
            a = acc[:, sl]
            rs = lax.rsqrt(jnp.mean(a * a, axis=-1, keepdims=True) + NORM_EPS)
            o_ref[:, sl] = a * rs * hw_ref[:, sl]

    @pl.when(j >= n_qk_blocks)
    def _():
        o_ref[...] = acc


def _in_proj(x2, norm_w, sc, sh, w_bf, head_w, tokens_per_batch, n_qk_cols):
    t, d = x2.shape
    n = w_bf.shape[1]
    tm = min(1024, tokens_per_batch)
    bn = 512
    tpb = tokens_per_batch // tm
    kern = functools.partial(_in_proj_kernel, n_qk_blocks=n_qk_cols // bn)
    return pl.pallas_call(
        kern,
        grid=(t // tm, n // bn),
        in_specs=[pl.BlockSpec((tm, d), lambda i, j: (i, 0)),
                  pl.BlockSpec((1, d), lambda i, j: (0, 0)),
                  pl.BlockSpec((1, 1, d), lambda i, j: (i // tpb, 0, 0)),
                  pl.BlockSpec((1, 1, d), lambda i, j: (i // tpb, 0, 0)),
                  pl.BlockSpec((d, bn), lambda i, j: (0, j)),
                  pl.BlockSpec((1, bn), lambda i, j: (0, j))],
        out_specs=pl.BlockSpec((tm, bn), lambda i, j: (i, j)),
        out_shape=jax.ShapeDtypeStruct((t, n), F32),
        scratch_shapes=[pltpu.VMEM((tm, d), BF16)],
        compiler_params=_cparams(("parallel", "arbitrary")),
        name="in_proj",
    )(x2, norm_w, sc, sh, w_bf, head_w)


def _log_sigmoid(x):
    return jnp.minimum(x, 0.0) - jnp.log(1.0 + jnp.exp(-jnp.abs(x)))


def _fox_cum_kernel(f_ref, b_ref, o_ref, *, blk):
    s = f_ref.shape[0]
    row = lax.broadcasted_iota(jnp.int32, (blk, blk), 0)
    col = lax.broadcasted_iota(jnp.int32, (blk, blk), 1)
    tri = (row >= col).astype(F32)
    carry = jnp.zeros((1, f_ref.shape[1]), F32)
    for i in range(s // blk):
        lf = _log_sigmoid(f_ref[i * blk:(i + 1) * blk, :] + b_ref[...])
        cs = _dot(tri, lf, HIGHEST) + carry
        o_ref[i * blk:(i + 1) * blk, :] = cs * LOG2_E
        carry = cs[blk - 1:blk, :]


def _fox_cum(proj, f_bias_row, batch, seq, col_block):
    blk = min(256, seq)
    return pl.pallas_call(
        functools.partial(_fox_cum_kernel, blk=blk),
        grid=(batch,),
        in_specs=[pl.BlockSpec((seq, LANES), lambda b: (b, col_block)),
                  pl.BlockSpec((1, LANES), lambda b: (0, 0))],
        out_specs=pl.BlockSpec((seq, LANES), lambda b: (b, 0)),
        out_shape=jax.ShapeDtypeStruct((batch * seq, LANES), F32),
        compiler_params=_cparams(("parallel",)),
        name="fox_cum",
    )(proj, f_bias_row)


def _fox_attn_kernel(qt_ref, kt_ref, q_ref, k_ref, v_ref, c_ref, o_ref, m_scr, l_scr, acc_scr):
    t = pl.program_id(2)
    qi = qt_ref[t]
    ki = kt_ref[t]
    hd = FOX_HEAD_DIM
    n_heads = q_ref.shape[1] // hd

    @pl.when(ki == 0)
    def _():
        m_scr[...] = jnp.full(m_scr.shape, NEG_BIG, F32)
        l_scr[...] = jnp.zeros(l_scr.shape, F32)
        acc_scr[...] = jnp.zeros(acc_scr.shape, F32)

    def step(masked):
        hs = range(n_heads)
        sl = [slice(h * hd, (h + 1) * hd) for h in hs]
        s = [_dot_nt(q_ref[:, sl[h]].astype(BF16), k_ref[:, sl[h]].astype(BF16)) - c_ref[0, h] for h in hs]
        if masked:
            row = lax.broadcasted_iota(jnp.int32, s[0].shape, 0)
            col = lax.broadcasted_iota(jnp.int32, s[0].shape, 1)
            s = [jnp.where(row >= col, x, NEG_BIG) for x in s]
        m_prev = [m_scr[h] for h in hs]
        m_new = [jnp.maximum(m_prev[h], jnp.max(s[h], axis=-1, keepdims=True)) for h in hs]
        n_rep = s[0].shape[1] // hd
        p = [jnp.exp2(s[h] - jnp.concatenate([m_new[h]] * n_rep, axis=1)) for h in hs]
        alpha = [jnp.exp2(m_prev[h] - m_new[h]) for h in hs]
        ones = jnp.ones((k_ref.shape[0], hd), BF16)
        pv = [_dot(p[h].astype(BF16), jnp.concatenate([v_ref[:, sl[h]].astype(BF16), ones], axis=1))
              for h in hs]
        for h in hs:
            l_scr[h] = alpha[h] * l_scr[h] + pv[h][:, hd:]
            acc_scr[:, sl[h]] = alpha[h] * acc_scr[:, sl[h]] + pv[h][:, :hd]
            m_scr[h] = m_new[h]

    @pl.when(ki < qi)
    def _():
        step(False)

    @pl.when(ki == qi)
    def _():
        step(True)
        for h in range(n_heads):
            sl = slice(h * hd, (h + 1) * hd)
            o_ref[:, sl] = acc_scr[:, sl] / l_scr[h]


def _fox_attn(proj, cum_rows, batch, seq):
    tq = min(512, seq)
    nq = seq // tq
    hps = 4
    w = hps * FOX_HEAD_DIM
    kcol = FOX_HEADS // hps
    tri = [(q, k) for q in range(nq) for k in range(q + 1)]
    qt = jnp.asarray([q for q, _ in tri], jnp.int32)
    kt = jnp.asarray([k for _, k in tri], jnp.int32)
    grid_spec = pltpu.PrefetchScalarGridSpec(
        num_scalar_prefetch=2,
        grid=(batch, kcol, len(tri)),
        in_specs=[
            pl.BlockSpec((tq, w), lambda b, h, t, qt, kt: (b * nq + qt[t], h)),
            pl.BlockSpec((tq, w), lambda b, h, t, qt, kt: (b * nq + kt[t], kcol + h)),
            pl.BlockSpec((tq, w), lambda b, h, t, qt, kt: (b * nq + kt[t], 2 * kcol + h)),
            pl.BlockSpec((1, hps, 1, tq), lambda b, h, t, qt, kt: (b, h, 0, kt[t])),
        ],
        out_specs=pl.BlockSpec((tq, w), lambda b, h, t, qt, kt: (b * nq + qt[t], h)),
        scratch_shapes=[pltpu.VMEM((hps, tq, FOX_HEAD_DIM), F32), pltpu.VMEM((hps, tq, FOX_HEAD_DIM), F32),
                        pltpu.VMEM((tq, w), F32)],
    )
    return pl.pallas_call(
        _fox_attn_kernel,
        grid_spec=grid_spec,
        out_shape=jax.ShapeDtypeStruct((batch * seq, FOX_HEADS * FOX_HEAD_DIM), F32),
        compiler_params=_cparams(("parallel", "parallel", "arbitrary")),
        name="fox_attn",
    )(qt, kt, proj, proj, proj, cum_rows)


def _head_sum(x, bd):
    hi, lo = _split_bf16(x)
    parts = [_dot(hi[:, j * LANES:(j + 1) * LANES], bd) + _dot(lo[:, j * LANES:(j + 1) * LANES], bd)
             for j in range(x.shape[1] // LANES)]
    return jnp.concatenate(parts, axis=1)


def _head_block_diag():
    r = lax.broadcasted_iota(jnp.int32, (LANES, LANES), 0) // RWKV_HEAD_DIM
    c = lax.broadcasted_iota(jnp.int32, (LANES, LANES), 1) // RWKV_HEAD_DIM
    return (r == c).astype(BF16)


def _rwkv_prep_kernel(r_ref, k_ref, v_ref, gd_ref, lo_ref, pr_ref, pk_ref, pv_ref, pgd_ref, plo_ref,
                      mu_r, mu_k, mu_v, mu_gd, mu_lo, w0_ref, wup_ref, a0_ref, aup_ref, gup_ref,
                      kk_ref, ka_ref, rk_ref,
                      or_ref, olw_ref, ok_ref, ov_ref, oa_ref, ob_ref, og_ref, obonus_ref, *, tiles_per_batch):
    i = pl.program_id(0)
    first = (i % tiles_per_batch) == 0

    def shifted(cur_ref, prv_ref, mu_ref):
        cur = cur_ref[...]
        last = jnp.where(first, 0.0, prv_ref[7:8, :])
        row = lax.broadcasted_iota(jnp.int32, cur.shape, 0)
        prev = jnp.where(row == 0, last, pltpu.roll(cur, 1, 0))
        return cur + (prev - cur) * mu_ref[...]

    r = shifted(r_ref, pr_ref, mu_r)
    k = shifted(k_ref, pk_ref, mu_k)
    v = shifted(v_ref, pv_ref, mu_v)
    gd = shifted(gd_ref, pgd_ref, mu_gd)
    lo = shifted(lo_ref, plo_ref, mu_lo)
    wd = lo[:, :LANES]
    ad = lo[:, LANES:]

    w_pre = w0_ref[...] + _dot_bf16x3(jnp.tanh(wd), wup_ref[...])
    w_raw = _log_sigmoid(w_pre) - 0.5
    log_decay = -jnp.exp(w_raw)
    a = jax.nn.sigmoid(a0_ref[...] + _dot_bf16x3(ad, aup_ref[...]))
    g = _dot_bf16x3(jax.nn.sigmoid(gd), gup_ref[...])

    bd = _head_block_diag()
    kk = k * kk_ref[...]
    nrm = jnp.maximum(jnp.sqrt(_head_sum(kk * kk, bd)), 1e-12)
    kk = kk / nrm
    k_mod = k * (1.0 + (a - 1.0) * ka_ref[...])
    bonus = _head_sum(r * k_mod * rk_ref[...], bd) * v

    or_ref[...] = r.astype(BF16)
    olw_ref[...] = log_decay
    ok_ref[...] = k_mod.astype(BF16)
    ov_ref[...] = v.astype(BF16)
    oa_ref[...] = (-kk).astype(BF16)
    ob_ref[...] = (kk * a).astype(BF16)
    og_ref[...] = g.astype(BF16)
    obonus_ref[...] = bonus.astype(BF16)


def _rwkv_prep(proj, cols, mus, w0, w_up, a0, a_up, g_up, k_k, k_a, r_k, tokens_per_batch):
    t = proj.shape[0]
    w = RWKV_HEADS * RWKV_HEAD_DIM
    tm = min(256, tokens_per_batch)
    tpb = tokens_per_batch // tm
    widths = [w, w, w, 256, 256]
    offs = [cols["rr"], cols["rk"], cols["rv"], cols["gd"], cols["lo"]]
    cur_specs = [pl.BlockSpec((tm, wd), functools.partial(lambda i, cb: (i, cb), cb=o // wd))
                 for wd, o in zip(widths, offs)]
    prv_specs = [pl.BlockSpec((8, wd), functools.partial(
        lambda i, cb: (jnp.maximum(i * (tm // 8) - 1, 0), cb), cb=o // wd))
        for wd, o in zip(widths, offs)]
    full = lambda a: pl.BlockSpec(a.shape, lambda i: (0,) * a.ndim)
    params = list(mus) + [w0, w_up, a0, a_up, g_up, k_k, k_a, r_k]
    out_spec = pl.BlockSpec((tm, w), lambda i: (i, 0))
    return pl.pallas_call(
        functools.partial(_rwkv_prep_kernel, tiles_per_batch=tpb),
        grid=(t // tm,),
        in_specs=cur_specs + prv_specs + [full(p) for p in params],
        out_specs=[out_spec] * 8,
        out_shape=[jax.ShapeDtypeStruct((t, w), dt) for dt in (BF16, F32, BF16, BF16, BF16, BF16, BF16, BF16)],
        compiler_params=_cparams(("parallel",)),
        name="rwkv_prep",
    )(*([proj] * 10), *params)


def _rwkv_scan_kernel(r_ref, lw_ref, k_ref, v_ref, a_ref, b_ref, o_ref, h_scr):
    c = pl.program_id(1)

    @pl.when(c == 0)
    def _():
        h_scr[...] = jnp.zeros(h_scr.shape, F32)

    C = r_ref.shape[0]
    n_pairs = r_ref.shape[1] // LANES
    P = HIGHEST

    row = lax.broadcasted_iota(jnp.int32, (C, C), 0)
    col = lax.broadcasted_iota(jnp.int32, (C, C), 1)
    tri = (row >= col).astype(F32)
    lw = lw_ref[...]
    cw = _dot(tri, lw, P)
    cw_end = cw[C - 1:C, :]
    e_pos = jnp.exp(cw)
    e_prev = jnp.exp(cw - lw)
    e_neg = jnp.exp(-cw)
    e_end = jnp.exp(cw_end - cw)
    w_end = jnp.exp(cw_end)

    a = a_ref[...].astype(F32)
    b = b_ref[...].astype(F32)
    k = k_ref[...].astype(F32)
    r = r_ref[...].astype(F32)
    at = a * e_prev
    bt = b * e_neg
    kt = k * e_neg
    rt = r * e_pos
    bh = b * e_end
    kh = k * e_end
    v = v_ref[...].astype(F32)

    lane = lax.broadcasted_iota(jnp.int32, (C, LANES), 1)
    head0 = lane < RWKV_HEAD_DIM
    r2 = lax.broadcasted_iota(jnp.int32, (2 * C, 2 * C), 0)
    c2 = lax.broadcasted_iota(jnp.int32, (2 * C, 2 * C), 1)
    same = (r2 // C) == (c2 // C)
    strict = same & (r2 > c2)
    incl = same & (r2 >= c2)
    eye = (r2 == c2).astype(F32)

    def two(x, p):
        xp = x[:, p * LANES:(p + 1) * LANES]
        return jnp.concatenate([jnp.where(head0, xp, 0.0), jnp.where(head0, 0.0, xp)], axis=0)

    def mm(x, y):
        return _dot(x.astype(BF16), y.astype(BF16))

    def split(x):
        hi = x.astype(BF16)
        return hi, (x - hi.astype(F32)).astype(BF16)

    def mm3(x, y):
        xh, xl = split(x)
        yh, yl = split(y)
        return _dot(xh, yh) + (_dot(xh, yl) + _dot(xl, yh))

    G = 2 * C
    pairs = range(n_pairs)
    at2 = [two(at, p) for p in pairs]
    rt2 = [two(rt, p) for p in pairs]
    v2 = [two(v, p).astype(BF16) for p in pairs]
    gram = [_dot_nt(jnp.concatenate([at2[p], rt2[p]], axis=0).astype(BF16),
                    jnp.concatenate([two(bt, p), two(kt, p)], axis=0).astype(BF16)) for p in pairs]
    a_ab = [jnp.where(strict, gram[p][:G, :G], 0.0) for p in pairs]
    nmat = list(a_ab)
    pw = list(a_ab)
    for _ in range(max(1, (C - 1).bit_length() - 1)):
        pw = [mm3(pw[p], pw[p]) for p in pairs]
        nmat = [nmat[p] + pw[p] + mm3(nmat[p], pw[p]) for p in pairs]
    akv = [mm(jnp.where(strict, gram[p][:G, G:], 0.0), v2[p]) for p in pairs]
    rhs = [jnp.concatenate([at2[p], akv[p]], axis=1) for p in pairs]
    pq = [(rhs[p] + mm(nmat[p], rhs[p])).astype(BF16) for p in pairs]
    ry = [mm(jnp.where(incl, gram[p][G:, :G], 0.0), pq[p]) for p in pairs]
    mv = [mm(jnp.where(incl, gram[p][G:, G:], 0.0), v2[p]) for p in pairs]
    ge = [_dot_tn(two(bh, p).astype(BF16), pq[p]) for p in pairs]
    kv = [_dot_tn(two(kh, p).astype(BF16), v2[p]) for p in pairs]
    for p in pairs:
        rr = rt2[p] + ry[p][:, :LANES]
        gm = eye * w_end[:, p * LANES:(p + 1) * LANES] + ge[p][:, :LANES]
        yh = mm(jnp.concatenate([rr, gm], axis=0), h_scr[p])
        y2 = yh[:G] + ry[p][:, LANES:] + mv[p]
        h_scr[p] = yh[G:] + ge[p][:, LANES:] + kv[p]
        o_ref[:, p * LANES:(p + 1) * LANES] = y2[:C, :] + y2[C:, :]


def _rwkv_scan(r, lw, k, v, a, b, batch, seq):
    w = r.shape[1]
    nc = seq // CHUNK
    spec = pl.BlockSpec((CHUNK, w), lambda bi, ci: (bi * nc + ci, 0))
    return pl.pallas_call(
        _rwkv_scan_kernel,
        grid=(batch, nc),
        in_specs=[spec] * 6,
        out_specs=spec,
        out_shape=jax.ShapeDtypeStruct((batch * seq, w), F32),
        scratch_shapes=[pltpu.VMEM((w // LANES, LANES, LANES), F32)],
        compiler_params=_cparams(("parallel", "arbitrary")),
        name="rwkv_scan",
    )(r, lw, k, v, a, b)


def _mix_out_kernel(o_ref, gate_ref, y_ref, bonus_ref, g_ref, x_ref, g1_ref, lnw_ref, lnb_ref, w_ref, out_ref):
    fox = o_ref[...] * jax.nn.sigmoid(gate_ref[...])
    bd = _head_block_diag()
    y = y_ref[...]
    inv_n = 1.0 / RWKV_HEAD_DIM
    mean = _head_sum(y, bd) * inv_n
    d = y - mean
    var = _head_sum(d * d, bd) * inv_n
    yn = d * lax.rsqrt(var + GN_EPS) * lnw_ref[...] + lnb_ref[...]
    rw = (yn + bonus_ref[...].astype(F32)) * g_ref[...].astype(F32)
    wf = fox.shape[1]
    mix = _dot(fox.astype(BF16), w_ref[:wf, :]) + _dot(rw.astype(BF16), w_ref[wf:, :])
    out_ref[...] = x_ref[...] + g1_ref[0] * mix


def _mix_out(o_fox, proj, gate_col, y, bonus, g, x2, g1, ln_w, ln_b, w_out_bf, tokens_per_batch):
    t, d = x2.shape
    wf = o_fox.shape[1]
    wr = y.shape[1]
    tm = min(256, tokens_per_batch)
    tpb = tokens_per_batch // tm
    return pl.pallas_call(
        _mix_out_kernel,
        grid=(t // tm,),
        in_specs=[pl.BlockSpec((tm, wf), lambda i: (i, 0)),
                  pl.BlockSpec((tm, wf), lambda i: (i, gate_col // wf)),
                  pl.BlockSpec((tm, wr), lambda i: (i, 0)),
                  pl.BlockSpec((tm, wr), lambda i: (i, 0)),
                  pl.BlockSpec((tm, wr), lambda i: (i, 0)),
                  pl.BlockSpec((tm, d), lambda i: (i, 0)),
                  pl.BlockSpec((1, 1, d), lambda i: (i // tpb, 0, 0)),
                  pl.BlockSpec((1, wr), lambda i: (0, 0)),
                  pl.BlockSpec((1, wr), lambda i: (0, 0)),
                  pl.BlockSpec(w_out_bf.shape, lambda i: (0, 0))],
        out_specs=pl.BlockSpec((tm, d), lambda i: (i, 0)),
        out_shape=jax.ShapeDtypeStruct((t, d), F32),
        compiler_params=_cparams(("parallel",)),
        name="mix_out",
    )(o_fox, proj, y, bonus, g, x2, g1, ln_w, ln_b, w_out_bf)


def _top_k_mask_rows(s, k, tie_safe):
    n = s.shape[0]
    row = lax.broadcasted_iota(jnp.int32, s.shape, 0).astype(F32)
    rank = jnp.full(s.shape, float(k), F32)
    vals = []
    for r in range(k):
        m = jnp.max(s, axis=0, keepdims=True)
        hit = s == m
        if tie_safe:
            pos = jnp.min(jnp.where(hit, row, float(n)), axis=0, keepdims=True)
            hit = row == pos
        vals.append(m)
        rank = jnp.where(hit, float(r), rank)
        s = jnp.where(hit, -jnp.inf, s)
    picked = jnp.sum(jnp.where(rank < float(k), 1.0, 0.0), axis=0, keepdims=True)
    return jnp.concatenate(vals, axis=0), rank, jnp.where(picked == float(k), 1.0, 0.0)


def _candidate_rows():
    k = PEER_TOPK
    groups, valid = [], []
    for r0 in range(k // 2):
        n1 = k // (r0 + 1)
        for g in range(-(-n1 // 8)):
            groups.append((r0, g * 8))
            valid.append([g * 8 + i < n1 for i in range(8)])
    groups.append((None, k // 2))
    valid.append([True] * 8)
    return groups, valid


def _route_head(sa, sb, tie_safe):
    k = PEER_TOPK
    tm = sa.shape[1]
    groups, valid = _candidate_rows()
    sub8 = lax.broadcasted_iota(jnp.int32, (8, tm), 0)
    top_a, rank_a, clean_a = _top_k_mask_rows(sa, k, tie_safe)
    top_b, rank_b, clean_b = _top_k_mask_rows(sb, k, tie_safe)
    parts = []
    for (r0, off), ok in zip(groups, valid):
        if r0 is None:
            part = top_a[off:off + 8, :] + top_b[0:1, :]
        else:
            part = top_a[r0:r0 + 1, :] + top_b[off:off + 8, :]
        if not all(ok):
            part = jnp.where(sub8 < sum(ok), part, -jnp.inf)
        parts.append(part)
    cand = jnp.concatenate(parts, axis=0)
    _, crank, clean_c = _top_k_mask_rows(cand, k, tie_safe)
    sel = crank < float(k)
    z = jnp.sum(jnp.where(sel, jnp.exp(cand - cand[0:1, :]), 0.0), axis=0, keepdims=True)
    self32 = jnp.where(sel, 1.0, 0.0)
    length = jnp.zeros_like(sa)
    tail = None
    for gi, (r0, off) in enumerate(groups):
        cnt = self32[gi * 8:(gi + 1) * 8, :]
        if r0 is None:
            tail = cnt
        else:
            length = length + jnp.where(rank_a == float(r0), jnp.sum(cnt, axis=0, keepdims=True), 0.0)
    for r in range(8):
        length = length + jnp.where(rank_a == float(k // 2 + r), tail[r:r + 1, :], 0.0)
    pa = jnp.exp(sa - top_a[0:1, :])
    qb = jnp.exp(sb - top_b[0:1, :]) / z
    return pa, length, qb, rank_b, clean_a * clean_b * clean_c


def _peer_route_kernel(x_ref, nw_ref, sc_ref, sh_ref, wq_ref, keys_ref,
                       h_ref, pa_ref, len_ref, qb_ref, rnk_ref):
    h = _modulated_norm(x_ref[...], nw_ref[...], sc_ref[0], sh_ref[0]).astype(BF16)
    h_ref[...] = h
    q = _dot(h, wq_ref[...])
    tm = q.shape[0]
    half = keys_ref.shape[-1]
    blocked = (PEER_NKEYS // PEER_SLABS, PEER_SLABS, tm)
    for hd in range(PEER_HEADS):
        qa = q[:, (hd * 2) * half:(hd * 2 + 1) * half]
        qb = q[:, (hd * 2 + 1) * half:(hd * 2 + 2) * half]
        sa = _dot_nt(keys_ref[hd, 0], qa, HIGHEST)
        sb = _dot_nt(keys_ref[hd, 1], qb, HIGHEST)

        def emit(tables):
            pa, length, qbt, rank_b, _ = tables
            pa_ref[:, hd] = pa.reshape(blocked)
            len_ref[:, hd] = length.reshape(blocked)
            qb_ref[hd] = qbt.astype(BF16)
            rnk_ref[hd] = rank_b.astype(BF16)

        fast = _route_head(sa, sb, tie_safe=False)
        emit(fast)
        all_clean = jnp.min(fast[4], axis=1, keepdims=True)[0, 0]

        @pl.when(all_clean < 0.5)
        def _():
            emit(_route_head(sa, sb, tie_safe=True))


def _peer_route(x1, norm_w, sc, sh, wq_bf, sub_keys, tokens_per_batch):
    t, d = x1.shape
    tm = min(128, tokens_per_batch)
    tpb = tokens_per_batch // tm
    n_blocks = PEER_NKEYS // PEER_SLABS
    dense = lambda dt: jax.ShapeDtypeStruct((PEER_HEADS, PEER_NKEYS, t), dt)
    dense_spec = pl.BlockSpec((PEER_HEADS, PEER_NKEYS, tm), lambda i: (0, 0, i))
    blocked = jax.ShapeDtypeStruct((n_blocks, PEER_HEADS, PEER_SLABS, t), F32)
    blocked_spec = pl.BlockSpec((n_blocks, PEER_HEADS, PEER_SLABS, tm), lambda i: (0, 0, 0, i))
    return pl.pallas_call(
        _peer_route_kernel,
        grid=(t // tm,),
        in_specs=[pl.BlockSpec((tm, d), lambda i: (i, 0)),
                  pl.BlockSpec((1, d), lambda i: (0, 0)),
                  pl.BlockSpec((1, 1, d), lambda i: (i // tpb, 0, 0)),
                  pl.BlockSpec((1, 1, d), lambda i: (i // tpb, 0, 0)),
                  pl.BlockSpec(wq_bf.shape, lambda i: (0, 0)),
                  pl.BlockSpec(sub_keys.shape, lambda i: (0, 0, 0, 0))],
        out_specs=[pl.BlockSpec((tm, d), lambda i: (i, 0)), blocked_spec, blocked_spec, dense_spec, dense_spec],
        out_shape=[jax.ShapeDtypeStruct((t, d), BF16), blocked, blocked, dense(BF16), dense(BF16)],
        compiler_params=_cparams(("parallel",)),
        name="peer_route",
    )(x1, norm_w, sc, sh, wq_bf, sub_keys)


def _gelu_exact(x):
    return 0.5 * x * (1.0 + lax.erf(x * 0.7071067811865476))


def _peer_expert_kernel(ht_ref, u_ref, vt_ref, pa_ref, len_ref, qb_ref, rnk_ref, x_ref, g2_ref, o_ref, acc_scr):
    j = pl.program_id(1)
    be = u_ref.shape[0]
    tm = ht_ref.shape[1]

    @pl.when(j == 0)
    def _():
        acc_scr[...] = jnp.zeros(acc_scr.shape, F32)

    n_slabs = be // PEER_NKEYS
    rows = 16
    n_groups = PEER_NKEYS // rows
    act = _gelu_exact(_dot(u_ref[...], ht_ref[...])).astype(BF16)

    def row_tile(ref, hd, s):
        return jnp.broadcast_to(ref[0, hd, s:s + 1, :], (rows, tm)).astype(BF16)

    zero = jnp.zeros((rows, tm), BF16)
    parts = [[None] * n_groups for _ in range(n_slabs)]
    slab_group = 4
    for s0 in range(0, n_slabs, slab_group):
        ss = range(s0, min(s0 + slab_group, n_slabs))
        ln = {(hd, s): row_tile(len_ref, hd, s) for hd in range(PEER_HEADS) for s in ss}
        pa = {(hd, s): row_tile(pa_ref, hd, s) for hd in range(PEER_HEADS) for s in ss}
        for g in range(n_groups):
            sl = slice(g * rows, (g + 1) * rows)
            acc = {}
            for hd in range(PEER_HEADS):
                rk = rnk_ref[hd, sl, :]
                qv = qb_ref[hd, sl, :]
                for s in ss:
                    term = jnp.where(rk < ln[hd, s], qv, zero) * pa[hd, s]
                    acc[s] = term if s not in acc else acc[s] + term
            for s in ss:
                lo = s * PEER_NKEYS + g * rows
                parts[s][g] = acc[s] * act[lo:lo + rows, :]
    p = jnp.concatenate([parts[s][g] for s in range(n_slabs) for g in range(n_groups)], axis=0)
    acc_scr[...] += _dot(vt_ref[0], p)

    @pl.when(j == pl.num_programs(1) - 1)
    def _():
        o_ref[...] = x_ref[...] + g2_ref[0] * acc_scr[...].T


def _peer_expert(ht_bf, u_bf, vt_bf, pa, ln, qb, rnk, x1, g2, tokens_per_batch):
    t, d = x1.shape
    ne = u_bf.shape[0]
    tm = min(512, tokens_per_batch)
    tpb = tokens_per_batch // tm
    be = PEER_EXPERT_BLOCK
    dense_spec = pl.BlockSpec((PEER_HEADS, PEER_NKEYS, tm), lambda i, j: (0, 0, i))
    blocked_spec = pl.BlockSpec((1, PEER_HEADS, PEER_SLABS, tm), lambda i, j: (j, 0, 0, i))
    return pl.pallas_call(
        _peer_expert_kernel,
        grid=(t // tm, ne // be),
        in_specs=[pl.BlockSpec((d, tm), lambda i, j: (0, i)),
                  pl.BlockSpec((be, d), lambda i, j: (j, 0)),
                  pl.BlockSpec((1, d, be), lambda i, j: (j, 0, 0)),
                  blocked_spec, blocked_spec, dense_spec, dense_spec,
                  pl.BlockSpec((tm, d), lambda i, j: (i, 0)),
                  pl.BlockSpec((1, 1, d), lambda i, j: (i // tpb, 0, 0))],
        out_specs=pl.BlockSpec((tm, d), lambda i, j: (i, 0)),
        out_shape=jax.ShapeDtypeStruct((t, d), F32),
        scratch_shapes=[pltpu.VMEM((d, tm), F32)],
        compiler_params=_cparams(("parallel", "arbitrary")),
        name="peer_expert",
    )(ht_bf, u_bf, vt_bf, pa, ln, qb, rnk, x1, g2)


def _pad_cols(a, n):
    return jnp.pad(a, ((0, 0), (0, n - a.shape[1])))


def _pad_rows(a, n):
    return jnp.pad(a, ((0, n - a.shape[0]), (0, 0)))


def _layer(x, c, w_ada, b_ada, norm_mix_w, w_in, fox_q_norm_w, fox_k_norm_w, fox_f_bias,
           rwkv_mu, rwkv_w0, rwkv_w_up, rwkv_a0, rwkv_a_up, rwkv_g_up, rwkv_k_k, rwkv_k_a,
           rwkv_r_k, rwkv_ln_w, rwkv_ln_b, w_out, norm_ffn_w, peer_w_query, peer_sub_keys,
           peer_u, peer_v):
    B, S, D = x.shape
    T = B * S
    fw = FOX_HEADS * FOX_HEAD_DIM
    rw = RWKV_HEADS * RWKV_HEAD_DIM
    w_lora = rwkv_w_up.shape[0]
    a_lora = rwkv_a_up.shape[0]
    g_lora = rwkv_g_up.shape[0]
    assert w_lora <= LANES and a_lora + FOX_HEADS <= LANES and g_lora == 256

    c_pad = _pad_rows(c, 8)
    mod = _ada_mod(c_pad, w_ada, b_ada)[:B]
    sh1, sc1, g1, sh2, sc2, g2 = [m.reshape(B, 1, D) for m in jnp.split(mod, 6, axis=-1)]

    fox_cols = 4 * fw + FOX_HEADS
    wi_fox, wi_rw = w_in[:, :fox_cols], w_in[:, fox_cols:]
    mu = rwkv_mu.reshape(1, -1)
    seg = lambda a, lo, n: a[:, lo:lo + n]
    w_perm = jnp.concatenate([
        seg(wi_fox, 0, 4 * fw),
        seg(wi_rw, 0, 3 * rw),
        seg(wi_rw, 3 * rw + w_lora + a_lora, g_lora),
        _pad_cols(seg(wi_rw, 3 * rw, w_lora), LANES),
        _pad_cols(jnp.concatenate([seg(wi_rw, 3 * rw + w_lora, a_lora), seg(wi_fox, 4 * fw, FOX_HEADS)], 1), LANES),
    ], axis=1).astype(BF16)
    cols = {"gate": 3 * fw, "rr": 4 * fw, "rk": 4 * fw + rw, "rv": 4 * fw + 2 * rw,
            "gd": 4 * fw + 3 * rw, "lo": 4 * fw + 3 * rw + g_lora}
    f_lane = a_lora
    f_block = (cols["lo"] + LANES) // LANES
    mus = [seg(mu, 0, rw), seg(mu, rw, rw), seg(mu, 2 * rw, rw),
           seg(mu, 3 * rw + w_lora + a_lora, g_lora),
           jnp.concatenate([_pad_cols(seg(mu, 3 * rw, w_lora), LANES),
                            _pad_cols(seg(mu, 3 * rw + w_lora, a_lora), LANES)], 1)]
    scale = FOX_HEAD_DIM ** -0.5 * LOG2_E
    head_w = _pad_cols(jnp.concatenate([jnp.tile(fox_q_norm_w * scale, FOX_HEADS),
                                        jnp.tile(fox_k_norm_w, FOX_HEADS)]).reshape(1, -1), w_perm.shape[1])

    x2 = x.reshape(T, D)
    proj = _in_proj(x2, norm_mix_w.reshape(1, D), sc1, sh1, w_perm, head_w, S, 2 * fw)

    f_bias_row = jnp.zeros((1, LANES), F32).at[0, f_lane:f_lane + FOX_HEADS].set(fox_f_bias)
    cum = _fox_cum(proj, f_bias_row, B, S, f_block)
    cum_rows = cum.reshape(B, S, LANES)[:, :, f_lane:f_lane + FOX_HEADS].transpose(0, 2, 1).reshape(B, FOX_HEADS, 1, S)
    o_fox = _fox_attn(proj, cum_rows, B, S)

    row = lambda a: a.reshape(1, -1)
    r, lw, k, v, a_vec, b_vec, g, bonus = _rwkv_prep(
        proj, cols, mus, row(rwkv_w0), _pad_rows(rwkv_w_up, LANES), row(rwkv_a0), _pad_rows(rwkv_a_up, LANES),
        rwkv_g_up, row(rwkv_k_k), row(rwkv_k_a), row(rwkv_r_k), S)
    y = _rwkv_scan(r, lw, k, v, a_vec, b_vec, B, S)

    x1 = _mix_out(o_fox, proj, cols["gate"], y, bonus, g, x2, g1, row(rwkv_ln_w), row(rwkv_ln_b),
                  w_out.astype(BF16), S)

    h2, pa, ln, qb, rnk = _peer_route(x1, norm_ffn_w.reshape(1, D), sc2, sh2,
                                      peer_w_query.astype(BF16), peer_sub_keys, S)
    vt = peer_v.astype(BF16).reshape(-1, PEER_EXPERT_BLOCK, D).transpose(0, 2, 1)
    out = _peer_expert(h2.T, peer_u.astype(BF16), vt, pa, ln, qb, rnk, x1, g2, S)
    return out.reshape(B, S, D)


def kernel(x, c, w_ada, b_ada, norm_mix_w, w_in, fox_q_norm_w, fox_k_norm_w, fox_f_bias, rwkv_mu, rwkv_w0,
           rwkv_w_up, rwkv_a0, rwkv_a_up, rwkv_g_up, rwkv_k_k, rwkv_k_a, rwkv_r_k, rwkv_ln_w, rwkv_ln_b,
           w_out, norm_ffn_w, peer_w_query, peer_sub_keys, peer_u, peer_v):
    params = (w_ada, b_ada, norm_mix_w, w_in, fox_q_norm_w, fox_k_norm_w, fox_f_bias, rwkv_mu, rwkv_w0,
              rwkv_w_up, rwkv_a0, rwkv_a_up, rwkv_g_up, rwkv_k_k, rwkv_k_a, rwkv_r_k, rwkv_ln_w, rwkv_ln_b,
              w_out, norm_ffn_w, peer_w_query, peer_sub_keys, peer_u, peer_v)
    for l in range(w_ada.shape[0]):
        x = _layer(x, c, *[p[l] for p in params])
    return x
```

```python
import functools

import jax
import jax.numpy as jnp
from jax import lax
from jax.experimental import pallas as pl
from jax.experimental.pallas import tpu as pltpu

F32 = jnp.float32
BF16 = jnp.bfloat16
HIGHEST = lax.Precision.HIGHEST

LANES = 128
NORM_EPS = 1e-6
GN_EPS = 64e-5
CHUNK = 64
FOX_HEADS = 8
FOX_HEAD_DIM = 128
RWKV_HEADS = 16
RWKV_HEAD_DIM = 64
PEER_HEADS = 8
PEER_NKEYS = 128
PEER_TOPK = 16
PEER_EXPERT_BLOCK = 1024
PEER_SLABS = PEER_EXPERT_BLOCK // PEER_NKEYS
NEG_BIG = -1e30
LOG2_E = 1.4426950408889634
VMEM_LIMIT = 56 * 1024 * 1024


def _cparams(sem):
    return pltpu.CompilerParams(dimension_semantics=sem, vmem_limit_bytes=VMEM_LIMIT)


def _dot(a, b, precision=None):
    return jnp.dot(a, b, preferred_element_type=F32, precision=precision)


def _dot_nt(a, b, precision=None):
    return lax.dot_general(a, b, (((1,), (1,)), ((), ())), preferred_element_type=F32,
                           precision=precision)


def _split_bf16(x):
    hi = x.astype(BF16)
    return hi, (x - hi.astype(F32)).astype(BF16)


def _dot_bf16x3(a, b):
    ah, al = _split_bf16(a)
    bh, bl = _split_bf16(b)
    return _dot(ah, bh) + (_dot(ah, bl) + _dot(al, bh))


def _dot_tn(a, b, precision=None):
    return lax.dot_general(a, b, (((0,), (0,)), ((), ())), preferred_element_type=F32,
                           precision=precision)


def _ada_kernel(c_ref, w_ref, b_ref, o_ref):
    c = c_ref[...]
    s = c * jax.nn.sigmoid(c)
    o_ref[...] = _dot(s, w_ref[...], HIGHEST) + b_ref[...]


def _ada_mod(c_pad, w_ada, b_ada):
    rows, d = c_pad.shape
    n = w_ada.shape[1]
    bn = 1024
    return pl.pallas_call(
        _ada_kernel,
        grid=(n // bn,),
        in_specs=[pl.BlockSpec((rows, d), lambda j: (0, 0)),
                  pl.BlockSpec((d, bn), lambda j: (0, j)),
                  pl.BlockSpec((1, bn), lambda j: (0, j))],
        out_specs=pl.BlockSpec((rows, bn), lambda j: (0, j)),
        out_shape=jax.ShapeDtypeStruct((rows, n), F32),
        compiler_params=_cparams(("arbitrary",)),
        name="ada_mod",
    )(c_pad, w_ada, b_ada.reshape(1, n))


def _modulated_norm(x, nw, sc, sh):
    y = x * lax.rsqrt(jnp.mean(x * x, axis=-1, keepdims=True) + NORM_EPS)
    return y * nw * (1.0 + sc) + sh


def _in_proj_kernel(x_ref, nw_ref, sc_ref, sh_ref, w_ref, hw_ref, o_ref, h_scr, *, n_qk_blocks):
    j = pl.program_id(1)

    @pl.when(j == 0)
    def _():
        h = _modulated_norm(x_ref[...], nw_ref[...], sc_ref[0], sh_ref[0])
        h_scr[...] = h.astype(BF16)

    acc = _dot(h_scr[...], w_ref[...])

    @pl.when(j < n_qk_blocks)
    def _():
        for hh in range(acc.shape[1] // FOX_HEAD_DIM):
            sl = slice(hh * FOX_HEAD_DIM, (hh + 1) * FOX_HEAD_DIM)
            a = acc[:, sl]
            rs = lax.rsqrt(jnp.mean(a * a, axis=-1, keepdims=True) + NORM_EPS)
            o_ref[:, sl] = a * rs * hw_ref[:, sl]

    @pl.when(j >= n_qk_blocks)
    def _():
        o_ref[...] = acc


def _in_proj(x2, norm_w, sc, sh, w_bf, head_w, tokens_per_batch, n_qk_cols):
    t, d = x2.shape
    n = w_bf.shape[1]
    tm = min(1024, tokens_per_batch)
    bn = 512
    tpb = tokens_per_batch // tm
    kern = functools.partial(_in_proj_kernel, n_qk_blocks=n_qk_cols // bn)
    return pl.pallas_call(
        kern,
        grid=(t // tm, n // bn),
        in_specs=[pl.BlockSpec((tm, d), lambda i, j: (i, 0)),
                  pl.BlockSpec((1, d), lambda i, j: (0, 0)),
                  pl.BlockSpec((1, 1, d), lambda i, j: (i // tpb, 0, 0)),
                  pl.BlockSpec((1, 1, d), lambda i, j: (i // tpb, 0, 0)),
                  pl.BlockSpec((d, bn), lambda i, j: (0, j)),
                  pl.BlockSpec((1, bn), lambda i, j: (0, j))],
        out_specs=pl.BlockSpec((tm, bn), lambda i, j: (i, j)),
        out_shape=jax.ShapeDtypeStruct((t, n), F32),
        scratch_shapes=[pltpu.VMEM((tm, d), BF16)],
        compiler_params=_cparams(("parallel", "arbitrary")),
        name="in_proj",
    )(x2, norm_w, sc, sh, w_bf, head_w)


def _log_sigmoid(x):
    return jnp.minimum(x, 0.0) - jnp.log(1.0 + jnp.exp(-jnp.abs(x)))


def _fox_cum_kernel(f_ref, b_ref, o_ref, *, blk):
    s = f_ref.shape[0]
    row = lax.broadcasted_iota(jnp.int32, (blk, blk), 0)
    col = lax.broadcasted_iota(jnp.int32, (blk, blk), 1)
    tri = (row >= col).astype(F32)
    carry = jnp.zeros((1, f_ref.shape[1]), F32)
    for i in range(s // blk):
        lf = _log_sigmoid(f_ref[i * blk:(i + 1) * blk, :] + b_ref[...])
        cs = _dot(tri, lf, HIGHEST) + carry
        o_ref[i * blk:(i + 1) * blk, :] = cs * LOG2_E
        carry = cs[blk - 1:blk, :]


def _fox_cum(proj, f_bias_row, batch, seq, col_block):
    blk = min(256, seq)
    return pl.pallas_call(
        functools.partial(_fox_cum_kernel, blk=blk),
        grid=(batch,),
        in_specs=[pl.BlockSpec((seq, LANES), lambda b: (b, col_block)),
                  pl.BlockSpec((1, LANES), lambda b: (0, 0))],
        out_specs=pl.BlockSpec((seq, LANES), lambda b: (b, 0)),
        out_shape=jax.ShapeDtypeStruct((batch * seq, LANES), F32),
        compiler_params=_cparams(("parallel",)),
        name="fox_cum",
    )(proj, f_bias_row)


def _fox_attn_kernel(qt_ref, kt_ref, q_ref, k_ref, v_ref, c_ref, o_ref, m_scr, l_scr, acc_scr):
    t = pl.program_id(2)
    qi = qt_ref[t]
    ki = kt_ref[t]
    hd = FOX_HEAD_DIM
    n_heads = q_ref.shape[1] // hd

    @pl.when(ki == 0)
    def _():
        m_scr[...] = jnp.full(m_scr.shape, NEG_BIG, F32)
        l_scr[...] = jnp.zeros(l_scr.shape, F32)
        acc_scr[...] = jnp.zeros(acc_scr.shape, F32)

    def step(masked):
        hs = range(n_heads)
        sl = [slice(h * hd, (h + 1) * hd) for h in hs]
        s = [_dot_nt(q_ref[:, sl[h]].astype(BF16), k_ref[:, sl[h]].astype(BF16)) - c_ref[0, h] for h in hs]
        if masked:
            row = lax.broadcasted_iota(jnp.int32, s[0].shape, 0)
            col = lax.broadcasted_iota(jnp.int32, s[0].shape, 1)
            s = [jnp.where(row >= col, x, NEG_BIG) for x in s]
        m_prev = [m_scr[h] for h in hs]
        m_new = [jnp.maximum(m_prev[h], jnp.max(s[h], axis=-1, keepdims=True)) for h in hs]
        n_rep = s[0].shape[1] // hd
        p = [jnp.exp2(s[h] - jnp.concatenate([m_new[h]] * n_rep, axis=1)) for h in hs]
        alpha = [jnp.exp2(m_prev[h] - m_new[h]) for h in hs]
        ones = jnp.ones((k_ref.shape[0], hd), BF16)
        pv = [_dot(p[h].astype(BF16), jnp.concatenate([v_ref[:, sl[h]].astype(BF16), ones], axis=1))
              for h in hs]
        for h in hs:
            l_scr[h] = alpha[h] * l_scr[h] + pv[h][:, hd:]
            acc_scr[:, sl[h]] = alpha[h] * acc_scr[:, sl[h]] + pv[h][:, :hd]
            m_scr[h] = m_new[h]

    @pl.when(ki < qi)
    def _():
        step(False)

    @pl.when(ki == qi)
    def _():
        step(True)
        for h in range(n_heads):
            sl = slice(h * hd, (h + 1) * hd)
            o_ref[:, sl] = acc_scr[:, sl] / l_scr[h]


def _fox_attn(proj, cum_rows, batch, seq):
    tq = min(512, seq)
    nq = seq // tq
    hps = 4
    w = hps * FOX_HEAD_DIM
    kcol = FOX_HEADS // hps
    tri = [(q, k) for q in range(nq) for k in range(q + 1)]
    qt = jnp.asarray([q for q, _ in tri], jnp.int32)
    kt = jnp.asarray([k for _, k in tri], jnp.int32)
    grid_spec = pltpu.PrefetchScalarGridSpec(
        num_scalar_prefetch=2,
        grid=(batch, kcol, len(tri)),
        in_specs=[
            pl.BlockSpec((tq, w), lambda b, h, t, qt, kt: (b * nq + qt[t], h)),
            pl.BlockSpec((tq, w), lambda b, h, t, qt, kt: (b * nq + kt[t], kcol + h)),
            pl.BlockSpec((tq, w), lambda b, h, t, qt, kt: (b * nq + kt[t], 2 * kcol + h)),
            pl.BlockSpec((1, hps, 1, tq), lambda b, h, t, qt, kt: (b, h, 0, kt[t])),
        ],
        out_specs=pl.BlockSpec((tq, w), lambda b, h, t, qt, kt: (b * nq + qt[t], h)),
        scratch_shapes=[pltpu.VMEM((hps, tq, FOX_HEAD_DIM), F32), pltpu.VMEM((hps, tq, FOX_HEAD_DIM), F32),
                        pltpu.VMEM((tq, w), F32)],
    )
    return pl.pallas_call(
        _fox_attn_kernel,
        grid_spec=grid_spec,
        out_shape=jax.ShapeDtypeStruct((batch * seq, FOX_HEADS * FOX_HEAD_DIM), F32),
        compiler_params=_cparams(("parallel", "parallel", "arbitrary")),
        name="fox_attn",
    )(qt, kt, proj, proj, proj, cum_rows)


def _head_sum(x, bd):
    hi, lo = _split_bf16(x)
    parts = [_dot(hi[:, j * LANES:(j + 1) * LANES], bd) + _dot(lo[:, j * LANES:(j + 1) * LANES], bd)
             for j in range(x.shape[1] // LANES)]
    return jnp.concatenate(parts, axis=1)


def _head_block_diag():
    r = lax.broadcasted_iota(jnp.int32, (LANES, LANES), 0) // RWKV_HEAD_DIM
    c = lax.broadcasted_iota(jnp.int32, (LANES, LANES), 1) // RWKV_HEAD_DIM
    return (r == c).astype(BF16)


def _rwkv_prep_kernel(r_ref, k_ref, v_ref, gd_ref, lo_ref, pr_ref, pk_ref, pv_ref, pgd_ref, plo_ref,
                      mu_r, mu_k, mu_v, mu_gd, mu_lo, w0_ref, wup_ref, a0_ref, aup_ref, gup_ref,
                      kk_ref, ka_ref, rk_ref,
                      or_ref, olw_ref, ok_ref, ov_ref, oa_ref, ob_ref, og_ref, obonus_ref, *, tiles_per_batch):
    i = pl.program_id(0)
    first = (i % tiles_per_batch) == 0

    def shifted(cur_ref, prv_ref, mu_ref):
        cur = cur_ref[...]
        last = jnp.where(first, 0.0, prv_ref[7:8, :])
        row = lax.broadcasted_iota(jnp.int32, cur.shape, 0)
        prev = jnp.where(row == 0, last, pltpu.roll(cur, 1, 0))
        return cur + (prev - cur) * mu_ref[...]

    r = shifted(r_ref, pr_ref, mu_r)
    k = shifted(k_ref, pk_ref, mu_k)
    v = shifted(v_ref, pv_ref, mu_v)
    gd = shifted(gd_ref, pgd_ref, mu_gd)
    lo = shifted(lo_ref, plo_ref, mu_lo)
    wd = lo[:, :LANES]
    ad = lo[:, LANES:]

    w_pre = w0_ref[...] + _dot_bf16x3(jnp.tanh(wd), wup_ref[...])
    w_raw = _log_sigmoid(w_pre) - 0.5
    log_decay = -jnp.exp(w_raw)
    a = jax.nn.sigmoid(a0_ref[...] + _dot_bf16x3(ad, aup_ref[...]))
    g = _dot_bf16x3(jax.nn.sigmoid(gd), gup_ref[...])

    bd = _head_block_diag()
    kk = k * kk_ref[...]
    nrm = jnp.maximum(jnp.sqrt(_head_sum(kk * kk, bd)), 1e-12)
    kk = kk / nrm
    k_mod = k * (1.0 + (a - 1.0) * ka_ref[...])
    bonus = _head_sum(r * k_mod * rk_ref[...], bd) * v

    or_ref[...] = r.astype(BF16)
    olw_ref[...] = log_decay
    ok_ref[...] = k_mod.astype(BF16)
    ov_ref[...] = v.astype(BF16)
    oa_ref[...] = (-kk).astype(BF16)
    ob_ref[...] = (kk * a).astype(BF16)
    og_ref[...] = g.astype(BF16)
    obonus_ref[...] = bonus.astype(BF16)


def _rwkv_prep(proj, cols, mus, w0, w_up, a0, a_up, g_up, k_k, k_a, r_k, tokens_per_batch):
    t = proj.shape[0]
    w = RWKV_HEADS * RWKV_HEAD_DIM
    tm = min(256, tokens_per_batch)
    tpb = tokens_per_batch // tm
    widths = [w, w, w, 256, 256]
    offs = [cols["rr"], cols["rk"], cols["rv"], cols["gd"], cols["lo"]]
    cur_specs = [pl.BlockSpec((tm, wd), functools.partial(lambda i, cb: (i, cb), cb=o // wd))
                 for wd, o in zip(widths, offs)]
    prv_specs = [pl.BlockSpec((8, wd), functools.partial(
        lambda i, cb: (jnp.maximum(i * (tm // 8) - 1, 0), cb), cb=o // wd))
        for wd, o in zip(widths, offs)]
    full = lambda a: pl.BlockSpec(a.shape, lambda i: (0,) * a.ndim)
    params = list(mus) + [w0, w_up, a0, a_up, g_up, k_k, k_a, r_k]
    out_spec = pl.BlockSpec((tm, w), lambda i: (i, 0))
    return pl.pallas_call(
        functools.partial(_rwkv_prep_kernel, tiles_per_batch=tpb),
        grid=(t // tm,),
        in_specs=cur_specs + prv_specs + [full(p) for p in params],
        out_specs=[out_spec] * 8,
        out_shape=[jax.ShapeDtypeStruct((t, w), dt) for dt in (BF16, F32, BF16, BF16, BF16, BF16, BF16, BF16)],
        compiler_params=_cparams(("parallel",)),
        name="rwkv_prep",
    )(*([proj] * 10), *params)


def _rwkv_scan_kernel(r_ref, lw_ref, k_ref, v_ref, a_ref, b_ref, o_ref, h_scr):
    c = pl.program_id(1)

    @pl.when(c == 0)
    def _():
        h_scr[...] = jnp.zeros(h_scr.shape, F32)

    C = r_ref.shape[0]
    n_pairs = r_ref.shape[1] // LANES
    P = HIGHEST

    row = lax.broadcasted_iota(jnp.int32, (C, C), 0)
    col = lax.broadcasted_iota(jnp.int32, (C, C), 1)
    tri = (row >= col).astype(F32)
    lw = lw_ref[...]
    cw = _dot(tri, lw, P)
    cw_end = cw[C - 1:C, :]
    e_pos = jnp.exp(cw)
    e_prev = jnp.exp(cw - lw)
    e_neg = jnp.exp(-cw)
    e_end = jnp.exp(cw_end - cw)
    w_end = jnp.exp(cw_end)

    a = a_ref[...].astype(F32)
    b = b_ref[...].astype(F32)
    k = k_ref[...].astype(F32)
    r = r_ref[...].astype(F32)
    at = a * e_prev
    bt = b * e_neg
    kt = k * e_neg
    rt = r * e_pos
    bh = b * e_end
    kh = k * e_end
    v = v_ref[...].astype(F32)

    lane = lax.broadcasted_iota(jnp.int32, (C, LANES), 1)
    head0 = lane < RWKV_HEAD_DIM
    r2 = lax.broadcasted_iota(jnp.int32, (2 * C, 2 * C), 0)
    c2 = lax.broadcasted_iota(jnp.int32, (2 * C, 2 * C), 1)
    same = (r2 // C) == (c2 // C)
    strict = same & (r2 > c2)
    incl = same & (r2 >= c2)
    eye = (r2 == c2).astype(F32)

    def two(x, p):
        xp = x[:, p * LANES:(p + 1) * LANES]
        return jnp.concatenate([jnp.where(head0, xp, 0.0), jnp.where(head0, 0.0, xp)], axis=0)

    def mm(x, y):
        return _dot(x.astype(BF16), y.astype(BF16))

    def split(x):
        hi = x.astype(BF16)
        return hi, (x - hi.astype(F32)).astype(BF16)

    def mm3(x, y):
        xh, xl = split(x)
        yh, yl = split(y)
        return _dot(xh, yh) + (_dot(xh, yl) + _dot(xl, yh))

    G = 2 * C
    pairs = range(n_pairs)
    at2 = [two(at, p) for p in pairs]
    rt2 = [two(rt, p) for p in pairs]
    v2 = [two(v, p).astype(BF16) for p in pairs]
    gram = [_dot_nt(jnp.concatenate([at2[p], rt2[p]], axis=0).astype(BF16),
                    jnp.concatenate([two(bt, p), two(kt, p)], axis=0).astype(BF16)) for p in pairs]
    a_ab = [jnp.where(strict, gram[p][:G, :G], 0.0) for p in pairs]
    nmat = list(a_ab)
    pw = list(a_ab)
    for _ in range(max(1, (C - 1).bit_length() - 1)):
        pw = [mm3(pw[p], pw[p]) for p in pairs]
        nmat = [nmat[p] + pw[p] + mm3(nmat[p], pw[p]) for p in pairs]
    akv = [mm(jnp.where(strict, gram[p][:G, G:], 0.0), v2[p]) for p in pairs]
    rhs = [jnp.concatenate([at2[p], akv[p]], axis=1) for p in pairs]
    pq = [(rhs[p] + mm(nmat[p], rhs[p])).astype(BF16) for p in pairs]
    ry = [mm(jnp.where(incl, gram[p][G:, :G], 0.0), pq[p]) for p in pairs]
    mv = [mm(jnp.where(incl, gram[p][G:, G:], 0.0), v2[p]) for p in pairs]
    ge = [_dot_tn(two(bh, p).astype(BF16), pq[p]) for p in pairs]
    kv = [_dot_tn(two(kh, p).astype(BF16), v2[p]) for p in pairs]
    for p in pairs:
        rr = rt2[p] + ry[p][:, :LANES]
        gm = eye * w_end[:, p * LANES:(p + 1) * LANES] + ge[p][:, :LANES]
        yh = mm(jnp.concatenate([rr, gm], axis=0), h_scr[p])
        y2 = yh[:G] + ry[p][:, LANES:] + mv[p]
        h_scr[p] = yh[G:] + ge[p][:, LANES:] + kv[p]
        o_ref[:, p * LANES:(p + 1) * LANES] = y2[:C, :] + y2[C:, :]


def _rwkv_scan(r, lw, k, v, a, b, batch, seq):
    w = r.shape[1]
    nc = seq // CHUNK
    spec = pl.BlockSpec((CHUNK, w), lambda bi, ci: (bi * nc + ci, 0))
    return pl.pallas_call(
        _rwkv_scan_kernel,
        grid=(batch, nc),
        in_specs=[spec] * 6,
        out_specs=spec,
        out_shape=jax.ShapeDtypeStruct((batch * seq, w), F32),
        scratch_shapes=[pltpu.VMEM((w // LANES, LANES, LANES), F32)],
        compiler_params=_cparams(("parallel", "arbitrary")),
        name="rwkv_scan",
    )(r, lw, k, v, a, b)


def _mix_out_kernel(o_ref, gate_ref, y_ref, bonus_ref, g_ref, x_ref, g1_ref, lnw_ref, lnb_ref, w_ref, out_ref):
    fox = o_ref[...] * jax.nn.sigmoid(gate_ref[...])
    bd = _head_block_diag()
    y = y_ref[...]
    inv_n = 1.0 / RWKV_HEAD_DIM
    mean = _head_sum(y, bd) * inv_n
    d = y - mean
    var = _head_sum(d * d, bd) * inv_n
    yn = d * lax.rsqrt(var + GN_EPS) * lnw_ref[...] + lnb_ref[...]
    rw = (yn + bonus_ref[...].astype(F32)) * g_ref[...].astype(F32)
    wf = fox.shape[1]
    mix = _dot(fox.astype(BF16), w_ref[:wf, :]) + _dot(rw.astype(BF16), w_ref[wf:, :])
    out_ref[...] = x_ref[...] + g1_ref[0] * mix


def _mix_out(o_fox, proj, gate_col, y, bonus, g, x2, g1, ln_w, ln_b, w_out_bf, tokens_per_batch):
    t, d = x2.shape
    wf = o_fox.shape[1]
    wr = y.shape[1]
    tm = min(256, tokens_per_batch)
    tpb = tokens_per_batch // tm
    return pl.pallas_call(
        _mix_out_kernel,
        grid=(t // tm,),
        in_specs=[pl.BlockSpec((tm, wf), lambda i: (i, 0)),
                  pl.BlockSpec((tm, wf), lambda i: (i, gate_col // wf)),
                  pl.BlockSpec((tm, wr), lambda i: (i, 0)),
                  pl.BlockSpec((tm, wr), lambda i: (i, 0)),
                  pl.BlockSpec((tm, wr), lambda i: (i, 0)),
                  pl.BlockSpec((tm, d), lambda i: (i, 0)),
                  pl.BlockSpec((1, 1, d), lambda i: (i // tpb, 0, 0)),
                  pl.BlockSpec((1, wr), lambda i: (0, 0)),
                  pl.BlockSpec((1, wr), lambda i: (0, 0)),
                  pl.BlockSpec(w_out_bf.shape, lambda i: (0, 0))],
        out_specs=pl.BlockSpec((tm, d), lambda i: (i, 0)),
        out_shape=jax.ShapeDtypeStruct((t, d), F32),
        compiler_params=_cparams(("parallel",)),
        name="mix_out",
    )(o_fox, proj, y, bonus, g, x2, g1, ln_w, ln_b, w_out_bf)


def _top_k_mask_rows(s, k, tie_safe):
    n = s.shape[0]
    row = lax.broadcasted_iota(jnp.int32, s.shape, 0).astype(F32)
    rank = jnp.full(s.shape, float(k), F32)
    vals = []
    for r in range(k):
        m = jnp.max(s, axis=0, keepdims=True)
        hit = s == m
        if tie_safe:
            pos = jnp.min(jnp.where(hit, row, float(n)), axis=0, keepdims=True)
            hit = row == pos
        vals.append(m)
        rank = jnp.where(hit, float(r), rank)
        s = jnp.where(hit, -jnp.inf, s)
    picked = jnp.sum(jnp.where(rank < float(k), 1.0, 0.0), axis=0, keepdims=True)
    return jnp.concatenate(vals, axis=0), rank, jnp.where(picked == float(k), 1.0, 0.0)


def _candidate_rows():
    k = PEER_TOPK
    groups, valid = [], []
    for r0 in range(k // 2):
        n1 = k // (r0 + 1)
        for g in range(-(-n1 // 8)):
            groups.append((r0, g * 8))
            valid.append([g * 8 + i < n1 for i in range(8)])
    groups.append((None, k // 2))
    valid.append([True] * 8)
    return groups, valid


def _route_head(sa, sb, tie_safe):
    k = PEER_TOPK
    tm = sa.shape[1]
    groups, valid = _candidate_rows()
    sub8 = lax.broadcasted_iota(jnp.int32, (8, tm), 0)
    top_a, rank_a, clean_a = _top_k_mask_rows(sa, k, tie_safe)
    top_b, rank_b, clean_b = _top_k_mask_rows(sb, k, tie_safe)
    parts = []
    for (r0, off), ok in zip(groups, valid):
        if r0 is None:
            part = top_a[off:off + 8, :] + top_b[0:1, :]
        else:
            part = top_a[r0:r0 + 1, :] + top_b[off:off + 8, :]
        if not all(ok):
            part = jnp.where(sub8 < sum(ok), part, -jnp.inf)
        parts.append(part)
    cand = jnp.concatenate(parts, axis=0)
    _, crank, clean_c = _top_k_mask_rows(cand, k, tie_safe)
    sel = crank < float(k)
    z = jnp.sum(jnp.where(sel, jnp.exp(cand - cand[0:1, :]), 0.0), axis=0, keepdims=True)
    self32 = jnp.where(sel, 1.0, 0.0)
    length = jnp.zeros_like(sa)
    tail = None
    for gi, (r0, off) in enumerate(groups):
        cnt = self32[gi * 8:(gi + 1) * 8, :]
        if r0 is None:
            tail = cnt
        else:
            length = length + jnp.where(rank_a == float(r0), jnp.sum(cnt, axis=0, keepdims=True), 0.0)
    for r in range(8):
        length = length + jnp.where(rank_a == float(k // 2 + r), tail[r:r + 1, :], 0.0)
    pa = jnp.exp(sa - top_a[0:1, :])
    qb = jnp.exp(sb - top_b[0:1, :]) / z
    return pa, length, qb, rank_b, clean_a * clean_b * clean_c


def _peer_route_kernel(x_ref, nw_ref, sc_ref, sh_ref, wq_ref, keys_ref,
                       h_ref, pa_ref, len_ref, qb_ref, rnk_ref):
    h = _modulated_norm(x_ref[...], nw_ref[...], sc_ref[0], sh_ref[0]).astype(BF16)
    h_ref[...] = h
    q = _dot(h, wq_ref[...])
    tm = q.shape[0]
    half = keys_ref.shape[-1]
    blocked = (PEER_NKEYS // PEER_SLABS, PEER_SLABS, tm)
    def scores(hd):
        qa = q[:, (hd * 2) * half:(hd * 2 + 1) * half]
        qb = q[:, (hd * 2 + 1) * half:(hd * 2 + 2) * half]
        return (_dot_nt(keys_ref[hd, 0], qa, HIGHEST),
                _dot_nt(keys_ref[hd, 1], qb, HIGHEST))

    def emit(hd, tables):
        pa, length, qbt, rank_b, _ = tables
        pa_ref[:, hd] = pa.reshape(blocked)
        len_ref[:, hd] = length.reshape(blocked)
        qb_ref[hd] = qbt.astype(BF16)
        rnk_ref[hd] = rank_b.astype(BF16)

    code = jnp.zeros((1, 1), F32)
    for hd in range(PEER_HEADS):
        fast = _route_head(*scores(hd), tie_safe=False)
        emit(hd, fast)
        dirty = 1.0 - jnp.min(fast[4], axis=1, keepdims=True)
        code = code + dirty * float(1 << hd)
    tie_bits = code[0, 0].astype(jnp.int32)
    for hd in range(PEER_HEADS):
        @pl.when(((tie_bits >> hd) & 1) == 1)
        def _(hd=hd):
            emit(hd, _route_head(*scores(hd), tie_safe=True))


def _peer_route(x1, norm_w, sc, sh, wq_bf, sub_keys, tokens_per_batch):
    t, d = x1.shape
    tm = min(256, tokens_per_batch)
    tpb = tokens_per_batch // tm
    n_blocks = PEER_NKEYS // PEER_SLABS
    dense = lambda dt: jax.ShapeDtypeStruct((PEER_HEADS, PEER_NKEYS, t), dt)
    dense_spec = pl.BlockSpec((PEER_HEADS, PEER_NKEYS, tm), lambda i: (0, 0, i))
    blocked = jax.ShapeDtypeStruct((n_blocks, PEER_HEADS, PEER_SLABS, t), F32)
    blocked_spec = pl.BlockSpec((n_blocks, PEER_HEADS, PEER_SLABS, tm), lambda i: (0, 0, 0, i))
    return pl.pallas_call(
        _peer_route_kernel,
        grid=(t // tm,),
        in_specs=[pl.BlockSpec((tm, d), lambda i: (i, 0)),
                  pl.BlockSpec((1, d), lambda i: (0, 0)),
                  pl.BlockSpec((1, 1, d), lambda i: (i // tpb, 0, 0)),
                  pl.BlockSpec((1, 1, d), lambda i: (i // tpb, 0, 0)),
                  pl.BlockSpec(wq_bf.shape, lambda i: (0, 0)),
                  pl.BlockSpec(sub_keys.shape, lambda i: (0, 0, 0, 0))],
        out_specs=[pl.BlockSpec((tm, d), lambda i: (i, 0)), blocked_spec, blocked_spec, dense_spec, dense_spec],
        out_shape=[jax.ShapeDtypeStruct((t, d), BF16), blocked, blocked, dense(BF16), dense(BF16)],
        compiler_params=_cparams(("parallel",)),
        name="peer_route",
    )(x1, norm_w, sc, sh, wq_bf, sub_keys)


def _gelu_exact(x):
    return 0.5 * x * (1.0 + lax.erf(x * 0.7071067811865476))


def _peer_expert_kernel(ht_ref, u_ref, vt_ref, pa_ref, len_ref, qb_ref, rnk_ref, x_ref, g2_ref, o_ref, acc_scr):
    j = pl.program_id(1)
    be = u_ref.shape[0]
    tm = ht_ref.shape[1]

    @pl.when(j == 0)
    def _():
        acc_scr[...] = jnp.zeros(acc_scr.shape, F32)

    n_slabs = be // PEER_NKEYS
    rows = 16
    n_groups = PEER_NKEYS // rows
    act = _gelu_exact(_dot(u_ref[...], ht_ref[...])).astype(BF16)

    def row_tile(ref, hd, s):
        return jnp.broadcast_to(ref[0, hd, s:s + 1, :], (rows, tm)).astype(BF16)

    zero = jnp.zeros((rows, tm), BF16)
    parts = [[None] * n_groups for _ in range(n_slabs)]
    slab_group = 4
    for s0 in range(0, n_slabs, slab_group):
        ss = range(s0, min(s0 + slab_group, n_slabs))
        ln = {(hd, s): row_tile(len_ref, hd, s) for hd in range(PEER_HEADS) for s in ss}
        pa = {(hd, s): row_tile(pa_ref, hd, s) for hd in range(PEER_HEADS) for s in ss}
        for g in range(n_groups):
            sl = slice(g * rows, (g + 1) * rows)
            acc = {}
            for hd in range(PEER_HEADS):
                rk = rnk_ref[hd, sl, :]
                qv = qb_ref[hd, sl, :]
                for s in ss:
                    term = jnp.where(rk < ln[hd, s], qv, zero) * pa[hd, s]
                    acc[s] = term if s not in acc else acc[s] + term
            for s in ss:
                lo = s * PEER_NKEYS + g * rows
                parts[s][g] = acc[s] * act[lo:lo + rows, :]
    p = jnp.concatenate([parts[s][g] for s in range(n_slabs) for g in range(n_groups)], axis=0)
    acc_scr[...] += _dot(vt_ref[0], p)

    @pl.when(j == pl.num_programs(1) - 1)
    def _():
        o_ref[...] = x_ref[...] + g2_ref[0] * acc_scr[...].T


def _peer_expert(ht_bf, u_bf, vt_bf, pa, ln, qb, rnk, x1, g2, tokens_per_batch):
    t, d = x1.shape
    ne = u_bf.shape[0]
    tm = min(512, tokens_per_batch)
    tpb = tokens_per_batch // tm
    be = PEER_EXPERT_BLOCK
    dense_spec = pl.BlockSpec((PEER_HEADS, PEER_NKEYS, tm), lambda i, j: (0, 0, i))
    blocked_spec = pl.BlockSpec((1, PEER_HEADS, PEER_SLABS, tm), lambda i, j: (j, 0, 0, i))
    return pl.pallas_call(
        _peer_expert_kernel,
        grid=(t // tm, ne // be),
        in_specs=[pl.BlockSpec((d, tm), lambda i, j: (0, i)),
                  pl.BlockSpec((be, d), lambda i, j: (j, 0)),
                  pl.BlockSpec((1, d, be), lambda i, j: (j, 0, 0)),
                  blocked_spec, blocked_spec, dense_spec, dense_spec,
                  pl.BlockSpec((tm, d), lambda i, j: (i, 0)),
                  pl.BlockSpec((1, 1, d), lambda i, j: (i // tpb, 0, 0))],
        out_specs=pl.BlockSpec((tm, d), lambda i, j: (i, 0)),
        out_shape=jax.ShapeDtypeStruct((t, d), F32),
        scratch_shapes=[pltpu.VMEM((d, tm), F32)],
        compiler_params=_cparams(("parallel", "arbitrary")),
        name="peer_expert",
    )(ht_bf, u_bf, vt_bf, pa, ln, qb, rnk, x1, g2)


def _pad_cols(a, n):
    return jnp.pad(a, ((0, 0), (0, n - a.shape[1])))


def _pad_rows(a, n):
    return jnp.pad(a, ((0, n - a.shape[0]), (0, 0)))


def _layer(x, c, w_ada, b_ada, norm_mix_w, w_in, fox_q_norm_w, fox_k_norm_w, fox_f_bias,
           rwkv_mu, rwkv_w0, rwkv_w_up, rwkv_a0, rwkv_a_up, rwkv_g_up, rwkv_k_k, rwkv_k_a,
           rwkv_r_k, rwkv_ln_w, rwkv_ln_b, w_out, norm_ffn_w, peer_w_query, peer_sub_keys,
           peer_u, peer_v):
    B, S, D = x.shape
    T = B * S
    fw = FOX_HEADS * FOX_HEAD_DIM
    rw = RWKV_HEADS * RWKV_HEAD_DIM
    w_lora = rwkv_w_up.shape[0]
    a_lora = rwkv_a_up.shape[0]
    g_lora = rwkv_g_up.shape[0]
    assert w_lora <= LANES and a_lora + FOX_HEADS <= LANES and g_lora == 256

    c_pad = _pad_rows(c, 8)
    mod = _ada_mod(c_pad, w_ada, b_ada)[:B]
    sh1, sc1, g1, sh2, sc2, g2 = [m.reshape(B, 1, D) for m in jnp.split(mod, 6, axis=-1)]

    fox_cols = 4 * fw + FOX_HEADS
    wi_fox, wi_rw = w_in[:, :fox_cols], w_in[:, fox_cols:]
    mu = rwkv_mu.reshape(1, -1)
    seg = lambda a, lo, n: a[:, lo:lo + n]
    w_perm = jnp.concatenate([
        seg(wi_fox, 0, 4 * fw),
        seg(wi_rw, 0, 3 * rw),
        seg(wi_rw, 3 * rw + w_lora + a_lora, g_lora),
        _pad_cols(seg(wi_rw, 3 * rw, w_lora), LANES),
        _pad_cols(jnp.concatenate([seg(wi_rw, 3 * rw + w_lora, a_lora), seg(wi_fox, 4 * fw, FOX_HEADS)], 1), LANES),
    ], axis=1).astype(BF16)
    cols = {"gate": 3 * fw, "rr": 4 * fw, "rk": 4 * fw + rw, "rv": 4 * fw + 2 * rw,
            "gd": 4 * fw + 3 * rw, "lo": 4 * fw + 3 * rw + g_lora}
    f_lane = a_lora
    f_block = (cols["lo"] + LANES) // LANES
    mus = [seg(mu, 0, rw), seg(mu, rw, rw), seg(mu, 2 * rw, rw),
           seg(mu, 3 * rw + w_lora + a_lora, g_lora),
           jnp.concatenate([_pad_cols(seg(mu, 3 * rw, w_lora), LANES),
                            _pad_cols(seg(mu, 3 * rw + w_lora, a_lora), LANES)], 1)]
    scale = FOX_HEAD_DIM ** -0.5 * LOG2_E
    head_w = _pad_cols(jnp.concatenate([jnp.tile(fox_q_norm_w * scale, FOX_HEADS),
                                        jnp.tile(fox_k_norm_w, FOX_HEADS)]).reshape(1, -1), w_perm.shape[1])

    x2 = x.reshape(T, D)
    proj = _in_proj(x2, norm_mix_w.reshape(1, D), sc1, sh1, w_perm, head_w, S, 2 * fw)

    f_bias_row = jnp.zeros((1, LANES), F32).at[0, f_lane:f_lane + FOX_HEADS].set(fox_f_bias)
    cum = _fox_cum(proj, f_bias_row, B, S, f_block)
    cum_rows = cum.reshape(B, S, LANES)[:, :, f_lane:f_lane + FOX_HEADS].transpose(0, 2, 1).reshape(B, FOX_HEADS, 1, S)
    o_fox = _fox_attn(proj, cum_rows, B, S)

    row = lambda a: a.reshape(1, -1)
    r, lw, k, v, a_vec, b_vec, g, bonus = _rwkv_prep(
        proj, cols, mus, row(rwkv_w0), _pad_rows(rwkv_w_up, LANES), row(rwkv_a0), _pad_rows(rwkv_a_up, LANES),
        rwkv_g_up, row(rwkv_k_k), row(rwkv_k_a), row(rwkv_r_k), S)
    y = _rwkv_scan(r, lw, k, v, a_vec, b_vec, B, S)

    x1 = _mix_out(o_fox, proj, cols["gate"], y, bonus, g, x2, g1, row(rwkv_ln_w), row(rwkv_ln_b),
                  w_out.astype(BF16), S)

    h2, pa, ln, qb, rnk = _peer_route(x1, norm_ffn_w.reshape(1, D), sc2, sh2,
                                      peer_w_query.astype(BF16), peer_sub_keys, S)
    vt = peer_v.astype(BF16).reshape(-1, PEER_EXPERT_BLOCK, D).transpose(0, 2, 1)
    out = _peer_expert(h2.T, peer_u.astype(BF16), vt, pa, ln, qb, rnk, x1, g2, S)
    return out.reshape(B, S, D)


def kernel(x, c, w_ada, b_ada, norm_mix_w, w_in, fox_q_norm_w, fox_k_norm_w, fox_f_bias, rwkv_mu, rwkv_w0,
           rwkv_w_up, rwkv_a0, rwkv_a_up, rwkv_g_up, rwkv_k_k, rwkv_k_a, rwkv_r_k, rwkv_ln_w, rwkv_ln_b,
           w_out, norm_ffn_w, peer_w_query, peer_sub_keys, peer_u, peer_v):
    params = (w_ada, b_ada, norm_mix_w, w_in, fox_q_norm_w, fox_k_norm_w, fox_f_bias, rwkv_mu, rwkv_w0,
              rwkv_w_up, rwkv_a0, rwkv_a_up, rwkv_g_up, rwkv_k_k, rwkv_k_a, rwkv_r_k, rwkv_ln_w, rwkv_ln_b,
              w_out, norm_ffn_w, peer_w_query, peer_sub_keys, peer_u, peer_v)
    for l in range(w_ada.shape[0]):
        x = _layer(x, c, *[p[l] for p in params])
    return x
```

```python
import functools

import jax
import jax.numpy as jnp
from jax import lax
from jax.experimental import pallas as pl
from jax.experimental.pallas import tpu as pltpu

F32 = jnp.float32
BF16 = jnp.bfloat16
HIGHEST = lax.Precision.HIGHEST

LANES = 128
BF16_ROWS = 16
NORM_EPS = 1e-6
GN_EPS = 64e-5
CHUNK = 64
FOX_HEADS = 8
FOX_HEAD_DIM = 128
RWKV_HEADS = 16
RWKV_HEAD_DIM = 64
PEER_HEADS = 8
PEER_NKEYS = 128
PEER_TOPK = 16
PEER_EXPERT_BLOCK = 1024
PEER_SLABS = PEER_EXPERT_BLOCK // PEER_NKEYS
NEG_BIG = -1e30
LOG2_E = 1.4426950408889634
VMEM_LIMIT = 56 * 1024 * 1024


def _cparams(sem):
    return pltpu.CompilerParams(dimension_semantics=sem, vmem_limit_bytes=VMEM_LIMIT)


def _dot(a, b, precision=None):
    return jnp.dot(a, b, preferred_element_type=F32, precision=precision)


def _dot_nt(a, b, precision=None):
    return lax.dot_general(a, b, (((1,), (1,)), ((), ())), preferred_element_type=F32,
                           precision=precision)


def _split_bf16(x):
    hi = x.astype(BF16)
    return hi, (x - hi.astype(F32)).astype(BF16)


def _dot_bf16x3(a, b):
    ah, al = _split_bf16(a)
    bh, bl = _split_bf16(b)
    return _dot(ah, bh) + (_dot(ah, bl) + _dot(al, bh))


def _dot_tn(a, b, precision=None):
    return lax.dot_general(a, b, (((0,), (0,)), ((), ())), preferred_element_type=F32,
                           precision=precision)


def _ada_kernel(c_ref, w_ref, b_ref, o_ref):
    c = c_ref[...]
    s = c * jax.nn.sigmoid(c)
    o_ref[...] = _dot(s, w_ref[...], HIGHEST) + b_ref[...]


def _ada_mod(c_pad, w_ada, b_ada):
    rows, d = c_pad.shape
    n = w_ada.shape[1]
    bn = 1024
    return pl.pallas_call(
        _ada_kernel,
        grid=(n // bn,),
        in_specs=[pl.BlockSpec((rows, d), lambda j: (0, 0)),
                  pl.BlockSpec((d, bn), lambda j: (0, j)),
                  pl.BlockSpec((1, bn), lambda j: (0, j))],
        out_specs=pl.BlockSpec((rows, bn), lambda j: (0, j)),
        out_shape=jax.ShapeDtypeStruct((rows, n), F32),
        compiler_params=_cparams(("arbitrary",)),
        name="ada_mod",
    )(c_pad, w_ada, b_ada.reshape(1, n))


def _modulated_norm(x, nw, sc, sh):
    y = x * lax.rsqrt(jnp.mean(x * x, axis=-1, keepdims=True) + NORM_EPS)
    return y * nw * (1.0 + sc) + sh


def _in_proj_kernel(x_ref, nw_ref, sc_ref, sh_ref, w_ref, hw_ref, o_ref, h_scr, *, n_qk_blocks):
    j = pl.program_id(1)

    @pl.when(j == 0)
    def _():
        h = _modulated_norm(x_ref[...], nw_ref[...], sc_ref[0], sh_ref[0])
        h_scr[...] = h.astype(BF16)

    acc = _dot(h_scr[...], w_ref[...])

    @pl.when(j < n_qk_blocks)
    def _():
        for hh in range(acc.shape[1] // FOX_HEAD_DIM):
            sl = slice(hh * FOX_HEAD_DIM, (hh + 1) * FOX_HEAD_DIM)
            a = acc[:, sl]
            rs = lax.rsqrt(jnp.mean(a * a, axis=-1, keepdims=True) + NORM_EPS)
            o_ref[:, sl] = a * rs * hw_ref[:, sl]

    @pl.when(j >= n_qk_blocks)
    def _():
        o_ref[...] = acc


def _in_proj(x2, norm_w, sc, sh, w_bf, head_w, tokens_per_batch, n_qk_cols):
    t, d = x2.shape
    n = w_bf.shape[1]
    tm = min(512, tokens_per_batch)
    bn = 512
    tpb = tokens_per_batch // tm
    kern = functools.partial(_in_proj_kernel, n_qk_blocks=n_qk_cols // bn)
    return pl.pallas_call(
        kern,
        grid=(t // tm, n // bn),
        in_specs=[pl.BlockSpec((tm, d), lambda i, j: (i, 0)),
                  pl.BlockSpec((1, d), lambda i, j: (0, 0)),
                  pl.BlockSpec((1, 1, d), lambda i, j: (i // tpb, 0, 0)),
                  pl.BlockSpec((1, 1, d), lambda i, j: (i // tpb, 0, 0)),
                  pl.BlockSpec((d, bn), lambda i, j: (0, j)),
                  pl.BlockSpec((1, bn), lambda i, j: (0, j))],
        out_specs=pl.BlockSpec((tm, bn), lambda i, j: (i, j)),
        out_shape=jax.ShapeDtypeStruct((t, n), F32),
        scratch_shapes=[pltpu.VMEM((tm, d), BF16)],
        compiler_params=_cparams(("parallel", "arbitrary")),
        name="in_proj",
    )(x2, norm_w, sc, sh, w_bf, head_w)


def _log_sigmoid(x):
    return jnp.minimum(x, 0.0) - jnp.log(1.0 + jnp.exp(-jnp.abs(x)))


def _fox_cum_kernel(f_ref, b_ref, o_ref, *, blk):
    s = f_ref.shape[0]
    row = lax.broadcasted_iota(jnp.int32, (blk, blk), 0)
    col = lax.broadcasted_iota(jnp.int32, (blk, blk), 1)
    tri = (row >= col).astype(F32)
    carry = jnp.zeros((1, f_ref.shape[1]), F32)
    for i in range(s // blk):
        lf = _log_sigmoid(f_ref[i * blk:(i + 1) * blk, :] + b_ref[...])
        cs = _dot(tri, lf, HIGHEST) + carry
        o_ref[i * blk:(i + 1) * blk, :] = cs * LOG2_E
        carry = cs[blk - 1:blk, :]


def _fox_cum(proj, f_bias_row, batch, seq, col_block):
    blk = min(256, seq)
    return pl.pallas_call(
        functools.partial(_fox_cum_kernel, blk=blk),
        grid=(batch,),
        in_specs=[pl.BlockSpec((seq, LANES), lambda b: (b, col_block)),
                  pl.BlockSpec((1, LANES), lambda b: (0, 0))],
        out_specs=pl.BlockSpec((seq, LANES), lambda b: (b, 0)),
        out_shape=jax.ShapeDtypeStruct((batch * seq, LANES), F32),
        compiler_params=_cparams(("parallel",)),
        name="fox_cum",
    )(proj, f_bias_row)


def _fox_attn_kernel(qt_ref, kt_ref, q_ref, k_ref, v_ref, c_ref, o_ref, m_scr, l_scr, acc_scr):
    t = pl.program_id(2)
    qi = qt_ref[t]
    ki = kt_ref[t]
    hd = FOX_HEAD_DIM
    n_heads = q_ref.shape[1] // hd

    @pl.when(ki == 0)
    def _():
        m_scr[...] = jnp.full(m_scr.shape, NEG_BIG, F32)
        l_scr[...] = jnp.zeros(l_scr.shape, F32)
        acc_scr[...] = jnp.zeros(acc_scr.shape, F32)

    def step(masked):
        hs = range(n_heads)
        sl = [slice(h * hd, (h + 1) * hd) for h in hs]
        s = [_dot_nt(q_ref[:, sl[h]].astype(BF16), k_ref[:, sl[h]].astype(BF16)) - c_ref[0, h] for h in hs]
        if masked:
            row = lax.broadcasted_iota(jnp.int32, s[0].shape, 0)
            col = lax.broadcasted_iota(jnp.int32, s[0].shape, 1)
            s = [jnp.where(row >= col, x, NEG_BIG) for x in s]
        m_prev = [m_scr[h] for h in hs]
        m_new = [jnp.maximum(m_prev[h], jnp.max(s[h], axis=-1, keepdims=True)) for h in hs]
        n_rep = s[0].shape[1] // hd
        p = [jnp.exp2(s[h] - jnp.concatenate([m_new[h]] * n_rep, axis=1)) for h in hs]
        alpha = [jnp.exp2(m_prev[h] - m_new[h]) for h in hs]
        ones = jnp.ones((k_ref.shape[0], hd), BF16)
        pv = [_dot(p[h].astype(BF16), jnp.concatenate([v_ref[:, sl[h]].astype(BF16), ones], axis=1))
              for h in hs]
        for h in hs:
            l_scr[h] = alpha[h] * l_scr[h] + pv[h][:, hd:]
            acc_scr[:, sl[h]] = alpha[h] * acc_scr[:, sl[h]] + pv[h][:, :hd]
            m_scr[h] = m_new[h]

    @pl.when(ki < qi)
    def _():
        step(False)

    @pl.when(ki == qi)
    def _():
        step(True)
        for h in range(n_heads):
            sl = slice(h * hd, (h + 1) * hd)
            o_ref[:, sl] = acc_scr[:, sl] / l_scr[h]


def _fox_attn(proj, cum_rows, batch, seq):
    tq = min(512, seq)
    nq = seq // tq
    hps = 4
    w = hps * FOX_HEAD_DIM
    kcol = FOX_HEADS // hps
    tri = [(q, k) for q in range(nq) for k in range(q + 1)]
    qt = jnp.asarray([q for q, _ in tri], jnp.int32)
    kt = jnp.asarray([k for _, k in tri], jnp.int32)
    grid_spec = pltpu.PrefetchScalarGridSpec(
        num_scalar_prefetch=2,
        grid=(batch, kcol, len(tri)),
        in_specs=[
            pl.BlockSpec((tq, w), lambda b, h, t, qt, kt: (b * nq + qt[t], h)),
            pl.BlockSpec((tq, w), lambda b, h, t, qt, kt: (b * nq + kt[t], kcol + h)),
            pl.BlockSpec((tq, w), lambda b, h, t, qt, kt: (b * nq + kt[t], 2 * kcol + h)),
            pl.BlockSpec((1, hps, 1, tq), lambda b, h, t, qt, kt: (b, h, 0, kt[t])),
        ],
        out_specs=pl.BlockSpec((tq, w), lambda b, h, t, qt, kt: (b * nq + qt[t], h)),
        scratch_shapes=[pltpu.VMEM((hps, tq, FOX_HEAD_DIM), F32), pltpu.VMEM((hps, tq, FOX_HEAD_DIM), F32),
                        pltpu.VMEM((tq, w), F32)],
    )
    return pl.pallas_call(
        _fox_attn_kernel,
        grid_spec=grid_spec,
        out_shape=jax.ShapeDtypeStruct((batch * seq, FOX_HEADS * FOX_HEAD_DIM), F32),
        compiler_params=_cparams(("parallel", "parallel", "arbitrary")),
        name="fox_attn",
    )(qt, kt, proj, proj, proj, cum_rows)


def _head_sum(x, bd):
    hi, lo = _split_bf16(x)
    parts = [_dot(hi[:, j * LANES:(j + 1) * LANES], bd) + _dot(lo[:, j * LANES:(j + 1) * LANES], bd)
             for j in range(x.shape[1] // LANES)]
    return jnp.concatenate(parts, axis=1)


def _head_block_diag():
    r = lax.broadcasted_iota(jnp.int32, (LANES, LANES), 0) // RWKV_HEAD_DIM
    c = lax.broadcasted_iota(jnp.int32, (LANES, LANES), 1) // RWKV_HEAD_DIM
    return (r == c).astype(BF16)


def _rwkv_prep_kernel(r_ref, k_ref, v_ref, gd_ref, lo_ref, pr_ref, pk_ref, pv_ref, pgd_ref, plo_ref,
                      mu_r, mu_k, mu_v, mu_gd, mu_lo, w0_ref, wup_ref, a0_ref, aup_ref, gup_ref,
                      kk_ref, ka_ref, rk_ref,
                      or_ref, olw_ref, ok_ref, ov_ref, oa_ref, ob_ref, og_ref, obonus_ref, *, tiles_per_batch):
    i = pl.program_id(0)
    first = (i % tiles_per_batch) == 0

    def shifted(cur_ref, prv_ref, mu_ref):
        cur = cur_ref[...]
        last = jnp.where(first, 0.0, prv_ref[7:8, :])
        row = lax.broadcasted_iota(jnp.int32, cur.shape, 0)
        prev = jnp.where(row == 0, last, pltpu.roll(cur, 1, 0))
        return cur + (prev - cur) * mu_ref[...]

    r = shifted(r_ref, pr_ref, mu_r)
    k = shifted(k_ref, pk_ref, mu_k)
    v = shifted(v_ref, pv_ref, mu_v)
    gd = shifted(gd_ref, pgd_ref, mu_gd)
    lo = shifted(lo_ref, plo_ref, mu_lo)
    wd = lo[:, :LANES]
    ad = lo[:, LANES:]

    w_pre = w0_ref[...] + _dot_bf16x3(jnp.tanh(wd), wup_ref[...])
    w_raw = _log_sigmoid(w_pre) - 0.5
    log_decay = -jnp.exp(w_raw)
    a = jax.nn.sigmoid(a0_ref[...] + _dot_bf16x3(ad, aup_ref[...]))
    g = _dot_bf16x3(jax.nn.sigmoid(gd), gup_ref[...])

    bd = _head_block_diag()
    kk = k * kk_ref[...]
    nrm = jnp.maximum(jnp.sqrt(_head_sum(kk * kk, bd)), 1e-12)
    kk = kk / nrm
    k_mod = k * (1.0 + (a - 1.0) * ka_ref[...])
    bonus = _head_sum(r * k_mod * rk_ref[...], bd) * v

    or_ref[...] = r.astype(BF16)
    olw_ref[...] = log_decay
    ok_ref[...] = k_mod.astype(BF16)
    ov_ref[...] = v.astype(BF16)
    oa_ref[...] = (-kk).astype(BF16)
    ob_ref[...] = (kk * a).astype(BF16)
    og_ref[...] = g.astype(BF16)
    obonus_ref[...] = bonus.astype(BF16)


def _rwkv_prep(proj, cols, mus, w0, w_up, a0, a_up, g_up, k_k, k_a, r_k, tokens_per_batch):
    t = proj.shape[0]
    w = RWKV_HEADS * RWKV_HEAD_DIM
    tm = min(256, tokens_per_batch)
    tpb = tokens_per_batch // tm
    widths = [w, w, w, 256, 256]
    offs = [cols["rr"], cols["rk"], cols["rv"], cols["gd"], cols["lo"]]
    cur_specs = [pl.BlockSpec((tm, wd), functools.partial(lambda i, cb: (i, cb), cb=o // wd))
                 for wd, o in zip(widths, offs)]
    prv_specs = [pl.BlockSpec((8, wd), functools.partial(
        lambda i, cb: (jnp.maximum(i * (tm // 8) - 1, 0), cb), cb=o // wd))
        for wd, o in zip(widths, offs)]
    full = lambda a: pl.BlockSpec(a.shape, lambda i: (0,) * a.ndim)
    params = list(mus) + [w0, w_up, a0, a_up, g_up, k_k, k_a, r_k]
    out_spec = pl.BlockSpec((tm, w), lambda i: (i, 0))
    return pl.pallas_call(
        functools.partial(_rwkv_prep_kernel, tiles_per_batch=tpb),
        grid=(t // tm,),
        in_specs=cur_specs + prv_specs + [full(p) for p in params],
        out_specs=[out_spec] * 8,
        out_shape=[jax.ShapeDtypeStruct((t, w), dt) for dt in (BF16, F32, BF16, BF16, BF16, BF16, BF16, BF16)],
        compiler_params=_cparams(("parallel",)),
        name="rwkv_prep",
    )(*([proj] * 10), *params)


def _rwkv_scan_kernel(r_ref, lw_ref, k_ref, v_ref, a_ref, b_ref, o_ref, h_scr):
    c = pl.program_id(1)

    @pl.when(c == 0)
    def _():
        h_scr[...] = jnp.zeros(h_scr.shape, F32)

    C = r_ref.shape[0]
    n_pairs = r_ref.shape[1] // LANES
    P = HIGHEST

    row = lax.broadcasted_iota(jnp.int32, (C, C), 0)
    col = lax.broadcasted_iota(jnp.int32, (C, C), 1)
    tri = (row >= col).astype(F32)
    lw = lw_ref[...]
    cw = _dot(tri, lw, P)
    cw_end = cw[C - 1:C, :]
    e_pos = jnp.exp(cw)
    e_prev = jnp.exp(cw - lw)
    e_neg = jnp.exp(-cw)
    e_end = jnp.exp(cw_end - cw)
    w_end = jnp.exp(cw_end)

    a = a_ref[...].astype(F32)
    b = b_ref[...].astype(F32)
    k = k_ref[...].astype(F32)
    r = r_ref[...].astype(F32)
    at = a * e_prev
    bt = b * e_neg
    kt = k * e_neg
    rt = r * e_pos
    bh = b * e_end
    kh = k * e_end
    v = v_ref[...].astype(F32)

    lane = lax.broadcasted_iota(jnp.int32, (C, LANES), 1)
    head0 = lane < RWKV_HEAD_DIM
    r2 = lax.broadcasted_iota(jnp.int32, (2 * C, 2 * C), 0)
    c2 = lax.broadcasted_iota(jnp.int32, (2 * C, 2 * C), 1)
    same = (r2 // C) == (c2 // C)
    strict = same & (r2 > c2)
    incl = same & (r2 >= c2)
    eye = (r2 == c2).astype(F32)

    def two(x, p):
        xp = x[:, p * LANES:(p + 1) * LANES]
        return jnp.concatenate([jnp.where(head0, xp, 0.0), jnp.where(head0, 0.0, xp)], axis=0)

    def mm(x, y):
        return _dot(x.astype(BF16), y.astype(BF16))

    def split(x):
        hi = x.astype(BF16)
        return hi, (x - hi.astype(F32)).astype(BF16)

    def mm3(x, y):
        xh, xl = split(x)
        yh, yl = split(y)
        return _dot(xh, yh) + (_dot(xh, yl) + _dot(xl, yh))

    G = 2 * C
    pairs = range(n_pairs)
    at2 = [two(at, p) for p in pairs]
    rt2 = [two(rt, p) for p in pairs]
    v2 = [two(v, p).astype(BF16) for p in pairs]
    gram = [_dot_nt(jnp.concatenate([at2[p], rt2[p]], axis=0).astype(BF16),
                    jnp.concatenate([two(bt, p), two(kt, p)], axis=0).astype(BF16)) for p in pairs]
    a_ab = [jnp.where(strict, gram[p][:G, :G], 0.0) for p in pairs]
    nmat = list(a_ab)
    pw = list(a_ab)
    for _ in range(max(1, (C - 1).bit_length() - 1)):
        pw = [mm3(pw[p], pw[p]) for p in pairs]
        nmat = [nmat[p] + pw[p] + mm3(nmat[p], pw[p]) for p in pairs]
    akv = [mm(jnp.where(strict, gram[p][:G, G:], 0.0), v2[p]) for p in pairs]
    rhs = [jnp.concatenate([at2[p], akv[p]], axis=1) for p in pairs]
    pq = [(rhs[p] + mm(nmat[p], rhs[p])).astype(BF16) for p in pairs]
    ry = [mm(jnp.where(incl, gram[p][G:, :G], 0.0), pq[p]) for p in pairs]
    mv = [mm(jnp.where(incl, gram[p][G:, G:], 0.0), v2[p]) for p in pairs]
    ge = [_dot_tn(two(bh, p).astype(BF16), pq[p]) for p in pairs]
    kv = [_dot_tn(two(kh, p).astype(BF16), v2[p]) for p in pairs]
    for p in pairs:
        rr = rt2[p] + ry[p][:, :LANES]
        gm = eye * w_end[:, p * LANES:(p + 1) * LANES] + ge[p][:, :LANES]
        yh = mm(jnp.concatenate([rr, gm], axis=0), h_scr[p])
        y2 = yh[:G] + ry[p][:, LANES:] + mv[p]
        h_scr[p] = yh[G:] + ge[p][:, LANES:] + kv[p]
        o_ref[:, p * LANES:(p + 1) * LANES] = y2[:C, :] + y2[C:, :]


def _rwkv_scan(r, lw, k, v, a, b, batch, seq):
    w = r.shape[1]
    nc = seq // CHUNK
    spec = pl.BlockSpec((CHUNK, w), lambda bi, ci: (bi * nc + ci, 0))
    return pl.pallas_call(
        _rwkv_scan_kernel,
        grid=(batch, nc),
        in_specs=[spec] * 6,
        out_specs=spec,
        out_shape=jax.ShapeDtypeStruct((batch * seq, w), F32),
        scratch_shapes=[pltpu.VMEM((w // LANES, LANES, LANES), F32)],
        compiler_params=_cparams(("parallel", "arbitrary")),
        name="rwkv_scan",
    )(r, lw, k, v, a, b)


def _mix_out_kernel(o_ref, gate_ref, y_ref, bonus_ref, g_ref, x_ref, g1_ref, lnw_ref, lnb_ref, w_ref, out_ref):
    fox = o_ref[...] * jax.nn.sigmoid(gate_ref[...])
    bd = _head_block_diag()
    y = y_ref[...]
    inv_n = 1.0 / RWKV_HEAD_DIM
    mean = _head_sum(y, bd) * inv_n
    d = y - mean
    var = _head_sum(d * d, bd) * inv_n
    yn = d * lax.rsqrt(var + GN_EPS) * lnw_ref[...] + lnb_ref[...]
    rw = (yn + bonus_ref[...].astype(F32)) * g_ref[...].astype(F32)
    wf = fox.shape[1]
    mix = _dot(fox.astype(BF16), w_ref[:wf, :]) + _dot(rw.astype(BF16), w_ref[wf:, :])
    out_ref[...] = x_ref[...] + g1_ref[0] * mix


def _mix_out(o_fox, proj, gate_col, y, bonus, g, x2, g1, ln_w, ln_b, w_out_bf, tokens_per_batch):
    t, d = x2.shape
    wf = o_fox.shape[1]
    wr = y.shape[1]
    tm = min(256, tokens_per_batch)
    tpb = tokens_per_batch // tm
    return pl.pallas_call(
        _mix_out_kernel,
        grid=(t // tm,),
        in_specs=[pl.BlockSpec((tm, wf), lambda i: (i, 0)),
                  pl.BlockSpec((tm, wf), lambda i: (i, gate_col // wf)),
                  pl.BlockSpec((tm, wr), lambda i: (i, 0)),
                  pl.BlockSpec((tm, wr), lambda i: (i, 0)),
                  pl.BlockSpec((tm, wr), lambda i: (i, 0)),
                  pl.BlockSpec((tm, d), lambda i: (i, 0)),
                  pl.BlockSpec((1, 1, d), lambda i: (i // tpb, 0, 0)),
                  pl.BlockSpec((1, wr), lambda i: (0, 0)),
                  pl.BlockSpec((1, wr), lambda i: (0, 0)),
                  pl.BlockSpec(w_out_bf.shape, lambda i: (0, 0))],
        out_specs=pl.BlockSpec((tm, d), lambda i: (i, 0)),
        out_shape=jax.ShapeDtypeStruct((t, d), F32),
        compiler_params=_cparams(("parallel",)),
        name="mix_out",
    )(o_fox, proj, y, bonus, g, x2, g1, ln_w, ln_b, w_out_bf)


def _top_k_mask_rows(s, k, tie_safe, want_rank=True):
    n = s.shape[0]
    s0 = s
    row = lax.broadcasted_iota(jnp.int32, s.shape, 0).astype(F32)
    rank = jnp.full(s.shape, float(k), F32) if want_rank else None
    vals = []
    for r in range(k):
        m = jnp.max(s, axis=0, keepdims=True)
        hit = s == m
        if tie_safe:
            pos = jnp.min(jnp.where(hit, row, float(n)), axis=0, keepdims=True)
            hit = row == pos
        vals.append(m)
        if want_rank:
            rank = jnp.where(hit, float(r), rank)
        s = jnp.where(hit, -jnp.inf, s)
    picked = (s == -jnp.inf) & (s0 != -jnp.inf)
    n_picked = jnp.sum(jnp.where(picked, 1.0, 0.0), axis=0, keepdims=True)
    return jnp.concatenate(vals, axis=0), rank, picked, jnp.where(n_picked == float(k), 1.0, 0.0)


def _candidate_rows():
    k = PEER_TOPK
    groups, valid = [], []
    for r0 in range(k // 2):
        n1 = k // (r0 + 1)
        for g in range(-(-n1 // 8)):
            groups.append((r0, g * 8))
            valid.append([g * 8 + i < n1 for i in range(8)])
    groups.append((None, k // 2))
    valid.append([True] * 8)
    return groups, valid


def _route_head(sa, sb, tie_safe):
    k = PEER_TOPK
    tm = sa.shape[1]
    groups, valid = _candidate_rows()
    sub8 = lax.broadcasted_iota(jnp.int32, (8, tm), 0)
    top_a, rank_a, _, clean_a = _top_k_mask_rows(sa, k, tie_safe)
    top_b, rank_b, _, clean_b = _top_k_mask_rows(sb, k, tie_safe)
    parts = []
    for (r0, off), ok in zip(groups, valid):
        if r0 is None:
            part = top_a[off:off + 8, :] + top_b[0:1, :]
        else:
            part = top_a[r0:r0 + 1, :] + top_b[off:off + 8, :]
        if not all(ok):
            part = jnp.where(sub8 < sum(ok), part, -jnp.inf)
        parts.append(part)
    cand = jnp.concatenate(parts, axis=0)
    _, _, sel, clean_c = _top_k_mask_rows(cand, k, tie_safe, want_rank=False)
    z = jnp.sum(jnp.where(sel, jnp.exp(cand - cand[0:1, :]), 0.0), axis=0, keepdims=True)
    self32 = jnp.where(sel, 1.0, 0.0)
    n_keys = sa.shape[0]
    rank3 = rank_a.astype(BF16).reshape(n_keys // BF16_ROWS, BF16_ROWS, tm)
    length3 = jnp.zeros(rank3.shape, BF16)

    def add_count(r0, count_row):
        tile = jnp.broadcast_to(count_row, (BF16_ROWS, tm)).astype(BF16)
        return length3 + jnp.where(rank3 == float(r0), tile[None], 0.0).astype(BF16)

    tail = None
    per_rank = {}
    for gi, (r0, off) in enumerate(groups):
        cnt = self32[gi * 8:(gi + 1) * 8, :]
        if r0 is None:
            tail = cnt
        else:
            c = jnp.sum(cnt, axis=0, keepdims=True)
            per_rank[r0] = c if r0 not in per_rank else per_rank[r0] + c
    for r in range(8):
        per_rank[k // 2 + r] = tail[r:r + 1, :]
    for r0, c in per_rank.items():
        length3 = add_count(r0, c)
    length = length3.reshape(n_keys, tm).astype(F32)
    pa = jnp.exp(sa - top_a[0:1, :])
    qb = jnp.exp(sb - top_b[0:1, :]) / z
    return pa, length, qb, rank_b, clean_a * clean_b * clean_c


def _peer_route_kernel(x_ref, nw_ref, sc_ref, sh_ref, wq_ref, keys_ref,
                       h_ref, pa_ref, len_ref, qb_ref, rnk_ref):
    h = _modulated_norm(x_ref[...], nw_ref[...], sc_ref[0], sh_ref[0]).astype(BF16)
    h_ref[...] = h
    q = _dot(h, wq_ref[...])
    tm = q.shape[0]
    half = keys_ref.shape[-1]
    blocked = (PEER_NKEYS // PEER_SLABS, PEER_SLABS, tm)
    def scores(hd):
        qa = q[:, (hd * 2) * half:(hd * 2 + 1) * half]
        qb = q[:, (hd * 2 + 1) * half:(hd * 2 + 2) * half]
        return (_dot_nt(keys_ref[hd, 0], qa, HIGHEST),
                _dot_nt(keys_ref[hd, 1], qb, HIGHEST))

    def emit(hd, tables):
        pa, length, qbt, rank_b, _ = tables
        pa_ref[:, hd] = pa.reshape(blocked)
        len_ref[:, hd] = length.reshape(blocked)
        qb_ref[hd] = qbt.astype(BF16)
        rnk_ref[hd] = rank_b.astype(BF16)

    code = jnp.zeros((1, 1), F32)
    for hd in range(PEER_HEADS):
        fast = _route_head(*scores(hd), tie_safe=False)
        emit(hd, fast)
        dirty = 1.0 - jnp.min(fast[4], axis=1, keepdims=True)
        code = code + dirty * float(1 << hd)
    tie_bits = code[0, 0].astype(jnp.int32)
    for hd in range(PEER_HEADS):
        @pl.when(((tie_bits >> hd) & 1) == 1)
        def _(hd=hd):
            emit(hd, _route_head(*scores(hd), tie_safe=True))


def _peer_route(x1, norm_w, sc, sh, wq_bf, sub_keys, tokens_per_batch):
    t, d = x1.shape
    tm = min(256, tokens_per_batch)
    tpb = tokens_per_batch // tm
    n_blocks = PEER_NKEYS // PEER_SLABS
    dense = lambda dt: jax.ShapeDtypeStruct((PEER_HEADS, PEER_NKEYS, t), dt)
    dense_spec = pl.BlockSpec((PEER_HEADS, PEER_NKEYS, tm), lambda i: (0, 0, i))
    blocked = jax.ShapeDtypeStruct((n_blocks, PEER_HEADS, PEER_SLABS, t), F32)
    blocked_spec = pl.BlockSpec((n_blocks, PEER_HEADS, PEER_SLABS, tm), lambda i: (0, 0, 0, i))
    return pl.pallas_call(
        _peer_route_kernel,
        grid=(t // tm,),
        in_specs=[pl.BlockSpec((tm, d), lambda i: (i, 0)),
                  pl.BlockSpec((1, d), lambda i: (0, 0)),
                  pl.BlockSpec((1, 1, d), lambda i: (i // tpb, 0, 0)),
                  pl.BlockSpec((1, 1, d), lambda i: (i // tpb, 0, 0)),
                  pl.BlockSpec(wq_bf.shape, lambda i: (0, 0)),
                  pl.BlockSpec(sub_keys.shape, lambda i: (0, 0, 0, 0))],
        out_specs=[pl.BlockSpec((tm, d), lambda i: (i, 0)), blocked_spec, blocked_spec, dense_spec, dense_spec],
        out_shape=[jax.ShapeDtypeStruct((t, d), BF16), blocked, blocked, dense(BF16), dense(BF16)],
        compiler_params=_cparams(("parallel",)),
        name="peer_route",
    )(x1, norm_w, sc, sh, wq_bf, sub_keys)


def _gelu_exact(x):
    return 0.5 * x * (1.0 + lax.erf(x * 0.7071067811865476))


def _peer_expert_kernel(ht_ref, u_ref, vt_ref, pa_ref, len_ref, qb_ref, rnk_ref, x_ref, g2_ref, o_ref, acc_scr):
    j = pl.program_id(1)
    be = u_ref.shape[0]
    tm = ht_ref.shape[1]

    @pl.when(j == 0)
    def _():
        acc_scr[...] = jnp.zeros(acc_scr.shape, F32)

    n_slabs = be // PEER_NKEYS
    rows = BF16_ROWS
    n_groups = PEER_NKEYS // rows
    act = _gelu_exact(_dot(u_ref[...], ht_ref[...])).astype(BF16)

    def row_tile(ref, hd, s):
        return jnp.broadcast_to(ref[0, hd, s:s + 1, :], (rows, tm)).astype(BF16)

    zero = jnp.zeros((rows, tm), BF16)
    parts = [[None] * n_groups for _ in range(n_slabs)]
    slab_group = 4
    for s0 in range(0, n_slabs, slab_group):
        ss = range(s0, min(s0 + slab_group, n_slabs))
        ln = {(hd, s): row_tile(len_ref, hd, s) for hd in range(PEER_HEADS) for s in ss}
        pa = {(hd, s): row_tile(pa_ref, hd, s) for hd in range(PEER_HEADS) for s in ss}
        for g in range(n_groups):
            sl = slice(g * rows, (g + 1) * rows)
            acc = {}
            for hd in range(PEER_HEADS):
                rk = rnk_ref[hd, sl, :]
                qv = qb_ref[hd, sl, :]
                for s in ss:
                    term = jnp.where(rk < ln[hd, s], qv, zero) * pa[hd, s]
                    acc[s] = term if s not in acc else acc[s] + term
            for s in ss:
                lo = s * PEER_NKEYS + g * rows
                parts[s][g] = acc[s] * act[lo:lo + rows, :]
    p = jnp.concatenate([parts[s][g] for s in range(n_slabs) for g in range(n_groups)], axis=0)
    acc_scr[...] += _dot(vt_ref[0], p)

    @pl.when(j == pl.num_programs(1) - 1)
    def _():
        o_ref[...] = x_ref[...] + g2_ref[0] * acc_scr[...].T


def _peer_expert(ht_bf, u_bf, vt_bf, pa, ln, qb, rnk, x1, g2, tokens_per_batch):
    t, d = x1.shape
    ne = u_bf.shape[0]
    tm = min(512, tokens_per_batch)
    tpb = tokens_per_batch // tm
    be = PEER_EXPERT_BLOCK
    dense_spec = pl.BlockSpec((PEER_HEADS, PEER_NKEYS, tm), lambda i, j: (0, 0, i))
    blocked_spec = pl.BlockSpec((1, PEER_HEADS, PEER_SLABS, tm), lambda i, j: (j, 0, 0, i))
    return pl.pallas_call(
        _peer_expert_kernel,
        grid=(t // tm, ne // be),
        in_specs=[pl.BlockSpec((d, tm), lambda i, j: (0, i)),
                  pl.BlockSpec((be, d), lambda i, j: (j, 0)),
                  pl.BlockSpec((1, d, be), lambda i, j: (j, 0, 0)),
                  blocked_spec, blocked_spec, dense_spec, dense_spec,
                  pl.BlockSpec((tm, d), lambda i, j: (i, 0)),
                  pl.BlockSpec((1, 1, d), lambda i, j: (i // tpb, 0, 0))],
        out_specs=pl.BlockSpec((tm, d), lambda i, j: (i, 0)),
        out_shape=jax.ShapeDtypeStruct((t, d), F32),
        scratch_shapes=[pltpu.VMEM((d, tm), F32)],
        compiler_params=_cparams(("parallel", "arbitrary")),
        name="peer_expert",
    )(ht_bf, u_bf, vt_bf, pa, ln, qb, rnk, x1, g2)


def _pad_cols(a, n):
    return jnp.pad(a, ((0, 0), (0, n - a.shape[1])))


def _pad_rows(a, n):
    return jnp.pad(a, ((0, n - a.shape[0]), (0, 0)))


def _layer(x, c, w_ada, b_ada, norm_mix_w, w_in, fox_q_norm_w, fox_k_norm_w, fox_f_bias,
           rwkv_mu, rwkv_w0, rwkv_w_up, rwkv_a0, rwkv_a_up, rwkv_g_up, rwkv_k_k, rwkv_k_a,
           rwkv_r_k, rwkv_ln_w, rwkv_ln_b, w_out, norm_ffn_w, peer_w_query, peer_sub_keys,
           peer_u, peer_v):
    B, S, D = x.shape
    T = B * S
    fw = FOX_HEADS * FOX_HEAD_DIM
    rw = RWKV_HEADS * RWKV_HEAD_DIM
    w_lora = rwkv_w_up.shape[0]
    a_lora = rwkv_a_up.shape[0]
    g_lora = rwkv_g_up.shape[0]
    assert w_lora <= LANES and a_lora + FOX_HEADS <= LANES and g_lora == 256

    c_pad = _pad_rows(c, 8)
    mod = _ada_mod(c_pad, w_ada, b_ada)[:B]
    sh1, sc1, g1, sh2, sc2, g2 = [m.reshape(B, 1, D) for m in jnp.split(mod, 6, axis=-1)]

    fox_cols = 4 * fw + FOX_HEADS
    wi_fox, wi_rw = w_in[:, :fox_cols], w_in[:, fox_cols:]
    mu = rwkv_mu.reshape(1, -1)
    seg = lambda a, lo, n: a[:, lo:lo + n]
    w_perm = jnp.concatenate([
        seg(wi_fox, 0, 4 * fw),
        seg(wi_rw, 0, 3 * rw),
        seg(wi_rw, 3 * rw + w_lora + a_lora, g_lora),
        _pad_cols(seg(wi_rw, 3 * rw, w_lora), LANES),
        _pad_cols(jnp.concatenate([seg(wi_rw, 3 * rw + w_lora, a_lora), seg(wi_fox, 4 * fw, FOX_HEADS)], 1), LANES),
    ], axis=1).astype(BF16)
    cols = {"gate": 3 * fw, "rr": 4 * fw, "rk": 4 * fw + rw, "rv": 4 * fw + 2 * rw,
            "gd": 4 * fw + 3 * rw, "lo": 4 * fw + 3 * rw + g_lora}
    f_lane = a_lora
    f_block = (cols["lo"] + LANES) // LANES
    mus = [seg(mu, 0, rw), seg(mu, rw, rw), seg(mu, 2 * rw, rw),
           seg(mu, 3 * rw + w_lora + a_lora, g_lora),
           jnp.concatenate([_pad_cols(seg(mu, 3 * rw, w_lora), LANES),
                            _pad_cols(seg(mu, 3 * rw + w_lora, a_lora), LANES)], 1)]
    scale = FOX_HEAD_DIM ** -0.5 * LOG2_E
    head_w = _pad_cols(jnp.concatenate([jnp.tile(fox_q_norm_w * scale, FOX_HEADS),
                                        jnp.tile(fox_k_norm_w, FOX_HEADS)]).reshape(1, -1), w_perm.shape[1])

    x2 = x.reshape(T, D)
    proj = _in_proj(x2, norm_mix_w.reshape(1, D), sc1, sh1, w_perm, head_w, S, 2 * fw)

    f_bias_row = jnp.zeros((1, LANES), F32).at[0, f_lane:f_lane + FOX_HEADS].set(fox_f_bias)
    cum = _fox_cum(proj, f_bias_row, B, S, f_block)
    cum_rows = cum.reshape(B, S, LANES)[:, :, f_lane:f_lane + FOX_HEADS].transpose(0, 2, 1).reshape(B, FOX_HEADS, 1, S)
    o_fox = _fox_attn(proj, cum_rows, B, S)

    row = lambda a: a.reshape(1, -1)
    r, lw, k, v, a_vec, b_vec, g, bonus = _rwkv_prep(
        proj, cols, mus, row(rwkv_w0), _pad_rows(rwkv_w_up, LANES), row(rwkv_a0), _pad_rows(rwkv_a_up, LANES),
        rwkv_g_up, row(rwkv_k_k), row(rwkv_k_a), row(rwkv_r_k), S)
    y = _rwkv_scan(r, lw, k, v, a_vec, b_vec, B, S)

    x1 = _mix_out(o_fox, proj, cols["gate"], y, bonus, g, x2, g1, row(rwkv_ln_w), row(rwkv_ln_b),
                  w_out.astype(BF16), S)

    h2, pa, ln, qb, rnk = _peer_route(x1, norm_ffn_w.reshape(1, D), sc2, sh2,
                                      peer_w_query.astype(BF16), peer_sub_keys, S)
    vt = peer_v.astype(BF16).reshape(-1, PEER_EXPERT_BLOCK, D).transpose(0, 2, 1)
    out = _peer_expert(h2.T, peer_u.astype(BF16), vt, pa, ln, qb, rnk, x1, g2, S)
    return out.reshape(B, S, D)


def kernel(x, c, w_ada, b_ada, norm_mix_w, w_in, fox_q_norm_w, fox_k_norm_w, fox_f_bias, rwkv_mu, rwkv_w0,
           rwkv_w_up, rwkv_a0, rwkv_a_up, rwkv_g_up, rwkv_k_k, rwkv_k_a, rwkv_r_k, rwkv_ln_w, rwkv_ln_b,
           w_out, norm_ffn_w, peer_w_query, peer_sub_keys, peer_u, peer_v):
    params = (w_ada, b_ada, norm_mix_w, w_in, fox_q_norm_w, fox_k_norm_w, fox_f_bias, rwkv_mu, rwkv_w0,
              rwkv_w_up, rwkv_a0, rwkv_a_up, rwkv_g_up, rwkv_k_k, rwkv_k_a, rwkv_r_k, rwkv_ln_w, rwkv_ln_b,
              w_out, norm_ffn_w, peer_w_query, peer_sub_keys, peer_u, peer_v)
    for l in range(w_ada.shape[0]):
        x = _layer(x, c, *[p[l] for p in params])
    return x
```

```python
import functools

import jax
import jax.numpy as jnp
from jax import lax
from jax.experimental import pallas as pl
from jax.experimental.pallas import tpu as pltpu

F32 = jnp.float32
BF16 = jnp.bfloat16
HIGHEST = lax.Precision.HIGHEST

LANES = 128
BF16_ROWS = 16
NORM_EPS = 1e-6
GN_EPS = 64e-5
CHUNK = 64
FOX_HEADS = 8
FOX_HEAD_DIM = 128
RWKV_HEADS = 16
RWKV_HEAD_DIM = 64
PEER_HEADS = 8
PEER_NKEYS = 128
PEER_TOPK = 16
PEER_EXPERT_BLOCK = 1024
PEER_SLABS = PEER_EXPERT_BLOCK // PEER_NKEYS
NEG_BIG = -1e30
LOG2_E = 1.4426950408889634
VMEM_LIMIT = 56 * 1024 * 1024


def _cparams(sem):
    return pltpu.CompilerParams(dimension_semantics=sem, vmem_limit_bytes=VMEM_LIMIT)


def _dot(a, b, precision=None):
    return jnp.dot(a, b, preferred_element_type=F32, precision=precision)


def _dot_nt(a, b, precision=None):
    return lax.dot_general(a, b, (((1,), (1,)), ((), ())), preferred_element_type=F32,
                           precision=precision)


def _split_bf16(x):
    hi = x.astype(BF16)
    return hi, (x - hi.astype(F32)).astype(BF16)


def _dot_bf16x3(a, b):
    ah, al = _split_bf16(a)
    bh, bl = _split_bf16(b)
    return _dot(ah, bh) + (_dot(ah, bl) + _dot(al, bh))


def _dot_tn(a, b, precision=None):
    return lax.dot_general(a, b, (((0,), (0,)), ((), ())), preferred_element_type=F32,
                           precision=precision)


def _ada_kernel(c_ref, w_ref, b_ref, o_ref):
    c = c_ref[...]
    s = c * jax.nn.sigmoid(c)
    o_ref[...] = _dot(s, w_ref[...], HIGHEST) + b_ref[...]


def _ada_mod(c_pad, w_ada, b_ada):
    rows, d = c_pad.shape
    n = w_ada.shape[1]
    bn = 1024
    return pl.pallas_call(
        _ada_kernel,
        grid=(n // bn,),
        in_specs=[pl.BlockSpec((rows, d), lambda j: (0, 0)),
                  pl.BlockSpec((d, bn), lambda j: (0, j)),
                  pl.BlockSpec((1, bn), lambda j: (0, j))],
        out_specs=pl.BlockSpec((rows, bn), lambda j: (0, j)),
        out_shape=jax.ShapeDtypeStruct((rows, n), F32),
        compiler_params=_cparams(("arbitrary",)),
        name="ada_mod",
    )(c_pad, w_ada, b_ada.reshape(1, n))


def _modulated_norm(x, nw, sc, sh):
    y = x * lax.rsqrt(jnp.mean(x * x, axis=-1, keepdims=True) + NORM_EPS)
    return y * nw * (1.0 + sc) + sh


def _norm_mod_kernel(x_ref, nw_ref, sc_ref, sh_ref, o_ref):
    o_ref[...] = _modulated_norm(x_ref[...], nw_ref[...], sc_ref[0], sh_ref[0]).astype(BF16)


def _norm_mod(x2, norm_w, sc, sh, tokens_per_batch):
    t, d = x2.shape
    tm = min(512, tokens_per_batch)
    tpb = tokens_per_batch // tm
    return pl.pallas_call(
        _norm_mod_kernel,
        grid=(t // tm,),
        in_specs=[pl.BlockSpec((tm, d), lambda i: (i, 0)),
                  pl.BlockSpec((1, d), lambda i: (0, 0)),
                  pl.BlockSpec((1, 1, d), lambda i: (i // tpb, 0, 0)),
                  pl.BlockSpec((1, 1, d), lambda i: (i // tpb, 0, 0))],
        out_specs=pl.BlockSpec((tm, d), lambda i: (i, 0)),
        out_shape=jax.ShapeDtypeStruct((t, d), BF16),
        compiler_params=_cparams(("parallel",)),
        name="norm_mod",
    )(x2, norm_w, sc, sh)


def _in_proj_kernel(h_ref, w_ref, hw_ref, o_ref, *, n_qk_blocks):
    j = pl.program_id(1)
    acc = _dot(h_ref[...], w_ref[...])

    @pl.when(j < n_qk_blocks)
    def _():
        for hh in range(acc.shape[1] // FOX_HEAD_DIM):
            sl = slice(hh * FOX_HEAD_DIM, (hh + 1) * FOX_HEAD_DIM)
            a = acc[:, sl]
            rs = lax.rsqrt(jnp.mean(a * a, axis=-1, keepdims=True) + NORM_EPS)
            o_ref[:, sl] = a * rs * hw_ref[:, sl]

    @pl.when(j >= n_qk_blocks)
    def _():
        o_ref[...] = acc


def _in_proj(h_bf, w_bf, head_w, n_qk_cols):
    t, d = h_bf.shape
    n = w_bf.shape[1]
    tm = min(2048, t)
    bn = 512
    kern = functools.partial(_in_proj_kernel, n_qk_blocks=n_qk_cols // bn)
    return pl.pallas_call(
        kern,
        grid=(t // tm, n // bn),
        in_specs=[pl.BlockSpec((tm, d), lambda i, j: (i, 0)),
                  pl.BlockSpec((d, bn), lambda i, j: (0, j)),
                  pl.BlockSpec((1, bn), lambda i, j: (0, j))],
        out_specs=pl.BlockSpec((tm, bn), lambda i, j: (i, j)),
        out_shape=jax.ShapeDtypeStruct((t, n), F32),
        compiler_params=_cparams(("parallel", "arbitrary")),
        name="in_proj",
    )(h_bf, w_bf, head_w)


def _log_sigmoid(x):
    return jnp.minimum(x, 0.0) - jnp.log(1.0 + jnp.exp(-jnp.abs(x)))


def _fox_cum_kernel(f_ref, b_ref, o_ref, *, blk):
    s = f_ref.shape[0]
    row = lax.broadcasted_iota(jnp.int32, (blk, blk), 0)
    col = lax.broadcasted_iota(jnp.int32, (blk, blk), 1)
    tri = (row >= col).astype(F32)
    carry = jnp.zeros((1, f_ref.shape[1]), F32)
    for i in range(s // blk):
        lf = _log_sigmoid(f_ref[i * blk:(i + 1) * blk, :] + b_ref[...])
        cs = _dot(tri, lf, HIGHEST) + carry
        o_ref[i * blk:(i + 1) * blk, :] = cs * LOG2_E
        carry = cs[blk - 1:blk, :]


def _fox_cum(proj, f_bias_row, batch, seq, col_block):
    blk = min(256, seq)
    return pl.pallas_call(
        functools.partial(_fox_cum_kernel, blk=blk),
        grid=(batch,),
        in_specs=[pl.BlockSpec((seq, LANES), lambda b: (b, col_block)),
                  pl.BlockSpec((1, LANES), lambda b: (0, 0))],
        out_specs=pl.BlockSpec((seq, LANES), lambda b: (b, 0)),
        out_shape=jax.ShapeDtypeStruct((batch * seq, LANES), F32),
        compiler_params=_cparams(("parallel",)),
        name="fox_cum",
    )(proj, f_bias_row)


def _fox_attn_kernel(qt_ref, kt_ref, q_ref, k_ref, v_ref, c_ref, o_ref, m_scr, l_scr, acc_scr):
    t = pl.program_id(2)
    qi = qt_ref[t]
    ki = kt_ref[t]
    hd = FOX_HEAD_DIM
    n_heads = q_ref.shape[1] // hd

    @pl.when(ki == 0)
    def _():
        m_scr[...] = jnp.full(m_scr.shape, NEG_BIG, F32)
        l_scr[...] = jnp.zeros(l_scr.shape, F32)
        acc_scr[...] = jnp.zeros(acc_scr.shape, F32)

    def step(masked):
        hs = range(n_heads)
        sl = [slice(h * hd, (h + 1) * hd) for h in hs]
        s = [_dot_nt(q_ref[:, sl[h]].astype(BF16), k_ref[:, sl[h]].astype(BF16)) - c_ref[0, h] for h in hs]
        if masked:
            row = lax.broadcasted_iota(jnp.int32, s[0].shape, 0)
            col = lax.broadcasted_iota(jnp.int32, s[0].shape, 1)
            s = [jnp.where(row >= col, x, NEG_BIG) for x in s]
        m_prev = [m_scr[h] for h in hs]
        m_new = [jnp.maximum(m_prev[h], jnp.max(s[h], axis=-1, keepdims=True)) for h in hs]
        n_rep = s[0].shape[1] // hd
        p = [jnp.exp2(s[h] - jnp.concatenate([m_new[h]] * n_rep, axis=1)) for h in hs]
        alpha = [jnp.exp2(m_prev[h] - m_new[h]) for h in hs]
        ones = jnp.ones((k_ref.shape[0], hd), BF16)
        pv = [_dot(p[h].astype(BF16), jnp.concatenate([v_ref[:, sl[h]].astype(BF16), ones], axis=1))
              for h in hs]
        for h in hs:
            l_scr[h] = alpha[h] * l_scr[h] + pv[h][:, hd:]
            acc_scr[:, sl[h]] = alpha[h] * acc_scr[:, sl[h]] + pv[h][:, :hd]
            m_scr[h] = m_new[h]

    @pl.when(ki < qi)
    def _():
        step(False)

    @pl.when(ki == qi)
    def _():
        step(True)
        for h in range(n_heads):
            sl = slice(h * hd, (h + 1) * hd)
            o_ref[:, sl] = acc_scr[:, sl] / l_scr[h]


def _fox_attn(proj, cum_rows, batch, seq):
    tq = min(512, seq)
    nq = seq // tq
    hps = 4
    w = hps * FOX_HEAD_DIM
    kcol = FOX_HEADS // hps
    tri = [(q, k) for q in range(nq) for k in range(q + 1)]
    qt = jnp.asarray([q for q, _ in tri], jnp.int32)
    kt = jnp.asarray([k for _, k in tri], jnp.int32)
    grid_spec = pltpu.PrefetchScalarGridSpec(
        num_scalar_prefetch=2,
        grid=(batch, kcol, len(tri)),
        in_specs=[
            pl.BlockSpec((tq, w), lambda b, h, t, qt, kt: (b * nq + qt[t], h)),
            pl.BlockSpec((tq, w), lambda b, h, t, qt, kt: (b * nq + kt[t], kcol + h)),
            pl.BlockSpec((tq, w), lambda b, h, t, qt, kt: (b * nq + kt[t], 2 * kcol + h)),
            pl.BlockSpec((1, hps, 1, tq), lambda b, h, t, qt, kt: (b, h, 0, kt[t])),
        ],
        out_specs=pl.BlockSpec((tq, w), lambda b, h, t, qt, kt: (b * nq + qt[t], h)),
        scratch_shapes=[pltpu.VMEM((hps, tq, FOX_HEAD_DIM), F32), pltpu.VMEM((hps, tq, FOX_HEAD_DIM), F32),
                        pltpu.VMEM((tq, w), F32)],
    )
    return pl.pallas_call(
        _fox_attn_kernel,
        grid_spec=grid_spec,
        out_shape=jax.ShapeDtypeStruct((batch * seq, FOX_HEADS * FOX_HEAD_DIM), F32),
        compiler_params=_cparams(("parallel", "parallel", "arbitrary")),
        name="fox_attn",
    )(qt, kt, proj, proj, proj, cum_rows)


def _head_sum(x, bd):
    hi, lo = _split_bf16(x)
    parts = [_dot(hi[:, j * LANES:(j + 1) * LANES], bd) + _dot(lo[:, j * LANES:(j + 1) * LANES], bd)
             for j in range(x.shape[1] // LANES)]
    return jnp.concatenate(parts, axis=1)


def _head_block_diag():
    r = lax.broadcasted_iota(jnp.int32, (LANES, LANES), 0) // RWKV_HEAD_DIM
    c = lax.broadcasted_iota(jnp.int32, (LANES, LANES), 1) // RWKV_HEAD_DIM
    return (r == c).astype(BF16)


def _rwkv_prep_kernel(r_ref, k_ref, v_ref, gd_ref, lo_ref, pr_ref, pk_ref, pv_ref, pgd_ref, plo_ref,
                      mu_r, mu_k, mu_v, mu_gd, mu_lo, w0_ref, wup_ref, a0_ref, aup_ref, gup_ref,
                      kk_ref, ka_ref, rk_ref,
                      or_ref, olw_ref, ok_ref, ov_ref, oa_ref, ob_ref, og_ref, obonus_ref, *, tiles_per_batch):
    i = pl.program_id(0)
    first = (i % tiles_per_batch) == 0

    def shifted(cur_ref, prv_ref, mu_ref):
        cur = cur_ref[...]
        last = jnp.where(first, 0.0, prv_ref[7:8, :])
        row = lax.broadcasted_iota(jnp.int32, cur.shape, 0)
        prev = jnp.where(row == 0, last, pltpu.roll(cur, 1, 0))
        return cur + (prev - cur) * mu_ref[...]

    r = shifted(r_ref, pr_ref, mu_r)
    k = shifted(k_ref, pk_ref, mu_k)
    v = shifted(v_ref, pv_ref, mu_v)
    gd = shifted(gd_ref, pgd_ref, mu_gd)
    lo = shifted(lo_ref, plo_ref, mu_lo)
    wd = lo[:, :LANES]
    ad = lo[:, LANES:]

    w_pre = w0_ref[...] + _dot_bf16x3(jnp.tanh(wd), wup_ref[...])
    w_raw = _log_sigmoid(w_pre) - 0.5
    log_decay = -jnp.exp(w_raw)
    a = jax.nn.sigmoid(a0_ref[...] + _dot_bf16x3(ad, aup_ref[...]))
    g = _dot_bf16x3(jax.nn.sigmoid(gd), gup_ref[...])

    bd = _head_block_diag()
    kk = k * kk_ref[...]
    nrm = jnp.maximum(jnp.sqrt(_head_sum(kk * kk, bd)), 1e-12)
    kk = kk / nrm
    k_mod = k * (1.0 + (a - 1.0) * ka_ref[...])
    bonus = _head_sum(r * k_mod * rk_ref[...], bd) * v

    or_ref[...] = r.astype(BF16)
    olw_ref[...] = log_decay
    ok_ref[...] = k_mod.astype(BF16)
    ov_ref[...] = v.astype(BF16)
    oa_ref[...] = (-kk).astype(BF16)
    ob_ref[...] = (kk * a).astype(BF16)
    og_ref[...] = g.astype(BF16)
    obonus_ref[...] = bonus.astype(BF16)


def _rwkv_prep(proj, cols, mus, w0, w_up, a0, a_up, g_up, k_k, k_a, r_k, tokens_per_batch):
    t = proj.shape[0]
    w = RWKV_HEADS * RWKV_HEAD_DIM
    tm = min(256, tokens_per_batch)
    tpb = tokens_per_batch // tm
    widths = [w, w, w, 256, 256]
    offs = [cols["rr"], cols["rk"], cols["rv"], cols["gd"], cols["lo"]]
    cur_specs = [pl.BlockSpec((tm, wd), functools.partial(lambda i, cb: (i, cb), cb=o // wd))
                 for wd, o in zip(widths, offs)]
    prv_specs = [pl.BlockSpec((8, wd), functools.partial(
        lambda i, cb: (jnp.maximum(i * (tm // 8) - 1, 0), cb), cb=o // wd))
        for wd, o in zip(widths, offs)]
    full = lambda a: pl.BlockSpec(a.shape, lambda i: (0,) * a.ndim)
    params = list(mus) + [w0, w_up, a0, a_up, g_up, k_k, k_a, r_k]
    out_spec = pl.BlockSpec((tm, w), lambda i: (i, 0))
    return pl.pallas_call(
        functools.partial(_rwkv_prep_kernel, tiles_per_batch=tpb),
        grid=(t // tm,),
        in_specs=cur_specs + prv_specs + [full(p) for p in params],
        out_specs=[out_spec] * 8,
        out_shape=[jax.ShapeDtypeStruct((t, w), dt) for dt in (BF16, F32, BF16, BF16, BF16, BF16, BF16, BF16)],
        compiler_params=_cparams(("parallel",)),
        name="rwkv_prep",
    )(*([proj] * 10), *params)


def _rwkv_scan_kernel(r_ref, lw_ref, k_ref, v_ref, a_ref, b_ref, o_ref, h_scr):
    c = pl.program_id(1)

    @pl.when(c == 0)
    def _():
        h_scr[...] = jnp.zeros(h_scr.shape, F32)

    C = r_ref.shape[0]
    n_pairs = r_ref.shape[1] // LANES
    P = HIGHEST

    row = lax.broadcasted_iota(jnp.int32, (C, C), 0)
    col = lax.broadcasted_iota(jnp.int32, (C, C), 1)
    tri = (row >= col).astype(F32)
    lw = lw_ref[...]
    cw = _dot(tri, lw, P)
    cw_end = cw[C - 1:C, :]
    e_pos = jnp.exp(cw)
    e_prev = jnp.exp(cw - lw)
    e_neg = jnp.exp(-cw)
    e_end = jnp.exp(cw_end - cw)
    w_end = jnp.exp(cw_end)

    a = a_ref[...].astype(F32)
    b = b_ref[...].astype(F32)
    k = k_ref[...].astype(F32)
    r = r_ref[...].astype(F32)
    at = a * e_prev
    bt = b * e_neg
    kt = k * e_neg
    rt = r * e_pos
    bh = b * e_end
    kh = k * e_end
    v = v_ref[...].astype(F32)

    lane = lax.broadcasted_iota(jnp.int32, (C, LANES), 1)
    head0 = lane < RWKV_HEAD_DIM
    r2 = lax.broadcasted_iota(jnp.int32, (2 * C, 2 * C), 0)
    c2 = lax.broadcasted_iota(jnp.int32, (2 * C, 2 * C), 1)
    same = (r2 // C) == (c2 // C)
    strict = same & (r2 > c2)
    incl = same & (r2 >= c2)
    eye = (r2 == c2).astype(F32)

    def two(x, p):
        xp = x[:, p * LANES:(p + 1) * LANES]
        return jnp.concatenate([jnp.where(head0, xp, 0.0), jnp.where(head0, 0.0, xp)], axis=0)

    def mm(x, y):
        return _dot(x.astype(BF16), y.astype(BF16))

    def split(x):
        hi = x.astype(BF16)
        return hi, (x - hi.astype(F32)).astype(BF16)

    def mm3(x, y):
        xh, xl = split(x)
        yh, yl = split(y)
        return _dot(xh, yh) + (_dot(xh, yl) + _dot(xl, yh))

    G = 2 * C
    pairs = range(n_pairs)
    at2 = [two(at, p) for p in pairs]
    rt2 = [two(rt, p) for p in pairs]
    v2 = [two(v, p).astype(BF16) for p in pairs]
    gram = [_dot_nt(jnp.concatenate([at2[p], rt2[p]], axis=0).astype(BF16),
                    jnp.concatenate([two(bt, p), two(kt, p)], axis=0).astype(BF16)) for p in pairs]
    a_ab = [jnp.where(strict, gram[p][:G, :G], 0.0) for p in pairs]
    nmat = list(a_ab)
    pw = list(a_ab)
    for _ in range(max(1, (C - 1).bit_length() - 1)):
        pw = [mm3(pw[p], pw[p]) for p in pairs]
        nmat = [nmat[p] + pw[p] + mm3(nmat[p], pw[p]) for p in pairs]
    akv = [mm(jnp.where(strict, gram[p][:G, G:], 0.0), v2[p]) for p in pairs]
    rhs = [jnp.concatenate([at2[p], akv[p]], axis=1) for p in pairs]
    pq = [(rhs[p] + mm(nmat[p], rhs[p])).astype(BF16) for p in pairs]
    ry = [mm(jnp.where(incl, gram[p][G:, :G], 0.0), pq[p]) for p in pairs]
    mv = [mm(jnp.where(incl, gram[p][G:, G:], 0.0), v2[p]) for p in pairs]
    ge = [_dot_tn(two(bh, p).astype(BF16), pq[p]) for p in pairs]
    kv = [_dot_tn(two(kh, p).astype(BF16), v2[p]) for p in pairs]
    for p in pairs:
        rr = rt2[p] + ry[p][:, :LANES]
        gm = eye * w_end[:, p * LANES:(p + 1) * LANES] + ge[p][:, :LANES]
        yh = mm(jnp.concatenate([rr, gm], axis=0), h_scr[p])
        y2 = yh[:G] + ry[p][:, LANES:] + mv[p]
        h_scr[p] = yh[G:] + ge[p][:, LANES:] + kv[p]
        o_ref[:, p * LANES:(p + 1) * LANES] = y2[:C, :] + y2[C:, :]


def _rwkv_scan(r, lw, k, v, a, b, batch, seq):
    w = r.shape[1]
    nc = seq // CHUNK
    spec = pl.BlockSpec((CHUNK, w), lambda bi, ci: (bi * nc + ci, 0))
    return pl.pallas_call(
        _rwkv_scan_kernel,
        grid=(batch, nc),
        in_specs=[spec] * 6,
        out_specs=spec,
        out_shape=jax.ShapeDtypeStruct((batch * seq, w), F32),
        scratch_shapes=[pltpu.VMEM((w // LANES, LANES, LANES), F32)],
        compiler_params=_cparams(("parallel", "arbitrary")),
        name="rwkv_scan",
    )(r, lw, k, v, a, b)


def _mix_out_kernel(o_ref, gate_ref, y_ref, bonus_ref, g_ref, x_ref, g1_ref, lnw_ref, lnb_ref, w_ref, out_ref):
    fox = o_ref[...] * jax.nn.sigmoid(gate_ref[...])
    bd = _head_block_diag()
    y = y_ref[...]
    inv_n = 1.0 / RWKV_HEAD_DIM
    mean = _head_sum(y, bd) * inv_n
    d = y - mean
    var = _head_sum(d * d, bd) * inv_n
    yn = d * lax.rsqrt(var + GN_EPS) * lnw_ref[...] + lnb_ref[...]
    rw = (yn + bonus_ref[...].astype(F32)) * g_ref[...].astype(F32)
    wf = fox.shape[1]
    mix = _dot(fox.astype(BF16), w_ref[:wf, :]) + _dot(rw.astype(BF16), w_ref[wf:, :])
    out_ref[...] = x_ref[...] + g1_ref[0] * mix


def _mix_out(o_fox, proj, gate_col, y, bonus, g, x2, g1, ln_w, ln_b, w_out_bf, tokens_per_batch):
    t, d = x2.shape
    wf = o_fox.shape[1]
    wr = y.shape[1]
    tm = min(256, tokens_per_batch)
    tpb = tokens_per_batch // tm
    return pl.pallas_call(
        _mix_out_kernel,
        grid=(t // tm,),
        in_specs=[pl.BlockSpec((tm, wf), lambda i: (i, 0)),
                  pl.BlockSpec((tm, wf), lambda i: (i, gate_col // wf)),
                  pl.BlockSpec((tm, wr), lambda i: (i, 0)),
                  pl.BlockSpec((tm, wr), lambda i: (i, 0)),
                  pl.BlockSpec((tm, wr), lambda i: (i, 0)),
                  pl.BlockSpec((tm, d), lambda i: (i, 0)),
                  pl.BlockSpec((1, 1, d), lambda i: (i // tpb, 0, 0)),
                  pl.BlockSpec((1, wr), lambda i: (0, 0)),
                  pl.BlockSpec((1, wr), lambda i: (0, 0)),
                  pl.BlockSpec(w_out_bf.shape, lambda i: (0, 0))],
        out_specs=pl.BlockSpec((tm, d), lambda i: (i, 0)),
        out_shape=jax.ShapeDtypeStruct((t, d), F32),
        compiler_params=_cparams(("parallel",)),
        name="mix_out",
    )(o_fox, proj, y, bonus, g, x2, g1, ln_w, ln_b, w_out_bf)


def _top_k_mask_rows(s, k, tie_safe, want_rank=True):
    n = s.shape[0]
    s0 = s
    row = lax.broadcasted_iota(jnp.int32, s.shape, 0).astype(F32)
    rank = jnp.full(s.shape, float(k), F32) if want_rank else None
    vals = []
    for r in range(k):
        m = jnp.max(s, axis=0, keepdims=True)
        hit = s == m
        if tie_safe:
            pos = jnp.min(jnp.where(hit, row, float(n)), axis=0, keepdims=True)
            hit = row == pos
        vals.append(m)
        if want_rank:
            rank = jnp.where(hit, float(r), rank)
        s = jnp.where(hit, -jnp.inf, s)
    picked = (s == -jnp.inf) & (s0 != -jnp.inf)
    n_picked = jnp.sum(jnp.where(picked, 1.0, 0.0), axis=0, keepdims=True)
    return jnp.concatenate(vals, axis=0), rank, picked, jnp.where(n_picked == float(k), 1.0, 0.0)


def _candidate_rows():
    k = PEER_TOPK
    groups, valid = [], []
    for r0 in range(k // 2):
        n1 = k // (r0 + 1)
        for g in range(-(-n1 // 8)):
            groups.append((r0, g * 8))
            valid.append([g * 8 + i < n1 for i in range(8)])
    groups.append((None, k // 2))
    valid.append([True] * 8)
    return groups, valid


def _route_head(sa, sb, tie_safe):
    k = PEER_TOPK
    tm = sa.shape[1]
    groups, valid = _candidate_rows()
    sub8 = lax.broadcasted_iota(jnp.int32, (8, tm), 0)
    top_a, rank_a, _, clean_a = _top_k_mask_rows(sa, k, tie_safe)
    top_b, rank_b, _, clean_b = _top_k_mask_rows(sb, k, tie_safe)
    parts = []
    for (r0, off), ok in zip(groups, valid):
        if r0 is None:
            part = top_a[off:off + 8, :] + top_b[0:1, :]
        else:
            part = top_a[r0:r0 + 1, :] + top_b[off:off + 8, :]
        if not all(ok):
            part = jnp.where(sub8 < sum(ok), part, -jnp.inf)
        parts.append(part)
    cand = jnp.concatenate(parts, axis=0)
    _, _, sel, clean_c = _top_k_mask_rows(cand, k, tie_safe, want_rank=False)
    z = jnp.sum(jnp.where(sel, jnp.exp(cand - cand[0:1, :]), 0.0), axis=0, keepdims=True)
    self32 = jnp.where(sel, 1.0, 0.0)
    n_keys = sa.shape[0]
    rank3 = rank_a.astype(BF16).reshape(n_keys // BF16_ROWS, BF16_ROWS, tm)
    length3 = jnp.zeros(rank3.shape, BF16)

    def add_count(r0, count_row):
        tile = jnp.broadcast_to(count_row, (BF16_ROWS, tm)).astype(BF16)
        return length3 + jnp.where(rank3 == float(r0), tile[None], 0.0).astype(BF16)

    tail = None
    per_rank = {}
    for gi, (r0, off) in enumerate(groups):
        cnt = self32[gi * 8:(gi + 1) * 8, :]
        if r0 is None:
            tail = cnt
        else:
            c = jnp.sum(cnt, axis=0, keepdims=True)
            per_rank[r0] = c if r0 not in per_rank else per_rank[r0] + c
    for r in range(8):
        per_rank[k // 2 + r] = tail[r:r + 1, :]
    for r0, c in per_rank.items():
        length3 = add_count(r0, c)
    length = length3.reshape(n_keys, tm).astype(F32)
    pa = jnp.exp(sa - top_a[0:1, :])
    qb = jnp.exp(sb - top_b[0:1, :]) / z
    return pa, length, qb, rank_b, clean_a * clean_b * clean_c


def _peer_route_kernel(x_ref, nw_ref, sc_ref, sh_ref, wq_ref, keys_ref,
                       h_ref, pa_ref, len_ref, qb_ref, rnk_ref):
    h = _modulated_norm(x_ref[...], nw_ref[...], sc_ref[0], sh_ref[0]).astype(BF16)
    h_ref[...] = h
    q = _dot(h, wq_ref[...])
    tm = q.shape[0]
    half = keys_ref.shape[-1]
    blocked = (PEER_NKEYS // PEER_SLABS, PEER_SLABS, tm)
    def scores(hd):
        qa = q[:, (hd * 2) * half:(hd * 2 + 1) * half]
        qb = q[:, (hd * 2 + 1) * half:(hd * 2 + 2) * half]
        return (_dot_nt(keys_ref[hd, 0], qa, HIGHEST),
                _dot_nt(keys_ref[hd, 1], qb, HIGHEST))

    def emit(hd, tables):
        pa, length, qbt, rank_b, _ = tables
        pa_ref[:, hd] = pa.reshape(blocked)
        len_ref[:, hd] = length.reshape(blocked)
        qb_ref[hd] = qbt.astype(BF16)
        rnk_ref[hd] = rank_b.astype(BF16)

    code = jnp.zeros((1, 1), F32)
    for hd in range(PEER_HEADS):
        fast = _route_head(*scores(hd), tie_safe=False)
        emit(hd, fast)
        dirty = 1.0 - jnp.min(fast[4], axis=1, keepdims=True)
        code = code + dirty * float(1 << hd)
    tie_bits = code[0, 0].astype(jnp.int32)
    for hd in range(PEER_HEADS):
        @pl.when(((tie_bits >> hd) & 1) == 1)
        def _(hd=hd):
            emit(hd, _route_head(*scores(hd), tie_safe=True))


def _peer_route(x1, norm_w, sc, sh, wq_bf, sub_keys, tokens_per_batch):
    t, d = x1.shape
    tm = min(256, tokens_per_batch)
    tpb = tokens_per_batch // tm
    n_blocks = PEER_NKEYS // PEER_SLABS
    dense = lambda dt: jax.ShapeDtypeStruct((PEER_HEADS, PEER_NKEYS, t), dt)
    dense_spec = pl.BlockSpec((PEER_HEADS, PEER_NKEYS, tm), lambda i: (0, 0, i))
    blocked = jax.ShapeDtypeStruct((n_blocks, PEER_HEADS, PEER_SLABS, t), F32)
    blocked_spec = pl.BlockSpec((n_blocks, PEER_HEADS, PEER_SLABS, tm), lambda i: (0, 0, 0, i))
    return pl.pallas_call(
        _peer_route_kernel,
        grid=(t // tm,),
        in_specs=[pl.BlockSpec((tm, d), lambda i: (i, 0)),
                  pl.BlockSpec((1, d), lambda i: (0, 0)),
                  pl.BlockSpec((1, 1, d), lambda i: (i // tpb, 0, 0)),
                  pl.BlockSpec((1, 1, d), lambda i: (i // tpb, 0, 0)),
                  pl.BlockSpec(wq_bf.shape, lambda i: (0, 0)),
                  pl.BlockSpec(sub_keys.shape, lambda i: (0, 0, 0, 0))],
        out_specs=[pl.BlockSpec((tm, d), lambda i: (i, 0)), blocked_spec, blocked_spec, dense_spec, dense_spec],
        out_shape=[jax.ShapeDtypeStruct((t, d), BF16), blocked, blocked, dense(BF16), dense(BF16)],
        compiler_params=_cparams(("parallel",)),
        name="peer_route",
    )(x1, norm_w, sc, sh, wq_bf, sub_keys)


def _gelu_exact(x):
    return 0.5 * x * (1.0 + lax.erf(x * 0.7071067811865476))


def _peer_expert_kernel(ht_ref, u_ref, vt_ref, pa_ref, len_ref, qb_ref, rnk_ref, x_ref, g2_ref, o_ref, acc_scr):
    j = pl.program_id(1)
    be = u_ref.shape[0]
    tm = ht_ref.shape[1]

    @pl.when(j == 0)
    def _():
        acc_scr[...] = jnp.zeros(acc_scr.shape, F32)

    n_slabs = be // PEER_NKEYS
    rows = BF16_ROWS
    n_groups = PEER_NKEYS // rows
    act = _gelu_exact(_dot(u_ref[...], ht_ref[...])).astype(BF16)

    def row_tile(ref, hd, s):
        return jnp.broadcast_to(ref[0, hd, s:s + 1, :], (rows, tm)).astype(BF16)

    zero = jnp.zeros((rows, tm), BF16)
    parts = [[None] * n_groups for _ in range(n_slabs)]
    slab_group = 4
    for s0 in range(0, n_slabs, slab_group):
        ss = range(s0, min(s0 + slab_group, n_slabs))
        ln = {(hd, s): row_tile(len_ref, hd, s) for hd in range(PEER_HEADS) for s in ss}
        pa = {(hd, s): row_tile(pa_ref, hd, s) for hd in range(PEER_HEADS) for s in ss}
        for g in range(n_groups):
            sl = slice(g * rows, (g + 1) * rows)
            acc = {}
            for hd in range(PEER_HEADS):
                rk = rnk_ref[hd, sl, :]
                qv = qb_ref[hd, sl, :]
                for s in ss:
                    term = jnp.where(rk < ln[hd, s], qv, zero) * pa[hd, s]
                    acc[s] = term if s not in acc else acc[s] + term
            for s in ss:
                lo = s * PEER_NKEYS + g * rows
                parts[s][g] = acc[s] * act[lo:lo + rows, :]
    p = jnp.concatenate([parts[s][g] for s in range(n_slabs) for g in range(n_groups)], axis=0)
    acc_scr[...] += _dot(vt_ref[0], p)

    @pl.when(j == pl.num_programs(1) - 1)
    def _():
        o_ref[...] = x_ref[...] + g2_ref[0] * acc_scr[...].T


def _peer_expert(ht_bf, u_bf, vt_bf, pa, ln, qb, rnk, x1, g2, tokens_per_batch):
    t, d = x1.shape
    ne = u_bf.shape[0]
    tm = min(512, tokens_per_batch)
    tpb = tokens_per_batch // tm
    be = PEER_EXPERT_BLOCK
    dense_spec = pl.BlockSpec((PEER_HEADS, PEER_NKEYS, tm), lambda i, j: (0, 0, i))
    blocked_spec = pl.BlockSpec((1, PEER_HEADS, PEER_SLABS, tm), lambda i, j: (j, 0, 0, i))
    return pl.pallas_call(
        _peer_expert_kernel,
        grid=(t // tm, ne // be),
        in_specs=[pl.BlockSpec((d, tm), lambda i, j: (0, i)),
                  pl.BlockSpec((be, d), lambda i, j: (j, 0)),
                  pl.BlockSpec((1, d, be), lambda i, j: (j, 0, 0)),
                  blocked_spec, blocked_spec, dense_spec, dense_spec,
                  pl.BlockSpec((tm, d), lambda i, j: (i, 0)),
                  pl.BlockSpec((1, 1, d), lambda i, j: (i // tpb, 0, 0))],
        out_specs=pl.BlockSpec((tm, d), lambda i, j: (i, 0)),
        out_shape=jax.ShapeDtypeStruct((t, d), F32),
        scratch_shapes=[pltpu.VMEM((d, tm), F32)],
        compiler_params=_cparams(("parallel", "arbitrary")),
        name="peer_expert",
    )(ht_bf, u_bf, vt_bf, pa, ln, qb, rnk, x1, g2)


def _pad_cols(a, n):
    return jnp.pad(a, ((0, 0), (0, n - a.shape[1])))


def _pad_rows(a, n):
    return jnp.pad(a, ((0, n - a.shape[0]), (0, 0)))


def _layer(x, c, w_ada, b_ada, norm_mix_w, w_in, fox_q_norm_w, fox_k_norm_w, fox_f_bias,
           rwkv_mu, rwkv_w0, rwkv_w_up, rwkv_a0, rwkv_a_up, rwkv_g_up, rwkv_k_k, rwkv_k_a,
           rwkv_r_k, rwkv_ln_w, rwkv_ln_b, w_out, norm_ffn_w, peer_w_query, peer_sub_keys,
           peer_u, peer_v):
    B, S, D = x.shape
    T = B * S
    fw = FOX_HEADS * FOX_HEAD_DIM
    rw = RWKV_HEADS * RWKV_HEAD_DIM
    w_lora = rwkv_w_up.shape[0]
    a_lora = rwkv_a_up.shape[0]
    g_lora = rwkv_g_up.shape[0]
    assert w_lora <= LANES and a_lora + FOX_HEADS <= LANES and g_lora == 256

    c_pad = _pad_rows(c, 8)
    mod = _ada_mod(c_pad, w_ada, b_ada)[:B]
    sh1, sc1, g1, sh2, sc2, g2 = [m.reshape(B, 1, D) for m in jnp.split(mod, 6, axis=-1)]

    fox_cols = 4 * fw + FOX_HEADS
    wi_fox, wi_rw = w_in[:, :fox_cols], w_in[:, fox_cols:]
    mu = rwkv_mu.reshape(1, -1)
    seg = lambda a, lo, n: a[:, lo:lo + n]
    w_perm = jnp.concatenate([
        seg(wi_fox, 0, 4 * fw),
        seg(wi_rw, 0, 3 * rw),
        seg(wi_rw, 3 * rw + w_lora + a_lora, g_lora),
        _pad_cols(seg(wi_rw, 3 * rw, w_lora), LANES),
        _pad_cols(jnp.concatenate([seg(wi_rw, 3 * rw + w_lora, a_lora), seg(wi_fox, 4 * fw, FOX_HEADS)], 1), LANES),
    ], axis=1).astype(BF16)
    cols = {"gate": 3 * fw, "rr": 4 * fw, "rk": 4 * fw + rw, "rv": 4 * fw + 2 * rw,
            "gd": 4 * fw + 3 * rw, "lo": 4 * fw + 3 * rw + g_lora}
    f_lane = a_lora
    f_block = (cols["lo"] + LANES) // LANES
    mus = [seg(mu, 0, rw), seg(mu, rw, rw), seg(mu, 2 * rw, rw),
           seg(mu, 3 * rw + w_lora + a_lora, g_lora),
           jnp.concatenate([_pad_cols(seg(mu, 3 * rw, w_lora), LANES),
                            _pad_cols(seg(mu, 3 * rw + w_lora, a_lora), LANES)], 1)]
    scale = FOX_HEAD_DIM ** -0.5 * LOG2_E
    head_w = _pad_cols(jnp.concatenate([jnp.tile(fox_q_norm_w * scale, FOX_HEADS),
                                        jnp.tile(fox_k_norm_w, FOX_HEADS)]).reshape(1, -1), w_perm.shape[1])

    x2 = x.reshape(T, D)
    proj = _in_proj(_norm_mod(x2, norm_mix_w.reshape(1, D), sc1, sh1, S), w_perm, head_w, 2 * fw)

    f_bias_row = jnp.zeros((1, LANES), F32).at[0, f_lane:f_lane + FOX_HEADS].set(fox_f_bias)
    cum = _fox_cum(proj, f_bias_row, B, S, f_block)
    cum_rows = cum.reshape(B, S, LANES)[:, :, f_lane:f_lane + FOX_HEADS].transpose(0, 2, 1).reshape(B, FOX_HEADS, 1, S)
    o_fox = _fox_attn(proj, cum_rows, B, S)

    row = lambda a: a.reshape(1, -1)
    r, lw, k, v, a_vec, b_vec, g, bonus = _rwkv_prep(
        proj, cols, mus, row(rwkv_w0), _pad_rows(rwkv_w_up, LANES), row(rwkv_a0), _pad_rows(rwkv_a_up, LANES),
        rwkv_g_up, row(rwkv_k_k), row(rwkv_k_a), row(rwkv_r_k), S)
    y = _rwkv_scan(r, lw, k, v, a_vec, b_vec, B, S)

    x1 = _mix_out(o_fox, proj, cols["gate"], y, bonus, g, x2, g1, row(rwkv_ln_w), row(rwkv_ln_b),
                  w_out.astype(BF16), S)

    h2, pa, ln, qb, rnk = _peer_route(x1, norm_ffn_w.reshape(1, D), sc2, sh2,
                                      peer_w_query.astype(BF16), peer_sub_keys, S)
    vt = peer_v.astype(BF16).reshape(-1, PEER_EXPERT_BLOCK, D).transpose(0, 2, 1)
    out = _peer_expert(h2.T, peer_u.astype(BF16), vt, pa, ln, qb, rnk, x1, g2, S)
    return out.reshape(B, S, D)


def kernel(x, c, w_ada, b_ada, norm_mix_w, w_in, fox_q_norm_w, fox_k_norm_w, fox_f_bias, rwkv_mu, rwkv_w0,
           rwkv_w_up, rwkv_a0, rwkv_a_up, rwkv_g_up, rwkv_k_k, rwkv_k_a, rwkv_r_k, rwkv_ln_w, rwkv_ln_b,
           w_out, norm_ffn_w, peer_w_query, peer_sub_keys, peer_u, peer_v):
    params = (w_ada, b_ada, norm_mix_w, w_in, fox_q_norm_w, fox_k_norm_w, fox_f_bias, rwkv_mu, rwkv_w0,
              rwkv_w_up, rwkv_a0, rwkv_a_up, rwkv_g_up, rwkv_k_k, rwkv_k_a, rwkv_r_k, rwkv_ln_w, rwkv_ln_b,
              w_out, norm_ffn_w, peer_w_query, peer_sub_keys, peer_u, peer_v)
    for l in range(w_ada.shape[0]):
        x = _layer(x, c, *[p[l] for p in params])
    return x
```

```python
import functools

import jax
import jax.numpy as jnp
from jax import lax
from jax.experimental import pallas as pl
from jax.experimental.pallas import tpu as pltpu

F32 = jnp.float32
BF16 = jnp.bfloat16
HIGHEST = lax.Precision.HIGHEST

LANES = 128
BF16_ROWS = 16
NORM_EPS = 1e-6
GN_EPS = 64e-5
CHUNK = 64
FOX_HEADS = 8
FOX_HEAD_DIM = 128
RWKV_HEADS = 16
RWKV_HEAD_DIM = 64
PEER_HEADS = 8
PEER_NKEYS = 128
PEER_TOPK = 16
PEER_EXPERT_BLOCK = 1024
PEER_SLABS = PEER_EXPERT_BLOCK // PEER_NKEYS
NEG_BIG = -1e30
LOG2_E = 1.4426950408889634
VMEM_LIMIT = 56 * 1024 * 1024


def _cparams(sem):
    return pltpu.CompilerParams(dimension_semantics=sem, vmem_limit_bytes=VMEM_LIMIT)


def _dot(a, b, precision=None):
    return jnp.dot(a, b, preferred_element_type=F32, precision=precision)


def _dot_nt(a, b, precision=None):
    return lax.dot_general(a, b, (((1,), (1,)), ((), ())), preferred_element_type=F32,
                           precision=precision)


def _split_bf16(x):
    hi = x.astype(BF16)
    return hi, (x - hi.astype(F32)).astype(BF16)


def _dot_bf16x3(a, b):
    ah, al = _split_bf16(a)
    bh, bl = _split_bf16(b)
    return _dot(ah, bh) + (_dot(ah, bl) + _dot(al, bh))


def _dot_tn(a, b, precision=None):
    return lax.dot_general(a, b, (((0,), (0,)), ((), ())), preferred_element_type=F32,
                           precision=precision)


def _ada_kernel(c_ref, w_ref, b_ref, o_ref):
    c = c_ref[...]
    s = c * jax.nn.sigmoid(c)
    o_ref[...] = _dot(s, w_ref[...], HIGHEST) + b_ref[...]


def _ada_mod(c_pad, w_ada, b_ada):
    rows, d = c_pad.shape
    n = w_ada.shape[1]
    bn = 1024
    return pl.pallas_call(
        _ada_kernel,
        grid=(n // bn,),
        in_specs=[pl.BlockSpec((rows, d), lambda j: (0, 0)),
                  pl.BlockSpec((d, bn), lambda j: (0, j)),
                  pl.BlockSpec((1, bn), lambda j: (0, j))],
        out_specs=pl.BlockSpec((rows, bn), lambda j: (0, j)),
        out_shape=jax.ShapeDtypeStruct((rows, n), F32),
        compiler_params=_cparams(("arbitrary",)),
        name="ada_mod",
    )(c_pad, w_ada, b_ada.reshape(1, n))


def _modulated_norm(x, nw, sc, sh):
    y = x * lax.rsqrt(jnp.mean(x * x, axis=-1, keepdims=True) + NORM_EPS)
    return y * nw * (1.0 + sc) + sh


def _norm_mod_kernel(x_ref, nw_ref, sc_ref, sh_ref, o_ref):
    o_ref[...] = _modulated_norm(x_ref[...], nw_ref[...], sc_ref[0], sh_ref[0]).astype(BF16)


def _norm_mod(x2, norm_w, sc, sh, tokens_per_batch):
    t, d = x2.shape
    tm = min(512, tokens_per_batch)
    tpb = tokens_per_batch // tm
    return pl.pallas_call(
        _norm_mod_kernel,
        grid=(t // tm,),
        in_specs=[pl.BlockSpec((tm, d), lambda i: (i, 0)),
                  pl.BlockSpec((1, d), lambda i: (0, 0)),
                  pl.BlockSpec((1, 1, d), lambda i: (i // tpb, 0, 0)),
                  pl.BlockSpec((1, 1, d), lambda i: (i // tpb, 0, 0))],
        out_specs=pl.BlockSpec((tm, d), lambda i: (i, 0)),
        out_shape=jax.ShapeDtypeStruct((t, d), BF16),
        compiler_params=_cparams(("parallel",)),
        name="norm_mod",
    )(x2, norm_w, sc, sh)


def _in_proj_kernel(h_ref, w_ref, hw_ref, o_ref, *, n_qk_blocks):
    j = pl.program_id(1)
    acc = _dot(h_ref[...], w_ref[...])

    @pl.when(j < n_qk_blocks)
    def _():
        for hh in range(acc.shape[1] // FOX_HEAD_DIM):
            sl = slice(hh * FOX_HEAD_DIM, (hh + 1) * FOX_HEAD_DIM)
            a = acc[:, sl]
            rs = lax.rsqrt(jnp.mean(a * a, axis=-1, keepdims=True) + NORM_EPS)
            o_ref[:, sl] = a * rs * hw_ref[:, sl]

    @pl.when(j >= n_qk_blocks)
    def _():
        o_ref[...] = acc


def _in_proj(h_bf, w_bf, head_w, n_qk_cols):
    t, d = h_bf.shape
    n = w_bf.shape[1]
    tm = min(2048, t)
    bn = 512
    kern = functools.partial(_in_proj_kernel, n_qk_blocks=n_qk_cols // bn)
    return pl.pallas_call(
        kern,
        grid=(t // tm, n // bn),
        in_specs=[pl.BlockSpec((tm, d), lambda i, j: (i, 0)),
                  pl.BlockSpec((d, bn), lambda i, j: (0, j)),
                  pl.BlockSpec((1, bn), lambda i, j: (0, j))],
        out_specs=pl.BlockSpec((tm, bn), lambda i, j: (i, j)),
        out_shape=jax.ShapeDtypeStruct((t, n), F32),
        compiler_params=_cparams(("parallel", "arbitrary")),
        name="in_proj",
    )(h_bf, w_bf, head_w)


def _log_sigmoid(x):
    return jnp.minimum(x, 0.0) - jnp.log(1.0 + jnp.exp(-jnp.abs(x)))


def _fox_cum_kernel(f_ref, b_ref, o_ref, *, blk):
    s = f_ref.shape[0]
    row = lax.broadcasted_iota(jnp.int32, (blk, blk), 0)
    col = lax.broadcasted_iota(jnp.int32, (blk, blk), 1)
    tri = (row >= col).astype(F32)
    carry = jnp.zeros((1, f_ref.shape[1]), F32)
    for i in range(s // blk):
        lf = _log_sigmoid(f_ref[i * blk:(i + 1) * blk, :] + b_ref[...])
        cs = _dot(tri, lf, HIGHEST) + carry
        o_ref[i * blk:(i + 1) * blk, :] = cs * LOG2_E
        carry = cs[blk - 1:blk, :]


def _fox_cum(proj, f_bias_row, batch, seq, col_block):
    blk = min(256, seq)
    return pl.pallas_call(
        functools.partial(_fox_cum_kernel, blk=blk),
        grid=(batch,),
        in_specs=[pl.BlockSpec((seq, LANES), lambda b: (b, col_block)),
                  pl.BlockSpec((1, LANES), lambda b: (0, 0))],
        out_specs=pl.BlockSpec((seq, LANES), lambda b: (b, 0)),
        out_shape=jax.ShapeDtypeStruct((batch * seq, LANES), F32),
        compiler_params=_cparams(("parallel",)),
        name="fox_cum",
    )(proj, f_bias_row)


def _fox_attn_kernel(qt_ref, kt_ref, q_ref, k_ref, v_ref, c_ref, o_ref, m_scr, l_scr, acc_scr):
    t = pl.program_id(2)
    qi = qt_ref[t]
    ki = kt_ref[t]
    hd = FOX_HEAD_DIM
    n_heads = q_ref.shape[1] // hd

    @pl.when(ki == 0)
    def _():
        m_scr[...] = jnp.full(m_scr.shape, NEG_BIG, F32)
        l_scr[...] = jnp.zeros(l_scr.shape, F32)
        acc_scr[...] = jnp.zeros(acc_scr.shape, F32)

    def step(masked):
        hs = range(n_heads)
        sl = [slice(h * hd, (h + 1) * hd) for h in hs]
        s = [_dot_nt(q_ref[:, sl[h]].astype(BF16), k_ref[:, sl[h]].astype(BF16)) - c_ref[0, h] for h in hs]
        if masked:
            row = lax.broadcasted_iota(jnp.int32, s[0].shape, 0)
            col = lax.broadcasted_iota(jnp.int32, s[0].shape, 1)
            s = [jnp.where(row >= col, x, NEG_BIG) for x in s]
        m_prev = [m_scr[h] for h in hs]
        m_new = [jnp.maximum(m_prev[h], jnp.max(s[h], axis=-1, keepdims=True)) for h in hs]
        n_rep = s[0].shape[1] // hd
        p = [jnp.exp2(s[h] - jnp.concatenate([m_new[h]] * n_rep, axis=1)) for h in hs]
        alpha = [jnp.exp2(m_prev[h] - m_new[h]) for h in hs]
        ones = jnp.ones((k_ref.shape[0], hd), BF16)
        pv = [_dot(p[h].astype(BF16), jnp.concatenate([v_ref[:, sl[h]].astype(BF16), ones], axis=1))
              for h in hs]
        for h in hs:
            l_scr[h] = alpha[h] * l_scr[h] + pv[h][:, hd:]
            acc_scr[:, sl[h]] = alpha[h] * acc_scr[:, sl[h]] + pv[h][:, :hd]
            m_scr[h] = m_new[h]

    @pl.when(ki < qi)
    def _():
        step(False)

    @pl.when(ki == qi)
    def _():
        step(True)
        for h in range(n_heads):
            sl = slice(h * hd, (h + 1) * hd)
            o_ref[:, sl] = acc_scr[:, sl] / l_scr[h]


def _fox_attn(proj, cum_rows, batch, seq):
    tq = min(512, seq)
    nq = seq // tq
    hps = 4
    w = hps * FOX_HEAD_DIM
    kcol = FOX_HEADS // hps
    tri = [(q, k) for q in range(nq) for k in range(q + 1)]
    qt = jnp.asarray([q for q, _ in tri], jnp.int32)
    kt = jnp.asarray([k for _, k in tri], jnp.int32)
    grid_spec = pltpu.PrefetchScalarGridSpec(
        num_scalar_prefetch=2,
        grid=(batch, kcol, len(tri)),
        in_specs=[
            pl.BlockSpec((tq, w), lambda b, h, t, qt, kt: (b * nq + qt[t], h)),
            pl.BlockSpec((tq, w), lambda b, h, t, qt, kt: (b * nq + kt[t], kcol + h)),
            pl.BlockSpec((tq, w), lambda b, h, t, qt, kt: (b * nq + kt[t], 2 * kcol + h)),
            pl.BlockSpec((1, hps, 1, tq), lambda b, h, t, qt, kt: (b, h, 0, kt[t])),
        ],
        out_specs=pl.BlockSpec((tq, w), lambda b, h, t, qt, kt: (b * nq + qt[t], h)),
        scratch_shapes=[pltpu.VMEM((hps, tq, FOX_HEAD_DIM), F32), pltpu.VMEM((hps, tq, FOX_HEAD_DIM), F32),
                        pltpu.VMEM((tq, w), F32)],
    )
    return pl.pallas_call(
        _fox_attn_kernel,
        grid_spec=grid_spec,
        out_shape=jax.ShapeDtypeStruct((batch * seq, FOX_HEADS * FOX_HEAD_DIM), F32),
        compiler_params=_cparams(("parallel", "parallel", "arbitrary")),
        name="fox_attn",
    )(qt, kt, proj, proj, proj, cum_rows)


def _head_sum(x, bd):
    hi, lo = _split_bf16(x)
    parts = [_dot(hi[:, j * LANES:(j + 1) * LANES], bd) + _dot(lo[:, j * LANES:(j + 1) * LANES], bd)
             for j in range(x.shape[1] // LANES)]
    return jnp.concatenate(parts, axis=1)


def _head_block_diag():
    r = lax.broadcasted_iota(jnp.int32, (LANES, LANES), 0) // RWKV_HEAD_DIM
    c = lax.broadcasted_iota(jnp.int32, (LANES, LANES), 1) // RWKV_HEAD_DIM
    return (r == c).astype(BF16)


def _rwkv_prep_kernel(r_ref, k_ref, v_ref, gd_ref, lo_ref, pr_ref, pk_ref, pv_ref, pgd_ref, plo_ref,
                      mu_r, mu_k, mu_v, mu_gd, mu_lo, w0_ref, wup_ref, a0_ref, aup_ref, gup_ref,
                      kk_ref, ka_ref, rk_ref,
                      or_ref, olw_ref, ok_ref, ov_ref, oa_ref, ob_ref, og_ref, obonus_ref, *, tiles_per_batch):
    i = pl.program_id(0)
    first = (i % tiles_per_batch) == 0

    def shifted(cur_ref, prv_ref, mu_ref):
        cur = cur_ref[...]
        last = jnp.where(first, 0.0, prv_ref[7:8, :])
        row = lax.broadcasted_iota(jnp.int32, cur.shape, 0)
        prev = jnp.where(row == 0, last, pltpu.roll(cur, 1, 0))
        return cur + (prev - cur) * mu_ref[...]

    r = shifted(r_ref, pr_ref, mu_r)
    k = shifted(k_ref, pk_ref, mu_k)
    v = shifted(v_ref, pv_ref, mu_v)
    gd = shifted(gd_ref, pgd_ref, mu_gd)
    lo = shifted(lo_ref, plo_ref, mu_lo)
    wd = lo[:, :LANES]
    ad = lo[:, LANES:]

    w_pre = w0_ref[...] + _dot_bf16x3(jnp.tanh(wd), wup_ref[...])
    w_raw = _log_sigmoid(w_pre) - 0.5
    log_decay = -jnp.exp(w_raw)
    a = jax.nn.sigmoid(a0_ref[...] + _dot_bf16x3(ad, aup_ref[...]))
    g = _dot_bf16x3(jax.nn.sigmoid(gd), gup_ref[...])

    bd = _head_block_diag()
    kk = k * kk_ref[...]
    nrm = jnp.maximum(jnp.sqrt(_head_sum(kk * kk, bd)), 1e-12)
    kk = kk / nrm
    k_mod = k * (1.0 + (a - 1.0) * ka_ref[...])
    bonus = _head_sum(r * k_mod * rk_ref[...], bd) * v

    or_ref[...] = r.astype(BF16)
    olw_ref[...] = log_decay
    ok_ref[...] = k_mod.astype(BF16)
    ov_ref[...] = v.astype(BF16)
    oa_ref[...] = (-kk).astype(BF16)
    ob_ref[...] = (kk * a).astype(BF16)
    og_ref[...] = g.astype(BF16)
    obonus_ref[...] = bonus.astype(BF16)


def _rwkv_prep(proj, cols, mus, w0, w_up, a0, a_up, g_up, k_k, k_a, r_k, tokens_per_batch):
    t = proj.shape[0]
    w = RWKV_HEADS * RWKV_HEAD_DIM
    tm = min(512, tokens_per_batch)
    tpb = tokens_per_batch // tm
    widths = [w, w, w, 256, 256]
    offs = [cols["rr"], cols["rk"], cols["rv"], cols["gd"], cols["lo"]]
    cur_specs = [pl.BlockSpec((tm, wd), functools.partial(lambda i, cb: (i, cb), cb=o // wd))
                 for wd, o in zip(widths, offs)]
    prv_specs = [pl.BlockSpec((8, wd), functools.partial(
        lambda i, cb: (jnp.maximum(i * (tm // 8) - 1, 0), cb), cb=o // wd))
        for wd, o in zip(widths, offs)]
    full = lambda a: pl.BlockSpec(a.shape, lambda i: (0,) * a.ndim)
    params = list(mus) + [w0, w_up, a0, a_up, g_up, k_k, k_a, r_k]
    out_spec = pl.BlockSpec((tm, w), lambda i: (i, 0))
    return pl.pallas_call(
        functools.partial(_rwkv_prep_kernel, tiles_per_batch=tpb),
        grid=(t // tm,),
        in_specs=cur_specs + prv_specs + [full(p) for p in params],
        out_specs=[out_spec] * 8,
        out_shape=[jax.ShapeDtypeStruct((t, w), dt) for dt in (BF16, F32, BF16, BF16, BF16, BF16, BF16, BF16)],
        compiler_params=_cparams(("parallel",)),
        name="rwkv_prep",
    )(*([proj] * 10), *params)


def _rwkv_scan_kernel(r_ref, lw_ref, k_ref, v_ref, a_ref, b_ref, o_ref, h_scr):
    c = pl.program_id(1)

    @pl.when(c == 0)
    def _():
        h_scr[...] = jnp.zeros(h_scr.shape, F32)

    C = r_ref.shape[0]
    n_pairs = r_ref.shape[1] // LANES
    P = HIGHEST

    row = lax.broadcasted_iota(jnp.int32, (C, C), 0)
    col = lax.broadcasted_iota(jnp.int32, (C, C), 1)
    tri = (row >= col).astype(F32)
    lw = lw_ref[...]
    cw = _dot(tri, lw, P)
    cw_end = cw[C - 1:C, :]
    e_pos = jnp.exp(cw)
    e_prev = jnp.exp(cw - lw)
    e_neg = jnp.exp(-cw)
    e_end = jnp.exp(cw_end - cw)
    w_end = jnp.exp(cw_end)

    a = a_ref[...].astype(F32)
    b = b_ref[...].astype(F32)
    k = k_ref[...].astype(F32)
    r = r_ref[...].astype(F32)
    at = a * e_prev
    bt = b * e_neg
    kt = k * e_neg
    rt = r * e_pos
    bh = b * e_end
    kh = k * e_end
    v = v_ref[...].astype(F32)

    lane = lax.broadcasted_iota(jnp.int32, (C, LANES), 1)
    head0 = lane < RWKV_HEAD_DIM
    r2 = lax.broadcasted_iota(jnp.int32, (2 * C, 2 * C), 0)
    c2 = lax.broadcasted_iota(jnp.int32, (2 * C, 2 * C), 1)
    same = (r2 // C) == (c2 // C)
    strict = same & (r2 > c2)
    incl = same & (r2 >= c2)
    eye = (r2 == c2).astype(F32)

    def two(x, p):
        xp = x[:, p * LANES:(p + 1) * LANES]
        return jnp.concatenate([jnp.where(head0, xp, 0.0), jnp.where(head0, 0.0, xp)], axis=0)

    def mm(x, y):
        return _dot(x.astype(BF16), y.astype(BF16))

    def split(x):
        hi = x.astype(BF16)
        return hi, (x - hi.astype(F32)).astype(BF16)

    def mm3(x, y):
        xh, xl = split(x)
        yh, yl = split(y)
        return _dot(xh, yh) + (_dot(xh, yl) + _dot(xl, yh))

    G = 2 * C
    pairs = range(n_pairs)
    at2 = [two(at, p) for p in pairs]
    rt2 = [two(rt, p) for p in pairs]
    v2 = [two(v, p).astype(BF16) for p in pairs]
    gram = [_dot_nt(jnp.concatenate([at2[p], rt2[p]], axis=0).astype(BF16),
                    jnp.concatenate([two(bt, p), two(kt, p)], axis=0).astype(BF16)) for p in pairs]
    a_ab = [jnp.where(strict, gram[p][:G, :G], 0.0) for p in pairs]
    nmat = list(a_ab)
    pw = list(a_ab)
    for _ in range(max(1, (C - 1).bit_length() - 1)):
        pw = [mm3(pw[p], pw[p]) for p in pairs]
        nmat = [nmat[p] + pw[p] + mm3(nmat[p], pw[p]) for p in pairs]
    akv = [mm(jnp.where(strict, gram[p][:G, G:], 0.0), v2[p]) for p in pairs]
    rhs = [jnp.concatenate([at2[p], akv[p]], axis=1) for p in pairs]
    pq = [(rhs[p] + mm(nmat[p], rhs[p])).astype(BF16) for p in pairs]
    ry = [mm(jnp.where(incl, gram[p][G:, :G], 0.0), pq[p]) for p in pairs]
    mv = [mm(jnp.where(incl, gram[p][G:, G:], 0.0), v2[p]) for p in pairs]
    ge = [_dot_tn(two(bh, p).astype(BF16), pq[p]) for p in pairs]
    kv = [_dot_tn(two(kh, p).astype(BF16), v2[p]) for p in pairs]
    for p in pairs:
        rr = rt2[p] + ry[p][:, :LANES]
        gm = eye * w_end[:, p * LANES:(p + 1) * LANES] + ge[p][:, :LANES]
        yh = mm(jnp.concatenate([rr, gm], axis=0), h_scr[p])
        y2 = yh[:G] + ry[p][:, LANES:] + mv[p]
        h_scr[p] = yh[G:] + ge[p][:, LANES:] + kv[p]
        o_ref[:, p * LANES:(p + 1) * LANES] = y2[:C, :] + y2[C:, :]


def _rwkv_scan(r, lw, k, v, a, b, batch, seq):
    w = r.shape[1]
    nc = seq // CHUNK
    spec = pl.BlockSpec((CHUNK, w), lambda bi, ci: (bi * nc + ci, 0))
    return pl.pallas_call(
        _rwkv_scan_kernel,
        grid=(batch, nc),
        in_specs=[spec] * 6,
        out_specs=spec,
        out_shape=jax.ShapeDtypeStruct((batch * seq, w), F32),
        scratch_shapes=[pltpu.VMEM((w // LANES, LANES, LANES), F32)],
        compiler_params=_cparams(("parallel", "arbitrary")),
        name="rwkv_scan",
    )(r, lw, k, v, a, b)


def _mix_out_kernel(o_ref, gate_ref, y_ref, bonus_ref, g_ref, x_ref, g1_ref, lnw_ref, lnb_ref, w_ref, out_ref):
    fox = o_ref[...] * jax.nn.sigmoid(gate_ref[...])
    bd = _head_block_diag()
    y = y_ref[...]
    inv_n = 1.0 / RWKV_HEAD_DIM
    mean = _head_sum(y, bd) * inv_n
    d = y - mean
    var = _head_sum(d * d, bd) * inv_n
    yn = d * lax.rsqrt(var + GN_EPS) * lnw_ref[...] + lnb_ref[...]
    rw = (yn + bonus_ref[...].astype(F32)) * g_ref[...].astype(F32)
    wf = fox.shape[1]
    mix = _dot(fox.astype(BF16), w_ref[:wf, :]) + _dot(rw.astype(BF16), w_ref[wf:, :])
    out_ref[...] = x_ref[...] + g1_ref[0] * mix


def _mix_out(o_fox, proj, gate_col, y, bonus, g, x2, g1, ln_w, ln_b, w_out_bf, tokens_per_batch):
    t, d = x2.shape
    wf = o_fox.shape[1]
    wr = y.shape[1]
    tm = min(512, tokens_per_batch)
    tpb = tokens_per_batch // tm
    return pl.pallas_call(
        _mix_out_kernel,
        grid=(t // tm,),
        in_specs=[pl.BlockSpec((tm, wf), lambda i: (i, 0)),
                  pl.BlockSpec((tm, wf), lambda i: (i, gate_col // wf)),
                  pl.BlockSpec((tm, wr), lambda i: (i, 0)),
                  pl.BlockSpec((tm, wr), lambda i: (i, 0)),
                  pl.BlockSpec((tm, wr), lambda i: (i, 0)),
                  pl.BlockSpec((tm, d), lambda i: (i, 0)),
                  pl.BlockSpec((1, 1, d), lambda i: (i // tpb, 0, 0)),
                  pl.BlockSpec((1, wr), lambda i: (0, 0)),
                  pl.BlockSpec((1, wr), lambda i: (0, 0)),
                  pl.BlockSpec(w_out_bf.shape, lambda i: (0, 0))],
        out_specs=pl.BlockSpec((tm, d), lambda i: (i, 0)),
        out_shape=jax.ShapeDtypeStruct((t, d), F32),
        compiler_params=_cparams(("parallel",)),
        name="mix_out",
    )(o_fox, proj, y, bonus, g, x2, g1, ln_w, ln_b, w_out_bf)


def _top_k_mask_rows(s, k, tie_safe, want_rank=True):
    n = s.shape[0]
    s0 = s
    row = lax.broadcasted_iota(jnp.int32, s.shape, 0).astype(F32)
    rank = jnp.full(s.shape, float(k), F32) if want_rank else None
    vals = []
    for r in range(k):
        m = jnp.max(s, axis=0, keepdims=True)
        hit = s == m
        if tie_safe:
            pos = jnp.min(jnp.where(hit, row, float(n)), axis=0, keepdims=True)
            hit = row == pos
        vals.append(m)
        if want_rank:
            rank = jnp.where(hit, float(r), rank)
        s = jnp.where(hit, -jnp.inf, s)
    picked = (s == -jnp.inf) & (s0 != -jnp.inf)
    n_picked = jnp.sum(jnp.where(picked, 1.0, 0.0), axis=0, keepdims=True)
    return jnp.concatenate(vals, axis=0), rank, picked, jnp.where(n_picked == float(k), 1.0, 0.0)


def _candidate_rows():
    k = PEER_TOPK
    groups, valid = [], []
    for r0 in range(k // 2):
        n1 = k // (r0 + 1)
        for g in range(-(-n1 // 8)):
            groups.append((r0, g * 8))
            valid.append([g * 8 + i < n1 for i in range(8)])
    groups.append((None, k // 2))
    valid.append([True] * 8)
    return groups, valid


def _route_head(sa, sb, tie_safe):
    k = PEER_TOPK
    tm = sa.shape[1]
    groups, valid = _candidate_rows()
    sub8 = lax.broadcasted_iota(jnp.int32, (8, tm), 0)
    top_a, rank_a, _, clean_a = _top_k_mask_rows(sa, k, tie_safe)
    top_b, rank_b, _, clean_b = _top_k_mask_rows(sb, k, tie_safe)
    parts = []
    for (r0, off), ok in zip(groups, valid):
        if r0 is None:
            part = top_a[off:off + 8, :] + top_b[0:1, :]
        else:
            part = top_a[r0:r0 + 1, :] + top_b[off:off + 8, :]
        if not all(ok):
            part = jnp.where(sub8 < sum(ok), part, -jnp.inf)
        parts.append(part)
    cand = jnp.concatenate(parts, axis=0)
    _, _, sel, clean_c = _top_k_mask_rows(cand, k, tie_safe, want_rank=False)
    z = jnp.sum(jnp.where(sel, jnp.exp(cand - cand[0:1, :]), 0.0), axis=0, keepdims=True)
    self32 = jnp.where(sel, 1.0, 0.0)
    n_keys = sa.shape[0]
    rank3 = rank_a.astype(BF16).reshape(n_keys // BF16_ROWS, BF16_ROWS, tm)
    length3 = jnp.zeros(rank3.shape, BF16)

    def add_count(r0, count_row):
        tile = jnp.broadcast_to(count_row, (BF16_ROWS, tm)).astype(BF16)
        return length3 + jnp.where(rank3 == float(r0), tile[None], 0.0).astype(BF16)

    tail = None
    per_rank = {}
    for gi, (r0, off) in enumerate(groups):
        cnt = self32[gi * 8:(gi + 1) * 8, :]
        if r0 is None:
            tail = cnt
        else:
            c = jnp.sum(cnt, axis=0, keepdims=True)
            per_rank[r0] = c if r0 not in per_rank else per_rank[r0] + c
    for r in range(8):
        per_rank[k // 2 + r] = tail[r:r + 1, :]
    for r0, c in per_rank.items():
        length3 = add_count(r0, c)
    length = length3.reshape(n_keys, tm).astype(F32)
    pa = jnp.exp(sa - top_a[0:1, :])
    qb = jnp.exp(sb - top_b[0:1, :]) / z
    return pa, length, qb, rank_b, clean_a * clean_b * clean_c


def _peer_route_kernel(x_ref, nw_ref, sc_ref, sh_ref, wq_ref, keys_ref,
                       h_ref, pa_ref, len_ref, qb_ref, rnk_ref):
    h = _modulated_norm(x_ref[...], nw_ref[...], sc_ref[0], sh_ref[0]).astype(BF16)
    h_ref[...] = h
    q = _dot(h, wq_ref[...])
    tm = q.shape[0]
    half = keys_ref.shape[-1]
    blocked = (PEER_NKEYS // PEER_SLABS, PEER_SLABS, tm)
    def scores(hd):
        qa = q[:, (hd * 2) * half:(hd * 2 + 1) * half]
        qb = q[:, (hd * 2 + 1) * half:(hd * 2 + 2) * half]
        return (_dot_nt(keys_ref[hd, 0], qa, HIGHEST),
                _dot_nt(keys_ref[hd, 1], qb, HIGHEST))

    def emit(hd, tables):
        pa, length, qbt, rank_b, _ = tables
        pa_ref[:, hd] = pa.reshape(blocked)
        len_ref[:, hd] = length.reshape(blocked)
        qb_ref[hd] = qbt.astype(BF16)
        rnk_ref[hd] = rank_b.astype(BF16)

    code = jnp.zeros((1, 1), F32)
    for hd in range(PEER_HEADS):
        fast = _route_head(*scores(hd), tie_safe=False)
        emit(hd, fast)
        dirty = 1.0 - jnp.min(fast[4], axis=1, keepdims=True)
        code = code + dirty * float(1 << hd)
    tie_bits = code[0, 0].astype(jnp.int32)
    for hd in range(PEER_HEADS):
        @pl.when(((tie_bits >> hd) & 1) == 1)
        def _(hd=hd):
            emit(hd, _route_head(*scores(hd), tie_safe=True))


def _peer_route(x1, norm_w, sc, sh, wq_bf, sub_keys, tokens_per_batch):
    t, d = x1.shape
    tm = min(256, tokens_per_batch)
    tpb = tokens_per_batch // tm
    n_blocks = PEER_NKEYS // PEER_SLABS
    dense = lambda dt: jax.ShapeDtypeStruct((PEER_HEADS, PEER_NKEYS, t), dt)
    dense_spec = pl.BlockSpec((PEER_HEADS, PEER_NKEYS, tm), lambda i: (0, 0, i))
    blocked = jax.ShapeDtypeStruct((n_blocks, PEER_HEADS, PEER_SLABS, t), F32)
    blocked_spec = pl.BlockSpec((n_blocks, PEER_HEADS, PEER_SLABS, tm), lambda i: (0, 0, 0, i))
    return pl.pallas_call(
        _peer_route_kernel,
        grid=(t // tm,),
        in_specs=[pl.BlockSpec((tm, d), lambda i: (i, 0)),
                  pl.BlockSpec((1, d), lambda i: (0, 0)),
                  pl.BlockSpec((1, 1, d), lambda i: (i // tpb, 0, 0)),
                  pl.BlockSpec((1, 1, d), lambda i: (i // tpb, 0, 0)),
                  pl.BlockSpec(wq_bf.shape, lambda i: (0, 0)),
                  pl.BlockSpec(sub_keys.shape, lambda i: (0, 0, 0, 0))],
        out_specs=[pl.BlockSpec((tm, d), lambda i: (i, 0)), blocked_spec, blocked_spec, dense_spec, dense_spec],
        out_shape=[jax.ShapeDtypeStruct((t, d), BF16), blocked, blocked, dense(BF16), dense(BF16)],
        compiler_params=_cparams(("parallel",)),
        name="peer_route",
    )(x1, norm_w, sc, sh, wq_bf, sub_keys)


def _gelu_exact(x):
    return 0.5 * x * (1.0 + lax.erf(x * 0.7071067811865476))


def _peer_expert_kernel(ht_ref, u_ref, vt_ref, pa_ref, len_ref, qb_ref, rnk_ref, x_ref, g2_ref, o_ref, acc_scr):
    j = pl.program_id(1)
    be = u_ref.shape[0]
    tm = ht_ref.shape[1]

    @pl.when(j == 0)
    def _():
        acc_scr[...] = jnp.zeros(acc_scr.shape, F32)

    n_slabs = be // PEER_NKEYS
    rows = BF16_ROWS
    n_groups = PEER_NKEYS // rows
    act = _gelu_exact(_dot(u_ref[...], ht_ref[...])).astype(BF16)

    def row_tile(ref, hd, s):
        return jnp.broadcast_to(ref[0, hd, s:s + 1, :], (rows, tm)).astype(BF16)

    zero = jnp.zeros((rows, tm), BF16)
    parts = [[None] * n_groups for _ in range(n_slabs)]
    slab_group = 8
    for s0 in range(0, n_slabs, slab_group):
        ss = range(s0, min(s0 + slab_group, n_slabs))
        ln = {(hd, s): row_tile(len_ref, hd, s) for hd in range(PEER_HEADS) for s in ss}
        pa = {(hd, s): row_tile(pa_ref, hd, s) for hd in range(PEER_HEADS) for s in ss}
        for g in range(n_groups):
            sl = slice(g * rows, (g + 1) * rows)
            acc = {}
            for hd in range(PEER_HEADS):
                rk = rnk_ref[hd, sl, :]
                qv = qb_ref[hd, sl, :]
                for s in ss:
                    term = jnp.where(rk < ln[hd, s], qv, zero) * pa[hd, s]
                    acc[s] = term if s not in acc else acc[s] + term
            for s in ss:
                lo = s * PEER_NKEYS + g * rows
                parts[s][g] = acc[s] * act[lo:lo + rows, :]
    p = jnp.concatenate([parts[s][g] for s in range(n_slabs) for g in range(n_groups)], axis=0)
    acc_scr[...] += _dot(vt_ref[0], p)

    @pl.when(j == pl.num_programs(1) - 1)
    def _():
        o_ref[...] = x_ref[...] + g2_ref[0] * acc_scr[...].T


def _peer_expert(ht_bf, u_bf, vt_bf, pa, ln, qb, rnk, x1, g2, tokens_per_batch):
    t, d = x1.shape
    ne = u_bf.shape[0]
    tm = min(512, tokens_per_batch)
    tpb = tokens_per_batch // tm
    be = PEER_EXPERT_BLOCK
    dense_spec = pl.BlockSpec((PEER_HEADS, PEER_NKEYS, tm), lambda i, j: (0, 0, i))
    blocked_spec = pl.BlockSpec((1, PEER_HEADS, PEER_SLABS, tm), lambda i, j: (j, 0, 0, i))
    return pl.pallas_call(
        _peer_expert_kernel,
        grid=(t // tm, ne // be),
        in_specs=[pl.BlockSpec((d, tm), lambda i, j: (0, i)),
                  pl.BlockSpec((be, d), lambda i, j: (j, 0)),
                  pl.BlockSpec((1, d, be), lambda i, j: (j, 0, 0)),
                  blocked_spec, blocked_spec, dense_spec, dense_spec,
                  pl.BlockSpec((tm, d), lambda i, j: (i, 0)),
                  pl.BlockSpec((1, 1, d), lambda i, j: (i // tpb, 0, 0))],
        out_specs=pl.BlockSpec((tm, d), lambda i, j: (i, 0)),
        out_shape=jax.ShapeDtypeStruct((t, d), F32),
        scratch_shapes=[pltpu.VMEM((d, tm), F32)],
        compiler_params=_cparams(("parallel", "arbitrary")),
        name="peer_expert",
    )(ht_bf, u_bf, vt_bf, pa, ln, qb, rnk, x1, g2)


def _pad_cols(a, n):
    return jnp.pad(a, ((0, 0), (0, n - a.shape[1])))


def _pad_rows(a, n):
    return jnp.pad(a, ((0, n - a.shape[0]), (0, 0)))


def _layer(x, c, w_ada, b_ada, norm_mix_w, w_in, fox_q_norm_w, fox_k_norm_w, fox_f_bias,
           rwkv_mu, rwkv_w0, rwkv_w_up, rwkv_a0, rwkv_a_up, rwkv_g_up, rwkv_k_k, rwkv_k_a,
           rwkv_r_k, rwkv_ln_w, rwkv_ln_b, w_out, norm_ffn_w, peer_w_query, peer_sub_keys,
           peer_u, peer_v):
    B, S, D = x.shape
    T = B * S
    fw = FOX_HEADS * FOX_HEAD_DIM
    rw = RWKV_HEADS * RWKV_HEAD_DIM
    w_lora = rwkv_w_up.shape[0]
    a_lora = rwkv_a_up.shape[0]
    g_lora = rwkv_g_up.shape[0]
    assert w_lora <= LANES and a_lora + FOX_HEADS <= LANES and g_lora == 256

    c_pad = _pad_rows(c, 8)
    mod = _ada_mod(c_pad, w_ada, b_ada)[:B]
    sh1, sc1, g1, sh2, sc2, g2 = [m.reshape(B, 1, D) for m in jnp.split(mod, 6, axis=-1)]

    fox_cols = 4 * fw + FOX_HEADS
    wi_fox, wi_rw = w_in[:, :fox_cols], w_in[:, fox_cols:]
    mu = rwkv_mu.reshape(1, -1)
    seg = lambda a, lo, n: a[:, lo:lo + n]
    w_perm = jnp.concatenate([
        seg(wi_fox, 0, 4 * fw),
        seg(wi_rw, 0, 3 * rw),
        seg(wi_rw, 3 * rw + w_lora + a_lora, g_lora),
        _pad_cols(seg(wi_rw, 3 * rw, w_lora), LANES),
        _pad_cols(jnp.concatenate([seg(wi_rw, 3 * rw + w_lora, a_lora), seg(wi_fox, 4 * fw, FOX_HEADS)], 1), LANES),
    ], axis=1).astype(BF16)
    cols = {"gate": 3 * fw, "rr": 4 * fw, "rk": 4 * fw + rw, "rv": 4 * fw + 2 * rw,
            "gd": 4 * fw + 3 * rw, "lo": 4 * fw + 3 * rw + g_lora}
    f_lane = a_lora
    f_block = (cols["lo"] + LANES) // LANES
    mus = [seg(mu, 0, rw), seg(mu, rw, rw), seg(mu, 2 * rw, rw),
           seg(mu, 3 * rw + w_lora + a_lora, g_lora),
           jnp.concatenate([_pad_cols(seg(mu, 3 * rw, w_lora), LANES),
                            _pad_cols(seg(mu, 3 * rw + w_lora, a_lora), LANES)], 1)]
    scale = FOX_HEAD_DIM ** -0.5 * LOG2_E
    head_w = _pad_cols(jnp.concatenate([jnp.tile(fox_q_norm_w * scale, FOX_HEADS),
                                        jnp.tile(fox_k_norm_w, FOX_HEADS)]).reshape(1, -1), w_perm.shape[1])

    x2 = x.reshape(T, D)
    proj = _in_proj(_norm_mod(x2, norm_mix_w.reshape(1, D), sc1, sh1, S), w_perm, head_w, 2 * fw)

    f_bias_row = jnp.zeros((1, LANES), F32).at[0, f_lane:f_lane + FOX_HEADS].set(fox_f_bias)
    cum = _fox_cum(proj, f_bias_row, B, S, f_block)
    cum_rows = cum.reshape(B, S, LANES)[:, :, f_lane:f_lane + FOX_HEADS].transpose(0, 2, 1).reshape(B, FOX_HEADS, 1, S)
    o_fox = _fox_attn(proj, cum_rows, B, S)

    row = lambda a: a.reshape(1, -1)
    r, lw, k, v, a_vec, b_vec, g, bonus = _rwkv_prep(
        proj, cols, mus, row(rwkv_w0), _pad_rows(rwkv_w_up, LANES), row(rwkv_a0), _pad_rows(rwkv_a_up, LANES),
        rwkv_g_up, row(rwkv_k_k), row(rwkv_k_a), row(rwkv_r_k), S)
    y = _rwkv_scan(r, lw, k, v, a_vec, b_vec, B, S)

    x1 = _mix_out(o_fox, proj, cols["gate"], y, bonus, g, x2, g1, row(rwkv_ln_w), row(rwkv_ln_b),
                  w_out.astype(BF16), S)

    h2, pa, ln, qb, rnk = _peer_route(x1, norm_ffn_w.reshape(1, D), sc2, sh2,
                                      peer_w_query.astype(BF16), peer_sub_keys, S)
    vt = peer_v.astype(BF16).reshape(-1, PEER_EXPERT_BLOCK, D).transpose(0, 2, 1)
    out = _peer_expert(h2.T, peer_u.astype(BF16), vt, pa, ln, qb, rnk, x1, g2, S)
    return out.reshape(B, S, D)


def kernel(x, c, w_ada, b_ada, norm_mix_w, w_in, fox_q_norm_w, fox_k_norm_w, fox_f_bias, rwkv_mu, rwkv_w0,
           rwkv_w_up, rwkv_a0, rwkv_a_up, rwkv_g_up, rwkv_k_k, rwkv_k_a, rwkv_r_k, rwkv_ln_w, rwkv_ln_b,
           w_out, norm_ffn_w, peer_w_query, peer_sub_keys, peer_u, peer_v):
    params = (w_ada, b_ada, norm_mix_w, w_in, fox_q_norm_w, fox_k_norm_w, fox_f_bias, rwkv_mu, rwkv_w0,
              rwkv_w_up, rwkv_a0, rwkv_a_up, rwkv_g_up, rwkv_k_k, rwkv_k_a, rwkv_r_k, rwkv_ln_w, rwkv_ln_b,
              w_out, norm_ffn_w, peer_w_query, peer_sub_keys, peer_u, peer_v)
    for l in range(w_ada.shape[0]):
        x = _layer(x, c, *[p[l] for p in params])
    return x
```

```python
import functools

import jax
import jax.numpy as jnp
from jax import lax
from jax.experimental import pallas as pl
from jax.experimental.pallas import tpu as pltpu

F32 = jnp.float32
BF16 = jnp.bfloat16
HIGHEST = lax.Precision.HIGHEST

LANES = 128
BF16_ROWS = 16
NORM_EPS = 1e-6
GN_EPS = 64e-5
CHUNK = 64
FOX_HEADS = 8
FOX_HEAD_DIM = 128
RWKV_HEADS = 16
RWKV_HEAD_DIM = 64
PEER_HEADS = 8
PEER_NKEYS = 128
PEER_TOPK = 16
PEER_EXPERT_BLOCK = 1024
PEER_SLABS = PEER_EXPERT_BLOCK // PEER_NKEYS
NEG_BIG = -1e30
LOG2_E = 1.4426950408889634
VMEM_LIMIT = 56 * 1024 * 1024


def _cparams(sem):
    return pltpu.CompilerParams(dimension_semantics=sem, vmem_limit_bytes=VMEM_LIMIT)


def _dot(a, b, precision=None):
    return jnp.dot(a, b, preferred_element_type=F32, precision=precision)


def _dot_nt(a, b, precision=None):
    return lax.dot_general(a, b, (((1,), (1,)), ((), ())), preferred_element_type=F32,
                           precision=precision)


def _split_bf16(x):
    hi = x.astype(BF16)
    return hi, (x - hi.astype(F32)).astype(BF16)


def _dot_bf16x3(a, b):
    ah, al = _split_bf16(a)
    bh, bl = _split_bf16(b)
    return _dot(ah, bh) + (_dot(ah, bl) + _dot(al, bh))


def _dot_tn(a, b, precision=None):
    return lax.dot_general(a, b, (((0,), (0,)), ((), ())), preferred_element_type=F32,
                           precision=precision)


def _ada_kernel(c_ref, w_ref, b_ref, o_ref):
    c = c_ref[...]
    s = c * jax.nn.sigmoid(c)
    o_ref[...] = _dot(s, w_ref[...], HIGHEST) + b_ref[...]


def _ada_mod(c_pad, w_ada, b_ada):
    rows, d = c_pad.shape
    n = w_ada.shape[1]
    bn = 1024
    return pl.pallas_call(
        _ada_kernel,
        grid=(n // bn,),
        in_specs=[pl.BlockSpec((rows, d), lambda j: (0, 0)),
                  pl.BlockSpec((d, bn), lambda j: (0, j)),
                  pl.BlockSpec((1, bn), lambda j: (0, j))],
        out_specs=pl.BlockSpec((rows, bn), lambda j: (0, j)),
        out_shape=jax.ShapeDtypeStruct((rows, n), F32),
        compiler_params=_cparams(("arbitrary",)),
        name="ada_mod",
    )(c_pad, w_ada, b_ada.reshape(1, n))


def _modulated_norm(x, nw, sc, sh):
    y = x * lax.rsqrt(jnp.mean(x * x, axis=-1, keepdims=True) + NORM_EPS)
    return y * nw * (1.0 + sc) + sh


def _norm_mod_kernel(x_ref, nw_ref, sc_ref, sh_ref, o_ref):
    o_ref[...] = _modulated_norm(x_ref[...], nw_ref[...], sc_ref[0], sh_ref[0]).astype(BF16)


def _norm_mod(x2, norm_w, sc, sh, tokens_per_batch):
    t, d = x2.shape
    tm = min(512, tokens_per_batch)
    tpb = tokens_per_batch // tm
    return pl.pallas_call(
        _norm_mod_kernel,
        grid=(t // tm,),
        in_specs=[pl.BlockSpec((tm, d), lambda i: (i, 0)),
                  pl.BlockSpec((1, d), lambda i: (0, 0)),
                  pl.BlockSpec((1, 1, d), lambda i: (i // tpb, 0, 0)),
                  pl.BlockSpec((1, 1, d), lambda i: (i // tpb, 0, 0))],
        out_specs=pl.BlockSpec((tm, d), lambda i: (i, 0)),
        out_shape=jax.ShapeDtypeStruct((t, d), BF16),
        compiler_params=_cparams(("parallel",)),
        name="norm_mod",
    )(x2, norm_w, sc, sh)


def _in_proj_kernel(h_ref, w_ref, hw_ref, o_ref, *, n_qk_blocks):
    j = pl.program_id(1)
    acc = _dot(h_ref[...], w_ref[...])

    @pl.when(j < n_qk_blocks)
    def _():
        for hh in range(acc.shape[1] // FOX_HEAD_DIM):
            sl = slice(hh * FOX_HEAD_DIM, (hh + 1) * FOX_HEAD_DIM)
            a = acc[:, sl]
            rs = lax.rsqrt(jnp.mean(a * a, axis=-1, keepdims=True) + NORM_EPS)
            o_ref[:, sl] = a * rs * hw_ref[:, sl]

    @pl.when(j >= n_qk_blocks)
    def _():
        o_ref[...] = acc


def _in_proj(h_bf, w_bf, head_w, n_qk_cols):
    t, d = h_bf.shape
    n = w_bf.shape[1]
    tm = min(2048, t)
    bn = 512
    kern = functools.partial(_in_proj_kernel, n_qk_blocks=n_qk_cols // bn)
    return pl.pallas_call(
        kern,
        grid=(t // tm, n // bn),
        in_specs=[pl.BlockSpec((tm, d), lambda i, j: (i, 0)),
                  pl.BlockSpec((d, bn), lambda i, j: (0, j)),
                  pl.BlockSpec((1, bn), lambda i, j: (0, j))],
        out_specs=pl.BlockSpec((tm, bn), lambda i, j: (i, j)),
        out_shape=jax.ShapeDtypeStruct((t, n), F32),
        compiler_params=_cparams(("parallel", "arbitrary")),
        name="in_proj",
    )(h_bf, w_bf, head_w)


def _log_sigmoid(x):
    return jnp.minimum(x, 0.0) - jnp.log(1.0 + jnp.exp(-jnp.abs(x)))


def _fox_cum_kernel(f_ref, b_ref, o_ref, *, blk):
    s = f_ref.shape[0]
    row = lax.broadcasted_iota(jnp.int32, (blk, blk), 0)
    col = lax.broadcasted_iota(jnp.int32, (blk, blk), 1)
    tri = (row >= col).astype(F32)
    carry = jnp.zeros((1, f_ref.shape[1]), F32)
    for i in range(s // blk):
        lf = _log_sigmoid(f_ref[i * blk:(i + 1) * blk, :] + b_ref[...])
        cs = _dot(tri, lf, HIGHEST) + carry
        o_ref[i * blk:(i + 1) * blk, :] = cs * LOG2_E
        carry = cs[blk - 1:blk, :]


def _fox_cum(proj, f_bias_row, batch, seq, col_block):
    blk = min(256, seq)
    return pl.pallas_call(
        functools.partial(_fox_cum_kernel, blk=blk),
        grid=(batch,),
        in_specs=[pl.BlockSpec((seq, LANES), lambda b: (b, col_block)),
                  pl.BlockSpec((1, LANES), lambda b: (0, 0))],
        out_specs=pl.BlockSpec((seq, LANES), lambda b: (b, 0)),
        out_shape=jax.ShapeDtypeStruct((batch * seq, LANES), F32),
        compiler_params=_cparams(("parallel",)),
        name="fox_cum",
    )(proj, f_bias_row)


def _fox_attn_kernel(qt_ref, kt_ref, q_ref, k_ref, v_ref, c_ref, o_ref, m_scr, l_scr, acc_scr):
    t = pl.program_id(2)
    qi = qt_ref[t]
    ki = kt_ref[t]
    hd = FOX_HEAD_DIM
    n_heads = q_ref.shape[1] // hd

    @pl.when(ki == 0)
    def _():
        m_scr[...] = jnp.full(m_scr.shape, NEG_BIG, F32)
        l_scr[...] = jnp.zeros(l_scr.shape, F32)
        acc_scr[...] = jnp.zeros(acc_scr.shape, F32)

    def step(masked):
        hs = range(n_heads)
        sl = [slice(h * hd, (h + 1) * hd) for h in hs]
        s = [_dot_nt(q_ref[:, sl[h]].astype(BF16), k_ref[:, sl[h]].astype(BF16)) - c_ref[0, h] for h in hs]
        if masked:
            row = lax.broadcasted_iota(jnp.int32, s[0].shape, 0)
            col = lax.broadcasted_iota(jnp.int32, s[0].shape, 1)
            s = [jnp.where(row >= col, x, NEG_BIG) for x in s]
        m_prev = [m_scr[h] for h in hs]
        m_new = [jnp.maximum(m_prev[h], jnp.max(s[h], axis=-1, keepdims=True)) for h in hs]
        n_rep = s[0].shape[1] // hd
        p = [jnp.exp2(s[h] - jnp.concatenate([m_new[h]] * n_rep, axis=1)) for h in hs]
        alpha = [jnp.exp2(m_prev[h] - m_new[h]) for h in hs]
        ones = jnp.ones((k_ref.shape[0], hd), BF16)
        pv = [_dot(p[h].astype(BF16), jnp.concatenate([v_ref[:, sl[h]].astype(BF16), ones], axis=1))
              for h in hs]
        for h in hs:
            l_scr[h] = alpha[h] * l_scr[h] + pv[h][:, hd:]
            acc_scr[:, sl[h]] = alpha[h] * acc_scr[:, sl[h]] + pv[h][:, :hd]
            m_scr[h] = m_new[h]

    @pl.when(ki < qi)
    def _():
        step(False)

    @pl.when(ki == qi)
    def _():
        step(True)
        for h in range(n_heads):
            sl = slice(h * hd, (h + 1) * hd)
            o_ref[:, sl] = acc_scr[:, sl] / l_scr[h]


def _fox_attn(proj, cum_rows, batch, seq):
    tq = min(512, seq)
    nq = seq // tq
    hps = 4
    w = hps * FOX_HEAD_DIM
    kcol = FOX_HEADS // hps
    tri = [(q, k) for q in range(nq) for k in range(q + 1)]
    qt = jnp.asarray([q for q, _ in tri], jnp.int32)
    kt = jnp.asarray([k for _, k in tri], jnp.int32)
    grid_spec = pltpu.PrefetchScalarGridSpec(
        num_scalar_prefetch=2,
        grid=(batch, kcol, len(tri)),
        in_specs=[
            pl.BlockSpec((tq, w), lambda b, h, t, qt, kt: (b * nq + qt[t], h)),
            pl.BlockSpec((tq, w), lambda b, h, t, qt, kt: (b * nq + kt[t], kcol + h)),
            pl.BlockSpec((tq, w), lambda b, h, t, qt, kt: (b * nq + kt[t], 2 * kcol + h)),
            pl.BlockSpec((1, hps, 1, tq), lambda b, h, t, qt, kt: (b, h, 0, kt[t])),
        ],
        out_specs=pl.BlockSpec((tq, w), lambda b, h, t, qt, kt: (b * nq + qt[t], h)),
        scratch_shapes=[pltpu.VMEM((hps, tq, FOX_HEAD_DIM), F32), pltpu.VMEM((hps, tq, FOX_HEAD_DIM), F32),
                        pltpu.VMEM((tq, w), F32)],
    )
    return pl.pallas_call(
        _fox_attn_kernel,
        grid_spec=grid_spec,
        out_shape=jax.ShapeDtypeStruct((batch * seq, FOX_HEADS * FOX_HEAD_DIM), F32),
        compiler_params=_cparams(("parallel", "parallel", "arbitrary")),
        name="fox_attn",
    )(qt, kt, proj, proj, proj, cum_rows)


def _head_sum(x, bd):
    hi, lo = _split_bf16(x)
    parts = [_dot(hi[:, j * LANES:(j + 1) * LANES], bd) + _dot(lo[:, j * LANES:(j + 1) * LANES], bd)
             for j in range(x.shape[1] // LANES)]
    return jnp.concatenate(parts, axis=1)


def _head_block_diag():
    r = lax.broadcasted_iota(jnp.int32, (LANES, LANES), 0) // RWKV_HEAD_DIM
    c = lax.broadcasted_iota(jnp.int32, (LANES, LANES), 1) // RWKV_HEAD_DIM
    return (r == c).astype(BF16)


def _rwkv_prep_kernel(r_ref, k_ref, v_ref, gd_ref, lo_ref, pr_ref, pk_ref, pv_ref, pgd_ref, plo_ref,
                      mu_r, mu_k, mu_v, mu_gd, mu_lo, w0_ref, wup_ref, a0_ref, aup_ref, gup_ref,
                      kk_ref, ka_ref, rk_ref,
                      or_ref, olw_ref, ok_ref, ov_ref, oa_ref, ob_ref, og_ref, obonus_ref, *, tiles_per_batch):
    i = pl.program_id(0)
    first = (i % tiles_per_batch) == 0

    def shifted(cur_ref, prv_ref, mu_ref):
        cur = cur_ref[...]
        last = jnp.where(first, 0.0, prv_ref[7:8, :])
        row = lax.broadcasted_iota(jnp.int32, cur.shape, 0)
        prev = jnp.where(row == 0, last, pltpu.roll(cur, 1, 0))
        return cur + (prev - cur) * mu_ref[...]

    r = shifted(r_ref, pr_ref, mu_r)
    k = shifted(k_ref, pk_ref, mu_k)
    v = shifted(v_ref, pv_ref, mu_v)
    gd = shifted(gd_ref, pgd_ref, mu_gd)
    lo = shifted(lo_ref, plo_ref, mu_lo)
    wd = lo[:, :LANES]
    ad = lo[:, LANES:]

    w_pre = w0_ref[...] + _dot_bf16x3(jnp.tanh(wd), wup_ref[...])
    w_raw = _log_sigmoid(w_pre) - 0.5
    log_decay = -jnp.exp(w_raw)
    a = jax.nn.sigmoid(a0_ref[...] + _dot_bf16x3(ad, aup_ref[...]))
    g = _dot_bf16x3(jax.nn.sigmoid(gd), gup_ref[...])

    bd = _head_block_diag()
    kk = k * kk_ref[...]
    nrm = jnp.maximum(jnp.sqrt(_head_sum(kk * kk, bd)), 1e-12)
    kk = kk / nrm
    k_mod = k * (1.0 + (a - 1.0) * ka_ref[...])
    bonus = _head_sum(r * k_mod * rk_ref[...], bd) * v

    or_ref[...] = r.astype(BF16)
    olw_ref[...] = log_decay
    ok_ref[...] = k_mod.astype(BF16)
    ov_ref[...] = v.astype(BF16)
    oa_ref[...] = (-kk).astype(BF16)
    ob_ref[...] = (kk * a).astype(BF16)
    og_ref[...] = g.astype(BF16)
    obonus_ref[...] = bonus.astype(BF16)


def _rwkv_prep(proj, cols, mus, w0, w_up, a0, a_up, g_up, k_k, k_a, r_k, tokens_per_batch):
    t = proj.shape[0]
    w = RWKV_HEADS * RWKV_HEAD_DIM
    tm = min(512, tokens_per_batch)
    tpb = tokens_per_batch // tm
    widths = [w, w, w, 256, 256]
    offs = [cols["rr"], cols["rk"], cols["rv"], cols["gd"], cols["lo"]]
    cur_specs = [pl.BlockSpec((tm, wd), functools.partial(lambda i, cb: (i, cb), cb=o // wd))
                 for wd, o in zip(widths, offs)]
    prv_specs = [pl.BlockSpec((8, wd), functools.partial(
        lambda i, cb: (jnp.maximum(i * (tm // 8) - 1, 0), cb), cb=o // wd))
        for wd, o in zip(widths, offs)]
    full = lambda a: pl.BlockSpec(a.shape, lambda i: (0,) * a.ndim)
    params = list(mus) + [w0, w_up, a0, a_up, g_up, k_k, k_a, r_k]
    out_spec = pl.BlockSpec((tm, w), lambda i: (i, 0))
    return pl.pallas_call(
        functools.partial(_rwkv_prep_kernel, tiles_per_batch=tpb),
        grid=(t // tm,),
        in_specs=cur_specs + prv_specs + [full(p) for p in params],
        out_specs=[out_spec] * 8,
        out_shape=[jax.ShapeDtypeStruct((t, w), dt) for dt in (BF16, F32, BF16, BF16, BF16, BF16, BF16, BF16)],
        compiler_params=_cparams(("parallel",)),
        name="rwkv_prep",
    )(*([proj] * 10), *params)


def _rwkv_scan_kernel(r_ref, lw_ref, k_ref, v_ref, a_ref, b_ref, o_ref, h_scr):
    c = pl.program_id(1)

    @pl.when(c == 0)
    def _():
        h_scr[...] = jnp.zeros(h_scr.shape, F32)

    C = r_ref.shape[0]
    n_pairs = r_ref.shape[1] // LANES
    P = HIGHEST

    row = lax.broadcasted_iota(jnp.int32, (C, C), 0)
    col = lax.broadcasted_iota(jnp.int32, (C, C), 1)
    tri = (row >= col).astype(F32)
    lw = lw_ref[...]
    cw = _dot(tri, lw, P)
    cw_end = cw[C - 1:C, :]
    e_pos = jnp.exp(cw)
    e_prev = jnp.exp(cw - lw)
    e_neg = jnp.exp(-cw)
    e_end = jnp.exp(cw_end - cw)
    w_end = jnp.exp(cw_end)

    a = a_ref[...].astype(F32)
    b = b_ref[...].astype(F32)
    k = k_ref[...].astype(F32)
    r = r_ref[...].astype(F32)
    at = a * e_prev
    bt = b * e_neg
    kt = k * e_neg
    rt = r * e_pos
    bh = b * e_end
    kh = k * e_end
    v = v_ref[...].astype(F32)

    lane = lax.broadcasted_iota(jnp.int32, (C, LANES), 1)
    head0 = lane < RWKV_HEAD_DIM
    r2 = lax.broadcasted_iota(jnp.int32, (2 * C, 2 * C), 0)
    c2 = lax.broadcasted_iota(jnp.int32, (2 * C, 2 * C), 1)
    same = (r2 // C) == (c2 // C)
    strict = same & (r2 > c2)
    incl = same & (r2 >= c2)
    eye = (r2 == c2).astype(F32)

    def two(x, p):
        xp = x[:, p * LANES:(p + 1) * LANES]
        return jnp.concatenate([jnp.where(head0, xp, 0.0), jnp.where(head0, 0.0, xp)], axis=0)

    def mm(x, y):
        return _dot(x.astype(BF16), y.astype(BF16))

    def split(x):
        hi = x.astype(BF16)
        return hi, (x - hi.astype(F32)).astype(BF16)

    def mm3(x, y):
        xh, xl = split(x)
        yh, yl = split(y)
        return _dot(xh, yh) + (_dot(xh, yl) + _dot(xl, yh))

    G = 2 * C
    pairs = range(n_pairs)
    at2 = [two(at, p) for p in pairs]
    rt2 = [two(rt, p) for p in pairs]
    v2 = [two(v, p).astype(BF16) for p in pairs]
    gram = [_dot_nt(jnp.concatenate([at2[p], rt2[p]], axis=0).astype(BF16),
                    jnp.concatenate([two(bt, p), two(kt, p)], axis=0).astype(BF16)) for p in pairs]
    a_ab = [jnp.where(strict, gram[p][:G, :G], 0.0) for p in pairs]
    nmat = list(a_ab)
    pw = list(a_ab)
    for _ in range(max(1, (C - 1).bit_length() - 1)):
        pw = [mm3(pw[p], pw[p]) for p in pairs]
        nmat = [nmat[p] + pw[p] + mm3(nmat[p], pw[p]) for p in pairs]
    akv = [mm(jnp.where(strict, gram[p][:G, G:], 0.0), v2[p]) for p in pairs]
    rhs = [jnp.concatenate([at2[p], akv[p]], axis=1) for p in pairs]
    pq = [(rhs[p] + mm(nmat[p], rhs[p])).astype(BF16) for p in pairs]
    ry = [mm(jnp.where(incl, gram[p][G:, :G], 0.0), pq[p]) for p in pairs]
    mv = [mm(jnp.where(incl, gram[p][G:, G:], 0.0), v2[p]) for p in pairs]
    ge = [_dot_tn(two(bh, p).astype(BF16), pq[p]) for p in pairs]
    kv = [_dot_tn(two(kh, p).astype(BF16), v2[p]) for p in pairs]
    for p in pairs:
        rr = rt2[p] + ry[p][:, :LANES]
        gm = eye * w_end[:, p * LANES:(p + 1) * LANES] + ge[p][:, :LANES]
        yh = mm(jnp.concatenate([rr, gm], axis=0), h_scr[p])
        y2 = yh[:G] + ry[p][:, LANES:] + mv[p]
        h_scr[p] = yh[G:] + ge[p][:, LANES:] + kv[p]
        o_ref[:, p * LANES:(p + 1) * LANES] = y2[:C, :] + y2[C:, :]


def _rwkv_scan(r, lw, k, v, a, b, batch, seq):
    w = r.shape[1]
    nc = seq // CHUNK
    spec = pl.BlockSpec((CHUNK, w), lambda bi, ci: (bi * nc + ci, 0))
    return pl.pallas_call(
        _rwkv_scan_kernel,
        grid=(batch, nc),
        in_specs=[spec] * 6,
        out_specs=spec,
        out_shape=jax.ShapeDtypeStruct((batch * seq, w), F32),
        scratch_shapes=[pltpu.VMEM((w // LANES, LANES, LANES), F32)],
        compiler_params=_cparams(("parallel", "arbitrary")),
        name="rwkv_scan",
    )(r, lw, k, v, a, b)


def _mix_out_kernel(o_ref, gate_ref, y_ref, bonus_ref, g_ref, x_ref, g1_ref, lnw_ref, lnb_ref, w_ref, out_ref):
    fox = o_ref[...] * jax.nn.sigmoid(gate_ref[...])
    bd = _head_block_diag()
    y = y_ref[...]
    inv_n = 1.0 / RWKV_HEAD_DIM
    mean = _head_sum(y, bd) * inv_n
    d = y - mean
    var = _head_sum(d * d, bd) * inv_n
    yn = d * lax.rsqrt(var + GN_EPS) * lnw_ref[...] + lnb_ref[...]
    rw = (yn + bonus_ref[...].astype(F32)) * g_ref[...].astype(F32)
    wf = fox.shape[1]
    mix = _dot(fox.astype(BF16), w_ref[:wf, :]) + _dot(rw.astype(BF16), w_ref[wf:, :])
    out_ref[...] = x_ref[...] + g1_ref[0] * mix


def _mix_out(o_fox, proj, gate_col, y, bonus, g, x2, g1, ln_w, ln_b, w_out_bf, tokens_per_batch):
    t, d = x2.shape
    wf = o_fox.shape[1]
    wr = y.shape[1]
    tm = min(512, tokens_per_batch)
    tpb = tokens_per_batch // tm
    return pl.pallas_call(
        _mix_out_kernel,
        grid=(t // tm,),
        in_specs=[pl.BlockSpec((tm, wf), lambda i: (i, 0)),
                  pl.BlockSpec((tm, wf), lambda i: (i, gate_col // wf)),
                  pl.BlockSpec((tm, wr), lambda i: (i, 0)),
                  pl.BlockSpec((tm, wr), lambda i: (i, 0)),
                  pl.BlockSpec((tm, wr), lambda i: (i, 0)),
                  pl.BlockSpec((tm, d), lambda i: (i, 0)),
                  pl.BlockSpec((1, 1, d), lambda i: (i // tpb, 0, 0)),
                  pl.BlockSpec((1, wr), lambda i: (0, 0)),
                  pl.BlockSpec((1, wr), lambda i: (0, 0)),
                  pl.BlockSpec(w_out_bf.shape, lambda i: (0, 0))],
        out_specs=pl.BlockSpec((tm, d), lambda i: (i, 0)),
        out_shape=jax.ShapeDtypeStruct((t, d), F32),
        compiler_params=_cparams(("parallel",)),
        name="mix_out",
    )(o_fox, proj, y, bonus, g, x2, g1, ln_w, ln_b, w_out_bf)


def _top_k_mask_rows(s, k, tie_safe, want_rank=True):
    n = s.shape[0]
    s0 = s
    row = lax.broadcasted_iota(jnp.int32, s.shape, 0).astype(F32)
    rank = jnp.full(s.shape, float(k), F32) if want_rank else None
    vals = []
    for r in range(k):
        m = jnp.max(s, axis=0, keepdims=True)
        hit = s == m
        if tie_safe:
            pos = jnp.min(jnp.where(hit, row, float(n)), axis=0, keepdims=True)
            hit = row == pos
        vals.append(m)
        if want_rank:
            rank = jnp.where(hit, float(r), rank)
        s = jnp.where(hit, -jnp.inf, s)
    picked = (s == -jnp.inf) & (s0 != -jnp.inf)
    n_picked = jnp.sum(jnp.where(picked, 1.0, 0.0), axis=0, keepdims=True)
    return jnp.concatenate(vals, axis=0), rank, picked, jnp.where(n_picked == float(k), 1.0, 0.0)


def _candidate_rows():
    k = PEER_TOPK
    groups, valid = [], []
    for r0 in range(k // 2):
        n1 = k // (r0 + 1)
        for g in range(-(-n1 // 8)):
            groups.append((r0, g * 8))
            valid.append([g * 8 + i < n1 for i in range(8)])
    groups.append((None, k // 2))
    valid.append([True] * 8)
    return groups, valid


def _route_head(sa, sb, tie_safe):
    k = PEER_TOPK
    tm = sa.shape[1]
    groups, valid = _candidate_rows()
    sub8 = lax.broadcasted_iota(jnp.int32, (8, tm), 0)
    top_a, rank_a, _, clean_a = _top_k_mask_rows(sa, k, tie_safe)
    top_b, rank_b, _, clean_b = _top_k_mask_rows(sb, k, tie_safe)
    parts = []
    for (r0, off), ok in zip(groups, valid):
        if r0 is None:
            part = top_a[off:off + 8, :] + top_b[0:1, :]
        else:
            part = top_a[r0:r0 + 1, :] + top_b[off:off + 8, :]
        if not all(ok):
            part = jnp.where(sub8 < sum(ok), part, -jnp.inf)
        parts.append(part)
    cand = jnp.concatenate(parts, axis=0)
    _, _, sel, clean_c = _top_k_mask_rows(cand, k, tie_safe, want_rank=False)
    z = jnp.sum(jnp.where(sel, jnp.exp(cand - cand[0:1, :]), 0.0), axis=0, keepdims=True)
    self32 = jnp.where(sel, 1.0, 0.0)
    n_keys = sa.shape[0]
    rank3 = rank_a.astype(BF16).reshape(n_keys // BF16_ROWS, BF16_ROWS, tm)
    length3 = jnp.zeros(rank3.shape, BF16)

    def add_count(r0, count_row):
        tile = jnp.broadcast_to(count_row, (BF16_ROWS, tm)).astype(BF16)
        return length3 + jnp.where(rank3 == float(r0), tile[None], 0.0).astype(BF16)

    tail = None
    per_rank = {}
    for gi, (r0, off) in enumerate(groups):
        cnt = self32[gi * 8:(gi + 1) * 8, :]
        if r0 is None:
            tail = cnt
        else:
            c = jnp.sum(cnt, axis=0, keepdims=True)
            per_rank[r0] = c if r0 not in per_rank else per_rank[r0] + c
    for r in range(8):
        per_rank[k // 2 + r] = tail[r:r + 1, :]
    for r0, c in per_rank.items():
        length3 = add_count(r0, c)
    length = length3.reshape(n_keys, tm).astype(F32)
    pa = jnp.exp(sa - top_a[0:1, :])
    qb = jnp.exp(sb - top_b[0:1, :]) / z
    return pa, length, qb, rank_b, clean_a * clean_b * clean_c


def _peer_query_kernel(x_ref, nw_ref, sc_ref, sh_ref, wq_ref, ht_ref, q_ref):
    h = _modulated_norm(x_ref[...], nw_ref[...], sc_ref[0], sh_ref[0])
    ht_ref[...] = h.T.astype(BF16)
    q = _dot(h.astype(BF16), wq_ref[...])
    dk = q.shape[1] // PEER_HEADS
    for hd in range(PEER_HEADS):
        q_ref[hd] = q[:, hd * dk:(hd + 1) * dk]


def _peer_query(x1, norm_w, sc, sh, wq_bf, tokens_per_batch):
    t, d = x1.shape
    dk = wq_bf.shape[1] // PEER_HEADS
    tm = min(512, tokens_per_batch)
    tpb = tokens_per_batch // tm
    return pl.pallas_call(
        _peer_query_kernel,
        grid=(t // tm,),
        in_specs=[pl.BlockSpec((tm, d), lambda i: (i, 0)),
                  pl.BlockSpec((1, d), lambda i: (0, 0)),
                  pl.BlockSpec((1, 1, d), lambda i: (i // tpb, 0, 0)),
                  pl.BlockSpec((1, 1, d), lambda i: (i // tpb, 0, 0)),
                  pl.BlockSpec(wq_bf.shape, lambda i: (0, 0))],
        out_specs=[pl.BlockSpec((d, tm), lambda i: (0, i)),
                   pl.BlockSpec((PEER_HEADS, tm, dk), lambda i: (0, i, 0))],
        out_shape=[jax.ShapeDtypeStruct((d, t), BF16),
                   jax.ShapeDtypeStruct((PEER_HEADS, t, dk), F32)],
        compiler_params=_cparams(("parallel",)),
        name="peer_query",
    )(x1, norm_w, sc, sh, wq_bf)


def _gelu_exact(x):
    return 0.5 * x * (1.0 + lax.erf(x * 0.7071067811865476))


def _peer_route_act_kernel(ht_ref, q_ref, keys_ref, u_ref,
                           act_ref, pa_ref, len_ref, qb_ref, rnk_ref, code_scr):
    hd = pl.program_id(1)
    tm = ht_ref.shape[1]
    half = keys_ref.shape[-1]
    blocked = (PEER_NKEYS // PEER_SLABS, PEER_SLABS, tm)

    @pl.when(hd == 0)
    def _():
        code_scr[...] = jnp.zeros(code_scr.shape, F32)

    act_ref[...] = _gelu_exact(_dot(u_ref[...], ht_ref[...])).astype(BF16)

    def scores(h):
        q = q_ref[h]
        return (_dot_nt(keys_ref[h, 0], q[:, :half], HIGHEST),
                _dot_nt(keys_ref[h, 1], q[:, half:], HIGHEST))

    def emit(h, tables):
        pa, length, qbt, rank_b, _ = tables
        pa_ref[h] = pa.reshape(blocked)
        len_ref[h] = length.reshape(blocked)
        qb_ref[h] = qbt.astype(BF16)
        rnk_ref[h] = rank_b.astype(BF16)

    fast = _route_head(*scores(hd), tie_safe=False)
    emit(hd, fast)
    dirty = 1.0 - jnp.min(fast[4], axis=1, keepdims=True)
    code_scr[0:1, 0:1] += dirty * lax.shift_left(1, hd).astype(F32)

    @pl.when(hd == pl.num_programs(1) - 1)
    def _():
        tie_bits = code_scr[0, 0].astype(jnp.int32)

        def redo(h, carry):
            @pl.when(((tie_bits >> h) & 1) == 1)
            def _():
                emit(h, _route_head(*scores(h), tie_safe=True))
            return carry

        lax.fori_loop(0, PEER_HEADS, redo, 0)


def _peer_route_act(ht_bf, q_heads, sub_keys, u_bf, tokens_per_batch):
    d, t = ht_bf.shape
    ne = u_bf.shape[0]
    bu = ne // PEER_HEADS
    dk = q_heads.shape[-1]
    tm = min(512, tokens_per_batch)
    n_blocks = PEER_NKEYS // PEER_SLABS
    dense = jax.ShapeDtypeStruct((PEER_HEADS, PEER_NKEYS, t), BF16)
    dense_spec = pl.BlockSpec((PEER_HEADS, PEER_NKEYS, tm), lambda i, j: (0, 0, i))
    blocked = jax.ShapeDtypeStruct((PEER_HEADS, n_blocks, PEER_SLABS, t), F32)
    blocked_spec = pl.BlockSpec((PEER_HEADS, n_blocks, PEER_SLABS, tm), lambda i, j: (0, 0, 0, i))
    return pl.pallas_call(
        _peer_route_act_kernel,
        grid=(t // tm, PEER_HEADS),
        in_specs=[pl.BlockSpec((d, tm), lambda i, j: (0, i)),
                  pl.BlockSpec((PEER_HEADS, tm, dk), lambda i, j: (0, i, 0)),
                  pl.BlockSpec(sub_keys.shape, lambda i, j: (0, 0, 0, 0)),
                  pl.BlockSpec((bu, d), lambda i, j: (j, 0))],
        out_specs=[pl.BlockSpec((bu, tm), lambda i, j: (j, i)),
                   blocked_spec, blocked_spec, dense_spec, dense_spec],
        out_shape=[jax.ShapeDtypeStruct((ne, t), BF16), blocked, blocked, dense, dense],
        scratch_shapes=[pltpu.VMEM((8, LANES), F32)],
        compiler_params=_cparams(("parallel", "arbitrary")),
        name="peer_route_act",
    )(ht_bf, q_heads, sub_keys, u_bf)


def _peer_expert_kernel(act_ref, vt_ref, pa_ref, len_ref, qb_ref, rnk_ref, x_ref, g2_ref, o_ref, acc_scr):
    j = pl.program_id(1)
    be, tm = act_ref.shape

    @pl.when(j == 0)
    def _():
        acc_scr[...] = jnp.zeros(acc_scr.shape, F32)

    n_slabs = be // PEER_NKEYS
    rows = BF16_ROWS
    n_groups = PEER_NKEYS // rows

    def row_tile(ref, hd, s):
        return jnp.broadcast_to(ref[hd, 0, s:s + 1, :], (rows, tm)).astype(BF16)

    zero = jnp.zeros((rows, tm), BF16)
    parts = [[None] * n_groups for _ in range(n_slabs)]
    slab_group = 8
    for s0 in range(0, n_slabs, slab_group):
        ss = range(s0, min(s0 + slab_group, n_slabs))
        ln = {(hd, s): row_tile(len_ref, hd, s) for hd in range(PEER_HEADS) for s in ss}
        pa = {(hd, s): row_tile(pa_ref, hd, s) for hd in range(PEER_HEADS) for s in ss}
        for g in range(n_groups):
            sl = slice(g * rows, (g + 1) * rows)
            acc = {}
            for hd in range(PEER_HEADS):
                rk = rnk_ref[hd, sl, :]
                qv = qb_ref[hd, sl, :]
                for s in ss:
                    term = jnp.where(rk < ln[hd, s], qv, zero) * pa[hd, s]
                    acc[s] = term if s not in acc else acc[s] + term
            for s in ss:
                lo = s * PEER_NKEYS + g * rows
                parts[s][g] = acc[s] * act_ref[lo:lo + rows, :]
    p = jnp.concatenate([parts[s][g] for s in range(n_slabs) for g in range(n_groups)], axis=0)
    acc_scr[...] += _dot(vt_ref[0], p)

    @pl.when(j == pl.num_programs(1) - 1)
    def _():
        o_ref[...] = x_ref[...] + g2_ref[0] * acc_scr[...].T


def _peer_expert(act, vt_bf, pa, ln, qb, rnk, x1, g2, tokens_per_batch):
    t, d = x1.shape
    ne = act.shape[0]
    tm = min(512, tokens_per_batch)
    tpb = tokens_per_batch // tm
    be = PEER_EXPERT_BLOCK
    dense_spec = pl.BlockSpec((PEER_HEADS, PEER_NKEYS, tm), lambda i, j: (0, 0, i))
    blocked_spec = pl.BlockSpec((PEER_HEADS, 1, PEER_SLABS, tm), lambda i, j: (0, j, 0, i))
    return pl.pallas_call(
        _peer_expert_kernel,
        grid=(t // tm, ne // be),
        in_specs=[pl.BlockSpec((be, tm), lambda i, j: (j, i)),
                  pl.BlockSpec((1, d, be), lambda i, j: (j, 0, 0)),
                  blocked_spec, blocked_spec, dense_spec, dense_spec,
                  pl.BlockSpec((tm, d), lambda i, j: (i, 0)),
                  pl.BlockSpec((1, 1, d), lambda i, j: (i // tpb, 0, 0))],
        out_specs=pl.BlockSpec((tm, d), lambda i, j: (i, 0)),
        out_shape=jax.ShapeDtypeStruct((t, d), F32),
        scratch_shapes=[pltpu.VMEM((d, tm), F32)],
        compiler_params=_cparams(("parallel", "arbitrary")),
        name="peer_expert",
    )(act, vt_bf, pa, ln, qb, rnk, x1, g2)


def _pad_cols(a, n):
    return jnp.pad(a, ((0, 0), (0, n - a.shape[1])))


def _pad_rows(a, n):
    return jnp.pad(a, ((0, n - a.shape[0]), (0, 0)))


def _layer(x, c, w_ada, b_ada, norm_mix_w, w_in, fox_q_norm_w, fox_k_norm_w, fox_f_bias,
           rwkv_mu, rwkv_w0, rwkv_w_up, rwkv_a0, rwkv_a_up, rwkv_g_up, rwkv_k_k, rwkv_k_a,
           rwkv_r_k, rwkv_ln_w, rwkv_ln_b, w_out, norm_ffn_w, peer_w_query, peer_sub_keys,
           peer_u, peer_v):
    B, S, D = x.shape
    T = B * S
    fw = FOX_HEADS * FOX_HEAD_DIM
    rw = RWKV_HEADS * RWKV_HEAD_DIM
    w_lora = rwkv_w_up.shape[0]
    a_lora = rwkv_a_up.shape[0]
    g_lora = rwkv_g_up.shape[0]
    assert w_lora <= LANES and a_lora + FOX_HEADS <= LANES and g_lora == 256

    c_pad = _pad_rows(c, 8)
    mod = _ada_mod(c_pad, w_ada, b_ada)[:B]
    sh1, sc1, g1, sh2, sc2, g2 = [m.reshape(B, 1, D) for m in jnp.split(mod, 6, axis=-1)]

    fox_cols = 4 * fw + FOX_HEADS
    wi_fox, wi_rw = w_in[:, :fox_cols], w_in[:, fox_cols:]
    mu = rwkv_mu.reshape(1, -1)
    seg = lambda a, lo, n: a[:, lo:lo + n]
    w_perm = jnp.concatenate([
        seg(wi_fox, 0, 4 * fw),
        seg(wi_rw, 0, 3 * rw),
        seg(wi_rw, 3 * rw + w_lora + a_lora, g_lora),
        _pad_cols(seg(wi_rw, 3 * rw, w_lora), LANES),
        _pad_cols(jnp.concatenate([seg(wi_rw, 3 * rw + w_lora, a_lora), seg(wi_fox, 4 * fw, FOX_HEADS)], 1), LANES),
    ], axis=1).astype(BF16)
    cols = {"gate": 3 * fw, "rr": 4 * fw, "rk": 4 * fw + rw, "rv": 4 * fw + 2 * rw,
            "gd": 4 * fw + 3 * rw, "lo": 4 * fw + 3 * rw + g_lora}
    f_lane = a_lora
    f_block = (cols["lo"] + LANES) // LANES
    mus = [seg(mu, 0, rw), seg(mu, rw, rw), seg(mu, 2 * rw, rw),
           seg(mu, 3 * rw + w_lora + a_lora, g_lora),
           jnp.concatenate([_pad_cols(seg(mu, 3 * rw, w_lora), LANES),
                            _pad_cols(seg(mu, 3 * rw + w_lora, a_lora), LANES)], 1)]
    scale = FOX_HEAD_DIM ** -0.5 * LOG2_E
    head_w = _pad_cols(jnp.concatenate([jnp.tile(fox_q_norm_w * scale, FOX_HEADS),
                                        jnp.tile(fox_k_norm_w, FOX_HEADS)]).reshape(1, -1), w_perm.shape[1])

    x2 = x.reshape(T, D)
    proj = _in_proj(_norm_mod(x2, norm_mix_w.reshape(1, D), sc1, sh1, S), w_perm, head_w, 2 * fw)

    f_bias_row = jnp.zeros((1, LANES), F32).at[0, f_lane:f_lane + FOX_HEADS].set(fox_f_bias)
    cum = _fox_cum(proj, f_bias_row, B, S, f_block)
    cum_rows = cum.reshape(B, S, LANES)[:, :, f_lane:f_lane + FOX_HEADS].transpose(0, 2, 1).reshape(B, FOX_HEADS, 1, S)
    o_fox = _fox_attn(proj, cum_rows, B, S)

    row = lambda a: a.reshape(1, -1)
    r, lw, k, v, a_vec, b_vec, g, bonus = _rwkv_prep(
        proj, cols, mus, row(rwkv_w0), _pad_rows(rwkv_w_up, LANES), row(rwkv_a0), _pad_rows(rwkv_a_up, LANES),
        rwkv_g_up, row(rwkv_k_k), row(rwkv_k_a), row(rwkv_r_k), S)
    y = _rwkv_scan(r, lw, k, v, a_vec, b_vec, B, S)

    x1 = _mix_out(o_fox, proj, cols["gate"], y, bonus, g, x2, g1, row(rwkv_ln_w), row(rwkv_ln_b),
                  w_out.astype(BF16), S)

    ht, q_heads = _peer_query(x1, norm_ffn_w.reshape(1, D), sc2, sh2, peer_w_query.astype(BF16), S)
    act, pa, ln, qb, rnk = _peer_route_act(ht, q_heads, peer_sub_keys, peer_u.astype(BF16), S)
    vt = peer_v.astype(BF16).reshape(-1, PEER_EXPERT_BLOCK, D).transpose(0, 2, 1)
    out = _peer_expert(act, vt, pa, ln, qb, rnk, x1, g2, S)
    return out.reshape(B, S, D)


def kernel(x, c, w_ada, b_ada, norm_mix_w, w_in, fox_q_norm_w, fox_k_norm_w, fox_f_bias, rwkv_mu, rwkv_w0,
           rwkv_w_up, rwkv_a0, rwkv_a_up, rwkv_g_up, rwkv_k_k, rwkv_k_a, rwkv_r_k, rwkv_ln_w, rwkv_ln_b,
           w_out, norm_ffn_w, peer_w_query, peer_sub_keys, peer_u, peer_v):
    params = (w_ada, b_ada, norm_mix_w, w_in, fox_q_norm_w, fox_k_norm_w, fox_f_bias, rwkv_mu, rwkv_w0,
              rwkv_w_up, rwkv_a0, rwkv_a_up, rwkv_g_up, rwkv_k_k, rwkv_k_a, rwkv_r_k, rwkv_ln_w, rwkv_ln_b,
              w_out, norm_ffn_w, peer_w_query, peer_sub_keys, peer_u, peer_v)
    for l in range(w_ada.shape[0]):
        x = _layer(x, c, *[p[l] for p in params])
    return x
```

```python
import functools

import jax
import jax.numpy as jnp
from jax import lax
from jax.experimental import pallas as pl
from jax.experimental.pallas import tpu as pltpu

F32 = jnp.float32
BF16 = jnp.bfloat16
HIGHEST = lax.Precision.HIGHEST

LANES = 128
BF16_ROWS = 16
NORM_EPS = 1e-6
GN_EPS = 64e-5
CHUNK = 64
FOX_HEADS = 8
FOX_HEAD_DIM = 128
RWKV_HEADS = 16
RWKV_HEAD_DIM = 64
PEER_HEADS = 8
PEER_NKEYS = 128
PEER_TOPK = 16
PEER_EXPERT_BLOCK = 1024
PEER_SLABS = PEER_EXPERT_BLOCK // PEER_NKEYS
ROUTE_SUB_TOKENS = 256
NEG_BIG = -1e30
LOG2_E = 1.4426950408889634
VMEM_LIMIT = 56 * 1024 * 1024


def _cparams(sem):
    return pltpu.CompilerParams(dimension_semantics=sem, vmem_limit_bytes=VMEM_LIMIT)


def _dot(a, b, precision=None):
    return jnp.dot(a, b, preferred_element_type=F32, precision=precision)


def _dot_nt(a, b, precision=None):
    return lax.dot_general(a, b, (((1,), (1,)), ((), ())), preferred_element_type=F32,
                           precision=precision)


def _split_bf16(x):
    hi = x.astype(BF16)
    return hi, (x - hi.astype(F32)).astype(BF16)


def _dot_bf16x3(a, b):
    ah, al = _split_bf16(a)
    bh, bl = _split_bf16(b)
    return _dot(ah, bh) + (_dot(ah, bl) + _dot(al, bh))


def _dot_tn(a, b, precision=None):
    return lax.dot_general(a, b, (((0,), (0,)), ((), ())), preferred_element_type=F32,
                           precision=precision)


def _ada_kernel(c_ref, w_ref, b_ref, o_ref):
    c = c_ref[...]
    s = c * jax.nn.sigmoid(c)
    o_ref[...] = _dot(s, w_ref[...], HIGHEST) + b_ref[...]


def _ada_mod(c_pad, w_ada, b_ada):
    rows, d = c_pad.shape
    n = w_ada.shape[1]
    bn = 1024
    return pl.pallas_call(
        _ada_kernel,
        grid=(n // bn,),
        in_specs=[pl.BlockSpec((rows, d), lambda j: (0, 0)),
                  pl.BlockSpec((d, bn), lambda j: (0, j)),
                  pl.BlockSpec((1, bn), lambda j: (0, j))],
        out_specs=pl.BlockSpec((rows, bn), lambda j: (0, j)),
        out_shape=jax.ShapeDtypeStruct((rows, n), F32),
        compiler_params=_cparams(("arbitrary",)),
        name="ada_mod",
    )(c_pad, w_ada, b_ada.reshape(1, n))


def _modulated_norm(x, nw, sc, sh):
    y = x * lax.rsqrt(jnp.mean(x * x, axis=-1, keepdims=True) + NORM_EPS)
    return y * nw * (1.0 + sc) + sh


def _norm_mod_kernel(x_ref, nw_ref, sc_ref, sh_ref, o_ref):
    o_ref[...] = _modulated_norm(x_ref[...], nw_ref[...], sc_ref[0], sh_ref[0]).astype(BF16)


def _norm_mod(x2, norm_w, sc, sh, tokens_per_batch):
    t, d = x2.shape
    tm = min(512, tokens_per_batch)
    tpb = tokens_per_batch // tm
    return pl.pallas_call(
        _norm_mod_kernel,
        grid=(t // tm,),
        in_specs=[pl.BlockSpec((tm, d), lambda i: (i, 0)),
                  pl.BlockSpec((1, d), lambda i: (0, 0)),
                  pl.BlockSpec((1, 1, d), lambda i: (i // tpb, 0, 0)),
                  pl.BlockSpec((1, 1, d), lambda i: (i // tpb, 0, 0))],
        out_specs=pl.BlockSpec((tm, d), lambda i: (i, 0)),
        out_shape=jax.ShapeDtypeStruct((t, d), BF16),
        compiler_params=_cparams(("parallel",)),
        name="norm_mod",
    )(x2, norm_w, sc, sh)


def _in_proj_kernel(h_ref, w_ref, hw_ref, o_ref, *, n_qk_blocks):
    j = pl.program_id(1)
    acc = _dot(h_ref[...], w_ref[...])

    @pl.when(j < n_qk_blocks)
    def _():
        for hh in range(acc.shape[1] // FOX_HEAD_DIM):
            sl = slice(hh * FOX_HEAD_DIM, (hh + 1) * FOX_HEAD_DIM)
            a = acc[:, sl]
            rs = lax.rsqrt(jnp.mean(a * a, axis=-1, keepdims=True) + NORM_EPS)
            o_ref[:, sl] = a * rs * hw_ref[:, sl]

    @pl.when(j >= n_qk_blocks)
    def _():
        o_ref[...] = acc


def _in_proj(h_bf, w_bf, head_w, n_qk_cols):
    t, d = h_bf.shape
    n = w_bf.shape[1]
    tm = min(2048, t)
    bn = 512
    kern = functools.partial(_in_proj_kernel, n_qk_blocks=n_qk_cols // bn)
    return pl.pallas_call(
        kern,
        grid=(t // tm, n // bn),
        in_specs=[pl.BlockSpec((tm, d), lambda i, j: (i, 0)),
                  pl.BlockSpec((d, bn), lambda i, j: (0, j)),
                  pl.BlockSpec((1, bn), lambda i, j: (0, j))],
        out_specs=pl.BlockSpec((tm, bn), lambda i, j: (i, j)),
        out_shape=jax.ShapeDtypeStruct((t, n), F32),
        compiler_params=_cparams(("parallel", "arbitrary")),
        name="in_proj",
    )(h_bf, w_bf, head_w)


def _log_sigmoid(x):
    return jnp.minimum(x, 0.0) - jnp.log(1.0 + jnp.exp(-jnp.abs(x)))


def _fox_cum_kernel(f_ref, b_ref, o_ref, *, blk):
    s = f_ref.shape[0]
    row = lax.broadcasted_iota(jnp.int32, (blk, blk), 0)
    col = lax.broadcasted_iota(jnp.int32, (blk, blk), 1)
    tri = (row >= col).astype(F32)
    carry = jnp.zeros((1, f_ref.shape[1]), F32)
    for i in range(s // blk):
        lf = _log_sigmoid(f_ref[i * blk:(i + 1) * blk, :] + b_ref[...])
        cs = _dot(tri, lf, HIGHEST) + carry
        o_ref[i * blk:(i + 1) * blk, :] = cs * LOG2_E
        carry = cs[blk - 1:blk, :]


def _fox_cum(proj, f_bias_row, batch, seq, col_block):
    blk = min(256, seq)
    return pl.pallas_call(
        functools.partial(_fox_cum_kernel, blk=blk),
        grid=(batch,),
        in_specs=[pl.BlockSpec((seq, LANES), lambda b: (b, col_block)),
                  pl.BlockSpec((1, LANES), lambda b: (0, 0))],
        out_specs=pl.BlockSpec((seq, LANES), lambda b: (b, 0)),
        out_shape=jax.ShapeDtypeStruct((batch * seq, LANES), F32),
        compiler_params=_cparams(("parallel",)),
        name="fox_cum",
    )(proj, f_bias_row)


def _fox_attn_kernel(qt_ref, kt_ref, q_ref, k_ref, v_ref, c_ref, o_ref, m_scr, l_scr, acc_scr):
    t = pl.program_id(2)
    qi = qt_ref[t]
    ki = kt_ref[t]
    hd = FOX_HEAD_DIM
    n_heads = q_ref.shape[1] // hd

    @pl.when(ki == 0)
    def _():
        m_scr[...] = jnp.full(m_scr.shape, NEG_BIG, F32)
        l_scr[...] = jnp.zeros(l_scr.shape, F32)
        acc_scr[...] = jnp.zeros(acc_scr.shape, F32)

    def step(masked):
        hs = range(n_heads)
        sl = [slice(h * hd, (h + 1) * hd) for h in hs]
        s = [_dot_nt(q_ref[:, sl[h]].astype(BF16), k_ref[:, sl[h]].astype(BF16)) - c_ref[0, h] for h in hs]
        if masked:
            row = lax.broadcasted_iota(jnp.int32, s[0].shape, 0)
            col = lax.broadcasted_iota(jnp.int32, s[0].shape, 1)
            s = [jnp.where(row >= col, x, NEG_BIG) for x in s]
        m_prev = [m_scr[h] for h in hs]
        m_new = [jnp.maximum(m_prev[h], jnp.max(s[h], axis=-1, keepdims=True)) for h in hs]
        n_rep = s[0].shape[1] // hd
        p = [jnp.exp2(s[h] - jnp.concatenate([m_new[h]] * n_rep, axis=1)) for h in hs]
        alpha = [jnp.exp2(m_prev[h] - m_new[h]) for h in hs]
        ones = jnp.ones((k_ref.shape[0], hd), BF16)
        pv = [_dot(p[h].astype(BF16), jnp.concatenate([v_ref[:, sl[h]].astype(BF16), ones], axis=1))
              for h in hs]
        for h in hs:
            l_scr[h] = alpha[h] * l_scr[h] + pv[h][:, hd:]
            acc_scr[:, sl[h]] = alpha[h] * acc_scr[:, sl[h]] + pv[h][:, :hd]
            m_scr[h] = m_new[h]

    @pl.when(ki < qi)
    def _():
        step(False)

    @pl.when(ki == qi)
    def _():
        step(True)
        for h in range(n_heads):
            sl = slice(h * hd, (h + 1) * hd)
            o_ref[:, sl] = acc_scr[:, sl] / l_scr[h]


def _fox_attn(proj, cum_rows, batch, seq):
    tq = min(512, seq)
    nq = seq // tq
    hps = 4
    w = hps * FOX_HEAD_DIM
    kcol = FOX_HEADS // hps
    tri = [(q, k) for q in range(nq) for k in range(q + 1)]
    qt = jnp.asarray([q for q, _ in tri], jnp.int32)
    kt = jnp.asarray([k for _, k in tri], jnp.int32)
    grid_spec = pltpu.PrefetchScalarGridSpec(
        num_scalar_prefetch=2,
        grid=(batch, kcol, len(tri)),
        in_specs=[
            pl.BlockSpec((tq, w), lambda b, h, t, qt, kt: (b * nq + qt[t], h)),
            pl.BlockSpec((tq, w), lambda b, h, t, qt, kt: (b * nq + kt[t], kcol + h)),
            pl.BlockSpec((tq, w), lambda b, h, t, qt, kt: (b * nq + kt[t], 2 * kcol + h)),
            pl.BlockSpec((1, hps, 1, tq), lambda b, h, t, qt, kt: (b, h, 0, kt[t])),
        ],
        out_specs=pl.BlockSpec((tq, w), lambda b, h, t, qt, kt: (b * nq + qt[t], h)),
        scratch_shapes=[pltpu.VMEM((hps, tq, FOX_HEAD_DIM), F32), pltpu.VMEM((hps, tq, FOX_HEAD_DIM), F32),
                        pltpu.VMEM((tq, w), F32)],
    )
    return pl.pallas_call(
        _fox_attn_kernel,
        grid_spec=grid_spec,
        out_shape=jax.ShapeDtypeStruct((batch * seq, FOX_HEADS * FOX_HEAD_DIM), F32),
        compiler_params=_cparams(("parallel", "parallel", "arbitrary")),
        name="fox_attn",
    )(qt, kt, proj, proj, proj, cum_rows)


def _head_sum(x, bd):
    hi, lo = _split_bf16(x)
    parts = [_dot(hi[:, j * LANES:(j + 1) * LANES], bd) + _dot(lo[:, j * LANES:(j + 1) * LANES], bd)
             for j in range(x.shape[1] // LANES)]
    return jnp.concatenate(parts, axis=1)


def _head_block_diag():
    r = lax.broadcasted_iota(jnp.int32, (LANES, LANES), 0) // RWKV_HEAD_DIM
    c = lax.broadcasted_iota(jnp.int32, (LANES, LANES), 1) // RWKV_HEAD_DIM
    return (r == c).astype(BF16)


def _rwkv_prep_kernel(r_ref, k_ref, v_ref, gd_ref, lo_ref, pr_ref, pk_ref, pv_ref, pgd_ref, plo_ref,
                      mu_r, mu_k, mu_v, mu_gd, mu_lo, w0_ref, wup_ref, a0_ref, aup_ref, gup_ref,
                      kk_ref, ka_ref, rk_ref,
                      or_ref, olw_ref, ok_ref, ov_ref, oa_ref, ob_ref, og_ref, obonus_ref, *, tiles_per_batch):
    i = pl.program_id(0)
    first = (i % tiles_per_batch) == 0

    def shifted(cur_ref, prv_ref, mu_ref):
        cur = cur_ref[...]
        last = jnp.where(first, 0.0, prv_ref[7:8, :])
        row = lax.broadcasted_iota(jnp.int32, cur.shape, 0)
        prev = jnp.where(row == 0, last, pltpu.roll(cur, 1, 0))
        return cur + (prev - cur) * mu_ref[...]

    r = shifted(r_ref, pr_ref, mu_r)
    k = shifted(k_ref, pk_ref, mu_k)
    v = shifted(v_ref, pv_ref, mu_v)
    gd = shifted(gd_ref, pgd_ref, mu_gd)
    lo = shifted(lo_ref, plo_ref, mu_lo)
    wd = lo[:, :LANES]
    ad = lo[:, LANES:]

    w_pre = w0_ref[...] + _dot_bf16x3(jnp.tanh(wd), wup_ref[...])
    w_raw = _log_sigmoid(w_pre) - 0.5
    log_decay = -jnp.exp(w_raw)
    a = jax.nn.sigmoid(a0_ref[...] + _dot_bf16x3(ad, aup_ref[...]))
    g = _dot_bf16x3(jax.nn.sigmoid(gd), gup_ref[...])

    bd = _head_block_diag()
    kk = k * kk_ref[...]
    nrm = jnp.maximum(jnp.sqrt(_head_sum(kk * kk, bd)), 1e-12)
    kk = kk / nrm
    k_mod = k * (1.0 + (a - 1.0) * ka_ref[...])
    bonus = _head_sum(r * k_mod * rk_ref[...], bd) * v

    or_ref[...] = r.astype(BF16)
    olw_ref[...] = log_decay
    ok_ref[...] = k_mod.astype(BF16)
    ov_ref[...] = v.astype(BF16)
    oa_ref[...] = (-kk).astype(BF16)
    ob_ref[...] = (kk * a).astype(BF16)
    og_ref[...] = g.astype(BF16)
    obonus_ref[...] = bonus.astype(BF16)


def _rwkv_prep(proj, cols, mus, w0, w_up, a0, a_up, g_up, k_k, k_a, r_k, tokens_per_batch):
    t = proj.shape[0]
    w = RWKV_HEADS * RWKV_HEAD_DIM
    tm = min(512, tokens_per_batch)
    tpb = tokens_per_batch // tm
    widths = [w, w, w, 256, 256]
    offs = [cols["rr"], cols["rk"], cols["rv"], cols["gd"], cols["lo"]]
    cur_specs = [pl.BlockSpec((tm, wd), functools.partial(lambda i, cb: (i, cb), cb=o // wd))
                 for wd, o in zip(widths, offs)]
    prv_specs = [pl.BlockSpec((8, wd), functools.partial(
        lambda i, cb: (jnp.maximum(i * (tm // 8) - 1, 0), cb), cb=o // wd))
        for wd, o in zip(widths, offs)]
    full = lambda a: pl.BlockSpec(a.shape, lambda i: (0,) * a.ndim)
    params = list(mus) + [w0, w_up, a0, a_up, g_up, k_k, k_a, r_k]
    out_spec = pl.BlockSpec((tm, w), lambda i: (i, 0))
    return pl.pallas_call(
        functools.partial(_rwkv_prep_kernel, tiles_per_batch=tpb),
        grid=(t // tm,),
        in_specs=cur_specs + prv_specs + [full(p) for p in params],
        out_specs=[out_spec] * 8,
        out_shape=[jax.ShapeDtypeStruct((t, w), dt) for dt in (BF16, F32, BF16, BF16, BF16, BF16, BF16, BF16)],
        compiler_params=_cparams(("parallel",)),
        name="rwkv_prep",
    )(*([proj] * 10), *params)


def _rwkv_scan_kernel(r_ref, lw_ref, k_ref, v_ref, a_ref, b_ref, o_ref, h_scr):
    c = pl.program_id(1)

    @pl.when(c == 0)
    def _():
        h_scr[...] = jnp.zeros(h_scr.shape, F32)

    C = r_ref.shape[0]
    n_pairs = r_ref.shape[1] // LANES
    P = HIGHEST

    row = lax.broadcasted_iota(jnp.int32, (C, C), 0)
    col = lax.broadcasted_iota(jnp.int32, (C, C), 1)
    tri = (row >= col).astype(F32)
    lw = lw_ref[...]
    cw = _dot(tri, lw, P)
    cw_end = cw[C - 1:C, :]
    e_pos = jnp.exp(cw)
    e_prev = jnp.exp(cw - lw)
    e_neg = jnp.exp(-cw)
    e_end = jnp.exp(cw_end - cw)
    w_end = jnp.exp(cw_end)

    a = a_ref[...].astype(F32)
    b = b_ref[...].astype(F32)
    k = k_ref[...].astype(F32)
    r = r_ref[...].astype(F32)
    at = a * e_prev
    bt = b * e_neg
    kt = k * e_neg
    rt = r * e_pos
    bh = b * e_end
    kh = k * e_end
    v = v_ref[...].astype(F32)

    lane = lax.broadcasted_iota(jnp.int32, (C, LANES), 1)
    head0 = lane < RWKV_HEAD_DIM
    r2 = lax.broadcasted_iota(jnp.int32, (2 * C, 2 * C), 0)
    c2 = lax.broadcasted_iota(jnp.int32, (2 * C, 2 * C), 1)
    same = (r2 // C) == (c2 // C)
    strict = same & (r2 > c2)
    incl = same & (r2 >= c2)
    eye = (r2 == c2).astype(F32)

    def two(x, p):
        xp = x[:, p * LANES:(p + 1) * LANES]
        return jnp.concatenate([jnp.where(head0, xp, 0.0), jnp.where(head0, 0.0, xp)], axis=0)

    def mm(x, y):
        return _dot(x.astype(BF16), y.astype(BF16))

    def split(x):
        hi = x.astype(BF16)
        return hi, (x - hi.astype(F32)).astype(BF16)

    def mm3(x, y):
        xh, xl = split(x)
        yh, yl = split(y)
        return _dot(xh, yh) + (_dot(xh, yl) + _dot(xl, yh))

    G = 2 * C
    pairs = range(n_pairs)
    at2 = [two(at, p) for p in pairs]
    rt2 = [two(rt, p) for p in pairs]
    v2 = [two(v, p).astype(BF16) for p in pairs]
    gram = [_dot_nt(jnp.concatenate([at2[p], rt2[p]], axis=0).astype(BF16),
                    jnp.concatenate([two(bt, p), two(kt, p)], axis=0).astype(BF16)) for p in pairs]
    a_ab = [jnp.where(strict, gram[p][:G, :G], 0.0) for p in pairs]
    nmat = list(a_ab)
    pw = list(a_ab)
    for _ in range(max(1, (C - 1).bit_length() - 1)):
        pw = [mm3(pw[p], pw[p]) for p in pairs]
        nmat = [nmat[p] + pw[p] + mm3(nmat[p], pw[p]) for p in pairs]
    akv = [mm(jnp.where(strict, gram[p][:G, G:], 0.0), v2[p]) for p in pairs]
    rhs = [jnp.concatenate([at2[p], akv[p]], axis=1) for p in pairs]
    pq = [(rhs[p] + mm(nmat[p], rhs[p])).astype(BF16) for p in pairs]
    ry = [mm(jnp.where(incl, gram[p][G:, :G], 0.0), pq[p]) for p in pairs]
    mv = [mm(jnp.where(incl, gram[p][G:, G:], 0.0), v2[p]) for p in pairs]
    ge = [_dot_tn(two(bh, p).astype(BF16), pq[p]) for p in pairs]
    kv = [_dot_tn(two(kh, p).astype(BF16), v2[p]) for p in pairs]
    for p in pairs:
        rr = rt2[p] + ry[p][:, :LANES]
        gm = eye * w_end[:, p * LANES:(p + 1) * LANES] + ge[p][:, :LANES]
        yh = mm(jnp.concatenate([rr, gm], axis=0), h_scr[p])
        y2 = yh[:G] + ry[p][:, LANES:] + mv[p]
        h_scr[p] = yh[G:] + ge[p][:, LANES:] + kv[p]
        o_ref[:, p * LANES:(p + 1) * LANES] = y2[:C, :] + y2[C:, :]


def _rwkv_scan(r, lw, k, v, a, b, batch, seq):
    w = r.shape[1]
    nc = seq // CHUNK
    spec = pl.BlockSpec((CHUNK, w), lambda bi, ci: (bi * nc + ci, 0))
    return pl.pallas_call(
        _rwkv_scan_kernel,
        grid=(batch, nc),
        in_specs=[spec] * 6,
        out_specs=spec,
        out_shape=jax.ShapeDtypeStruct((batch * seq, w), F32),
        scratch_shapes=[pltpu.VMEM((w // LANES, LANES, LANES), F32)],
        compiler_params=_cparams(("parallel", "arbitrary")),
        name="rwkv_scan",
    )(r, lw, k, v, a, b)


def _mix_out_kernel(o_ref, gate_ref, y_ref, bonus_ref, g_ref, x_ref, g1_ref, lnw_ref, lnb_ref, w_ref, out_ref):
    fox = o_ref[...] * jax.nn.sigmoid(gate_ref[...])
    bd = _head_block_diag()
    y = y_ref[...]
    inv_n = 1.0 / RWKV_HEAD_DIM
    mean = _head_sum(y, bd) * inv_n
    d = y - mean
    var = _head_sum(d * d, bd) * inv_n
    yn = d * lax.rsqrt(var + GN_EPS) * lnw_ref[...] + lnb_ref[...]
    rw = (yn + bonus_ref[...].astype(F32)) * g_ref[...].astype(F32)
    wf = fox.shape[1]
    mix = _dot(fox.astype(BF16), w_ref[:wf, :]) + _dot(rw.astype(BF16), w_ref[wf:, :])
    out_ref[...] = x_ref[...] + g1_ref[0] * mix


def _mix_out(o_fox, proj, gate_col, y, bonus, g, x2, g1, ln_w, ln_b, w_out_bf, tokens_per_batch):
    t, d = x2.shape
    wf = o_fox.shape[1]
    wr = y.shape[1]
    tm = min(512, tokens_per_batch)
    tpb = tokens_per_batch // tm
    return pl.pallas_call(
        _mix_out_kernel,
        grid=(t // tm,),
        in_specs=[pl.BlockSpec((tm, wf), lambda i: (i, 0)),
                  pl.BlockSpec((tm, wf), lambda i: (i, gate_col // wf)),
                  pl.BlockSpec((tm, wr), lambda i: (i, 0)),
                  pl.BlockSpec((tm, wr), lambda i: (i, 0)),
                  pl.BlockSpec((tm, wr), lambda i: (i, 0)),
                  pl.BlockSpec((tm, d), lambda i: (i, 0)),
                  pl.BlockSpec((1, 1, d), lambda i: (i // tpb, 0, 0)),
                  pl.BlockSpec((1, wr), lambda i: (0, 0)),
                  pl.BlockSpec((1, wr), lambda i: (0, 0)),
                  pl.BlockSpec(w_out_bf.shape, lambda i: (0, 0))],
        out_specs=pl.BlockSpec((tm, d), lambda i: (i, 0)),
        out_shape=jax.ShapeDtypeStruct((t, d), F32),
        compiler_params=_cparams(("parallel",)),
        name="mix_out",
    )(o_fox, proj, y, bonus, g, x2, g1, ln_w, ln_b, w_out_bf)


def _top_k_mask_rows(s, k, tie_safe, want_rank=True):
    n = s.shape[0]
    s0 = s
    row = lax.broadcasted_iota(jnp.int32, s.shape, 0).astype(F32)
    rank = jnp.full(s.shape, float(k), F32) if want_rank else None
    vals = []
    for r in range(k):
        m = jnp.max(s, axis=0, keepdims=True)
        hit = s == m
        if tie_safe:
            pos = jnp.min(jnp.where(hit, row, float(n)), axis=0, keepdims=True)
            hit = row == pos
        vals.append(m)
        if want_rank:
            rank = jnp.where(hit, float(r), rank)
        s = jnp.where(hit, -jnp.inf, s)
    picked = (s == -jnp.inf) & (s0 != -jnp.inf)
    n_picked = jnp.sum(jnp.where(picked, 1.0, 0.0), axis=0, keepdims=True)
    return jnp.concatenate(vals, axis=0), rank, picked, jnp.where(n_picked == float(k), 1.0, 0.0)


def _candidate_rows():
    k = PEER_TOPK
    groups, valid = [], []
    for r0 in range(k // 2):
        n1 = k // (r0 + 1)
        for g in range(-(-n1 // 8)):
            groups.append((r0, g * 8))
            valid.append([g * 8 + i < n1 for i in range(8)])
    groups.append((None, k // 2))
    valid.append([True] * 8)
    return groups, valid


def _route_head(sa, sb, tie_safe):
    k = PEER_TOPK
    tm = sa.shape[1]
    groups, valid = _candidate_rows()
    sub8 = lax.broadcasted_iota(jnp.int32, (8, tm), 0)
    top_a, rank_a, _, clean_a = _top_k_mask_rows(sa, k, tie_safe)
    top_b, rank_b, _, clean_b = _top_k_mask_rows(sb, k, tie_safe)
    parts = []
    for (r0, off), ok in zip(groups, valid):
        if r0 is None:
            part = top_a[off:off + 8, :] + top_b[0:1, :]
        else:
            part = top_a[r0:r0 + 1, :] + top_b[off:off + 8, :]
        if not all(ok):
            part = jnp.where(sub8 < sum(ok), part, -jnp.inf)
        parts.append(part)
    cand = jnp.concatenate(parts, axis=0)
    _, _, sel, clean_c = _top_k_mask_rows(cand, k, tie_safe, want_rank=False)
    z = jnp.sum(jnp.where(sel, jnp.exp(cand - cand[0:1, :]), 0.0), axis=0, keepdims=True)
    self32 = jnp.where(sel, 1.0, 0.0)
    n_keys = sa.shape[0]
    rank3 = rank_a.astype(BF16).reshape(n_keys // BF16_ROWS, BF16_ROWS, tm)
    length3 = jnp.zeros(rank3.shape, BF16)

    def add_count(r0, count_row):
        tile = jnp.broadcast_to(count_row, (BF16_ROWS, tm)).astype(BF16)
        return length3 + jnp.where(rank3 == float(r0), tile[None], 0.0).astype(BF16)

    tail = None
    per_rank = {}
    for gi, (r0, off) in enumerate(groups):
        cnt = self32[gi * 8:(gi + 1) * 8, :]
        if r0 is None:
            tail = cnt
        else:
            c = jnp.sum(cnt, axis=0, keepdims=True)
            per_rank[r0] = c if r0 not in per_rank else per_rank[r0] + c
    for r in range(8):
        per_rank[k // 2 + r] = tail[r:r + 1, :]
    for r0, c in per_rank.items():
        length3 = add_count(r0, c)
    length = length3.reshape(n_keys, tm).astype(F32)
    pa = jnp.exp(sa - top_a[0:1, :])
    qb = jnp.exp(sb - top_b[0:1, :]) / z
    return pa, length, qb, rank_b, clean_a * clean_b * clean_c


def _peer_query_kernel(x_ref, nw_ref, sc_ref, sh_ref, wq_ref, ht_ref, q_ref):
    h = _modulated_norm(x_ref[...], nw_ref[...], sc_ref[0], sh_ref[0])
    ht_ref[...] = h.T.astype(BF16)
    q = _dot(h.astype(BF16), wq_ref[...])
    dk = q.shape[1] // PEER_HEADS
    for hd in range(PEER_HEADS):
        q_ref[hd] = q[:, hd * dk:(hd + 1) * dk]


def _peer_query(x1, norm_w, sc, sh, wq_bf, tokens_per_batch):
    t, d = x1.shape
    dk = wq_bf.shape[1] // PEER_HEADS
    tm = min(512, tokens_per_batch)
    tpb = tokens_per_batch // tm
    return pl.pallas_call(
        _peer_query_kernel,
        grid=(t // tm,),
        in_specs=[pl.BlockSpec((tm, d), lambda i: (i, 0)),
                  pl.BlockSpec((1, d), lambda i: (0, 0)),
                  pl.BlockSpec((1, 1, d), lambda i: (i // tpb, 0, 0)),
                  pl.BlockSpec((1, 1, d), lambda i: (i // tpb, 0, 0)),
                  pl.BlockSpec(wq_bf.shape, lambda i: (0, 0))],
        out_specs=[pl.BlockSpec((d, tm), lambda i: (0, i)),
                   pl.BlockSpec((PEER_HEADS, tm, dk), lambda i: (0, i, 0))],
        out_shape=[jax.ShapeDtypeStruct((d, t), BF16),
                   jax.ShapeDtypeStruct((PEER_HEADS, t, dk), F32)],
        compiler_params=_cparams(("parallel",)),
        name="peer_query",
    )(x1, norm_w, sc, sh, wq_bf)


def _gelu_exact(x):
    return 0.5 * x * (1.0 + lax.erf(x * 0.7071067811865476))


def _peer_route_act_kernel(ht_ref, q_ref, keys_ref, u_ref,
                           act_ref, pa_ref, len_ref, qb_ref, rnk_ref, code_scr):
    hd = pl.program_id(1)
    tm = ht_ref.shape[1]
    half = keys_ref.shape[-1]
    sub = min(ROUTE_SUB_TOKENS, tm)
    blocked = (PEER_NKEYS // PEER_SLABS, PEER_SLABS, sub)

    @pl.when(hd == 0)
    def _():
        code_scr[...] = jnp.zeros(code_scr.shape, F32)

    act_ref[...] = _gelu_exact(_dot(u_ref[...], ht_ref[...])).astype(BF16)

    def route(h, t0, tie_safe):
        q = q_ref[h, t0:t0 + sub, :]
        sa = _dot_nt(keys_ref[h, 0], q[:, :half], HIGHEST)
        sb = _dot_nt(keys_ref[h, 1], q[:, half:], HIGHEST)
        pa, length, qbt, rank_b, clean = _route_head(sa, sb, tie_safe)
        pa_ref[h, :, :, t0:t0 + sub] = pa.reshape(blocked)
        len_ref[h, :, :, t0:t0 + sub] = length.reshape(blocked)
        qb_ref[h, :, t0:t0 + sub] = qbt.astype(BF16)
        rnk_ref[h, :, t0:t0 + sub] = rank_b.astype(BF16)
        return clean

    dirty = jnp.zeros((1, 1), F32)
    for t0 in range(0, tm, sub):
        dirty = jnp.maximum(dirty, 1.0 - jnp.min(route(hd, t0, False), axis=1, keepdims=True))
    code_scr[0:1, 0:1] += dirty * lax.shift_left(1, hd).astype(F32)

    @pl.when(hd == pl.num_programs(1) - 1)
    def _():
        tie_bits = code_scr[0, 0].astype(jnp.int32)

        def redo(h, carry):
            @pl.when(((tie_bits >> h) & 1) == 1)
            def _():
                for t0 in range(0, tm, sub):
                    route(h, t0, True)
            return carry

        lax.fori_loop(0, PEER_HEADS, redo, 0)


def _peer_route_act(ht_bf, q_heads, sub_keys, u_bf, tokens_per_batch):
    d, t = ht_bf.shape
    ne = u_bf.shape[0]
    bu = ne // PEER_HEADS
    dk = q_heads.shape[-1]
    tm = min(512, tokens_per_batch)
    n_blocks = PEER_NKEYS // PEER_SLABS
    dense = jax.ShapeDtypeStruct((PEER_HEADS, PEER_NKEYS, t), BF16)
    dense_spec = pl.BlockSpec((PEER_HEADS, PEER_NKEYS, tm), lambda i, j: (0, 0, i))
    blocked = jax.ShapeDtypeStruct((PEER_HEADS, n_blocks, PEER_SLABS, t), F32)
    blocked_spec = pl.BlockSpec((PEER_HEADS, n_blocks, PEER_SLABS, tm), lambda i, j: (0, 0, 0, i))
    return pl.pallas_call(
        _peer_route_act_kernel,
        grid=(t // tm, PEER_HEADS),
        in_specs=[pl.BlockSpec((d, tm), lambda i, j: (0, i)),
                  pl.BlockSpec((PEER_HEADS, tm, dk), lambda i, j: (0, i, 0)),
                  pl.BlockSpec(sub_keys.shape, lambda i, j: (0, 0, 0, 0)),
                  pl.BlockSpec((bu, d), lambda i, j: (j, 0))],
        out_specs=[pl.BlockSpec((bu, tm), lambda i, j: (j, i)),
                   blocked_spec, blocked_spec, dense_spec, dense_spec],
        out_shape=[jax.ShapeDtypeStruct((ne, t), BF16), blocked, blocked, dense, dense],
        scratch_shapes=[pltpu.VMEM((8, LANES), F32)],
        compiler_params=_cparams(("parallel", "arbitrary")),
        name="peer_route_act",
    )(ht_bf, q_heads, sub_keys, u_bf)


def _peer_expert_kernel(act_ref, vt_ref, pa_ref, len_ref, qb_ref, rnk_ref, x_ref, g2_ref, o_ref, acc_scr):
    j = pl.program_id(1)
    be, tm = act_ref.shape

    @pl.when(j == 0)
    def _():
        acc_scr[...] = jnp.zeros(acc_scr.shape, F32)

    n_slabs = be // PEER_NKEYS
    rows = BF16_ROWS
    n_groups = PEER_NKEYS // rows

    def row_tile(ref, hd, s):
        return jnp.broadcast_to(ref[hd, 0, s:s + 1, :], (rows, tm)).astype(BF16)

    zero = jnp.zeros((rows, tm), BF16)
    parts = [[None] * n_groups for _ in range(n_slabs)]
    slab_group = 8
    for s0 in range(0, n_slabs, slab_group):
        ss = range(s0, min(s0 + slab_group, n_slabs))
        ln = {(hd, s): row_tile(len_ref, hd, s) for hd in range(PEER_HEADS) for s in ss}
        pa = {(hd, s): row_tile(pa_ref, hd, s) for hd in range(PEER_HEADS) for s in ss}
        for g in range(n_groups):
            sl = slice(g * rows, (g + 1) * rows)
            acc = {}
            for hd in range(PEER_HEADS):
                rk = rnk_ref[hd, sl, :]
                qv = qb_ref[hd, sl, :]
                for s in ss:
                    term = jnp.where(rk < ln[hd, s], qv, zero) * pa[hd, s]
                    acc[s] = term if s not in acc else acc[s] + term
            for s in ss:
                lo = s * PEER_NKEYS + g * rows
                parts[s][g] = acc[s] * act_ref[lo:lo + rows, :]
    p = jnp.concatenate([parts[s][g] for s in range(n_slabs) for g in range(n_groups)], axis=0)
    acc_scr[...] += _dot(vt_ref[0], p)

    @pl.when(j == pl.num_programs(1) - 1)
    def _():
        o_ref[...] = x_ref[...] + g2_ref[0] * acc_scr[...].T


def _peer_expert(act, vt_bf, pa, ln, qb, rnk, x1, g2, tokens_per_batch):
    t, d = x1.shape
    ne = act.shape[0]
    tm = min(512, tokens_per_batch)
    tpb = tokens_per_batch // tm
    be = PEER_EXPERT_BLOCK
    dense_spec = pl.BlockSpec((PEER_HEADS, PEER_NKEYS, tm), lambda i, j: (0, 0, i))
    blocked_spec = pl.BlockSpec((PEER_HEADS, 1, PEER_SLABS, tm), lambda i, j: (0, j, 0, i))
    return pl.pallas_call(
        _peer_expert_kernel,
        grid=(t // tm, ne // be),
        in_specs=[pl.BlockSpec((be, tm), lambda i, j: (j, i)),
                  pl.BlockSpec((1, d, be), lambda i, j: (j, 0, 0)),
                  blocked_spec, blocked_spec, dense_spec, dense_spec,
                  pl.BlockSpec((tm, d), lambda i, j: (i, 0)),
                  pl.BlockSpec((1, 1, d), lambda i, j: (i // tpb, 0, 0))],
        out_specs=pl.BlockSpec((tm, d), lambda i, j: (i, 0)),
        out_shape=jax.ShapeDtypeStruct((t, d), F32),
        scratch_shapes=[pltpu.VMEM((d, tm), F32)],
        compiler_params=_cparams(("parallel", "arbitrary")),
        name="peer_expert",
    )(act, vt_bf, pa, ln, qb, rnk, x1, g2)


def _pad_cols(a, n):
    return jnp.pad(a, ((0, 0), (0, n - a.shape[1])))


def _pad_rows(a, n):
    return jnp.pad(a, ((0, n - a.shape[0]), (0, 0)))


def _layer(x, c, w_ada, b_ada, norm_mix_w, w_in, fox_q_norm_w, fox_k_norm_w, fox_f_bias,
           rwkv_mu, rwkv_w0, rwkv_w_up, rwkv_a0, rwkv_a_up, rwkv_g_up, rwkv_k_k, rwkv_k_a,
           rwkv_r_k, rwkv_ln_w, rwkv_ln_b, w_out, norm_ffn_w, peer_w_query, peer_sub_keys,
           peer_u, peer_v):
    B, S, D = x.shape
    T = B * S
    fw = FOX_HEADS * FOX_HEAD_DIM
    rw = RWKV_HEADS * RWKV_HEAD_DIM
    w_lora = rwkv_w_up.shape[0]
    a_lora = rwkv_a_up.shape[0]
    g_lora = rwkv_g_up.shape[0]
    assert w_lora <= LANES and a_lora + FOX_HEADS <= LANES and g_lora == 256

    c_pad = _pad_rows(c, 8)
    mod = _ada_mod(c_pad, w_ada, b_ada)[:B]
    sh1, sc1, g1, sh2, sc2, g2 = [m.reshape(B, 1, D) for m in jnp.split(mod, 6, axis=-1)]

    fox_cols = 4 * fw + FOX_HEADS
    wi_fox, wi_rw = w_in[:, :fox_cols], w_in[:, fox_cols:]
    mu = rwkv_mu.reshape(1, -1)
    seg = lambda a, lo, n: a[:, lo:lo + n]
    w_perm = jnp.concatenate([
        seg(wi_fox, 0, 4 * fw),
        seg(wi_rw, 0, 3 * rw),
        seg(wi_rw, 3 * rw + w_lora + a_lora, g_lora),
        _pad_cols(seg(wi_rw, 3 * rw, w_lora), LANES),
        _pad_cols(jnp.concatenate([seg(wi_rw, 3 * rw + w_lora, a_lora), seg(wi_fox, 4 * fw, FOX_HEADS)], 1), LANES),
    ], axis=1).astype(BF16)
    cols = {"gate": 3 * fw, "rr": 4 * fw, "rk": 4 * fw + rw, "rv": 4 * fw + 2 * rw,
            "gd": 4 * fw + 3 * rw, "lo": 4 * fw + 3 * rw + g_lora}
    f_lane = a_lora
    f_block = (cols["lo"] + LANES) // LANES
    mus = [seg(mu, 0, rw), seg(mu, rw, rw), seg(mu, 2 * rw, rw),
           seg(mu, 3 * rw + w_lora + a_lora, g_lora),
           jnp.concatenate([_pad_cols(seg(mu, 3 * rw, w_lora), LANES),
                            _pad_cols(seg(mu, 3 * rw + w_lora, a_lora), LANES)], 1)]
    scale = FOX_HEAD_DIM ** -0.5 * LOG2_E
    head_w = _pad_cols(jnp.concatenate([jnp.tile(fox_q_norm_w * scale, FOX_HEADS),
                                        jnp.tile(fox_k_norm_w, FOX_HEADS)]).reshape(1, -1), w_perm.shape[1])

    x2 = x.reshape(T, D)
    proj = _in_proj(_norm_mod(x2, norm_mix_w.reshape(1, D), sc1, sh1, S), w_perm, head_w, 2 * fw)

    f_bias_row = jnp.zeros((1, LANES), F32).at[0, f_lane:f_lane + FOX_HEADS].set(fox_f_bias)
    cum = _fox_cum(proj, f_bias_row, B, S, f_block)
    cum_rows = cum.reshape(B, S, LANES)[:, :, f_lane:f_lane + FOX_HEADS].transpose(0, 2, 1).reshape(B, FOX_HEADS, 1, S)
    o_fox = _fox_attn(proj, cum_rows, B, S)

    row = lambda a: a.reshape(1, -1)
    r, lw, k, v, a_vec, b_vec, g, bonus = _rwkv_prep(
        proj, cols, mus, row(rwkv_w0), _pad_rows(rwkv_w_up, LANES), row(rwkv_a0), _pad_rows(rwkv_a_up, LANES),
        rwkv_g_up, row(rwkv_k_k), row(rwkv_k_a), row(rwkv_r_k), S)
    y = _rwkv_scan(r, lw, k, v, a_vec, b_vec, B, S)

    x1 = _mix_out(o_fox, proj, cols["gate"], y, bonus, g, x2, g1, row(rwkv_ln_w), row(rwkv_ln_b),
                  w_out.astype(BF16), S)

    ht, q_heads = _peer_query(x1, norm_ffn_w.reshape(1, D), sc2, sh2, peer_w_query.astype(BF16), S)
    act, pa, ln, qb, rnk = _peer_route_act(ht, q_heads, peer_sub_keys, peer_u.astype(BF16), S)
    vt = peer_v.astype(BF16).reshape(-1, PEER_EXPERT_BLOCK, D).transpose(0, 2, 1)
    out = _peer_expert(act, vt, pa, ln, qb, rnk, x1, g2, S)
    return out.reshape(B, S, D)


def kernel(x, c, w_ada, b_ada, norm_mix_w, w_in, fox_q_norm_w, fox_k_norm_w, fox_f_bias, rwkv_mu, rwkv_w0,
           rwkv_w_up, rwkv_a0, rwkv_a_up, rwkv_g_up, rwkv_k_k, rwkv_k_a, rwkv_r_k, rwkv_ln_w, rwkv_ln_b,
           w_out, norm_ffn_w, peer_w_query, peer_sub_keys, peer_u, peer_v):
    params = (w_ada, b_ada, norm_mix_w, w_in, fox_q_norm_w, fox_k_norm_w, fox_f_bias, rwkv_mu, rwkv_w0,
              rwkv_w_up, rwkv_a0, rwkv_a_up, rwkv_g_up, rwkv_k_k, rwkv_k_a, rwkv_r_k, rwkv_ln_w, rwkv_ln_b,
              w_out, norm_ffn_w, peer_w_query, peer_sub_keys, peer_u, peer_v)
    for l in range(w_ada.shape[0]):
        x = _layer(x, c, *[p[l] for p in params])
    return x
```

```python
import functools

import jax
import jax.numpy as jnp
from jax import lax
from jax.experimental import pallas as pl
from jax.experimental.pallas import tpu as pltpu

F32 = jnp.float32
BF16 = jnp.bfloat16
HIGHEST = lax.Precision.HIGHEST

LANES = 128
BF16_ROWS = 16
NORM_EPS = 1e-6
GN_EPS = 64e-5
CHUNK = 64
FOX_HEADS = 8
FOX_HEAD_DIM = 128
RWKV_HEADS = 16
RWKV_HEAD_DIM = 64
PEER_HEADS = 8
PEER_NKEYS = 128
PEER_TOPK = 16
PEER_EXPERT_BLOCK = 1024
PEER_SLABS = PEER_EXPERT_BLOCK // PEER_NKEYS
ROUTE_SUB_TOKENS = 128
NEG_BIG = -1e30
LOG2_E = 1.4426950408889634
VMEM_LIMIT = 56 * 1024 * 1024


def _cparams(sem):
    return pltpu.CompilerParams(dimension_semantics=sem, vmem_limit_bytes=VMEM_LIMIT)


def _dot(a, b, precision=None):
    return jnp.dot(a, b, preferred_element_type=F32, precision=precision)


def _dot_nt(a, b, precision=None):
    return lax.dot_general(a, b, (((1,), (1,)), ((), ())), preferred_element_type=F32,
                           precision=precision)


def _split_bf16(x):
    hi = x.astype(BF16)
    return hi, (x - hi.astype(F32)).astype(BF16)


def _dot_bf16x3(a, b):
    ah, al = _split_bf16(a)
    bh, bl = _split_bf16(b)
    return _dot(ah, bh) + (_dot(ah, bl) + _dot(al, bh))


def _dot_tn(a, b, precision=None):
    return lax.dot_general(a, b, (((0,), (0,)), ((), ())), preferred_element_type=F32,
                           precision=precision)


def _ada_kernel(c_ref, w_ref, b_ref, o_ref):
    c = c_ref[...]
    s = c * jax.nn.sigmoid(c)
    o_ref[...] = _dot(s, w_ref[...], HIGHEST) + b_ref[...]


def _ada_mod(c_pad, w_ada, b_ada):
    rows, d = c_pad.shape
    n = w_ada.shape[1]
    bn = 1024
    return pl.pallas_call(
        _ada_kernel,
        grid=(n // bn,),
        in_specs=[pl.BlockSpec((rows, d), lambda j: (0, 0)),
                  pl.BlockSpec((d, bn), lambda j: (0, j)),
                  pl.BlockSpec((1, bn), lambda j: (0, j))],
        out_specs=pl.BlockSpec((rows, bn), lambda j: (0, j)),
        out_shape=jax.ShapeDtypeStruct((rows, n), F32),
        compiler_params=_cparams(("arbitrary",)),
        name="ada_mod",
    )(c_pad, w_ada, b_ada.reshape(1, n))


def _modulated_norm(x, nw, sc, sh):
    y = x * lax.rsqrt(jnp.mean(x * x, axis=-1, keepdims=True) + NORM_EPS)
    return y * nw * (1.0 + sc) + sh


def _norm_mod_kernel(x_ref, nw_ref, sc_ref, sh_ref, o_ref):
    o_ref[...] = _modulated_norm(x_ref[...], nw_ref[...], sc_ref[0], sh_ref[0]).astype(BF16)


def _norm_mod(x2, norm_w, sc, sh, tokens_per_batch):
    t, d = x2.shape
    tm = min(512, tokens_per_batch)
    tpb = tokens_per_batch // tm
    return pl.pallas_call(
        _norm_mod_kernel,
        grid=(t // tm,),
        in_specs=[pl.BlockSpec((tm, d), lambda i: (i, 0)),
                  pl.BlockSpec((1, d), lambda i: (0, 0)),
                  pl.BlockSpec((1, 1, d), lambda i: (i // tpb, 0, 0)),
                  pl.BlockSpec((1, 1, d), lambda i: (i // tpb, 0, 0))],
        out_specs=pl.BlockSpec((tm, d), lambda i: (i, 0)),
        out_shape=jax.ShapeDtypeStruct((t, d), BF16),
        compiler_params=_cparams(("parallel",)),
        name="norm_mod",
    )(x2, norm_w, sc, sh)


def _in_proj_kernel(h_ref, w_ref, hw_ref, o_ref, *, n_qk_blocks):
    j = pl.program_id(1)
    acc = _dot(h_ref[...], w_ref[...])

    @pl.when(j < n_qk_blocks)
    def _():
        for hh in range(acc.shape[1] // FOX_HEAD_DIM):
            sl = slice(hh * FOX_HEAD_DIM, (hh + 1) * FOX_HEAD_DIM)
            a = acc[:, sl]
            rs = lax.rsqrt(jnp.mean(a * a, axis=-1, keepdims=True) + NORM_EPS)
            o_ref[:, sl] = a * rs * hw_ref[:, sl]

    @pl.when(j >= n_qk_blocks)
    def _():
        o_ref[...] = acc


def _in_proj(h_bf, w_bf, head_w, n_qk_cols):
    t, d = h_bf.shape
    n = w_bf.shape[1]
    tm = min(2048, t)
    bn = 512
    kern = functools.partial(_in_proj_kernel, n_qk_blocks=n_qk_cols // bn)
    return pl.pallas_call(
        kern,
        grid=(t // tm, n // bn),
        in_specs=[pl.BlockSpec((tm, d), lambda i, j: (i, 0)),
                  pl.BlockSpec((d, bn), lambda i, j: (0, j)),
                  pl.BlockSpec((1, bn), lambda i, j: (0, j))],
        out_specs=pl.BlockSpec((tm, bn), lambda i, j: (i, j)),
        out_shape=jax.ShapeDtypeStruct((t, n), F32),
        compiler_params=_cparams(("parallel", "arbitrary")),
        name="in_proj",
    )(h_bf, w_bf, head_w)


def _log_sigmoid(x):
    return jnp.minimum(x, 0.0) - jnp.log(1.0 + jnp.exp(-jnp.abs(x)))


def _fox_cum_kernel(f_ref, b_ref, o_ref, *, blk):
    s = f_ref.shape[0]
    row = lax.broadcasted_iota(jnp.int32, (blk, blk), 0)
    col = lax.broadcasted_iota(jnp.int32, (blk, blk), 1)
    tri = (row >= col).astype(F32)
    carry = jnp.zeros((1, f_ref.shape[1]), F32)
    for i in range(s // blk):
        lf = _log_sigmoid(f_ref[i * blk:(i + 1) * blk, :] + b_ref[...])
        cs = _dot(tri, lf, HIGHEST) + carry
        o_ref[i * blk:(i + 1) * blk, :] = cs * LOG2_E
        carry = cs[blk - 1:blk, :]


def _fox_cum(proj, f_bias_row, batch, seq, col_block):
    blk = min(256, seq)
    return pl.pallas_call(
        functools.partial(_fox_cum_kernel, blk=blk),
        grid=(batch,),
        in_specs=[pl.BlockSpec((seq, LANES), lambda b: (b, col_block)),
                  pl.BlockSpec((1, LANES), lambda b: (0, 0))],
        out_specs=pl.BlockSpec((seq, LANES), lambda b: (b, 0)),
        out_shape=jax.ShapeDtypeStruct((batch * seq, LANES), F32),
        compiler_params=_cparams(("parallel",)),
        name="fox_cum",
    )(proj, f_bias_row)


def _fox_attn_kernel(qt_ref, kt_ref, q_ref, k_ref, v_ref, c_ref, o_ref, m_scr, l_scr, acc_scr):
    t = pl.program_id(2)
    qi = qt_ref[t]
    ki = kt_ref[t]
    hd = FOX_HEAD_DIM
    n_heads = q_ref.shape[1] // hd

    @pl.when(ki == 0)
    def _():
        m_scr[...] = jnp.full(m_scr.shape, NEG_BIG, F32)
        l_scr[...] = jnp.zeros(l_scr.shape, F32)
        acc_scr[...] = jnp.zeros(acc_scr.shape, F32)

    def step(masked):
        hs = range(n_heads)
        sl = [slice(h * hd, (h + 1) * hd) for h in hs]
        s = [_dot_nt(q_ref[:, sl[h]].astype(BF16), k_ref[:, sl[h]].astype(BF16)) - c_ref[0, h] for h in hs]
        if masked:
            row = lax.broadcasted_iota(jnp.int32, s[0].shape, 0)
            col = lax.broadcasted_iota(jnp.int32, s[0].shape, 1)
            s = [jnp.where(row >= col, x, NEG_BIG) for x in s]
        m_prev = [m_scr[h] for h in hs]
        m_new = [jnp.maximum(m_prev[h], jnp.max(s[h], axis=-1, keepdims=True)) for h in hs]
        n_rep = s[0].shape[1] // hd
        p = [jnp.exp2(s[h] - jnp.concatenate([m_new[h]] * n_rep, axis=1)) for h in hs]
        alpha = [jnp.exp2(m_prev[h] - m_new[h]) for h in hs]
        ones = jnp.ones((k_ref.shape[0], hd), BF16)
        pv = [_dot(p[h].astype(BF16), jnp.concatenate([v_ref[:, sl[h]].astype(BF16), ones], axis=1))
              for h in hs]
        for h in hs:
            l_scr[h] = alpha[h] * l_scr[h] + pv[h][:, hd:]
            acc_scr[:, sl[h]] = alpha[h] * acc_scr[:, sl[h]] + pv[h][:, :hd]
            m_scr[h] = m_new[h]

    @pl.when(ki < qi)
    def _():
        step(False)

    @pl.when(ki == qi)
    def _():
        step(True)
        for h in range(n_heads):
            sl = slice(h * hd, (h + 1) * hd)
            o_ref[:, sl] = acc_scr[:, sl] / l_scr[h]


def _fox_attn(proj, cum_rows, batch, seq):
    tq = min(512, seq)
    nq = seq // tq
    hps = 4
    w = hps * FOX_HEAD_DIM
    kcol = FOX_HEADS // hps
    tri = [(q, k) for q in range(nq) for k in range(q + 1)]
    qt = jnp.asarray([q for q, _ in tri], jnp.int32)
    kt = jnp.asarray([k for _, k in tri], jnp.int32)
    grid_spec = pltpu.PrefetchScalarGridSpec(
        num_scalar_prefetch=2,
        grid=(batch, kcol, len(tri)),
        in_specs=[
            pl.BlockSpec((tq, w), lambda b, h, t, qt, kt: (b * nq + qt[t], h)),
            pl.BlockSpec((tq, w), lambda b, h, t, qt, kt: (b * nq + kt[t], kcol + h)),
            pl.BlockSpec((tq, w), lambda b, h, t, qt, kt: (b * nq + kt[t], 2 * kcol + h)),
            pl.BlockSpec((1, hps, 1, tq), lambda b, h, t, qt, kt: (b, h, 0, kt[t])),
        ],
        out_specs=pl.BlockSpec((tq, w), lambda b, h, t, qt, kt: (b * nq + qt[t], h)),
        scratch_shapes=[pltpu.VMEM((hps, tq, FOX_HEAD_DIM), F32), pltpu.VMEM((hps, tq, FOX_HEAD_DIM), F32),
                        pltpu.VMEM((tq, w), F32)],
    )
    return pl.pallas_call(
        _fox_attn_kernel,
        grid_spec=grid_spec,
        out_shape=jax.ShapeDtypeStruct((batch * seq, FOX_HEADS * FOX_HEAD_DIM), F32),
        compiler_params=_cparams(("parallel", "parallel", "arbitrary")),
        name="fox_attn",
    )(qt, kt, proj, proj, proj, cum_rows)


def _head_sum(x, bd):
    hi, lo = _split_bf16(x)
    parts = [_dot(hi[:, j * LANES:(j + 1) * LANES], bd) + _dot(lo[:, j * LANES:(j + 1) * LANES], bd)
             for j in range(x.shape[1] // LANES)]
    return jnp.concatenate(parts, axis=1)


def _head_block_diag():
    r = lax.broadcasted_iota(jnp.int32, (LANES, LANES), 0) // RWKV_HEAD_DIM
    c = lax.broadcasted_iota(jnp.int32, (LANES, LANES), 1) // RWKV_HEAD_DIM
    return (r == c).astype(BF16)


def _rwkv_prep_kernel(r_ref, k_ref, v_ref, gd_ref, lo_ref, pr_ref, pk_ref, pv_ref, pgd_ref, plo_ref,
                      mu_r, mu_k, mu_v, mu_gd, mu_lo, w0_ref, wup_ref, a0_ref, aup_ref, gup_ref,
                      kk_ref, ka_ref, rk_ref,
                      or_ref, olw_ref, ok_ref, ov_ref, oa_ref, ob_ref, og_ref, obonus_ref, *, tiles_per_batch):
    i = pl.program_id(0)
    first = (i % tiles_per_batch) == 0

    def shifted(cur_ref, prv_ref, mu_ref):
        cur = cur_ref[...]
        last = jnp.where(first, 0.0, prv_ref[7:8, :])
        row = lax.broadcasted_iota(jnp.int32, cur.shape, 0)
        prev = jnp.where(row == 0, last, pltpu.roll(cur, 1, 0))
        return cur + (prev - cur) * mu_ref[...]

    r = shifted(r_ref, pr_ref, mu_r)
    k = shifted(k_ref, pk_ref, mu_k)
    v = shifted(v_ref, pv_ref, mu_v)
    gd = shifted(gd_ref, pgd_ref, mu_gd)
    lo = shifted(lo_ref, plo_ref, mu_lo)
    wd = lo[:, :LANES]
    ad = lo[:, LANES:]

    w_pre = w0_ref[...] + _dot_bf16x3(jnp.tanh(wd), wup_ref[...])
    w_raw = _log_sigmoid(w_pre) - 0.5
    log_decay = -jnp.exp(w_raw)
    a = jax.nn.sigmoid(a0_ref[...] + _dot_bf16x3(ad, aup_ref[...]))
    g = _dot_bf16x3(jax.nn.sigmoid(gd), gup_ref[...])

    bd = _head_block_diag()
    kk = k * kk_ref[...]
    nrm = jnp.maximum(jnp.sqrt(_head_sum(kk * kk, bd)), 1e-12)
    kk = kk / nrm
    k_mod = k * (1.0 + (a - 1.0) * ka_ref[...])
    bonus = _head_sum(r * k_mod * rk_ref[...], bd) * v

    or_ref[...] = r.astype(BF16)
    olw_ref[...] = log_decay
    ok_ref[...] = k_mod.astype(BF16)
    ov_ref[...] = v.astype(BF16)
    oa_ref[...] = (-kk).astype(BF16)
    ob_ref[...] = (kk * a).astype(BF16)
    og_ref[...] = g.astype(BF16)
    obonus_ref[...] = bonus.astype(BF16)


def _rwkv_prep(proj, cols, mus, w0, w_up, a0, a_up, g_up, k_k, k_a, r_k, tokens_per_batch):
    t = proj.shape[0]
    w = RWKV_HEADS * RWKV_HEAD_DIM
    tm = min(512, tokens_per_batch)
    tpb = tokens_per_batch // tm
    widths = [w, w, w, 256, 256]
    offs = [cols["rr"], cols["rk"], cols["rv"], cols["gd"], cols["lo"]]
    cur_specs = [pl.BlockSpec((tm, wd), functools.partial(lambda i, cb: (i, cb), cb=o // wd))
                 for wd, o in zip(widths, offs)]
    prv_specs = [pl.BlockSpec((8, wd), functools.partial(
        lambda i, cb: (jnp.maximum(i * (tm // 8) - 1, 0), cb), cb=o // wd))
        for wd, o in zip(widths, offs)]
    full = lambda a: pl.BlockSpec(a.shape, lambda i: (0,) * a.ndim)
    params = list(mus) + [w0, w_up, a0, a_up, g_up, k_k, k_a, r_k]
    out_spec = pl.BlockSpec((tm, w), lambda i: (i, 0))
    return pl.pallas_call(
        functools.partial(_rwkv_prep_kernel, tiles_per_batch=tpb),
        grid=(t // tm,),
        in_specs=cur_specs + prv_specs + [full(p) for p in params],
        out_specs=[out_spec] * 8,
        out_shape=[jax.ShapeDtypeStruct((t, w), dt) for dt in (BF16, F32, BF16, BF16, BF16, BF16, BF16, BF16)],
        compiler_params=_cparams(("parallel",)),
        name="rwkv_prep",
    )(*([proj] * 10), *params)


def _rwkv_scan_kernel(r_ref, lw_ref, k_ref, v_ref, a_ref, b_ref, o_ref, h_scr):
    c = pl.program_id(1)

    @pl.when(c == 0)
    def _():
        h_scr[...] = jnp.zeros(h_scr.shape, F32)

    C = r_ref.shape[0]
    n_pairs = r_ref.shape[1] // LANES
    P = HIGHEST

    row = lax.broadcasted_iota(jnp.int32, (C, C), 0)
    col = lax.broadcasted_iota(jnp.int32, (C, C), 1)
    tri = (row >= col).astype(F32)
    lw = lw_ref[...]
    cw = _dot(tri, lw, P)
    cw_end = cw[C - 1:C, :]
    e_pos = jnp.exp(cw)
    e_prev = jnp.exp(cw - lw)
    e_neg = jnp.exp(-cw)
    e_end = jnp.exp(cw_end - cw)
    w_end = jnp.exp(cw_end)

    a = a_ref[...].astype(F32)
    b = b_ref[...].astype(F32)
    k = k_ref[...].astype(F32)
    r = r_ref[...].astype(F32)
    at = a * e_prev
    bt = b * e_neg
    kt = k * e_neg
    rt = r * e_pos
    bh = b * e_end
    kh = k * e_end
    v = v_ref[...].astype(F32)

    lane = lax.broadcasted_iota(jnp.int32, (C, LANES), 1)
    head0 = lane < RWKV_HEAD_DIM
    r2 = lax.broadcasted_iota(jnp.int32, (2 * C, 2 * C), 0)
    c2 = lax.broadcasted_iota(jnp.int32, (2 * C, 2 * C), 1)
    same = (r2 // C) == (c2 // C)
    strict = same & (r2 > c2)
    incl = same & (r2 >= c2)
    eye = (r2 == c2).astype(F32)

    def two(x, p):
        xp = x[:, p * LANES:(p + 1) * LANES]
        return jnp.concatenate([jnp.where(head0, xp, 0.0), jnp.where(head0, 0.0, xp)], axis=0)

    def mm(x, y):
        return _dot(x.astype(BF16), y.astype(BF16))

    def split(x):
        hi = x.astype(BF16)
        return hi, (x - hi.astype(F32)).astype(BF16)

    def mm3(x, y):
        xh, xl = split(x)
        yh, yl = split(y)
        return _dot(xh, yh) + (_dot(xh, yl) + _dot(xl, yh))

    G = 2 * C
    pairs = range(n_pairs)
    at2 = [two(at, p) for p in pairs]
    rt2 = [two(rt, p) for p in pairs]
    v2 = [two(v, p).astype(BF16) for p in pairs]
    gram = [_dot_nt(jnp.concatenate([at2[p], rt2[p]], axis=0).astype(BF16),
                    jnp.concatenate([two(bt, p), two(kt, p)], axis=0).astype(BF16)) for p in pairs]
    a_ab = [jnp.where(strict, gram[p][:G, :G], 0.0) for p in pairs]
    nmat = list(a_ab)
    pw = list(a_ab)
    for _ in range(max(1, (C - 1).bit_length() - 1)):
        pw = [mm3(pw[p], pw[p]) for p in pairs]
        nmat = [nmat[p] + pw[p] + mm3(nmat[p], pw[p]) for p in pairs]
    akv = [mm(jnp.where(strict, gram[p][:G, G:], 0.0), v2[p]) for p in pairs]
    rhs = [jnp.concatenate([at2[p], akv[p]], axis=1) for p in pairs]
    pq = [(rhs[p] + mm(nmat[p], rhs[p])).astype(BF16) for p in pairs]
    ry = [mm(jnp.where(incl, gram[p][G:, :G], 0.0), pq[p]) for p in pairs]
    mv = [mm(jnp.where(incl, gram[p][G:, G:], 0.0), v2[p]) for p in pairs]
    ge = [_dot_tn(two(bh, p).astype(BF16), pq[p]) for p in pairs]
    kv = [_dot_tn(two(kh, p).astype(BF16), v2[p]) for p in pairs]
    for p in pairs:
        rr = rt2[p] + ry[p][:, :LANES]
        gm = eye * w_end[:, p * LANES:(p + 1) * LANES] + ge[p][:, :LANES]
        yh = mm(jnp.concatenate([rr, gm], axis=0), h_scr[p])
        y2 = yh[:G] + ry[p][:, LANES:] + mv[p]
        h_scr[p] = yh[G:] + ge[p][:, LANES:] + kv[p]
        o_ref[:, p * LANES:(p + 1) * LANES] = y2[:C, :] + y2[C:, :]


def _rwkv_scan(r, lw, k, v, a, b, batch, seq):
    w = r.shape[1]
    nc = seq // CHUNK
    spec = pl.BlockSpec((CHUNK, w), lambda bi, ci: (bi * nc + ci, 0))
    return pl.pallas_call(
        _rwkv_scan_kernel,
        grid=(batch, nc),
        in_specs=[spec] * 6,
        out_specs=spec,
        out_shape=jax.ShapeDtypeStruct((batch * seq, w), F32),
        scratch_shapes=[pltpu.VMEM((w // LANES, LANES, LANES), F32)],
        compiler_params=_cparams(("parallel", "arbitrary")),
        name="rwkv_scan",
    )(r, lw, k, v, a, b)


def _mix_out_kernel(o_ref, gate_ref, y_ref, bonus_ref, g_ref, x_ref, g1_ref, lnw_ref, lnb_ref, w_ref, out_ref):
    fox = o_ref[...] * jax.nn.sigmoid(gate_ref[...])
    bd = _head_block_diag()
    y = y_ref[...]
    inv_n = 1.0 / RWKV_HEAD_DIM
    mean = _head_sum(y, bd) * inv_n
    d = y - mean
    var = _head_sum(d * d, bd) * inv_n
    yn = d * lax.rsqrt(var + GN_EPS) * lnw_ref[...] + lnb_ref[...]
    rw = (yn + bonus_ref[...].astype(F32)) * g_ref[...].astype(F32)
    wf = fox.shape[1]
    mix = _dot(fox.astype(BF16), w_ref[:wf, :]) + _dot(rw.astype(BF16), w_ref[wf:, :])
    out_ref[...] = x_ref[...] + g1_ref[0] * mix


def _mix_out(o_fox, proj, gate_col, y, bonus, g, x2, g1, ln_w, ln_b, w_out_bf, tokens_per_batch):
    t, d = x2.shape
    wf = o_fox.shape[1]
    wr = y.shape[1]
    tm = min(512, tokens_per_batch)
    tpb = tokens_per_batch // tm
    return pl.pallas_call(
        _mix_out_kernel,
        grid=(t // tm,),
        in_specs=[pl.BlockSpec((tm, wf), lambda i: (i, 0)),
                  pl.BlockSpec((tm, wf), lambda i: (i, gate_col // wf)),
                  pl.BlockSpec((tm, wr), lambda i: (i, 0)),
                  pl.BlockSpec((tm, wr), lambda i: (i, 0)),
                  pl.BlockSpec((tm, wr), lambda i: (i, 0)),
                  pl.BlockSpec((tm, d), lambda i: (i, 0)),
                  pl.BlockSpec((1, 1, d), lambda i: (i // tpb, 0, 0)),
                  pl.BlockSpec((1, wr), lambda i: (0, 0)),
                  pl.BlockSpec((1, wr), lambda i: (0, 0)),
                  pl.BlockSpec(w_out_bf.shape, lambda i: (0, 0))],
        out_specs=pl.BlockSpec((tm, d), lambda i: (i, 0)),
        out_shape=jax.ShapeDtypeStruct((t, d), F32),
        compiler_params=_cparams(("parallel",)),
        name="mix_out",
    )(o_fox, proj, y, bonus, g, x2, g1, ln_w, ln_b, w_out_bf)


def _top_k_mask_rows(s, k, tie_safe, want_rank=True):
    n = s.shape[0]
    s0 = s
    row = lax.broadcasted_iota(jnp.int32, s.shape, 0).astype(F32)
    rank = jnp.full(s.shape, float(k), F32) if want_rank else None
    vals = []
    for r in range(k):
        m = jnp.max(s, axis=0, keepdims=True)
        hit = s == m
        if tie_safe:
            pos = jnp.min(jnp.where(hit, row, float(n)), axis=0, keepdims=True)
            hit = row == pos
        vals.append(m)
        if want_rank:
            rank = jnp.where(hit, float(r), rank)
        s = jnp.where(hit, -jnp.inf, s)
    picked = (s == -jnp.inf) & (s0 != -jnp.inf)
    n_picked = jnp.sum(jnp.where(picked, 1.0, 0.0), axis=0, keepdims=True)
    return jnp.concatenate(vals, axis=0), rank, picked, jnp.where(n_picked == float(k), 1.0, 0.0)


def _candidate_rows():
    k = PEER_TOPK
    groups, valid = [], []
    for r0 in range(k // 2):
        n1 = k // (r0 + 1)
        for g in range(-(-n1 // 8)):
            groups.append((r0, g * 8))
            valid.append([g * 8 + i < n1 for i in range(8)])
    groups.append((None, k // 2))
    valid.append([True] * 8)
    return groups, valid


def _route_head(sa, sb, tie_safe):
    k = PEER_TOPK
    tm = sa.shape[1]
    groups, valid = _candidate_rows()
    sub8 = lax.broadcasted_iota(jnp.int32, (8, tm), 0)
    top_a, rank_a, _, clean_a = _top_k_mask_rows(sa, k, tie_safe)
    top_b, rank_b, _, clean_b = _top_k_mask_rows(sb, k, tie_safe)
    parts = []
    for (r0, off), ok in zip(groups, valid):
        if r0 is None:
            part = top_a[off:off + 8, :] + top_b[0:1, :]
        else:
            part = top_a[r0:r0 + 1, :] + top_b[off:off + 8, :]
        if not all(ok):
            part = jnp.where(sub8 < sum(ok), part, -jnp.inf)
        parts.append(part)
    cand = jnp.concatenate(parts, axis=0)
    _, _, sel, clean_c = _top_k_mask_rows(cand, k, tie_safe, want_rank=False)
    z = jnp.sum(jnp.where(sel, jnp.exp(cand - cand[0:1, :]), 0.0), axis=0, keepdims=True)
    self32 = jnp.where(sel, 1.0, 0.0)
    n_keys = sa.shape[0]
    rank3 = rank_a.astype(BF16).reshape(n_keys // BF16_ROWS, BF16_ROWS, tm)
    length3 = jnp.zeros(rank3.shape, BF16)

    def add_count(r0, count_row):
        tile = jnp.broadcast_to(count_row, (BF16_ROWS, tm)).astype(BF16)
        return length3 + jnp.where(rank3 == float(r0), tile[None], 0.0).astype(BF16)

    tail = None
    per_rank = {}
    for gi, (r0, off) in enumerate(groups):
        cnt = self32[gi * 8:(gi + 1) * 8, :]
        if r0 is None:
            tail = cnt
        else:
            c = jnp.sum(cnt, axis=0, keepdims=True)
            per_rank[r0] = c if r0 not in per_rank else per_rank[r0] + c
    for r in range(8):
        per_rank[k // 2 + r] = tail[r:r + 1, :]
    for r0, c in per_rank.items():
        length3 = add_count(r0, c)
    length = length3.reshape(n_keys, tm).astype(F32)
    pa = jnp.exp(sa - top_a[0:1, :])
    qb = jnp.exp(sb - top_b[0:1, :]) / z
    return pa, length, qb, rank_b, clean_a * clean_b * clean_c


def _peer_query_kernel(x_ref, nw_ref, sc_ref, sh_ref, wq_ref, ht_ref, q_ref):
    h = _modulated_norm(x_ref[...], nw_ref[...], sc_ref[0], sh_ref[0])
    ht_ref[...] = h.T.astype(BF16)
    q = _dot(h.astype(BF16), wq_ref[...])
    dk = q.shape[1] // PEER_HEADS
    for hd in range(PEER_HEADS):
        q_ref[hd] = q[:, hd * dk:(hd + 1) * dk]


def _peer_query(x1, norm_w, sc, sh, wq_bf, tokens_per_batch):
    t, d = x1.shape
    dk = wq_bf.shape[1] // PEER_HEADS
    tm = min(512, tokens_per_batch)
    tpb = tokens_per_batch // tm
    return pl.pallas_call(
        _peer_query_kernel,
        grid=(t // tm,),
        in_specs=[pl.BlockSpec((tm, d), lambda i: (i, 0)),
                  pl.BlockSpec((1, d), lambda i: (0, 0)),
                  pl.BlockSpec((1, 1, d), lambda i: (i // tpb, 0, 0)),
                  pl.BlockSpec((1, 1, d), lambda i: (i // tpb, 0, 0)),
                  pl.BlockSpec(wq_bf.shape, lambda i: (0, 0))],
        out_specs=[pl.BlockSpec((d, tm), lambda i: (0, i)),
                   pl.BlockSpec((PEER_HEADS, tm, dk), lambda i: (0, i, 0))],
        out_shape=[jax.ShapeDtypeStruct((d, t), BF16),
                   jax.ShapeDtypeStruct((PEER_HEADS, t, dk), F32)],
        compiler_params=_cparams(("parallel",)),
        name="peer_query",
    )(x1, norm_w, sc, sh, wq_bf)


def _gelu_exact(x):
    return 0.5 * x * (1.0 + lax.erf(x * 0.7071067811865476))


def _peer_route_act_kernel(ht_ref, q_ref, keys_ref, u_ref,
                           act_ref, pa_ref, len_ref, qb_ref, rnk_ref, code_scr):
    hd = pl.program_id(1)
    tm = ht_ref.shape[1]
    half = keys_ref.shape[-1]
    sub = min(ROUTE_SUB_TOKENS, tm)
    blocked = (PEER_NKEYS // PEER_SLABS, PEER_SLABS, sub)

    @pl.when(hd == 0)
    def _():
        code_scr[...] = jnp.zeros(code_scr.shape, F32)

    act_ref[...] = _gelu_exact(_dot(u_ref[...], ht_ref[...])).astype(BF16)

    def route(h, t0, tie_safe):
        q = q_ref[h, t0:t0 + sub, :]
        sa = _dot_nt(keys_ref[h, 0], q[:, :half], HIGHEST)
        sb = _dot_nt(keys_ref[h, 1], q[:, half:], HIGHEST)
        pa, length, qbt, rank_b, clean = _route_head(sa, sb, tie_safe)
        pa_ref[h, :, :, t0:t0 + sub] = pa.reshape(blocked)
        len_ref[h, :, :, t0:t0 + sub] = length.reshape(blocked)
        qb_ref[h, :, t0:t0 + sub] = qbt.astype(BF16)
        rnk_ref[h, :, t0:t0 + sub] = rank_b.astype(BF16)
        return clean

    dirty = jnp.zeros((1, 1), F32)
    for t0 in range(0, tm, sub):
        dirty = jnp.maximum(dirty, 1.0 - jnp.min(route(hd, t0, False), axis=1, keepdims=True))
    code_scr[0:1, 0:1] += dirty * lax.shift_left(1, hd).astype(F32)

    @pl.when(hd == pl.num_programs(1) - 1)
    def _():
        tie_bits = code_scr[0, 0].astype(jnp.int32)

        def redo(h, carry):
            @pl.when(((tie_bits >> h) & 1) == 1)
            def _():
                for t0 in range(0, tm, sub):
                    route(h, t0, True)
            return carry

        lax.fori_loop(0, PEER_HEADS, redo, 0)


def _peer_route_act(ht_bf, q_heads, sub_keys, u_bf, tokens_per_batch):
    d, t = ht_bf.shape
    ne = u_bf.shape[0]
    bu = ne // PEER_HEADS
    dk = q_heads.shape[-1]
    tm = min(512, tokens_per_batch)
    n_blocks = PEER_NKEYS // PEER_SLABS
    dense = jax.ShapeDtypeStruct((PEER_HEADS, PEER_NKEYS, t), BF16)
    dense_spec = pl.BlockSpec((PEER_HEADS, PEER_NKEYS, tm), lambda i, j: (0, 0, i))
    blocked = jax.ShapeDtypeStruct((PEER_HEADS, n_blocks, PEER_SLABS, t), F32)
    blocked_spec = pl.BlockSpec((PEER_HEADS, n_blocks, PEER_SLABS, tm), lambda i, j: (0, 0, 0, i))
    return pl.pallas_call(
        _peer_route_act_kernel,
        grid=(t // tm, PEER_HEADS),
        in_specs=[pl.BlockSpec((d, tm), lambda i, j: (0, i)),
                  pl.BlockSpec((PEER_HEADS, tm, dk), lambda i, j: (0, i, 0)),
                  pl.BlockSpec(sub_keys.shape, lambda i, j: (0, 0, 0, 0)),
                  pl.BlockSpec((bu, d), lambda i, j: (j, 0))],
        out_specs=[pl.BlockSpec((bu, tm), lambda i, j: (j, i)),
                   blocked_spec, blocked_spec, dense_spec, dense_spec],
        out_shape=[jax.ShapeDtypeStruct((ne, t), BF16), blocked, blocked, dense, dense],
        scratch_shapes=[pltpu.VMEM((8, LANES), F32)],
        compiler_params=_cparams(("parallel", "arbitrary")),
        name="peer_route_act",
    )(ht_bf, q_heads, sub_keys, u_bf)


def _peer_expert_kernel(act_ref, vt_ref, pa_ref, len_ref, qb_ref, rnk_ref, x_ref, g2_ref, o_ref, acc_scr):
    j = pl.program_id(1)
    be, tm = act_ref.shape

    @pl.when(j == 0)
    def _():
        acc_scr[...] = jnp.zeros(acc_scr.shape, F32)

    n_slabs = be // PEER_NKEYS
    rows = BF16_ROWS
    n_groups = PEER_NKEYS // rows

    def row_tile(ref, hd, s):
        return jnp.broadcast_to(ref[hd, 0, s:s + 1, :], (rows, tm)).astype(BF16)

    zero = jnp.zeros((rows, tm), BF16)
    parts = [[None] * n_groups for _ in range(n_slabs)]
    slab_group = 8
    for s0 in range(0, n_slabs, slab_group):
        ss = range(s0, min(s0 + slab_group, n_slabs))
        ln = {(hd, s): row_tile(len_ref, hd, s) for hd in range(PEER_HEADS) for s in ss}
        pa = {(hd, s): row_tile(pa_ref, hd, s) for hd in range(PEER_HEADS) for s in ss}
        for g in range(n_groups):
            sl = slice(g * rows, (g + 1) * rows)
            acc = {}
            for hd in range(PEER_HEADS):
                rk = rnk_ref[hd, sl, :]
                qv = qb_ref[hd, sl, :]
                for s in ss:
                    term = jnp.where(rk < ln[hd, s], qv, zero) * pa[hd, s]
                    acc[s] = term if s not in acc else acc[s] + term
            for s in ss:
                lo = s * PEER_NKEYS + g * rows
                parts[s][g] = acc[s] * act_ref[lo:lo + rows, :]
    p = jnp.concatenate([parts[s][g] for s in range(n_slabs) for g in range(n_groups)], axis=0)
    acc_scr[...] += _dot(vt_ref[0], p)

    @pl.when(j == pl.num_programs(1) - 1)
    def _():
        o_ref[...] = x_ref[...] + g2_ref[0] * acc_scr[...].T


def _peer_expert(act, vt_bf, pa, ln, qb, rnk, x1, g2, tokens_per_batch):
    t, d = x1.shape
    ne = act.shape[0]
    tm = min(512, tokens_per_batch)
    tpb = tokens_per_batch // tm
    be = PEER_EXPERT_BLOCK
    dense_spec = pl.BlockSpec((PEER_HEADS, PEER_NKEYS, tm), lambda i, j: (0, 0, i))
    blocked_spec = pl.BlockSpec((PEER_HEADS, 1, PEER_SLABS, tm), lambda i, j: (0, j, 0, i))
    return pl.pallas_call(
        _peer_expert_kernel,
        grid=(t // tm, ne // be),
        in_specs=[pl.BlockSpec((be, tm), lambda i, j: (j, i)),
                  pl.BlockSpec((1, d, be), lambda i, j: (j, 0, 0)),
                  blocked_spec, blocked_spec, dense_spec, dense_spec,
                  pl.BlockSpec((tm, d), lambda i, j: (i, 0)),
                  pl.BlockSpec((1, 1, d), lambda i, j: (i // tpb, 0, 0))],
        out_specs=pl.BlockSpec((tm, d), lambda i, j: (i, 0)),
        out_shape=jax.ShapeDtypeStruct((t, d), F32),
        scratch_shapes=[pltpu.VMEM((d, tm), F32)],
        compiler_params=_cparams(("parallel", "arbitrary")),
        name="peer_expert",
    )(act, vt_bf, pa, ln, qb, rnk, x1, g2)


def _pad_cols(a, n):
    return jnp.pad(a, ((0, 0), (0, n - a.shape[1])))


def _pad_rows(a, n):
    return jnp.pad(a, ((0, n - a.shape[0]), (0, 0)))


def _layer(x, c, w_ada, b_ada, norm_mix_w, w_in, fox_q_norm_w, fox_k_norm_w, fox_f_bias,
           rwkv_mu, rwkv_w0, rwkv_w_up, rwkv_a0, rwkv_a_up, rwkv_g_up, rwkv_k_k, rwkv_k_a,
           rwkv_r_k, rwkv_ln_w, rwkv_ln_b, w_out, norm_ffn_w, peer_w_query, peer_sub_keys,
           peer_u, peer_v):
    B, S, D = x.shape
    T = B * S
    fw = FOX_HEADS * FOX_HEAD_DIM
    rw = RWKV_HEADS * RWKV_HEAD_DIM
    w_lora = rwkv_w_up.shape[0]
    a_lora = rwkv_a_up.shape[0]
    g_lora = rwkv_g_up.shape[0]
    assert w_lora <= LANES and a_lora + FOX_HEADS <= LANES and g_lora == 256

    c_pad = _pad_rows(c, 8)
    mod = _ada_mod(c_pad, w_ada, b_ada)[:B]
    sh1, sc1, g1, sh2, sc2, g2 = [m.reshape(B, 1, D) for m in jnp.split(mod, 6, axis=-1)]

    fox_cols = 4 * fw + FOX_HEADS
    wi_fox, wi_rw = w_in[:, :fox_cols], w_in[:, fox_cols:]
    mu = rwkv_mu.reshape(1, -1)
    seg = lambda a, lo, n: a[:, lo:lo + n]
    w_perm = jnp.concatenate([
        seg(wi_fox, 0, 4 * fw),
        seg(wi_rw, 0, 3 * rw),
        seg(wi_rw, 3 * rw + w_lora + a_lora, g_lora),
        _pad_cols(seg(wi_rw, 3 * rw, w_lora), LANES),
        _pad_cols(jnp.concatenate([seg(wi_rw, 3 * rw + w_lora, a_lora), seg(wi_fox, 4 * fw, FOX_HEADS)], 1), LANES),
    ], axis=1).astype(BF16)
    cols = {"gate": 3 * fw, "rr": 4 * fw, "rk": 4 * fw + rw, "rv": 4 * fw + 2 * rw,
            "gd": 4 * fw + 3 * rw, "lo": 4 * fw + 3 * rw + g_lora}
    f_lane = a_lora
    f_block = (cols["lo"] + LANES) // LANES
    mus = [seg(mu, 0, rw), seg(mu, rw, rw), seg(mu, 2 * rw, rw),
           seg(mu, 3 * rw + w_lora + a_lora, g_lora),
           jnp.concatenate([_pad_cols(seg(mu, 3 * rw, w_lora), LANES),
                            _pad_cols(seg(mu, 3 * rw + w_lora, a_lora), LANES)], 1)]
    scale = FOX_HEAD_DIM ** -0.5 * LOG2_E
    head_w = _pad_cols(jnp.concatenate([jnp.tile(fox_q_norm_w * scale, FOX_HEADS),
                                        jnp.tile(fox_k_norm_w, FOX_HEADS)]).reshape(1, -1), w_perm.shape[1])

    x2 = x.reshape(T, D)
    proj = _in_proj(_norm_mod(x2, norm_mix_w.reshape(1, D), sc1, sh1, S), w_perm, head_w, 2 * fw)

    f_bias_row = jnp.zeros((1, LANES), F32).at[0, f_lane:f_lane + FOX_HEADS].set(fox_f_bias)
    cum = _fox_cum(proj, f_bias_row, B, S, f_block)
    cum_rows = cum.reshape(B, S, LANES)[:, :, f_lane:f_lane + FOX_HEADS].transpose(0, 2, 1).reshape(B, FOX_HEADS, 1, S)
    o_fox = _fox_attn(proj, cum_rows, B, S)

    row = lambda a: a.reshape(1, -1)
    r, lw, k, v, a_vec, b_vec, g, bonus = _rwkv_prep(
        proj, cols, mus, row(rwkv_w0), _pad_rows(rwkv_w_up, LANES), row(rwkv_a0), _pad_rows(rwkv_a_up, LANES),
        rwkv_g_up, row(rwkv_k_k), row(rwkv_k_a), row(rwkv_r_k), S)
    y = _rwkv_scan(r, lw, k, v, a_vec, b_vec, B, S)

    x1 = _mix_out(o_fox, proj, cols["gate"], y, bonus, g, x2, g1, row(rwkv_ln_w), row(rwkv_ln_b),
                  w_out.astype(BF16), S)

    ht, q_heads = _peer_query(x1, norm_ffn_w.reshape(1, D), sc2, sh2, peer_w_query.astype(BF16), S)
    act, pa, ln, qb, rnk = _peer_route_act(ht, q_heads, peer_sub_keys, peer_u.astype(BF16), S)
    vt = peer_v.astype(BF16).reshape(-1, PEER_EXPERT_BLOCK, D).transpose(0, 2, 1)
    out = _peer_expert(act, vt, pa, ln, qb, rnk, x1, g2, S)
    return out.reshape(B, S, D)


def kernel(x, c, w_ada, b_ada, norm_mix_w, w_in, fox_q_norm_w, fox_k_norm_w, fox_f_bias, rwkv_mu, rwkv_w0,
           rwkv_w_up, rwkv_a0, rwkv_a_up, rwkv_g_up, rwkv_k_k, rwkv_k_a, rwkv_r_k, rwkv_ln_w, rwkv_ln_b,
           w_out, norm_ffn_w, peer_w_query, peer_sub_keys, peer_u, peer_v):
    params = (w_ada, b_ada, norm_mix_w, w_in, fox_q_norm_w, fox_k_norm_w, fox_f_bias, rwkv_mu, rwkv_w0,
              rwkv_w_up, rwkv_a0, rwkv_a_up, rwkv_g_up, rwkv_k_k, rwkv_k_a, rwkv_r_k, rwkv_ln_w, rwkv_ln_b,
              w_out, norm_ffn_w, peer_w_query, peer_sub_keys, peer_u, peer_v)
    for l in range(w_ada.shape[0]):
        x = _layer(x, c, *[p[l] for p in params])
    return x
```

```python
import functools

import jax
import jax.numpy as jnp
from jax import lax
from jax.experimental import pallas as pl
from jax.experimental.pallas import tpu as pltpu

F32 = jnp.float32
BF16 = jnp.bfloat16
HIGHEST = lax.Precision.HIGHEST

LANES = 128
BF16_ROWS = 16
NORM_EPS = 1e-6
GN_EPS = 64e-5
CHUNK = 64
FOX_HEADS = 8
FOX_HEAD_DIM = 128
RWKV_HEADS = 16
RWKV_HEAD_DIM = 64
PEER_HEADS = 8
PEER_NKEYS = 128
PEER_TOPK = 16
PEER_EXPERT_BLOCK = 1024
PEER_SLABS = PEER_EXPERT_BLOCK // PEER_NKEYS
ROUTE_SUB_TOKENS = 256
NEG_BIG = -1e30
LOG2_E = 1.4426950408889634
VMEM_LIMIT = 56 * 1024 * 1024


def _cparams(sem):
    return pltpu.CompilerParams(dimension_semantics=sem, vmem_limit_bytes=VMEM_LIMIT)


def _dot(a, b, precision=None):
    return jnp.dot(a, b, preferred_element_type=F32, precision=precision)


def _dot_nt(a, b, precision=None):
    return lax.dot_general(a, b, (((1,), (1,)), ((), ())), preferred_element_type=F32,
                           precision=precision)


def _split_bf16(x):
    hi = x.astype(BF16)
    return hi, (x - hi.astype(F32)).astype(BF16)


def _dot_bf16x3(a, b):
    ah, al = _split_bf16(a)
    bh, bl = _split_bf16(b)
    return _dot(ah, bh) + (_dot(ah, bl) + _dot(al, bh))


def _dot_tn(a, b, precision=None):
    return lax.dot_general(a, b, (((0,), (0,)), ((), ())), preferred_element_type=F32,
                           precision=precision)


def _ada_kernel(c_ref, w_ref, b_ref, o_ref):
    c = c_ref[...]
    s = c * jax.nn.sigmoid(c)
    o_ref[...] = _dot(s, w_ref[...], HIGHEST) + b_ref[...]


def _ada_mod(c_pad, w_ada, b_ada):
    rows, d = c_pad.shape
    n = w_ada.shape[1]
    bn = 1024
    return pl.pallas_call(
        _ada_kernel,
        grid=(n // bn,),
        in_specs=[pl.BlockSpec((rows, d), lambda j: (0, 0)),
                  pl.BlockSpec((d, bn), lambda j: (0, j)),
                  pl.BlockSpec((1, bn), lambda j: (0, j))],
        out_specs=pl.BlockSpec((rows, bn), lambda j: (0, j)),
        out_shape=jax.ShapeDtypeStruct((rows, n), F32),
        compiler_params=_cparams(("arbitrary",)),
        name="ada_mod",
    )(c_pad, w_ada, b_ada.reshape(1, n))


def _modulated_norm(x, nw, sc, sh):
    y = x * lax.rsqrt(jnp.mean(x * x, axis=-1, keepdims=True) + NORM_EPS)
    return y * nw * (1.0 + sc) + sh


def _norm_mod_kernel(x_ref, nw_ref, sc_ref, sh_ref, o_ref):
    o_ref[...] = _modulated_norm(x_ref[...], nw_ref[...], sc_ref[0], sh_ref[0]).astype(BF16)


def _norm_mod(x2, norm_w, sc, sh, tokens_per_batch):
    t, d = x2.shape
    tm = min(512, tokens_per_batch)
    tpb = tokens_per_batch // tm
    return pl.pallas_call(
        _norm_mod_kernel,
        grid=(t // tm,),
        in_specs=[pl.BlockSpec((tm, d), lambda i: (i, 0)),
                  pl.BlockSpec((1, d), lambda i: (0, 0)),
                  pl.BlockSpec((1, 1, d), lambda i: (i // tpb, 0, 0)),
                  pl.BlockSpec((1, 1, d), lambda i: (i // tpb, 0, 0))],
        out_specs=pl.BlockSpec((tm, d), lambda i: (i, 0)),
        out_shape=jax.ShapeDtypeStruct((t, d), BF16),
        compiler_params=_cparams(("parallel",)),
        name="norm_mod",
    )(x2, norm_w, sc, sh)


def _in_proj_kernel(h_ref, w_ref, hw_ref, o_ref, *, n_qk_blocks):
    j = pl.program_id(1)
    acc = _dot(h_ref[...], w_ref[...])

    @pl.when(j < n_qk_blocks)
    def _():
        for hh in range(acc.shape[1] // FOX_HEAD_DIM):
            sl = slice(hh * FOX_HEAD_DIM, (hh + 1) * FOX_HEAD_DIM)
            a = acc[:, sl]
            rs = lax.rsqrt(jnp.mean(a * a, axis=-1, keepdims=True) + NORM_EPS)
            o_ref[:, sl] = a * rs * hw_ref[:, sl]

    @pl.when(j >= n_qk_blocks)
    def _():
        o_ref[...] = acc


def _in_proj(h_bf, w_bf, head_w, n_qk_cols):
    t, d = h_bf.shape
    n = w_bf.shape[1]
    tm = min(2048, t)
    bn = 512
    kern = functools.partial(_in_proj_kernel, n_qk_blocks=n_qk_cols // bn)
    return pl.pallas_call(
        kern,
        grid=(t // tm, n // bn),
        in_specs=[pl.BlockSpec((tm, d), lambda i, j: (i, 0)),
                  pl.BlockSpec((d, bn), lambda i, j: (0, j)),
                  pl.BlockSpec((1, bn), lambda i, j: (0, j))],
        out_specs=pl.BlockSpec((tm, bn), lambda i, j: (i, j)),
        out_shape=jax.ShapeDtypeStruct((t, n), F32),
        compiler_params=_cparams(("parallel", "arbitrary")),
        name="in_proj",
    )(h_bf, w_bf, head_w)


def _log_sigmoid(x):
    return jnp.minimum(x, 0.0) - jnp.log(1.0 + jnp.exp(-jnp.abs(x)))


def _fox_cum_kernel(f_ref, b_ref, o_ref, *, blk):
    s = f_ref.shape[0]
    row = lax.broadcasted_iota(jnp.int32, (blk, blk), 0)
    col = lax.broadcasted_iota(jnp.int32, (blk, blk), 1)
    tri = (row >= col).astype(F32)
    carry = jnp.zeros((1, f_ref.shape[1]), F32)
    for i in range(s // blk):
        lf = _log_sigmoid(f_ref[i * blk:(i + 1) * blk, :] + b_ref[...])
        cs = _dot(tri, lf, HIGHEST) + carry
        o_ref[i * blk:(i + 1) * blk, :] = cs * LOG2_E
        carry = cs[blk - 1:blk, :]


def _fox_cum(proj, f_bias_row, batch, seq, col_block):
    blk = min(256, seq)
    return pl.pallas_call(
        functools.partial(_fox_cum_kernel, blk=blk),
        grid=(batch,),
        in_specs=[pl.BlockSpec((seq, LANES), lambda b: (b, col_block)),
                  pl.BlockSpec((1, LANES), lambda b: (0, 0))],
        out_specs=pl.BlockSpec((seq, LANES), lambda b: (b, 0)),
        out_shape=jax.ShapeDtypeStruct((batch * seq, LANES), F32),
        compiler_params=_cparams(("parallel",)),
        name="fox_cum",
    )(proj, f_bias_row)


def _fox_attn_kernel(qt_ref, kt_ref, q_ref, k_ref, v_ref, c_ref, o_ref, m_scr, l_scr, acc_scr):
    t = pl.program_id(2)
    qi = qt_ref[t]
    ki = kt_ref[t]
    hd = FOX_HEAD_DIM
    n_heads = q_ref.shape[1] // hd

    @pl.when(ki == 0)
    def _():
        m_scr[...] = jnp.full(m_scr.shape, NEG_BIG, F32)
        l_scr[...] = jnp.zeros(l_scr.shape, F32)
        acc_scr[...] = jnp.zeros(acc_scr.shape, F32)

    def step(masked):
        hs = range(n_heads)
        sl = [slice(h * hd, (h + 1) * hd) for h in hs]
        s = [_dot_nt(q_ref[:, sl[h]].astype(BF16), k_ref[:, sl[h]].astype(BF16)) - c_ref[0, h] for h in hs]
        if masked:
            row = lax.broadcasted_iota(jnp.int32, s[0].shape, 0)
            col = lax.broadcasted_iota(jnp.int32, s[0].shape, 1)
            s = [jnp.where(row >= col, x, NEG_BIG) for x in s]
        m_prev = [m_scr[h] for h in hs]
        m_new = [jnp.maximum(m_prev[h], jnp.max(s[h], axis=-1, keepdims=True)) for h in hs]
        n_rep = s[0].shape[1] // hd
        p = [jnp.exp2(s[h] - jnp.concatenate([m_new[h]] * n_rep, axis=1)) for h in hs]
        alpha = [jnp.exp2(m_prev[h] - m_new[h]) for h in hs]
        ones = jnp.ones((k_ref.shape[0], hd), BF16)
        pv = [_dot(p[h].astype(BF16), jnp.concatenate([v_ref[:, sl[h]].astype(BF16), ones], axis=1))
              for h in hs]
        for h in hs:
            l_scr[h] = alpha[h] * l_scr[h] + pv[h][:, hd:]
            acc_scr[:, sl[h]] = alpha[h] * acc_scr[:, sl[h]] + pv[h][:, :hd]
            m_scr[h] = m_new[h]

    @pl.when(ki < qi)
    def _():
        step(False)

    @pl.when(ki == qi)
    def _():
        step(True)
        for h in range(n_heads):
            sl = slice(h * hd, (h + 1) * hd)
            o_ref[:, sl] = acc_scr[:, sl] / l_scr[h]


def _fox_attn(proj, cum_rows, batch, seq):
    tq = min(512, seq)
    nq = seq // tq
    hps = 4
    w = hps * FOX_HEAD_DIM
    kcol = FOX_HEADS // hps
    tri = [(q, k) for q in range(nq) for k in range(q + 1)]
    qt = jnp.asarray([q for q, _ in tri], jnp.int32)
    kt = jnp.asarray([k for _, k in tri], jnp.int32)
    grid_spec = pltpu.PrefetchScalarGridSpec(
        num_scalar_prefetch=2,
        grid=(batch, kcol, len(tri)),
        in_specs=[
            pl.BlockSpec((tq, w), lambda b, h, t, qt, kt: (b * nq + qt[t], h)),
            pl.BlockSpec((tq, w), lambda b, h, t, qt, kt: (b * nq + kt[t], kcol + h)),
            pl.BlockSpec((tq, w), lambda b, h, t, qt, kt: (b * nq + kt[t], 2 * kcol + h)),
            pl.BlockSpec((1, hps, 1, tq), lambda b, h, t, qt, kt: (b, h, 0, kt[t])),
        ],
        out_specs=pl.BlockSpec((tq, w), lambda b, h, t, qt, kt: (b * nq + qt[t], h)),
        scratch_shapes=[pltpu.VMEM((hps, tq, FOX_HEAD_DIM), F32), pltpu.VMEM((hps, tq, FOX_HEAD_DIM), F32),
                        pltpu.VMEM((tq, w), F32)],
    )
    return pl.pallas_call(
        _fox_attn_kernel,
        grid_spec=grid_spec,
        out_shape=jax.ShapeDtypeStruct((batch * seq, FOX_HEADS * FOX_HEAD_DIM), F32),
        compiler_params=_cparams(("parallel", "parallel", "arbitrary")),
        name="fox_attn",
    )(qt, kt, proj, proj, proj, cum_rows)


def _head_sum(x, bd):
    hi, lo = _split_bf16(x)
    parts = [_dot(hi[:, j * LANES:(j + 1) * LANES], bd) + _dot(lo[:, j * LANES:(j + 1) * LANES], bd)
             for j in range(x.shape[1] // LANES)]
    return jnp.concatenate(parts, axis=1)


def _head_block_diag():
    r = lax.broadcasted_iota(jnp.int32, (LANES, LANES), 0) // RWKV_HEAD_DIM
    c = lax.broadcasted_iota(jnp.int32, (LANES, LANES), 1) // RWKV_HEAD_DIM
    return (r == c).astype(BF16)


def _rwkv_prep_kernel(r_ref, k_ref, v_ref, gd_ref, lo_ref, pr_ref, pk_ref, pv_ref, pgd_ref, plo_ref,
                      mu_r, mu_k, mu_v, mu_gd, mu_lo, w0_ref, wup_ref, a0_ref, aup_ref, gup_ref,
                      kk_ref, ka_ref, rk_ref,
                      or_ref, olw_ref, ok_ref, ov_ref, oa_ref, ob_ref, og_ref, obonus_ref, *, tiles_per_batch):
    i = pl.program_id(0)
    first = (i % tiles_per_batch) == 0

    def shifted(cur_ref, prv_ref, mu_ref):
        cur = cur_ref[...]
        last = jnp.where(first, 0.0, prv_ref[7:8, :])
        row = lax.broadcasted_iota(jnp.int32, cur.shape, 0)
        prev = jnp.where(row == 0, last, pltpu.roll(cur, 1, 0))
        return cur + (prev - cur) * mu_ref[...]

    r = shifted(r_ref, pr_ref, mu_r)
    k = shifted(k_ref, pk_ref, mu_k)
    v = shifted(v_ref, pv_ref, mu_v)
    gd = shifted(gd_ref, pgd_ref, mu_gd)
    lo = shifted(lo_ref, plo_ref, mu_lo)
    wd = lo[:, :LANES]
    ad = lo[:, LANES:]

    w_pre = w0_ref[...] + _dot_bf16x3(jnp.tanh(wd), wup_ref[...])
    w_raw = _log_sigmoid(w_pre) - 0.5
    log_decay = -jnp.exp(w_raw)
    a = jax.nn.sigmoid(a0_ref[...] + _dot_bf16x3(ad, aup_ref[...]))
    g = _dot_bf16x3(jax.nn.sigmoid(gd), gup_ref[...])

    bd = _head_block_diag()
    kk = k * kk_ref[...]
    nrm = jnp.maximum(jnp.sqrt(_head_sum(kk * kk, bd)), 1e-12)
    kk = kk / nrm
    k_mod = k * (1.0 + (a - 1.0) * ka_ref[...])
    bonus = _head_sum(r * k_mod * rk_ref[...], bd) * v

    or_ref[...] = r.astype(BF16)
    olw_ref[...] = log_decay
    ok_ref[...] = k_mod.astype(BF16)
    ov_ref[...] = v.astype(BF16)
    oa_ref[...] = (-kk).astype(BF16)
    ob_ref[...] = (kk * a).astype(BF16)
    og_ref[...] = g.astype(BF16)
    obonus_ref[...] = bonus.astype(BF16)


def _rwkv_prep(proj, cols, mus, w0, w_up, a0, a_up, g_up, k_k, k_a, r_k, tokens_per_batch):
    t = proj.shape[0]
    w = RWKV_HEADS * RWKV_HEAD_DIM
    tm = min(512, tokens_per_batch)
    tpb = tokens_per_batch // tm
    widths = [w, w, w, 256, 256]
    offs = [cols["rr"], cols["rk"], cols["rv"], cols["gd"], cols["lo"]]
    cur_specs = [pl.BlockSpec((tm, wd), functools.partial(lambda i, cb: (i, cb), cb=o // wd))
                 for wd, o in zip(widths, offs)]
    prv_specs = [pl.BlockSpec((8, wd), functools.partial(
        lambda i, cb: (jnp.maximum(i * (tm // 8) - 1, 0), cb), cb=o // wd))
        for wd, o in zip(widths, offs)]
    full = lambda a: pl.BlockSpec(a.shape, lambda i: (0,) * a.ndim)
    params = list(mus) + [w0, w_up, a0, a_up, g_up, k_k, k_a, r_k]
    out_spec = pl.BlockSpec((tm, w), lambda i: (i, 0))
    return pl.pallas_call(
        functools.partial(_rwkv_prep_kernel, tiles_per_batch=tpb),
        grid=(t // tm,),
        in_specs=cur_specs + prv_specs + [full(p) for p in params],
        out_specs=[out_spec] * 8,
        out_shape=[jax.ShapeDtypeStruct((t, w), dt) for dt in (BF16, F32, BF16, BF16, BF16, BF16, BF16, BF16)],
        compiler_params=_cparams(("parallel",)),
        name="rwkv_prep",
    )(*([proj] * 10), *params)


def _rwkv_scan_kernel(r_ref, lw_ref, k_ref, v_ref, a_ref, b_ref, o_ref, h_scr):
    c = pl.program_id(1)

    @pl.when(c == 0)
    def _():
        h_scr[...] = jnp.zeros(h_scr.shape, F32)

    C = r_ref.shape[0]
    n_pairs = r_ref.shape[1] // LANES
    P = HIGHEST

    row = lax.broadcasted_iota(jnp.int32, (C, C), 0)
    col = lax.broadcasted_iota(jnp.int32, (C, C), 1)
    tri = (row >= col).astype(F32)
    lw = lw_ref[...]
    cw = _dot(tri, lw, P)
    cw_end = cw[C - 1:C, :]
    e_pos = jnp.exp(cw)
    e_prev = jnp.exp(cw - lw)
    e_neg = jnp.exp(-cw)
    e_end = jnp.exp(cw_end - cw)
    w_end = jnp.exp(cw_end)

    a = a_ref[...].astype(F32)
    b = b_ref[...].astype(F32)
    k = k_ref[...].astype(F32)
    r = r_ref[...].astype(F32)
    at = a * e_prev
    bt = b * e_neg
    kt = k * e_neg
    rt = r * e_pos
    bh = b * e_end
    kh = k * e_end
    v = v_ref[...].astype(F32)

    lane = lax.broadcasted_iota(jnp.int32, (C, LANES), 1)
    head0 = lane < RWKV_HEAD_DIM
    r2 = lax.broadcasted_iota(jnp.int32, (2 * C, 2 * C), 0)
    c2 = lax.broadcasted_iota(jnp.int32, (2 * C, 2 * C), 1)
    same = (r2 // C) == (c2 // C)
    strict = same & (r2 > c2)
    incl = same & (r2 >= c2)
    eye = (r2 == c2).astype(F32)

    def two(x, p):
        xp = x[:, p * LANES:(p + 1) * LANES]
        return jnp.concatenate([jnp.where(head0, xp, 0.0), jnp.where(head0, 0.0, xp)], axis=0)

    def mm(x, y):
        return _dot(x.astype(BF16), y.astype(BF16))

    def split(x):
        hi = x.astype(BF16)
        return hi, (x - hi.astype(F32)).astype(BF16)

    def mm3(x, y):
        xh, xl = split(x)
        yh, yl = split(y)
        return _dot(xh, yh) + (_dot(xh, yl) + _dot(xl, yh))

    G = 2 * C
    pairs = range(n_pairs)
    at2 = [two(at, p) for p in pairs]
    rt2 = [two(rt, p) for p in pairs]
    v2 = [two(v, p).astype(BF16) for p in pairs]
    gram = [_dot_nt(jnp.concatenate([at2[p], rt2[p]], axis=0).astype(BF16),
                    jnp.concatenate([two(bt, p), two(kt, p)], axis=0).astype(BF16)) for p in pairs]
    a_ab = [jnp.where(strict, gram[p][:G, :G], 0.0) for p in pairs]
    nmat = list(a_ab)
    pw = list(a_ab)
    for _ in range(max(1, (C - 1).bit_length() - 1)):
        pw = [mm3(pw[p], pw[p]) for p in pairs]
        nmat = [nmat[p] + pw[p] + mm3(nmat[p], pw[p]) for p in pairs]
    akv = [mm(jnp.where(strict, gram[p][:G, G:], 0.0), v2[p]) for p in pairs]
    rhs = [jnp.concatenate([at2[p], akv[p]], axis=1) for p in pairs]
    pq = [(rhs[p] + mm(nmat[p], rhs[p])).astype(BF16) for p in pairs]
    ry = [mm(jnp.where(incl, gram[p][G:, :G], 0.0), pq[p]) for p in pairs]
    mv = [mm(jnp.where(incl, gram[p][G:, G:], 0.0), v2[p]) for p in pairs]
    ge = [_dot_tn(two(bh, p).astype(BF16), pq[p]) for p in pairs]
    kv = [_dot_tn(two(kh, p).astype(BF16), v2[p]) for p in pairs]
    for p in pairs:
        rr = rt2[p] + ry[p][:, :LANES]
        gm = eye * w_end[:, p * LANES:(p + 1) * LANES] + ge[p][:, :LANES]
        yh = mm(jnp.concatenate([rr, gm], axis=0), h_scr[p])
        y2 = yh[:G] + ry[p][:, LANES:] + mv[p]
        h_scr[p] = yh[G:] + ge[p][:, LANES:] + kv[p]
        o_ref[:, p * LANES:(p + 1) * LANES] = y2[:C, :] + y2[C:, :]


def _rwkv_scan(r, lw, k, v, a, b, batch, seq):
    w = r.shape[1]
    nc = seq // CHUNK
    spec = pl.BlockSpec((CHUNK, w), lambda bi, ci: (bi * nc + ci, 0))
    return pl.pallas_call(
        _rwkv_scan_kernel,
        grid=(batch, nc),
        in_specs=[spec] * 6,
        out_specs=spec,
        out_shape=jax.ShapeDtypeStruct((batch * seq, w), F32),
        scratch_shapes=[pltpu.VMEM((w // LANES, LANES, LANES), F32)],
        compiler_params=_cparams(("parallel", "arbitrary")),
        name="rwkv_scan",
    )(r, lw, k, v, a, b)


def _mix_out_kernel(o_ref, gate_ref, y_ref, bonus_ref, g_ref, x_ref, g1_ref, lnw_ref, lnb_ref, w_ref, out_ref):
    fox = o_ref[...] * jax.nn.sigmoid(gate_ref[...])
    bd = _head_block_diag()
    y = y_ref[...]
    inv_n = 1.0 / RWKV_HEAD_DIM
    mean = _head_sum(y, bd) * inv_n
    d = y - mean
    var = _head_sum(d * d, bd) * inv_n
    yn = d * lax.rsqrt(var + GN_EPS) * lnw_ref[...] + lnb_ref[...]
    rw = (yn + bonus_ref[...].astype(F32)) * g_ref[...].astype(F32)
    wf = fox.shape[1]
    mix = _dot(fox.astype(BF16), w_ref[:wf, :]) + _dot(rw.astype(BF16), w_ref[wf:, :])
    out_ref[...] = x_ref[...] + g1_ref[0] * mix


def _mix_out(o_fox, proj, gate_col, y, bonus, g, x2, g1, ln_w, ln_b, w_out_bf, tokens_per_batch):
    t, d = x2.shape
    wf = o_fox.shape[1]
    wr = y.shape[1]
    tm = min(512, tokens_per_batch)
    tpb = tokens_per_batch // tm
    return pl.pallas_call(
        _mix_out_kernel,
        grid=(t // tm,),
        in_specs=[pl.BlockSpec((tm, wf), lambda i: (i, 0)),
                  pl.BlockSpec((tm, wf), lambda i: (i, gate_col // wf)),
                  pl.BlockSpec((tm, wr), lambda i: (i, 0)),
                  pl.BlockSpec((tm, wr), lambda i: (i, 0)),
                  pl.BlockSpec((tm, wr), lambda i: (i, 0)),
                  pl.BlockSpec((tm, d), lambda i: (i, 0)),
                  pl.BlockSpec((1, 1, d), lambda i: (i // tpb, 0, 0)),
                  pl.BlockSpec((1, wr), lambda i: (0, 0)),
                  pl.BlockSpec((1, wr), lambda i: (0, 0)),
                  pl.BlockSpec(w_out_bf.shape, lambda i: (0, 0))],
        out_specs=pl.BlockSpec((tm, d), lambda i: (i, 0)),
        out_shape=jax.ShapeDtypeStruct((t, d), F32),
        compiler_params=_cparams(("parallel",)),
        name="mix_out",
    )(o_fox, proj, y, bonus, g, x2, g1, ln_w, ln_b, w_out_bf)


def _top_k_mask_rows(s, k, tie_safe, want_rank=True):
    n = s.shape[0]
    s0 = s
    row = lax.broadcasted_iota(jnp.int32, s.shape, 0).astype(F32)
    rank = jnp.full(s.shape, float(k), F32) if want_rank else None
    vals = []
    for r in range(k):
        m = jnp.max(s, axis=0, keepdims=True)
        hit = s == m
        if tie_safe:
            pos = jnp.min(jnp.where(hit, row, float(n)), axis=0, keepdims=True)
            hit = row == pos
        vals.append(m)
        if want_rank:
            rank = jnp.where(hit, float(r), rank)
        s = jnp.where(hit, -jnp.inf, s)
    picked = (s == -jnp.inf) & (s0 != -jnp.inf)
    n_picked = jnp.sum(jnp.where(picked, 1.0, 0.0), axis=0, keepdims=True)
    return jnp.concatenate(vals, axis=0), rank, picked, jnp.where(n_picked == float(k), 1.0, 0.0)


def _candidate_rows():
    k = PEER_TOPK
    groups, valid = [], []
    for r0 in range(k // 2):
        n1 = k // (r0 + 1)
        for g in range(-(-n1 // 8)):
            groups.append((r0, g * 8))
            valid.append([g * 8 + i < n1 for i in range(8)])
    groups.append((None, k // 2))
    valid.append([True] * 8)
    return groups, valid


def _route_head(sa, sb, tie_safe):
    k = PEER_TOPK
    tm = sa.shape[1]
    groups, valid = _candidate_rows()
    sub8 = lax.broadcasted_iota(jnp.int32, (8, tm), 0)
    top_a, rank_a, _, clean_a = _top_k_mask_rows(sa, k, tie_safe)
    top_b, rank_b, _, clean_b = _top_k_mask_rows(sb, k, tie_safe)
    parts = []
    for (r0, off), ok in zip(groups, valid):
        if r0 is None:
            part = top_a[off:off + 8, :] + top_b[0:1, :]
        else:
            part = top_a[r0:r0 + 1, :] + top_b[off:off + 8, :]
        if not all(ok):
            part = jnp.where(sub8 < sum(ok), part, -jnp.inf)
        parts.append(part)
    cand = jnp.concatenate(parts, axis=0)
    _, _, sel, clean_c = _top_k_mask_rows(cand, k, tie_safe, want_rank=False)
    z = jnp.sum(jnp.where(sel, jnp.exp(cand - cand[0:1, :]), 0.0), axis=0, keepdims=True)
    self32 = jnp.where(sel, 1.0, 0.0)
    n_keys = sa.shape[0]
    rank3 = rank_a.astype(BF16).reshape(n_keys // BF16_ROWS, BF16_ROWS, tm)
    length3 = jnp.zeros(rank3.shape, BF16)

    def add_count(r0, count_row):
        tile = jnp.broadcast_to(count_row, (BF16_ROWS, tm)).astype(BF16)
        return length3 + jnp.where(rank3 == float(r0), tile[None], 0.0).astype(BF16)

    tail = None
    per_rank = {}
    for gi, (r0, off) in enumerate(groups):
        cnt = self32[gi * 8:(gi + 1) * 8, :]
        if r0 is None:
            tail = cnt
        else:
            c = jnp.sum(cnt, axis=0, keepdims=True)
            per_rank[r0] = c if r0 not in per_rank else per_rank[r0] + c
    for r in range(8):
        per_rank[k // 2 + r] = tail[r:r + 1, :]
    for r0, c in per_rank.items():
        length3 = add_count(r0, c)
    length = length3.reshape(n_keys, tm).astype(F32)
    pa = jnp.exp(sa - top_a[0:1, :])
    qb = jnp.exp(sb - top_b[0:1, :]) / z
    return pa, length, qb, rank_b, clean_a * clean_b * clean_c


def _peer_query_kernel(x_ref, nw_ref, sc_ref, sh_ref, wq_ref, ht_ref, q_ref):
    h = _modulated_norm(x_ref[...], nw_ref[...], sc_ref[0], sh_ref[0])
    ht_ref[...] = h.T.astype(BF16)
    q = _dot(h.astype(BF16), wq_ref[...])
    dk = q.shape[1] // PEER_HEADS
    for hd in range(PEER_HEADS):
        q_ref[hd] = q[:, hd * dk:(hd + 1) * dk]


def _peer_query(x1, norm_w, sc, sh, wq_bf, tokens_per_batch):
    t, d = x1.shape
    dk = wq_bf.shape[1] // PEER_HEADS
    tm = min(512, tokens_per_batch)
    tpb = tokens_per_batch // tm
    return pl.pallas_call(
        _peer_query_kernel,
        grid=(t // tm,),
        in_specs=[pl.BlockSpec((tm, d), lambda i: (i, 0)),
                  pl.BlockSpec((1, d), lambda i: (0, 0)),
                  pl.BlockSpec((1, 1, d), lambda i: (i // tpb, 0, 0)),
                  pl.BlockSpec((1, 1, d), lambda i: (i // tpb, 0, 0)),
                  pl.BlockSpec(wq_bf.shape, lambda i: (0, 0))],
        out_specs=[pl.BlockSpec((d, tm), lambda i: (0, i)),
                   pl.BlockSpec((PEER_HEADS, tm, dk), lambda i: (0, i, 0))],
        out_shape=[jax.ShapeDtypeStruct((d, t), BF16),
                   jax.ShapeDtypeStruct((PEER_HEADS, t, dk), F32)],
        compiler_params=_cparams(("parallel",)),
        name="peer_query",
    )(x1, norm_w, sc, sh, wq_bf)


def _gelu_exact(x):
    return 0.5 * x * (1.0 + lax.erf(x * 0.7071067811865476))


def _peer_route_act_kernel(ht_ref, q_ref, keys_ref, u_ref,
                           act_ref, pa_ref, len_ref, qb_ref, rnk_ref, code_scr):
    hd = pl.program_id(1)
    tm = ht_ref.shape[1]
    half = keys_ref.shape[-1]
    sub = min(ROUTE_SUB_TOKENS, tm)
    blocked = (PEER_NKEYS // PEER_SLABS, PEER_SLABS, sub)

    @pl.when(hd == 0)
    def _():
        code_scr[...] = jnp.zeros(code_scr.shape, F32)

    act_ref[...] = _dot(u_ref[...], ht_ref[...]).astype(BF16)

    def route(h, t0, tie_safe):
        q = q_ref[h, t0:t0 + sub, :]
        sa = _dot_nt(keys_ref[h, 0], q[:, :half], HIGHEST)
        sb = _dot_nt(keys_ref[h, 1], q[:, half:], HIGHEST)
        pa, length, qbt, rank_b, clean = _route_head(sa, sb, tie_safe)
        pa_ref[h, :, :, t0:t0 + sub] = pa.reshape(blocked)
        len_ref[h, :, :, t0:t0 + sub] = length.reshape(blocked)
        qb_ref[h, :, t0:t0 + sub] = qbt.astype(BF16)
        rnk_ref[h, :, t0:t0 + sub] = rank_b.astype(BF16)
        return clean

    dirty = jnp.zeros((1, 1), F32)
    for t0 in range(0, tm, sub):
        dirty = jnp.maximum(dirty, 1.0 - jnp.min(route(hd, t0, False), axis=1, keepdims=True))
    code_scr[0:1, 0:1] += dirty * lax.shift_left(1, hd).astype(F32)

    @pl.when(hd == pl.num_programs(1) - 1)
    def _():
        tie_bits = code_scr[0, 0].astype(jnp.int32)

        def redo(h, carry):
            @pl.when(((tie_bits >> h) & 1) == 1)
            def _():
                for t0 in range(0, tm, sub):
                    route(h, t0, True)
            return carry

        lax.fori_loop(0, PEER_HEADS, redo, 0)


def _peer_route_act(ht_bf, q_heads, sub_keys, u_bf, tokens_per_batch):
    d, t = ht_bf.shape
    ne = u_bf.shape[0]
    bu = ne // PEER_HEADS
    dk = q_heads.shape[-1]
    tm = min(512, tokens_per_batch)
    n_blocks = PEER_NKEYS // PEER_SLABS
    dense = jax.ShapeDtypeStruct((PEER_HEADS, PEER_NKEYS, t), BF16)
    dense_spec = pl.BlockSpec((PEER_HEADS, PEER_NKEYS, tm), lambda i, j: (0, 0, i))
    blocked = jax.ShapeDtypeStruct((PEER_HEADS, n_blocks, PEER_SLABS, t), F32)
    blocked_spec = pl.BlockSpec((PEER_HEADS, n_blocks, PEER_SLABS, tm), lambda i, j: (0, 0, 0, i))
    return pl.pallas_call(
        _peer_route_act_kernel,
        grid=(t // tm, PEER_HEADS),
        in_specs=[pl.BlockSpec((d, tm), lambda i, j: (0, i)),
                  pl.BlockSpec((PEER_HEADS, tm, dk), lambda i, j: (0, i, 0)),
                  pl.BlockSpec(sub_keys.shape, lambda i, j: (0, 0, 0, 0)),
                  pl.BlockSpec((bu, d), lambda i, j: (j, 0))],
        out_specs=[pl.BlockSpec((bu, tm), lambda i, j: (j, i)),
                   blocked_spec, blocked_spec, dense_spec, dense_spec],
        out_shape=[jax.ShapeDtypeStruct((ne, t), BF16), blocked, blocked, dense, dense],
        scratch_shapes=[pltpu.VMEM((8, LANES), F32)],
        compiler_params=_cparams(("parallel", "arbitrary")),
        name="peer_route_act",
    )(ht_bf, q_heads, sub_keys, u_bf)


def _peer_expert_kernel(act_ref, vt_ref, pa_ref, len_ref, qb_ref, rnk_ref, x_ref, g2_ref, o_ref, acc_scr):
    j = pl.program_id(1)
    be, tm = act_ref.shape

    @pl.when(j == 0)
    def _():
        acc_scr[...] = jnp.zeros(acc_scr.shape, F32)

    n_slabs = be // PEER_NKEYS
    rows = BF16_ROWS
    n_groups = PEER_NKEYS // rows

    def row_tile(ref, hd, s):
        return jnp.broadcast_to(ref[hd, 0, s:s + 1, :], (rows, tm)).astype(BF16)

    zero = jnp.zeros((rows, tm), BF16)
    parts = [[None] * n_groups for _ in range(n_slabs)]
    slab_group = 8
    for s0 in range(0, n_slabs, slab_group):
        ss = range(s0, min(s0 + slab_group, n_slabs))
        ln = {(hd, s): row_tile(len_ref, hd, s) for hd in range(PEER_HEADS) for s in ss}
        pa = {(hd, s): row_tile(pa_ref, hd, s) for hd in range(PEER_HEADS) for s in ss}
        for g in range(n_groups):
            sl = slice(g * rows, (g + 1) * rows)
            acc = {}
            for hd in range(PEER_HEADS):
                rk = rnk_ref[hd, sl, :]
                qv = qb_ref[hd, sl, :]
                for s in ss:
                    term = jnp.where(rk < ln[hd, s], qv, zero) * pa[hd, s]
                    acc[s] = term if s not in acc else acc[s] + term
            for s in ss:
                lo = s * PEER_NKEYS + g * rows
                parts[s][g] = acc[s] * _gelu_exact(act_ref[lo:lo + rows, :].astype(F32)).astype(BF16)
    p = jnp.concatenate([parts[s][g] for s in range(n_slabs) for g in range(n_groups)], axis=0)
    acc_scr[...] += _dot(vt_ref[0], p)

    @pl.when(j == pl.num_programs(1) - 1)
    def _():
        o_ref[...] = x_ref[...] + g2_ref[0] * acc_scr[...].T


def _peer_expert(act, vt_bf, pa, ln, qb, rnk, x1, g2, tokens_per_batch):
    t, d = x1.shape
    ne = act.shape[0]
    tm = min(512, tokens_per_batch)
    tpb = tokens_per_batch // tm
    be = PEER_EXPERT_BLOCK
    dense_spec = pl.BlockSpec((PEER_HEADS, PEER_NKEYS, tm), lambda i, j: (0, 0, i))
    blocked_spec = pl.BlockSpec((PEER_HEADS, 1, PEER_SLABS, tm), lambda i, j: (0, j, 0, i))
    return pl.pallas_call(
        _peer_expert_kernel,
        grid=(t // tm, ne // be),
        in_specs=[pl.BlockSpec((be, tm), lambda i, j: (j, i)),
                  pl.BlockSpec((1, d, be), lambda i, j: (j, 0, 0)),
                  blocked_spec, blocked_spec, dense_spec, dense_spec,
                  pl.BlockSpec((tm, d), lambda i, j: (i, 0)),
                  pl.BlockSpec((1, 1, d), lambda i, j: (i // tpb, 0, 0))],
        out_specs=pl.BlockSpec((tm, d), lambda i, j: (i, 0)),
        out_shape=jax.ShapeDtypeStruct((t, d), F32),
        scratch_shapes=[pltpu.VMEM((d, tm), F32)],
        compiler_params=_cparams(("parallel", "arbitrary")),
        name="peer_expert",
    )(act, vt_bf, pa, ln, qb, rnk, x1, g2)


def _pad_cols(a, n):
    return jnp.pad(a, ((0, 0), (0, n - a.shape[1])))


def _pad_rows(a, n):
    return jnp.pad(a, ((0, n - a.shape[0]), (0, 0)))


def _layer(x, c, w_ada, b_ada, norm_mix_w, w_in, fox_q_norm_w, fox_k_norm_w, fox_f_bias,
           rwkv_mu, rwkv_w0, rwkv_w_up, rwkv_a0, rwkv_a_up, rwkv_g_up, rwkv_k_k, rwkv_k_a,
           rwkv_r_k, rwkv_ln_w, rwkv_ln_b, w_out, norm_ffn_w, peer_w_query, peer_sub_keys,
           peer_u, peer_v):
    B, S, D = x.shape
    T = B * S
    fw = FOX_HEADS * FOX_HEAD_DIM
    rw = RWKV_HEADS * RWKV_HEAD_DIM
    w_lora = rwkv_w_up.shape[0]
    a_lora = rwkv_a_up.shape[0]
    g_lora = rwkv_g_up.shape[0]
    assert w_lora <= LANES and a_lora + FOX_HEADS <= LANES and g_lora == 256

    c_pad = _pad_rows(c, 8)
    mod = _ada_mod(c_pad, w_ada, b_ada)[:B]
    sh1, sc1, g1, sh2, sc2, g2 = [m.reshape(B, 1, D) for m in jnp.split(mod, 6, axis=-1)]

    fox_cols = 4 * fw + FOX_HEADS
    wi_fox, wi_rw = w_in[:, :fox_cols], w_in[:, fox_cols:]
    mu = rwkv_mu.reshape(1, -1)
    seg = lambda a, lo, n: a[:, lo:lo + n]
    w_perm = jnp.concatenate([
        seg(wi_fox, 0, 4 * fw),
        seg(wi_rw, 0, 3 * rw),
        seg(wi_rw, 3 * rw + w_lora + a_lora, g_lora),
        _pad_cols(seg(wi_rw, 3 * rw, w_lora), LANES),
        _pad_cols(jnp.concatenate([seg(wi_rw, 3 * rw + w_lora, a_lora), seg(wi_fox, 4 * fw, FOX_HEADS)], 1), LANES),
    ], axis=1).astype(BF16)
    cols = {"gate": 3 * fw, "rr": 4 * fw, "rk": 4 * fw + rw, "rv": 4 * fw + 2 * rw,
            "gd": 4 * fw + 3 * rw, "lo": 4 * fw + 3 * rw + g_lora}
    f_lane = a_lora
    f_block = (cols["lo"] + LANES) // LANES
    mus = [seg(mu, 0, rw), seg(mu, rw, rw), seg(mu, 2 * rw, rw),
           seg(mu, 3 * rw + w_lora + a_lora, g_lora),
           jnp.concatenate([_pad_cols(seg(mu, 3 * rw, w_lora), LANES),
                            _pad_cols(seg(mu, 3 * rw + w_lora, a_lora), LANES)], 1)]
    scale = FOX_HEAD_DIM ** -0.5 * LOG2_E
    head_w = _pad_cols(jnp.concatenate([jnp.tile(fox_q_norm_w * scale, FOX_HEADS),
                                        jnp.tile(fox_k_norm_w, FOX_HEADS)]).reshape(1, -1), w_perm.shape[1])

    x2 = x.reshape(T, D)
    proj = _in_proj(_norm_mod(x2, norm_mix_w.reshape(1, D), sc1, sh1, S), w_perm, head_w, 2 * fw)

    f_bias_row = jnp.zeros((1, LANES), F32).at[0, f_lane:f_lane + FOX_HEADS].set(fox_f_bias)
    cum = _fox_cum(proj, f_bias_row, B, S, f_block)
    cum_rows = cum.reshape(B, S, LANES)[:, :, f_lane:f_lane + FOX_HEADS].transpose(0, 2, 1).reshape(B, FOX_HEADS, 1, S)
    o_fox = _fox_attn(proj, cum_rows, B, S)

    row = lambda a: a.reshape(1, -1)
    r, lw, k, v, a_vec, b_vec, g, bonus = _rwkv_prep(
        proj, cols, mus, row(rwkv_w0), _pad_rows(rwkv_w_up, LANES), row(rwkv_a0), _pad_rows(rwkv_a_up, LANES),
        rwkv_g_up, row(rwkv_k_k), row(rwkv_k_a), row(rwkv_r_k), S)
    y = _rwkv_scan(r, lw, k, v, a_vec, b_vec, B, S)

    x1 = _mix_out(o_fox, proj, cols["gate"], y, bonus, g, x2, g1, row(rwkv_ln_w), row(rwkv_ln_b),
                  w_out.astype(BF16), S)

    ht, q_heads = _peer_query(x1, norm_ffn_w.reshape(1, D), sc2, sh2, peer_w_query.astype(BF16), S)
    act, pa, ln, qb, rnk = _peer_route_act(ht, q_heads, peer_sub_keys, peer_u.astype(BF16), S)
    vt = peer_v.astype(BF16).reshape(-1, PEER_EXPERT_BLOCK, D).transpose(0, 2, 1)
    out = _peer_expert(act, vt, pa, ln, qb, rnk, x1, g2, S)
    return out.reshape(B, S, D)


def kernel(x, c, w_ada, b_ada, norm_mix_w, w_in, fox_q_norm_w, fox_k_norm_w, fox_f_bias, rwkv_mu, rwkv_w0,
           rwkv_w_up, rwkv_a0, rwkv_a_up, rwkv_g_up, rwkv_k_k, rwkv_k_a, rwkv_r_k, rwkv_ln_w, rwkv_ln_b,
           w_out, norm_ffn_w, peer_w_query, peer_sub_keys, peer_u, peer_v):
    params = (w_ada, b_ada, norm_mix_w, w_in, fox_q_norm_w, fox_k_norm_w, fox_f_bias, rwkv_mu, rwkv_w0,
              rwkv_w_up, rwkv_a0, rwkv_a_up, rwkv_g_up, rwkv_k_k, rwkv_k_a, rwkv_r_k, rwkv_ln_w, rwkv_ln_b,
              w_out, norm_ffn_w, peer_w_query, peer_sub_keys, peer_u, peer_v)
    for l in range(w_ada.shape[0]):
        x = _layer(x, c, *[p[l] for p in params])
    return x
```

```python
import functools

import jax
import jax.numpy as jnp
from jax import lax
from jax.experimental import pallas as pl
from jax.experimental.pallas import tpu as pltpu

F32 = jnp.float32
BF16 = jnp.bfloat16
HIGHEST = lax.Precision.HIGHEST

LANES = 128
BF16_ROWS = 16
NORM_EPS = 1e-6
GN_EPS = 64e-5
CHUNK = 64
FOX_HEADS = 8
FOX_HEAD_DIM = 128
RWKV_HEADS = 16
RWKV_HEAD_DIM = 64
PEER_HEADS = 8
PEER_NKEYS = 128
PEER_TOPK = 16
PEER_EXPERT_BLOCK = 2048
PEER_SLABS = PEER_EXPERT_BLOCK // PEER_NKEYS
ROUTE_SUB_TOKENS = 256
NEG_BIG = -1e30
LOG2_E = 1.4426950408889634
VMEM_LIMIT = 56 * 1024 * 1024


def _cparams(sem):
    return pltpu.CompilerParams(dimension_semantics=sem, vmem_limit_bytes=VMEM_LIMIT)


def _dot(a, b, precision=None):
    return jnp.dot(a, b, preferred_element_type=F32, precision=precision)


def _dot_nt(a, b, precision=None):
    return lax.dot_general(a, b, (((1,), (1,)), ((), ())), preferred_element_type=F32,
                           precision=precision)


def _split_bf16(x):
    hi = x.astype(BF16)
    return hi, (x - hi.astype(F32)).astype(BF16)


def _dot_bf16x3(a, b):
    ah, al = _split_bf16(a)
    bh, bl = _split_bf16(b)
    return _dot(ah, bh) + (_dot(ah, bl) + _dot(al, bh))


def _dot_tn(a, b, precision=None):
    return lax.dot_general(a, b, (((0,), (0,)), ((), ())), preferred_element_type=F32,
                           precision=precision)


def _ada_kernel(c_ref, w_ref, b_ref, o_ref):
    c = c_ref[...]
    s = c * jax.nn.sigmoid(c)
    o_ref[...] = _dot(s, w_ref[...], HIGHEST) + b_ref[...]


def _ada_mod(c_pad, w_ada, b_ada):
    rows, d = c_pad.shape
    n = w_ada.shape[1]
    bn = 1024
    return pl.pallas_call(
        _ada_kernel,
        grid=(n // bn,),
        in_specs=[pl.BlockSpec((rows, d), lambda j: (0, 0)),
                  pl.BlockSpec((d, bn), lambda j: (0, j)),
                  pl.BlockSpec((1, bn), lambda j: (0, j))],
        out_specs=pl.BlockSpec((rows, bn), lambda j: (0, j)),
        out_shape=jax.ShapeDtypeStruct((rows, n), F32),
        compiler_params=_cparams(("arbitrary",)),
        name="ada_mod",
    )(c_pad, w_ada, b_ada.reshape(1, n))


def _modulated_norm(x, nw, sc, sh):
    y = x * lax.rsqrt(jnp.mean(x * x, axis=-1, keepdims=True) + NORM_EPS)
    return y * nw * (1.0 + sc) + sh


def _norm_mod_kernel(x_ref, nw_ref, sc_ref, sh_ref, o_ref):
    o_ref[...] = _modulated_norm(x_ref[...], nw_ref[...], sc_ref[0], sh_ref[0]).astype(BF16)


def _norm_mod(x2, norm_w, sc, sh, tokens_per_batch):
    t, d = x2.shape
    tm = min(512, tokens_per_batch)
    tpb = tokens_per_batch // tm
    return pl.pallas_call(
        _norm_mod_kernel,
        grid=(t // tm,),
        in_specs=[pl.BlockSpec((tm, d), lambda i: (i, 0)),
                  pl.BlockSpec((1, d), lambda i: (0, 0)),
                  pl.BlockSpec((1, 1, d), lambda i: (i // tpb, 0, 0)),
                  pl.BlockSpec((1, 1, d), lambda i: (i // tpb, 0, 0))],
        out_specs=pl.BlockSpec((tm, d), lambda i: (i, 0)),
        out_shape=jax.ShapeDtypeStruct((t, d), BF16),
        compiler_params=_cparams(("parallel",)),
        name="norm_mod",
    )(x2, norm_w, sc, sh)


def _in_proj_kernel(h_ref, w_ref, hw_ref, o_ref, *, n_qk_blocks):
    j = pl.program_id(1)
    acc = _dot(h_ref[...], w_ref[...])

    @pl.when(j < n_qk_blocks)
    def _():
        for hh in range(acc.shape[1] // FOX_HEAD_DIM):
            sl = slice(hh * FOX_HEAD_DIM, (hh + 1) * FOX_HEAD_DIM)
            a = acc[:, sl]
            rs = lax.rsqrt(jnp.mean(a * a, axis=-1, keepdims=True) + NORM_EPS)
            o_ref[:, sl] = a * rs * hw_ref[:, sl]

    @pl.when(j >= n_qk_blocks)
    def _():
        o_ref[...] = acc


def _in_proj(h_bf, w_bf, head_w, n_qk_cols):
    t, d = h_bf.shape
    n = w_bf.shape[1]
    tm = min(2048, t)
    bn = 512
    kern = functools.partial(_in_proj_kernel, n_qk_blocks=n_qk_cols // bn)
    return pl.pallas_call(
        kern,
        grid=(t // tm, n // bn),
        in_specs=[pl.BlockSpec((tm, d), lambda i, j: (i, 0)),
                  pl.BlockSpec((d, bn), lambda i, j: (0, j)),
                  pl.BlockSpec((1, bn), lambda i, j: (0, j))],
        out_specs=pl.BlockSpec((tm, bn), lambda i, j: (i, j)),
        out_shape=jax.ShapeDtypeStruct((t, n), F32),
        compiler_params=_cparams(("parallel", "arbitrary")),
        name="in_proj",
    )(h_bf, w_bf, head_w)


def _log_sigmoid(x):
    return jnp.minimum(x, 0.0) - jnp.log(1.0 + jnp.exp(-jnp.abs(x)))


def _fox_cum_kernel(f_ref, b_ref, o_ref, *, blk):
    s = f_ref.shape[0]
    row = lax.broadcasted_iota(jnp.int32, (blk, blk), 0)
    col = lax.broadcasted_iota(jnp.int32, (blk, blk), 1)
    tri = (row >= col).astype(F32)
    carry = jnp.zeros((1, f_ref.shape[1]), F32)
    for i in range(s // blk):
        lf = _log_sigmoid(f_ref[i * blk:(i + 1) * blk, :] + b_ref[...])
        cs = _dot(tri, lf, HIGHEST) + carry
        o_ref[i * blk:(i + 1) * blk, :] = cs * LOG2_E
        carry = cs[blk - 1:blk, :]


def _fox_cum(proj, f_bias_row, batch, seq, col_block):
    blk = min(256, seq)
    return pl.pallas_call(
        functools.partial(_fox_cum_kernel, blk=blk),
        grid=(batch,),
        in_specs=[pl.BlockSpec((seq, LANES), lambda b: (b, col_block)),
                  pl.BlockSpec((1, LANES), lambda b: (0, 0))],
        out_specs=pl.BlockSpec((seq, LANES), lambda b: (b, 0)),
        out_shape=jax.ShapeDtypeStruct((batch * seq, LANES), F32),
        compiler_params=_cparams(("parallel",)),
        name="fox_cum",
    )(proj, f_bias_row)


def _fox_attn_kernel(qt_ref, kt_ref, q_ref, k_ref, v_ref, c_ref, o_ref, m_scr, l_scr, acc_scr):
    t = pl.program_id(2)
    qi = qt_ref[t]
    ki = kt_ref[t]
    hd = FOX_HEAD_DIM
    n_heads = q_ref.shape[1] // hd

    @pl.when(ki == 0)
    def _():
        m_scr[...] = jnp.full(m_scr.shape, NEG_BIG, F32)
        l_scr[...] = jnp.zeros(l_scr.shape, F32)
        acc_scr[...] = jnp.zeros(acc_scr.shape, F32)

    def step(masked):
        hs = range(n_heads)
        sl = [slice(h * hd, (h + 1) * hd) for h in hs]
        s = [_dot_nt(q_ref[:, sl[h]].astype(BF16), k_ref[:, sl[h]].astype(BF16)) - c_ref[0, h] for h in hs]
        if masked:
            row = lax.broadcasted_iota(jnp.int32, s[0].shape, 0)
            col = lax.broadcasted_iota(jnp.int32, s[0].shape, 1)
            s = [jnp.where(row >= col, x, NEG_BIG) for x in s]
        m_prev = [m_scr[h] for h in hs]
        m_new = [jnp.maximum(m_prev[h], jnp.max(s[h], axis=-1, keepdims=True)) for h in hs]
        n_rep = s[0].shape[1] // hd
        p = [jnp.exp2(s[h] - jnp.concatenate([m_new[h]] * n_rep, axis=1)) for h in hs]
        alpha = [jnp.exp2(m_prev[h] - m_new[h]) for h in hs]
        ones = jnp.ones((k_ref.shape[0], hd), BF16)
        pv = [_dot(p[h].astype(BF16), jnp.concatenate([v_ref[:, sl[h]].astype(BF16), ones], axis=1))
              for h in hs]
        for h in hs:
            l_scr[h] = alpha[h] * l_scr[h] + pv[h][:, hd:]
            acc_scr[:, sl[h]] = alpha[h] * acc_scr[:, sl[h]] + pv[h][:, :hd]
            m_scr[h] = m_new[h]

    @pl.when(ki < qi)
    def _():
        step(False)

    @pl.when(ki == qi)
    def _():
        step(True)
        for h in range(n_heads):
            sl = slice(h * hd, (h + 1) * hd)
            o_ref[:, sl] = acc_scr[:, sl] / l_scr[h]


def _fox_attn(proj, cum_rows, batch, seq):
    tq = min(512, seq)
    nq = seq // tq
    hps = 4
    w = hps * FOX_HEAD_DIM
    kcol = FOX_HEADS // hps
    tri = [(q, k) for q in range(nq) for k in range(q + 1)]
    qt = jnp.asarray([q for q, _ in tri], jnp.int32)
    kt = jnp.asarray([k for _, k in tri], jnp.int32)
    grid_spec = pltpu.PrefetchScalarGridSpec(
        num_scalar_prefetch=2,
        grid=(batch, kcol, len(tri)),
        in_specs=[
            pl.BlockSpec((tq, w), lambda b, h, t, qt, kt: (b * nq + qt[t], h)),
            pl.BlockSpec((tq, w), lambda b, h, t, qt, kt: (b * nq + kt[t], kcol + h)),
            pl.BlockSpec((tq, w), lambda b, h, t, qt, kt: (b * nq + kt[t], 2 * kcol + h)),
            pl.BlockSpec((1, hps, 1, tq), lambda b, h, t, qt, kt: (b, h, 0, kt[t])),
        ],
        out_specs=pl.BlockSpec((tq, w), lambda b, h, t, qt, kt: (b * nq + qt[t], h)),
        scratch_shapes=[pltpu.VMEM((hps, tq, FOX_HEAD_DIM), F32), pltpu.VMEM((hps, tq, FOX_HEAD_DIM), F32),
                        pltpu.VMEM((tq, w), F32)],
    )
    return pl.pallas_call(
        _fox_attn_kernel,
        grid_spec=grid_spec,
        out_shape=jax.ShapeDtypeStruct((batch * seq, FOX_HEADS * FOX_HEAD_DIM), F32),
        compiler_params=_cparams(("parallel", "parallel", "arbitrary")),
        name="fox_attn",
    )(qt, kt, proj, proj, proj, cum_rows)


def _head_sum(x, bd):
    hi, lo = _split_bf16(x)
    parts = [_dot(hi[:, j * LANES:(j + 1) * LANES], bd) + _dot(lo[:, j * LANES:(j + 1) * LANES], bd)
             for j in range(x.shape[1] // LANES)]
    return jnp.concatenate(parts, axis=1)


def _head_block_diag():
    r = lax.broadcasted_iota(jnp.int32, (LANES, LANES), 0) // RWKV_HEAD_DIM
    c = lax.broadcasted_iota(jnp.int32, (LANES, LANES), 1) // RWKV_HEAD_DIM
    return (r == c).astype(BF16)


def _rwkv_prep_kernel(r_ref, k_ref, v_ref, gd_ref, lo_ref, pr_ref, pk_ref, pv_ref, pgd_ref, plo_ref,
                      mu_r, mu_k, mu_v, mu_gd, mu_lo, w0_ref, wup_ref, a0_ref, aup_ref, gup_ref,
                      kk_ref, ka_ref, rk_ref,
                      or_ref, olw_ref, ok_ref, ov_ref, oa_ref, ob_ref, og_ref, obonus_ref, *, tiles_per_batch):
    i = pl.program_id(0)
    first = (i % tiles_per_batch) == 0

    def shifted(cur_ref, prv_ref, mu_ref):
        cur = cur_ref[...]
        last = jnp.where(first, 0.0, prv_ref[7:8, :])
        row = lax.broadcasted_iota(jnp.int32, cur.shape, 0)
        prev = jnp.where(row == 0, last, pltpu.roll(cur, 1, 0))
        return cur + (prev - cur) * mu_ref[...]

    r = shifted(r_ref, pr_ref, mu_r)
    k = shifted(k_ref, pk_ref, mu_k)
    v = shifted(v_ref, pv_ref, mu_v)
    gd = shifted(gd_ref, pgd_ref, mu_gd)
    lo = shifted(lo_ref, plo_ref, mu_lo)
    wd = lo[:, :LANES]
    ad = lo[:, LANES:]

    w_pre = w0_ref[...] + _dot_bf16x3(jnp.tanh(wd), wup_ref[...])
    w_raw = _log_sigmoid(w_pre) - 0.5
    log_decay = -jnp.exp(w_raw)
    a = jax.nn.sigmoid(a0_ref[...] + _dot_bf16x3(ad, aup_ref[...]))
    g = _dot_bf16x3(jax.nn.sigmoid(gd), gup_ref[...])

    bd = _head_block_diag()
    kk = k * kk_ref[...]
    nrm = jnp.maximum(jnp.sqrt(_head_sum(kk * kk, bd)), 1e-12)
    kk = kk / nrm
    k_mod = k * (1.0 + (a - 1.0) * ka_ref[...])
    bonus = _head_sum(r * k_mod * rk_ref[...], bd) * v

    or_ref[...] = r.astype(BF16)
    olw_ref[...] = log_decay
    ok_ref[...] = k_mod.astype(BF16)
    ov_ref[...] = v.astype(BF16)
    oa_ref[...] = (-kk).astype(BF16)
    ob_ref[...] = (kk * a).astype(BF16)
    og_ref[...] = g.astype(BF16)
    obonus_ref[...] = bonus.astype(BF16)


def _rwkv_prep(proj, cols, mus, w0, w_up, a0, a_up, g_up, k_k, k_a, r_k, tokens_per_batch):
    t = proj.shape[0]
    w = RWKV_HEADS * RWKV_HEAD_DIM
    tm = min(512, tokens_per_batch)
    tpb = tokens_per_batch // tm
    widths = [w, w, w, 256, 256]
    offs = [cols["rr"], cols["rk"], cols["rv"], cols["gd"], cols["lo"]]
    cur_specs = [pl.BlockSpec((tm, wd), functools.partial(lambda i, cb: (i, cb), cb=o // wd))
                 for wd, o in zip(widths, offs)]
    prv_specs = [pl.BlockSpec((8, wd), functools.partial(
        lambda i, cb: (jnp.maximum(i * (tm // 8) - 1, 0), cb), cb=o // wd))
        for wd, o in zip(widths, offs)]
    full = lambda a: pl.BlockSpec(a.shape, lambda i: (0,) * a.ndim)
    params = list(mus) + [w0, w_up, a0, a_up, g_up, k_k, k_a, r_k]
    out_spec = pl.BlockSpec((tm, w), lambda i: (i, 0))
    return pl.pallas_call(
        functools.partial(_rwkv_prep_kernel, tiles_per_batch=tpb),
        grid=(t // tm,),
        in_specs=cur_specs + prv_specs + [full(p) for p in params],
        out_specs=[out_spec] * 8,
        out_shape=[jax.ShapeDtypeStruct((t, w), dt) for dt in (BF16, F32, BF16, BF16, BF16, BF16, BF16, BF16)],
        compiler_params=_cparams(("parallel",)),
        name="rwkv_prep",
    )(*([proj] * 10), *params)


def _rwkv_scan_kernel(r_ref, lw_ref, k_ref, v_ref, a_ref, b_ref, o_ref, h_scr):
    c = pl.program_id(1)

    @pl.when(c == 0)
    def _():
        h_scr[...] = jnp.zeros(h_scr.shape, F32)

    C = r_ref.shape[0]
    n_pairs = r_ref.shape[1] // LANES
    P = HIGHEST

    row = lax.broadcasted_iota(jnp.int32, (C, C), 0)
    col = lax.broadcasted_iota(jnp.int32, (C, C), 1)
    tri = (row >= col).astype(F32)
    lw = lw_ref[...]
    cw = _dot(tri, lw, P)
    cw_end = cw[C - 1:C, :]
    e_pos = jnp.exp(cw)
    e_prev = jnp.exp(cw - lw)
    e_neg = jnp.exp(-cw)
    e_end = jnp.exp(cw_end - cw)
    w_end = jnp.exp(cw_end)

    a = a_ref[...].astype(F32)
    b = b_ref[...].astype(F32)
    k = k_ref[...].astype(F32)
    r = r_ref[...].astype(F32)
    at = a * e_prev
    bt = b * e_neg
    kt = k * e_neg
    rt = r * e_pos
    bh = b * e_end
    kh = k * e_end
    v = v_ref[...].astype(F32)

    lane = lax.broadcasted_iota(jnp.int32, (C, LANES), 1)
    head0 = lane < RWKV_HEAD_DIM
    r2 = lax.broadcasted_iota(jnp.int32, (2 * C, 2 * C), 0)
    c2 = lax.broadcasted_iota(jnp.int32, (2 * C, 2 * C), 1)
    same = (r2 // C) == (c2 // C)
    strict = same & (r2 > c2)
    incl = same & (r2 >= c2)
    eye = (r2 == c2).astype(F32)

    def two(x, p):
        xp = x[:, p * LANES:(p + 1) * LANES]
        return jnp.concatenate([jnp.where(head0, xp, 0.0), jnp.where(head0, 0.0, xp)], axis=0)

    def mm(x, y):
        return _dot(x.astype(BF16), y.astype(BF16))

    def split(x):
        hi = x.astype(BF16)
        return hi, (x - hi.astype(F32)).astype(BF16)

    def mm3(x, y):
        xh, xl = split(x)
        yh, yl = split(y)
        return _dot(xh, yh) + (_dot(xh, yl) + _dot(xl, yh))

    G = 2 * C
    pairs = range(n_pairs)
    at2 = [two(at, p) for p in pairs]
    rt2 = [two(rt, p) for p in pairs]
    v2 = [two(v, p).astype(BF16) for p in pairs]
    gram = [_dot_nt(jnp.concatenate([at2[p], rt2[p]], axis=0).astype(BF16),
                    jnp.concatenate([two(bt, p), two(kt, p)], axis=0).astype(BF16)) for p in pairs]
    a_ab = [jnp.where(strict, gram[p][:G, :G], 0.0) for p in pairs]
    nmat = list(a_ab)
    pw = list(a_ab)
    for _ in range(max(1, (C - 1).bit_length() - 1)):
        pw = [mm3(pw[p], pw[p]) for p in pairs]
        nmat = [nmat[p] + pw[p] + mm3(nmat[p], pw[p]) for p in pairs]
    akv = [mm(jnp.where(strict, gram[p][:G, G:], 0.0), v2[p]) for p in pairs]
    rhs = [jnp.concatenate([at2[p], akv[p]], axis=1) for p in pairs]
    pq = [(rhs[p] + mm(nmat[p], rhs[p])).astype(BF16) for p in pairs]
    ry = [mm(jnp.where(incl, gram[p][G:, :G], 0.0), pq[p]) for p in pairs]
    mv = [mm(jnp.where(incl, gram[p][G:, G:], 0.0), v2[p]) for p in pairs]
    ge = [_dot_tn(two(bh, p).astype(BF16), pq[p]) for p in pairs]
    kv = [_dot_tn(two(kh, p).astype(BF16), v2[p]) for p in pairs]
    for p in pairs:
        rr = rt2[p] + ry[p][:, :LANES]
        gm = eye * w_end[:, p * LANES:(p + 1) * LANES] + ge[p][:, :LANES]
        yh = mm(jnp.concatenate([rr, gm], axis=0), h_scr[p])
        y2 = yh[:G] + ry[p][:, LANES:] + mv[p]
        h_scr[p] = yh[G:] + ge[p][:, LANES:] + kv[p]
        o_ref[:, p * LANES:(p + 1) * LANES] = y2[:C, :] + y2[C:, :]


def _rwkv_scan(r, lw, k, v, a, b, batch, seq):
    w = r.shape[1]
    nc = seq // CHUNK
    spec = pl.BlockSpec((CHUNK, w), lambda bi, ci: (bi * nc + ci, 0))
    return pl.pallas_call(
        _rwkv_scan_kernel,
        grid=(batch, nc),
        in_specs=[spec] * 6,
        out_specs=spec,
        out_shape=jax.ShapeDtypeStruct((batch * seq, w), F32),
        scratch_shapes=[pltpu.VMEM((w // LANES, LANES, LANES), F32)],
        compiler_params=_cparams(("parallel", "arbitrary")),
        name="rwkv_scan",
    )(r, lw, k, v, a, b)


def _mix_out_kernel(o_ref, gate_ref, y_ref, bonus_ref, g_ref, x_ref, g1_ref, lnw_ref, lnb_ref, w_ref, out_ref):
    fox = o_ref[...] * jax.nn.sigmoid(gate_ref[...])
    bd = _head_block_diag()
    y = y_ref[...]
    inv_n = 1.0 / RWKV_HEAD_DIM
    mean = _head_sum(y, bd) * inv_n
    d = y - mean
    var = _head_sum(d * d, bd) * inv_n
    yn = d * lax.rsqrt(var + GN_EPS) * lnw_ref[...] + lnb_ref[...]
    rw = (yn + bonus_ref[...].astype(F32)) * g_ref[...].astype(F32)
    wf = fox.shape[1]
    mix = _dot(fox.astype(BF16), w_ref[:wf, :]) + _dot(rw.astype(BF16), w_ref[wf:, :])
    out_ref[...] = x_ref[...] + g1_ref[0] * mix


def _mix_out(o_fox, proj, gate_col, y, bonus, g, x2, g1, ln_w, ln_b, w_out_bf, tokens_per_batch):
    t, d = x2.shape
    wf = o_fox.shape[1]
    wr = y.shape[1]
    tm = min(512, tokens_per_batch)
    tpb = tokens_per_batch // tm
    return pl.pallas_call(
        _mix_out_kernel,
        grid=(t // tm,),
        in_specs=[pl.BlockSpec((tm, wf), lambda i: (i, 0)),
                  pl.BlockSpec((tm, wf), lambda i: (i, gate_col // wf)),
                  pl.BlockSpec((tm, wr), lambda i: (i, 0)),
                  pl.BlockSpec((tm, wr), lambda i: (i, 0)),
                  pl.BlockSpec((tm, wr), lambda i: (i, 0)),
                  pl.BlockSpec((tm, d), lambda i: (i, 0)),
                  pl.BlockSpec((1, 1, d), lambda i: (i // tpb, 0, 0)),
                  pl.BlockSpec((1, wr), lambda i: (0, 0)),
                  pl.BlockSpec((1, wr), lambda i: (0, 0)),
                  pl.BlockSpec(w_out_bf.shape, lambda i: (0, 0))],
        out_specs=pl.BlockSpec((tm, d), lambda i: (i, 0)),
        out_shape=jax.ShapeDtypeStruct((t, d), F32),
        compiler_params=_cparams(("parallel",)),
        name="mix_out",
    )(o_fox, proj, y, bonus, g, x2, g1, ln_w, ln_b, w_out_bf)


def _top_k_mask_rows(s, k, tie_safe, want_rank=True):
    n = s.shape[0]
    s0 = s
    row = lax.broadcasted_iota(jnp.int32, s.shape, 0).astype(F32)
    rank = jnp.full(s.shape, float(k), F32) if want_rank else None
    vals = []
    for r in range(k):
        m = jnp.max(s, axis=0, keepdims=True)
        hit = s == m
        if tie_safe:
            pos = jnp.min(jnp.where(hit, row, float(n)), axis=0, keepdims=True)
            hit = row == pos
        vals.append(m)
        if want_rank:
            rank = jnp.where(hit, float(r), rank)
        s = jnp.where(hit, -jnp.inf, s)
    picked = (s == -jnp.inf) & (s0 != -jnp.inf)
    n_picked = jnp.sum(jnp.where(picked, 1.0, 0.0), axis=0, keepdims=True)
    return jnp.concatenate(vals, axis=0), rank, picked, jnp.where(n_picked == float(k), 1.0, 0.0)


def _candidate_rows():
    k = PEER_TOPK
    groups, valid = [], []
    for r0 in range(k // 2):
        n1 = k // (r0 + 1)
        for g in range(-(-n1 // 8)):
            groups.append((r0, g * 8))
            valid.append([g * 8 + i < n1 for i in range(8)])
    groups.append((None, k // 2))
    valid.append([True] * 8)
    return groups, valid


def _route_head(sa, sb, tie_safe):
    k = PEER_TOPK
    tm = sa.shape[1]
    groups, valid = _candidate_rows()
    sub8 = lax.broadcasted_iota(jnp.int32, (8, tm), 0)
    top_a, rank_a, _, clean_a = _top_k_mask_rows(sa, k, tie_safe)
    top_b, rank_b, _, clean_b = _top_k_mask_rows(sb, k, tie_safe)
    parts = []
    for (r0, off), ok in zip(groups, valid):
        if r0 is None:
            part = top_a[off:off + 8, :] + top_b[0:1, :]
        else:
            part = top_a[r0:r0 + 1, :] + top_b[off:off + 8, :]
        if not all(ok):
            part = jnp.where(sub8 < sum(ok), part, -jnp.inf)
        parts.append(part)
    cand = jnp.concatenate(parts, axis=0)
    _, _, sel, clean_c = _top_k_mask_rows(cand, k, tie_safe, want_rank=False)
    z = jnp.sum(jnp.where(sel, jnp.exp(cand - cand[0:1, :]), 0.0), axis=0, keepdims=True)
    self32 = jnp.where(sel, 1.0, 0.0)
    n_keys = sa.shape[0]
    rank3 = rank_a.astype(BF16).reshape(n_keys // BF16_ROWS, BF16_ROWS, tm)
    length3 = jnp.zeros(rank3.shape, BF16)

    def add_count(r0, count_row):
        tile = jnp.broadcast_to(count_row, (BF16_ROWS, tm)).astype(BF16)
        return length3 + jnp.where(rank3 == float(r0), tile[None], 0.0).astype(BF16)

    tail = None
    per_rank = {}
    for gi, (r0, off) in enumerate(groups):
        cnt = self32[gi * 8:(gi + 1) * 8, :]
        if r0 is None:
            tail = cnt
        else:
            c = jnp.sum(cnt, axis=0, keepdims=True)
            per_rank[r0] = c if r0 not in per_rank else per_rank[r0] + c
    for r in range(8):
        per_rank[k // 2 + r] = tail[r:r + 1, :]
    for r0, c in per_rank.items():
        length3 = add_count(r0, c)
    length = length3.reshape(n_keys, tm).astype(F32)
    pa = jnp.exp(sa - top_a[0:1, :])
    qb = jnp.exp(sb - top_b[0:1, :]) / z
    return pa, length, qb, rank_b, clean_a * clean_b * clean_c


def _peer_query_kernel(x_ref, nw_ref, sc_ref, sh_ref, wq_ref, ht_ref, q_ref):
    h = _modulated_norm(x_ref[...], nw_ref[...], sc_ref[0], sh_ref[0])
    ht_ref[...] = h.T.astype(BF16)
    q = _dot(h.astype(BF16), wq_ref[...])
    dk = q.shape[1] // PEER_HEADS
    for hd in range(PEER_HEADS):
        q_ref[hd] = q[:, hd * dk:(hd + 1) * dk]


def _peer_query(x1, norm_w, sc, sh, wq_bf, tokens_per_batch):
    t, d = x1.shape
    dk = wq_bf.shape[1] // PEER_HEADS
    tm = min(512, tokens_per_batch)
    tpb = tokens_per_batch // tm
    return pl.pallas_call(
        _peer_query_kernel,
        grid=(t // tm,),
        in_specs=[pl.BlockSpec((tm, d), lambda i: (i, 0)),
                  pl.BlockSpec((1, d), lambda i: (0, 0)),
                  pl.BlockSpec((1, 1, d), lambda i: (i // tpb, 0, 0)),
                  pl.BlockSpec((1, 1, d), lambda i: (i // tpb, 0, 0)),
                  pl.BlockSpec(wq_bf.shape, lambda i: (0, 0))],
        out_specs=[pl.BlockSpec((d, tm), lambda i: (0, i)),
                   pl.BlockSpec((PEER_HEADS, tm, dk), lambda i: (0, i, 0))],
        out_shape=[jax.ShapeDtypeStruct((d, t), BF16),
                   jax.ShapeDtypeStruct((PEER_HEADS, t, dk), F32)],
        compiler_params=_cparams(("parallel",)),
        name="peer_query",
    )(x1, norm_w, sc, sh, wq_bf)


def _gelu_exact(x):
    return 0.5 * x * (1.0 + lax.erf(x * 0.7071067811865476))


def _peer_route_act_kernel(ht_ref, q_ref, keys_ref, u_ref,
                           act_ref, pa_ref, len_ref, qb_ref, rnk_ref, code_scr):
    hd = pl.program_id(1)
    tm = ht_ref.shape[1]
    half = keys_ref.shape[-1]
    sub = min(ROUTE_SUB_TOKENS, tm)
    blocked = (PEER_NKEYS // PEER_SLABS, PEER_SLABS, sub)

    @pl.when(hd == 0)
    def _():
        code_scr[...] = jnp.zeros(code_scr.shape, F32)

    act_ref[...] = _dot(u_ref[...], ht_ref[...]).astype(BF16)

    def route(h, t0, tie_safe):
        q = q_ref[h, t0:t0 + sub, :]
        sa = _dot_nt(keys_ref[h, 0], q[:, :half], HIGHEST)
        sb = _dot_nt(keys_ref[h, 1], q[:, half:], HIGHEST)
        pa, length, qbt, rank_b, clean = _route_head(sa, sb, tie_safe)
        pa_ref[h, :, :, t0:t0 + sub] = pa.reshape(blocked)
        len_ref[h, :, :, t0:t0 + sub] = length.reshape(blocked)
        qb_ref[h, :, t0:t0 + sub] = qbt.astype(BF16)
        rnk_ref[h, :, t0:t0 + sub] = rank_b.astype(BF16)
        return clean

    dirty = jnp.zeros((1, 1), F32)
    for t0 in range(0, tm, sub):
        dirty = jnp.maximum(dirty, 1.0 - jnp.min(route(hd, t0, False), axis=1, keepdims=True))
    code_scr[0:1, 0:1] += dirty * lax.shift_left(1, hd).astype(F32)

    @pl.when(hd == pl.num_programs(1) - 1)
    def _():
        tie_bits = code_scr[0, 0].astype(jnp.int32)

        def redo(h, carry):
            @pl.when(((tie_bits >> h) & 1) == 1)
            def _():
                for t0 in range(0, tm, sub):
                    route(h, t0, True)
            return carry

        lax.fori_loop(0, PEER_HEADS, redo, 0)


def _peer_route_act(ht_bf, q_heads, sub_keys, u_bf, tokens_per_batch):
    d, t = ht_bf.shape
    ne = u_bf.shape[0]
    bu = ne // PEER_HEADS
    dk = q_heads.shape[-1]
    tm = min(512, tokens_per_batch)
    n_blocks = PEER_NKEYS // PEER_SLABS
    dense = jax.ShapeDtypeStruct((PEER_HEADS, PEER_NKEYS, t), BF16)
    dense_spec = pl.BlockSpec((PEER_HEADS, PEER_NKEYS, tm), lambda i, j: (0, 0, i))
    blocked = jax.ShapeDtypeStruct((PEER_HEADS, n_blocks, PEER_SLABS, t), F32)
    blocked_spec = pl.BlockSpec((PEER_HEADS, n_blocks, PEER_SLABS, tm), lambda i, j: (0, 0, 0, i))
    return pl.pallas_call(
        _peer_route_act_kernel,
        grid=(t // tm, PEER_HEADS),
        in_specs=[pl.BlockSpec((d, tm), lambda i, j: (0, i)),
                  pl.BlockSpec((PEER_HEADS, tm, dk), lambda i, j: (0, i, 0)),
                  pl.BlockSpec(sub_keys.shape, lambda i, j: (0, 0, 0, 0)),
                  pl.BlockSpec((bu, d), lambda i, j: (j, 0))],
        out_specs=[pl.BlockSpec((bu, tm), lambda i, j: (j, i)),
                   blocked_spec, blocked_spec, dense_spec, dense_spec],
        out_shape=[jax.ShapeDtypeStruct((ne, t), BF16), blocked, blocked, dense, dense],
        scratch_shapes=[pltpu.VMEM((8, LANES), F32)],
        compiler_params=_cparams(("parallel", "arbitrary")),
        name="peer_route_act",
    )(ht_bf, q_heads, sub_keys, u_bf)


def _peer_expert_kernel(act_ref, vt_ref, pa_ref, len_ref, qb_ref, rnk_ref, x_ref, g2_ref, o_ref, acc_scr):
    j = pl.program_id(1)
    be, tm = act_ref.shape

    @pl.when(j == 0)
    def _():
        acc_scr[...] = jnp.zeros(acc_scr.shape, F32)

    n_slabs = be // PEER_NKEYS
    rows = BF16_ROWS
    n_groups = PEER_NKEYS // rows

    def row_tile(ref, hd, s):
        return jnp.broadcast_to(ref[hd, 0, s:s + 1, :], (rows, tm)).astype(BF16)

    zero = jnp.zeros((rows, tm), BF16)
    parts = [[None] * n_groups for _ in range(n_slabs)]
    slab_group = 8
    for s0 in range(0, n_slabs, slab_group):
        ss = range(s0, min(s0 + slab_group, n_slabs))
        ln = {(hd, s): row_tile(len_ref, hd, s) for hd in range(PEER_HEADS) for s in ss}
        pa = {(hd, s): row_tile(pa_ref, hd, s) for hd in range(PEER_HEADS) for s in ss}
        for g in range(n_groups):
            sl = slice(g * rows, (g + 1) * rows)
            acc = {}
            for hd in range(PEER_HEADS):
                rk = rnk_ref[hd, sl, :]
                qv = qb_ref[hd, sl, :]
                for s in ss:
                    term = jnp.where(rk < ln[hd, s], qv, zero) * pa[hd, s]
                    acc[s] = term if s not in acc else acc[s] + term
            for s in ss:
                lo = s * PEER_NKEYS + g * rows
                parts[s][g] = acc[s] * _gelu_exact(act_ref[lo:lo + rows, :].astype(F32)).astype(BF16)
    p = jnp.concatenate([parts[s][g] for s in range(n_slabs) for g in range(n_groups)], axis=0)
    acc_scr[...] += _dot(vt_ref[0], p)

    @pl.when(j == pl.num_programs(1) - 1)
    def _():
        o_ref[...] = x_ref[...] + g2_ref[0] * acc_scr[...].T


def _peer_expert(act, vt_bf, pa, ln, qb, rnk, x1, g2, tokens_per_batch):
    t, d = x1.shape
    ne = act.shape[0]
    tm = min(512, tokens_per_batch)
    tpb = tokens_per_batch // tm
    be = PEER_EXPERT_BLOCK
    dense_spec = pl.BlockSpec((PEER_HEADS, PEER_NKEYS, tm), lambda i, j: (0, 0, i))
    blocked_spec = pl.BlockSpec((PEER_HEADS, 1, PEER_SLABS, tm), lambda i, j: (0, j, 0, i))
    return pl.pallas_call(
        _peer_expert_kernel,
        grid=(t // tm, ne // be),
        in_specs=[pl.BlockSpec((be, tm), lambda i, j: (j, i)),
                  pl.BlockSpec((1, d, be), lambda i, j: (j, 0, 0)),
                  blocked_spec, blocked_spec, dense_spec, dense_spec,
                  pl.BlockSpec((tm, d), lambda i, j: (i, 0)),
                  pl.BlockSpec((1, 1, d), lambda i, j: (i // tpb, 0, 0))],
        out_specs=pl.BlockSpec((tm, d), lambda i, j: (i, 0)),
        out_shape=jax.ShapeDtypeStruct((t, d), F32),
        scratch_shapes=[pltpu.VMEM((d, tm), F32)],
        compiler_params=_cparams(("parallel", "arbitrary")),
        name="peer_expert",
    )(act, vt_bf, pa, ln, qb, rnk, x1, g2)


def _pad_cols(a, n):
    return jnp.pad(a, ((0, 0), (0, n - a.shape[1])))


def _pad_rows(a, n):
    return jnp.pad(a, ((0, n - a.shape[0]), (0, 0)))


def _layer(x, c, w_ada, b_ada, norm_mix_w, w_in, fox_q_norm_w, fox_k_norm_w, fox_f_bias,
           rwkv_mu, rwkv_w0, rwkv_w_up, rwkv_a0, rwkv_a_up, rwkv_g_up, rwkv_k_k, rwkv_k_a,
           rwkv_r_k, rwkv_ln_w, rwkv_ln_b, w_out, norm_ffn_w, peer_w_query, peer_sub_keys,
           peer_u, peer_v):
    B, S, D = x.shape
    T = B * S
    fw = FOX_HEADS * FOX_HEAD_DIM
    rw = RWKV_HEADS * RWKV_HEAD_DIM
    w_lora = rwkv_w_up.shape[0]
    a_lora = rwkv_a_up.shape[0]
    g_lora = rwkv_g_up.shape[0]
    assert w_lora <= LANES and a_lora + FOX_HEADS <= LANES and g_lora == 256

    c_pad = _pad_rows(c, 8)
    mod = _ada_mod(c_pad, w_ada, b_ada)[:B]
    sh1, sc1, g1, sh2, sc2, g2 = [m.reshape(B, 1, D) for m in jnp.split(mod, 6, axis=-1)]

    fox_cols = 4 * fw + FOX_HEADS
    wi_fox, wi_rw = w_in[:, :fox_cols], w_in[:, fox_cols:]
    mu = rwkv_mu.reshape(1, -1)
    seg = lambda a, lo, n: a[:, lo:lo + n]
    w_perm = jnp.concatenate([
        seg(wi_fox, 0, 4 * fw),
        seg(wi_rw, 0, 3 * rw),
        seg(wi_rw, 3 * rw + w_lora + a_lora, g_lora),
        _pad_cols(seg(wi_rw, 3 * rw, w_lora), LANES),
        _pad_cols(jnp.concatenate([seg(wi_rw, 3 * rw + w_lora, a_lora), seg(wi_fox, 4 * fw, FOX_HEADS)], 1), LANES),
    ], axis=1).astype(BF16)
    cols = {"gate": 3 * fw, "rr": 4 * fw, "rk": 4 * fw + rw, "rv": 4 * fw + 2 * rw,
            "gd": 4 * fw + 3 * rw, "lo": 4 * fw + 3 * rw + g_lora}
    f_lane = a_lora
    f_block = (cols["lo"] + LANES) // LANES
    mus = [seg(mu, 0, rw), seg(mu, rw, rw), seg(mu, 2 * rw, rw),
           seg(mu, 3 * rw + w_lora + a_lora, g_lora),
           jnp.concatenate([_pad_cols(seg(mu, 3 * rw, w_lora), LANES),
                            _pad_cols(seg(mu, 3 * rw + w_lora, a_lora), LANES)], 1)]
    scale = FOX_HEAD_DIM ** -0.5 * LOG2_E
    head_w = _pad_cols(jnp.concatenate([jnp.tile(fox_q_norm_w * scale, FOX_HEADS),
                                        jnp.tile(fox_k_norm_w, FOX_HEADS)]).reshape(1, -1), w_perm.shape[1])

    x2 = x.reshape(T, D)
    proj = _in_proj(_norm_mod(x2, norm_mix_w.reshape(1, D), sc1, sh1, S), w_perm, head_w, 2 * fw)

    f_bias_row = jnp.zeros((1, LANES), F32).at[0, f_lane:f_lane + FOX_HEADS].set(fox_f_bias)
    cum = _fox_cum(proj, f_bias_row, B, S, f_block)
    cum_rows = cum.reshape(B, S, LANES)[:, :, f_lane:f_lane + FOX_HEADS].transpose(0, 2, 1).reshape(B, FOX_HEADS, 1, S)
    o_fox = _fox_attn(proj, cum_rows, B, S)

    row = lambda a: a.reshape(1, -1)
    r, lw, k, v, a_vec, b_vec, g, bonus = _rwkv_prep(
        proj, cols, mus, row(rwkv_w0), _pad_rows(rwkv_w_up, LANES), row(rwkv_a0), _pad_rows(rwkv_a_up, LANES),
        rwkv_g_up, row(rwkv_k_k), row(rwkv_k_a), row(rwkv_r_k), S)
    y = _rwkv_scan(r, lw, k, v, a_vec, b_vec, B, S)

    x1 = _mix_out(o_fox, proj, cols["gate"], y, bonus, g, x2, g1, row(rwkv_ln_w), row(rwkv_ln_b),
                  w_out.astype(BF16), S)

    ht, q_heads = _peer_query(x1, norm_ffn_w.reshape(1, D), sc2, sh2, peer_w_query.astype(BF16), S)
    act, pa, ln, qb, rnk = _peer_route_act(ht, q_heads, peer_sub_keys, peer_u.astype(BF16), S)
    vt = peer_v.astype(BF16).reshape(-1, PEER_EXPERT_BLOCK, D).transpose(0, 2, 1)
    out = _peer_expert(act, vt, pa, ln, qb, rnk, x1, g2, S)
    return out.reshape(B, S, D)


def kernel(x, c, w_ada, b_ada, norm_mix_w, w_in, fox_q_norm_w, fox_k_norm_w, fox_f_bias, rwkv_mu, rwkv_w0,
           rwkv_w_up, rwkv_a0, rwkv_a_up, rwkv_g_up, rwkv_k_k, rwkv_k_a, rwkv_r_k, rwkv_ln_w, rwkv_ln_b,
           w_out, norm_ffn_w, peer_w_query, peer_sub_keys, peer_u, peer_v):
    params = (w_ada, b_ada, norm_mix_w, w_in, fox_q_norm_w, fox_k_norm_w, fox_f_bias, rwkv_mu, rwkv_w0,
              rwkv_w_up, rwkv_a0, rwkv_a_up, rwkv_g_up, rwkv_k_k, rwkv_k_a, rwkv_r_k, rwkv_ln_w, rwkv_ln_b,
              w_out, norm_ffn_w, peer_w_query, peer_sub_keys, peer_u, peer_v)
    for l in range(w_ada.shape[0]):
        x = _layer(x, c, *[p[l] for p in params])
    return x
```

```python
import functools

import jax
import jax.numpy as jnp
from jax import lax
from jax.experimental import pallas as pl
from jax.experimental.pallas import tpu as pltpu

F32 = jnp.float32
BF16 = jnp.bfloat16
HIGHEST = lax.Precision.HIGHEST

LANES = 128
BF16_ROWS = 16
NORM_EPS = 1e-6
GN_EPS = 64e-5
CHUNK = 64
FOX_HEADS = 8
FOX_HEAD_DIM = 128
RWKV_HEADS = 16
RWKV_HEAD_DIM = 64
PEER_HEADS = 8
PEER_NKEYS = 128
PEER_TOPK = 16
PEER_EXPERT_BLOCK = 2048
PEER_SLABS = PEER_EXPERT_BLOCK // PEER_NKEYS
ROUTE_SUB_TOKENS = 256
NEG_BIG = -1e30
LOG2_E = 1.4426950408889634
VMEM_LIMIT = 56 * 1024 * 1024


def _cparams(sem):
    return pltpu.CompilerParams(dimension_semantics=sem, vmem_limit_bytes=VMEM_LIMIT)


def _dot(a, b, precision=None):
    return jnp.dot(a, b, preferred_element_type=F32, precision=precision)


def _dot_nt(a, b, precision=None):
    return lax.dot_general(a, b, (((1,), (1,)), ((), ())), preferred_element_type=F32,
                           precision=precision)


def _split_bf16(x):
    hi = x.astype(BF16)
    return hi, (x - hi.astype(F32)).astype(BF16)


def _dot_bf16x3(a, b):
    ah, al = _split_bf16(a)
    bh, bl = _split_bf16(b)
    return _dot(ah, bh) + (_dot(ah, bl) + _dot(al, bh))


def _dot_tn(a, b, precision=None):
    return lax.dot_general(a, b, (((0,), (0,)), ((), ())), preferred_element_type=F32,
                           precision=precision)


def _ada_kernel(c_ref, w_ref, b_ref, o_ref):
    c = c_ref[...]
    s = c * jax.nn.sigmoid(c)
    o_ref[...] = _dot(s, w_ref[...], HIGHEST) + b_ref[...]


def _ada_mod(c_pad, w_ada, b_ada):
    rows, d = c_pad.shape
    n = w_ada.shape[1]
    bn = 1024
    return pl.pallas_call(
        _ada_kernel,
        grid=(n // bn,),
        in_specs=[pl.BlockSpec((rows, d), lambda j: (0, 0)),
                  pl.BlockSpec((d, bn), lambda j: (0, j)),
                  pl.BlockSpec((1, bn), lambda j: (0, j))],
        out_specs=pl.BlockSpec((rows, bn), lambda j: (0, j)),
        out_shape=jax.ShapeDtypeStruct((rows, n), F32),
        compiler_params=_cparams(("arbitrary",)),
        name="ada_mod",
    )(c_pad, w_ada, b_ada.reshape(1, n))


def _modulated_norm(x, nw, sc, sh):
    y = x * lax.rsqrt(jnp.mean(x * x, axis=-1, keepdims=True) + NORM_EPS)
    return y * nw * (1.0 + sc) + sh


def _norm_mod_kernel(x_ref, nw_ref, sc_ref, sh_ref, o_ref):
    o_ref[...] = _modulated_norm(x_ref[...], nw_ref[...], sc_ref[0], sh_ref[0]).astype(BF16)


def _norm_mod(x2, norm_w, sc, sh, tokens_per_batch):
    t, d = x2.shape
    tm = min(512, tokens_per_batch)
    tpb = tokens_per_batch // tm
    return pl.pallas_call(
        _norm_mod_kernel,
        grid=(t // tm,),
        in_specs=[pl.BlockSpec((tm, d), lambda i: (i, 0)),
                  pl.BlockSpec((1, d), lambda i: (0, 0)),
                  pl.BlockSpec((1, 1, d), lambda i: (i // tpb, 0, 0)),
                  pl.BlockSpec((1, 1, d), lambda i: (i // tpb, 0, 0))],
        out_specs=pl.BlockSpec((tm, d), lambda i: (i, 0)),
        out_shape=jax.ShapeDtypeStruct((t, d), BF16),
        compiler_params=_cparams(("parallel",)),
        name="norm_mod",
    )(x2, norm_w, sc, sh)


def _in_proj_kernel(h_ref, w_ref, hw_ref, o_ref, *, n_qk_blocks):
    j = pl.program_id(1)
    acc = _dot(h_ref[...], w_ref[...])

    @pl.when(j < n_qk_blocks)
    def _():
        for hh in range(acc.shape[1] // FOX_HEAD_DIM):
            sl = slice(hh * FOX_HEAD_DIM, (hh + 1) * FOX_HEAD_DIM)
            a = acc[:, sl]
            rs = lax.rsqrt(jnp.mean(a * a, axis=-1, keepdims=True) + NORM_EPS)
            o_ref[:, sl] = a * rs * hw_ref[:, sl]

    @pl.when(j >= n_qk_blocks)
    def _():
        o_ref[...] = acc


def _in_proj(h_bf, w_bf, head_w, n_qk_cols):
    t, d = h_bf.shape
    n = w_bf.shape[1]
    tm = min(2048, t)
    bn = 512
    kern = functools.partial(_in_proj_kernel, n_qk_blocks=n_qk_cols // bn)
    return pl.pallas_call(
        kern,
        grid=(t // tm, n // bn),
        in_specs=[pl.BlockSpec((tm, d), lambda i, j: (i, 0)),
                  pl.BlockSpec((d, bn), lambda i, j: (0, j)),
                  pl.BlockSpec((1, bn), lambda i, j: (0, j))],
        out_specs=pl.BlockSpec((tm, bn), lambda i, j: (i, j)),
        out_shape=jax.ShapeDtypeStruct((t, n), F32),
        compiler_params=_cparams(("parallel", "arbitrary")),
        name="in_proj",
    )(h_bf, w_bf, head_w)


def _log_sigmoid(x):
    return jnp.minimum(x, 0.0) - jnp.log(1.0 + jnp.exp(-jnp.abs(x)))


def _fox_cum_kernel(f_ref, b_ref, o_ref, *, blk):
    s = f_ref.shape[0]
    row = lax.broadcasted_iota(jnp.int32, (blk, blk), 0)
    col = lax.broadcasted_iota(jnp.int32, (blk, blk), 1)
    tri = (row >= col).astype(F32)
    carry = jnp.zeros((1, f_ref.shape[1]), F32)
    for i in range(s // blk):
        lf = _log_sigmoid(f_ref[i * blk:(i + 1) * blk, :] + b_ref[...])
        cs = _dot(tri, lf, HIGHEST) + carry
        o_ref[i * blk:(i + 1) * blk, :] = cs * LOG2_E
        carry = cs[blk - 1:blk, :]


def _fox_cum(proj, f_bias_row, batch, seq, col_block):
    blk = min(256, seq)
    return pl.pallas_call(
        functools.partial(_fox_cum_kernel, blk=blk),
        grid=(batch,),
        in_specs=[pl.BlockSpec((seq, LANES), lambda b: (b, col_block)),
                  pl.BlockSpec((1, LANES), lambda b: (0, 0))],
        out_specs=pl.BlockSpec((seq, LANES), lambda b: (b, 0)),
        out_shape=jax.ShapeDtypeStruct((batch * seq, LANES), F32),
        compiler_params=_cparams(("parallel",)),
        name="fox_cum",
    )(proj, f_bias_row)


def _fox_attn_kernel(qt_ref, kt_ref, q_ref, k_ref, v_ref, c_ref, o_ref, m_scr, l_scr, acc_scr):
    t = pl.program_id(2)
    qi = qt_ref[t]
    ki = kt_ref[t]
    hd = FOX_HEAD_DIM
    n_heads = q_ref.shape[1] // hd

    @pl.when(ki == 0)
    def _():
        m_scr[...] = jnp.full(m_scr.shape, NEG_BIG, F32)
        l_scr[...] = jnp.zeros(l_scr.shape, F32)
        acc_scr[...] = jnp.zeros(acc_scr.shape, F32)

    def step(masked):
        hs = range(n_heads)
        sl = [slice(h * hd, (h + 1) * hd) for h in hs]
        s = [_dot_nt(q_ref[:, sl[h]].astype(BF16), k_ref[:, sl[h]].astype(BF16)) - c_ref[0, h] for h in hs]
        if masked:
            row = lax.broadcasted_iota(jnp.int32, s[0].shape, 0)
            col = lax.broadcasted_iota(jnp.int32, s[0].shape, 1)
            s = [jnp.where(row >= col, x, NEG_BIG) for x in s]
        m_prev = [m_scr[h] for h in hs]
        m_new = [jnp.maximum(m_prev[h], jnp.max(s[h], axis=-1, keepdims=True)) for h in hs]
        n_rep = s[0].shape[1] // hd
        p = [jnp.exp2(s[h] - jnp.concatenate([m_new[h]] * n_rep, axis=1)) for h in hs]
        alpha = [jnp.exp2(m_prev[h] - m_new[h]) for h in hs]
        ones = jnp.ones((k_ref.shape[0], hd), BF16)
        pv = [_dot(p[h].astype(BF16), jnp.concatenate([v_ref[:, sl[h]].astype(BF16), ones], axis=1))
              for h in hs]
        for h in hs:
            l_scr[h] = alpha[h] * l_scr[h] + pv[h][:, hd:]
            acc_scr[:, sl[h]] = alpha[h] * acc_scr[:, sl[h]] + pv[h][:, :hd]
            m_scr[h] = m_new[h]

    @pl.when(ki < qi)
    def _():
        step(False)

    @pl.when(ki == qi)
    def _():
        step(True)
        for h in range(n_heads):
            sl = slice(h * hd, (h + 1) * hd)
            o_ref[:, sl] = acc_scr[:, sl] / l_scr[h]


def _fox_attn(proj, cum_rows, batch, seq):
    tq = min(512, seq)
    nq = seq // tq
    hps = 4
    w = hps * FOX_HEAD_DIM
    kcol = FOX_HEADS // hps
    tri = [(q, k) for q in range(nq) for k in range(q + 1)]
    qt = jnp.asarray([q for q, _ in tri], jnp.int32)
    kt = jnp.asarray([k for _, k in tri], jnp.int32)
    grid_spec = pltpu.PrefetchScalarGridSpec(
        num_scalar_prefetch=2,
        grid=(batch, kcol, len(tri)),
        in_specs=[
            pl.BlockSpec((tq, w), lambda b, h, t, qt, kt: (b * nq + qt[t], h)),
            pl.BlockSpec((tq, w), lambda b, h, t, qt, kt: (b * nq + kt[t], kcol + h)),
            pl.BlockSpec((tq, w), lambda b, h, t, qt, kt: (b * nq + kt[t], 2 * kcol + h)),
            pl.BlockSpec((1, hps, 1, tq), lambda b, h, t, qt, kt: (b, h, 0, kt[t])),
        ],
        out_specs=pl.BlockSpec((tq, w), lambda b, h, t, qt, kt: (b * nq + qt[t], h)),
        scratch_shapes=[pltpu.VMEM((hps, tq, FOX_HEAD_DIM), F32), pltpu.VMEM((hps, tq, FOX_HEAD_DIM), F32),
                        pltpu.VMEM((tq, w), F32)],
    )
    return pl.pallas_call(
        _fox_attn_kernel,
        grid_spec=grid_spec,
        out_shape=jax.ShapeDtypeStruct((batch * seq, FOX_HEADS * FOX_HEAD_DIM), F32),
        compiler_params=_cparams(("parallel", "parallel", "arbitrary")),
        name="fox_attn",
    )(qt, kt, proj, proj, proj, cum_rows)


def _head_sum(x, bd):
    hi, lo = _split_bf16(x)
    parts = [_dot(hi[:, j * LANES:(j + 1) * LANES], bd) + _dot(lo[:, j * LANES:(j + 1) * LANES], bd)
             for j in range(x.shape[1] // LANES)]
    return jnp.concatenate(parts, axis=1)


def _head_block_diag():
    r = lax.broadcasted_iota(jnp.int32, (LANES, LANES), 0) // RWKV_HEAD_DIM
    c = lax.broadcasted_iota(jnp.int32, (LANES, LANES), 1) // RWKV_HEAD_DIM
    return (r == c).astype(BF16)


def _rwkv_prep_kernel(r_ref, k_ref, v_ref, gd_ref, lo_ref, pr_ref, pk_ref, pv_ref, pgd_ref, plo_ref,
                      mu_r, mu_k, mu_v, mu_gd, mu_lo, w0_ref, wup_ref, a0_ref, aup_ref, gup_ref,
                      kk_ref, ka_ref, rk_ref,
                      or_ref, olw_ref, ok_ref, ov_ref, oa_ref, ob_ref, og_ref, obonus_ref, *, tiles_per_batch):
    i = pl.program_id(0)
    first = (i % tiles_per_batch) == 0

    def shifted(cur_ref, prv_ref, mu_ref):
        cur = cur_ref[...]
        last = jnp.where(first, 0.0, prv_ref[7:8, :])
        row = lax.broadcasted_iota(jnp.int32, cur.shape, 0)
        prev = jnp.where(row == 0, last, pltpu.roll(cur, 1, 0))
        return cur + (prev - cur) * mu_ref[...]

    r = shifted(r_ref, pr_ref, mu_r)
    k = shifted(k_ref, pk_ref, mu_k)
    v = shifted(v_ref, pv_ref, mu_v)
    gd = shifted(gd_ref, pgd_ref, mu_gd)
    lo = shifted(lo_ref, plo_ref, mu_lo)
    wd = lo[:, :LANES]
    ad = lo[:, LANES:]

    w_pre = w0_ref[...] + _dot_bf16x3(jnp.tanh(wd), wup_ref[...])
    w_raw = _log_sigmoid(w_pre) - 0.5
    log_decay = -jnp.exp(w_raw)
    a = jax.nn.sigmoid(a0_ref[...] + _dot_bf16x3(ad, aup_ref[...]))
    g = _dot_bf16x3(jax.nn.sigmoid(gd), gup_ref[...])

    bd = _head_block_diag()
    kk = k * kk_ref[...]
    nrm = jnp.maximum(jnp.sqrt(_head_sum(kk * kk, bd)), 1e-12)
    kk = kk / nrm
    k_mod = k * (1.0 + (a - 1.0) * ka_ref[...])
    bonus = _head_sum(r * k_mod * rk_ref[...], bd) * v

    or_ref[...] = r.astype(BF16)
    olw_ref[...] = log_decay
    ok_ref[...] = k_mod.astype(BF16)
    ov_ref[...] = v.astype(BF16)
    oa_ref[...] = (-kk).astype(BF16)
    ob_ref[...] = (kk * a).astype(BF16)
    og_ref[...] = g.astype(BF16)
    obonus_ref[...] = bonus.astype(BF16)


def _rwkv_prep(proj, cols, mus, w0, w_up, a0, a_up, g_up, k_k, k_a, r_k, tokens_per_batch):
    t = proj.shape[0]
    w = RWKV_HEADS * RWKV_HEAD_DIM
    tm = min(512, tokens_per_batch)
    tpb = tokens_per_batch // tm
    widths = [w, w, w, 256, 256]
    offs = [cols["rr"], cols["rk"], cols["rv"], cols["gd"], cols["lo"]]
    cur_specs = [pl.BlockSpec((tm, wd), functools.partial(lambda i, cb: (i, cb), cb=o // wd))
                 for wd, o in zip(widths, offs)]
    prv_specs = [pl.BlockSpec((8, wd), functools.partial(
        lambda i, cb: (jnp.maximum(i * (tm // 8) - 1, 0), cb), cb=o // wd))
        for wd, o in zip(widths, offs)]
    full = lambda a: pl.BlockSpec(a.shape, lambda i: (0,) * a.ndim)
    params = list(mus) + [w0, w_up, a0, a_up, g_up, k_k, k_a, r_k]
    out_spec = pl.BlockSpec((tm, w), lambda i: (i, 0))
    return pl.pallas_call(
        functools.partial(_rwkv_prep_kernel, tiles_per_batch=tpb),
        grid=(t // tm,),
        in_specs=cur_specs + prv_specs + [full(p) for p in params],
        out_specs=[out_spec] * 8,
        out_shape=[jax.ShapeDtypeStruct((t, w), dt) for dt in (BF16, F32, BF16, BF16, BF16, BF16, BF16, BF16)],
        compiler_params=_cparams(("parallel",)),
        name="rwkv_prep",
    )(*([proj] * 10), *params)


def _rwkv_scan_kernel(r_ref, lw_ref, k_ref, v_ref, a_ref, b_ref, o_ref, h_scr):
    c = pl.program_id(1)

    @pl.when(c == 0)
    def _():
        h_scr[...] = jnp.zeros(h_scr.shape, F32)

    C = r_ref.shape[0]
    n_pairs = r_ref.shape[1] // LANES
    P = HIGHEST

    row = lax.broadcasted_iota(jnp.int32, (C, C), 0)
    col = lax.broadcasted_iota(jnp.int32, (C, C), 1)
    tri = (row >= col).astype(F32)
    lw = lw_ref[...]
    cw = _dot(tri, lw, P)
    cw_end = cw[C - 1:C, :]
    e_pos = jnp.exp(cw)
    e_prev = jnp.exp(cw - lw)
    e_neg = jnp.exp(-cw)
    e_end = jnp.exp(cw_end - cw)
    w_end = jnp.exp(cw_end)

    a = a_ref[...].astype(F32)
    b = b_ref[...].astype(F32)
    k = k_ref[...].astype(F32)
    r = r_ref[...].astype(F32)
    at = a * e_prev
    bt = b * e_neg
    kt = k * e_neg
    rt = r * e_pos
    bh = b * e_end
    kh = k * e_end
    v = v_ref[...].astype(F32)

    lane = lax.broadcasted_iota(jnp.int32, (C, LANES), 1)
    head0 = lane < RWKV_HEAD_DIM
    r2 = lax.broadcasted_iota(jnp.int32, (2 * C, 2 * C), 0)
    c2 = lax.broadcasted_iota(jnp.int32, (2 * C, 2 * C), 1)
    same = (r2 // C) == (c2 // C)
    strict = same & (r2 > c2)
    incl = same & (r2 >= c2)
    eye = (r2 == c2).astype(F32)

    def two(x, p):
        xp = x[:, p * LANES:(p + 1) * LANES]
        return jnp.concatenate([jnp.where(head0, xp, 0.0), jnp.where(head0, 0.0, xp)], axis=0)

    def mm(x, y):
        return _dot(x.astype(BF16), y.astype(BF16))

    def mm3(x, y):
        xh, xl = _split_bf16(x)
        yh, yl = _split_bf16(y)
        n = y.shape[1]
        wide = _dot(xh, jnp.concatenate([yh, yl], axis=1))
        return wide[:, :n] + (wide[:, n:] + _dot(xl, yh))

    G = 2 * C
    pairs = range(n_pairs)
    at2 = [two(at, p) for p in pairs]
    rt2 = [two(rt, p) for p in pairs]
    v2 = [two(v, p).astype(BF16) for p in pairs]
    gram = [_dot_nt(jnp.concatenate([at2[p], rt2[p]], axis=0).astype(BF16),
                    jnp.concatenate([two(bt, p), two(kt, p)], axis=0).astype(BF16)) for p in pairs]
    a_ab = [jnp.where(strict, gram[p][:G, :G], 0.0) for p in pairs]
    nmat = list(a_ab)
    pw = list(a_ab)
    for _ in range(max(1, (C - 1).bit_length() - 1)):
        pw = [mm3(pw[p], pw[p]) for p in pairs]
        nmat = [nmat[p] + pw[p] + mm3(nmat[p], pw[p]) for p in pairs]
    akv = [mm(jnp.where(strict, gram[p][:G, G:], 0.0), v2[p]) for p in pairs]
    rhs = [jnp.concatenate([at2[p], akv[p]], axis=1) for p in pairs]
    pq = [(rhs[p] + mm(nmat[p], rhs[p])).astype(BF16) for p in pairs]
    ry = [mm(jnp.where(incl, gram[p][G:, :G], 0.0), pq[p]) for p in pairs]
    mv = [mm(jnp.where(incl, gram[p][G:, G:], 0.0), v2[p]) for p in pairs]
    ge = [_dot_tn(two(bh, p).astype(BF16), pq[p]) for p in pairs]
    kv = [_dot_tn(two(kh, p).astype(BF16), v2[p]) for p in pairs]
    for p in pairs:
        rr = rt2[p] + ry[p][:, :LANES]
        gm = eye * w_end[:, p * LANES:(p + 1) * LANES] + ge[p][:, :LANES]
        yh = mm(jnp.concatenate([rr, gm], axis=0), h_scr[p])
        y2 = yh[:G] + ry[p][:, LANES:] + mv[p]
        h_scr[p] = yh[G:] + ge[p][:, LANES:] + kv[p]
        o_ref[:, p * LANES:(p + 1) * LANES] = y2[:C, :] + y2[C:, :]


def _rwkv_scan(r, lw, k, v, a, b, batch, seq):
    w = r.shape[1]
    nc = seq // CHUNK
    spec = pl.BlockSpec((CHUNK, w), lambda bi, ci: (bi * nc + ci, 0))
    return pl.pallas_call(
        _rwkv_scan_kernel,
        grid=(batch, nc),
        in_specs=[spec] * 6,
        out_specs=spec,
        out_shape=jax.ShapeDtypeStruct((batch * seq, w), F32),
        scratch_shapes=[pltpu.VMEM((w // LANES, LANES, LANES), F32)],
        compiler_params=_cparams(("parallel", "arbitrary")),
        name="rwkv_scan",
    )(r, lw, k, v, a, b)


def _mix_out_kernel(o_ref, gate_ref, y_ref, bonus_ref, g_ref, x_ref, g1_ref, lnw_ref, lnb_ref, w_ref, out_ref):
    fox = o_ref[...] * jax.nn.sigmoid(gate_ref[...])
    bd = _head_block_diag()
    y = y_ref[...]
    inv_n = 1.0 / RWKV_HEAD_DIM
    mean = _head_sum(y, bd) * inv_n
    d = y - mean
    var = _head_sum(d * d, bd) * inv_n
    yn = d * lax.rsqrt(var + GN_EPS) * lnw_ref[...] + lnb_ref[...]
    rw = (yn + bonus_ref[...].astype(F32)) * g_ref[...].astype(F32)
    wf = fox.shape[1]
    mix = _dot(fox.astype(BF16), w_ref[:wf, :]) + _dot(rw.astype(BF16), w_ref[wf:, :])
    out_ref[...] = x_ref[...] + g1_ref[0] * mix


def _mix_out(o_fox, proj, gate_col, y, bonus, g, x2, g1, ln_w, ln_b, w_out_bf, tokens_per_batch):
    t, d = x2.shape
    wf = o_fox.shape[1]
    wr = y.shape[1]
    tm = min(512, tokens_per_batch)
    tpb = tokens_per_batch // tm
    return pl.pallas_call(
        _mix_out_kernel,
        grid=(t // tm,),
        in_specs=[pl.BlockSpec((tm, wf), lambda i: (i, 0)),
                  pl.BlockSpec((tm, wf), lambda i: (i, gate_col // wf)),
                  pl.BlockSpec((tm, wr), lambda i: (i, 0)),
                  pl.BlockSpec((tm, wr), lambda i: (i, 0)),
                  pl.BlockSpec((tm, wr), lambda i: (i, 0)),
                  pl.BlockSpec((tm, d), lambda i: (i, 0)),
                  pl.BlockSpec((1, 1, d), lambda i: (i // tpb, 0, 0)),
                  pl.BlockSpec((1, wr), lambda i: (0, 0)),
                  pl.BlockSpec((1, wr), lambda i: (0, 0)),
                  pl.BlockSpec(w_out_bf.shape, lambda i: (0, 0))],
        out_specs=pl.BlockSpec((tm, d), lambda i: (i, 0)),
        out_shape=jax.ShapeDtypeStruct((t, d), F32),
        compiler_params=_cparams(("parallel",)),
        name="mix_out",
    )(o_fox, proj, y, bonus, g, x2, g1, ln_w, ln_b, w_out_bf)


def _top_k_mask_rows(s, k, tie_safe, want_rank=True):
    n = s.shape[0]
    s0 = s
    row = lax.broadcasted_iota(jnp.int32, s.shape, 0).astype(F32)
    rank = jnp.full(s.shape, float(k), F32) if want_rank else None
    vals = []
    for r in range(k):
        m = jnp.max(s, axis=0, keepdims=True)
        hit = s == m
        if tie_safe:
            pos = jnp.min(jnp.where(hit, row, float(n)), axis=0, keepdims=True)
            hit = row == pos
        vals.append(m)
        if want_rank:
            rank = jnp.where(hit, float(r), rank)
        s = jnp.where(hit, -jnp.inf, s)
    picked = (s == -jnp.inf) & (s0 != -jnp.inf)
    n_picked = jnp.sum(jnp.where(picked, 1.0, 0.0), axis=0, keepdims=True)
    return jnp.concatenate(vals, axis=0), rank, picked, jnp.where(n_picked == float(k), 1.0, 0.0)


def _candidate_rows():
    k = PEER_TOPK
    groups, valid = [], []
    for r0 in range(k // 2):
        n1 = k // (r0 + 1)
        for g in range(-(-n1 // 8)):
            groups.append((r0, g * 8))
            valid.append([g * 8 + i < n1 for i in range(8)])
    groups.append((None, k // 2))
    valid.append([True] * 8)
    return groups, valid


def _route_head(sa, sb, tie_safe):
    k = PEER_TOPK
    tm = sa.shape[1]
    groups, valid = _candidate_rows()
    sub8 = lax.broadcasted_iota(jnp.int32, (8, tm), 0)
    top_a, rank_a, _, clean_a = _top_k_mask_rows(sa, k, tie_safe)
    top_b, rank_b, _, clean_b = _top_k_mask_rows(sb, k, tie_safe)
    parts = []
    for (r0, off), ok in zip(groups, valid):
        if r0 is None:
            part = top_a[off:off + 8, :] + top_b[0:1, :]
        else:
            part = top_a[r0:r0 + 1, :] + top_b[off:off + 8, :]
        if not all(ok):
            part = jnp.where(sub8 < sum(ok), part, -jnp.inf)
        parts.append(part)
    cand = jnp.concatenate(parts, axis=0)
    _, _, sel, clean_c = _top_k_mask_rows(cand, k, tie_safe, want_rank=False)
    z = jnp.sum(jnp.where(sel, jnp.exp(cand - cand[0:1, :]), 0.0), axis=0, keepdims=True)
    self32 = jnp.where(sel, 1.0, 0.0)
    n_keys = sa.shape[0]
    rank3 = rank_a.astype(BF16).reshape(n_keys // BF16_ROWS, BF16_ROWS, tm)
    length3 = jnp.zeros(rank3.shape, BF16)

    def add_count(r0, count_row):
        tile = jnp.broadcast_to(count_row, (BF16_ROWS, tm)).astype(BF16)
        return length3 + jnp.where(rank3 == float(r0), tile[None], 0.0).astype(BF16)

    tail = None
    per_rank = {}
    for gi, (r0, off) in enumerate(groups):
        cnt = self32[gi * 8:(gi + 1) * 8, :]
        if r0 is None:
            tail = cnt
        else:
            c = jnp.sum(cnt, axis=0, keepdims=True)
            per_rank[r0] = c if r0 not in per_rank else per_rank[r0] + c
    for r in range(8):
        per_rank[k // 2 + r] = tail[r:r + 1, :]
    for r0, c in per_rank.items():
        length3 = add_count(r0, c)
    length = length3.reshape(n_keys, tm).astype(F32)
    pa = jnp.exp(sa - top_a[0:1, :])
    qb = jnp.exp(sb - top_b[0:1, :]) / z
    return pa, length, qb, rank_b, clean_a * clean_b * clean_c


def _peer_query_kernel(x_ref, nw_ref, sc_ref, sh_ref, wq_ref, ht_ref, q_ref):
    h = _modulated_norm(x_ref[...], nw_ref[...], sc_ref[0], sh_ref[0])
    ht_ref[...] = h.T.astype(BF16)
    q = _dot(h.astype(BF16), wq_ref[...])
    dk = q.shape[1] // PEER_HEADS
    for hd in range(PEER_HEADS):
        q_ref[hd] = q[:, hd * dk:(hd + 1) * dk]


def _peer_query(x1, norm_w, sc, sh, wq_bf, tokens_per_batch):
    t, d = x1.shape
    dk = wq_bf.shape[1] // PEER_HEADS
    tm = min(512, tokens_per_batch)
    tpb = tokens_per_batch // tm
    return pl.pallas_call(
        _peer_query_kernel,
        grid=(t // tm,),
        in_specs=[pl.BlockSpec((tm, d), lambda i: (i, 0)),
                  pl.BlockSpec((1, d), lambda i: (0, 0)),
                  pl.BlockSpec((1, 1, d), lambda i: (i // tpb, 0, 0)),
                  pl.BlockSpec((1, 1, d), lambda i: (i // tpb, 0, 0)),
                  pl.BlockSpec(wq_bf.shape, lambda i: (0, 0))],
        out_specs=[pl.BlockSpec((d, tm), lambda i: (0, i)),
                   pl.BlockSpec((PEER_HEADS, tm, dk), lambda i: (0, i, 0))],
        out_shape=[jax.ShapeDtypeStruct((d, t), BF16),
                   jax.ShapeDtypeStruct((PEER_HEADS, t, dk), F32)],
        compiler_params=_cparams(("parallel",)),
        name="peer_query",
    )(x1, norm_w, sc, sh, wq_bf)


def _gelu_exact(x):
    return 0.5 * x * (1.0 + lax.erf(x * 0.7071067811865476))


def _peer_route_act_kernel(ht_ref, q_ref, keys_ref, u_ref,
                           act_ref, pa_ref, len_ref, qb_ref, rnk_ref, code_scr):
    hd = pl.program_id(1)
    tm = ht_ref.shape[1]
    half = keys_ref.shape[-1]
    sub = min(ROUTE_SUB_TOKENS, tm)
    blocked = (PEER_NKEYS // PEER_SLABS, PEER_SLABS, sub)

    @pl.when(hd == 0)
    def _():
        code_scr[...] = jnp.zeros(code_scr.shape, F32)

    act_ref[...] = _dot(u_ref[...], ht_ref[...]).astype(BF16)

    def route(h, t0, tie_safe):
        q = q_ref[h, t0:t0 + sub, :]
        sa = _dot_nt(keys_ref[h, 0], q[:, :half], HIGHEST)
        sb = _dot_nt(keys_ref[h, 1], q[:, half:], HIGHEST)
        pa, length, qbt, rank_b, clean = _route_head(sa, sb, tie_safe)
        pa_ref[h, :, :, t0:t0 + sub] = pa.reshape(blocked)
        len_ref[h, :, :, t0:t0 + sub] = length.reshape(blocked)
        qb_ref[h, :, t0:t0 + sub] = qbt.astype(BF16)
        rnk_ref[h, :, t0:t0 + sub] = rank_b.astype(BF16)
        return clean

    dirty = jnp.zeros((1, 1), F32)
    for t0 in range(0, tm, sub):
        dirty = jnp.maximum(dirty, 1.0 - jnp.min(route(hd, t0, False), axis=1, keepdims=True))
    code_scr[0:1, 0:1] += dirty * lax.shift_left(1, hd).astype(F32)

    @pl.when(hd == pl.num_programs(1) - 1)
    def _():
        tie_bits = code_scr[0, 0].astype(jnp.int32)

        def redo(h, carry):
            @pl.when(((tie_bits >> h) & 1) == 1)
            def _():
                for t0 in range(0, tm, sub):
                    route(h, t0, True)
            return carry

        lax.fori_loop(0, PEER_HEADS, redo, 0)


def _peer_route_act(ht_bf, q_heads, sub_keys, u_bf, tokens_per_batch):
    d, t = ht_bf.shape
    ne = u_bf.shape[0]
    bu = ne // PEER_HEADS
    dk = q_heads.shape[-1]
    tm = min(512, tokens_per_batch)
    n_blocks = PEER_NKEYS // PEER_SLABS
    dense = jax.ShapeDtypeStruct((PEER_HEADS, PEER_NKEYS, t), BF16)
    dense_spec = pl.BlockSpec((PEER_HEADS, PEER_NKEYS, tm), lambda i, j: (0, 0, i))
    blocked = jax.ShapeDtypeStruct((PEER_HEADS, n_blocks, PEER_SLABS, t), F32)
    blocked_spec = pl.BlockSpec((PEER_HEADS, n_blocks, PEER_SLABS, tm), lambda i, j: (0, 0, 0, i))
    return pl.pallas_call(
        _peer_route_act_kernel,
        grid=(t // tm, PEER_HEADS),
        in_specs=[pl.BlockSpec((d, tm), lambda i, j: (0, i)),
                  pl.BlockSpec((PEER_HEADS, tm, dk), lambda i, j: (0, i, 0)),
                  pl.BlockSpec(sub_keys.shape, lambda i, j: (0, 0, 0, 0)),
                  pl.BlockSpec((bu, d), lambda i, j: (j, 0))],
        out_specs=[pl.BlockSpec((bu, tm), lambda i, j: (j, i)),
                   blocked_spec, blocked_spec, dense_spec, dense_spec],
        out_shape=[jax.ShapeDtypeStruct((ne, t), BF16), blocked, blocked, dense, dense],
        scratch_shapes=[pltpu.VMEM((8, LANES), F32)],
        compiler_params=_cparams(("parallel", "arbitrary")),
        name="peer_route_act",
    )(ht_bf, q_heads, sub_keys, u_bf)


def _peer_expert_kernel(act_ref, vt_ref, pa_ref, len_ref, qb_ref, rnk_ref, x_ref, g2_ref, o_ref, acc_scr):
    j = pl.program_id(1)
    be, tm = act_ref.shape

    @pl.when(j == 0)
    def _():
        acc_scr[...] = jnp.zeros(acc_scr.shape, F32)

    n_slabs = be // PEER_NKEYS
    rows = BF16_ROWS
    n_groups = PEER_NKEYS // rows

    def row_tile(ref, hd, s):
        return jnp.broadcast_to(ref[hd, 0, s:s + 1, :], (rows, tm)).astype(BF16)

    zero = jnp.zeros((rows, tm), BF16)
    parts = [[None] * n_groups for _ in range(n_slabs)]
    slab_group = 8
    for s0 in range(0, n_slabs, slab_group):
        ss = range(s0, min(s0 + slab_group, n_slabs))
        ln = {(hd, s): row_tile(len_ref, hd, s) for hd in range(PEER_HEADS) for s in ss}
        pa = {(hd, s): row_tile(pa_ref, hd, s) for hd in range(PEER_HEADS) for s in ss}
        for g in range(n_groups):
            sl = slice(g * rows, (g + 1) * rows)
            acc = {}
            for hd in range(PEER_HEADS):
                rk = rnk_ref[hd, sl, :]
                qv = qb_ref[hd, sl, :]
                for s in ss:
                    term = jnp.where(rk < ln[hd, s], qv, zero) * pa[hd, s]
                    acc[s] = term if s not in acc else acc[s] + term
            for s in ss:
                lo = s * PEER_NKEYS + g * rows
                parts[s][g] = acc[s] * _gelu_exact(act_ref[lo:lo + rows, :].astype(F32)).astype(BF16)
    p = jnp.concatenate([parts[s][g] for s in range(n_slabs) for g in range(n_groups)], axis=0)
    acc_scr[...] += _dot(vt_ref[0], p)

    @pl.when(j == pl.num_programs(1) - 1)
    def _():
        o_ref[...] = x_ref[...] + g2_ref[0] * acc_scr[...].T


def _peer_expert(act, vt_bf, pa, ln, qb, rnk, x1, g2, tokens_per_batch):
    t, d = x1.shape
    ne = act.shape[0]
    tm = min(512, tokens_per_batch)
    tpb = tokens_per_batch // tm
    be = PEER_EXPERT_BLOCK
    dense_spec = pl.BlockSpec((PEER_HEADS, PEER_NKEYS, tm), lambda i, j: (0, 0, i))
    blocked_spec = pl.BlockSpec((PEER_HEADS, 1, PEER_SLABS, tm), lambda i, j: (0, j, 0, i))
    return pl.pallas_call(
        _peer_expert_kernel,
        grid=(t // tm, ne // be),
        in_specs=[pl.BlockSpec((be, tm), lambda i, j: (j, i)),
                  pl.BlockSpec((1, d, be), lambda i, j: (j, 0, 0)),
                  blocked_spec, blocked_spec, dense_spec, dense_spec,
                  pl.BlockSpec((tm, d), lambda i, j: (i, 0)),
                  pl.BlockSpec((1, 1, d), lambda i, j: (i // tpb, 0, 0))],
        out_specs=pl.BlockSpec((tm, d), lambda i, j: (i, 0)),
        out_shape=jax.ShapeDtypeStruct((t, d), F32),
        scratch_shapes=[pltpu.VMEM((d, tm), F32)],
        compiler_params=_cparams(("parallel", "arbitrary")),
        name="peer_expert",
    )(act, vt_bf, pa, ln, qb, rnk, x1, g2)


def _pad_cols(a, n):
    return jnp.pad(a, ((0, 0), (0, n - a.shape[1])))


def _pad_rows(a, n):
    return jnp.pad(a, ((0, n - a.shape[0]), (0, 0)))


def _layer(x, c, w_ada, b_ada, norm_mix_w, w_in, fox_q_norm_w, fox_k_norm_w, fox_f_bias,
           rwkv_mu, rwkv_w0, rwkv_w_up, rwkv_a0, rwkv_a_up, rwkv_g_up, rwkv_k_k, rwkv_k_a,
           rwkv_r_k, rwkv_ln_w, rwkv_ln_b, w_out, norm_ffn_w, peer_w_query, peer_sub_keys,
           peer_u, peer_v):
    B, S, D = x.shape
    T = B * S
    fw = FOX_HEADS * FOX_HEAD_DIM
    rw = RWKV_HEADS * RWKV_HEAD_DIM
    w_lora = rwkv_w_up.shape[0]
    a_lora = rwkv_a_up.shape[0]
    g_lora = rwkv_g_up.shape[0]
    assert w_lora <= LANES and a_lora + FOX_HEADS <= LANES and g_lora == 256

    c_pad = _pad_rows(c, 8)
    mod = _ada_mod(c_pad, w_ada, b_ada)[:B]
    sh1, sc1, g1, sh2, sc2, g2 = [m.reshape(B, 1, D) for m in jnp.split(mod, 6, axis=-1)]

    fox_cols = 4 * fw + FOX_HEADS
    wi_fox, wi_rw = w_in[:, :fox_cols], w_in[:, fox_cols:]
    mu = rwkv_mu.reshape(1, -1)
    seg = lambda a, lo, n: a[:, lo:lo + n]
    w_perm = jnp.concatenate([
        seg(wi_fox, 0, 4 * fw),
        seg(wi_rw, 0, 3 * rw),
        seg(wi_rw, 3 * rw + w_lora + a_lora, g_lora),
        _pad_cols(seg(wi_rw, 3 * rw, w_lora), LANES),
        _pad_cols(jnp.concatenate([seg(wi_rw, 3 * rw + w_lora, a_lora), seg(wi_fox, 4 * fw, FOX_HEADS)], 1), LANES),
    ], axis=1).astype(BF16)
    cols = {"gate": 3 * fw, "rr": 4 * fw, "rk": 4 * fw + rw, "rv": 4 * fw + 2 * rw,
            "gd": 4 * fw + 3 * rw, "lo": 4 * fw + 3 * rw + g_lora}
    f_lane = a_lora
    f_block = (cols["lo"] + LANES) // LANES
    mus = [seg(mu, 0, rw), seg(mu, rw, rw), seg(mu, 2 * rw, rw),
           seg(mu, 3 * rw + w_lora + a_lora, g_lora),
           jnp.concatenate([_pad_cols(seg(mu, 3 * rw, w_lora), LANES),
                            _pad_cols(seg(mu, 3 * rw + w_lora, a_lora), LANES)], 1)]
    scale = FOX_HEAD_DIM ** -0.5 * LOG2_E
    head_w = _pad_cols(jnp.concatenate([jnp.tile(fox_q_norm_w * scale, FOX_HEADS),
                                        jnp.tile(fox_k_norm_w, FOX_HEADS)]).reshape(1, -1), w_perm.shape[1])

    x2 = x.reshape(T, D)
    proj = _in_proj(_norm_mod(x2, norm_mix_w.reshape(1, D), sc1, sh1, S), w_perm, head_w, 2 * fw)

    f_bias_row = jnp.zeros((1, LANES), F32).at[0, f_lane:f_lane + FOX_HEADS].set(fox_f_bias)
    cum = _fox_cum(proj, f_bias_row, B, S, f_block)
    cum_rows = cum.reshape(B, S, LANES)[:, :, f_lane:f_lane + FOX_HEADS].transpose(0, 2, 1).reshape(B, FOX_HEADS, 1, S)
    o_fox = _fox_attn(proj, cum_rows, B, S)

    row = lambda a: a.reshape(1, -1)
    r, lw, k, v, a_vec, b_vec, g, bonus = _rwkv_prep(
        proj, cols, mus, row(rwkv_w0), _pad_rows(rwkv_w_up, LANES), row(rwkv_a0), _pad_rows(rwkv_a_up, LANES),
        rwkv_g_up, row(rwkv_k_k), row(rwkv_k_a), row(rwkv_r_k), S)
    y = _rwkv_scan(r, lw, k, v, a_vec, b_vec, B, S)

    x1 = _mix_out(o_fox, proj, cols["gate"], y, bonus, g, x2, g1, row(rwkv_ln_w), row(rwkv_ln_b),
                  w_out.astype(BF16), S)

    ht, q_heads = _peer_query(x1, norm_ffn_w.reshape(1, D), sc2, sh2, peer_w_query.astype(BF16), S)
    act, pa, ln, qb, rnk = _peer_route_act(ht, q_heads, peer_sub_keys, peer_u.astype(BF16), S)
    vt = peer_v.astype(BF16).reshape(-1, PEER_EXPERT_BLOCK, D).transpose(0, 2, 1)
    out = _peer_expert(act, vt, pa, ln, qb, rnk, x1, g2, S)
    return out.reshape(B, S, D)


def kernel(x, c, w_ada, b_ada, norm_mix_w, w_in, fox_q_norm_w, fox_k_norm_w, fox_f_bias, rwkv_mu, rwkv_w0,
           rwkv_w_up, rwkv_a0, rwkv_a_up, rwkv_g_up, rwkv_k_k, rwkv_k_a, rwkv_r_k, rwkv_ln_w, rwkv_ln_b,
           w_out, norm_ffn_w, peer_w_query, peer_sub_keys, peer_u, peer_v):
    params = (w_ada, b_ada, norm_mix_w, w_in, fox_q_norm_w, fox_k_norm_w, fox_f_bias, rwkv_mu, rwkv_w0,
              rwkv_w_up, rwkv_a0, rwkv_a_up, rwkv_g_up, rwkv_k_k, rwkv_k_a, rwkv_r_k, rwkv_ln_w, rwkv_ln_b,
              w_out, norm_ffn_w, peer_w_query, peer_sub_keys, peer_u, peer_v)
    for l in range(w_ada.shape[0]):
        x = _layer(x, c, *[p[l] for p in params])
    return x
```

```python
import functools

import jax
import jax.numpy as jnp
from jax import lax
from jax.experimental import pallas as pl
from jax.experimental.pallas import tpu as pltpu

F32 = jnp.float32
BF16 = jnp.bfloat16
HIGHEST = lax.Precision.HIGHEST

LANES = 128
BF16_ROWS = 16
NORM_EPS = 1e-6
GN_EPS = 64e-5
CHUNK = 64
FOX_HEADS = 8
FOX_HEAD_DIM = 128
RWKV_HEADS = 16
RWKV_HEAD_DIM = 64
PEER_HEADS = 8
PEER_NKEYS = 128
PEER_TOPK = 16
PEER_EXPERT_BLOCK = 2048
PEER_SLABS = PEER_EXPERT_BLOCK // PEER_NKEYS
ROUTE_SUB_TOKENS = 256
NEG_BIG = -1e30
LOG2_E = 1.4426950408889634
VMEM_LIMIT = 56 * 1024 * 1024


def _cparams(sem):
    return pltpu.CompilerParams(dimension_semantics=sem, vmem_limit_bytes=VMEM_LIMIT)


def _dot(a, b, precision=None):
    return jnp.dot(a, b, preferred_element_type=F32, precision=precision)


def _dot_nt(a, b, precision=None):
    return lax.dot_general(a, b, (((1,), (1,)), ((), ())), preferred_element_type=F32,
                           precision=precision)


def _split_bf16(x):
    hi = x.astype(BF16)
    return hi, (x - hi.astype(F32)).astype(BF16)


def _dot_bf16x3(a, b):
    ah, al = _split_bf16(a)
    bh, bl = _split_bf16(b)
    return _dot(ah, bh) + (_dot(ah, bl) + _dot(al, bh))


def _dot_tn(a, b, precision=None):
    return lax.dot_general(a, b, (((0,), (0,)), ((), ())), preferred_element_type=F32,
                           precision=precision)


def _ada_kernel(c_ref, w_ref, b_ref, o_ref):
    c = c_ref[...]
    s = c * jax.nn.sigmoid(c)
    o_ref[...] = _dot(s, w_ref[...], HIGHEST) + b_ref[...]


def _ada_mod(c_pad, w_ada, b_ada):
    rows, d = c_pad.shape
    n = w_ada.shape[1]
    bn = 1024
    return pl.pallas_call(
        _ada_kernel,
        grid=(n // bn,),
        in_specs=[pl.BlockSpec((rows, d), lambda j: (0, 0)),
                  pl.BlockSpec((d, bn), lambda j: (0, j)),
                  pl.BlockSpec((1, bn), lambda j: (0, j))],
        out_specs=pl.BlockSpec((rows, bn), lambda j: (0, j)),
        out_shape=jax.ShapeDtypeStruct((rows, n), F32),
        compiler_params=_cparams(("arbitrary",)),
        name="ada_mod",
    )(c_pad, w_ada, b_ada.reshape(1, n))


def _modulated_norm(x, nw, sc, sh):
    y = x * lax.rsqrt(jnp.mean(x * x, axis=-1, keepdims=True) + NORM_EPS)
    return y * nw * (1.0 + sc) + sh


def _norm_mod_kernel(x_ref, nw_ref, sc_ref, sh_ref, o_ref):
    o_ref[...] = _modulated_norm(x_ref[...], nw_ref[...], sc_ref[0], sh_ref[0]).astype(BF16)


def _norm_mod(x2, norm_w, sc, sh, tokens_per_batch):
    t, d = x2.shape
    tm = min(512, tokens_per_batch)
    tpb = tokens_per_batch // tm
    return pl.pallas_call(
        _norm_mod_kernel,
        grid=(t // tm,),
        in_specs=[pl.BlockSpec((tm, d), lambda i: (i, 0)),
                  pl.BlockSpec((1, d), lambda i: (0, 0)),
                  pl.BlockSpec((1, 1, d), lambda i: (i // tpb, 0, 0)),
                  pl.BlockSpec((1, 1, d), lambda i: (i // tpb, 0, 0))],
        out_specs=pl.BlockSpec((tm, d), lambda i: (i, 0)),
        out_shape=jax.ShapeDtypeStruct((t, d), BF16),
        compiler_params=_cparams(("parallel",)),
        name="norm_mod",
    )(x2, norm_w, sc, sh)


def _in_proj_kernel(h_ref, w_ref, hw_ref, o_ref, *, n_qk_blocks):
    j = pl.program_id(1)
    acc = _dot(h_ref[...], w_ref[...])

    @pl.when(j < n_qk_blocks)
    def _():
        for hh in range(acc.shape[1] // FOX_HEAD_DIM):
            sl = slice(hh * FOX_HEAD_DIM, (hh + 1) * FOX_HEAD_DIM)
            a = acc[:, sl]
            rs = lax.rsqrt(jnp.mean(a * a, axis=-1, keepdims=True) + NORM_EPS)
            o_ref[:, sl] = a * rs * hw_ref[:, sl]

    @pl.when(j >= n_qk_blocks)
    def _():
        o_ref[...] = acc


def _in_proj(h_bf, w_bf, head_w, n_qk_cols):
    t, d = h_bf.shape
    n = w_bf.shape[1]
    tm = min(2048, t)
    bn = 512
    kern = functools.partial(_in_proj_kernel, n_qk_blocks=n_qk_cols // bn)
    return pl.pallas_call(
        kern,
        grid=(t // tm, n // bn),
        in_specs=[pl.BlockSpec((tm, d), lambda i, j: (i, 0)),
                  pl.BlockSpec((d, bn), lambda i, j: (0, j)),
                  pl.BlockSpec((1, bn), lambda i, j: (0, j))],
        out_specs=pl.BlockSpec((tm, bn), lambda i, j: (i, j)),
        out_shape=jax.ShapeDtypeStruct((t, n), F32),
        compiler_params=_cparams(("parallel", "arbitrary")),
        name="in_proj",
    )(h_bf, w_bf, head_w)


def _log_sigmoid(x):
    return jnp.minimum(x, 0.0) - jnp.log(1.0 + jnp.exp(-jnp.abs(x)))


def _fox_cum_kernel(f_ref, b_ref, o_ref, *, blk):
    s = f_ref.shape[0]
    row = lax.broadcasted_iota(jnp.int32, (blk, blk), 0)
    col = lax.broadcasted_iota(jnp.int32, (blk, blk), 1)
    tri = (row >= col).astype(F32)
    carry = jnp.zeros((1, f_ref.shape[1]), F32)
    for i in range(s // blk):
        lf = _log_sigmoid(f_ref[i * blk:(i + 1) * blk, :] + b_ref[...])
        cs = _dot(tri, lf, HIGHEST) + carry
        o_ref[i * blk:(i + 1) * blk, :] = cs * LOG2_E
        carry = cs[blk - 1:blk, :]


def _fox_cum(proj, f_bias_row, batch, seq, col_block):
    blk = min(256, seq)
    return pl.pallas_call(
        functools.partial(_fox_cum_kernel, blk=blk),
        grid=(batch,),
        in_specs=[pl.BlockSpec((seq, LANES), lambda b: (b, col_block)),
                  pl.BlockSpec((1, LANES), lambda b: (0, 0))],
        out_specs=pl.BlockSpec((seq, LANES), lambda b: (b, 0)),
        out_shape=jax.ShapeDtypeStruct((batch * seq, LANES), F32),
        compiler_params=_cparams(("parallel",)),
        name="fox_cum",
    )(proj, f_bias_row)


def _fox_attn_kernel(qt_ref, kt_ref, q_ref, k_ref, v_ref, c_ref, o_ref, m_scr, l_scr, acc_scr):
    t = pl.program_id(2)
    qi = qt_ref[t]
    ki = kt_ref[t]
    hd = FOX_HEAD_DIM
    n_heads = q_ref.shape[1] // hd

    @pl.when(ki == 0)
    def _():
        m_scr[...] = jnp.full(m_scr.shape, NEG_BIG, F32)
        l_scr[...] = jnp.zeros(l_scr.shape, F32)
        acc_scr[...] = jnp.zeros(acc_scr.shape, F32)

    def step(masked):
        hs = range(n_heads)
        sl = [slice(h * hd, (h + 1) * hd) for h in hs]
        s = [_dot_nt(q_ref[:, sl[h]].astype(BF16), k_ref[:, sl[h]].astype(BF16)) - c_ref[0, h] for h in hs]
        if masked:
            row = lax.broadcasted_iota(jnp.int32, s[0].shape, 0)
            col = lax.broadcasted_iota(jnp.int32, s[0].shape, 1)
            s = [jnp.where(row >= col, x, NEG_BIG) for x in s]
        m_prev = [m_scr[h] for h in hs]
        m_new = [jnp.maximum(m_prev[h], jnp.max(s[h], axis=-1, keepdims=True)) for h in hs]
        n_rep = s[0].shape[1] // hd
        p = [jnp.exp2(s[h] - jnp.concatenate([m_new[h]] * n_rep, axis=1)) for h in hs]
        alpha = [jnp.exp2(m_prev[h] - m_new[h]) for h in hs]
        ones = jnp.ones((k_ref.shape[0], hd), BF16)
        pv = [_dot(p[h].astype(BF16), jnp.concatenate([v_ref[:, sl[h]].astype(BF16), ones], axis=1))
              for h in hs]
        for h in hs:
            l_scr[h] = alpha[h] * l_scr[h] + pv[h][:, hd:]
            acc_scr[:, sl[h]] = alpha[h] * acc_scr[:, sl[h]] + pv[h][:, :hd]
            m_scr[h] = m_new[h]

    @pl.when(ki < qi)
    def _():
        step(False)

    @pl.when(ki == qi)
    def _():
        step(True)
        for h in range(n_heads):
            sl = slice(h * hd, (h + 1) * hd)
            o_ref[:, sl] = acc_scr[:, sl] / l_scr[h]


def _fox_attn(proj, cum_rows, batch, seq):
    tq = min(512, seq)
    nq = seq // tq
    hps = 4
    w = hps * FOX_HEAD_DIM
    kcol = FOX_HEADS // hps
    tri = [(q, k) for q in range(nq) for k in range(q + 1)]
    qt = jnp.asarray([q for q, _ in tri], jnp.int32)
    kt = jnp.asarray([k for _, k in tri], jnp.int32)
    grid_spec = pltpu.PrefetchScalarGridSpec(
        num_scalar_prefetch=2,
        grid=(batch, kcol, len(tri)),
        in_specs=[
            pl.BlockSpec((tq, w), lambda b, h, t, qt, kt: (b * nq + qt[t], h)),
            pl.BlockSpec((tq, w), lambda b, h, t, qt, kt: (b * nq + kt[t], kcol + h)),
            pl.BlockSpec((tq, w), lambda b, h, t, qt, kt: (b * nq + kt[t], 2 * kcol + h)),
            pl.BlockSpec((1, hps, 1, tq), lambda b, h, t, qt, kt: (b, h, 0, kt[t])),
        ],
        out_specs=pl.BlockSpec((tq, w), lambda b, h, t, qt, kt: (b * nq + qt[t], h)),
        scratch_shapes=[pltpu.VMEM((hps, tq, FOX_HEAD_DIM), F32), pltpu.VMEM((hps, tq, FOX_HEAD_DIM), F32),
                        pltpu.VMEM((tq, w), F32)],
    )
    return pl.pallas_call(
        _fox_attn_kernel,
        grid_spec=grid_spec,
        out_shape=jax.ShapeDtypeStruct((batch * seq, FOX_HEADS * FOX_HEAD_DIM), F32),
        compiler_params=_cparams(("parallel", "parallel", "arbitrary")),
        name="fox_attn",
    )(qt, kt, proj, proj, proj, cum_rows)


def _head_sum(x, bd):
    hi, lo = _split_bf16(x)
    parts = [_dot(hi[:, j * LANES:(j + 1) * LANES], bd) + _dot(lo[:, j * LANES:(j + 1) * LANES], bd)
             for j in range(x.shape[1] // LANES)]
    return jnp.concatenate(parts, axis=1)


def _head_block_diag():
    r = lax.broadcasted_iota(jnp.int32, (LANES, LANES), 0) // RWKV_HEAD_DIM
    c = lax.broadcasted_iota(jnp.int32, (LANES, LANES), 1) // RWKV_HEAD_DIM
    return (r == c).astype(BF16)


def _rwkv_prep_kernel(r_ref, k_ref, v_ref, gd_ref, lo_ref, pr_ref, pk_ref, pv_ref, pgd_ref, plo_ref,
                      mu_r, mu_k, mu_v, mu_gd, mu_lo, w0_ref, wup_ref, a0_ref, aup_ref, gup_ref,
                      kk_ref, ka_ref, rk_ref,
                      or_ref, olw_ref, ok_ref, ov_ref, oa_ref, ob_ref, og_ref, obonus_ref, *, tiles_per_batch):
    i = pl.program_id(0)
    first = (i % tiles_per_batch) == 0

    def shifted(cur_ref, prv_ref, mu_ref):
        cur = cur_ref[...]
        last = jnp.where(first, 0.0, prv_ref[7:8, :])
        row = lax.broadcasted_iota(jnp.int32, cur.shape, 0)
        prev = jnp.where(row == 0, last, pltpu.roll(cur, 1, 0))
        return cur + (prev - cur) * mu_ref[...]

    r = shifted(r_ref, pr_ref, mu_r)
    k = shifted(k_ref, pk_ref, mu_k)
    v = shifted(v_ref, pv_ref, mu_v)
    gd = shifted(gd_ref, pgd_ref, mu_gd)
    lo = shifted(lo_ref, plo_ref, mu_lo)
    wd = lo[:, :LANES]
    ad = lo[:, LANES:]

    w_pre = w0_ref[...] + _dot_bf16x3(jnp.tanh(wd), wup_ref[...])
    w_raw = _log_sigmoid(w_pre) - 0.5
    log_decay = -jnp.exp(w_raw)
    a = jax.nn.sigmoid(a0_ref[...] + _dot_bf16x3(ad, aup_ref[...]))
    g = _dot_bf16x3(jax.nn.sigmoid(gd), gup_ref[...])

    bd = _head_block_diag()
    kk = k * kk_ref[...]
    nrm = jnp.maximum(jnp.sqrt(_head_sum(kk * kk, bd)), 1e-12)
    kk = kk / nrm
    k_mod = k * (1.0 + (a - 1.0) * ka_ref[...])
    bonus = _head_sum(r * k_mod * rk_ref[...], bd) * v

    or_ref[...] = r.astype(BF16)
    olw_ref[...] = log_decay
    ok_ref[...] = k_mod.astype(BF16)
    ov_ref[...] = v.astype(BF16)
    oa_ref[...] = (-kk).astype(BF16)
    ob_ref[...] = (kk * a).astype(BF16)
    og_ref[...] = g.astype(BF16)
    obonus_ref[...] = bonus.astype(BF16)


def _rwkv_prep(proj, cols, mus, w0, w_up, a0, a_up, g_up, k_k, k_a, r_k, tokens_per_batch):
    t = proj.shape[0]
    w = RWKV_HEADS * RWKV_HEAD_DIM
    tm = min(512, tokens_per_batch)
    tpb = tokens_per_batch // tm
    widths = [w, w, w, 256, 256]
    offs = [cols["rr"], cols["rk"], cols["rv"], cols["gd"], cols["lo"]]
    cur_specs = [pl.BlockSpec((tm, wd), functools.partial(lambda i, cb: (i, cb), cb=o // wd))
                 for wd, o in zip(widths, offs)]
    prv_specs = [pl.BlockSpec((8, wd), functools.partial(
        lambda i, cb: (jnp.maximum(i * (tm // 8) - 1, 0), cb), cb=o // wd))
        for wd, o in zip(widths, offs)]
    full = lambda a: pl.BlockSpec(a.shape, lambda i: (0,) * a.ndim)
    params = list(mus) + [w0, w_up, a0, a_up, g_up, k_k, k_a, r_k]
    out_spec = pl.BlockSpec((tm, w), lambda i: (i, 0))
    return pl.pallas_call(
        functools.partial(_rwkv_prep_kernel, tiles_per_batch=tpb),
        grid=(t // tm,),
        in_specs=cur_specs + prv_specs + [full(p) for p in params],
        out_specs=[out_spec] * 8,
        out_shape=[jax.ShapeDtypeStruct((t, w), dt) for dt in (BF16, F32, BF16, BF16, BF16, BF16, BF16, BF16)],
        compiler_params=_cparams(("parallel",)),
        name="rwkv_prep",
    )(*([proj] * 10), *params)


def _rwkv_scan_kernel(r_ref, lw_ref, k_ref, v_ref, a_ref, b_ref, o_ref, h_scr):
    c = pl.program_id(1)

    @pl.when(c == 0)
    def _():
        h_scr[...] = jnp.zeros(h_scr.shape, F32)

    C = r_ref.shape[0]
    n_pairs = r_ref.shape[1] // LANES
    P = HIGHEST

    row = lax.broadcasted_iota(jnp.int32, (C, C), 0)
    col = lax.broadcasted_iota(jnp.int32, (C, C), 1)
    tri = (row >= col).astype(F32)
    lw = lw_ref[...]
    cw = _dot(tri, lw, P)
    cw_end = cw[C - 1:C, :]
    e_pos = jnp.exp(cw)
    e_prev = jnp.exp(cw - lw)
    e_neg = jnp.exp(-cw)
    e_end = jnp.exp(cw_end - cw)
    w_end = jnp.exp(cw_end)

    a = a_ref[...].astype(F32)
    b = b_ref[...].astype(F32)
    k = k_ref[...].astype(F32)
    r = r_ref[...].astype(F32)
    at = a * e_prev
    bt = b * e_neg
    kt = k * e_neg
    rt = r * e_pos
    bh = b * e_end
    kh = k * e_end
    v = v_ref[...].astype(F32)

    lane = lax.broadcasted_iota(jnp.int32, (C, LANES), 1)
    head0 = lane < RWKV_HEAD_DIM
    r2 = lax.broadcasted_iota(jnp.int32, (2 * C, 2 * C), 0)
    c2 = lax.broadcasted_iota(jnp.int32, (2 * C, 2 * C), 1)
    same = (r2 // C) == (c2 // C)
    strict = same & (r2 > c2)
    incl = same & (r2 >= c2)
    eye = (r2 == c2).astype(F32)

    def two(x, p):
        xp = x[:, p * LANES:(p + 1) * LANES]
        return jnp.concatenate([jnp.where(head0, xp, 0.0), jnp.where(head0, 0.0, xp)], axis=0)

    def mm(x, y):
        return _dot(x.astype(BF16), y.astype(BF16))

    def mm3(x, y):
        xh, xl = _split_bf16(x)
        yh, yl = _split_bf16(y)
        m, n = x.shape[0], y.shape[1]
        w = _dot(jnp.concatenate([xh, xl], axis=0), jnp.concatenate([yh, yl], axis=1))
        return (w[:m, :n] + w[:m, n:]) + (w[m:, :n] + w[m:, n:])

    G = 2 * C
    pairs = range(n_pairs)
    at2 = [two(at, p) for p in pairs]
    rt2 = [two(rt, p) for p in pairs]
    v2 = [two(v, p).astype(BF16) for p in pairs]
    gram = [_dot_nt(jnp.concatenate([at2[p], rt2[p]], axis=0).astype(BF16),
                    jnp.concatenate([two(bt, p), two(kt, p)], axis=0).astype(BF16)) for p in pairs]
    a_ab = [jnp.where(strict, gram[p][:G, :G], 0.0) for p in pairs]
    nmat = list(a_ab)
    pw = list(a_ab)
    for _ in range(max(1, (C - 1).bit_length() - 1)):
        pw = [mm3(pw[p], pw[p]) for p in pairs]
        nmat = [nmat[p] + pw[p] + mm3(nmat[p], pw[p]) for p in pairs]
    akv = [mm(jnp.where(strict, gram[p][:G, G:], 0.0), v2[p]) for p in pairs]
    rhs = [jnp.concatenate([at2[p], akv[p]], axis=1) for p in pairs]
    pq = [(rhs[p] + mm(nmat[p], rhs[p])).astype(BF16) for p in pairs]
    ry = [mm(jnp.where(incl, gram[p][G:, :G], 0.0), pq[p]) for p in pairs]
    mv = [mm(jnp.where(incl, gram[p][G:, G:], 0.0), v2[p]) for p in pairs]
    ge = [_dot_tn(two(bh, p).astype(BF16), pq[p]) for p in pairs]
    kv = [_dot_tn(two(kh, p).astype(BF16), v2[p]) for p in pairs]
    for p in pairs:
        rr = rt2[p] + ry[p][:, :LANES]
        gm = eye * w_end[:, p * LANES:(p + 1) * LANES] + ge[p][:, :LANES]
        yh = mm(jnp.concatenate([rr, gm], axis=0), h_scr[p])
        y2 = yh[:G] + ry[p][:, LANES:] + mv[p]
        h_scr[p] = yh[G:] + ge[p][:, LANES:] + kv[p]
        o_ref[:, p * LANES:(p + 1) * LANES] = y2[:C, :] + y2[C:, :]


def _rwkv_scan(r, lw, k, v, a, b, batch, seq):
    w = r.shape[1]
    nc = seq // CHUNK
    spec = pl.BlockSpec((CHUNK, w), lambda bi, ci: (bi * nc + ci, 0))
    return pl.pallas_call(
        _rwkv_scan_kernel,
        grid=(batch, nc),
        in_specs=[spec] * 6,
        out_specs=spec,
        out_shape=jax.ShapeDtypeStruct((batch * seq, w), F32),
        scratch_shapes=[pltpu.VMEM((w // LANES, LANES, LANES), F32)],
        compiler_params=_cparams(("parallel", "arbitrary")),
        name="rwkv_scan",
    )(r, lw, k, v, a, b)


def _mix_out_kernel(o_ref, gate_ref, y_ref, bonus_ref, g_ref, x_ref, g1_ref, lnw_ref, lnb_ref, w_ref, out_ref):
    fox = o_ref[...] * jax.nn.sigmoid(gate_ref[...])
    bd = _head_block_diag()
    y = y_ref[...]
    inv_n = 1.0 / RWKV_HEAD_DIM
    mean = _head_sum(y, bd) * inv_n
    d = y - mean
    var = _head_sum(d * d, bd) * inv_n
    yn = d * lax.rsqrt(var + GN_EPS) * lnw_ref[...] + lnb_ref[...]
    rw = (yn + bonus_ref[...].astype(F32)) * g_ref[...].astype(F32)
    wf = fox.shape[1]
    mix = _dot(fox.astype(BF16), w_ref[:wf, :]) + _dot(rw.astype(BF16), w_ref[wf:, :])
    out_ref[...] = x_ref[...] + g1_ref[0] * mix


def _mix_out(o_fox, proj, gate_col, y, bonus, g, x2, g1, ln_w, ln_b, w_out_bf, tokens_per_batch):
    t, d = x2.shape
    wf = o_fox.shape[1]
    wr = y.shape[1]
    tm = min(512, tokens_per_batch)
    tpb = tokens_per_batch // tm
    return pl.pallas_call(
        _mix_out_kernel,
        grid=(t // tm,),
        in_specs=[pl.BlockSpec((tm, wf), lambda i: (i, 0)),
                  pl.BlockSpec((tm, wf), lambda i: (i, gate_col // wf)),
                  pl.BlockSpec((tm, wr), lambda i: (i, 0)),
                  pl.BlockSpec((tm, wr), lambda i: (i, 0)),
                  pl.BlockSpec((tm, wr), lambda i: (i, 0)),
                  pl.BlockSpec((tm, d), lambda i: (i, 0)),
                  pl.BlockSpec((1, 1, d), lambda i: (i // tpb, 0, 0)),
                  pl.BlockSpec((1, wr), lambda i: (0, 0)),
                  pl.BlockSpec((1, wr), lambda i: (0, 0)),
                  pl.BlockSpec(w_out_bf.shape, lambda i: (0, 0))],
        out_specs=pl.BlockSpec((tm, d), lambda i: (i, 0)),
        out_shape=jax.ShapeDtypeStruct((t, d), F32),
        compiler_params=_cparams(("parallel",)),
        name="mix_out",
    )(o_fox, proj, y, bonus, g, x2, g1, ln_w, ln_b, w_out_bf)


def _top_k_mask_rows(s, k, tie_safe, want_rank=True):
    n = s.shape[0]
    s0 = s
    row = lax.broadcasted_iota(jnp.int32, s.shape, 0).astype(F32)
    rank = jnp.full(s.shape, float(k), F32) if want_rank else None
    vals = []
    for r in range(k):
        m = jnp.max(s, axis=0, keepdims=True)
        hit = s == m
        if tie_safe:
            pos = jnp.min(jnp.where(hit, row, float(n)), axis=0, keepdims=True)
            hit = row == pos
        vals.append(m)
        if want_rank:
            rank = jnp.where(hit, float(r), rank)
        s = jnp.where(hit, -jnp.inf, s)
    picked = (s == -jnp.inf) & (s0 != -jnp.inf)
    n_picked = jnp.sum(jnp.where(picked, 1.0, 0.0), axis=0, keepdims=True)
    return jnp.concatenate(vals, axis=0), rank, picked, jnp.where(n_picked == float(k), 1.0, 0.0)


def _candidate_rows():
    k = PEER_TOPK
    groups, valid = [], []
    for r0 in range(k // 2):
        n1 = k // (r0 + 1)
        for g in range(-(-n1 // 8)):
            groups.append((r0, g * 8))
            valid.append([g * 8 + i < n1 for i in range(8)])
    groups.append((None, k // 2))
    valid.append([True] * 8)
    return groups, valid


def _route_head(sa, sb, tie_safe):
    k = PEER_TOPK
    tm = sa.shape[1]
    groups, valid = _candidate_rows()
    sub8 = lax.broadcasted_iota(jnp.int32, (8, tm), 0)
    top_a, rank_a, _, clean_a = _top_k_mask_rows(sa, k, tie_safe)
    top_b, rank_b, _, clean_b = _top_k_mask_rows(sb, k, tie_safe)
    parts = []
    for (r0, off), ok in zip(groups, valid):
        if r0 is None:
            part = top_a[off:off + 8, :] + top_b[0:1, :]
        else:
            part = top_a[r0:r0 + 1, :] + top_b[off:off + 8, :]
        if not all(ok):
            part = jnp.where(sub8 < sum(ok), part, -jnp.inf)
        parts.append(part)
    cand = jnp.concatenate(parts, axis=0)
    _, _, sel, clean_c = _top_k_mask_rows(cand, k, tie_safe, want_rank=False)
    z = jnp.sum(jnp.where(sel, jnp.exp(cand - cand[0:1, :]), 0.0), axis=0, keepdims=True)
    self32 = jnp.where(sel, 1.0, 0.0)
    n_keys = sa.shape[0]
    rank3 = rank_a.astype(BF16).reshape(n_keys // BF16_ROWS, BF16_ROWS, tm)
    length3 = jnp.zeros(rank3.shape, BF16)

    def add_count(r0, count_row):
        tile = jnp.broadcast_to(count_row, (BF16_ROWS, tm)).astype(BF16)
        return length3 + jnp.where(rank3 == float(r0), tile[None], 0.0).astype(BF16)

    tail = None
    per_rank = {}
    for gi, (r0, off) in enumerate(groups):
        cnt = self32[gi * 8:(gi + 1) * 8, :]
        if r0 is None:
            tail = cnt
        else:
            c = jnp.sum(cnt, axis=0, keepdims=True)
            per_rank[r0] = c if r0 not in per_rank else per_rank[r0] + c
    for r in range(8):
        per_rank[k // 2 + r] = tail[r:r + 1, :]
    for r0, c in per_rank.items():
        length3 = add_count(r0, c)
    length = length3.reshape(n_keys, tm).astype(F32)
    pa = jnp.exp(sa - top_a[0:1, :])
    qb = jnp.exp(sb - top_b[0:1, :]) / z
    return pa, length, qb, rank_b, clean_a * clean_b * clean_c


def _peer_query_kernel(x_ref, nw_ref, sc_ref, sh_ref, wq_ref, ht_ref, q_ref):
    h = _modulated_norm(x_ref[...], nw_ref[...], sc_ref[0], sh_ref[0])
    ht_ref[...] = h.T.astype(BF16)
    q = _dot(h.astype(BF16), wq_ref[...])
    dk = q.shape[1] // PEER_HEADS
    for hd in range(PEER_HEADS):
        q_ref[hd] = q[:, hd * dk:(hd + 1) * dk]


def _peer_query(x1, norm_w, sc, sh, wq_bf, tokens_per_batch):
    t, d = x1.shape
    dk = wq_bf.shape[1] // PEER_HEADS
    tm = min(512, tokens_per_batch)
    tpb = tokens_per_batch // tm
    return pl.pallas_call(
        _peer_query_kernel,
        grid=(t // tm,),
        in_specs=[pl.BlockSpec((tm, d), lambda i: (i, 0)),
                  pl.BlockSpec((1, d), lambda i: (0, 0)),
                  pl.BlockSpec((1, 1, d), lambda i: (i // tpb, 0, 0)),
                  pl.BlockSpec((1, 1, d), lambda i: (i // tpb, 0, 0)),
                  pl.BlockSpec(wq_bf.shape, lambda i: (0, 0))],
        out_specs=[pl.BlockSpec((d, tm), lambda i: (0, i)),
                   pl.BlockSpec((PEER_HEADS, tm, dk), lambda i: (0, i, 0))],
        out_shape=[jax.ShapeDtypeStruct((d, t), BF16),
                   jax.ShapeDtypeStruct((PEER_HEADS, t, dk), F32)],
        compiler_params=_cparams(("parallel",)),
        name="peer_query",
    )(x1, norm_w, sc, sh, wq_bf)


def _gelu_exact(x):
    return 0.5 * x * (1.0 + lax.erf(x * 0.7071067811865476))


def _peer_route_act_kernel(ht_ref, q_ref, keys_ref, u_ref,
                           act_ref, pa_ref, len_ref, qb_ref, rnk_ref, code_scr):
    hd = pl.program_id(1)
    tm = ht_ref.shape[1]
    half = keys_ref.shape[-1]
    sub = min(ROUTE_SUB_TOKENS, tm)
    blocked = (PEER_NKEYS // PEER_SLABS, PEER_SLABS, sub)

    @pl.when(hd == 0)
    def _():
        code_scr[...] = jnp.zeros(code_scr.shape, F32)

    act_ref[...] = _dot(u_ref[...], ht_ref[...]).astype(BF16)

    def route(h, t0, tie_safe):
        q = q_ref[h, t0:t0 + sub, :]
        sa = _dot_nt(keys_ref[h, 0], q[:, :half], HIGHEST)
        sb = _dot_nt(keys_ref[h, 1], q[:, half:], HIGHEST)
        pa, length, qbt, rank_b, clean = _route_head(sa, sb, tie_safe)
        pa_ref[h, :, :, t0:t0 + sub] = pa.reshape(blocked)
        len_ref[h, :, :, t0:t0 + sub] = length.reshape(blocked)
        qb_ref[h, :, t0:t0 + sub] = qbt.astype(BF16)
        rnk_ref[h, :, t0:t0 + sub] = rank_b.astype(BF16)
        return clean

    dirty = jnp.zeros((1, 1), F32)
    for t0 in range(0, tm, sub):
        dirty = jnp.maximum(dirty, 1.0 - jnp.min(route(hd, t0, False), axis=1, keepdims=True))
    code_scr[0:1, 0:1] += dirty * lax.shift_left(1, hd).astype(F32)

    @pl.when(hd == pl.num_programs(1) - 1)
    def _():
        tie_bits = code_scr[0, 0].astype(jnp.int32)

        def redo(h, carry):
            @pl.when(((tie_bits >> h) & 1) == 1)
            def _():
                for t0 in range(0, tm, sub):
                    route(h, t0, True)
            return carry

        lax.fori_loop(0, PEER_HEADS, redo, 0)


def _peer_route_act(ht_bf, q_heads, sub_keys, u_bf, tokens_per_batch):
    d, t = ht_bf.shape
    ne = u_bf.shape[0]
    bu = ne // PEER_HEADS
    dk = q_heads.shape[-1]
    tm = min(512, tokens_per_batch)
    n_blocks = PEER_NKEYS // PEER_SLABS
    dense = jax.ShapeDtypeStruct((PEER_HEADS, PEER_NKEYS, t), BF16)
    dense_spec = pl.BlockSpec((PEER_HEADS, PEER_NKEYS, tm), lambda i, j: (0, 0, i))
    blocked = jax.ShapeDtypeStruct((PEER_HEADS, n_blocks, PEER_SLABS, t), F32)
    blocked_spec = pl.BlockSpec((PEER_HEADS, n_blocks, PEER_SLABS, tm), lambda i, j: (0, 0, 0, i))
    return pl.pallas_call(
        _peer_route_act_kernel,
        grid=(t // tm, PEER_HEADS),
        in_specs=[pl.BlockSpec((d, tm), lambda i, j: (0, i)),
                  pl.BlockSpec((PEER_HEADS, tm, dk), lambda i, j: (0, i, 0)),
                  pl.BlockSpec(sub_keys.shape, lambda i, j: (0, 0, 0, 0)),
                  pl.BlockSpec((bu, d), lambda i, j: (j, 0))],
        out_specs=[pl.BlockSpec((bu, tm), lambda i, j: (j, i)),
                   blocked_spec, blocked_spec, dense_spec, dense_spec],
        out_shape=[jax.ShapeDtypeStruct((ne, t), BF16), blocked, blocked, dense, dense],
        scratch_shapes=[pltpu.VMEM((8, LANES), F32)],
        compiler_params=_cparams(("parallel", "arbitrary")),
        name="peer_route_act",
    )(ht_bf, q_heads, sub_keys, u_bf)


def _peer_expert_kernel(act_ref, vt_ref, pa_ref, len_ref, qb_ref, rnk_ref, x_ref, g2_ref, o_ref, acc_scr):
    j = pl.program_id(1)
    be, tm = act_ref.shape

    @pl.when(j == 0)
    def _():
        acc_scr[...] = jnp.zeros(acc_scr.shape, F32)

    n_slabs = be // PEER_NKEYS
    rows = BF16_ROWS
    n_groups = PEER_NKEYS // rows

    def row_tile(ref, hd, s):
        return jnp.broadcast_to(ref[hd, 0, s:s + 1, :], (rows, tm)).astype(BF16)

    zero = jnp.zeros((rows, tm), BF16)
    parts = [[None] * n_groups for _ in range(n_slabs)]
    slab_group = 8
    for s0 in range(0, n_slabs, slab_group):
        ss = range(s0, min(s0 + slab_group, n_slabs))
        ln = {(hd, s): row_tile(len_ref, hd, s) for hd in range(PEER_HEADS) for s in ss}
        pa = {(hd, s): row_tile(pa_ref, hd, s) for hd in range(PEER_HEADS) for s in ss}
        for g in range(n_groups):
            sl = slice(g * rows, (g + 1) * rows)
            acc = {}
            for hd in range(PEER_HEADS):
                rk = rnk_ref[hd, sl, :]
                qv = qb_ref[hd, sl, :]
                for s in ss:
                    term = jnp.where(rk < ln[hd, s], qv, zero) * pa[hd, s]
                    acc[s] = term if s not in acc else acc[s] + term
            for s in ss:
                lo = s * PEER_NKEYS + g * rows
                parts[s][g] = acc[s] * _gelu_exact(act_ref[lo:lo + rows, :].astype(F32)).astype(BF16)
    p = jnp.concatenate([parts[s][g] for s in range(n_slabs) for g in range(n_groups)], axis=0)
    acc_scr[...] += _dot(vt_ref[0], p)

    @pl.when(j == pl.num_programs(1) - 1)
    def _():
        o_ref[...] = x_ref[...] + g2_ref[0] * acc_scr[...].T


def _peer_expert(act, vt_bf, pa, ln, qb, rnk, x1, g2, tokens_per_batch):
    t, d = x1.shape
    ne = act.shape[0]
    tm = min(512, tokens_per_batch)
    tpb = tokens_per_batch // tm
    be = PEER_EXPERT_BLOCK
    dense_spec = pl.BlockSpec((PEER_HEADS, PEER_NKEYS, tm), lambda i, j: (0, 0, i))
    blocked_spec = pl.BlockSpec((PEER_HEADS, 1, PEER_SLABS, tm), lambda i, j: (0, j, 0, i))
    return pl.pallas_call(
        _peer_expert_kernel,
        grid=(t // tm, ne // be),
        in_specs=[pl.BlockSpec((be, tm), lambda i, j: (j, i)),
                  pl.BlockSpec((1, d, be), lambda i, j: (j, 0, 0)),
                  blocked_spec, blocked_spec, dense_spec, dense_spec,
                  pl.BlockSpec((tm, d), lambda i, j: (i, 0)),
                  pl.BlockSpec((1, 1, d), lambda i, j: (i // tpb, 0, 0))],
        out_specs=pl.BlockSpec((tm, d), lambda i, j: (i, 0)),
        out_shape=jax.ShapeDtypeStruct((t, d), F32),
        scratch_shapes=[pltpu.VMEM((d, tm), F32)],
        compiler_params=_cparams(("parallel", "arbitrary")),
        name="peer_expert",
    )(act, vt_bf, pa, ln, qb, rnk, x1, g2)


def _pad_cols(a, n):
    return jnp.pad(a, ((0, 0), (0, n - a.shape[1])))


def _pad_rows(a, n):
    return jnp.pad(a, ((0, n - a.shape[0]), (0, 0)))


def _layer(x, c, w_ada, b_ada, norm_mix_w, w_in, fox_q_norm_w, fox_k_norm_w, fox_f_bias,
           rwkv_mu, rwkv_w0, rwkv_w_up, rwkv_a0, rwkv_a_up, rwkv_g_up, rwkv_k_k, rwkv_k_a,
           rwkv_r_k, rwkv_ln_w, rwkv_ln_b, w_out, norm_ffn_w, peer_w_query, peer_sub_keys,
           peer_u, peer_v):
    B, S, D = x.shape
    T = B * S
    fw = FOX_HEADS * FOX_HEAD_DIM
    rw = RWKV_HEADS * RWKV_HEAD_DIM
    w_lora = rwkv_w_up.shape[0]
    a_lora = rwkv_a_up.shape[0]
    g_lora = rwkv_g_up.shape[0]
    assert w_lora <= LANES and a_lora + FOX_HEADS <= LANES and g_lora == 256

    c_pad = _pad_rows(c, 8)
    mod = _ada_mod(c_pad, w_ada, b_ada)[:B]
    sh1, sc1, g1, sh2, sc2, g2 = [m.reshape(B, 1, D) for m in jnp.split(mod, 6, axis=-1)]

    fox_cols = 4 * fw + FOX_HEADS
    wi_fox, wi_rw = w_in[:, :fox_cols], w_in[:, fox_cols:]
    mu = rwkv_mu.reshape(1, -1)
    seg = lambda a, lo, n: a[:, lo:lo + n]
    w_perm = jnp.concatenate([
        seg(wi_fox, 0, 4 * fw),
        seg(wi_rw, 0, 3 * rw),
        seg(wi_rw, 3 * rw + w_lora + a_lora, g_lora),
        _pad_cols(seg(wi_rw, 3 * rw, w_lora), LANES),
        _pad_cols(jnp.concatenate([seg(wi_rw, 3 * rw + w_lora, a_lora), seg(wi_fox, 4 * fw, FOX_HEADS)], 1), LANES),
    ], axis=1).astype(BF16)
    cols = {"gate": 3 * fw, "rr": 4 * fw, "rk": 4 * fw + rw, "rv": 4 * fw + 2 * rw,
            "gd": 4 * fw + 3 * rw, "lo": 4 * fw + 3 * rw + g_lora}
    f_lane = a_lora
    f_block = (cols["lo"] + LANES) // LANES
    mus = [seg(mu, 0, rw), seg(mu, rw, rw), seg(mu, 2 * rw, rw),
           seg(mu, 3 * rw + w_lora + a_lora, g_lora),
           jnp.concatenate([_pad_cols(seg(mu, 3 * rw, w_lora), LANES),
                            _pad_cols(seg(mu, 3 * rw + w_lora, a_lora), LANES)], 1)]
    scale = FOX_HEAD_DIM ** -0.5 * LOG2_E
    head_w = _pad_cols(jnp.concatenate([jnp.tile(fox_q_norm_w * scale, FOX_HEADS),
                                        jnp.tile(fox_k_norm_w, FOX_HEADS)]).reshape(1, -1), w_perm.shape[1])

    x2 = x.reshape(T, D)
    proj = _in_proj(_norm_mod(x2, norm_mix_w.reshape(1, D), sc1, sh1, S), w_perm, head_w, 2 * fw)

    f_bias_row = jnp.zeros((1, LANES), F32).at[0, f_lane:f_lane + FOX_HEADS].set(fox_f_bias)
    cum = _fox_cum(proj, f_bias_row, B, S, f_block)
    cum_rows = cum.reshape(B, S, LANES)[:, :, f_lane:f_lane + FOX_HEADS].transpose(0, 2, 1).reshape(B, FOX_HEADS, 1, S)
    o_fox = _fox_attn(proj, cum_rows, B, S)

    row = lambda a: a.reshape(1, -1)
    r, lw, k, v, a_vec, b_vec, g, bonus = _rwkv_prep(
        proj, cols, mus, row(rwkv_w0), _pad_rows(rwkv_w_up, LANES), row(rwkv_a0), _pad_rows(rwkv_a_up, LANES),
        rwkv_g_up, row(rwkv_k_k), row(rwkv_k_a), row(rwkv_r_k), S)
    y = _rwkv_scan(r, lw, k, v, a_vec, b_vec, B, S)

    x1 = _mix_out(o_fox, proj, cols["gate"], y, bonus, g, x2, g1, row(rwkv_ln_w), row(rwkv_ln_b),
                  w_out.astype(BF16), S)

    ht, q_heads = _peer_query(x1, norm_ffn_w.reshape(1, D), sc2, sh2, peer_w_query.astype(BF16), S)
    act, pa, ln, qb, rnk = _peer_route_act(ht, q_heads, peer_sub_keys, peer_u.astype(BF16), S)
    vt = peer_v.astype(BF16).reshape(-1, PEER_EXPERT_BLOCK, D).transpose(0, 2, 1)
    out = _peer_expert(act, vt, pa, ln, qb, rnk, x1, g2, S)
    return out.reshape(B, S, D)


def kernel(x, c, w_ada, b_ada, norm_mix_w, w_in, fox_q_norm_w, fox_k_norm_w, fox_f_bias, rwkv_mu, rwkv_w0,
           rwkv_w_up, rwkv_a0, rwkv_a_up, rwkv_g_up, rwkv_k_k, rwkv_k_a, rwkv_r_k, rwkv_ln_w, rwkv_ln_b,
           w_out, norm_ffn_w, peer_w_query, peer_sub_keys, peer_u, peer_v):
    params = (w_ada, b_ada, norm_mix_w, w_in, fox_q_norm_w, fox_k_norm_w, fox_f_bias, rwkv_mu, rwkv_w0,
              rwkv_w_up, rwkv_a0, rwkv_a_up, rwkv_g_up, rwkv_k_k, rwkv_k_a, rwkv_r_k, rwkv_ln_w, rwkv_ln_b,
              w_out, norm_ffn_w, peer_w_query, peer_sub_keys, peer_u, peer_v)
    for l in range(w_ada.shape[0]):
        x = _layer(x, c, *[p[l] for p in params])
    return x
```

```python
import functools

import jax
import jax.numpy as jnp
from jax import lax
from jax.experimental import pallas as pl
from jax.experimental.pallas import tpu as pltpu

F32 = jnp.float32
BF16 = jnp.bfloat16
HIGHEST = lax.Precision.HIGHEST

LANES = 128
BF16_ROWS = 16
NORM_EPS = 1e-6
GN_EPS = 64e-5
CHUNK = 64
FOX_HEADS = 8
FOX_HEAD_DIM = 128
RWKV_HEADS = 16
RWKV_HEAD_DIM = 64
PEER_HEADS = 8
PEER_NKEYS = 128
PEER_TOPK = 16
PEER_EXPERT_BLOCK = 2048
PEER_SLABS = PEER_EXPERT_BLOCK // PEER_NKEYS
ROUTE_SUB_TOKENS = 256
NEG_BIG = -1e30
LOG2_E = 1.4426950408889634
VMEM_LIMIT = 56 * 1024 * 1024


def _cparams(sem):
    return pltpu.CompilerParams(dimension_semantics=sem, vmem_limit_bytes=VMEM_LIMIT)


def _dot(a, b, precision=None):
    return jnp.dot(a, b, preferred_element_type=F32, precision=precision)


def _dot_nt(a, b, precision=None):
    return lax.dot_general(a, b, (((1,), (1,)), ((), ())), preferred_element_type=F32,
                           precision=precision)


def _split_bf16(x):
    hi = x.astype(BF16)
    return hi, (x - hi.astype(F32)).astype(BF16)


def _dot_bf16x3(a, b):
    ah, al = _split_bf16(a)
    bh, bl = _split_bf16(b)
    return _dot(ah, bh) + (_dot(ah, bl) + _dot(al, bh))


def _dot_tn(a, b, precision=None):
    return lax.dot_general(a, b, (((0,), (0,)), ((), ())), preferred_element_type=F32,
                           precision=precision)


def _ada_kernel(c_ref, w_ref, b_ref, o_ref):
    c = c_ref[...]
    s = c * jax.nn.sigmoid(c)
    o_ref[...] = _dot(s, w_ref[...], HIGHEST) + b_ref[...]


def _ada_mod(c_pad, w_ada, b_ada):
    rows, d = c_pad.shape
    n = w_ada.shape[1]
    bn = 1024
    return pl.pallas_call(
        _ada_kernel,
        grid=(n // bn,),
        in_specs=[pl.BlockSpec((rows, d), lambda j: (0, 0)),
                  pl.BlockSpec((d, bn), lambda j: (0, j)),
                  pl.BlockSpec((1, bn), lambda j: (0, j))],
        out_specs=pl.BlockSpec((rows, bn), lambda j: (0, j)),
        out_shape=jax.ShapeDtypeStruct((rows, n), F32),
        compiler_params=_cparams(("arbitrary",)),
        name="ada_mod",
    )(c_pad, w_ada, b_ada.reshape(1, n))


def _modulated_norm(x, nw, sc, sh):
    y = x * lax.rsqrt(jnp.mean(x * x, axis=-1, keepdims=True) + NORM_EPS)
    return y * nw * (1.0 + sc) + sh


def _norm_mod_kernel(x_ref, nw_ref, sc_ref, sh_ref, o_ref):
    o_ref[...] = _modulated_norm(x_ref[...], nw_ref[...], sc_ref[0], sh_ref[0]).astype(BF16)


def _norm_mod(x2, norm_w, sc, sh, tokens_per_batch):
    t, d = x2.shape
    tm = min(512, tokens_per_batch)
    tpb = tokens_per_batch // tm
    return pl.pallas_call(
        _norm_mod_kernel,
        grid=(t // tm,),
        in_specs=[pl.BlockSpec((tm, d), lambda i: (i, 0)),
                  pl.BlockSpec((1, d), lambda i: (0, 0)),
                  pl.BlockSpec((1, 1, d), lambda i: (i // tpb, 0, 0)),
                  pl.BlockSpec((1, 1, d), lambda i: (i // tpb, 0, 0))],
        out_specs=pl.BlockSpec((tm, d), lambda i: (i, 0)),
        out_shape=jax.ShapeDtypeStruct((t, d), BF16),
        compiler_params=_cparams(("parallel",)),
        name="norm_mod",
    )(x2, norm_w, sc, sh)


def _in_proj_kernel(h_ref, w_ref, hw_ref, o_ref, *, n_qk_blocks):
    j = pl.program_id(1)
    acc = _dot(h_ref[...], w_ref[...])

    @pl.when(j < n_qk_blocks)
    def _():
        for hh in range(acc.shape[1] // FOX_HEAD_DIM):
            sl = slice(hh * FOX_HEAD_DIM, (hh + 1) * FOX_HEAD_DIM)
            a = acc[:, sl]
            rs = lax.rsqrt(jnp.mean(a * a, axis=-1, keepdims=True) + NORM_EPS)
            o_ref[:, sl] = a * rs * hw_ref[:, sl]

    @pl.when(j >= n_qk_blocks)
    def _():
        o_ref[...] = acc


def _in_proj(h_bf, w_bf, head_w, n_qk_cols):
    t, d = h_bf.shape
    n = w_bf.shape[1]
    tm = min(2048, t)
    bn = 512
    kern = functools.partial(_in_proj_kernel, n_qk_blocks=n_qk_cols // bn)
    return pl.pallas_call(
        kern,
        grid=(t // tm, n // bn),
        in_specs=[pl.BlockSpec((tm, d), lambda i, j: (i, 0)),
                  pl.BlockSpec((d, bn), lambda i, j: (0, j)),
                  pl.BlockSpec((1, bn), lambda i, j: (0, j))],
        out_specs=pl.BlockSpec((tm, bn), lambda i, j: (i, j)),
        out_shape=jax.ShapeDtypeStruct((t, n), F32),
        compiler_params=_cparams(("parallel", "arbitrary")),
        name="in_proj",
    )(h_bf, w_bf, head_w)


def _log_sigmoid(x):
    return jnp.minimum(x, 0.0) - jnp.log(1.0 + jnp.exp(-jnp.abs(x)))


def _fox_cum_kernel(f_ref, b_ref, o_ref, *, blk):
    s = f_ref.shape[0]
    row = lax.broadcasted_iota(jnp.int32, (blk, blk), 0)
    col = lax.broadcasted_iota(jnp.int32, (blk, blk), 1)
    tri = (row >= col).astype(F32)
    carry = jnp.zeros((1, f_ref.shape[1]), F32)
    for i in range(s // blk):
        lf = _log_sigmoid(f_ref[i * blk:(i + 1) * blk, :] + b_ref[...])
        cs = _dot(tri, lf, HIGHEST) + carry
        o_ref[i * blk:(i + 1) * blk, :] = cs * LOG2_E
        carry = cs[blk - 1:blk, :]


def _fox_cum(proj, f_bias_row, batch, seq, col_block):
    blk = min(256, seq)
    return pl.pallas_call(
        functools.partial(_fox_cum_kernel, blk=blk),
        grid=(batch,),
        in_specs=[pl.BlockSpec((seq, LANES), lambda b: (b, col_block)),
                  pl.BlockSpec((1, LANES), lambda b: (0, 0))],
        out_specs=pl.BlockSpec((seq, LANES), lambda b: (b, 0)),
        out_shape=jax.ShapeDtypeStruct((batch * seq, LANES), F32),
        compiler_params=_cparams(("parallel",)),
        name="fox_cum",
    )(proj, f_bias_row)


def _fox_attn_kernel(qt_ref, kt_ref, q_ref, k_ref, v_ref, c_ref, o_ref, m_scr, l_scr, acc_scr):
    t = pl.program_id(2)
    qi = qt_ref[t]
    ki = kt_ref[t]
    hd = FOX_HEAD_DIM
    n_heads = q_ref.shape[1] // hd

    @pl.when(ki == 0)
    def _():
        m_scr[...] = jnp.full(m_scr.shape, NEG_BIG, F32)
        l_scr[...] = jnp.zeros(l_scr.shape, F32)
        acc_scr[...] = jnp.zeros(acc_scr.shape, F32)

    def step(masked):
        hs = range(n_heads)
        sl = [slice(h * hd, (h + 1) * hd) for h in hs]
        s = [_dot_nt(q_ref[:, sl[h]].astype(BF16), k_ref[:, sl[h]].astype(BF16)) - c_ref[0, h] for h in hs]
        if masked:
            row = lax.broadcasted_iota(jnp.int32, s[0].shape, 0)
            col = lax.broadcasted_iota(jnp.int32, s[0].shape, 1)
            s = [jnp.where(row >= col, x, NEG_BIG) for x in s]
        m_prev = [m_scr[h] for h in hs]
        m_new = [jnp.maximum(m_prev[h], jnp.max(s[h], axis=-1, keepdims=True)) for h in hs]
        n_rep = s[0].shape[1] // hd
        p = [jnp.exp2(s[h] - jnp.concatenate([m_new[h]] * n_rep, axis=1)) for h in hs]
        alpha = [jnp.exp2(m_prev[h] - m_new[h]) for h in hs]
        ones = jnp.ones((k_ref.shape[0], hd), BF16)
        pv = [_dot(p[h].astype(BF16), jnp.concatenate([v_ref[:, sl[h]].astype(BF16), ones], axis=1))
              for h in hs]
        for h in hs:
            l_scr[h] = alpha[h] * l_scr[h] + pv[h][:, hd:]
            acc_scr[:, sl[h]] = alpha[h] * acc_scr[:, sl[h]] + pv[h][:, :hd]
            m_scr[h] = m_new[h]

    @pl.when(ki < qi)
    def _():
        step(False)

    @pl.when(ki == qi)
    def _():
        step(True)
        for h in range(n_heads):
            sl = slice(h * hd, (h + 1) * hd)
            o_ref[:, sl] = acc_scr[:, sl] / l_scr[h]


def _fox_attn(proj, cum_rows, batch, seq):
    tq = min(512, seq)
    nq = seq // tq
    hps = 8
    w = hps * FOX_HEAD_DIM
    kcol = FOX_HEADS // hps
    tri = [(q, k) for q in range(nq) for k in range(q + 1)]
    qt = jnp.asarray([q for q, _ in tri], jnp.int32)
    kt = jnp.asarray([k for _, k in tri], jnp.int32)
    grid_spec = pltpu.PrefetchScalarGridSpec(
        num_scalar_prefetch=2,
        grid=(batch, kcol, len(tri)),
        in_specs=[
            pl.BlockSpec((tq, w), lambda b, h, t, qt, kt: (b * nq + qt[t], h)),
            pl.BlockSpec((tq, w), lambda b, h, t, qt, kt: (b * nq + kt[t], kcol + h)),
            pl.BlockSpec((tq, w), lambda b, h, t, qt, kt: (b * nq + kt[t], 2 * kcol + h)),
            pl.BlockSpec((1, hps, 1, tq), lambda b, h, t, qt, kt: (b, h, 0, kt[t])),
        ],
        out_specs=pl.BlockSpec((tq, w), lambda b, h, t, qt, kt: (b * nq + qt[t], h)),
        scratch_shapes=[pltpu.VMEM((hps, tq, FOX_HEAD_DIM), F32), pltpu.VMEM((hps, tq, FOX_HEAD_DIM), F32),
                        pltpu.VMEM((tq, w), F32)],
    )
    return pl.pallas_call(
        _fox_attn_kernel,
        grid_spec=grid_spec,
        out_shape=jax.ShapeDtypeStruct((batch * seq, FOX_HEADS * FOX_HEAD_DIM), F32),
        compiler_params=_cparams(("parallel", "parallel", "arbitrary")),
        name="fox_attn",
    )(qt, kt, proj, proj, proj, cum_rows)


def _head_sum(x, bd):
    hi, lo = _split_bf16(x)
    parts = [_dot(hi[:, j * LANES:(j + 1) * LANES], bd) + _dot(lo[:, j * LANES:(j + 1) * LANES], bd)
             for j in range(x.shape[1] // LANES)]
    return jnp.concatenate(parts, axis=1)


def _head_block_diag():
    r = lax.broadcasted_iota(jnp.int32, (LANES, LANES), 0) // RWKV_HEAD_DIM
    c = lax.broadcasted_iota(jnp.int32, (LANES, LANES), 1) // RWKV_HEAD_DIM
    return (r == c).astype(BF16)


def _rwkv_prep_kernel(r_ref, k_ref, v_ref, gd_ref, lo_ref, pr_ref, pk_ref, pv_ref, pgd_ref, plo_ref,
                      mu_r, mu_k, mu_v, mu_gd, mu_lo, w0_ref, wup_ref, a0_ref, aup_ref, gup_ref,
                      kk_ref, ka_ref, rk_ref,
                      or_ref, olw_ref, ok_ref, ov_ref, oa_ref, ob_ref, og_ref, obonus_ref, *, tiles_per_batch):
    i = pl.program_id(0)
    first = (i % tiles_per_batch) == 0

    def shifted(cur_ref, prv_ref, mu_ref):
        cur = cur_ref[...]
        last = jnp.where(first, 0.0, prv_ref[7:8, :])
        row = lax.broadcasted_iota(jnp.int32, cur.shape, 0)
        prev = jnp.where(row == 0, last, pltpu.roll(cur, 1, 0))
        return cur + (prev - cur) * mu_ref[...]

    r = shifted(r_ref, pr_ref, mu_r)
    k = shifted(k_ref, pk_ref, mu_k)
    v = shifted(v_ref, pv_ref, mu_v)
    gd = shifted(gd_ref, pgd_ref, mu_gd)
    lo = shifted(lo_ref, plo_ref, mu_lo)
    wd = lo[:, :LANES]
    ad = lo[:, LANES:]

    w_pre = w0_ref[...] + _dot_bf16x3(jnp.tanh(wd), wup_ref[...])
    w_raw = _log_sigmoid(w_pre) - 0.5
    log_decay = -jnp.exp(w_raw)
    a = jax.nn.sigmoid(a0_ref[...] + _dot_bf16x3(ad, aup_ref[...]))
    g = _dot_bf16x3(jax.nn.sigmoid(gd), gup_ref[...])

    bd = _head_block_diag()
    kk = k * kk_ref[...]
    nrm = jnp.maximum(jnp.sqrt(_head_sum(kk * kk, bd)), 1e-12)
    kk = kk / nrm
    k_mod = k * (1.0 + (a - 1.0) * ka_ref[...])
    bonus = _head_sum(r * k_mod * rk_ref[...], bd) * v

    or_ref[...] = r.astype(BF16)
    olw_ref[...] = log_decay
    ok_ref[...] = k_mod.astype(BF16)
    ov_ref[...] = v.astype(BF16)
    oa_ref[...] = (-kk).astype(BF16)
    ob_ref[...] = (kk * a).astype(BF16)
    og_ref[...] = g.astype(BF16)
    obonus_ref[...] = bonus.astype(BF16)


def _rwkv_prep(proj, cols, mus, w0, w_up, a0, a_up, g_up, k_k, k_a, r_k, tokens_per_batch):
    t = proj.shape[0]
    w = RWKV_HEADS * RWKV_HEAD_DIM
    tm = min(512, tokens_per_batch)
    tpb = tokens_per_batch // tm
    widths = [w, w, w, 256, 256]
    offs = [cols["rr"], cols["rk"], cols["rv"], cols["gd"], cols["lo"]]
    cur_specs = [pl.BlockSpec((tm, wd), functools.partial(lambda i, cb: (i, cb), cb=o // wd))
                 for wd, o in zip(widths, offs)]
    prv_specs = [pl.BlockSpec((8, wd), functools.partial(
        lambda i, cb: (jnp.maximum(i * (tm // 8) - 1, 0), cb), cb=o // wd))
        for wd, o in zip(widths, offs)]
    full = lambda a: pl.BlockSpec(a.shape, lambda i: (0,) * a.ndim)
    params = list(mus) + [w0, w_up, a0, a_up, g_up, k_k, k_a, r_k]
    out_spec = pl.BlockSpec((tm, w), lambda i: (i, 0))
    return pl.pallas_call(
        functools.partial(_rwkv_prep_kernel, tiles_per_batch=tpb),
        grid=(t // tm,),
        in_specs=cur_specs + prv_specs + [full(p) for p in params],
        out_specs=[out_spec] * 8,
        out_shape=[jax.ShapeDtypeStruct((t, w), dt) for dt in (BF16, F32, BF16, BF16, BF16, BF16, BF16, BF16)],
        compiler_params=_cparams(("parallel",)),
        name="rwkv_prep",
    )(*([proj] * 10), *params)


def _rwkv_scan_kernel(r_ref, lw_ref, k_ref, v_ref, a_ref, b_ref, o_ref, h_scr):
    c = pl.program_id(1)

    @pl.when(c == 0)
    def _():
        h_scr[...] = jnp.zeros(h_scr.shape, F32)

    C = r_ref.shape[0]
    n_pairs = r_ref.shape[1] // LANES
    P = HIGHEST

    row = lax.broadcasted_iota(jnp.int32, (C, C), 0)
    col = lax.broadcasted_iota(jnp.int32, (C, C), 1)
    tri = (row >= col).astype(F32)
    lw = lw_ref[...]
    cw = _dot(tri, lw, P)
    cw_end = cw[C - 1:C, :]
    e_pos = jnp.exp(cw)
    e_prev = jnp.exp(cw - lw)
    e_neg = jnp.exp(-cw)
    e_end = jnp.exp(cw_end - cw)
    w_end = jnp.exp(cw_end)

    a = a_ref[...].astype(F32)
    b = b_ref[...].astype(F32)
    k = k_ref[...].astype(F32)
    r = r_ref[...].astype(F32)
    at = a * e_prev
    bt = b * e_neg
    kt = k * e_neg
    rt = r * e_pos
    bh = b * e_end
    kh = k * e_end
    v = v_ref[...].astype(F32)

    lane = lax.broadcasted_iota(jnp.int32, (C, LANES), 1)
    head0 = lane < RWKV_HEAD_DIM
    r2 = lax.broadcasted_iota(jnp.int32, (2 * C, 2 * C), 0)
    c2 = lax.broadcasted_iota(jnp.int32, (2 * C, 2 * C), 1)
    same = (r2 // C) == (c2 // C)
    strict = same & (r2 > c2)
    incl = same & (r2 >= c2)
    eye = (r2 == c2).astype(F32)

    def two(x, p):
        xp = x[:, p * LANES:(p + 1) * LANES]
        return jnp.concatenate([jnp.where(head0, xp, 0.0), jnp.where(head0, 0.0, xp)], axis=0)

    def mm(x, y):
        return _dot(x.astype(BF16), y.astype(BF16))

    def mm3(x, y):
        xh, xl = _split_bf16(x)
        yh, yl = _split_bf16(y)
        n = y.shape[1]
        wide = _dot(xh, jnp.concatenate([yh, yl], axis=1))
        return wide[:, :n] + (wide[:, n:] + _dot(xl, yh))

    G = 2 * C
    pairs = range(n_pairs)
    at2 = [two(at, p) for p in pairs]
    rt2 = [two(rt, p) for p in pairs]
    v2 = [two(v, p).astype(BF16) for p in pairs]
    gram = [_dot_nt(jnp.concatenate([at2[p], rt2[p]], axis=0).astype(BF16),
                    jnp.concatenate([two(bt, p), two(kt, p)], axis=0).astype(BF16)) for p in pairs]
    a_ab = [jnp.where(strict, gram[p][:G, :G], 0.0) for p in pairs]
    nmat = list(a_ab)
    pw = list(a_ab)
    for _ in range(max(1, (C - 1).bit_length() - 1)):
        pw = [mm3(pw[p], pw[p]) for p in pairs]
        nmat = [nmat[p] + pw[p] + mm3(nmat[p], pw[p]) for p in pairs]
    akv = [mm(jnp.where(strict, gram[p][:G, G:], 0.0), v2[p]) for p in pairs]
    rhs = [jnp.concatenate([at2[p], akv[p]], axis=1) for p in pairs]
    pq = [(rhs[p] + mm(nmat[p], rhs[p])).astype(BF16) for p in pairs]
    ry = [mm(jnp.where(incl, gram[p][G:, :G], 0.0), pq[p]) for p in pairs]
    mv = [mm(jnp.where(incl, gram[p][G:, G:], 0.0), v2[p]) for p in pairs]
    ge = [_dot_tn(two(bh, p).astype(BF16), pq[p]) for p in pairs]
    kv = [_dot_tn(two(kh, p).astype(BF16), v2[p]) for p in pairs]
    for p in pairs:
        rr = rt2[p] + ry[p][:, :LANES]
        gm = eye * w_end[:, p * LANES:(p + 1) * LANES] + ge[p][:, :LANES]
        yh = mm(jnp.concatenate([rr, gm], axis=0), h_scr[p])
        y2 = yh[:G] + ry[p][:, LANES:] + mv[p]
        h_scr[p] = yh[G:] + ge[p][:, LANES:] + kv[p]
        o_ref[:, p * LANES:(p + 1) * LANES] = y2[:C, :] + y2[C:, :]


def _rwkv_scan(r, lw, k, v, a, b, batch, seq):
    w = r.shape[1]
    nc = seq // CHUNK
    spec = pl.BlockSpec((CHUNK, w), lambda bi, ci: (bi * nc + ci, 0))
    return pl.pallas_call(
        _rwkv_scan_kernel,
        grid=(batch, nc),
        in_specs=[spec] * 6,
        out_specs=spec,
        out_shape=jax.ShapeDtypeStruct((batch * seq, w), F32),
        scratch_shapes=[pltpu.VMEM((w // LANES, LANES, LANES), F32)],
        compiler_params=_cparams(("parallel", "arbitrary")),
        name="rwkv_scan",
    )(r, lw, k, v, a, b)


def _mix_out_kernel(o_ref, gate_ref, y_ref, bonus_ref, g_ref, x_ref, g1_ref, lnw_ref, lnb_ref, w_ref, out_ref):
    fox = o_ref[...] * jax.nn.sigmoid(gate_ref[...])
    bd = _head_block_diag()
    y = y_ref[...]
    inv_n = 1.0 / RWKV_HEAD_DIM
    mean = _head_sum(y, bd) * inv_n
    d = y - mean
    var = _head_sum(d * d, bd) * inv_n
    yn = d * lax.rsqrt(var + GN_EPS) * lnw_ref[...] + lnb_ref[...]
    rw = (yn + bonus_ref[...].astype(F32)) * g_ref[...].astype(F32)
    wf = fox.shape[1]
    mix = _dot(fox.astype(BF16), w_ref[:wf, :]) + _dot(rw.astype(BF16), w_ref[wf:, :])
    out_ref[...] = x_ref[...] + g1_ref[0] * mix


def _mix_out(o_fox, proj, gate_col, y, bonus, g, x2, g1, ln_w, ln_b, w_out_bf, tokens_per_batch):
    t, d = x2.shape
    wf = o_fox.shape[1]
    wr = y.shape[1]
    tm = min(512, tokens_per_batch)
    tpb = tokens_per_batch // tm
    return pl.pallas_call(
        _mix_out_kernel,
        grid=(t // tm,),
        in_specs=[pl.BlockSpec((tm, wf), lambda i: (i, 0)),
                  pl.BlockSpec((tm, wf), lambda i: (i, gate_col // wf)),
                  pl.BlockSpec((tm, wr), lambda i: (i, 0)),
                  pl.BlockSpec((tm, wr), lambda i: (i, 0)),
                  pl.BlockSpec((tm, wr), lambda i: (i, 0)),
                  pl.BlockSpec((tm, d), lambda i: (i, 0)),
                  pl.BlockSpec((1, 1, d), lambda i: (i // tpb, 0, 0)),
                  pl.BlockSpec((1, wr), lambda i: (0, 0)),
                  pl.BlockSpec((1, wr), lambda i: (0, 0)),
                  pl.BlockSpec(w_out_bf.shape, lambda i: (0, 0))],
        out_specs=pl.BlockSpec((tm, d), lambda i: (i, 0)),
        out_shape=jax.ShapeDtypeStruct((t, d), F32),
        compiler_params=_cparams(("parallel",)),
        name="mix_out",
    )(o_fox, proj, y, bonus, g, x2, g1, ln_w, ln_b, w_out_bf)


def _top_k_mask_rows(s, k, tie_safe, want_rank=True):
    n = s.shape[0]
    s0 = s
    row = lax.broadcasted_iota(jnp.int32, s.shape, 0).astype(F32)
    rank = jnp.full(s.shape, float(k), F32) if want_rank else None
    vals = []
    for r in range(k):
        m = jnp.max(s, axis=0, keepdims=True)
        hit = s == m
        if tie_safe:
            pos = jnp.min(jnp.where(hit, row, float(n)), axis=0, keepdims=True)
            hit = row == pos
        vals.append(m)
        if want_rank:
            rank = jnp.where(hit, float(r), rank)
        s = jnp.where(hit, -jnp.inf, s)
    picked = (s == -jnp.inf) & (s0 != -jnp.inf)
    n_picked = jnp.sum(jnp.where(picked, 1.0, 0.0), axis=0, keepdims=True)
    return jnp.concatenate(vals, axis=0), rank, picked, jnp.where(n_picked == float(k), 1.0, 0.0)


def _candidate_rows():
    k = PEER_TOPK
    groups, valid = [], []
    for r0 in range(k // 2):
        n1 = k // (r0 + 1)
        for g in range(-(-n1 // 8)):
            groups.append((r0, g * 8))
            valid.append([g * 8 + i < n1 for i in range(8)])
    groups.append((None, k // 2))
    valid.append([True] * 8)
    return groups, valid


def _route_head(sa, sb, tie_safe):
    k = PEER_TOPK
    tm = sa.shape[1]
    groups, valid = _candidate_rows()
    sub8 = lax.broadcasted_iota(jnp.int32, (8, tm), 0)
    top_a, rank_a, _, clean_a = _top_k_mask_rows(sa, k, tie_safe)
    top_b, rank_b, _, clean_b = _top_k_mask_rows(sb, k, tie_safe)
    parts = []
    for (r0, off), ok in zip(groups, valid):
        if r0 is None:
            part = top_a[off:off + 8, :] + top_b[0:1, :]
        else:
            part = top_a[r0:r0 + 1, :] + top_b[off:off + 8, :]
        if not all(ok):
            part = jnp.where(sub8 < sum(ok), part, -jnp.inf)
        parts.append(part)
    cand = jnp.concatenate(parts, axis=0)
    _, _, sel, clean_c = _top_k_mask_rows(cand, k, tie_safe, want_rank=False)
    z = jnp.sum(jnp.where(sel, jnp.exp(cand - cand[0:1, :]), 0.0), axis=0, keepdims=True)
    self32 = jnp.where(sel, 1.0, 0.0)
    n_keys = sa.shape[0]
    rank3 = rank_a.astype(BF16).reshape(n_keys // BF16_ROWS, BF16_ROWS, tm)
    length3 = jnp.zeros(rank3.shape, BF16)

    def add_count(r0, count_row):
        tile = jnp.broadcast_to(count_row, (BF16_ROWS, tm)).astype(BF16)
        return length3 + jnp.where(rank3 == float(r0), tile[None], 0.0).astype(BF16)

    tail = None
    per_rank = {}
    for gi, (r0, off) in enumerate(groups):
        cnt = self32[gi * 8:(gi + 1) * 8, :]
        if r0 is None:
            tail = cnt
        else:
            c = jnp.sum(cnt, axis=0, keepdims=True)
            per_rank[r0] = c if r0 not in per_rank else per_rank[r0] + c
    for r in range(8):
        per_rank[k // 2 + r] = tail[r:r + 1, :]
    for r0, c in per_rank.items():
        length3 = add_count(r0, c)
    length = length3.reshape(n_keys, tm).astype(F32)
    pa = jnp.exp(sa - top_a[0:1, :])
    qb = jnp.exp(sb - top_b[0:1, :]) / z
    return pa, length, qb, rank_b, clean_a * clean_b * clean_c


def _peer_query_kernel(x_ref, nw_ref, sc_ref, sh_ref, wq_ref, ht_ref, q_ref):
    h = _modulated_norm(x_ref[...], nw_ref[...], sc_ref[0], sh_ref[0])
    ht_ref[...] = h.T.astype(BF16)
    q = _dot(h.astype(BF16), wq_ref[...])
    dk = q.shape[1] // PEER_HEADS
    for hd in range(PEER_HEADS):
        q_ref[hd] = q[:, hd * dk:(hd + 1) * dk]


def _peer_query(x1, norm_w, sc, sh, wq_bf, tokens_per_batch):
    t, d = x1.shape
    dk = wq_bf.shape[1] // PEER_HEADS
    tm = min(512, tokens_per_batch)
    tpb = tokens_per_batch // tm
    return pl.pallas_call(
        _peer_query_kernel,
        grid=(t // tm,),
        in_specs=[pl.BlockSpec((tm, d), lambda i: (i, 0)),
                  pl.BlockSpec((1, d), lambda i: (0, 0)),
                  pl.BlockSpec((1, 1, d), lambda i: (i // tpb, 0, 0)),
                  pl.BlockSpec((1, 1, d), lambda i: (i // tpb, 0, 0)),
                  pl.BlockSpec(wq_bf.shape, lambda i: (0, 0))],
        out_specs=[pl.BlockSpec((d, tm), lambda i: (0, i)),
                   pl.BlockSpec((PEER_HEADS, tm, dk), lambda i: (0, i, 0))],
        out_shape=[jax.ShapeDtypeStruct((d, t), BF16),
                   jax.ShapeDtypeStruct((PEER_HEADS, t, dk), F32)],
        compiler_params=_cparams(("parallel",)),
        name="peer_query",
    )(x1, norm_w, sc, sh, wq_bf)


def _gelu_exact(x):
    return 0.5 * x * (1.0 + lax.erf(x * 0.7071067811865476))


def _peer_route_act_kernel(ht_ref, q_ref, keys_ref, u_ref,
                           act_ref, pa_ref, len_ref, qb_ref, rnk_ref, code_scr):
    hd = pl.program_id(1)
    tm = ht_ref.shape[1]
    half = keys_ref.shape[-1]
    sub = min(ROUTE_SUB_TOKENS, tm)
    blocked = (PEER_NKEYS // PEER_SLABS, PEER_SLABS, sub)

    @pl.when(hd == 0)
    def _():
        code_scr[...] = jnp.zeros(code_scr.shape, F32)

    act_ref[...] = _dot(u_ref[...], ht_ref[...]).astype(BF16)

    def route(h, t0, tie_safe):
        q = q_ref[h, t0:t0 + sub, :]
        sa = _dot_nt(keys_ref[h, 0], q[:, :half], HIGHEST)
        sb = _dot_nt(keys_ref[h, 1], q[:, half:], HIGHEST)
        pa, length, qbt, rank_b, clean = _route_head(sa, sb, tie_safe)
        pa_ref[h, :, :, t0:t0 + sub] = pa.reshape(blocked)
        len_ref[h, :, :, t0:t0 + sub] = length.reshape(blocked)
        qb_ref[h, :, t0:t0 + sub] = qbt.astype(BF16)
        rnk_ref[h, :, t0:t0 + sub] = rank_b.astype(BF16)
        return clean

    dirty = jnp.zeros((1, 1), F32)
    for t0 in range(0, tm, sub):
        dirty = jnp.maximum(dirty, 1.0 - jnp.min(route(hd, t0, False), axis=1, keepdims=True))
    code_scr[0:1, 0:1] += dirty * lax.shift_left(1, hd).astype(F32)

    @pl.when(hd == pl.num_programs(1) - 1)
    def _():
        tie_bits = code_scr[0, 0].astype(jnp.int32)

        def redo(h, carry):
            @pl.when(((tie_bits >> h) & 1) == 1)
            def _():
                for t0 in range(0, tm, sub):
                    route(h, t0, True)
            return carry

        lax.fori_loop(0, PEER_HEADS, redo, 0)


def _peer_route_act(ht_bf, q_heads, sub_keys, u_bf, tokens_per_batch):
    d, t = ht_bf.shape
    ne = u_bf.shape[0]
    bu = ne // PEER_HEADS
    dk = q_heads.shape[-1]
    tm = min(512, tokens_per_batch)
    n_blocks = PEER_NKEYS // PEER_SLABS
    dense = jax.ShapeDtypeStruct((PEER_HEADS, PEER_NKEYS, t), BF16)
    dense_spec = pl.BlockSpec((PEER_HEADS, PEER_NKEYS, tm), lambda i, j: (0, 0, i))
    blocked = jax.ShapeDtypeStruct((PEER_HEADS, n_blocks, PEER_SLABS, t), F32)
    blocked_spec = pl.BlockSpec((PEER_HEADS, n_blocks, PEER_SLABS, tm), lambda i, j: (0, 0, 0, i))
    return pl.pallas_call(
        _peer_route_act_kernel,
        grid=(t // tm, PEER_HEADS),
        in_specs=[pl.BlockSpec((d, tm), lambda i, j: (0, i)),
                  pl.BlockSpec((PEER_HEADS, tm, dk), lambda i, j: (0, i, 0)),
                  pl.BlockSpec(sub_keys.shape, lambda i, j: (0, 0, 0, 0)),
                  pl.BlockSpec((bu, d), lambda i, j: (j, 0))],
        out_specs=[pl.BlockSpec((bu, tm), lambda i, j: (j, i)),
                   blocked_spec, blocked_spec, dense_spec, dense_spec],
        out_shape=[jax.ShapeDtypeStruct((ne, t), BF16), blocked, blocked, dense, dense],
        scratch_shapes=[pltpu.VMEM((8, LANES), F32)],
        compiler_params=_cparams(("parallel", "arbitrary")),
        name="peer_route_act",
    )(ht_bf, q_heads, sub_keys, u_bf)


def _peer_expert_kernel(act_ref, vt_ref, pa_ref, len_ref, qb_ref, rnk_ref, x_ref, g2_ref, o_ref, acc_scr):
    j = pl.program_id(1)
    be, tm = act_ref.shape

    @pl.when(j == 0)
    def _():
        acc_scr[...] = jnp.zeros(acc_scr.shape, F32)

    n_slabs = be // PEER_NKEYS
    rows = BF16_ROWS
    n_groups = PEER_NKEYS // rows

    def row_tile(ref, hd, s):
        return jnp.broadcast_to(ref[hd, 0, s:s + 1, :], (rows, tm)).astype(BF16)

    zero = jnp.zeros((rows, tm), BF16)
    parts = [[None] * n_groups for _ in range(n_slabs)]
    slab_group = 8
    for s0 in range(0, n_slabs, slab_group):
        ss = range(s0, min(s0 + slab_group, n_slabs))
        ln = {(hd, s): row_tile(len_ref, hd, s) for hd in range(PEER_HEADS) for s in ss}
        pa = {(hd, s): row_tile(pa_ref, hd, s) for hd in range(PEER_HEADS) for s in ss}
        for g in range(n_groups):
            sl = slice(g * rows, (g + 1) * rows)
            acc = {}
            for hd in range(PEER_HEADS):
                rk = rnk_ref[hd, sl, :]
                qv = qb_ref[hd, sl, :]
                for s in ss:
                    term = jnp.where(rk < ln[hd, s], qv, zero) * pa[hd, s]
                    acc[s] = term if s not in acc else acc[s] + term
            for s in ss:
                lo = s * PEER_NKEYS + g * rows
                parts[s][g] = acc[s] * _gelu_exact(act_ref[lo:lo + rows, :].astype(F32)).astype(BF16)
    p = jnp.concatenate([parts[s][g] for s in range(n_slabs) for g in range(n_groups)], axis=0)
    acc_scr[...] += _dot(vt_ref[0], p)

    @pl.when(j == pl.num_programs(1) - 1)
    def _():
        o_ref[...] = x_ref[...] + g2_ref[0] * acc_scr[...].T


def _peer_expert(act, vt_bf, pa, ln, qb, rnk, x1, g2, tokens_per_batch):
    t, d = x1.shape
    ne = act.shape[0]
    tm = min(512, tokens_per_batch)
    tpb = tokens_per_batch // tm
    be = PEER_EXPERT_BLOCK
    dense_spec = pl.BlockSpec((PEER_HEADS, PEER_NKEYS, tm), lambda i, j: (0, 0, i))
    blocked_spec = pl.BlockSpec((PEER_HEADS, 1, PEER_SLABS, tm), lambda i, j: (0, j, 0, i))
    return pl.pallas_call(
        _peer_expert_kernel,
        grid=(t // tm, ne // be),
        in_specs=[pl.BlockSpec((be, tm), lambda i, j: (j, i)),
                  pl.BlockSpec((1, d, be), lambda i, j: (j, 0, 0)),
                  blocked_spec, blocked_spec, dense_spec, dense_spec,
                  pl.BlockSpec((tm, d), lambda i, j: (i, 0)),
                  pl.BlockSpec((1, 1, d), lambda i, j: (i // tpb, 0, 0))],
        out_specs=pl.BlockSpec((tm, d), lambda i, j: (i, 0)),
        out_shape=jax.ShapeDtypeStruct((t, d), F32),
        scratch_shapes=[pltpu.VMEM((d, tm), F32)],
        compiler_params=_cparams(("parallel", "arbitrary")),
        name="peer_expert",
    )(act, vt_bf, pa, ln, qb, rnk, x1, g2)


def _pad_cols(a, n):
    return jnp.pad(a, ((0, 0), (0, n - a.shape[1])))


def _pad_rows(a, n):
    return jnp.pad(a, ((0, n - a.shape[0]), (0, 0)))


def _layer(x, c, w_ada, b_ada, norm_mix_w, w_in, fox_q_norm_w, fox_k_norm_w, fox_f_bias,
           rwkv_mu, rwkv_w0, rwkv_w_up, rwkv_a0, rwkv_a_up, rwkv_g_up, rwkv_k_k, rwkv_k_a,
           rwkv_r_k, rwkv_ln_w, rwkv_ln_b, w_out, norm_ffn_w, peer_w_query, peer_sub_keys,
           peer_u, peer_v):
    B, S, D = x.shape
    T = B * S
    fw = FOX_HEADS * FOX_HEAD_DIM
    rw = RWKV_HEADS * RWKV_HEAD_DIM
    w_lora = rwkv_w_up.shape[0]
    a_lora = rwkv_a_up.shape[0]
    g_lora = rwkv_g_up.shape[0]
    assert w_lora <= LANES and a_lora + FOX_HEADS <= LANES and g_lora == 256

    c_pad = _pad_rows(c, 8)
    mod = _ada_mod(c_pad, w_ada, b_ada)[:B]
    sh1, sc1, g1, sh2, sc2, g2 = [m.reshape(B, 1, D) for m in jnp.split(mod, 6, axis=-1)]

    fox_cols = 4 * fw + FOX_HEADS
    wi_fox, wi_rw = w_in[:, :fox_cols], w_in[:, fox_cols:]
    mu = rwkv_mu.reshape(1, -1)
    seg = lambda a, lo, n: a[:, lo:lo + n]
    w_perm = jnp.concatenate([
        seg(wi_fox, 0, 4 * fw),
        seg(wi_rw, 0, 3 * rw),
        seg(wi_rw, 3 * rw + w_lora + a_lora, g_lora),
        _pad_cols(seg(wi_rw, 3 * rw, w_lora), LANES),
        _pad_cols(jnp.concatenate([seg(wi_rw, 3 * rw + w_lora, a_lora), seg(wi_fox, 4 * fw, FOX_HEADS)], 1), LANES),
    ], axis=1).astype(BF16)
    cols = {"gate": 3 * fw, "rr": 4 * fw, "rk": 4 * fw + rw, "rv": 4 * fw + 2 * rw,
            "gd": 4 * fw + 3 * rw, "lo": 4 * fw + 3 * rw + g_lora}
    f_lane = a_lora
    f_block = (cols["lo"] + LANES) // LANES
    mus = [seg(mu, 0, rw), seg(mu, rw, rw), seg(mu, 2 * rw, rw),
           seg(mu, 3 * rw + w_lora + a_lora, g_lora),
           jnp.concatenate([_pad_cols(seg(mu, 3 * rw, w_lora), LANES),
                            _pad_cols(seg(mu, 3 * rw + w_lora, a_lora), LANES)], 1)]
    scale = FOX_HEAD_DIM ** -0.5 * LOG2_E
    head_w = _pad_cols(jnp.concatenate([jnp.tile(fox_q_norm_w * scale, FOX_HEADS),
                                        jnp.tile(fox_k_norm_w, FOX_HEADS)]).reshape(1, -1), w_perm.shape[1])

    x2 = x.reshape(T, D)
    proj = _in_proj(_norm_mod(x2, norm_mix_w.reshape(1, D), sc1, sh1, S), w_perm, head_w, 2 * fw)

    f_bias_row = jnp.zeros((1, LANES), F32).at[0, f_lane:f_lane + FOX_HEADS].set(fox_f_bias)
    cum = _fox_cum(proj, f_bias_row, B, S, f_block)
    cum_rows = cum.reshape(B, S, LANES)[:, :, f_lane:f_lane + FOX_HEADS].transpose(0, 2, 1).reshape(B, FOX_HEADS, 1, S)
    o_fox = _fox_attn(proj, cum_rows, B, S)

    row = lambda a: a.reshape(1, -1)
    r, lw, k, v, a_vec, b_vec, g, bonus = _rwkv_prep(
        proj, cols, mus, row(rwkv_w0), _pad_rows(rwkv_w_up, LANES), row(rwkv_a0), _pad_rows(rwkv_a_up, LANES),
        rwkv_g_up, row(rwkv_k_k), row(rwkv_k_a), row(rwkv_r_k), S)
    y = _rwkv_scan(r, lw, k, v, a_vec, b_vec, B, S)

    x1 = _mix_out(o_fox, proj, cols["gate"], y, bonus, g, x2, g1, row(rwkv_ln_w), row(rwkv_ln_b),
                  w_out.astype(BF16), S)

    ht, q_heads = _peer_query(x1, norm_ffn_w.reshape(1, D), sc2, sh2, peer_w_query.astype(BF16), S)
    act, pa, ln, qb, rnk = _peer_route_act(ht, q_heads, peer_sub_keys, peer_u.astype(BF16), S)
    vt = peer_v.reshape(-1, PEER_EXPERT_BLOCK, D).transpose(0, 2, 1).astype(BF16)
    out = _peer_expert(act, vt, pa, ln, qb, rnk, x1, g2, S)
    return out.reshape(B, S, D)


def kernel(x, c, w_ada, b_ada, norm_mix_w, w_in, fox_q_norm_w, fox_k_norm_w, fox_f_bias, rwkv_mu, rwkv_w0,
           rwkv_w_up, rwkv_a0, rwkv_a_up, rwkv_g_up, rwkv_k_k, rwkv_k_a, rwkv_r_k, rwkv_ln_w, rwkv_ln_b,
           w_out, norm_ffn_w, peer_w_query, peer_sub_keys, peer_u, peer_v):
    params = (w_ada, b_ada, norm_mix_w, w_in, fox_q_norm_w, fox_k_norm_w, fox_f_bias, rwkv_mu, rwkv_w0,
              rwkv_w_up, rwkv_a0, rwkv_a_up, rwkv_g_up, rwkv_k_k, rwkv_k_a, rwkv_r_k, rwkv_ln_w, rwkv_ln_b,
              w_out, norm_ffn_w, peer_w_query, peer_sub_keys, peer_u, peer_v)
    for l in range(w_ada.shape[0]):
        x = _layer(x, c, *[p[l] for p in params])
    return x
```

```python
import functools

import jax
import jax.numpy as jnp
from jax import lax
from jax.experimental import pallas as pl
from jax.experimental.pallas import tpu as pltpu

F32 = jnp.float32
BF16 = jnp.bfloat16
HIGHEST = lax.Precision.HIGHEST

LANES = 128
BF16_ROWS = 16
NORM_EPS = 1e-6
GN_EPS = 64e-5
CHUNK = 64
FOX_HEADS = 8
FOX_HEAD_DIM = 128
RWKV_HEADS = 16
RWKV_HEAD_DIM = 64
PEER_HEADS = 8
PEER_NKEYS = 128
PEER_TOPK = 16
PEER_EXPERT_BLOCK = 2048
PEER_SLABS = PEER_EXPERT_BLOCK // PEER_NKEYS
ROUTE_SUB_TOKENS = 256
NEG_BIG = -1e30
LOG2_E = 1.4426950408889634
VMEM_LIMIT = 56 * 1024 * 1024


def _cparams(sem):
    return pltpu.CompilerParams(dimension_semantics=sem, vmem_limit_bytes=VMEM_LIMIT)


def _dot(a, b, precision=None):
    return jnp.dot(a, b, preferred_element_type=F32, precision=precision)


def _dot_nt(a, b, precision=None):
    return lax.dot_general(a, b, (((1,), (1,)), ((), ())), preferred_element_type=F32,
                           precision=precision)


def _split_bf16(x):
    hi = x.astype(BF16)
    return hi, (x - hi.astype(F32)).astype(BF16)


def _dot_bf16x3(a, b):
    ah, al = _split_bf16(a)
    bh, bl = _split_bf16(b)
    return _dot(ah, bh) + (_dot(ah, bl) + _dot(al, bh))


def _dot_tn(a, b, precision=None):
    return lax.dot_general(a, b, (((0,), (0,)), ((), ())), preferred_element_type=F32,
                           precision=precision)


def _ada_kernel(c_ref, w_ref, b_ref, o_ref):
    c = c_ref[...]
    s = c * jax.nn.sigmoid(c)
    o_ref[...] = _dot(s, w_ref[...], HIGHEST) + b_ref[...]


def _ada_mod(c_pad, w_ada, b_ada):
    rows, d = c_pad.shape
    n = w_ada.shape[1]
    bn = 1024
    return pl.pallas_call(
        _ada_kernel,
        grid=(n // bn,),
        in_specs=[pl.BlockSpec((rows, d), lambda j: (0, 0)),
                  pl.BlockSpec((d, bn), lambda j: (0, j)),
                  pl.BlockSpec((1, bn), lambda j: (0, j))],
        out_specs=pl.BlockSpec((rows, bn), lambda j: (0, j)),
        out_shape=jax.ShapeDtypeStruct((rows, n), F32),
        compiler_params=_cparams(("arbitrary",)),
        name="ada_mod",
    )(c_pad, w_ada, b_ada.reshape(1, n))


def _modulated_norm(x, nw, sc, sh):
    y = x * lax.rsqrt(jnp.mean(x * x, axis=-1, keepdims=True) + NORM_EPS)
    return y * nw * (1.0 + sc) + sh


def _norm_mod_kernel(x_ref, nw_ref, sc_ref, sh_ref, o_ref):
    o_ref[...] = _modulated_norm(x_ref[...], nw_ref[...], sc_ref[0], sh_ref[0]).astype(BF16)


def _norm_mod(x2, norm_w, sc, sh, tokens_per_batch):
    t, d = x2.shape
    tm = min(512, tokens_per_batch)
    tpb = tokens_per_batch // tm
    return pl.pallas_call(
        _norm_mod_kernel,
        grid=(t // tm,),
        in_specs=[pl.BlockSpec((tm, d), lambda i: (i, 0)),
                  pl.BlockSpec((1, d), lambda i: (0, 0)),
                  pl.BlockSpec((1, 1, d), lambda i: (i // tpb, 0, 0)),
                  pl.BlockSpec((1, 1, d), lambda i: (i // tpb, 0, 0))],
        out_specs=pl.BlockSpec((tm, d), lambda i: (i, 0)),
        out_shape=jax.ShapeDtypeStruct((t, d), BF16),
        compiler_params=_cparams(("parallel",)),
        name="norm_mod",
    )(x2, norm_w, sc, sh)


def _in_proj_kernel(h_ref, w_ref, hw_ref, o_ref, *, n_qk_blocks):
    j = pl.program_id(1)
    acc = _dot(h_ref[...], w_ref[...])

    @pl.when(j < n_qk_blocks)
    def _():
        for hh in range(acc.shape[1] // FOX_HEAD_DIM):
            sl = slice(hh * FOX_HEAD_DIM, (hh + 1) * FOX_HEAD_DIM)
            a = acc[:, sl]
            rs = lax.rsqrt(jnp.mean(a * a, axis=-1, keepdims=True) + NORM_EPS)
            o_ref[:, sl] = a * rs * hw_ref[:, sl]

    @pl.when(j >= n_qk_blocks)
    def _():
        o_ref[...] = acc


def _in_proj(h_bf, w_bf, head_w, n_qk_cols):
    t, d = h_bf.shape
    n = w_bf.shape[1]
    tm = min(2048, t)
    bn = 512
    kern = functools.partial(_in_proj_kernel, n_qk_blocks=n_qk_cols // bn)
    return pl.pallas_call(
        kern,
        grid=(t // tm, n // bn),
        in_specs=[pl.BlockSpec((tm, d), lambda i, j: (i, 0)),
                  pl.BlockSpec((d, bn), lambda i, j: (0, j)),
                  pl.BlockSpec((1, bn), lambda i, j: (0, j))],
        out_specs=pl.BlockSpec((tm, bn), lambda i, j: (i, j)),
        out_shape=jax.ShapeDtypeStruct((t, n), F32),
        compiler_params=_cparams(("parallel", "arbitrary")),
        name="in_proj",
    )(h_bf, w_bf, head_w)


def _log_sigmoid(x):
    return jnp.minimum(x, 0.0) - jnp.log(1.0 + jnp.exp(-jnp.abs(x)))


def _fox_cum_kernel(f_ref, b_ref, o_ref, *, blk):
    s = f_ref.shape[0]
    row = lax.broadcasted_iota(jnp.int32, (blk, blk), 0)
    col = lax.broadcasted_iota(jnp.int32, (blk, blk), 1)
    tri = (row >= col).astype(F32)
    carry = jnp.zeros((1, f_ref.shape[1]), F32)
    for i in range(s // blk):
        lf = _log_sigmoid(f_ref[i * blk:(i + 1) * blk, :] + b_ref[...])
        cs = _dot(tri, lf, HIGHEST) + carry
        o_ref[i * blk:(i + 1) * blk, :] = cs * LOG2_E
        carry = cs[blk - 1:blk, :]


def _fox_cum(proj, f_bias_row, batch, seq, col_block):
    blk = min(256, seq)
    return pl.pallas_call(
        functools.partial(_fox_cum_kernel, blk=blk),
        grid=(batch,),
        in_specs=[pl.BlockSpec((seq, LANES), lambda b: (b, col_block)),
                  pl.BlockSpec((1, LANES), lambda b: (0, 0))],
        out_specs=pl.BlockSpec((seq, LANES), lambda b: (b, 0)),
        out_shape=jax.ShapeDtypeStruct((batch * seq, LANES), F32),
        compiler_params=_cparams(("parallel",)),
        name="fox_cum",
    )(proj, f_bias_row)


def _fox_attn_kernel(qt_ref, kt_ref, q_ref, k_ref, v_ref, c_ref, o_ref, m_scr, l_scr, acc_scr):
    t = pl.program_id(2)
    qi = qt_ref[t]
    ki = kt_ref[t]
    hd = FOX_HEAD_DIM
    n_heads = q_ref.shape[1] // hd

    @pl.when(ki == 0)
    def _():
        m_scr[...] = jnp.full(m_scr.shape, NEG_BIG, F32)
        l_scr[...] = jnp.zeros(l_scr.shape, F32)
        acc_scr[...] = jnp.zeros(acc_scr.shape, F32)

    def step(masked):
        hs = range(n_heads)
        sl = [slice(h * hd, (h + 1) * hd) for h in hs]
        s = [_dot_nt(q_ref[:, sl[h]].astype(BF16), k_ref[:, sl[h]].astype(BF16)) - c_ref[0, h] for h in hs]
        if masked:
            row = lax.broadcasted_iota(jnp.int32, s[0].shape, 0)
            col = lax.broadcasted_iota(jnp.int32, s[0].shape, 1)
            s = [jnp.where(row >= col, x, NEG_BIG) for x in s]
        m_prev = [m_scr[h] for h in hs]
        m_new = [jnp.maximum(m_prev[h], jnp.max(s[h], axis=-1, keepdims=True)) for h in hs]
        n_rep = s[0].shape[1] // hd
        p = [jnp.exp2(s[h] - jnp.concatenate([m_new[h]] * n_rep, axis=1)) for h in hs]
        alpha = [jnp.exp2(m_prev[h] - m_new[h]) for h in hs]
        ones = jnp.ones((k_ref.shape[0], hd), BF16)
        pv = [_dot(p[h].astype(BF16), jnp.concatenate([v_ref[:, sl[h]].astype(BF16), ones], axis=1))
              for h in hs]
        for h in hs:
            l_scr[h] = alpha[h] * l_scr[h] + pv[h][:, hd:]
            acc_scr[:, sl[h]] = alpha[h] * acc_scr[:, sl[h]] + pv[h][:, :hd]
            m_scr[h] = m_new[h]

    @pl.when(ki < qi)
    def _():
        step(False)

    @pl.when(ki == qi)
    def _():
        step(True)
        for h in range(n_heads):
            sl = slice(h * hd, (h + 1) * hd)
            o_ref[:, sl] = acc_scr[:, sl] / l_scr[h]


def _fox_attn(proj, cum_rows, batch, seq):
    tq = min(512, seq)
    nq = seq // tq
    hps = 8
    w = hps * FOX_HEAD_DIM
    kcol = FOX_HEADS // hps
    tri = [(q, k) for q in range(nq) for k in range(q + 1)]
    qt = jnp.asarray([q for q, _ in tri], jnp.int32)
    kt = jnp.asarray([k for _, k in tri], jnp.int32)
    grid_spec = pltpu.PrefetchScalarGridSpec(
        num_scalar_prefetch=2,
        grid=(batch, kcol, len(tri)),
        in_specs=[
            pl.BlockSpec((tq, w), lambda b, h, t, qt, kt: (b * nq + qt[t], h)),
            pl.BlockSpec((tq, w), lambda b, h, t, qt, kt: (b * nq + kt[t], kcol + h)),
            pl.BlockSpec((tq, w), lambda b, h, t, qt, kt: (b * nq + kt[t], 2 * kcol + h)),
            pl.BlockSpec((1, hps, 1, tq), lambda b, h, t, qt, kt: (b, h, 0, kt[t])),
        ],
        out_specs=pl.BlockSpec((tq, w), lambda b, h, t, qt, kt: (b * nq + qt[t], h)),
        scratch_shapes=[pltpu.VMEM((hps, tq, FOX_HEAD_DIM), F32), pltpu.VMEM((hps, tq, FOX_HEAD_DIM), F32),
                        pltpu.VMEM((tq, w), F32)],
    )
    return pl.pallas_call(
        _fox_attn_kernel,
        grid_spec=grid_spec,
        out_shape=jax.ShapeDtypeStruct((batch * seq, FOX_HEADS * FOX_HEAD_DIM), F32),
        compiler_params=_cparams(("parallel", "parallel", "arbitrary")),
        name="fox_attn",
    )(qt, kt, proj, proj, proj, cum_rows)


def _head_sum(x, bd):
    hi, lo = _split_bf16(x)
    parts = [_dot(hi[:, j * LANES:(j + 1) * LANES], bd) + _dot(lo[:, j * LANES:(j + 1) * LANES], bd)
             for j in range(x.shape[1] // LANES)]
    return jnp.concatenate(parts, axis=1)


def _head_block_diag():
    r = lax.broadcasted_iota(jnp.int32, (LANES, LANES), 0) // RWKV_HEAD_DIM
    c = lax.broadcasted_iota(jnp.int32, (LANES, LANES), 1) // RWKV_HEAD_DIM
    return (r == c).astype(BF16)


def _rwkv_prep_kernel(r_ref, k_ref, v_ref, gd_ref, lo_ref, pr_ref, pk_ref, pv_ref, pgd_ref, plo_ref,
                      mu_r, mu_k, mu_v, mu_gd, mu_lo, w0_ref, wup_ref, a0_ref, aup_ref, gup_ref,
                      kk_ref, ka_ref, rk_ref,
                      or_ref, olw_ref, ok_ref, ov_ref, oa_ref, ob_ref, og_ref, obonus_ref, *, tiles_per_batch):
    i = pl.program_id(0)
    first = (i % tiles_per_batch) == 0

    def shifted(cur_ref, prv_ref, mu_ref):
        cur = cur_ref[...]
        last = jnp.where(first, 0.0, prv_ref[7:8, :])
        row = lax.broadcasted_iota(jnp.int32, cur.shape, 0)
        prev = jnp.where(row == 0, last, pltpu.roll(cur, 1, 0))
        return cur + (prev - cur) * mu_ref[...]

    r = shifted(r_ref, pr_ref, mu_r)
    k = shifted(k_ref, pk_ref, mu_k)
    v = shifted(v_ref, pv_ref, mu_v)
    gd = shifted(gd_ref, pgd_ref, mu_gd)
    lo = shifted(lo_ref, plo_ref, mu_lo)
    wd = lo[:, :LANES]
    ad = lo[:, LANES:]

    w_pre = w0_ref[...] + _dot_bf16x3(jnp.tanh(wd), wup_ref[...])
    w_raw = _log_sigmoid(w_pre) - 0.5
    log_decay = -jnp.exp(w_raw)
    a = jax.nn.sigmoid(a0_ref[...] + _dot_bf16x3(ad, aup_ref[...]))
    g = _dot_bf16x3(jax.nn.sigmoid(gd), gup_ref[...])

    bd = _head_block_diag()
    kk = k * kk_ref[...]
    nrm = jnp.maximum(jnp.sqrt(_head_sum(kk * kk, bd)), 1e-12)
    kk = kk / nrm
    k_mod = k * (1.0 + (a - 1.0) * ka_ref[...])
    bonus = _head_sum(r * k_mod * rk_ref[...], bd) * v

    or_ref[...] = r.astype(BF16)
    olw_ref[...] = log_decay
    ok_ref[...] = k_mod.astype(BF16)
    ov_ref[...] = v.astype(BF16)
    oa_ref[...] = (-kk).astype(BF16)
    ob_ref[...] = (kk * a).astype(BF16)
    og_ref[...] = g.astype(BF16)
    obonus_ref[...] = bonus.astype(BF16)


def _rwkv_prep(proj, cols, mus, w0, w_up, a0, a_up, g_up, k_k, k_a, r_k, tokens_per_batch):
    t = proj.shape[0]
    w = RWKV_HEADS * RWKV_HEAD_DIM
    tm = min(512, tokens_per_batch)
    tpb = tokens_per_batch // tm
    widths = [w, w, w, 256, 256]
    offs = [cols["rr"], cols["rk"], cols["rv"], cols["gd"], cols["lo"]]
    cur_specs = [pl.BlockSpec((tm, wd), functools.partial(lambda i, cb: (i, cb), cb=o // wd))
                 for wd, o in zip(widths, offs)]
    prv_specs = [pl.BlockSpec((8, wd), functools.partial(
        lambda i, cb: (jnp.maximum(i * (tm // 8) - 1, 0), cb), cb=o // wd))
        for wd, o in zip(widths, offs)]
    full = lambda a: pl.BlockSpec(a.shape, lambda i: (0,) * a.ndim)
    params = list(mus) + [w0, w_up, a0, a_up, g_up, k_k, k_a, r_k]
    out_spec = pl.BlockSpec((tm, w), lambda i: (i, 0))
    return pl.pallas_call(
        functools.partial(_rwkv_prep_kernel, tiles_per_batch=tpb),
        grid=(t // tm,),
        in_specs=cur_specs + prv_specs + [full(p) for p in params],
        out_specs=[out_spec] * 8,
        out_shape=[jax.ShapeDtypeStruct((t, w), dt) for dt in (BF16, F32, BF16, BF16, BF16, BF16, BF16, BF16)],
        compiler_params=_cparams(("parallel",)),
        name="rwkv_prep",
    )(*([proj] * 10), *params)


def _rwkv_scan_kernel(r_ref, lw_ref, k_ref, v_ref, a_ref, b_ref, o_ref, h_scr):
    c = pl.program_id(1)

    @pl.when(c == 0)
    def _():
        h_scr[...] = jnp.zeros(h_scr.shape, F32)

    C = r_ref.shape[0]
    n_pairs = r_ref.shape[1] // LANES
    P = HIGHEST

    row = lax.broadcasted_iota(jnp.int32, (C, C), 0)
    col = lax.broadcasted_iota(jnp.int32, (C, C), 1)
    tri = (row >= col).astype(F32)
    lw = lw_ref[...]
    cw = _dot(tri, lw, P)
    cw_end = cw[C - 1:C, :]
    e_pos = jnp.exp(cw)
    e_prev = jnp.exp(cw - lw)
    e_neg = jnp.exp(-cw)
    e_end = jnp.exp(cw_end - cw)
    w_end = jnp.exp(cw_end)

    a = a_ref[...].astype(F32)
    b = b_ref[...].astype(F32)
    k = k_ref[...].astype(F32)
    r = r_ref[...].astype(F32)
    at = a * e_prev
    bt = b * e_neg
    kt = k * e_neg
    rt = r * e_pos
    bh = b * e_end
    kh = k * e_end
    v = v_ref[...].astype(F32)

    lane = lax.broadcasted_iota(jnp.int32, (C, LANES), 1)
    head0 = lane < RWKV_HEAD_DIM
    r2 = lax.broadcasted_iota(jnp.int32, (2 * C, 2 * C), 0)
    c2 = lax.broadcasted_iota(jnp.int32, (2 * C, 2 * C), 1)
    same = (r2 // C) == (c2 // C)
    strict = same & (r2 > c2)
    incl = same & (r2 >= c2)
    eye = (r2 == c2).astype(F32)

    def two(x, p):
        xp = x[:, p * LANES:(p + 1) * LANES]
        return jnp.concatenate([jnp.where(head0, xp, 0.0), jnp.where(head0, 0.0, xp)], axis=0)

    def mm(x, y):
        return _dot(x.astype(BF16), y.astype(BF16))

    def mm3(x, y):
        xh, xl = _split_bf16(x)
        yh, yl = _split_bf16(y)
        n = y.shape[1]
        wide = _dot(xh, jnp.concatenate([yh, yl], axis=1))
        return wide[:, :n] + (wide[:, n:] + _dot(xl, yh))

    G = 2 * C
    pairs = range(n_pairs)
    at2 = [two(at, p) for p in pairs]
    rt2 = [two(rt, p) for p in pairs]
    v2 = [two(v, p).astype(BF16) for p in pairs]
    gram = [_dot_nt(jnp.concatenate([at2[p], rt2[p]], axis=0).astype(BF16),
                    jnp.concatenate([two(bt, p), two(kt, p)], axis=0).astype(BF16)) for p in pairs]
    a_ab = [jnp.where(strict, gram[p][:G, :G], 0.0) for p in pairs]
    nmat = list(a_ab)
    pw = list(a_ab)
    for _ in range(max(1, (C - 1).bit_length() - 1)):
        pw = [mm3(pw[p], pw[p]) for p in pairs]
        nmat = [nmat[p] + pw[p] + mm3(nmat[p], pw[p]) for p in pairs]
    akv = [mm(jnp.where(strict, gram[p][:G, G:], 0.0), v2[p]) for p in pairs]
    rhs = [jnp.concatenate([at2[p], akv[p]], axis=1) for p in pairs]
    pq = [(rhs[p] + mm(nmat[p], rhs[p])).astype(BF16) for p in pairs]
    ry = [mm(jnp.where(incl, gram[p][G:, :G], 0.0), pq[p]) for p in pairs]
    mv = [mm(jnp.where(incl, gram[p][G:, G:], 0.0), v2[p]) for p in pairs]
    ge = [_dot_tn(two(bh, p).astype(BF16), pq[p]) for p in pairs]
    kv = [_dot_tn(two(kh, p).astype(BF16), v2[p]) for p in pairs]
    for p in pairs:
        rr = rt2[p] + ry[p][:, :LANES]
        gm = eye * w_end[:, p * LANES:(p + 1) * LANES] + ge[p][:, :LANES]
        yh = mm(jnp.concatenate([rr, gm], axis=0), h_scr[p])
        y2 = yh[:G] + ry[p][:, LANES:] + mv[p]
        h_scr[p] = yh[G:] + ge[p][:, LANES:] + kv[p]
        o_ref[:, p * LANES:(p + 1) * LANES] = y2[:C, :] + y2[C:, :]


def _rwkv_scan(r, lw, k, v, a, b, batch, seq):
    w = r.shape[1]
    nc = seq // CHUNK
    spec = pl.BlockSpec((CHUNK, w), lambda bi, ci: (bi * nc + ci, 0))
    return pl.pallas_call(
        _rwkv_scan_kernel,
        grid=(batch, nc),
        in_specs=[spec] * 6,
        out_specs=spec,
        out_shape=jax.ShapeDtypeStruct((batch * seq, w), F32),
        scratch_shapes=[pltpu.VMEM((w // LANES, LANES, LANES), F32)],
        compiler_params=_cparams(("parallel", "arbitrary")),
        name="rwkv_scan",
    )(r, lw, k, v, a, b)


def _mix_out_kernel(o_ref, gate_ref, y_ref, bonus_ref, g_ref, x_ref, g1_ref, lnw_ref, lnb_ref, w_ref, out_ref):
    fox = o_ref[...] * jax.nn.sigmoid(gate_ref[...])
    bd = _head_block_diag()
    y = y_ref[...]
    inv_n = 1.0 / RWKV_HEAD_DIM
    mean = _head_sum(y, bd) * inv_n
    d = y - mean
    var = _head_sum(d * d, bd) * inv_n
    yn = d * lax.rsqrt(var + GN_EPS) * lnw_ref[...] + lnb_ref[...]
    rw = (yn + bonus_ref[...].astype(F32)) * g_ref[...].astype(F32)
    wf = fox.shape[1]
    mix = _dot(fox.astype(BF16), w_ref[:wf, :]) + _dot(rw.astype(BF16), w_ref[wf:, :])
    out_ref[...] = x_ref[...] + g1_ref[0] * mix


def _mix_out(o_fox, proj, gate_col, y, bonus, g, x2, g1, ln_w, ln_b, w_out_bf, tokens_per_batch):
    t, d = x2.shape
    wf = o_fox.shape[1]
    wr = y.shape[1]
    tm = min(512, tokens_per_batch)
    tpb = tokens_per_batch // tm
    return pl.pallas_call(
        _mix_out_kernel,
        grid=(t // tm,),
        in_specs=[pl.BlockSpec((tm, wf), lambda i: (i, 0)),
                  pl.BlockSpec((tm, wf), lambda i: (i, gate_col // wf)),
                  pl.BlockSpec((tm, wr), lambda i: (i, 0)),
                  pl.BlockSpec((tm, wr), lambda i: (i, 0)),
                  pl.BlockSpec((tm, wr), lambda i: (i, 0)),
                  pl.BlockSpec((tm, d), lambda i: (i, 0)),
                  pl.BlockSpec((1, 1, d), lambda i: (i // tpb, 0, 0)),
                  pl.BlockSpec((1, wr), lambda i: (0, 0)),
                  pl.BlockSpec((1, wr), lambda i: (0, 0)),
                  pl.BlockSpec(w_out_bf.shape, lambda i: (0, 0))],
        out_specs=pl.BlockSpec((tm, d), lambda i: (i, 0)),
        out_shape=jax.ShapeDtypeStruct((t, d), F32),
        compiler_params=_cparams(("parallel",)),
        name="mix_out",
    )(o_fox, proj, y, bonus, g, x2, g1, ln_w, ln_b, w_out_bf)


def _top_k_mask_rows(s, k, tie_safe, want_rank=True):
    n = s.shape[0]
    s0 = s
    row = lax.broadcasted_iota(jnp.int32, s.shape, 0).astype(F32)
    rank = jnp.full(s.shape, float(k), F32) if want_rank else None
    vals = []
    for r in range(k):
        m = jnp.max(s, axis=0, keepdims=True)
        hit = s == m
        if tie_safe:
            pos = jnp.min(jnp.where(hit, row, float(n)), axis=0, keepdims=True)
            hit = row == pos
        vals.append(m)
        if want_rank:
            rank = jnp.where(hit, float(r), rank)
        s = jnp.where(hit, -jnp.inf, s)
    picked = (s == -jnp.inf) & (s0 != -jnp.inf)
    n_picked = jnp.sum(jnp.where(picked, 1.0, 0.0), axis=0, keepdims=True)
    return jnp.concatenate(vals, axis=0), rank, picked, jnp.where(n_picked == float(k), 1.0, 0.0)


def _candidate_rows():
    k = PEER_TOPK
    groups, valid = [], []
    for r0 in range(k // 2):
        n1 = k // (r0 + 1)
        for g in range(-(-n1 // 8)):
            groups.append((r0, g * 8))
            valid.append([g * 8 + i < n1 for i in range(8)])
    groups.append((None, k // 2))
    valid.append([True] * 8)
    return groups, valid


def _route_head(sa, sb, tie_safe):
    k = PEER_TOPK
    tm = sa.shape[1]
    groups, valid = _candidate_rows()
    sub8 = lax.broadcasted_iota(jnp.int32, (8, tm), 0)
    top_a, rank_a, _, clean_a = _top_k_mask_rows(sa, k, tie_safe)
    top_b, rank_b, _, clean_b = _top_k_mask_rows(sb, k, tie_safe)
    parts = []
    for (r0, off), ok in zip(groups, valid):
        if r0 is None:
            part = top_a[off:off + 8, :] + top_b[0:1, :]
        else:
            part = top_a[r0:r0 + 1, :] + top_b[off:off + 8, :]
        if not all(ok):
            part = jnp.where(sub8 < sum(ok), part, -jnp.inf)
        parts.append(part)
    cand = jnp.concatenate(parts, axis=0)
    _, _, sel, clean_c = _top_k_mask_rows(cand, k, tie_safe, want_rank=False)
    z = jnp.sum(jnp.where(sel, jnp.exp(cand - cand[0:1, :]), 0.0), axis=0, keepdims=True)
    self32 = jnp.where(sel, 1.0, 0.0)
    n_keys = sa.shape[0]
    rank3 = rank_a.astype(BF16).reshape(n_keys // BF16_ROWS, BF16_ROWS, tm)
    length3 = jnp.zeros(rank3.shape, BF16)

    def add_count(r0, count_row):
        tile = jnp.broadcast_to(count_row, (BF16_ROWS, tm)).astype(BF16)
        return length3 + jnp.where(rank3 == float(r0), tile[None], 0.0).astype(BF16)

    tail = None
    per_rank = {}
    for gi, (r0, off) in enumerate(groups):
        cnt = self32[gi * 8:(gi + 1) * 8, :]
        if r0 is None:
            tail = cnt
        else:
            c = jnp.sum(cnt, axis=0, keepdims=True)
            per_rank[r0] = c if r0 not in per_rank else per_rank[r0] + c
    for r in range(8):
        per_rank[k // 2 + r] = tail[r:r + 1, :]
    for r0, c in per_rank.items():
        length3 = add_count(r0, c)
    length = length3.reshape(n_keys, tm).astype(F32)
    pa = jnp.exp(sa - top_a[0:1, :])
    qb = jnp.exp(sb - top_b[0:1, :]) / z
    return pa, length, qb, rank_b, clean_a * clean_b * clean_c


def _peer_query_kernel(x_ref, nw_ref, sc_ref, sh_ref, wq_ref, ht_ref, q_ref):
    h = _modulated_norm(x_ref[...], nw_ref[...], sc_ref[0], sh_ref[0])
    ht_ref[...] = h.T.astype(BF16)
    q = _dot(h.astype(BF16), wq_ref[...])
    dk = q.shape[1] // PEER_HEADS
    for hd in range(PEER_HEADS):
        q_ref[hd] = q[:, hd * dk:(hd + 1) * dk]


def _peer_query(x1, norm_w, sc, sh, wq_bf, tokens_per_batch):
    t, d = x1.shape
    dk = wq_bf.shape[1] // PEER_HEADS
    tm = min(512, tokens_per_batch)
    tpb = tokens_per_batch // tm
    return pl.pallas_call(
        _peer_query_kernel,
        grid=(t // tm,),
        in_specs=[pl.BlockSpec((tm, d), lambda i: (i, 0)),
                  pl.BlockSpec((1, d), lambda i: (0, 0)),
                  pl.BlockSpec((1, 1, d), lambda i: (i // tpb, 0, 0)),
                  pl.BlockSpec((1, 1, d), lambda i: (i // tpb, 0, 0)),
                  pl.BlockSpec(wq_bf.shape, lambda i: (0, 0))],
        out_specs=[pl.BlockSpec((d, tm), lambda i: (0, i)),
                   pl.BlockSpec((PEER_HEADS, tm, dk), lambda i: (0, i, 0))],
        out_shape=[jax.ShapeDtypeStruct((d, t), BF16),
                   jax.ShapeDtypeStruct((PEER_HEADS, t, dk), F32)],
        compiler_params=_cparams(("parallel",)),
        name="peer_query",
    )(x1, norm_w, sc, sh, wq_bf)


def _gelu_exact(x):
    return 0.5 * x * (1.0 + lax.erf(x * 0.7071067811865476))


def _peer_route_act_kernel(ht_ref, q_ref, keys_ref, u_ref,
                           act_ref, pa_ref, len_ref, qb_ref, rnk_ref, code_scr):
    hd = pl.program_id(1)
    tm = ht_ref.shape[1]
    half = keys_ref.shape[-1]
    sub = min(ROUTE_SUB_TOKENS, tm)
    blocked = (PEER_NKEYS // PEER_SLABS, PEER_SLABS, sub)

    @pl.when(hd == 0)
    def _():
        code_scr[...] = jnp.zeros(code_scr.shape, F32)

    def scores(h, t0):
        q = q_ref[h, t0:t0 + sub, :]
        return (_dot_nt(keys_ref[h, 0], q[:, :half], HIGHEST),
                _dot_nt(keys_ref[h, 1], q[:, half:], HIGHEST))

    def route(h, t0, sa, sb, tie_safe):
        pa, length, qbt, rank_b, clean = _route_head(sa, sb, tie_safe)
        pa_ref[h, :, :, t0:t0 + sub] = pa.reshape(blocked)
        len_ref[h, :, :, t0:t0 + sub] = length.reshape(blocked)
        qb_ref[h, :, t0:t0 + sub] = qbt.astype(BF16)
        rnk_ref[h, :, t0:t0 + sub] = rank_b.astype(BF16)
        return clean

    starts = range(0, tm, sub)
    score_tiles = [scores(hd, t0) for t0 in starts]
    act_ref[...] = _dot(u_ref[...], ht_ref[...]).astype(BF16)

    dirty = jnp.zeros((1, 1), F32)
    for t0, (sa, sb) in zip(starts, score_tiles):
        dirty = jnp.maximum(dirty, 1.0 - jnp.min(route(hd, t0, sa, sb, False), axis=1, keepdims=True))
    code_scr[0:1, 0:1] += dirty * lax.shift_left(1, hd).astype(F32)

    @pl.when(hd == pl.num_programs(1) - 1)
    def _():
        tie_bits = code_scr[0, 0].astype(jnp.int32)

        def redo(h, carry):
            @pl.when(((tie_bits >> h) & 1) == 1)
            def _():
                for t0 in starts:
                    route(h, t0, *scores(h, t0), True)
            return carry

        lax.fori_loop(0, PEER_HEADS, redo, 0)


def _peer_route_act(ht_bf, q_heads, sub_keys, u_bf, tokens_per_batch):
    d, t = ht_bf.shape
    ne = u_bf.shape[0]
    bu = ne // PEER_HEADS
    dk = q_heads.shape[-1]
    tm = min(512, tokens_per_batch)
    n_blocks = PEER_NKEYS // PEER_SLABS
    dense = jax.ShapeDtypeStruct((PEER_HEADS, PEER_NKEYS, t), BF16)
    dense_spec = pl.BlockSpec((PEER_HEADS, PEER_NKEYS, tm), lambda i, j: (0, 0, i))
    blocked = jax.ShapeDtypeStruct((PEER_HEADS, n_blocks, PEER_SLABS, t), F32)
    blocked_spec = pl.BlockSpec((PEER_HEADS, n_blocks, PEER_SLABS, tm), lambda i, j: (0, 0, 0, i))
    return pl.pallas_call(
        _peer_route_act_kernel,
        grid=(t // tm, PEER_HEADS),
        in_specs=[pl.BlockSpec((d, tm), lambda i, j: (0, i)),
                  pl.BlockSpec((PEER_HEADS, tm, dk), lambda i, j: (0, i, 0)),
                  pl.BlockSpec(sub_keys.shape, lambda i, j: (0, 0, 0, 0)),
                  pl.BlockSpec((bu, d), lambda i, j: (j, 0))],
        out_specs=[pl.BlockSpec((bu, tm), lambda i, j: (j, i)),
                   blocked_spec, blocked_spec, dense_spec, dense_spec],
        out_shape=[jax.ShapeDtypeStruct((ne, t), BF16), blocked, blocked, dense, dense],
        scratch_shapes=[pltpu.VMEM((8, LANES), F32)],
        compiler_params=_cparams(("parallel", "arbitrary")),
        name="peer_route_act",
    )(ht_bf, q_heads, sub_keys, u_bf)


def _transpose_blocks_kernel(x_ref, o_ref):
    o_ref[0] = x_ref[...].T.astype(BF16)


def _transpose_blocks(w, block):
    n, d = w.shape
    return pl.pallas_call(
        _transpose_blocks_kernel,
        grid=(n // block,),
        in_specs=[pl.BlockSpec((block, d), lambda j: (j, 0))],
        out_specs=pl.BlockSpec((1, d, block), lambda j: (j, 0, 0)),
        out_shape=jax.ShapeDtypeStruct((n // block, d, block), BF16),
        compiler_params=_cparams(("parallel",)),
        name="transpose_blocks",
    )(w)


def _peer_expert_kernel(act_ref, vt_ref, pa_ref, len_ref, qb_ref, rnk_ref, x_ref, g2_ref, o_ref, acc_scr):
    j = pl.program_id(1)
    be, tm = act_ref.shape

    @pl.when(j == 0)
    def _():
        acc_scr[...] = jnp.zeros(acc_scr.shape, F32)

    n_slabs = be // PEER_NKEYS
    rows = BF16_ROWS
    n_groups = PEER_NKEYS // rows

    def row_tile(ref, hd, s):
        return jnp.broadcast_to(ref[hd, 0, s:s + 1, :], (rows, tm)).astype(BF16)

    zero = jnp.zeros((rows, tm), BF16)
    parts = [[None] * n_groups for _ in range(n_slabs)]
    slab_group = 8
    for s0 in range(0, n_slabs, slab_group):
        ss = range(s0, min(s0 + slab_group, n_slabs))
        ln = {(hd, s): row_tile(len_ref, hd, s) for hd in range(PEER_HEADS) for s in ss}
        pa = {(hd, s): row_tile(pa_ref, hd, s) for hd in range(PEER_HEADS) for s in ss}
        for g in range(n_groups):
            sl = slice(g * rows, (g + 1) * rows)
            acc = {}
            for hd in range(PEER_HEADS):
                rk = rnk_ref[hd, sl, :]
                qv = qb_ref[hd, sl, :]
                for s in ss:
                    term = jnp.where(rk < ln[hd, s], qv, zero) * pa[hd, s]
                    acc[s] = term if s not in acc else acc[s] + term
            for s in ss:
                lo = s * PEER_NKEYS + g * rows
                parts[s][g] = acc[s] * _gelu_exact(act_ref[lo:lo + rows, :].astype(F32)).astype(BF16)
    p = jnp.concatenate([parts[s][g] for s in range(n_slabs) for g in range(n_groups)], axis=0)
    acc_scr[...] += _dot(vt_ref[0], p)

    @pl.when(j == pl.num_programs(1) - 1)
    def _():
        o_ref[...] = x_ref[...] + g2_ref[0] * acc_scr[...].T


def _peer_expert(act, vt_bf, pa, ln, qb, rnk, x1, g2, tokens_per_batch):
    t, d = x1.shape
    ne = act.shape[0]
    tm = min(512, tokens_per_batch)
    tpb = tokens_per_batch // tm
    be = PEER_EXPERT_BLOCK
    dense_spec = pl.BlockSpec((PEER_HEADS, PEER_NKEYS, tm), lambda i, j: (0, 0, i))
    blocked_spec = pl.BlockSpec((PEER_HEADS, 1, PEER_SLABS, tm), lambda i, j: (0, j, 0, i))
    return pl.pallas_call(
        _peer_expert_kernel,
        grid=(t // tm, ne // be),
        in_specs=[pl.BlockSpec((be, tm), lambda i, j: (j, i)),
                  pl.BlockSpec((1, d, be), lambda i, j: (j, 0, 0)),
                  blocked_spec, blocked_spec, dense_spec, dense_spec,
                  pl.BlockSpec((tm, d), lambda i, j: (i, 0)),
                  pl.BlockSpec((1, 1, d), lambda i, j: (i // tpb, 0, 0))],
        out_specs=pl.BlockSpec((tm, d), lambda i, j: (i, 0)),
        out_shape=jax.ShapeDtypeStruct((t, d), F32),
        scratch_shapes=[pltpu.VMEM((d, tm), F32)],
        compiler_params=_cparams(("parallel", "arbitrary")),
        name="peer_expert",
    )(act, vt_bf, pa, ln, qb, rnk, x1, g2)


def _pad_cols(a, n):
    return jnp.pad(a, ((0, 0), (0, n - a.shape[1])))


def _pad_rows(a, n):
    return jnp.pad(a, ((0, n - a.shape[0]), (0, 0)))


def _layer(x, c, w_ada, b_ada, norm_mix_w, w_in, fox_q_norm_w, fox_k_norm_w, fox_f_bias,
           rwkv_mu, rwkv_w0, rwkv_w_up, rwkv_a0, rwkv_a_up, rwkv_g_up, rwkv_k_k, rwkv_k_a,
           rwkv_r_k, rwkv_ln_w, rwkv_ln_b, w_out, norm_ffn_w, peer_w_query, peer_sub_keys,
           peer_u, peer_v):
    B, S, D = x.shape
    T = B * S
    fw = FOX_HEADS * FOX_HEAD_DIM
    rw = RWKV_HEADS * RWKV_HEAD_DIM
    w_lora = rwkv_w_up.shape[0]
    a_lora = rwkv_a_up.shape[0]
    g_lora = rwkv_g_up.shape[0]
    assert w_lora <= LANES and a_lora + FOX_HEADS <= LANES and g_lora == 256

    c_pad = _pad_rows(c, 8)
    mod = _ada_mod(c_pad, w_ada, b_ada)[:B]
    sh1, sc1, g1, sh2, sc2, g2 = [m.reshape(B, 1, D) for m in jnp.split(mod, 6, axis=-1)]

    fox_cols = 4 * fw + FOX_HEADS
    wi_fox, wi_rw = w_in[:, :fox_cols], w_in[:, fox_cols:]
    mu = rwkv_mu.reshape(1, -1)
    seg = lambda a, lo, n: a[:, lo:lo + n]
    w_perm = jnp.concatenate([
        seg(wi_fox, 0, 4 * fw),
        seg(wi_rw, 0, 3 * rw),
        seg(wi_rw, 3 * rw + w_lora + a_lora, g_lora),
        _pad_cols(seg(wi_rw, 3 * rw, w_lora), LANES),
        _pad_cols(jnp.concatenate([seg(wi_rw, 3 * rw + w_lora, a_lora), seg(wi_fox, 4 * fw, FOX_HEADS)], 1), LANES),
    ], axis=1).astype(BF16)
    cols = {"gate": 3 * fw, "rr": 4 * fw, "rk": 4 * fw + rw, "rv": 4 * fw + 2 * rw,
            "gd": 4 * fw + 3 * rw, "lo": 4 * fw + 3 * rw + g_lora}
    f_lane = a_lora
    f_block = (cols["lo"] + LANES) // LANES
    mus = [seg(mu, 0, rw), seg(mu, rw, rw), seg(mu, 2 * rw, rw),
           seg(mu, 3 * rw + w_lora + a_lora, g_lora),
           jnp.concatenate([_pad_cols(seg(mu, 3 * rw, w_lora), LANES),
                            _pad_cols(seg(mu, 3 * rw + w_lora, a_lora), LANES)], 1)]
    scale = FOX_HEAD_DIM ** -0.5 * LOG2_E
    head_w = _pad_cols(jnp.concatenate([jnp.tile(fox_q_norm_w * scale, FOX_HEADS),
                                        jnp.tile(fox_k_norm_w, FOX_HEADS)]).reshape(1, -1), w_perm.shape[1])

    x2 = x.reshape(T, D)
    proj = _in_proj(_norm_mod(x2, norm_mix_w.reshape(1, D), sc1, sh1, S), w_perm, head_w, 2 * fw)

    f_bias_row = jnp.zeros((1, LANES), F32).at[0, f_lane:f_lane + FOX_HEADS].set(fox_f_bias)
    cum = _fox_cum(proj, f_bias_row, B, S, f_block)
    cum_rows = cum.reshape(B, S, LANES)[:, :, f_lane:f_lane + FOX_HEADS].transpose(0, 2, 1).reshape(B, FOX_HEADS, 1, S)
    o_fox = _fox_attn(proj, cum_rows, B, S)

    row = lambda a: a.reshape(1, -1)
    r, lw, k, v, a_vec, b_vec, g, bonus = _rwkv_prep(
        proj, cols, mus, row(rwkv_w0), _pad_rows(rwkv_w_up, LANES), row(rwkv_a0), _pad_rows(rwkv_a_up, LANES),
        rwkv_g_up, row(rwkv_k_k), row(rwkv_k_a), row(rwkv_r_k), S)
    y = _rwkv_scan(r, lw, k, v, a_vec, b_vec, B, S)

    x1 = _mix_out(o_fox, proj, cols["gate"], y, bonus, g, x2, g1, row(rwkv_ln_w), row(rwkv_ln_b),
                  w_out.astype(BF16), S)

    ht, q_heads = _peer_query(x1, norm_ffn_w.reshape(1, D), sc2, sh2, peer_w_query.astype(BF16), S)
    act, pa, ln, qb, rnk = _peer_route_act(ht, q_heads, peer_sub_keys, peer_u.astype(BF16), S)
    vt = _transpose_blocks(peer_v, PEER_EXPERT_BLOCK)
    out = _peer_expert(act, vt, pa, ln, qb, rnk, x1, g2, S)
    return out.reshape(B, S, D)


def kernel(x, c, w_ada, b_ada, norm_mix_w, w_in, fox_q_norm_w, fox_k_norm_w, fox_f_bias, rwkv_mu, rwkv_w0,
           rwkv_w_up, rwkv_a0, rwkv_a_up, rwkv_g_up, rwkv_k_k, rwkv_k_a, rwkv_r_k, rwkv_ln_w, rwkv_ln_b,
           w_out, norm_ffn_w, peer_w_query, peer_sub_keys, peer_u, peer_v):
    params = (w_ada, b_ada, norm_mix_w, w_in, fox_q_norm_w, fox_k_norm_w, fox_f_bias, rwkv_mu, rwkv_w0,
              rwkv_w_up, rwkv_a0, rwkv_a_up, rwkv_g_up, rwkv_k_k, rwkv_k_a, rwkv_r_k, rwkv_ln_w, rwkv_ln_b,
              w_out, norm_ffn_w, peer_w_query, peer_sub_keys, peer_u, peer_v)
    for l in range(w_ada.shape[0]):
        x = _layer(x, c, *[p[l] for p in params])
    return x
```

```python
import functools

import jax
import jax.numpy as jnp
from jax import lax
from jax.experimental import pallas as pl
from jax.experimental.pallas import tpu as pltpu

F32 = jnp.float32
BF16 = jnp.bfloat16
HIGHEST = lax.Precision.HIGHEST

LANES = 128
BF16_ROWS = 16
NORM_EPS = 1e-6
GN_EPS = 64e-5
CHUNK = 64
FOX_HEADS = 8
FOX_HEAD_DIM = 128
RWKV_HEADS = 16
RWKV_HEAD_DIM = 64
PEER_HEADS = 8
PEER_NKEYS = 128
PEER_TOPK = 16
PEER_EXPERT_BLOCK = 2048
PEER_SLABS = PEER_EXPERT_BLOCK // PEER_NKEYS
ROUTE_SUB_TOKENS = 256
NEG_BIG = -1e30
LOG2_E = 1.4426950408889634
VMEM_LIMIT = 56 * 1024 * 1024


def _cparams(sem):
    return pltpu.CompilerParams(dimension_semantics=sem, vmem_limit_bytes=VMEM_LIMIT)


def _dot(a, b, precision=None):
    return jnp.dot(a, b, preferred_element_type=F32, precision=precision)


def _dot_nt(a, b, precision=None):
    return lax.dot_general(a, b, (((1,), (1,)), ((), ())), preferred_element_type=F32,
                           precision=precision)


def _split_bf16(x):
    hi = x.astype(BF16)
    return hi, (x - hi.astype(F32)).astype(BF16)


def _dot_bf16x3(a, b):
    ah, al = _split_bf16(a)
    bh, bl = _split_bf16(b)
    return _dot(ah, bh) + (_dot(ah, bl) + _dot(al, bh))


def _dot_tn(a, b, precision=None):
    return lax.dot_general(a, b, (((0,), (0,)), ((), ())), preferred_element_type=F32,
                           precision=precision)


def _ada_kernel(c_ref, w_ref, b_ref, o_ref):
    c = c_ref[...]
    s = c * jax.nn.sigmoid(c)
    o_ref[...] = _dot(s, w_ref[...], HIGHEST) + b_ref[...]


def _ada_mod(c_pad, w_ada, b_ada):
    rows, d = c_pad.shape
    n = w_ada.shape[1]
    bn = 1024
    return pl.pallas_call(
        _ada_kernel,
        grid=(n // bn,),
        in_specs=[pl.BlockSpec((rows, d), lambda j: (0, 0)),
                  pl.BlockSpec((d, bn), lambda j: (0, j)),
                  pl.BlockSpec((1, bn), lambda j: (0, j))],
        out_specs=pl.BlockSpec((rows, bn), lambda j: (0, j)),
        out_shape=jax.ShapeDtypeStruct((rows, n), F32),
        compiler_params=_cparams(("arbitrary",)),
        name="ada_mod",
    )(c_pad, w_ada, b_ada.reshape(1, n))


def _modulated_norm(x, nw, sc, sh):
    y = x * lax.rsqrt(jnp.mean(x * x, axis=-1, keepdims=True) + NORM_EPS)
    return y * nw * (1.0 + sc) + sh


def _norm_mod_kernel(x_ref, nw_ref, sc_ref, sh_ref, o_ref):
    o_ref[...] = _modulated_norm(x_ref[...], nw_ref[...], sc_ref[0], sh_ref[0]).astype(BF16)


def _norm_mod(x2, norm_w, sc, sh, tokens_per_batch):
    t, d = x2.shape
    tm = min(512, tokens_per_batch)
    tpb = tokens_per_batch // tm
    return pl.pallas_call(
        _norm_mod_kernel,
        grid=(t // tm,),
        in_specs=[pl.BlockSpec((tm, d), lambda i: (i, 0)),
                  pl.BlockSpec((1, d), lambda i: (0, 0)),
                  pl.BlockSpec((1, 1, d), lambda i: (i // tpb, 0, 0)),
                  pl.BlockSpec((1, 1, d), lambda i: (i // tpb, 0, 0))],
        out_specs=pl.BlockSpec((tm, d), lambda i: (i, 0)),
        out_shape=jax.ShapeDtypeStruct((t, d), BF16),
        compiler_params=_cparams(("parallel",)),
        name="norm_mod",
    )(x2, norm_w, sc, sh)


def _in_proj_kernel(h_ref, w_ref, hw_ref, o_ref, *, n_qk_blocks):
    j = pl.program_id(1)
    acc = _dot(h_ref[...], w_ref[...])

    @pl.when(j < n_qk_blocks)
    def _():
        for hh in range(acc.shape[1] // FOX_HEAD_DIM):
            sl = slice(hh * FOX_HEAD_DIM, (hh + 1) * FOX_HEAD_DIM)
            a = acc[:, sl]
            rs = lax.rsqrt(jnp.mean(a * a, axis=-1, keepdims=True) + NORM_EPS)
            o_ref[:, sl] = a * rs * hw_ref[:, sl]

    @pl.when(j >= n_qk_blocks)
    def _():
        o_ref[...] = acc


def _in_proj(h_bf, w_bf, head_w, n_qk_cols):
    t, d = h_bf.shape
    n = w_bf.shape[1]
    tm = min(2048, t)
    bn = 512
    kern = functools.partial(_in_proj_kernel, n_qk_blocks=n_qk_cols // bn)
    return pl.pallas_call(
        kern,
        grid=(t // tm, n // bn),
        in_specs=[pl.BlockSpec((tm, d), lambda i, j: (i, 0)),
                  pl.BlockSpec((d, bn), lambda i, j: (0, j)),
                  pl.BlockSpec((1, bn), lambda i, j: (0, j))],
        out_specs=pl.BlockSpec((tm, bn), lambda i, j: (i, j)),
        out_shape=jax.ShapeDtypeStruct((t, n), F32),
        compiler_params=_cparams(("parallel", "arbitrary")),
        name="in_proj",
    )(h_bf, w_bf, head_w)


def _log_sigmoid(x):
    return jnp.minimum(x, 0.0) - jnp.log(1.0 + jnp.exp(-jnp.abs(x)))


def _fox_cum_kernel(f_ref, b_ref, o_ref, *, blk):
    s = f_ref.shape[0]
    row = lax.broadcasted_iota(jnp.int32, (blk, blk), 0)
    col = lax.broadcasted_iota(jnp.int32, (blk, blk), 1)
    tri = (row >= col).astype(F32)
    carry = jnp.zeros((1, f_ref.shape[1]), F32)
    for i in range(s // blk):
        lf = _log_sigmoid(f_ref[i * blk:(i + 1) * blk, :] + b_ref[...])
        cs = _dot(tri, lf, HIGHEST) + carry
        o_ref[i * blk:(i + 1) * blk, :] = cs * LOG2_E
        carry = cs[blk - 1:blk, :]


def _fox_cum(proj, f_bias_row, batch, seq, col_block):
    blk = min(256, seq)
    return pl.pallas_call(
        functools.partial(_fox_cum_kernel, blk=blk),
        grid=(batch,),
        in_specs=[pl.BlockSpec((seq, LANES), lambda b: (b, col_block)),
                  pl.BlockSpec((1, LANES), lambda b: (0, 0))],
        out_specs=pl.BlockSpec((seq, LANES), lambda b: (b, 0)),
        out_shape=jax.ShapeDtypeStruct((batch * seq, LANES), F32),
        compiler_params=_cparams(("parallel",)),
        name="fox_cum",
    )(proj, f_bias_row)


def _fox_attn_kernel(qt_ref, kt_ref, q_ref, k_ref, v_ref, c_ref, o_ref, m_scr, l_scr, acc_scr):
    t = pl.program_id(2)
    qi = qt_ref[t]
    ki = kt_ref[t]
    hd = FOX_HEAD_DIM
    n_heads = q_ref.shape[1] // hd

    @pl.when(ki == 0)
    def _():
        m_scr[...] = jnp.full(m_scr.shape, NEG_BIG, F32)
        l_scr[...] = jnp.zeros(l_scr.shape, F32)
        acc_scr[...] = jnp.zeros(acc_scr.shape, F32)

    def step(masked):
        hs = range(n_heads)
        sl = [slice(h * hd, (h + 1) * hd) for h in hs]
        s = [_dot_nt(q_ref[:, sl[h]].astype(BF16), k_ref[:, sl[h]].astype(BF16)) - c_ref[0, h] for h in hs]
        if masked:
            row = lax.broadcasted_iota(jnp.int32, s[0].shape, 0)
            col = lax.broadcasted_iota(jnp.int32, s[0].shape, 1)
            s = [jnp.where(row >= col, x, NEG_BIG) for x in s]
        m_prev = [m_scr[h] for h in hs]
        m_new = [jnp.maximum(m_prev[h], jnp.max(s[h], axis=-1, keepdims=True)) for h in hs]
        n_rep = s[0].shape[1] // hd
        p = [jnp.exp2(s[h] - jnp.concatenate([m_new[h]] * n_rep, axis=1)) for h in hs]
        alpha = [jnp.exp2(m_prev[h] - m_new[h]) for h in hs]
        ones = jnp.ones((k_ref.shape[0], hd), BF16)
        pv = [_dot(p[h].astype(BF16), jnp.concatenate([v_ref[:, sl[h]].astype(BF16), ones], axis=1))
              for h in hs]
        for h in hs:
            l_scr[h] = alpha[h] * l_scr[h] + pv[h][:, hd:]
            acc_scr[:, sl[h]] = alpha[h] * acc_scr[:, sl[h]] + pv[h][:, :hd]
            m_scr[h] = m_new[h]

    @pl.when(ki < qi)
    def _():
        step(False)

    @pl.when(ki == qi)
    def _():
        step(True)
        for h in range(n_heads):
            sl = slice(h * hd, (h + 1) * hd)
            o_ref[:, sl] = acc_scr[:, sl] / l_scr[h]


def _fox_attn(proj, cum_rows, batch, seq):
    tq = min(512, seq)
    nq = seq // tq
    hps = 8
    w = hps * FOX_HEAD_DIM
    kcol = FOX_HEADS // hps
    tri = [(q, k) for q in range(nq) for k in range(q + 1)]
    qt = jnp.asarray([q for q, _ in tri], jnp.int32)
    kt = jnp.asarray([k for _, k in tri], jnp.int32)
    grid_spec = pltpu.PrefetchScalarGridSpec(
        num_scalar_prefetch=2,
        grid=(batch, kcol, len(tri)),
        in_specs=[
            pl.BlockSpec((tq, w), lambda b, h, t, qt, kt: (b * nq + qt[t], h)),
            pl.BlockSpec((tq, w), lambda b, h, t, qt, kt: (b * nq + kt[t], kcol + h)),
            pl.BlockSpec((tq, w), lambda b, h, t, qt, kt: (b * nq + kt[t], 2 * kcol + h)),
            pl.BlockSpec((1, hps, 1, tq), lambda b, h, t, qt, kt: (b, h, 0, kt[t])),
        ],
        out_specs=pl.BlockSpec((tq, w), lambda b, h, t, qt, kt: (b * nq + qt[t], h)),
        scratch_shapes=[pltpu.VMEM((hps, tq, FOX_HEAD_DIM), F32), pltpu.VMEM((hps, tq, FOX_HEAD_DIM), F32),
                        pltpu.VMEM((tq, w), F32)],
    )
    return pl.pallas_call(
        _fox_attn_kernel,
        grid_spec=grid_spec,
        out_shape=jax.ShapeDtypeStruct((batch * seq, FOX_HEADS * FOX_HEAD_DIM), F32),
        compiler_params=_cparams(("parallel", "parallel", "arbitrary")),
        name="fox_attn",
    )(qt, kt, proj, proj, proj, cum_rows)


def _head_sum(x, bd):
    hi, lo = _split_bf16(x)
    parts = [_dot(hi[:, j * LANES:(j + 1) * LANES], bd) + _dot(lo[:, j * LANES:(j + 1) * LANES], bd)
             for j in range(x.shape[1] // LANES)]
    return jnp.concatenate(parts, axis=1)


def _head_block_diag():
    r = lax.broadcasted_iota(jnp.int32, (LANES, LANES), 0) // RWKV_HEAD_DIM
    c = lax.broadcasted_iota(jnp.int32, (LANES, LANES), 1) // RWKV_HEAD_DIM
    return (r == c).astype(BF16)


def _rwkv_prep_kernel(r_ref, k_ref, v_ref, gd_ref, lo_ref, pr_ref, pk_ref, pv_ref, pgd_ref, plo_ref,
                      mu_r, mu_k, mu_v, mu_gd, mu_lo, w0_ref, wup_ref, a0_ref, aup_ref, gup_ref,
                      kk_ref, ka_ref, rk_ref,
                      or_ref, olw_ref, ok_ref, ov_ref, oa_ref, ob_ref, og_ref, obonus_ref, *, tiles_per_batch):
    i = pl.program_id(0)
    first = (i % tiles_per_batch) == 0

    def shifted(cur_ref, prv_ref, mu_ref):
        cur = cur_ref[...]
        last = jnp.where(first, 0.0, prv_ref[7:8, :])
        row = lax.broadcasted_iota(jnp.int32, cur.shape, 0)
        prev = jnp.where(row == 0, last, pltpu.roll(cur, 1, 0))
        return cur + (prev - cur) * mu_ref[...]

    r = shifted(r_ref, pr_ref, mu_r)
    k = shifted(k_ref, pk_ref, mu_k)
    v = shifted(v_ref, pv_ref, mu_v)
    gd = shifted(gd_ref, pgd_ref, mu_gd)
    lo = shifted(lo_ref, plo_ref, mu_lo)
    wd = lo[:, :LANES]
    ad = lo[:, LANES:]

    w_pre = w0_ref[...] + _dot_bf16x3(jnp.tanh(wd), wup_ref[...])
    w_raw = _log_sigmoid(w_pre) - 0.5
    log_decay = -jnp.exp(w_raw)
    a = jax.nn.sigmoid(a0_ref[...] + _dot_bf16x3(ad, aup_ref[...]))
    g = _dot_bf16x3(jax.nn.sigmoid(gd), gup_ref[...])

    bd = _head_block_diag()
    kk = k * kk_ref[...]
    nrm = jnp.maximum(jnp.sqrt(_head_sum(kk * kk, bd)), 1e-12)
    kk = kk / nrm
    k_mod = k * (1.0 + (a - 1.0) * ka_ref[...])
    bonus = _head_sum(r * k_mod * rk_ref[...], bd) * v

    or_ref[...] = r.astype(BF16)
    olw_ref[...] = log_decay
    ok_ref[...] = k_mod.astype(BF16)
    ov_ref[...] = v.astype(BF16)
    oa_ref[...] = (-kk).astype(BF16)
    ob_ref[...] = (kk * a).astype(BF16)
    og_ref[...] = g.astype(BF16)
    obonus_ref[...] = bonus.astype(BF16)


def _rwkv_prep(proj, cols, mus, w0, w_up, a0, a_up, g_up, k_k, k_a, r_k, tokens_per_batch):
    t = proj.shape[0]
    w = RWKV_HEADS * RWKV_HEAD_DIM
    tm = min(512, tokens_per_batch)
    tpb = tokens_per_batch // tm
    widths = [w, w, w, 256, 256]
    offs = [cols["rr"], cols["rk"], cols["rv"], cols["gd"], cols["lo"]]
    cur_specs = [pl.BlockSpec((tm, wd), functools.partial(lambda i, cb: (i, cb), cb=o // wd))
                 for wd, o in zip(widths, offs)]
    prv_specs = [pl.BlockSpec((8, wd), functools.partial(
        lambda i, cb: (jnp.maximum(i * (tm // 8) - 1, 0), cb), cb=o // wd))
        for wd, o in zip(widths, offs)]
    full = lambda a: pl.BlockSpec(a.shape, lambda i: (0,) * a.ndim)
    params = list(mus) + [w0, w_up, a0, a_up, g_up, k_k, k_a, r_k]
    out_spec = pl.BlockSpec((tm, w), lambda i: (i, 0))
    return pl.pallas_call(
        functools.partial(_rwkv_prep_kernel, tiles_per_batch=tpb),
        grid=(t // tm,),
        in_specs=cur_specs + prv_specs + [full(p) for p in params],
        out_specs=[out_spec] * 8,
        out_shape=[jax.ShapeDtypeStruct((t, w), dt) for dt in (BF16, F32, BF16, BF16, BF16, BF16, BF16, BF16)],
        compiler_params=_cparams(("parallel",)),
        name="rwkv_prep",
    )(*([proj] * 10), *params)


def _rwkv_scan_kernel(r_ref, lw_ref, k_ref, v_ref, a_ref, b_ref, o_ref, h_scr):
    c = pl.program_id(1)

    @pl.when(c == 0)
    def _():
        h_scr[...] = jnp.zeros(h_scr.shape, F32)

    C = r_ref.shape[0]
    n_pairs = r_ref.shape[1] // LANES
    P = HIGHEST

    row = lax.broadcasted_iota(jnp.int32, (C, C), 0)
    col = lax.broadcasted_iota(jnp.int32, (C, C), 1)
    tri = (row >= col).astype(F32)
    lw = lw_ref[...]
    cw = _dot(tri, lw, P)
    cw_end = cw[C - 1:C, :]
    e_pos = jnp.exp(cw)
    e_prev = jnp.exp(cw - lw)
    e_neg = jnp.exp(-cw)
    e_end = jnp.exp(cw_end - cw)
    w_end = jnp.exp(cw_end)

    a = a_ref[...].astype(F32)
    b = b_ref[...].astype(F32)
    k = k_ref[...].astype(F32)
    r = r_ref[...].astype(F32)
    at = a * e_prev
    bt = b * e_neg
    kt = k * e_neg
    rt = r * e_pos
    bh = b * e_end
    kh = k * e_end
    v = v_ref[...].astype(F32)

    lane = lax.broadcasted_iota(jnp.int32, (C, LANES), 1)
    head0 = lane < RWKV_HEAD_DIM
    r2 = lax.broadcasted_iota(jnp.int32, (2 * C, 2 * C), 0)
    c2 = lax.broadcasted_iota(jnp.int32, (2 * C, 2 * C), 1)
    same = (r2 // C) == (c2 // C)
    strict = same & (r2 > c2)
    incl = same & (r2 >= c2)
    eye = (r2 == c2).astype(F32)

    def two(x, p):
        xp = x[:, p * LANES:(p + 1) * LANES]
        return jnp.concatenate([jnp.where(head0, xp, 0.0), jnp.where(head0, 0.0, xp)], axis=0)

    def mm(x, y):
        return _dot(x.astype(BF16), y.astype(BF16))

    def mm3(x, y):
        xh, xl = _split_bf16(x)
        yh, yl = _split_bf16(y)
        n = y.shape[1]
        wide = _dot(xh, jnp.concatenate([yh, yl], axis=1))
        return wide[:, :n] + (wide[:, n:] + _dot(xl, yh))

    G = 2 * C
    pairs = range(n_pairs)
    at2 = [two(at, p) for p in pairs]
    rt2 = [two(rt, p) for p in pairs]
    v2 = [two(v, p).astype(BF16) for p in pairs]
    gram = [_dot_nt(jnp.concatenate([at2[p], rt2[p]], axis=0).astype(BF16),
                    jnp.concatenate([two(bt, p), two(kt, p)], axis=0).astype(BF16)) for p in pairs]
    a_ab = [jnp.where(strict, gram[p][:G, :G], 0.0) for p in pairs]
    nmat = list(a_ab)
    pw = list(a_ab)
    for _ in range(max(1, (C - 1).bit_length() - 1)):
        pw = [mm3(pw[p], pw[p]) for p in pairs]
        nmat = [nmat[p] + pw[p] + mm3(nmat[p], pw[p]) for p in pairs]
    akv = [mm(jnp.where(strict, gram[p][:G, G:], 0.0), v2[p]) for p in pairs]
    rhs = [jnp.concatenate([at2[p], akv[p]], axis=1) for p in pairs]
    pq = [(rhs[p] + mm(nmat[p], rhs[p])).astype(BF16) for p in pairs]
    ry = [mm(jnp.where(incl, gram[p][G:, :G], 0.0), pq[p]) for p in pairs]
    mv = [mm(jnp.where(incl, gram[p][G:, G:], 0.0), v2[p]) for p in pairs]
    ge = [_dot_tn(two(bh, p).astype(BF16), pq[p]) for p in pairs]
    kv = [_dot_tn(two(kh, p).astype(BF16), v2[p]) for p in pairs]
    for p in pairs:
        rr = rt2[p] + ry[p][:, :LANES]
        gm = eye * w_end[:, p * LANES:(p + 1) * LANES] + ge[p][:, :LANES]
        yh = mm(jnp.concatenate([rr, gm], axis=0), h_scr[p])
        y2 = yh[:G] + ry[p][:, LANES:] + mv[p]
        h_scr[p] = yh[G:] + ge[p][:, LANES:] + kv[p]
        o_ref[:, p * LANES:(p + 1) * LANES] = y2[:C, :] + y2[C:, :]


def _rwkv_scan(r, lw, k, v, a, b, batch, seq):
    w = r.shape[1]
    nc = seq // CHUNK
    spec = pl.BlockSpec((CHUNK, w), lambda bi, ci: (bi * nc + ci, 0))
    return pl.pallas_call(
        _rwkv_scan_kernel,
        grid=(batch, nc),
        in_specs=[spec] * 6,
        out_specs=spec,
        out_shape=jax.ShapeDtypeStruct((batch * seq, w), F32),
        scratch_shapes=[pltpu.VMEM((w // LANES, LANES, LANES), F32)],
        compiler_params=_cparams(("parallel", "arbitrary")),
        name="rwkv_scan",
    )(r, lw, k, v, a, b)


def _mix_out_kernel(o_ref, gate_ref, y_ref, bonus_ref, g_ref, x_ref, g1_ref, lnw_ref, lnb_ref, w_ref, out_ref):
    fox = o_ref[...] * jax.nn.sigmoid(gate_ref[...])
    bd = _head_block_diag()
    y = y_ref[...]
    inv_n = 1.0 / RWKV_HEAD_DIM
    mean = _head_sum(y, bd) * inv_n
    d = y - mean
    var = _head_sum(d * d, bd) * inv_n
    yn = d * lax.rsqrt(var + GN_EPS) * lnw_ref[...] + lnb_ref[...]
    rw = (yn + bonus_ref[...].astype(F32)) * g_ref[...].astype(F32)
    wf = fox.shape[1]
    mix = _dot(fox.astype(BF16), w_ref[:wf, :]) + _dot(rw.astype(BF16), w_ref[wf:, :])
    out_ref[...] = x_ref[...] + g1_ref[0] * mix


def _mix_out(o_fox, proj, gate_col, y, bonus, g, x2, g1, ln_w, ln_b, w_out_bf, tokens_per_batch):
    t, d = x2.shape
    wf = o_fox.shape[1]
    wr = y.shape[1]
    tm = min(512, tokens_per_batch)
    tpb = tokens_per_batch // tm
    return pl.pallas_call(
        _mix_out_kernel,
        grid=(t // tm,),
        in_specs=[pl.BlockSpec((tm, wf), lambda i: (i, 0)),
                  pl.BlockSpec((tm, wf), lambda i: (i, gate_col // wf)),
                  pl.BlockSpec((tm, wr), lambda i: (i, 0)),
                  pl.BlockSpec((tm, wr), lambda i: (i, 0)),
                  pl.BlockSpec((tm, wr), lambda i: (i, 0)),
                  pl.BlockSpec((tm, d), lambda i: (i, 0)),
                  pl.BlockSpec((1, 1, d), lambda i: (i // tpb, 0, 0)),
                  pl.BlockSpec((1, wr), lambda i: (0, 0)),
                  pl.BlockSpec((1, wr), lambda i: (0, 0)),
                  pl.BlockSpec(w_out_bf.shape, lambda i: (0, 0))],
        out_specs=pl.BlockSpec((tm, d), lambda i: (i, 0)),
        out_shape=jax.ShapeDtypeStruct((t, d), F32),
        compiler_params=_cparams(("parallel",)),
        name="mix_out",
    )(o_fox, proj, y, bonus, g, x2, g1, ln_w, ln_b, w_out_bf)


def _top_k_mask_rows(s, k, tie_safe, want_rank=True):
    n = s.shape[0]
    s0 = s
    row = lax.broadcasted_iota(jnp.int32, s.shape, 0).astype(F32)
    rank = jnp.full(s.shape, float(k), F32) if want_rank else None
    vals = []
    for r in range(k):
        m = jnp.max(s, axis=0, keepdims=True)
        hit = s == m
        if tie_safe:
            pos = jnp.min(jnp.where(hit, row, float(n)), axis=0, keepdims=True)
            hit = row == pos
        vals.append(m)
        if want_rank:
            rank = jnp.where(hit, float(r), rank)
        s = jnp.where(hit, -jnp.inf, s)
    picked = (s == -jnp.inf) & (s0 != -jnp.inf)
    n_picked = jnp.sum(jnp.where(picked, 1.0, 0.0), axis=0, keepdims=True)
    return jnp.concatenate(vals, axis=0), rank, picked, jnp.where(n_picked == float(k), 1.0, 0.0)


def _candidate_rows():
    k = PEER_TOPK
    groups, valid = [], []
    for r0 in range(k // 2):
        n1 = k // (r0 + 1)
        for g in range(-(-n1 // 8)):
            groups.append((r0, g * 8))
            valid.append([g * 8 + i < n1 for i in range(8)])
    groups.append((None, k // 2))
    valid.append([True] * 8)
    return groups, valid


def _route_head(sa, sb, tie_safe):
    k = PEER_TOPK
    tm = sa.shape[1]
    groups, valid = _candidate_rows()
    sub8 = lax.broadcasted_iota(jnp.int32, (8, tm), 0)
    top_a, rank_a, _, clean_a = _top_k_mask_rows(sa, k, tie_safe)
    top_b, rank_b, _, clean_b = _top_k_mask_rows(sb, k, tie_safe)
    parts = []
    for (r0, off), ok in zip(groups, valid):
        if r0 is None:
            part = top_a[off:off + 8, :] + top_b[0:1, :]
        else:
            part = top_a[r0:r0 + 1, :] + top_b[off:off + 8, :]
        if not all(ok):
            part = jnp.where(sub8 < sum(ok), part, -jnp.inf)
        parts.append(part)
    cand = jnp.concatenate(parts, axis=0)
    _, _, sel, clean_c = _top_k_mask_rows(cand, k, tie_safe, want_rank=False)
    z = jnp.sum(jnp.where(sel, jnp.exp(cand - cand[0:1, :]), 0.0), axis=0, keepdims=True)
    self32 = jnp.where(sel, 1.0, 0.0)
    n_keys = sa.shape[0]
    rank3 = rank_a.astype(BF16).reshape(n_keys // BF16_ROWS, BF16_ROWS, tm)
    length3 = jnp.zeros(rank3.shape, BF16)

    def add_count(r0, count_row):
        tile = jnp.broadcast_to(count_row, (BF16_ROWS, tm)).astype(BF16)
        return length3 + jnp.where(rank3 == float(r0), tile[None], 0.0).astype(BF16)

    tail = None
    per_rank = {}
    for gi, (r0, off) in enumerate(groups):
        cnt = self32[gi * 8:(gi + 1) * 8, :]
        if r0 is None:
            tail = cnt
        else:
            c = jnp.sum(cnt, axis=0, keepdims=True)
            per_rank[r0] = c if r0 not in per_rank else per_rank[r0] + c
    for r in range(8):
        per_rank[k // 2 + r] = tail[r:r + 1, :]
    for r0, c in per_rank.items():
        length3 = add_count(r0, c)
    length = length3.reshape(n_keys, tm).astype(F32)
    pa = jnp.exp(sa - top_a[0:1, :])
    qb = jnp.exp(sb - top_b[0:1, :]) / z
    return pa, length, qb, rank_b, clean_a * clean_b * clean_c


def _peer_query_kernel(x_ref, nw_ref, sc_ref, sh_ref, wq_ref, ht_ref, q_ref):
    h = _modulated_norm(x_ref[...], nw_ref[...], sc_ref[0], sh_ref[0])
    ht_ref[...] = h.T.astype(BF16)
    q = _dot(h.astype(BF16), wq_ref[...])
    dk = q.shape[1] // PEER_HEADS
    for hd in range(PEER_HEADS):
        q_ref[hd] = q[:, hd * dk:(hd + 1) * dk]


def _peer_query(x1, norm_w, sc, sh, wq_bf, tokens_per_batch):
    t, d = x1.shape
    dk = wq_bf.shape[1] // PEER_HEADS
    tm = min(512, tokens_per_batch)
    tpb = tokens_per_batch // tm
    return pl.pallas_call(
        _peer_query_kernel,
        grid=(t // tm,),
        in_specs=[pl.BlockSpec((tm, d), lambda i: (i, 0)),
                  pl.BlockSpec((1, d), lambda i: (0, 0)),
                  pl.BlockSpec((1, 1, d), lambda i: (i // tpb, 0, 0)),
                  pl.BlockSpec((1, 1, d), lambda i: (i // tpb, 0, 0)),
                  pl.BlockSpec(wq_bf.shape, lambda i: (0, 0))],
        out_specs=[pl.BlockSpec((d, tm), lambda i: (0, i)),
                   pl.BlockSpec((PEER_HEADS, tm, dk), lambda i: (0, i, 0))],
        out_shape=[jax.ShapeDtypeStruct((d, t), BF16),
                   jax.ShapeDtypeStruct((PEER_HEADS, t, dk), F32)],
        compiler_params=_cparams(("parallel",)),
        name="peer_query",
    )(x1, norm_w, sc, sh, wq_bf)


def _gelu_exact(x):
    return 0.5 * x * (1.0 + lax.erf(x * 0.7071067811865476))


def _peer_route_act_kernel(ht_ref, q_ref, keys_ref, u_ref,
                           act_ref, pa_ref, len_ref, qb_ref, rnk_ref, code_scr):
    hd = pl.program_id(1)
    tm = ht_ref.shape[1]
    half = keys_ref.shape[-1]
    sub = min(ROUTE_SUB_TOKENS, tm)
    blocked = (PEER_NKEYS // PEER_SLABS, PEER_SLABS, sub)

    @pl.when(hd == 0)
    def _():
        code_scr[...] = jnp.zeros(code_scr.shape, F32)

    def scores(h, t0):
        q = q_ref[h, t0:t0 + sub, :]
        return (_dot_nt(keys_ref[h, 0], q[:, :half], HIGHEST),
                _dot_nt(keys_ref[h, 1], q[:, half:], HIGHEST))

    def route(h, t0, sa, sb, tie_safe):
        pa, length, qbt, rank_b, clean = _route_head(sa, sb, tie_safe)
        pa_ref[h, :, :, t0:t0 + sub] = pa.reshape(blocked)
        len_ref[h, :, :, t0:t0 + sub] = length.reshape(blocked)
        qb_ref[h, :, t0:t0 + sub] = qbt.astype(BF16)
        rnk_ref[h, :, t0:t0 + sub] = rank_b.astype(BF16)
        return clean

    starts = range(0, tm, sub)
    score_tiles = [scores(hd, t0) for t0 in starts]
    rows_per_dot = 512
    for r0 in range(0, u_ref.shape[0], rows_per_dot):
        act_ref[r0:r0 + rows_per_dot, :] = _dot(u_ref[r0:r0 + rows_per_dot, :], ht_ref[...]).astype(BF16)

    dirty = jnp.zeros((1, 1), F32)
    for t0, (sa, sb) in zip(starts, score_tiles):
        dirty = jnp.maximum(dirty, 1.0 - jnp.min(route(hd, t0, sa, sb, False), axis=1, keepdims=True))
    code_scr[0:1, 0:1] += dirty * lax.shift_left(1, hd).astype(F32)

    @pl.when(hd == pl.num_programs(1) - 1)
    def _():
        tie_bits = code_scr[0, 0].astype(jnp.int32)

        def redo(h, carry):
            @pl.when(((tie_bits >> h) & 1) == 1)
            def _():
                for t0 in starts:
                    route(h, t0, *scores(h, t0), True)
            return carry

        lax.fori_loop(0, PEER_HEADS, redo, 0)


def _peer_route_act(ht_bf, q_heads, sub_keys, u_bf, tokens_per_batch):
    d, t = ht_bf.shape
    ne = u_bf.shape[0]
    bu = ne // PEER_HEADS
    dk = q_heads.shape[-1]
    tm = min(512, tokens_per_batch)
    n_blocks = PEER_NKEYS // PEER_SLABS
    dense = jax.ShapeDtypeStruct((PEER_HEADS, PEER_NKEYS, t), BF16)
    dense_spec = pl.BlockSpec((PEER_HEADS, PEER_NKEYS, tm), lambda i, j: (0, 0, i))
    blocked = jax.ShapeDtypeStruct((PEER_HEADS, n_blocks, PEER_SLABS, t), F32)
    blocked_spec = pl.BlockSpec((PEER_HEADS, n_blocks, PEER_SLABS, tm), lambda i, j: (0, 0, 0, i))
    return pl.pallas_call(
        _peer_route_act_kernel,
        grid=(t // tm, PEER_HEADS),
        in_specs=[pl.BlockSpec((d, tm), lambda i, j: (0, i)),
                  pl.BlockSpec((PEER_HEADS, tm, dk), lambda i, j: (0, i, 0)),
                  pl.BlockSpec(sub_keys.shape, lambda i, j: (0, 0, 0, 0)),
                  pl.BlockSpec((bu, d), lambda i, j: (j, 0))],
        out_specs=[pl.BlockSpec((bu, tm), lambda i, j: (j, i)),
                   blocked_spec, blocked_spec, dense_spec, dense_spec],
        out_shape=[jax.ShapeDtypeStruct((ne, t), BF16), blocked, blocked, dense, dense],
        scratch_shapes=[pltpu.VMEM((8, LANES), F32)],
        compiler_params=_cparams(("parallel", "arbitrary")),
        name="peer_route_act",
    )(ht_bf, q_heads, sub_keys, u_bf)


def _transpose_blocks_kernel(x_ref, o_ref):
    o_ref[0] = x_ref[...].T.astype(BF16)


def _transpose_blocks(w, block):
    n, d = w.shape
    return pl.pallas_call(
        _transpose_blocks_kernel,
        grid=(n // block,),
        in_specs=[pl.BlockSpec((block, d), lambda j: (j, 0))],
        out_specs=pl.BlockSpec((1, d, block), lambda j: (j, 0, 0)),
        out_shape=jax.ShapeDtypeStruct((n // block, d, block), BF16),
        compiler_params=_cparams(("parallel",)),
        name="transpose_blocks",
    )(w)


def _peer_expert_kernel(act_ref, vt_ref, pa_ref, len_ref, qb_ref, rnk_ref, x_ref, g2_ref, o_ref, acc_scr):
    j = pl.program_id(1)
    be, tm = act_ref.shape

    @pl.when(j == 0)
    def _():
        acc_scr[...] = jnp.zeros(acc_scr.shape, F32)

    n_slabs = be // PEER_NKEYS
    rows = BF16_ROWS
    n_groups = PEER_NKEYS // rows

    def row_tile(ref, hd, s):
        return jnp.broadcast_to(ref[hd, 0, s:s + 1, :], (rows, tm)).astype(BF16)

    zero = jnp.zeros((rows, tm), BF16)
    parts = [[None] * n_groups for _ in range(n_slabs)]
    slab_group = 8
    for s0 in range(0, n_slabs, slab_group):
        ss = range(s0, min(s0 + slab_group, n_slabs))
        ln = {(hd, s): row_tile(len_ref, hd, s) for hd in range(PEER_HEADS) for s in ss}
        pa = {(hd, s): row_tile(pa_ref, hd, s) for hd in range(PEER_HEADS) for s in ss}
        for g in range(n_groups):
            sl = slice(g * rows, (g + 1) * rows)
            acc = {}
            for hd in range(PEER_HEADS):
                rk = rnk_ref[hd, sl, :]
                qv = qb_ref[hd, sl, :]
                for s in ss:
                    term = jnp.where(rk < ln[hd, s], qv, zero) * pa[hd, s]
                    acc[s] = term if s not in acc else acc[s] + term
            for s in ss:
                lo = s * PEER_NKEYS + g * rows
                parts[s][g] = acc[s] * _gelu_exact(act_ref[lo:lo + rows, :].astype(F32)).astype(BF16)
    p = jnp.concatenate([parts[s][g] for s in range(n_slabs) for g in range(n_groups)], axis=0)
    acc_scr[...] += _dot(vt_ref[0], p)

    @pl.when(j == pl.num_programs(1) - 1)
    def _():
        o_ref[...] = x_ref[...] + g2_ref[0] * acc_scr[...].T


def _peer_expert(act, vt_bf, pa, ln, qb, rnk, x1, g2, tokens_per_batch):
    t, d = x1.shape
    ne = act.shape[0]
    tm = min(512, tokens_per_batch)
    tpb = tokens_per_batch // tm
    be = PEER_EXPERT_BLOCK
    dense_spec = pl.BlockSpec((PEER_HEADS, PEER_NKEYS, tm), lambda i, j: (0, 0, i))
    blocked_spec = pl.BlockSpec((PEER_HEADS, 1, PEER_SLABS, tm), lambda i, j: (0, j, 0, i))
    return pl.pallas_call(
        _peer_expert_kernel,
        grid=(t // tm, ne // be),
        in_specs=[pl.BlockSpec((be, tm), lambda i, j: (j, i)),
                  pl.BlockSpec((1, d, be), lambda i, j: (j, 0, 0)),
                  blocked_spec, blocked_spec, dense_spec, dense_spec,
                  pl.BlockSpec((tm, d), lambda i, j: (i, 0)),
                  pl.BlockSpec((1, 1, d), lambda i, j: (i // tpb, 0, 0))],
        out_specs=pl.BlockSpec((tm, d), lambda i, j: (i, 0)),
        out_shape=jax.ShapeDtypeStruct((t, d), F32),
        scratch_shapes=[pltpu.VMEM((d, tm), F32)],
        compiler_params=_cparams(("parallel", "arbitrary")),
        name="peer_expert",
    )(act, vt_bf, pa, ln, qb, rnk, x1, g2)


def _pad_cols(a, n):
    return jnp.pad(a, ((0, 0), (0, n - a.shape[1])))


def _pad_rows(a, n):
    return jnp.pad(a, ((0, n - a.shape[0]), (0, 0)))


def _layer(x, c, w_ada, b_ada, norm_mix_w, w_in, fox_q_norm_w, fox_k_norm_w, fox_f_bias,
           rwkv_mu, rwkv_w0, rwkv_w_up, rwkv_a0, rwkv_a_up, rwkv_g_up, rwkv_k_k, rwkv_k_a,
           rwkv_r_k, rwkv_ln_w, rwkv_ln_b, w_out, norm_ffn_w, peer_w_query, peer_sub_keys,
           peer_u, peer_v):
    B, S, D = x.shape
    T = B * S
    fw = FOX_HEADS * FOX_HEAD_DIM
    rw = RWKV_HEADS * RWKV_HEAD_DIM
    w_lora = rwkv_w_up.shape[0]
    a_lora = rwkv_a_up.shape[0]
    g_lora = rwkv_g_up.shape[0]
    assert w_lora <= LANES and a_lora + FOX_HEADS <= LANES and g_lora == 256

    c_pad = _pad_rows(c, 8)
    mod = _ada_mod(c_pad, w_ada, b_ada)[:B]
    sh1, sc1, g1, sh2, sc2, g2 = [m.reshape(B, 1, D) for m in jnp.split(mod, 6, axis=-1)]

    fox_cols = 4 * fw + FOX_HEADS
    wi_fox, wi_rw = w_in[:, :fox_cols], w_in[:, fox_cols:]
    mu = rwkv_mu.reshape(1, -1)
    seg = lambda a, lo, n: a[:, lo:lo + n]
    w_perm = jnp.concatenate([
        seg(wi_fox, 0, 4 * fw),
        seg(wi_rw, 0, 3 * rw),
        seg(wi_rw, 3 * rw + w_lora + a_lora, g_lora),
        _pad_cols(seg(wi_rw, 3 * rw, w_lora), LANES),
        _pad_cols(jnp.concatenate([seg(wi_rw, 3 * rw + w_lora, a_lora), seg(wi_fox, 4 * fw, FOX_HEADS)], 1), LANES),
    ], axis=1).astype(BF16)
    cols = {"gate": 3 * fw, "rr": 4 * fw, "rk": 4 * fw + rw, "rv": 4 * fw + 2 * rw,
            "gd": 4 * fw + 3 * rw, "lo": 4 * fw + 3 * rw + g_lora}
    f_lane = a_lora
    f_block = (cols["lo"] + LANES) // LANES
    mus = [seg(mu, 0, rw), seg(mu, rw, rw), seg(mu, 2 * rw, rw),
           seg(mu, 3 * rw + w_lora + a_lora, g_lora),
           jnp.concatenate([_pad_cols(seg(mu, 3 * rw, w_lora), LANES),
                            _pad_cols(seg(mu, 3 * rw + w_lora, a_lora), LANES)], 1)]
    scale = FOX_HEAD_DIM ** -0.5 * LOG2_E
    head_w = _pad_cols(jnp.concatenate([jnp.tile(fox_q_norm_w * scale, FOX_HEADS),
                                        jnp.tile(fox_k_norm_w, FOX_HEADS)]).reshape(1, -1), w_perm.shape[1])

    x2 = x.reshape(T, D)
    proj = _in_proj(_norm_mod(x2, norm_mix_w.reshape(1, D), sc1, sh1, S), w_perm, head_w, 2 * fw)

    f_bias_row = jnp.zeros((1, LANES), F32).at[0, f_lane:f_lane + FOX_HEADS].set(fox_f_bias)
    cum = _fox_cum(proj, f_bias_row, B, S, f_block)
    cum_rows = cum.reshape(B, S, LANES)[:, :, f_lane:f_lane + FOX_HEADS].transpose(0, 2, 1).reshape(B, FOX_HEADS, 1, S)
    o_fox = _fox_attn(proj, cum_rows, B, S)

    row = lambda a: a.reshape(1, -1)
    r, lw, k, v, a_vec, b_vec, g, bonus = _rwkv_prep(
        proj, cols, mus, row(rwkv_w0), _pad_rows(rwkv_w_up, LANES), row(rwkv_a0), _pad_rows(rwkv_a_up, LANES),
        rwkv_g_up, row(rwkv_k_k), row(rwkv_k_a), row(rwkv_r_k), S)
    y = _rwkv_scan(r, lw, k, v, a_vec, b_vec, B, S)

    x1 = _mix_out(o_fox, proj, cols["gate"], y, bonus, g, x2, g1, row(rwkv_ln_w), row(rwkv_ln_b),
                  w_out.astype(BF16), S)

    ht, q_heads = _peer_query(x1, norm_ffn_w.reshape(1, D), sc2, sh2, peer_w_query.astype(BF16), S)
    act, pa, ln, qb, rnk = _peer_route_act(ht, q_heads, peer_sub_keys, peer_u.astype(BF16), S)
    vt = _transpose_blocks(peer_v, PEER_EXPERT_BLOCK)
    out = _peer_expert(act, vt, pa, ln, qb, rnk, x1, g2, S)
    return out.reshape(B, S, D)


def kernel(x, c, w_ada, b_ada, norm_mix_w, w_in, fox_q_norm_w, fox_k_norm_w, fox_f_bias, rwkv_mu, rwkv_w0,
           rwkv_w_up, rwkv_a0, rwkv_a_up, rwkv_g_up, rwkv_k_k, rwkv_k_a, rwkv_r_k, rwkv_ln_w, rwkv_ln_b,
           w_out, norm_ffn_w, peer_w_query, peer_sub_keys, peer_u, peer_v):
    params = (w_ada, b_ada, norm_mix_w, w_in, fox_q_norm_w, fox_k_norm_w, fox_f_bias, rwkv_mu, rwkv_w0,
              rwkv_w_up, rwkv_a0, rwkv_a_up, rwkv_g_up, rwkv_k_k, rwkv_k_a, rwkv_r_k, rwkv_ln_w, rwkv_ln_b,
              w_out, norm_ffn_w, peer_w_query, peer_sub_keys, peer_u, peer_v)
    for l in range(w_ada.shape[0]):
        x = _layer(x, c, *[p[l] for p in params])
    return x
```

```python
import functools

import jax
import jax.numpy as jnp
from jax import lax
from jax.experimental import pallas as pl
from jax.experimental.pallas import tpu as pltpu

F32 = jnp.float32
BF16 = jnp.bfloat16
HIGHEST = lax.Precision.HIGHEST

LANES = 128
BF16_ROWS = 16
NORM_EPS = 1e-6
GN_EPS = 64e-5
CHUNK = 64
FOX_HEADS = 8
FOX_HEAD_DIM = 128
RWKV_HEADS = 16
RWKV_HEAD_DIM = 64
PEER_HEADS = 8
PEER_NKEYS = 128
PEER_TOPK = 16
PEER_EXPERT_BLOCK = 2048
PEER_SLABS = PEER_EXPERT_BLOCK // PEER_NKEYS
ROUTE_SUB_TOKENS = 256
NEG_BIG = -1e30
LOG2_E = 1.4426950408889634
VMEM_LIMIT = 56 * 1024 * 1024


def _cparams(sem):
    return pltpu.CompilerParams(dimension_semantics=sem, vmem_limit_bytes=VMEM_LIMIT)


def _dot(a, b, precision=None):
    return jnp.dot(a, b, preferred_element_type=F32, precision=precision)


def _dot_nt(a, b, precision=None):
    return lax.dot_general(a, b, (((1,), (1,)), ((), ())), preferred_element_type=F32,
                           precision=precision)


def _split_bf16(x):
    hi = x.astype(BF16)
    return hi, (x - hi.astype(F32)).astype(BF16)


def _dot_bf16x3(a, b):
    ah, al = _split_bf16(a)
    bh, bl = _split_bf16(b)
    return _dot(ah, bh) + (_dot(ah, bl) + _dot(al, bh))


def _dot_tn(a, b, precision=None):
    return lax.dot_general(a, b, (((0,), (0,)), ((), ())), preferred_element_type=F32,
                           precision=precision)


def _ada_kernel(c_ref, w_ref, b_ref, o_ref):
    c = c_ref[...]
    s = c * jax.nn.sigmoid(c)
    o_ref[...] = _dot(s, w_ref[...], HIGHEST) + b_ref[...]


def _ada_mod(c_pad, w_ada, b_ada):
    rows, d = c_pad.shape
    n = w_ada.shape[1]
    bn = 1024
    return pl.pallas_call(
        _ada_kernel,
        grid=(n // bn,),
        in_specs=[pl.BlockSpec((rows, d), lambda j: (0, 0)),
                  pl.BlockSpec((d, bn), lambda j: (0, j)),
                  pl.BlockSpec((1, bn), lambda j: (0, j))],
        out_specs=pl.BlockSpec((rows, bn), lambda j: (0, j)),
        out_shape=jax.ShapeDtypeStruct((rows, n), F32),
        compiler_params=_cparams(("arbitrary",)),
        name="ada_mod",
    )(c_pad, w_ada, b_ada.reshape(1, n))


def _modulated_norm(x, nw, sc, sh):
    y = x * lax.rsqrt(jnp.mean(x * x, axis=-1, keepdims=True) + NORM_EPS)
    return y * nw * (1.0 + sc) + sh


def _norm_mod_kernel(x_ref, nw_ref, sc_ref, sh_ref, o_ref):
    o_ref[...] = _modulated_norm(x_ref[...], nw_ref[...], sc_ref[0], sh_ref[0]).astype(BF16)


def _norm_mod(x2, norm_w, sc, sh, tokens_per_batch):
    t, d = x2.shape
    tm = min(512, tokens_per_batch)
    tpb = tokens_per_batch // tm
    return pl.pallas_call(
        _norm_mod_kernel,
        grid=(t // tm,),
        in_specs=[pl.BlockSpec((tm, d), lambda i: (i, 0)),
                  pl.BlockSpec((1, d), lambda i: (0, 0)),
                  pl.BlockSpec((1, 1, d), lambda i: (i // tpb, 0, 0)),
                  pl.BlockSpec((1, 1, d), lambda i: (i // tpb, 0, 0))],
        out_specs=pl.BlockSpec((tm, d), lambda i: (i, 0)),
        out_shape=jax.ShapeDtypeStruct((t, d), BF16),
        compiler_params=_cparams(("parallel",)),
        name="norm_mod",
    )(x2, norm_w, sc, sh)


def _in_proj_kernel(h_ref, w_ref, hw_ref, o_ref, *, n_qk_blocks):
    j = pl.program_id(1)
    acc = _dot(h_ref[...], w_ref[...])

    @pl.when(j < n_qk_blocks)
    def _():
        for hh in range(acc.shape[1] // FOX_HEAD_DIM):
            sl = slice(hh * FOX_HEAD_DIM, (hh + 1) * FOX_HEAD_DIM)
            a = acc[:, sl]
            rs = lax.rsqrt(jnp.mean(a * a, axis=-1, keepdims=True) + NORM_EPS)
            o_ref[:, sl] = a * rs * hw_ref[:, sl]

    @pl.when(j >= n_qk_blocks)
    def _():
        o_ref[...] = acc


def _in_proj(h_bf, w_bf, head_w, n_qk_cols):
    t, d = h_bf.shape
    n = w_bf.shape[1]
    tm = min(2048, t)
    bn = 512
    kern = functools.partial(_in_proj_kernel, n_qk_blocks=n_qk_cols // bn)
    return pl.pallas_call(
        kern,
        grid=(t // tm, n // bn),
        in_specs=[pl.BlockSpec((tm, d), lambda i, j: (i, 0)),
                  pl.BlockSpec((d, bn), lambda i, j: (0, j)),
                  pl.BlockSpec((1, bn), lambda i, j: (0, j))],
        out_specs=pl.BlockSpec((tm, bn), lambda i, j: (i, j)),
        out_shape=jax.ShapeDtypeStruct((t, n), F32),
        compiler_params=_cparams(("parallel", "arbitrary")),
        name="in_proj",
    )(h_bf, w_bf, head_w)


def _log_sigmoid(x):
    return jnp.minimum(x, 0.0) - jnp.log(1.0 + jnp.exp(-jnp.abs(x)))


def _fox_cum_kernel(f_ref, b_ref, o_ref, *, blk):
    s = f_ref.shape[0]
    row = lax.broadcasted_iota(jnp.int32, (blk, blk), 0)
    col = lax.broadcasted_iota(jnp.int32, (blk, blk), 1)
    tri = (row >= col).astype(F32)
    carry = jnp.zeros((1, f_ref.shape[1]), F32)
    for i in range(s // blk):
        lf = _log_sigmoid(f_ref[i * blk:(i + 1) * blk, :] + b_ref[...])
        cs = _dot(tri, lf, HIGHEST) + carry
        o_ref[i * blk:(i + 1) * blk, :] = cs * LOG2_E
        carry = cs[blk - 1:blk, :]


def _fox_cum(proj, f_bias_row, batch, seq, col_block):
    blk = min(256, seq)
    return pl.pallas_call(
        functools.partial(_fox_cum_kernel, blk=blk),
        grid=(batch,),
        in_specs=[pl.BlockSpec((seq, LANES), lambda b: (b, col_block)),
                  pl.BlockSpec((1, LANES), lambda b: (0, 0))],
        out_specs=pl.BlockSpec((seq, LANES), lambda b: (b, 0)),
        out_shape=jax.ShapeDtypeStruct((batch * seq, LANES), F32),
        compiler_params=_cparams(("parallel",)),
        name="fox_cum",
    )(proj, f_bias_row)


def _fox_attn_kernel(qt_ref, kt_ref, q_ref, k_ref, v_ref, c_ref, o_ref, m_scr, l_scr, acc_scr):
    t = pl.program_id(2)
    qi = qt_ref[t]
    ki = kt_ref[t]
    hd = FOX_HEAD_DIM
    n_heads = q_ref.shape[1] // hd

    @pl.when(ki == 0)
    def _():
        m_scr[...] = jnp.full(m_scr.shape, NEG_BIG, F32)
        l_scr[...] = jnp.zeros(l_scr.shape, F32)
        acc_scr[...] = jnp.zeros(acc_scr.shape, F32)

    def step(masked):
        hs = range(n_heads)
        sl = [slice(h * hd, (h + 1) * hd) for h in hs]
        s = [_dot_nt(q_ref[:, sl[h]].astype(BF16), k_ref[:, sl[h]].astype(BF16)) - c_ref[0, h] for h in hs]
        if masked:
            row = lax.broadcasted_iota(jnp.int32, s[0].shape, 0)
            col = lax.broadcasted_iota(jnp.int32, s[0].shape, 1)
            s = [jnp.where(row >= col, x, NEG_BIG) for x in s]
        m_prev = [m_scr[h] for h in hs]
        m_new = [jnp.maximum(m_prev[h], jnp.max(s[h], axis=-1, keepdims=True)) for h in hs]
        n_rep = s[0].shape[1] // hd
        p = [jnp.exp2(s[h] - jnp.concatenate([m_new[h]] * n_rep, axis=1)) for h in hs]
        alpha = [jnp.exp2(m_prev[h] - m_new[h]) for h in hs]
        ones = jnp.ones((k_ref.shape[0], hd), BF16)
        pv = [_dot(p[h].astype(BF16), jnp.concatenate([v_ref[:, sl[h]].astype(BF16), ones], axis=1))
              for h in hs]
        for h in hs:
            l_scr[h] = alpha[h] * l_scr[h] + pv[h][:, hd:]
            acc_scr[:, sl[h]] = alpha[h] * acc_scr[:, sl[h]] + pv[h][:, :hd]
            m_scr[h] = m_new[h]

    @pl.when(ki < qi)
    def _():
        step(False)

    @pl.when(ki == qi)
    def _():
        step(True)
        for h in range(n_heads):
            sl = slice(h * hd, (h + 1) * hd)
            o_ref[:, sl] = acc_scr[:, sl] / l_scr[h]


def _fox_attn(proj, cum_rows, batch, seq):
    tq = min(512, seq)
    nq = seq // tq
    hps = 8
    w = hps * FOX_HEAD_DIM
    kcol = FOX_HEADS // hps
    tri = [(q, k) for q in range(nq) for k in range(q + 1)]
    qt = jnp.asarray([q for q, _ in tri], jnp.int32)
    kt = jnp.asarray([k for _, k in tri], jnp.int32)
    grid_spec = pltpu.PrefetchScalarGridSpec(
        num_scalar_prefetch=2,
        grid=(batch, kcol, len(tri)),
        in_specs=[
            pl.BlockSpec((tq, w), lambda b, h, t, qt, kt: (b * nq + qt[t], h)),
            pl.BlockSpec((tq, w), lambda b, h, t, qt, kt: (b * nq + kt[t], kcol + h)),
            pl.BlockSpec((tq, w), lambda b, h, t, qt, kt: (b * nq + kt[t], 2 * kcol + h)),
            pl.BlockSpec((1, hps, 1, tq), lambda b, h, t, qt, kt: (b, h, 0, kt[t])),
        ],
        out_specs=pl.BlockSpec((tq, w), lambda b, h, t, qt, kt: (b * nq + qt[t], h)),
        scratch_shapes=[pltpu.VMEM((hps, tq, FOX_HEAD_DIM), F32), pltpu.VMEM((hps, tq, FOX_HEAD_DIM), F32),
                        pltpu.VMEM((tq, w), F32)],
    )
    return pl.pallas_call(
        _fox_attn_kernel,
        grid_spec=grid_spec,
        out_shape=jax.ShapeDtypeStruct((batch * seq, FOX_HEADS * FOX_HEAD_DIM), F32),
        compiler_params=_cparams(("parallel", "parallel", "arbitrary")),
        name="fox_attn",
    )(qt, kt, proj, proj, proj, cum_rows)


def _head_sum(x, bd):
    hi, lo = _split_bf16(x)
    parts = [_dot(hi[:, j * LANES:(j + 1) * LANES], bd) + _dot(lo[:, j * LANES:(j + 1) * LANES], bd)
             for j in range(x.shape[1] // LANES)]
    return jnp.concatenate(parts, axis=1)


def _head_block_diag():
    r = lax.broadcasted_iota(jnp.int32, (LANES, LANES), 0) // RWKV_HEAD_DIM
    c = lax.broadcasted_iota(jnp.int32, (LANES, LANES), 1) // RWKV_HEAD_DIM
    return (r == c).astype(BF16)


def _rwkv_prep_kernel(r_ref, k_ref, v_ref, gd_ref, lo_ref, pr_ref, pk_ref, pv_ref, pgd_ref, plo_ref,
                      mu_r, mu_k, mu_v, mu_gd, mu_lo, w0_ref, wup_ref, a0_ref, aup_ref, gup_ref,
                      kk_ref, ka_ref, rk_ref,
                      or_ref, olw_ref, ok_ref, ov_ref, oa_ref, ob_ref, og_ref, obonus_ref, *, tiles_per_batch):
    i = pl.program_id(0)
    first = (i % tiles_per_batch) == 0

    def shifted(cur_ref, prv_ref, mu_ref):
        cur = cur_ref[...]
        last = jnp.where(first, 0.0, prv_ref[7:8, :])
        row = lax.broadcasted_iota(jnp.int32, cur.shape, 0)
        prev = jnp.where(row == 0, last, pltpu.roll(cur, 1, 0))
        return cur + (prev - cur) * mu_ref[...]

    r = shifted(r_ref, pr_ref, mu_r)
    k = shifted(k_ref, pk_ref, mu_k)
    v = shifted(v_ref, pv_ref, mu_v)
    gd = shifted(gd_ref, pgd_ref, mu_gd)
    lo = shifted(lo_ref, plo_ref, mu_lo)
    wd = lo[:, :LANES]
    ad = lo[:, LANES:]

    w_pre = w0_ref[...] + _dot_bf16x3(jnp.tanh(wd), wup_ref[...])
    w_raw = _log_sigmoid(w_pre) - 0.5
    log_decay = -jnp.exp(w_raw)
    a = jax.nn.sigmoid(a0_ref[...] + _dot_bf16x3(ad, aup_ref[...]))
    g = _dot_bf16x3(jax.nn.sigmoid(gd), gup_ref[...])

    bd = _head_block_diag()
    kk = k * kk_ref[...]
    nrm = jnp.maximum(jnp.sqrt(_head_sum(kk * kk, bd)), 1e-12)
    kk = kk / nrm
    k_mod = k * (1.0 + (a - 1.0) * ka_ref[...])
    bonus = _head_sum(r * k_mod * rk_ref[...], bd) * v

    or_ref[...] = r.astype(BF16)
    olw_ref[...] = log_decay
    ok_ref[...] = k_mod.astype(BF16)
    ov_ref[...] = v.astype(BF16)
    oa_ref[...] = (-kk).astype(BF16)
    ob_ref[...] = (kk * a).astype(BF16)
    og_ref[...] = g.astype(BF16)
    obonus_ref[...] = bonus.astype(BF16)


def _rwkv_prep(proj, cols, mus, w0, w_up, a0, a_up, g_up, k_k, k_a, r_k, tokens_per_batch):
    t = proj.shape[0]
    w = RWKV_HEADS * RWKV_HEAD_DIM
    tm = min(512, tokens_per_batch)
    tpb = tokens_per_batch // tm
    widths = [w, w, w, 256, 256]
    offs = [cols["rr"], cols["rk"], cols["rv"], cols["gd"], cols["lo"]]
    cur_specs = [pl.BlockSpec((tm, wd), functools.partial(lambda i, cb: (i, cb), cb=o // wd))
                 for wd, o in zip(widths, offs)]
    prv_specs = [pl.BlockSpec((8, wd), functools.partial(
        lambda i, cb: (jnp.maximum(i * (tm // 8) - 1, 0), cb), cb=o // wd))
        for wd, o in zip(widths, offs)]
    full = lambda a: pl.BlockSpec(a.shape, lambda i: (0,) * a.ndim)
    params = list(mus) + [w0, w_up, a0, a_up, g_up, k_k, k_a, r_k]
    out_spec = pl.BlockSpec((tm, w), lambda i: (i, 0))
    return pl.pallas_call(
        functools.partial(_rwkv_prep_kernel, tiles_per_batch=tpb),
        grid=(t // tm,),
        in_specs=cur_specs + prv_specs + [full(p) for p in params],
        out_specs=[out_spec] * 8,
        out_shape=[jax.ShapeDtypeStruct((t, w), dt) for dt in (BF16, F32, BF16, BF16, BF16, BF16, BF16, BF16)],
        compiler_params=_cparams(("parallel",)),
        name="rwkv_prep",
    )(*([proj] * 10), *params)


def _rwkv_scan_kernel(r_ref, lw_ref, k_ref, v_ref, a_ref, b_ref, o_ref, h_scr):
    c = pl.program_id(1)

    @pl.when(c == 0)
    def _():
        h_scr[...] = jnp.zeros(h_scr.shape, F32)

    C = r_ref.shape[0]
    n_pairs = r_ref.shape[1] // LANES
    P = HIGHEST

    row = lax.broadcasted_iota(jnp.int32, (C, C), 0)
    col = lax.broadcasted_iota(jnp.int32, (C, C), 1)
    tri = (row >= col).astype(F32)
    lw = lw_ref[...]
    cw = _dot(tri, lw, P)
    cw_end = cw[C - 1:C, :]
    e_pos = jnp.exp(cw)
    e_prev = jnp.exp(cw - lw)
    e_neg = jnp.exp(-cw)
    e_end = jnp.exp(cw_end - cw)
    w_end = jnp.exp(cw_end)

    a = a_ref[...].astype(F32)
    b = b_ref[...].astype(F32)
    k = k_ref[...].astype(F32)
    r = r_ref[...].astype(F32)
    at = a * e_prev
    bt = b * e_neg
    kt = k * e_neg
    rt = r * e_pos
    bh = b * e_end
    kh = k * e_end
    v = v_ref[...].astype(F32)

    lane = lax.broadcasted_iota(jnp.int32, (C, LANES), 1)
    head0 = lane < RWKV_HEAD_DIM
    r2 = lax.broadcasted_iota(jnp.int32, (2 * C, 2 * C), 0)
    c2 = lax.broadcasted_iota(jnp.int32, (2 * C, 2 * C), 1)
    same = (r2 // C) == (c2 // C)
    strict = same & (r2 > c2)
    incl = same & (r2 >= c2)
    eye = (r2 == c2).astype(F32)

    def two(x, p):
        xp = x[:, p * LANES:(p + 1) * LANES]
        return jnp.concatenate([jnp.where(head0, xp, 0.0), jnp.where(head0, 0.0, xp)], axis=0)

    def mm(x, y):
        return _dot(x.astype(BF16), y.astype(BF16))

    def mm3(x, y):
        xh, xl = _split_bf16(x)
        yh, yl = _split_bf16(y)
        n = y.shape[1]
        wide = _dot(xh, jnp.concatenate([yh, yl], axis=1))
        return wide[:, :n] + (wide[:, n:] + _dot(xl, yh))

    G = 2 * C
    pairs = range(n_pairs)
    at2 = [two(at, p) for p in pairs]
    rt2 = [two(rt, p) for p in pairs]
    v2 = [two(v, p).astype(BF16) for p in pairs]
    gram = [_dot_nt(jnp.concatenate([at2[p], rt2[p]], axis=0).astype(BF16),
                    jnp.concatenate([two(bt, p), two(kt, p)], axis=0).astype(BF16)) for p in pairs]
    a_ab = [jnp.where(strict, gram[p][:G, :G], 0.0) for p in pairs]
    nmat = list(a_ab)
    pw = list(a_ab)
    for _ in range(max(1, (C - 1).bit_length() - 1)):
        pw = [mm3(pw[p], pw[p]) for p in pairs]
        nmat = [nmat[p] + pw[p] + mm3(nmat[p], pw[p]) for p in pairs]
    akv = [mm(jnp.where(strict, gram[p][:G, G:], 0.0), v2[p]) for p in pairs]
    rhs = [jnp.concatenate([at2[p], akv[p]], axis=1) for p in pairs]
    pq = [(rhs[p] + mm(nmat[p], rhs[p])).astype(BF16) for p in pairs]
    ry = [mm(jnp.where(incl, gram[p][G:, :G], 0.0), pq[p]) for p in pairs]
    mv = [mm(jnp.where(incl, gram[p][G:, G:], 0.0), v2[p]) for p in pairs]
    ge = [_dot_tn(two(bh, p).astype(BF16), pq[p]) for p in pairs]
    kv = [_dot_tn(two(kh, p).astype(BF16), v2[p]) for p in pairs]
    for p in pairs:
        rr = rt2[p] + ry[p][:, :LANES]
        gm = eye * w_end[:, p * LANES:(p + 1) * LANES] + ge[p][:, :LANES]
        yh = mm(jnp.concatenate([rr, gm], axis=0), h_scr[p])
        y2 = yh[:G] + ry[p][:, LANES:] + mv[p]
        h_scr[p] = yh[G:] + ge[p][:, LANES:] + kv[p]
        o_ref[:, p * LANES:(p + 1) * LANES] = y2[:C, :] + y2[C:, :]


def _rwkv_scan(r, lw, k, v, a, b, batch, seq):
    w = r.shape[1]
    nc = seq // CHUNK
    spec = pl.BlockSpec((CHUNK, w), lambda bi, ci: (bi * nc + ci, 0))
    return pl.pallas_call(
        _rwkv_scan_kernel,
        grid=(batch, nc),
        in_specs=[spec] * 6,
        out_specs=spec,
        out_shape=jax.ShapeDtypeStruct((batch * seq, w), F32),
        scratch_shapes=[pltpu.VMEM((w // LANES, LANES, LANES), F32)],
        compiler_params=_cparams(("parallel", "arbitrary")),
        name="rwkv_scan",
    )(r, lw, k, v, a, b)


def _mix_out_kernel(o_ref, gate_ref, y_ref, bonus_ref, g_ref, x_ref, g1_ref, lnw_ref, lnb_ref, w_ref, out_ref):
    fox = o_ref[...] * jax.nn.sigmoid(gate_ref[...])
    bd = _head_block_diag()
    y = y_ref[...]
    inv_n = 1.0 / RWKV_HEAD_DIM
    mean = _head_sum(y, bd) * inv_n
    d = y - mean
    var = _head_sum(d * d, bd) * inv_n
    yn = d * lax.rsqrt(var + GN_EPS) * lnw_ref[...] + lnb_ref[...]
    rw = (yn + bonus_ref[...].astype(F32)) * g_ref[...].astype(F32)
    wf = fox.shape[1]
    mix = _dot(fox.astype(BF16), w_ref[:wf, :]) + _dot(rw.astype(BF16), w_ref[wf:, :])
    out_ref[...] = x_ref[...] + g1_ref[0] * mix


def _mix_out(o_fox, proj, gate_col, y, bonus, g, x2, g1, ln_w, ln_b, w_out_bf, tokens_per_batch):
    t, d = x2.shape
    wf = o_fox.shape[1]
    wr = y.shape[1]
    tm = min(512, tokens_per_batch)
    tpb = tokens_per_batch // tm
    return pl.pallas_call(
        _mix_out_kernel,
        grid=(t // tm,),
        in_specs=[pl.BlockSpec((tm, wf), lambda i: (i, 0)),
                  pl.BlockSpec((tm, wf), lambda i: (i, gate_col // wf)),
                  pl.BlockSpec((tm, wr), lambda i: (i, 0)),
                  pl.BlockSpec((tm, wr), lambda i: (i, 0)),
                  pl.BlockSpec((tm, wr), lambda i: (i, 0)),
                  pl.BlockSpec((tm, d), lambda i: (i, 0)),
                  pl.BlockSpec((1, 1, d), lambda i: (i // tpb, 0, 0)),
                  pl.BlockSpec((1, wr), lambda i: (0, 0)),
                  pl.BlockSpec((1, wr), lambda i: (0, 0)),
                  pl.BlockSpec(w_out_bf.shape, lambda i: (0, 0))],
        out_specs=pl.BlockSpec((tm, d), lambda i: (i, 0)),
        out_shape=jax.ShapeDtypeStruct((t, d), F32),
        compiler_params=_cparams(("parallel",)),
        name="mix_out",
    )(o_fox, proj, y, bonus, g, x2, g1, ln_w, ln_b, w_out_bf)


def _top_k_mask_rows(s, k, tie_safe, want_rank=True):
    n = s.shape[0]
    s0 = s
    row = lax.broadcasted_iota(jnp.int32, s.shape, 0).astype(F32)
    rank = jnp.full(s.shape, float(k), F32) if want_rank else None
    vals = []
    for r in range(k):
        m = jnp.max(s, axis=0, keepdims=True)
        hit = s == m
        if tie_safe:
            pos = jnp.min(jnp.where(hit, row, float(n)), axis=0, keepdims=True)
            hit = row == pos
        vals.append(m)
        if want_rank:
            rank = jnp.where(hit, float(r), rank)
        s = jnp.where(hit, -jnp.inf, s)
    picked = (s == -jnp.inf) & (s0 != -jnp.inf)
    n_picked = jnp.sum(jnp.where(picked, 1.0, 0.0), axis=0, keepdims=True)
    return jnp.concatenate(vals, axis=0), rank, picked, jnp.where(n_picked == float(k), 1.0, 0.0)


def _candidate_rows():
    k = PEER_TOPK
    groups, valid = [], []
    for r0 in range(k // 2):
        n1 = k // (r0 + 1)
        for g in range(-(-n1 // 8)):
            groups.append((r0, g * 8))
            valid.append([g * 8 + i < n1 for i in range(8)])
    groups.append((None, k // 2))
    valid.append([True] * 8)
    return groups, valid


def _route_head(sa, sb, tie_safe):
    k = PEER_TOPK
    tm = sa.shape[1]
    groups, valid = _candidate_rows()
    sub8 = lax.broadcasted_iota(jnp.int32, (8, tm), 0)
    top_a, rank_a, _, clean_a = _top_k_mask_rows(sa, k, tie_safe)
    top_b, rank_b, _, clean_b = _top_k_mask_rows(sb, k, tie_safe)
    parts = []
    for (r0, off), ok in zip(groups, valid):
        if r0 is None:
            part = top_a[off:off + 8, :] + top_b[0:1, :]
        else:
            part = top_a[r0:r0 + 1, :] + top_b[off:off + 8, :]
        if not all(ok):
            part = jnp.where(sub8 < sum(ok), part, -jnp.inf)
        parts.append(part)
    cand = jnp.concatenate(parts, axis=0)
    _, _, sel, clean_c = _top_k_mask_rows(cand, k, tie_safe, want_rank=False)
    z = jnp.sum(jnp.where(sel, jnp.exp(cand - cand[0:1, :]), 0.0), axis=0, keepdims=True)
    self32 = jnp.where(sel, 1.0, 0.0)
    n_keys = sa.shape[0]
    rank3 = rank_a.astype(BF16).reshape(n_keys // BF16_ROWS, BF16_ROWS, tm)
    length3 = jnp.zeros(rank3.shape, BF16)

    def add_count(r0, count_row):
        tile = jnp.broadcast_to(count_row, (BF16_ROWS, tm)).astype(BF16)
        return length3 + jnp.where(rank3 == float(r0), tile[None], 0.0).astype(BF16)

    tail = None
    per_rank = {}
    for gi, (r0, off) in enumerate(groups):
        cnt = self32[gi * 8:(gi + 1) * 8, :]
        if r0 is None:
            tail = cnt
        else:
            c = jnp.sum(cnt, axis=0, keepdims=True)
            per_rank[r0] = c if r0 not in per_rank else per_rank[r0] + c
    for r in range(8):
        per_rank[k // 2 + r] = tail[r:r + 1, :]
    for r0, c in per_rank.items():
        length3 = add_count(r0, c)
    length = length3.reshape(n_keys, tm).astype(F32)
    pa = jnp.exp(sa - top_a[0:1, :])
    qb = jnp.exp(sb - top_b[0:1, :]) / z
    return pa, length, qb, rank_b, clean_a * clean_b * clean_c


def _peer_query_kernel(x_ref, nw_ref, sc_ref, sh_ref, wq_ref, ht_ref, q_ref):
    h = _modulated_norm(x_ref[...], nw_ref[...], sc_ref[0], sh_ref[0])
    ht_ref[...] = h.T.astype(BF16)
    q = _dot(h.astype(BF16), wq_ref[...])
    dk = q.shape[1] // PEER_HEADS
    for hd in range(PEER_HEADS):
        q_ref[hd] = q[:, hd * dk:(hd + 1) * dk]


def _peer_query(x1, norm_w, sc, sh, wq_bf, tokens_per_batch):
    t, d = x1.shape
    dk = wq_bf.shape[1] // PEER_HEADS
    tm = min(512, tokens_per_batch)
    tpb = tokens_per_batch // tm
    return pl.pallas_call(
        _peer_query_kernel,
        grid=(t // tm,),
        in_specs=[pl.BlockSpec((tm, d), lambda i: (i, 0)),
                  pl.BlockSpec((1, d), lambda i: (0, 0)),
                  pl.BlockSpec((1, 1, d), lambda i: (i // tpb, 0, 0)),
                  pl.BlockSpec((1, 1, d), lambda i: (i // tpb, 0, 0)),
                  pl.BlockSpec(wq_bf.shape, lambda i: (0, 0))],
        out_specs=[pl.BlockSpec((d, tm), lambda i: (0, i)),
                   pl.BlockSpec((PEER_HEADS, tm, dk), lambda i: (0, i, 0))],
        out_shape=[jax.ShapeDtypeStruct((d, t), BF16),
                   jax.ShapeDtypeStruct((PEER_HEADS, t, dk), F32)],
        compiler_params=_cparams(("parallel",)),
        name="peer_query",
    )(x1, norm_w, sc, sh, wq_bf)


def _gelu_exact(x):
    return 0.5 * x * (1.0 + lax.erf(x * 0.7071067811865476))


def _peer_route_act_kernel(ht_ref, q_ref, keys_ref, u_ref,
                           act_ref, pa_ref, len_ref, qb_ref, rnk_ref, code_scr):
    hd = pl.program_id(1)
    tm = ht_ref.shape[1]
    half = keys_ref.shape[-1]
    sub = min(ROUTE_SUB_TOKENS, tm)
    blocked = (PEER_NKEYS // PEER_SLABS, PEER_SLABS, sub)

    @pl.when(hd == 0)
    def _():
        code_scr[...] = jnp.zeros(code_scr.shape, F32)

    def scores(h, t0):
        q = q_ref[h, t0:t0 + sub, :]
        return (_dot_nt(keys_ref[h, 0], q[:, :half], HIGHEST),
                _dot_nt(keys_ref[h, 1], q[:, half:], HIGHEST))

    def route(h, t0, sa, sb, tie_safe):
        pa, length, qbt, rank_b, clean = _route_head(sa, sb, tie_safe)
        pa_ref[h, :, :, t0:t0 + sub] = pa.reshape(blocked)
        len_ref[h, :, :, t0:t0 + sub] = length.reshape(blocked)
        qb_ref[h, :, t0:t0 + sub] = qbt.astype(BF16)
        rnk_ref[h, :, t0:t0 + sub] = rank_b.astype(BF16)
        return clean

    starts = range(0, tm, sub)
    score_tiles = [scores(hd, t0) for t0 in starts]
    rows_per_dot = 256
    for r0 in range(0, u_ref.shape[0], rows_per_dot):
        act_ref[r0:r0 + rows_per_dot, :] = _dot(u_ref[r0:r0 + rows_per_dot, :], ht_ref[...]).astype(BF16)

    dirty = jnp.zeros((1, 1), F32)
    for t0, (sa, sb) in zip(starts, score_tiles):
        dirty = jnp.maximum(dirty, 1.0 - jnp.min(route(hd, t0, sa, sb, False), axis=1, keepdims=True))
    code_scr[0:1, 0:1] += dirty * lax.shift_left(1, hd).astype(F32)

    @pl.when(hd == pl.num_programs(1) - 1)
    def _():
        tie_bits = code_scr[0, 0].astype(jnp.int32)

        def redo(h, carry):
            @pl.when(((tie_bits >> h) & 1) == 1)
            def _():
                for t0 in starts:
                    route(h, t0, *scores(h, t0), True)
            return carry

        lax.fori_loop(0, PEER_HEADS, redo, 0)


def _peer_route_act(ht_bf, q_heads, sub_keys, u_bf, tokens_per_batch):
    d, t = ht_bf.shape
    ne = u_bf.shape[0]
    bu = ne // PEER_HEADS
    dk = q_heads.shape[-1]
    tm = min(512, tokens_per_batch)
    n_blocks = PEER_NKEYS // PEER_SLABS
    dense = jax.ShapeDtypeStruct((PEER_HEADS, PEER_NKEYS, t), BF16)
    dense_spec = pl.BlockSpec((PEER_HEADS, PEER_NKEYS, tm), lambda i, j: (0, 0, i))
    blocked = jax.ShapeDtypeStruct((PEER_HEADS, n_blocks, PEER_SLABS, t), F32)
    blocked_spec = pl.BlockSpec((PEER_HEADS, n_blocks, PEER_SLABS, tm), lambda i, j: (0, 0, 0, i))
    return pl.pallas_call(
        _peer_route_act_kernel,
        grid=(t // tm, PEER_HEADS),
        in_specs=[pl.BlockSpec((d, tm), lambda i, j: (0, i)),
                  pl.BlockSpec((PEER_HEADS, tm, dk), lambda i, j: (0, i, 0)),
                  pl.BlockSpec(sub_keys.shape, lambda i, j: (0, 0, 0, 0)),
                  pl.BlockSpec((bu, d), lambda i, j: (j, 0))],
        out_specs=[pl.BlockSpec((bu, tm), lambda i, j: (j, i)),
                   blocked_spec, blocked_spec, dense_spec, dense_spec],
        out_shape=[jax.ShapeDtypeStruct((ne, t), BF16), blocked, blocked, dense, dense],
        scratch_shapes=[pltpu.VMEM((8, LANES), F32)],
        compiler_params=_cparams(("parallel", "arbitrary")),
        name="peer_route_act",
    )(ht_bf, q_heads, sub_keys, u_bf)


def _transpose_blocks_kernel(x_ref, o_ref):
    o_ref[0] = x_ref[...].T.astype(BF16)


def _transpose_blocks(w, block):
    n, d = w.shape
    return pl.pallas_call(
        _transpose_blocks_kernel,
        grid=(n // block,),
        in_specs=[pl.BlockSpec((block, d), lambda j: (j, 0))],
        out_specs=pl.BlockSpec((1, d, block), lambda j: (j, 0, 0)),
        out_shape=jax.ShapeDtypeStruct((n // block, d, block), BF16),
        compiler_params=_cparams(("parallel",)),
        name="transpose_blocks",
    )(w)


def _peer_expert_kernel(act_ref, vt_ref, pa_ref, len_ref, qb_ref, rnk_ref, x_ref, g2_ref, o_ref, acc_scr):
    j = pl.program_id(1)
    be, tm = act_ref.shape

    @pl.when(j == 0)
    def _():
        acc_scr[...] = jnp.zeros(acc_scr.shape, F32)

    n_slabs = be // PEER_NKEYS
    rows = BF16_ROWS
    n_groups = PEER_NKEYS // rows

    def row_tile(ref, hd, s):
        return jnp.broadcast_to(ref[hd, 0, s:s + 1, :], (rows, tm)).astype(BF16)

    zero = jnp.zeros((rows, tm), BF16)
    parts = [[None] * n_groups for _ in range(n_slabs)]
    slab_group = 8
    for s0 in range(0, n_slabs, slab_group):
        ss = range(s0, min(s0 + slab_group, n_slabs))
        ln = {(hd, s): row_tile(len_ref, hd, s) for hd in range(PEER_HEADS) for s in ss}
        pa = {(hd, s): row_tile(pa_ref, hd, s) for hd in range(PEER_HEADS) for s in ss}
        for g in range(n_groups):
            sl = slice(g * rows, (g + 1) * rows)
            acc = {}
            for hd in range(PEER_HEADS):
                rk = rnk_ref[hd, sl, :]
                qv = qb_ref[hd, sl, :]
                for s in ss:
                    term = jnp.where(rk < ln[hd, s], qv, zero) * pa[hd, s]
                    acc[s] = term if s not in acc else acc[s] + term
            for s in ss:
                lo = s * PEER_NKEYS + g * rows
                parts[s][g] = acc[s] * _gelu_exact(act_ref[lo:lo + rows, :].astype(F32)).astype(BF16)
    p = jnp.concatenate([parts[s][g] for s in range(n_slabs) for g in range(n_groups)], axis=0)
    acc_scr[...] += _dot(vt_ref[0], p)

    @pl.when(j == pl.num_programs(1) - 1)
    def _():
        o_ref[...] = x_ref[...] + g2_ref[0] * acc_scr[...].T


def _peer_expert(act, vt_bf, pa, ln, qb, rnk, x1, g2, tokens_per_batch):
    t, d = x1.shape
    ne = act.shape[0]
    tm = min(512, tokens_per_batch)
    tpb = tokens_per_batch // tm
    be = PEER_EXPERT_BLOCK
    dense_spec = pl.BlockSpec((PEER_HEADS, PEER_NKEYS, tm), lambda i, j: (0, 0, i))
    blocked_spec = pl.BlockSpec((PEER_HEADS, 1, PEER_SLABS, tm), lambda i, j: (0, j, 0, i))
    return pl.pallas_call(
        _peer_expert_kernel,
        grid=(t // tm, ne // be),
        in_specs=[pl.BlockSpec((be, tm), lambda i, j: (j, i)),
                  pl.BlockSpec((1, d, be), lambda i, j: (j, 0, 0)),
                  blocked_spec, blocked_spec, dense_spec, dense_spec,
                  pl.BlockSpec((tm, d), lambda i, j: (i, 0)),
                  pl.BlockSpec((1, 1, d), lambda i, j: (i // tpb, 0, 0))],
        out_specs=pl.BlockSpec((tm, d), lambda i, j: (i, 0)),
        out_shape=jax.ShapeDtypeStruct((t, d), F32),
        scratch_shapes=[pltpu.VMEM((d, tm), F32)],
        compiler_params=_cparams(("parallel", "arbitrary")),
        name="peer_expert",
    )(act, vt_bf, pa, ln, qb, rnk, x1, g2)


def _pad_cols(a, n):
    return jnp.pad(a, ((0, 0), (0, n - a.shape[1])))


def _pad_rows(a, n):
    return jnp.pad(a, ((0, n - a.shape[0]), (0, 0)))


def _layer(x, c, w_ada, b_ada, norm_mix_w, w_in, fox_q_norm_w, fox_k_norm_w, fox_f_bias,
           rwkv_mu, rwkv_w0, rwkv_w_up, rwkv_a0, rwkv_a_up, rwkv_g_up, rwkv_k_k, rwkv_k_a,
           rwkv_r_k, rwkv_ln_w, rwkv_ln_b, w_out, norm_ffn_w, peer_w_query, peer_sub_keys,
           peer_u, peer_v):
    B, S, D = x.shape
    T = B * S
    fw = FOX_HEADS * FOX_HEAD_DIM
    rw = RWKV_HEADS * RWKV_HEAD_DIM
    w_lora = rwkv_w_up.shape[0]
    a_lora = rwkv_a_up.shape[0]
    g_lora = rwkv_g_up.shape[0]
    assert w_lora <= LANES and a_lora + FOX_HEADS <= LANES and g_lora == 256

    c_pad = _pad_rows(c, 8)
    mod = _ada_mod(c_pad, w_ada, b_ada)[:B]
    sh1, sc1, g1, sh2, sc2, g2 = [m.reshape(B, 1, D) for m in jnp.split(mod, 6, axis=-1)]

    fox_cols = 4 * fw + FOX_HEADS
    wi_fox, wi_rw = w_in[:, :fox_cols], w_in[:, fox_cols:]
    mu = rwkv_mu.reshape(1, -1)
    seg = lambda a, lo, n: a[:, lo:lo + n]
    w_perm = jnp.concatenate([
        seg(wi_fox, 0, 4 * fw),
        seg(wi_rw, 0, 3 * rw),
        seg(wi_rw, 3 * rw + w_lora + a_lora, g_lora),
        _pad_cols(seg(wi_rw, 3 * rw, w_lora), LANES),
        _pad_cols(jnp.concatenate([seg(wi_rw, 3 * rw + w_lora, a_lora), seg(wi_fox, 4 * fw, FOX_HEADS)], 1), LANES),
    ], axis=1).astype(BF16)
    cols = {"gate": 3 * fw, "rr": 4 * fw, "rk": 4 * fw + rw, "rv": 4 * fw + 2 * rw,
            "gd": 4 * fw + 3 * rw, "lo": 4 * fw + 3 * rw + g_lora}
    f_lane = a_lora
    f_block = (cols["lo"] + LANES) // LANES
    mus = [seg(mu, 0, rw), seg(mu, rw, rw), seg(mu, 2 * rw, rw),
           seg(mu, 3 * rw + w_lora + a_lora, g_lora),
           jnp.concatenate([_pad_cols(seg(mu, 3 * rw, w_lora), LANES),
                            _pad_cols(seg(mu, 3 * rw + w_lora, a_lora), LANES)], 1)]
    scale = FOX_HEAD_DIM ** -0.5 * LOG2_E
    head_w = _pad_cols(jnp.concatenate([jnp.tile(fox_q_norm_w * scale, FOX_HEADS),
                                        jnp.tile(fox_k_norm_w, FOX_HEADS)]).reshape(1, -1), w_perm.shape[1])

    x2 = x.reshape(T, D)
    proj = _in_proj(_norm_mod(x2, norm_mix_w.reshape(1, D), sc1, sh1, S), w_perm, head_w, 2 * fw)

    f_bias_row = jnp.zeros((1, LANES), F32).at[0, f_lane:f_lane + FOX_HEADS].set(fox_f_bias)
    cum = _fox_cum(proj, f_bias_row, B, S, f_block)
    cum_rows = cum.reshape(B, S, LANES)[:, :, f_lane:f_lane + FOX_HEADS].transpose(0, 2, 1).reshape(B, FOX_HEADS, 1, S)
    o_fox = _fox_attn(proj, cum_rows, B, S)

    row = lambda a: a.reshape(1, -1)
    r, lw, k, v, a_vec, b_vec, g, bonus = _rwkv_prep(
        proj, cols, mus, row(rwkv_w0), _pad_rows(rwkv_w_up, LANES), row(rwkv_a0), _pad_rows(rwkv_a_up, LANES),
        rwkv_g_up, row(rwkv_k_k), row(rwkv_k_a), row(rwkv_r_k), S)
    y = _rwkv_scan(r, lw, k, v, a_vec, b_vec, B, S)

    x1 = _mix_out(o_fox, proj, cols["gate"], y, bonus, g, x2, g1, row(rwkv_ln_w), row(rwkv_ln_b),
                  w_out.astype(BF16), S)

    ht, q_heads = _peer_query(x1, norm_ffn_w.reshape(1, D), sc2, sh2, peer_w_query.astype(BF16), S)
    act, pa, ln, qb, rnk = _peer_route_act(ht, q_heads, peer_sub_keys, peer_u.astype(BF16), S)
    vt = _transpose_blocks(peer_v, PEER_EXPERT_BLOCK)
    out = _peer_expert(act, vt, pa, ln, qb, rnk, x1, g2, S)
    return out.reshape(B, S, D)


def kernel(x, c, w_ada, b_ada, norm_mix_w, w_in, fox_q_norm_w, fox_k_norm_w, fox_f_bias, rwkv_mu, rwkv_w0,
           rwkv_w_up, rwkv_a0, rwkv_a_up, rwkv_g_up, rwkv_k_k, rwkv_k_a, rwkv_r_k, rwkv_ln_w, rwkv_ln_b,
           w_out, norm_ffn_w, peer_w_query, peer_sub_keys, peer_u, peer_v):
    params = (w_ada, b_ada, norm_mix_w, w_in, fox_q_norm_w, fox_k_norm_w, fox_f_bias, rwkv_mu, rwkv_w0,
              rwkv_w_up, rwkv_a0, rwkv_a_up, rwkv_g_up, rwkv_k_k, rwkv_k_a, rwkv_r_k, rwkv_ln_w, rwkv_ln_b,
              w_out, norm_ffn_w, peer_w_query, peer_sub_keys, peer_u, peer_v)
    for l in range(w_ada.shape[0]):
        x = _layer(x, c, *[p[l] for p in params])
    return x
```

```python
import functools

import jax
import jax.numpy as jnp
from jax import lax
from jax.experimental import pallas as pl
from jax.experimental.pallas import tpu as pltpu

F32 = jnp.float32
BF16 = jnp.bfloat16
HIGHEST = lax.Precision.HIGHEST

LANES = 128
BF16_ROWS = 16
NORM_EPS = 1e-6
GN_EPS = 64e-5
CHUNK = 64
FOX_HEADS = 8
FOX_HEAD_DIM = 128
RWKV_HEADS = 16
RWKV_HEAD_DIM = 64
PEER_HEADS = 8
PEER_NKEYS = 128
PEER_TOPK = 16
PEER_EXPERT_BLOCK = 2048
PEER_SLABS = PEER_EXPERT_BLOCK // PEER_NKEYS
ROUTE_SUB_TOKENS = 256
NEG_BIG = -1e30
LOG2_E = 1.4426950408889634
VMEM_LIMIT = 56 * 1024 * 1024


def _cparams(sem):
    return pltpu.CompilerParams(dimension_semantics=sem, vmem_limit_bytes=VMEM_LIMIT)


def _dot(a, b, precision=None):
    return jnp.dot(a, b, preferred_element_type=F32, precision=precision)


def _dot_nt(a, b, precision=None):
    return lax.dot_general(a, b, (((1,), (1,)), ((), ())), preferred_element_type=F32,
                           precision=precision)


def _split_bf16(x):
    hi = x.astype(BF16)
    return hi, (x - hi.astype(F32)).astype(BF16)


def _dot_bf16x3(a, b):
    ah, al = _split_bf16(a)
    bh, bl = _split_bf16(b)
    return _dot(ah, bh) + (_dot(ah, bl) + _dot(al, bh))


def _dot_tn(a, b, precision=None):
    return lax.dot_general(a, b, (((0,), (0,)), ((), ())), preferred_element_type=F32,
                           precision=precision)


def _ada_kernel(c_ref, w_ref, b_ref, o_ref):
    c = c_ref[...]
    s = c * jax.nn.sigmoid(c)
    o_ref[...] = _dot(s, w_ref[...], HIGHEST) + b_ref[...]


def _ada_mod(c_pad, w_ada, b_ada):
    rows, d = c_pad.shape
    n = w_ada.shape[1]
    bn = 1024
    return pl.pallas_call(
        _ada_kernel,
        grid=(n // bn,),
        in_specs=[pl.BlockSpec((rows, d), lambda j: (0, 0)),
                  pl.BlockSpec((d, bn), lambda j: (0, j)),
                  pl.BlockSpec((1, bn), lambda j: (0, j))],
        out_specs=pl.BlockSpec((rows, bn), lambda j: (0, j)),
        out_shape=jax.ShapeDtypeStruct((rows, n), F32),
        compiler_params=_cparams(("arbitrary",)),
        name="ada_mod",
    )(c_pad, w_ada, b_ada.reshape(1, n))


def _modulated_norm(x, nw, sc, sh):
    y = x * lax.rsqrt(jnp.mean(x * x, axis=-1, keepdims=True) + NORM_EPS)
    return y * nw * (1.0 + sc) + sh


def _norm_mod_kernel(x_ref, nw_ref, sc_ref, sh_ref, o_ref):
    o_ref[...] = _modulated_norm(x_ref[...], nw_ref[...], sc_ref[0], sh_ref[0]).astype(BF16)


def _norm_mod(x2, norm_w, sc, sh, tokens_per_batch):
    t, d = x2.shape
    tm = min(512, tokens_per_batch)
    tpb = tokens_per_batch // tm
    return pl.pallas_call(
        _norm_mod_kernel,
        grid=(t // tm,),
        in_specs=[pl.BlockSpec((tm, d), lambda i: (i, 0)),
                  pl.BlockSpec((1, d), lambda i: (0, 0)),
                  pl.BlockSpec((1, 1, d), lambda i: (i // tpb, 0, 0)),
                  pl.BlockSpec((1, 1, d), lambda i: (i // tpb, 0, 0))],
        out_specs=pl.BlockSpec((tm, d), lambda i: (i, 0)),
        out_shape=jax.ShapeDtypeStruct((t, d), BF16),
        compiler_params=_cparams(("parallel",)),
        name="norm_mod",
    )(x2, norm_w, sc, sh)


def _in_proj_kernel(h_ref, w_ref, hw_ref, o_ref, *, n_qk_blocks):
    j = pl.program_id(1)
    acc = _dot(h_ref[...], w_ref[...])

    @pl.when(j < n_qk_blocks)
    def _():
        for hh in range(acc.shape[1] // FOX_HEAD_DIM):
            sl = slice(hh * FOX_HEAD_DIM, (hh + 1) * FOX_HEAD_DIM)
            a = acc[:, sl]
            rs = lax.rsqrt(jnp.mean(a * a, axis=-1, keepdims=True) + NORM_EPS)
            o_ref[:, sl] = a * rs * hw_ref[:, sl]

    @pl.when(j >= n_qk_blocks)
    def _():
        o_ref[...] = acc


def _in_proj(h_bf, w_bf, head_w, n_qk_cols):
    t, d = h_bf.shape
    n = w_bf.shape[1]
    tm = min(2048, t)
    bn = 512
    kern = functools.partial(_in_proj_kernel, n_qk_blocks=n_qk_cols // bn)
    return pl.pallas_call(
        kern,
        grid=(t // tm, n // bn),
        in_specs=[pl.BlockSpec((tm, d), lambda i, j: (i, 0)),
                  pl.BlockSpec((d, bn), lambda i, j: (0, j)),
                  pl.BlockSpec((1, bn), lambda i, j: (0, j))],
        out_specs=pl.BlockSpec((tm, bn), lambda i, j: (i, j)),
        out_shape=jax.ShapeDtypeStruct((t, n), F32),
        compiler_params=_cparams(("parallel", "arbitrary")),
        name="in_proj",
    )(h_bf, w_bf, head_w)


def _log_sigmoid(x):
    return jnp.minimum(x, 0.0) - jnp.log(1.0 + jnp.exp(-jnp.abs(x)))


def _fox_cum_kernel(f_ref, b_ref, o_ref, *, blk):
    s = f_ref.shape[0]
    row = lax.broadcasted_iota(jnp.int32, (blk, blk), 0)
    col = lax.broadcasted_iota(jnp.int32, (blk, blk), 1)
    tri = (row >= col).astype(F32)
    carry = jnp.zeros((1, f_ref.shape[1]), F32)
    for i in range(s // blk):
        lf = _log_sigmoid(f_ref[i * blk:(i + 1) * blk, :] + b_ref[...])
        cs = _dot(tri, lf, HIGHEST) + carry
        o_ref[i * blk:(i + 1) * blk, :] = cs * LOG2_E
        carry = cs[blk - 1:blk, :]


def _fox_cum(proj, f_bias_row, batch, seq, col_block):
    blk = min(256, seq)
    return pl.pallas_call(
        functools.partial(_fox_cum_kernel, blk=blk),
        grid=(batch,),
        in_specs=[pl.BlockSpec((seq, LANES), lambda b: (b, col_block)),
                  pl.BlockSpec((1, LANES), lambda b: (0, 0))],
        out_specs=pl.BlockSpec((seq, LANES), lambda b: (b, 0)),
        out_shape=jax.ShapeDtypeStruct((batch * seq, LANES), F32),
        compiler_params=_cparams(("parallel",)),
        name="fox_cum",
    )(proj, f_bias_row)


def _fox_attn_kernel(qt_ref, kt_ref, q_ref, k_ref, v_ref, c_ref, o_ref, m_scr, l_scr, acc_scr):
    t = pl.program_id(2)
    qi = qt_ref[t]
    ki = kt_ref[t]
    hd = FOX_HEAD_DIM
    n_heads = q_ref.shape[1] // hd

    @pl.when(ki == 0)
    def _():
        m_scr[...] = jnp.full(m_scr.shape, NEG_BIG, F32)
        l_scr[...] = jnp.zeros(l_scr.shape, F32)
        acc_scr[...] = jnp.zeros(acc_scr.shape, F32)

    def step(masked):
        hs = range(n_heads)
        sl = [slice(h * hd, (h + 1) * hd) for h in hs]
        s = [_dot_nt(q_ref[:, sl[h]].astype(BF16), k_ref[:, sl[h]].astype(BF16)) - c_ref[0, h] for h in hs]
        if masked:
            row = lax.broadcasted_iota(jnp.int32, s[0].shape, 0)
            col = lax.broadcasted_iota(jnp.int32, s[0].shape, 1)
            s = [jnp.where(row >= col, x, NEG_BIG) for x in s]
        m_prev = [m_scr[h] for h in hs]
        m_new = [jnp.maximum(m_prev[h], jnp.max(s[h], axis=-1, keepdims=True)) for h in hs]
        n_rep = s[0].shape[1] // hd
        p = [jnp.exp2(s[h] - jnp.concatenate([m_new[h]] * n_rep, axis=1)) for h in hs]
        alpha = [jnp.exp2(m_prev[h] - m_new[h]) for h in hs]
        ones = jnp.ones((k_ref.shape[0], hd), BF16)
        pv = [_dot(p[h].astype(BF16), jnp.concatenate([v_ref[:, sl[h]].astype(BF16), ones], axis=1))
              for h in hs]
        for h in hs:
            l_scr[h] = alpha[h] * l_scr[h] + pv[h][:, hd:]
            acc_scr[:, sl[h]] = alpha[h] * acc_scr[:, sl[h]] + pv[h][:, :hd]
            m_scr[h] = m_new[h]

    @pl.when(ki < qi)
    def _():
        step(False)

    @pl.when(ki == qi)
    def _():
        step(True)
        for h in range(n_heads):
            sl = slice(h * hd, (h + 1) * hd)
            o_ref[:, sl] = acc_scr[:, sl] / l_scr[h]


def _fox_attn(proj, cum_rows, batch, seq):
    tq = min(512, seq)
    nq = seq // tq
    hps = 8
    w = hps * FOX_HEAD_DIM
    kcol = FOX_HEADS // hps
    tri = [(q, k) for q in range(nq) for k in range(q + 1)]
    qt = jnp.asarray([q for q, _ in tri], jnp.int32)
    kt = jnp.asarray([k for _, k in tri], jnp.int32)
    grid_spec = pltpu.PrefetchScalarGridSpec(
        num_scalar_prefetch=2,
        grid=(batch, kcol, len(tri)),
        in_specs=[
            pl.BlockSpec((tq, w), lambda b, h, t, qt, kt: (b * nq + qt[t], h)),
            pl.BlockSpec((tq, w), lambda b, h, t, qt, kt: (b * nq + kt[t], kcol + h)),
            pl.BlockSpec((tq, w), lambda b, h, t, qt, kt: (b * nq + kt[t], 2 * kcol + h)),
            pl.BlockSpec((1, hps, 1, tq), lambda b, h, t, qt, kt: (b, h, 0, kt[t])),
        ],
        out_specs=pl.BlockSpec((tq, w), lambda b, h, t, qt, kt: (b * nq + qt[t], h)),
        scratch_shapes=[pltpu.VMEM((hps, tq, FOX_HEAD_DIM), F32), pltpu.VMEM((hps, tq, FOX_HEAD_DIM), F32),
                        pltpu.VMEM((tq, w), F32)],
    )
    return pl.pallas_call(
        _fox_attn_kernel,
        grid_spec=grid_spec,
        out_shape=jax.ShapeDtypeStruct((batch * seq, FOX_HEADS * FOX_HEAD_DIM), F32),
        compiler_params=_cparams(("parallel", "parallel", "arbitrary")),
        name="fox_attn",
    )(qt, kt, proj, proj, proj, cum_rows)


def _head_sum(x, bd):
    hi, lo = _split_bf16(x)
    parts = [_dot(hi[:, j * LANES:(j + 1) * LANES], bd) + _dot(lo[:, j * LANES:(j + 1) * LANES], bd)
             for j in range(x.shape[1] // LANES)]
    return jnp.concatenate(parts, axis=1)


def _head_block_diag():
    r = lax.broadcasted_iota(jnp.int32, (LANES, LANES), 0) // RWKV_HEAD_DIM
    c = lax.broadcasted_iota(jnp.int32, (LANES, LANES), 1) // RWKV_HEAD_DIM
    return (r == c).astype(BF16)


def _rwkv_prep_kernel(r_ref, k_ref, v_ref, gd_ref, lo_ref, pr_ref, pk_ref, pv_ref, pgd_ref, plo_ref,
                      mu_r, mu_k, mu_v, mu_gd, mu_lo, w0_ref, wup_ref, a0_ref, aup_ref, gup_ref,
                      kk_ref, ka_ref, rk_ref,
                      or_ref, olw_ref, ok_ref, ov_ref, oa_ref, ob_ref, og_ref, obonus_ref, *, tiles_per_batch):
    i = pl.program_id(0)
    first = (i % tiles_per_batch) == 0

    def shifted(cur_ref, prv_ref, mu_ref):
        cur = cur_ref[...]
        last = jnp.where(first, 0.0, prv_ref[7:8, :])
        row = lax.broadcasted_iota(jnp.int32, cur.shape, 0)
        prev = jnp.where(row == 0, last, pltpu.roll(cur, 1, 0))
        return cur + (prev - cur) * mu_ref[...]

    r = shifted(r_ref, pr_ref, mu_r)
    k = shifted(k_ref, pk_ref, mu_k)
    v = shifted(v_ref, pv_ref, mu_v)
    gd = shifted(gd_ref, pgd_ref, mu_gd)
    lo = shifted(lo_ref, plo_ref, mu_lo)
    wd = lo[:, :LANES]
    ad = lo[:, LANES:]

    w_pre = w0_ref[...] + _dot_bf16x3(jnp.tanh(wd), wup_ref[...])
    w_raw = _log_sigmoid(w_pre) - 0.5
    log_decay = -jnp.exp(w_raw)
    a = jax.nn.sigmoid(a0_ref[...] + _dot_bf16x3(ad, aup_ref[...]))
    g = _dot_bf16x3(jax.nn.sigmoid(gd), gup_ref[...])

    bd = _head_block_diag()
    kk = k * kk_ref[...]
    nrm = jnp.maximum(jnp.sqrt(_head_sum(kk * kk, bd)), 1e-12)
    kk = kk / nrm
    k_mod = k * (1.0 + (a - 1.0) * ka_ref[...])
    bonus = _head_sum(r * k_mod * rk_ref[...], bd) * v

    or_ref[...] = r.astype(BF16)
    olw_ref[...] = log_decay
    ok_ref[...] = k_mod.astype(BF16)
    ov_ref[...] = v.astype(BF16)
    oa_ref[...] = (-kk).astype(BF16)
    ob_ref[...] = (kk * a).astype(BF16)
    og_ref[...] = g.astype(BF16)
    obonus_ref[...] = bonus.astype(BF16)


def _rwkv_prep(proj, cols, mus, w0, w_up, a0, a_up, g_up, k_k, k_a, r_k, tokens_per_batch):
    t = proj.shape[0]
    w = RWKV_HEADS * RWKV_HEAD_DIM
    tm = min(512, tokens_per_batch)
    tpb = tokens_per_batch // tm
    widths = [w, w, w, 256, 256]
    offs = [cols["rr"], cols["rk"], cols["rv"], cols["gd"], cols["lo"]]
    cur_specs = [pl.BlockSpec((tm, wd), functools.partial(lambda i, cb: (i, cb), cb=o // wd))
                 for wd, o in zip(widths, offs)]
    prv_specs = [pl.BlockSpec((8, wd), functools.partial(
        lambda i, cb: (jnp.maximum(i * (tm // 8) - 1, 0), cb), cb=o // wd))
        for wd, o in zip(widths, offs)]
    full = lambda a: pl.BlockSpec(a.shape, lambda i: (0,) * a.ndim)
    params = list(mus) + [w0, w_up, a0, a_up, g_up, k_k, k_a, r_k]
    out_spec = pl.BlockSpec((tm, w), lambda i: (i, 0))
    return pl.pallas_call(
        functools.partial(_rwkv_prep_kernel, tiles_per_batch=tpb),
        grid=(t // tm,),
        in_specs=cur_specs + prv_specs + [full(p) for p in params],
        out_specs=[out_spec] * 8,
        out_shape=[jax.ShapeDtypeStruct((t, w), dt) for dt in (BF16, F32, BF16, BF16, BF16, BF16, BF16, BF16)],
        compiler_params=_cparams(("parallel",)),
        name="rwkv_prep",
    )(*([proj] * 10), *params)


def _rwkv_scan_kernel(r_ref, lw_ref, k_ref, v_ref, a_ref, b_ref, o_ref, h_scr):
    c = pl.program_id(1)

    @pl.when(c == 0)
    def _():
        h_scr[...] = jnp.zeros(h_scr.shape, F32)

    C = r_ref.shape[0]
    n_pairs = r_ref.shape[1] // LANES
    P = HIGHEST

    row = lax.broadcasted_iota(jnp.int32, (C, C), 0)
    col = lax.broadcasted_iota(jnp.int32, (C, C), 1)
    tri = (row >= col).astype(F32)
    lw = lw_ref[...]
    cw = _dot(tri, lw, P)
    cw_end = cw[C - 1:C, :]
    e_pos = jnp.exp(cw)
    e_prev = jnp.exp(cw - lw)
    e_neg = jnp.exp(-cw)
    e_end = jnp.exp(cw_end - cw)
    w_end = jnp.exp(cw_end)

    a = a_ref[...].astype(F32)
    b = b_ref[...].astype(F32)
    k = k_ref[...].astype(F32)
    r = r_ref[...].astype(F32)
    at = a * e_prev
    bt = b * e_neg
    kt = k * e_neg
    rt = r * e_pos
    bh = b * e_end
    kh = k * e_end
    v = v_ref[...].astype(F32)

    lane = lax.broadcasted_iota(jnp.int32, (C, LANES), 1)
    head0 = lane < RWKV_HEAD_DIM
    r2 = lax.broadcasted_iota(jnp.int32, (2 * C, 2 * C), 0)
    c2 = lax.broadcasted_iota(jnp.int32, (2 * C, 2 * C), 1)
    same = (r2 // C) == (c2 // C)
    strict = same & (r2 > c2)
    incl = same & (r2 >= c2)
    eye = (r2 == c2).astype(F32)

    def two(x, p):
        xp = x[:, p * LANES:(p + 1) * LANES]
        return jnp.concatenate([jnp.where(head0, xp, 0.0), jnp.where(head0, 0.0, xp)], axis=0)

    def mm(x, y):
        return _dot(x.astype(BF16), y.astype(BF16))

    def mm3(x, y):
        xh, xl = _split_bf16(x)
        yh, yl = _split_bf16(y)
        n = y.shape[1]
        wide = _dot(xh, jnp.concatenate([yh, yl], axis=1))
        return wide[:, :n] + (wide[:, n:] + _dot(xl, yh))

    G = 2 * C
    pairs = range(n_pairs)
    at2 = [two(at, p) for p in pairs]
    rt2 = [two(rt, p) for p in pairs]
    v2 = [two(v, p).astype(BF16) for p in pairs]
    gram = [_dot_nt(jnp.concatenate([at2[p], rt2[p]], axis=0).astype(BF16),
                    jnp.concatenate([two(bt, p), two(kt, p)], axis=0).astype(BF16)) for p in pairs]
    a_ab = [jnp.where(strict, gram[p][:G, :G], 0.0) for p in pairs]
    nmat = list(a_ab)
    pw = list(a_ab)
    for _ in range(max(1, (C - 1).bit_length() - 1)):
        pw = [mm3(pw[p], pw[p]) for p in pairs]
        nmat = [nmat[p] + pw[p] + mm3(nmat[p], pw[p]) for p in pairs]
    akv = [mm(jnp.where(strict, gram[p][:G, G:], 0.0), v2[p]) for p in pairs]
    rhs = [jnp.concatenate([at2[p], akv[p]], axis=1) for p in pairs]
    pq = [(rhs[p] + mm(nmat[p], rhs[p])).astype(BF16) for p in pairs]
    ry = [mm(jnp.where(incl, gram[p][G:, :G], 0.0), pq[p]) for p in pairs]
    mv = [mm(jnp.where(incl, gram[p][G:, G:], 0.0), v2[p]) for p in pairs]
    ge = [_dot_tn(two(bh, p).astype(BF16), pq[p]) for p in pairs]
    kv = [_dot_tn(two(kh, p).astype(BF16), v2[p]) for p in pairs]
    for p in pairs:
        rr = rt2[p] + ry[p][:, :LANES]
        gm = eye * w_end[:, p * LANES:(p + 1) * LANES] + ge[p][:, :LANES]
        yh = mm(jnp.concatenate([rr, gm], axis=0), h_scr[p])
        y2 = yh[:G] + ry[p][:, LANES:] + mv[p]
        h_scr[p] = yh[G:] + ge[p][:, LANES:] + kv[p]
        o_ref[:, p * LANES:(p + 1) * LANES] = y2[:C, :] + y2[C:, :]


def _rwkv_scan(r, lw, k, v, a, b, batch, seq):
    w = r.shape[1]
    nc = seq // CHUNK
    spec = pl.BlockSpec((CHUNK, w), lambda bi, ci: (bi * nc + ci, 0))
    return pl.pallas_call(
        _rwkv_scan_kernel,
        grid=(batch, nc),
        in_specs=[spec] * 6,
        out_specs=spec,
        out_shape=jax.ShapeDtypeStruct((batch * seq, w), F32),
        scratch_shapes=[pltpu.VMEM((w // LANES, LANES, LANES), F32)],
        compiler_params=_cparams(("parallel", "arbitrary")),
        name="rwkv_scan",
    )(r, lw, k, v, a, b)


def _mix_out_kernel(o_ref, gate_ref, y_ref, bonus_ref, g_ref, x_ref, g1_ref, lnw_ref, lnb_ref, w_ref, out_ref):
    fox = o_ref[...] * jax.nn.sigmoid(gate_ref[...])
    bd = _head_block_diag()
    y = y_ref[...]
    inv_n = 1.0 / RWKV_HEAD_DIM
    mean = _head_sum(y, bd) * inv_n
    d = y - mean
    var = _head_sum(d * d, bd) * inv_n
    yn = d * lax.rsqrt(var + GN_EPS) * lnw_ref[...] + lnb_ref[...]
    rw = (yn + bonus_ref[...].astype(F32)) * g_ref[...].astype(F32)
    wf = fox.shape[1]
    mix = _dot(fox.astype(BF16), w_ref[:wf, :]) + _dot(rw.astype(BF16), w_ref[wf:, :])
    out_ref[...] = x_ref[...] + g1_ref[0] * mix


def _mix_out(o_fox, proj, gate_col, y, bonus, g, x2, g1, ln_w, ln_b, w_out_bf, tokens_per_batch):
    t, d = x2.shape
    wf = o_fox.shape[1]
    wr = y.shape[1]
    tm = min(512, tokens_per_batch)
    tpb = tokens_per_batch // tm
    return pl.pallas_call(
        _mix_out_kernel,
        grid=(t // tm,),
        in_specs=[pl.BlockSpec((tm, wf), lambda i: (i, 0)),
                  pl.BlockSpec((tm, wf), lambda i: (i, gate_col // wf)),
                  pl.BlockSpec((tm, wr), lambda i: (i, 0)),
                  pl.BlockSpec((tm, wr), lambda i: (i, 0)),
                  pl.BlockSpec((tm, wr), lambda i: (i, 0)),
                  pl.BlockSpec((tm, d), lambda i: (i, 0)),
                  pl.BlockSpec((1, 1, d), lambda i: (i // tpb, 0, 0)),
                  pl.BlockSpec((1, wr), lambda i: (0, 0)),
                  pl.BlockSpec((1, wr), lambda i: (0, 0)),
                  pl.BlockSpec(w_out_bf.shape, lambda i: (0, 0))],
        out_specs=pl.BlockSpec((tm, d), lambda i: (i, 0)),
        out_shape=jax.ShapeDtypeStruct((t, d), F32),
        compiler_params=_cparams(("parallel",)),
        name="mix_out",
    )(o_fox, proj, y, bonus, g, x2, g1, ln_w, ln_b, w_out_bf)


def _top_k_mask_rows(s, k, tie_safe, want_rank=True):
    n = s.shape[0]
    s0 = s
    row = lax.broadcasted_iota(jnp.int32, s.shape, 0).astype(F32)
    rank = jnp.full(s.shape, float(k), F32) if want_rank else None
    vals = []
    for r in range(k):
        m = jnp.max(s, axis=0, keepdims=True)
        hit = s == m
        if tie_safe:
            pos = jnp.min(jnp.where(hit, row, float(n)), axis=0, keepdims=True)
            hit = row == pos
        vals.append(m)
        if want_rank:
            rank = jnp.where(hit, float(r), rank)
        s = jnp.where(hit, -jnp.inf, s)
    picked = (s == -jnp.inf) & (s0 != -jnp.inf)
    n_picked = jnp.sum(jnp.where(picked, 1.0, 0.0), axis=0, keepdims=True)
    return jnp.concatenate(vals, axis=0), rank, picked, jnp.where(n_picked == float(k), 1.0, 0.0)


def _candidate_rows():
    k = PEER_TOPK
    groups, valid = [], []
    for r0 in range(k // 2):
        n1 = k // (r0 + 1)
        for g in range(-(-n1 // 8)):
            groups.append((r0, g * 8))
            valid.append([g * 8 + i < n1 for i in range(8)])
    groups.append((None, k // 2))
    valid.append([True] * 8)
    return groups, valid


def _route_head(sa, sb, tie_safe):
    k = PEER_TOPK
    tm = sa.shape[1]
    groups, valid = _candidate_rows()
    sub8 = lax.broadcasted_iota(jnp.int32, (8, tm), 0)
    top_a, rank_a, _, clean_a = _top_k_mask_rows(sa, k, tie_safe)
    top_b, rank_b, _, clean_b = _top_k_mask_rows(sb, k, tie_safe)
    parts = []
    for (r0, off), ok in zip(groups, valid):
        if r0 is None:
            part = top_a[off:off + 8, :] + top_b[0:1, :]
        else:
            part = top_a[r0:r0 + 1, :] + top_b[off:off + 8, :]
        if not all(ok):
            part = jnp.where(sub8 < sum(ok), part, -jnp.inf)
        parts.append(part)
    cand = jnp.concatenate(parts, axis=0)
    _, _, sel, clean_c = _top_k_mask_rows(cand, k, tie_safe, want_rank=False)
    z = jnp.sum(jnp.where(sel, jnp.exp(cand - cand[0:1, :]), 0.0), axis=0, keepdims=True)
    self32 = jnp.where(sel, 1.0, 0.0)
    n_keys = sa.shape[0]
    rank3 = rank_a.astype(BF16).reshape(n_keys // BF16_ROWS, BF16_ROWS, tm)
    length3 = jnp.zeros(rank3.shape, BF16)

    def add_count(r0, count_row):
        tile = jnp.broadcast_to(count_row, (BF16_ROWS, tm)).astype(BF16)
        return length3 + jnp.where(rank3 == float(r0), tile[None], 0.0).astype(BF16)

    tail = None
    per_rank = {}
    for gi, (r0, off) in enumerate(groups):
        cnt = self32[gi * 8:(gi + 1) * 8, :]
        if r0 is None:
            tail = cnt
        else:
            c = jnp.sum(cnt, axis=0, keepdims=True)
            per_rank[r0] = c if r0 not in per_rank else per_rank[r0] + c
    for r in range(8):
        per_rank[k // 2 + r] = tail[r:r + 1, :]
    for r0, c in per_rank.items():
        length3 = add_count(r0, c)
    length = length3.reshape(n_keys, tm).astype(F32)
    pa = jnp.exp(sa - top_a[0:1, :])
    qb = jnp.exp(sb - top_b[0:1, :]) / z
    return pa, length, qb, rank_b, clean_a * clean_b * clean_c


def _peer_query_kernel(x_ref, nw_ref, sc_ref, sh_ref, wq_ref, ht_ref, q_ref):
    h = _modulated_norm(x_ref[...], nw_ref[...], sc_ref[0], sh_ref[0])
    ht_ref[...] = h.T.astype(BF16)
    q = _dot(h.astype(BF16), wq_ref[...])
    dk = q.shape[1] // PEER_HEADS
    for hd in range(PEER_HEADS):
        q_ref[hd] = q[:, hd * dk:(hd + 1) * dk]


def _peer_query(x1, norm_w, sc, sh, wq_bf, tokens_per_batch):
    t, d = x1.shape
    dk = wq_bf.shape[1] // PEER_HEADS
    tm = min(512, tokens_per_batch)
    tpb = tokens_per_batch // tm
    return pl.pallas_call(
        _peer_query_kernel,
        grid=(t // tm,),
        in_specs=[pl.BlockSpec((tm, d), lambda i: (i, 0)),
                  pl.BlockSpec((1, d), lambda i: (0, 0)),
                  pl.BlockSpec((1, 1, d), lambda i: (i // tpb, 0, 0)),
                  pl.BlockSpec((1, 1, d), lambda i: (i // tpb, 0, 0)),
                  pl.BlockSpec(wq_bf.shape, lambda i: (0, 0))],
        out_specs=[pl.BlockSpec((d, tm), lambda i: (0, i)),
                   pl.BlockSpec((PEER_HEADS, tm, dk), lambda i: (0, i, 0))],
        out_shape=[jax.ShapeDtypeStruct((d, t), BF16),
                   jax.ShapeDtypeStruct((PEER_HEADS, t, dk), F32)],
        compiler_params=_cparams(("parallel",)),
        name="peer_query",
    )(x1, norm_w, sc, sh, wq_bf)


def _gelu_exact(x):
    return 0.5 * x * (1.0 + lax.erf(x * 0.7071067811865476))


def _peer_route_act_kernel(ht_ref, q_ref, keys_ref, u_ref,
                           act_ref, pa_ref, len_ref, qb_ref, rnk_ref, code_scr):
    hd = pl.program_id(1)
    tm = ht_ref.shape[1]
    half = keys_ref.shape[-1]
    sub = min(ROUTE_SUB_TOKENS, tm)
    blocked = (PEER_NKEYS // PEER_SLABS, PEER_SLABS, sub)

    @pl.when(hd == 0)
    def _():
        code_scr[...] = jnp.zeros(code_scr.shape, F32)

    def scores(h, t0):
        q = q_ref[h, t0:t0 + sub, :]
        return (_dot_nt(keys_ref[h, 0], q[:, :half], HIGHEST),
                _dot_nt(keys_ref[h, 1], q[:, half:], HIGHEST))

    def route(h, t0, sa, sb, tie_safe):
        pa, length, qbt, rank_b, clean = _route_head(sa, sb, tie_safe)
        pa_ref[h, :, :, t0:t0 + sub] = pa.reshape(blocked)
        len_ref[h, :, :, t0:t0 + sub] = length.reshape(blocked)
        qb_ref[h, :, t0:t0 + sub] = qbt.astype(BF16)
        rnk_ref[h, :, t0:t0 + sub] = rank_b.astype(BF16)
        return clean

    starts = range(0, tm, sub)
    score_tiles = [scores(hd, t0) for t0 in starts]
    rows_per_dot = 256
    cols_per_dot = 256
    for r0 in range(0, u_ref.shape[0], rows_per_dot):
        for c0 in range(0, tm, cols_per_dot):
            act_ref[r0:r0 + rows_per_dot, c0:c0 + cols_per_dot] = _dot(
                u_ref[r0:r0 + rows_per_dot, :], ht_ref[:, c0:c0 + cols_per_dot]).astype(BF16)

    dirty = jnp.zeros((1, 1), F32)
    for t0, (sa, sb) in zip(starts, score_tiles):
        dirty = jnp.maximum(dirty, 1.0 - jnp.min(route(hd, t0, sa, sb, False), axis=1, keepdims=True))
    code_scr[0:1, 0:1] += dirty * lax.shift_left(1, hd).astype(F32)

    @pl.when(hd == pl.num_programs(1) - 1)
    def _():
        tie_bits = code_scr[0, 0].astype(jnp.int32)

        def redo(h, carry):
            @pl.when(((tie_bits >> h) & 1) == 1)
            def _():
                for t0 in starts:
                    route(h, t0, *scores(h, t0), True)
            return carry

        lax.fori_loop(0, PEER_HEADS, redo, 0)


def _peer_route_act(ht_bf, q_heads, sub_keys, u_bf, tokens_per_batch):
    d, t = ht_bf.shape
    ne = u_bf.shape[0]
    bu = ne // PEER_HEADS
    dk = q_heads.shape[-1]
    tm = min(512, tokens_per_batch)
    n_blocks = PEER_NKEYS // PEER_SLABS
    dense = jax.ShapeDtypeStruct((PEER_HEADS, PEER_NKEYS, t), BF16)
    dense_spec = pl.BlockSpec((PEER_HEADS, PEER_NKEYS, tm), lambda i, j: (0, 0, i))
    blocked = jax.ShapeDtypeStruct((PEER_HEADS, n_blocks, PEER_SLABS, t), F32)
    blocked_spec = pl.BlockSpec((PEER_HEADS, n_blocks, PEER_SLABS, tm), lambda i, j: (0, 0, 0, i))
    return pl.pallas_call(
        _peer_route_act_kernel,
        grid=(t // tm, PEER_HEADS),
        in_specs=[pl.BlockSpec((d, tm), lambda i, j: (0, i)),
                  pl.BlockSpec((PEER_HEADS, tm, dk), lambda i, j: (0, i, 0)),
                  pl.BlockSpec(sub_keys.shape, lambda i, j: (0, 0, 0, 0)),
                  pl.BlockSpec((bu, d), lambda i, j: (j, 0))],
        out_specs=[pl.BlockSpec((bu, tm), lambda i, j: (j, i)),
                   blocked_spec, blocked_spec, dense_spec, dense_spec],
        out_shape=[jax.ShapeDtypeStruct((ne, t), BF16), blocked, blocked, dense, dense],
        scratch_shapes=[pltpu.VMEM((8, LANES), F32)],
        compiler_params=_cparams(("parallel", "arbitrary")),
        name="peer_route_act",
    )(ht_bf, q_heads, sub_keys, u_bf)


def _transpose_blocks_kernel(x_ref, o_ref):
    o_ref[0] = x_ref[...].T.astype(BF16)


def _transpose_blocks(w, block):
    n, d = w.shape
    return pl.pallas_call(
        _transpose_blocks_kernel,
        grid=(n // block,),
        in_specs=[pl.BlockSpec((block, d), lambda j: (j, 0))],
        out_specs=pl.BlockSpec((1, d, block), lambda j: (j, 0, 0)),
        out_shape=jax.ShapeDtypeStruct((n // block, d, block), BF16),
        compiler_params=_cparams(("parallel",)),
        name="transpose_blocks",
    )(w)


def _peer_expert_kernel(act_ref, vt_ref, pa_ref, len_ref, qb_ref, rnk_ref, x_ref, g2_ref, o_ref, acc_scr):
    j = pl.program_id(1)
    be, tm = act_ref.shape

    @pl.when(j == 0)
    def _():
        acc_scr[...] = jnp.zeros(acc_scr.shape, F32)

    n_slabs = be // PEER_NKEYS
    rows = BF16_ROWS
    n_groups = PEER_NKEYS // rows

    def row_tile(ref, hd, s):
        return jnp.broadcast_to(ref[hd, 0, s:s + 1, :], (rows, tm)).astype(BF16)

    zero = jnp.zeros((rows, tm), BF16)
    parts = [[None] * n_groups for _ in range(n_slabs)]
    slab_group = 8
    for s0 in range(0, n_slabs, slab_group):
        ss = range(s0, min(s0 + slab_group, n_slabs))
        ln = {(hd, s): row_tile(len_ref, hd, s) for hd in range(PEER_HEADS) for s in ss}
        pa = {(hd, s): row_tile(pa_ref, hd, s) for hd in range(PEER_HEADS) for s in ss}
        for g in range(n_groups):
            sl = slice(g * rows, (g + 1) * rows)
            acc = {}
            for hd in range(PEER_HEADS):
                rk = rnk_ref[hd, sl, :]
                qv = qb_ref[hd, sl, :]
                for s in ss:
                    term = jnp.where(rk < ln[hd, s], qv, zero) * pa[hd, s]
                    acc[s] = term if s not in acc else acc[s] + term
            for s in ss:
                lo = s * PEER_NKEYS + g * rows
                parts[s][g] = acc[s] * _gelu_exact(act_ref[lo:lo + rows, :].astype(F32)).astype(BF16)
    p = jnp.concatenate([parts[s][g] for s in range(n_slabs) for g in range(n_groups)], axis=0)
    acc_scr[...] += _dot(vt_ref[0], p)

    @pl.when(j == pl.num_programs(1) - 1)
    def _():
        o_ref[...] = x_ref[...] + g2_ref[0] * acc_scr[...].T


def _peer_expert(act, vt_bf, pa, ln, qb, rnk, x1, g2, tokens_per_batch):
    t, d = x1.shape
    ne = act.shape[0]
    tm = min(512, tokens_per_batch)
    tpb = tokens_per_batch // tm
    be = PEER_EXPERT_BLOCK
    dense_spec = pl.BlockSpec((PEER_HEADS, PEER_NKEYS, tm), lambda i, j: (0, 0, i))
    blocked_spec = pl.BlockSpec((PEER_HEADS, 1, PEER_SLABS, tm), lambda i, j: (0, j, 0, i))
    return pl.pallas_call(
        _peer_expert_kernel,
        grid=(t // tm, ne // be),
        in_specs=[pl.BlockSpec((be, tm), lambda i, j: (j, i)),
                  pl.BlockSpec((1, d, be), lambda i, j: (j, 0, 0)),
                  blocked_spec, blocked_spec, dense_spec, dense_spec,
                  pl.BlockSpec((tm, d), lambda i, j: (i, 0)),
                  pl.BlockSpec((1, 1, d), lambda i, j: (i // tpb, 0, 0))],
        out_specs=pl.BlockSpec((tm, d), lambda i, j: (i, 0)),
        out_shape=jax.ShapeDtypeStruct((t, d), F32),
        scratch_shapes=[pltpu.VMEM((d, tm), F32)],
        compiler_params=_cparams(("parallel", "arbitrary")),
        name="peer_expert",
    )(act, vt_bf, pa, ln, qb, rnk, x1, g2)


def _pad_cols(a, n):
    return jnp.pad(a, ((0, 0), (0, n - a.shape[1])))


def _pad_rows(a, n):
    return jnp.pad(a, ((0, n - a.shape[0]), (0, 0)))


def _layer(x, c, w_ada, b_ada, norm_mix_w, w_in, fox_q_norm_w, fox_k_norm_w, fox_f_bias,
           rwkv_mu, rwkv_w0, rwkv_w_up, rwkv_a0, rwkv_a_up, rwkv_g_up, rwkv_k_k, rwkv_k_a,
           rwkv_r_k, rwkv_ln_w, rwkv_ln_b, w_out, norm_ffn_w, peer_w_query, peer_sub_keys,
           peer_u, peer_v):
    B, S, D = x.shape
    T = B * S
    fw = FOX_HEADS * FOX_HEAD_DIM
    rw = RWKV_HEADS * RWKV_HEAD_DIM
    w_lora = rwkv_w_up.shape[0]
    a_lora = rwkv_a_up.shape[0]
    g_lora = rwkv_g_up.shape[0]
    assert w_lora <= LANES and a_lora + FOX_HEADS <= LANES and g_lora == 256

    c_pad = _pad_rows(c, 8)
    mod = _ada_mod(c_pad, w_ada, b_ada)[:B]
    sh1, sc1, g1, sh2, sc2, g2 = [m.reshape(B, 1, D) for m in jnp.split(mod, 6, axis=-1)]

    fox_cols = 4 * fw + FOX_HEADS
    wi_fox, wi_rw = w_in[:, :fox_cols], w_in[:, fox_cols:]
    mu = rwkv_mu.reshape(1, -1)
    seg = lambda a, lo, n: a[:, lo:lo + n]
    w_perm = jnp.concatenate([
        seg(wi_fox, 0, 4 * fw),
        seg(wi_rw, 0, 3 * rw),
        seg(wi_rw, 3 * rw + w_lora + a_lora, g_lora),
        _pad_cols(seg(wi_rw, 3 * rw, w_lora), LANES),
        _pad_cols(jnp.concatenate([seg(wi_rw, 3 * rw + w_lora, a_lora), seg(wi_fox, 4 * fw, FOX_HEADS)], 1), LANES),
    ], axis=1).astype(BF16)
    cols = {"gate": 3 * fw, "rr": 4 * fw, "rk": 4 * fw + rw, "rv": 4 * fw + 2 * rw,
            "gd": 4 * fw + 3 * rw, "lo": 4 * fw + 3 * rw + g_lora}
    f_lane = a_lora
    f_block = (cols["lo"] + LANES) // LANES
    mus = [seg(mu, 0, rw), seg(mu, rw, rw), seg(mu, 2 * rw, rw),
           seg(mu, 3 * rw + w_lora + a_lora, g_lora),
           jnp.concatenate([_pad_cols(seg(mu, 3 * rw, w_lora), LANES),
                            _pad_cols(seg(mu, 3 * rw + w_lora, a_lora), LANES)], 1)]
    scale = FOX_HEAD_DIM ** -0.5 * LOG2_E
    head_w = _pad_cols(jnp.concatenate([jnp.tile(fox_q_norm_w * scale, FOX_HEADS),
                                        jnp.tile(fox_k_norm_w, FOX_HEADS)]).reshape(1, -1), w_perm.shape[1])

    x2 = x.reshape(T, D)
    proj = _in_proj(_norm_mod(x2, norm_mix_w.reshape(1, D), sc1, sh1, S), w_perm, head_w, 2 * fw)

    f_bias_row = jnp.zeros((1, LANES), F32).at[0, f_lane:f_lane + FOX_HEADS].set(fox_f_bias)
    cum = _fox_cum(proj, f_bias_row, B, S, f_block)
    cum_rows = cum.reshape(B, S, LANES)[:, :, f_lane:f_lane + FOX_HEADS].transpose(0, 2, 1).reshape(B, FOX_HEADS, 1, S)
    o_fox = _fox_attn(proj, cum_rows, B, S)

    row = lambda a: a.reshape(1, -1)
    r, lw, k, v, a_vec, b_vec, g, bonus = _rwkv_prep(
        proj, cols, mus, row(rwkv_w0), _pad_rows(rwkv_w_up, LANES), row(rwkv_a0), _pad_rows(rwkv_a_up, LANES),
        rwkv_g_up, row(rwkv_k_k), row(rwkv_k_a), row(rwkv_r_k), S)
    y = _rwkv_scan(r, lw, k, v, a_vec, b_vec, B, S)

    x1 = _mix_out(o_fox, proj, cols["gate"], y, bonus, g, x2, g1, row(rwkv_ln_w), row(rwkv_ln_b),
                  w_out.astype(BF16), S)

    ht, q_heads = _peer_query(x1, norm_ffn_w.reshape(1, D), sc2, sh2, peer_w_query.astype(BF16), S)
    act, pa, ln, qb, rnk = _peer_route_act(ht, q_heads, peer_sub_keys, peer_u.astype(BF16), S)
    vt = _transpose_blocks(peer_v, PEER_EXPERT_BLOCK)
    out = _peer_expert(act, vt, pa, ln, qb, rnk, x1, g2, S)
    return out.reshape(B, S, D)


def kernel(x, c, w_ada, b_ada, norm_mix_w, w_in, fox_q_norm_w, fox_k_norm_w, fox_f_bias, rwkv_mu, rwkv_w0,
           rwkv_w_up, rwkv_a0, rwkv_a_up, rwkv_g_up, rwkv_k_k, rwkv_k_a, rwkv_r_k, rwkv_ln_w, rwkv_ln_b,
           w_out, norm_ffn_w, peer_w_query, peer_sub_keys, peer_u, peer_v):
    params = (w_ada, b_ada, norm_mix_w, w_in, fox_q_norm_w, fox_k_norm_w, fox_f_bias, rwkv_mu, rwkv_w0,
              rwkv_w_up, rwkv_a0, rwkv_a_up, rwkv_g_up, rwkv_k_k, rwkv_k_a, rwkv_r_k, rwkv_ln_w, rwkv_ln_b,
              w_out, norm_ffn_w, peer_w_query, peer_sub_keys, peer_u, peer_v)
    for l in range(w_ada.shape[0]):
        x = _layer(x, c, *[p[l] for p in params])
    return x
```

```python
import functools

import jax
import jax.numpy as jnp
from jax import lax
from jax.experimental import pallas as pl
from jax.experimental.pallas import tpu as pltpu

F32 = jnp.float32
BF16 = jnp.bfloat16
HIGHEST = lax.Precision.HIGHEST

LANES = 128
BF16_ROWS = 16
NORM_EPS = 1e-6
GN_EPS = 64e-5
CHUNK = 64
FOX_HEADS = 8
FOX_HEAD_DIM = 128
RWKV_HEADS = 16
RWKV_HEAD_DIM = 64
PEER_HEADS = 8
PEER_NKEYS = 128
PEER_TOPK = 16
PEER_EXPERT_BLOCK = 2048
PEER_SLABS = PEER_EXPERT_BLOCK // PEER_NKEYS
ROUTE_SUB_TOKENS = 256
NEG_BIG = -1e30
LOG2_E = 1.4426950408889634
VMEM_LIMIT = 56 * 1024 * 1024


def _cparams(sem):
    return pltpu.CompilerParams(dimension_semantics=sem, vmem_limit_bytes=VMEM_LIMIT)


def _dot(a, b, precision=None):
    return jnp.dot(a, b, preferred_element_type=F32, precision=precision)


def _dot_nt(a, b, precision=None):
    return lax.dot_general(a, b, (((1,), (1,)), ((), ())), preferred_element_type=F32,
                           precision=precision)


def _split_bf16(x):
    hi = x.astype(BF16)
    return hi, (x - hi.astype(F32)).astype(BF16)


def _dot_bf16x3(a, b):
    ah, al = _split_bf16(a)
    bh, bl = _split_bf16(b)
    return _dot(ah, bh) + (_dot(ah, bl) + _dot(al, bh))


def _dot_tn(a, b, precision=None):
    return lax.dot_general(a, b, (((0,), (0,)), ((), ())), preferred_element_type=F32,
                           precision=precision)


def _ada_kernel(c_ref, w_ref, b_ref, o_ref):
    c = c_ref[...]
    s = c * jax.nn.sigmoid(c)
    o_ref[...] = _dot(s, w_ref[...], HIGHEST) + b_ref[...]


def _ada_mod(c_pad, w_ada, b_ada):
    rows, d = c_pad.shape
    n = w_ada.shape[1]
    bn = 1024
    return pl.pallas_call(
        _ada_kernel,
        grid=(n // bn,),
        in_specs=[pl.BlockSpec((rows, d), lambda j: (0, 0)),
                  pl.BlockSpec((d, bn), lambda j: (0, j)),
                  pl.BlockSpec((1, bn), lambda j: (0, j))],
        out_specs=pl.BlockSpec((rows, bn), lambda j: (0, j)),
        out_shape=jax.ShapeDtypeStruct((rows, n), F32),
        compiler_params=_cparams(("arbitrary",)),
        name="ada_mod",
    )(c_pad, w_ada, b_ada.reshape(1, n))


def _modulated_norm(x, nw, sc, sh):
    y = x * lax.rsqrt(jnp.mean(x * x, axis=-1, keepdims=True) + NORM_EPS)
    return y * nw * (1.0 + sc) + sh


def _norm_mod_kernel(x_ref, nw_ref, sc_ref, sh_ref, o_ref):
    o_ref[...] = _modulated_norm(x_ref[...], nw_ref[...], sc_ref[0], sh_ref[0]).astype(BF16)


def _norm_mod(x2, norm_w, sc, sh, tokens_per_batch):
    t, d = x2.shape
    tm = min(512, tokens_per_batch)
    tpb = tokens_per_batch // tm
    return pl.pallas_call(
        _norm_mod_kernel,
        grid=(t // tm,),
        in_specs=[pl.BlockSpec((tm, d), lambda i: (i, 0)),
                  pl.BlockSpec((1, d), lambda i: (0, 0)),
                  pl.BlockSpec((1, 1, d), lambda i: (i // tpb, 0, 0)),
                  pl.BlockSpec((1, 1, d), lambda i: (i // tpb, 0, 0))],
        out_specs=pl.BlockSpec((tm, d), lambda i: (i, 0)),
        out_shape=jax.ShapeDtypeStruct((t, d), BF16),
        compiler_params=_cparams(("parallel",)),
        name="norm_mod",
    )(x2, norm_w, sc, sh)


def _in_proj_kernel(h_ref, w_ref, hw_ref, o_ref, *, n_qk_blocks):
    j = pl.program_id(1)
    acc = _dot(h_ref[...], w_ref[...])

    @pl.when(j < n_qk_blocks)
    def _():
        for hh in range(acc.shape[1] // FOX_HEAD_DIM):
            sl = slice(hh * FOX_HEAD_DIM, (hh + 1) * FOX_HEAD_DIM)
            a = acc[:, sl]
            rs = lax.rsqrt(jnp.mean(a * a, axis=-1, keepdims=True) + NORM_EPS)
            o_ref[:, sl] = a * rs * hw_ref[:, sl]

    @pl.when(j >= n_qk_blocks)
    def _():
        o_ref[...] = acc


def _in_proj(h_bf, w_bf, head_w, n_qk_cols):
    t, d = h_bf.shape
    n = w_bf.shape[1]
    tm = min(2048, t)
    bn = 512
    kern = functools.partial(_in_proj_kernel, n_qk_blocks=n_qk_cols // bn)
    return pl.pallas_call(
        kern,
        grid=(t // tm, n // bn),
        in_specs=[pl.BlockSpec((tm, d), lambda i, j: (i, 0)),
                  pl.BlockSpec((d, bn), lambda i, j: (0, j)),
                  pl.BlockSpec((1, bn), lambda i, j: (0, j))],
        out_specs=pl.BlockSpec((tm, bn), lambda i, j: (i, j)),
        out_shape=jax.ShapeDtypeStruct((t, n), F32),
        compiler_params=_cparams(("parallel", "arbitrary")),
        name="in_proj",
    )(h_bf, w_bf, head_w)


def _log_sigmoid(x):
    return jnp.minimum(x, 0.0) - jnp.log(1.0 + jnp.exp(-jnp.abs(x)))


def _fox_cum_kernel(f_ref, b_ref, o_ref, *, blk):
    s = f_ref.shape[0]
    row = lax.broadcasted_iota(jnp.int32, (blk, blk), 0)
    col = lax.broadcasted_iota(jnp.int32, (blk, blk), 1)
    tri = (row >= col).astype(F32)
    carry = jnp.zeros((1, f_ref.shape[1]), F32)
    for i in range(s // blk):
        lf = _log_sigmoid(f_ref[i * blk:(i + 1) * blk, :] + b_ref[...])
        cs = _dot(tri, lf, HIGHEST) + carry
        o_ref[i * blk:(i + 1) * blk, :] = cs * LOG2_E
        carry = cs[blk - 1:blk, :]


def _fox_cum(proj, f_bias_row, batch, seq, col_block):
    blk = min(256, seq)
    return pl.pallas_call(
        functools.partial(_fox_cum_kernel, blk=blk),
        grid=(batch,),
        in_specs=[pl.BlockSpec((seq, LANES), lambda b: (b, col_block)),
                  pl.BlockSpec((1, LANES), lambda b: (0, 0))],
        out_specs=pl.BlockSpec((seq, LANES), lambda b: (b, 0)),
        out_shape=jax.ShapeDtypeStruct((batch * seq, LANES), F32),
        compiler_params=_cparams(("parallel",)),
        name="fox_cum",
    )(proj, f_bias_row)


def _fox_attn_kernel(qt_ref, kt_ref, q_ref, k_ref, v_ref, c_ref, o_ref, m_scr, l_scr, acc_scr):
    t = pl.program_id(2)
    qi = qt_ref[t]
    ki = kt_ref[t]
    hd = FOX_HEAD_DIM
    n_heads = q_ref.shape[1] // hd

    @pl.when(ki == 0)
    def _():
        m_scr[...] = jnp.full(m_scr.shape, NEG_BIG, F32)
        l_scr[...] = jnp.zeros(l_scr.shape, F32)
        acc_scr[...] = jnp.zeros(acc_scr.shape, F32)

    def step(masked):
        hs = range(n_heads)
        sl = [slice(h * hd, (h + 1) * hd) for h in hs]
        s = [_dot_nt(q_ref[:, sl[h]].astype(BF16), k_ref[:, sl[h]].astype(BF16)) - c_ref[0, h] for h in hs]
        if masked:
            row = lax.broadcasted_iota(jnp.int32, s[0].shape, 0)
            col = lax.broadcasted_iota(jnp.int32, s[0].shape, 1)
            s = [jnp.where(row >= col, x, NEG_BIG) for x in s]
        m_prev = [m_scr[h] for h in hs]
        m_new = [jnp.maximum(m_prev[h], jnp.max(s[h], axis=-1, keepdims=True)) for h in hs]
        n_rep = s[0].shape[1] // hd
        p = [jnp.exp2(s[h] - jnp.concatenate([m_new[h]] * n_rep, axis=1)) for h in hs]
        alpha = [jnp.exp2(m_prev[h] - m_new[h]) for h in hs]
        ones = jnp.ones((k_ref.shape[0], hd), BF16)
        pv = [_dot(p[h].astype(BF16), jnp.concatenate([v_ref[:, sl[h]].astype(BF16), ones], axis=1))
              for h in hs]
        for h in hs:
            l_scr[h] = alpha[h] * l_scr[h] + pv[h][:, hd:]
            acc_scr[:, sl[h]] = alpha[h] * acc_scr[:, sl[h]] + pv[h][:, :hd]
            m_scr[h] = m_new[h]

    @pl.when(ki < qi)
    def _():
        step(False)

    @pl.when(ki == qi)
    def _():
        step(True)
        for h in range(n_heads):
            sl = slice(h * hd, (h + 1) * hd)
            o_ref[:, sl] = acc_scr[:, sl] / l_scr[h]


def _fox_attn(proj, cum_rows, batch, seq):
    tq = min(512, seq)
    nq = seq // tq
    hps = 8
    w = hps * FOX_HEAD_DIM
    kcol = FOX_HEADS // hps
    tri = [(q, k) for q in range(nq) for k in range(q + 1)]
    qt = jnp.asarray([q for q, _ in tri], jnp.int32)
    kt = jnp.asarray([k for _, k in tri], jnp.int32)
    grid_spec = pltpu.PrefetchScalarGridSpec(
        num_scalar_prefetch=2,
        grid=(batch, kcol, len(tri)),
        in_specs=[
            pl.BlockSpec((tq, w), lambda b, h, t, qt, kt: (b * nq + qt[t], h)),
            pl.BlockSpec((tq, w), lambda b, h, t, qt, kt: (b * nq + kt[t], kcol + h)),
            pl.BlockSpec((tq, w), lambda b, h, t, qt, kt: (b * nq + kt[t], 2 * kcol + h)),
            pl.BlockSpec((1, hps, 1, tq), lambda b, h, t, qt, kt: (b, h, 0, kt[t])),
        ],
        out_specs=pl.BlockSpec((tq, w), lambda b, h, t, qt, kt: (b * nq + qt[t], h)),
        scratch_shapes=[pltpu.VMEM((hps, tq, FOX_HEAD_DIM), F32), pltpu.VMEM((hps, tq, FOX_HEAD_DIM), F32),
                        pltpu.VMEM((tq, w), F32)],
    )
    return pl.pallas_call(
        _fox_attn_kernel,
        grid_spec=grid_spec,
        out_shape=jax.ShapeDtypeStruct((batch * seq, FOX_HEADS * FOX_HEAD_DIM), F32),
        compiler_params=_cparams(("parallel", "parallel", "arbitrary")),
        name="fox_attn",
    )(qt, kt, proj, proj, proj, cum_rows)


def _head_sum(x, bd):
    hi, lo = _split_bf16(x)
    parts = [_dot(hi[:, j * LANES:(j + 1) * LANES], bd) + _dot(lo[:, j * LANES:(j + 1) * LANES], bd)
             for j in range(x.shape[1] // LANES)]
    return jnp.concatenate(parts, axis=1)


def _head_block_diag():
    r = lax.broadcasted_iota(jnp.int32, (LANES, LANES), 0) // RWKV_HEAD_DIM
    c = lax.broadcasted_iota(jnp.int32, (LANES, LANES), 1) // RWKV_HEAD_DIM
    return (r == c).astype(BF16)


def _rwkv_prep_kernel(r_ref, k_ref, v_ref, gd_ref, lo_ref, pr_ref, pk_ref, pv_ref, pgd_ref, plo_ref,
                      mu_r, mu_k, mu_v, mu_gd, mu_lo, w0_ref, wup_ref, a0_ref, aup_ref, gup_ref,
                      kk_ref, ka_ref, rk_ref,
                      or_ref, olw_ref, ok_ref, ov_ref, oa_ref, ob_ref, og_ref, obonus_ref, *, tiles_per_batch):
    i = pl.program_id(0)
    first = (i % tiles_per_batch) == 0

    def shifted(cur_ref, prv_ref, mu_ref):
        cur = cur_ref[...]
        last = jnp.where(first, 0.0, prv_ref[7:8, :])
        row = lax.broadcasted_iota(jnp.int32, cur.shape, 0)
        prev = jnp.where(row == 0, last, pltpu.roll(cur, 1, 0))
        return cur + (prev - cur) * mu_ref[...]

    r = shifted(r_ref, pr_ref, mu_r)
    k = shifted(k_ref, pk_ref, mu_k)
    v = shifted(v_ref, pv_ref, mu_v)
    gd = shifted(gd_ref, pgd_ref, mu_gd)
    lo = shifted(lo_ref, plo_ref, mu_lo)
    wd = lo[:, :LANES]
    ad = lo[:, LANES:]

    w_pre = w0_ref[...] + _dot_bf16x3(jnp.tanh(wd), wup_ref[...])
    w_raw = _log_sigmoid(w_pre) - 0.5
    log_decay = -jnp.exp(w_raw)
    a = jax.nn.sigmoid(a0_ref[...] + _dot_bf16x3(ad, aup_ref[...]))
    g = _dot_bf16x3(jax.nn.sigmoid(gd), gup_ref[...])

    bd = _head_block_diag()
    kk = k * kk_ref[...]
    nrm = jnp.maximum(jnp.sqrt(_head_sum(kk * kk, bd)), 1e-12)
    kk = kk / nrm
    k_mod = k * (1.0 + (a - 1.0) * ka_ref[...])
    bonus = _head_sum(r * k_mod * rk_ref[...], bd) * v

    or_ref[...] = r.astype(BF16)
    olw_ref[...] = log_decay
    ok_ref[...] = k_mod.astype(BF16)
    ov_ref[...] = v.astype(BF16)
    oa_ref[...] = (-kk).astype(BF16)
    ob_ref[...] = (kk * a).astype(BF16)
    og_ref[...] = g.astype(BF16)
    obonus_ref[...] = bonus.astype(BF16)


def _rwkv_prep(proj, cols, mus, w0, w_up, a0, a_up, g_up, k_k, k_a, r_k, tokens_per_batch):
    t = proj.shape[0]
    w = RWKV_HEADS * RWKV_HEAD_DIM
    tm = min(512, tokens_per_batch)
    tpb = tokens_per_batch // tm
    widths = [w, w, w, 256, 256]
    offs = [cols["rr"], cols["rk"], cols["rv"], cols["gd"], cols["lo"]]
    cur_specs = [pl.BlockSpec((tm, wd), functools.partial(lambda i, cb: (i, cb), cb=o // wd))
                 for wd, o in zip(widths, offs)]
    prv_specs = [pl.BlockSpec((8, wd), functools.partial(
        lambda i, cb: (jnp.maximum(i * (tm // 8) - 1, 0), cb), cb=o // wd))
        for wd, o in zip(widths, offs)]
    full = lambda a: pl.BlockSpec(a.shape, lambda i: (0,) * a.ndim)
    params = list(mus) + [w0, w_up, a0, a_up, g_up, k_k, k_a, r_k]
    out_spec = pl.BlockSpec((tm, w), lambda i: (i, 0))
    return pl.pallas_call(
        functools.partial(_rwkv_prep_kernel, tiles_per_batch=tpb),
        grid=(t // tm,),
        in_specs=cur_specs + prv_specs + [full(p) for p in params],
        out_specs=[out_spec] * 8,
        out_shape=[jax.ShapeDtypeStruct((t, w), dt) for dt in (BF16, F32, BF16, BF16, BF16, BF16, BF16, BF16)],
        compiler_params=_cparams(("parallel",)),
        name="rwkv_prep",
    )(*([proj] * 10), *params)


def _rwkv_scan_kernel(r_ref, lw_ref, k_ref, v_ref, a_ref, b_ref, o_ref, h_scr):
    c = pl.program_id(1)

    @pl.when(c == 0)
    def _():
        h_scr[...] = jnp.zeros(h_scr.shape, F32)

    C = r_ref.shape[0]
    n_pairs = r_ref.shape[1] // LANES
    P = HIGHEST

    row = lax.broadcasted_iota(jnp.int32, (C, C), 0)
    col = lax.broadcasted_iota(jnp.int32, (C, C), 1)
    tri = (row >= col).astype(F32)
    lw = lw_ref[...]
    cw = _dot(tri, lw, P)
    cw_end = cw[C - 1:C, :]
    e_pos = jnp.exp(cw)
    e_prev = jnp.exp(cw - lw)
    e_neg = jnp.exp(-cw)
    e_end = jnp.exp(cw_end - cw)
    w_end = jnp.exp(cw_end)

    a = a_ref[...].astype(F32)
    b = b_ref[...].astype(F32)
    k = k_ref[...].astype(F32)
    r = r_ref[...].astype(F32)
    at = a * e_prev
    bt = b * e_neg
    kt = k * e_neg
    rt = r * e_pos
    bh = b * e_end
    kh = k * e_end
    v = v_ref[...].astype(F32)

    lane = lax.broadcasted_iota(jnp.int32, (C, LANES), 1)
    head0 = lane < RWKV_HEAD_DIM
    r2 = lax.broadcasted_iota(jnp.int32, (2 * C, 2 * C), 0)
    c2 = lax.broadcasted_iota(jnp.int32, (2 * C, 2 * C), 1)
    same = (r2 // C) == (c2 // C)
    strict = same & (r2 > c2)
    incl = same & (r2 >= c2)
    eye = (r2 == c2).astype(F32)

    def two(x, p):
        xp = x[:, p * LANES:(p + 1) * LANES]
        return jnp.concatenate([jnp.where(head0, xp, 0.0), jnp.where(head0, 0.0, xp)], axis=0)

    def mm(x, y):
        return _dot(x.astype(BF16), y.astype(BF16))

    def mm3(x, y):
        xh, xl = _split_bf16(x)
        yh, yl = _split_bf16(y)
        n = y.shape[1]
        wide = _dot(xh, jnp.concatenate([yh, yl], axis=1))
        return wide[:, :n] + (wide[:, n:] + _dot(xl, yh))

    G = 2 * C
    pairs = range(n_pairs)
    at2 = [two(at, p) for p in pairs]
    rt2 = [two(rt, p) for p in pairs]
    v2 = [two(v, p).astype(BF16) for p in pairs]
    gram = [_dot_nt(jnp.concatenate([at2[p], rt2[p]], axis=0).astype(BF16),
                    jnp.concatenate([two(bt, p), two(kt, p)], axis=0).astype(BF16)) for p in pairs]
    a_ab = [jnp.where(strict, gram[p][:G, :G], 0.0) for p in pairs]
    nmat = list(a_ab)
    pw = list(a_ab)
    for _ in range(max(1, (C - 1).bit_length() - 1)):
        pw = [mm3(pw[p], pw[p]) for p in pairs]
        nmat = [nmat[p] + pw[p] + mm3(nmat[p], pw[p]) for p in pairs]
    akv = [mm(jnp.where(strict, gram[p][:G, G:], 0.0), v2[p]) for p in pairs]
    rhs = [jnp.concatenate([at2[p], akv[p]], axis=1) for p in pairs]
    pq = [(rhs[p] + mm(nmat[p], rhs[p])).astype(BF16) for p in pairs]
    ry = [mm(jnp.where(incl, gram[p][G:, :G], 0.0), pq[p]) for p in pairs]
    mv = [mm(jnp.where(incl, gram[p][G:, G:], 0.0), v2[p]) for p in pairs]
    ge = [_dot_tn(two(bh, p).astype(BF16), pq[p]) for p in pairs]
    kv = [_dot_tn(two(kh, p).astype(BF16), v2[p]) for p in pairs]
    for p in pairs:
        rr = rt2[p] + ry[p][:, :LANES]
        gm = eye * w_end[:, p * LANES:(p + 1) * LANES] + ge[p][:, :LANES]
        yh = mm(jnp.concatenate([rr, gm], axis=0), h_scr[p])
        y2 = yh[:G] + ry[p][:, LANES:] + mv[p]
        h_scr[p] = yh[G:] + ge[p][:, LANES:] + kv[p]
        o_ref[:, p * LANES:(p + 1) * LANES] = y2[:C, :] + y2[C:, :]


def _rwkv_scan(r, lw, k, v, a, b, batch, seq):
    w = r.shape[1]
    nc = seq // CHUNK
    spec = pl.BlockSpec((CHUNK, w), lambda bi, ci: (bi * nc + ci, 0))
    return pl.pallas_call(
        _rwkv_scan_kernel,
        grid=(batch, nc),
        in_specs=[spec] * 6,
        out_specs=spec,
        out_shape=jax.ShapeDtypeStruct((batch * seq, w), F32),
        scratch_shapes=[pltpu.VMEM((w // LANES, LANES, LANES), F32)],
        compiler_params=_cparams(("parallel", "arbitrary")),
        name="rwkv_scan",
    )(r, lw, k, v, a, b)


def _mix_out_kernel(o_ref, gate_ref, y_ref, bonus_ref, g_ref, x_ref, g1_ref, lnw_ref, lnb_ref, w_ref, out_ref):
    fox = o_ref[...] * jax.nn.sigmoid(gate_ref[...])
    bd = _head_block_diag()
    y = y_ref[...]
    inv_n = 1.0 / RWKV_HEAD_DIM
    mean = _head_sum(y, bd) * inv_n
    d = y - mean
    var = _head_sum(d * d, bd) * inv_n
    yn = d * lax.rsqrt(var + GN_EPS) * lnw_ref[...] + lnb_ref[...]
    rw = (yn + bonus_ref[...].astype(F32)) * g_ref[...].astype(F32)
    wf = fox.shape[1]
    mix = _dot(fox.astype(BF16), w_ref[:wf, :]) + _dot(rw.astype(BF16), w_ref[wf:, :])
    out_ref[...] = x_ref[...] + g1_ref[0] * mix


def _mix_out(o_fox, proj, gate_col, y, bonus, g, x2, g1, ln_w, ln_b, w_out_bf, tokens_per_batch):
    t, d = x2.shape
    wf = o_fox.shape[1]
    wr = y.shape[1]
    tm = min(512, tokens_per_batch)
    tpb = tokens_per_batch // tm
    return pl.pallas_call(
        _mix_out_kernel,
        grid=(t // tm,),
        in_specs=[pl.BlockSpec((tm, wf), lambda i: (i, 0)),
                  pl.BlockSpec((tm, wf), lambda i: (i, gate_col // wf)),
                  pl.BlockSpec((tm, wr), lambda i: (i, 0)),
                  pl.BlockSpec((tm, wr), lambda i: (i, 0)),
                  pl.BlockSpec((tm, wr), lambda i: (i, 0)),
                  pl.BlockSpec((tm, d), lambda i: (i, 0)),
                  pl.BlockSpec((1, 1, d), lambda i: (i // tpb, 0, 0)),
                  pl.BlockSpec((1, wr), lambda i: (0, 0)),
                  pl.BlockSpec((1, wr), lambda i: (0, 0)),
                  pl.BlockSpec(w_out_bf.shape, lambda i: (0, 0))],
        out_specs=pl.BlockSpec((tm, d), lambda i: (i, 0)),
        out_shape=jax.ShapeDtypeStruct((t, d), F32),
        compiler_params=_cparams(("parallel",)),
        name="mix_out",
    )(o_fox, proj, y, bonus, g, x2, g1, ln_w, ln_b, w_out_bf)


def _top_k_mask_rows(s, k, tie_safe, want_rank=True):
    n = s.shape[0]
    s0 = s
    row = lax.broadcasted_iota(jnp.int32, s.shape, 0).astype(F32)
    rank = jnp.full(s.shape, float(k), F32) if want_rank else None
    vals = []
    for r in range(k):
        m = jnp.max(s, axis=0, keepdims=True)
        hit = s == m
        if tie_safe:
            pos = jnp.min(jnp.where(hit, row, float(n)), axis=0, keepdims=True)
            hit = row == pos
        vals.append(m)
        if want_rank:
            rank = jnp.where(hit, float(r), rank)
        s = jnp.where(hit, -jnp.inf, s)
    picked = (s == -jnp.inf) & (s0 != -jnp.inf)
    n_picked = jnp.sum(jnp.where(picked, 1.0, 0.0), axis=0, keepdims=True)
    return jnp.concatenate(vals, axis=0), rank, picked, jnp.where(n_picked == float(k), 1.0, 0.0)


def _candidate_rows():
    k = PEER_TOPK
    groups, valid = [], []
    for r0 in range(k // 2):
        n1 = k // (r0 + 1)
        for g in range(-(-n1 // 8)):
            groups.append((r0, g * 8))
            valid.append([g * 8 + i < n1 for i in range(8)])
    groups.append((None, k // 2))
    valid.append([True] * 8)
    return groups, valid


def _route_head(sa, sb, tie_safe):
    k = PEER_TOPK
    tm = sa.shape[1]
    groups, valid = _candidate_rows()
    sub8 = lax.broadcasted_iota(jnp.int32, (8, tm), 0)
    top_a, rank_a, _, clean_a = _top_k_mask_rows(sa, k, tie_safe)
    top_b, rank_b, _, clean_b = _top_k_mask_rows(sb, k, tie_safe)
    parts = []
    for (r0, off), ok in zip(groups, valid):
        if r0 is None:
            part = top_a[off:off + 8, :] + top_b[0:1, :]
        else:
            part = top_a[r0:r0 + 1, :] + top_b[off:off + 8, :]
        if not all(ok):
            part = jnp.where(sub8 < sum(ok), part, -jnp.inf)
        parts.append(part)
    cand = jnp.concatenate(parts, axis=0)
    _, _, sel, clean_c = _top_k_mask_rows(cand, k, tie_safe, want_rank=False)
    z = jnp.sum(jnp.where(sel, jnp.exp(cand - cand[0:1, :]), 0.0), axis=0, keepdims=True)
    self32 = jnp.where(sel, 1.0, 0.0)
    n_keys = sa.shape[0]
    rank3 = rank_a.astype(BF16).reshape(n_keys // BF16_ROWS, BF16_ROWS, tm)
    length3 = jnp.zeros(rank3.shape, BF16)

    def add_count(r0, count_row):
        tile = jnp.broadcast_to(count_row, (BF16_ROWS, tm)).astype(BF16)
        return length3 + jnp.where(rank3 == float(r0), tile[None], 0.0).astype(BF16)

    tail = None
    per_rank = {}
    for gi, (r0, off) in enumerate(groups):
        cnt = self32[gi * 8:(gi + 1) * 8, :]
        if r0 is None:
            tail = cnt
        else:
            c = jnp.sum(cnt, axis=0, keepdims=True)
            per_rank[r0] = c if r0 not in per_rank else per_rank[r0] + c
    for r in range(8):
        per_rank[k // 2 + r] = tail[r:r + 1, :]
    for r0, c in per_rank.items():
        length3 = add_count(r0, c)
    length = length3.reshape(n_keys, tm).astype(F32)
    pa = jnp.exp(sa - top_a[0:1, :])
    qb = jnp.exp(sb - top_b[0:1, :]) / z
    return pa, length, qb, rank_b, clean_a * clean_b * clean_c


def _peer_query_kernel(x_ref, nw_ref, sc_ref, sh_ref, wq_ref, ht_ref, q_ref):
    h = _modulated_norm(x_ref[...], nw_ref[...], sc_ref[0], sh_ref[0])
    ht_ref[...] = h.T.astype(BF16)
    q = _dot(h.astype(BF16), wq_ref[...])
    dk = q.shape[1] // PEER_HEADS
    for hd in range(PEER_HEADS):
        q_ref[hd] = q[:, hd * dk:(hd + 1) * dk]


def _peer_query(x1, norm_w, sc, sh, wq_bf, tokens_per_batch):
    t, d = x1.shape
    dk = wq_bf.shape[1] // PEER_HEADS
    tm = min(512, tokens_per_batch)
    tpb = tokens_per_batch // tm
    return pl.pallas_call(
        _peer_query_kernel,
        grid=(t // tm,),
        in_specs=[pl.BlockSpec((tm, d), lambda i: (i, 0)),
                  pl.BlockSpec((1, d), lambda i: (0, 0)),
                  pl.BlockSpec((1, 1, d), lambda i: (i // tpb, 0, 0)),
                  pl.BlockSpec((1, 1, d), lambda i: (i // tpb, 0, 0)),
                  pl.BlockSpec(wq_bf.shape, lambda i: (0, 0))],
        out_specs=[pl.BlockSpec((d, tm), lambda i: (0, i)),
                   pl.BlockSpec((PEER_HEADS, tm, dk), lambda i: (0, i, 0))],
        out_shape=[jax.ShapeDtypeStruct((d, t), BF16),
                   jax.ShapeDtypeStruct((PEER_HEADS, t, dk), F32)],
        compiler_params=_cparams(("parallel",)),
        name="peer_query",
    )(x1, norm_w, sc, sh, wq_bf)


def _gelu_exact(x):
    return 0.5 * x * (1.0 + lax.erf(x * 0.7071067811865476))


def _peer_route_act_kernel(ht_ref, q_ref, keys_ref, u_ref,
                           act_ref, pa_ref, len_ref, qb_ref, rnk_ref, code_scr):
    hd = pl.program_id(1)
    tm = ht_ref.shape[1]
    half = keys_ref.shape[-1]
    sub = min(ROUTE_SUB_TOKENS, tm)
    blocked = (PEER_NKEYS // PEER_SLABS, PEER_SLABS, sub)

    @pl.when(hd == 0)
    def _():
        code_scr[...] = jnp.zeros(code_scr.shape, F32)

    def scores(h, t0):
        q = q_ref[h, t0:t0 + sub, :]
        return (_dot_nt(keys_ref[h, 0], q[:, :half], HIGHEST),
                _dot_nt(keys_ref[h, 1], q[:, half:], HIGHEST))

    def route(h, t0, sa, sb, tie_safe):
        pa, length, qbt, rank_b, clean = _route_head(sa, sb, tie_safe)
        pa_ref[h, :, :, t0:t0 + sub] = pa.reshape(blocked)
        len_ref[h, :, :, t0:t0 + sub] = length.reshape(blocked)
        qb_ref[h, :, t0:t0 + sub] = qbt.astype(BF16)
        rnk_ref[h, :, t0:t0 + sub] = rank_b.astype(BF16)
        return clean

    starts = range(0, tm, sub)
    score_tiles = [scores(hd, t0) for t0 in starts]
    rows_per_dot = 512
    for r0 in range(0, u_ref.shape[0], rows_per_dot):
        act_ref[r0:r0 + rows_per_dot, :] = _dot(u_ref[r0:r0 + rows_per_dot, :], ht_ref[...]).astype(BF16)

    dirty = jnp.zeros((1, 1), F32)
    for t0, (sa, sb) in zip(starts, score_tiles):
        dirty = jnp.maximum(dirty, 1.0 - jnp.min(route(hd, t0, sa, sb, False), axis=1, keepdims=True))
    code_scr[0:1, 0:1] += dirty * lax.shift_left(1, hd).astype(F32)

    @pl.when(hd == pl.num_programs(1) - 1)
    def _():
        tie_bits = code_scr[0, 0].astype(jnp.int32)

        def redo(h, carry):
            @pl.when(((tie_bits >> h) & 1) == 1)
            def _():
                for t0 in starts:
                    route(h, t0, *scores(h, t0), True)
            return carry

        lax.fori_loop(0, PEER_HEADS, redo, 0)


def _peer_route_act(ht_bf, q_heads, sub_keys, u_bf, tokens_per_batch):
    d, t = ht_bf.shape
    ne = u_bf.shape[0]
    bu = ne // PEER_HEADS
    dk = q_heads.shape[-1]
    tm = min(512, tokens_per_batch)
    n_blocks = PEER_NKEYS // PEER_SLABS
    dense = jax.ShapeDtypeStruct((PEER_HEADS, PEER_NKEYS, t), BF16)
    dense_spec = pl.BlockSpec((PEER_HEADS, PEER_NKEYS, tm), lambda i, j: (0, 0, i))
    blocked = jax.ShapeDtypeStruct((PEER_HEADS, n_blocks, PEER_SLABS, t), F32)
    blocked_spec = pl.BlockSpec((PEER_HEADS, n_blocks, PEER_SLABS, tm), lambda i, j: (0, 0, 0, i))
    return pl.pallas_call(
        _peer_route_act_kernel,
        grid=(t // tm, PEER_HEADS),
        in_specs=[pl.BlockSpec((d, tm), lambda i, j: (0, i)),
                  pl.BlockSpec((PEER_HEADS, tm, dk), lambda i, j: (0, i, 0)),
                  pl.BlockSpec(sub_keys.shape, lambda i, j: (0, 0, 0, 0)),
                  pl.BlockSpec((bu, d), lambda i, j: (j, 0))],
        out_specs=[pl.BlockSpec((bu, tm), lambda i, j: (j, i)),
                   blocked_spec, blocked_spec, dense_spec, dense_spec],
        out_shape=[jax.ShapeDtypeStruct((ne, t), BF16), blocked, blocked, dense, dense],
        scratch_shapes=[pltpu.VMEM((8, LANES), F32)],
        compiler_params=_cparams(("parallel", "arbitrary")),
        name="peer_route_act",
    )(ht_bf, q_heads, sub_keys, u_bf)


def _transpose_blocks_kernel(x_ref, o_ref):
    o_ref[0] = x_ref[...].T.astype(BF16)


def _transpose_blocks(w, block):
    n, d = w.shape
    return pl.pallas_call(
        _transpose_blocks_kernel,
        grid=(n // block,),
        in_specs=[pl.BlockSpec((block, d), lambda j: (j, 0))],
        out_specs=pl.BlockSpec((1, d, block), lambda j: (j, 0, 0)),
        out_shape=jax.ShapeDtypeStruct((n // block, d, block), BF16),
        compiler_params=_cparams(("parallel",)),
        name="transpose_blocks",
    )(w)


def _peer_expert_kernel(act_ref, vt_ref, pa_ref, len_ref, qb_ref, rnk_ref, x_ref, g2_ref, o_ref, acc_scr):
    j = pl.program_id(1)
    be, tm = act_ref.shape

    @pl.when(j == 0)
    def _():
        acc_scr[...] = jnp.zeros(acc_scr.shape, F32)

    n_slabs = be // PEER_NKEYS
    rows = BF16_ROWS
    n_groups = PEER_NKEYS // rows

    def row_tile(ref, hd, s):
        return jnp.broadcast_to(ref[hd, 0, s:s + 1, :], (rows, tm)).astype(BF16)

    zero = jnp.zeros((rows, tm), BF16)
    parts = [[None] * n_groups for _ in range(n_slabs)]
    slab_group = 8
    for s0 in range(0, n_slabs, slab_group):
        ss = range(s0, min(s0 + slab_group, n_slabs))
        ln = {(hd, s): row_tile(len_ref, hd, s) for hd in range(PEER_HEADS) for s in ss}
        pa = {(hd, s): row_tile(pa_ref, hd, s) for hd in range(PEER_HEADS) for s in ss}
        for g in range(n_groups):
            sl = slice(g * rows, (g + 1) * rows)
            acc = {}
            for hd in range(PEER_HEADS):
                rk = rnk_ref[hd, sl, :]
                qv = qb_ref[hd, sl, :]
                for s in ss:
                    term = jnp.where(rk < ln[hd, s], qv, zero) * pa[hd, s]
                    acc[s] = term if s not in acc else acc[s] + term
            for s in ss:
                lo = s * PEER_NKEYS + g * rows
                parts[s][g] = acc[s] * _gelu_exact(act_ref[lo:lo + rows, :].astype(F32)).astype(BF16)
    p = jnp.concatenate([parts[s][g] for s in range(n_slabs) for g in range(n_groups)], axis=0)
    rows_per_dot = 512
    for r0 in range(0, acc_scr.shape[0], rows_per_dot):
        acc_scr[r0:r0 + rows_per_dot, :] += _dot(vt_ref[0, r0:r0 + rows_per_dot, :], p)

    @pl.when(j == pl.num_programs(1) - 1)
    def _():
        o_ref[...] = x_ref[...] + g2_ref[0] * acc_scr[...].T


def _peer_expert(act, vt_bf, pa, ln, qb, rnk, x1, g2, tokens_per_batch):
    t, d = x1.shape
    ne = act.shape[0]
    tm = min(512, tokens_per_batch)
    tpb = tokens_per_batch // tm
    be = PEER_EXPERT_BLOCK
    dense_spec = pl.BlockSpec((PEER_HEADS, PEER_NKEYS, tm), lambda i, j: (0, 0, i))
    blocked_spec = pl.BlockSpec((PEER_HEADS, 1, PEER_SLABS, tm), lambda i, j: (0, j, 0, i))
    return pl.pallas_call(
        _peer_expert_kernel,
        grid=(t // tm, ne // be),
        in_specs=[pl.BlockSpec((be, tm), lambda i, j: (j, i)),
                  pl.BlockSpec((1, d, be), lambda i, j: (j, 0, 0)),
                  blocked_spec, blocked_spec, dense_spec, dense_spec,
                  pl.BlockSpec((tm, d), lambda i, j: (i, 0)),
                  pl.BlockSpec((1, 1, d), lambda i, j: (i // tpb, 0, 0))],
        out_specs=pl.BlockSpec((tm, d), lambda i, j: (i, 0)),
        out_shape=jax.ShapeDtypeStruct((t, d), F32),
        scratch_shapes=[pltpu.VMEM((d, tm), F32)],
        compiler_params=_cparams(("parallel", "arbitrary")),
        name="peer_expert",
    )(act, vt_bf, pa, ln, qb, rnk, x1, g2)


def _pad_cols(a, n):
    return jnp.pad(a, ((0, 0), (0, n - a.shape[1])))


def _pad_rows(a, n):
    return jnp.pad(a, ((0, n - a.shape[0]), (0, 0)))


def _layer(x, c, w_ada, b_ada, norm_mix_w, w_in, fox_q_norm_w, fox_k_norm_w, fox_f_bias,
           rwkv_mu, rwkv_w0, rwkv_w_up, rwkv_a0, rwkv_a_up, rwkv_g_up, rwkv_k_k, rwkv_k_a,
           rwkv_r_k, rwkv_ln_w, rwkv_ln_b, w_out, norm_ffn_w, peer_w_query, peer_sub_keys,
           peer_u, peer_v):
    B, S, D = x.shape
    T = B * S
    fw = FOX_HEADS * FOX_HEAD_DIM
    rw = RWKV_HEADS * RWKV_HEAD_DIM
    w_lora = rwkv_w_up.shape[0]
    a_lora = rwkv_a_up.shape[0]
    g_lora = rwkv_g_up.shape[0]
    assert w_lora <= LANES and a_lora + FOX_HEADS <= LANES and g_lora == 256

    c_pad = _pad_rows(c, 8)
    mod = _ada_mod(c_pad, w_ada, b_ada)[:B]
    sh1, sc1, g1, sh2, sc2, g2 = [m.reshape(B, 1, D) for m in jnp.split(mod, 6, axis=-1)]

    fox_cols = 4 * fw + FOX_HEADS
    wi_fox, wi_rw = w_in[:, :fox_cols], w_in[:, fox_cols:]
    mu = rwkv_mu.reshape(1, -1)
    seg = lambda a, lo, n: a[:, lo:lo + n]
    w_perm = jnp.concatenate([
        seg(wi_fox, 0, 4 * fw),
        seg(wi_rw, 0, 3 * rw),
        seg(wi_rw, 3 * rw + w_lora + a_lora, g_lora),
        _pad_cols(seg(wi_rw, 3 * rw, w_lora), LANES),
        _pad_cols(jnp.concatenate([seg(wi_rw, 3 * rw + w_lora, a_lora), seg(wi_fox, 4 * fw, FOX_HEADS)], 1), LANES),
    ], axis=1).astype(BF16)
    cols = {"gate": 3 * fw, "rr": 4 * fw, "rk": 4 * fw + rw, "rv": 4 * fw + 2 * rw,
            "gd": 4 * fw + 3 * rw, "lo": 4 * fw + 3 * rw + g_lora}
    f_lane = a_lora
    f_block = (cols["lo"] + LANES) // LANES
    mus = [seg(mu, 0, rw), seg(mu, rw, rw), seg(mu, 2 * rw, rw),
           seg(mu, 3 * rw + w_lora + a_lora, g_lora),
           jnp.concatenate([_pad_cols(seg(mu, 3 * rw, w_lora), LANES),
                            _pad_cols(seg(mu, 3 * rw + w_lora, a_lora), LANES)], 1)]
    scale = FOX_HEAD_DIM ** -0.5 * LOG2_E
    head_w = _pad_cols(jnp.concatenate([jnp.tile(fox_q_norm_w * scale, FOX_HEADS),
                                        jnp.tile(fox_k_norm_w, FOX_HEADS)]).reshape(1, -1), w_perm.shape[1])

    x2 = x.reshape(T, D)
    proj = _in_proj(_norm_mod(x2, norm_mix_w.reshape(1, D), sc1, sh1, S), w_perm, head_w, 2 * fw)

    f_bias_row = jnp.zeros((1, LANES), F32).at[0, f_lane:f_lane + FOX_HEADS].set(fox_f_bias)
    cum = _fox_cum(proj, f_bias_row, B, S, f_block)
    cum_rows = cum.reshape(B, S, LANES)[:, :, f_lane:f_lane + FOX_HEADS].transpose(0, 2, 1).reshape(B, FOX_HEADS, 1, S)
    o_fox = _fox_attn(proj, cum_rows, B, S)

    row = lambda a: a.reshape(1, -1)
    r, lw, k, v, a_vec, b_vec, g, bonus = _rwkv_prep(
        proj, cols, mus, row(rwkv_w0), _pad_rows(rwkv_w_up, LANES), row(rwkv_a0), _pad_rows(rwkv_a_up, LANES),
        rwkv_g_up, row(rwkv_k_k), row(rwkv_k_a), row(rwkv_r_k), S)
    y = _rwkv_scan(r, lw, k, v, a_vec, b_vec, B, S)

    x1 = _mix_out(o_fox, proj, cols["gate"], y, bonus, g, x2, g1, row(rwkv_ln_w), row(rwkv_ln_b),
                  w_out.astype(BF16), S)

    ht, q_heads = _peer_query(x1, norm_ffn_w.reshape(1, D), sc2, sh2, peer_w_query.astype(BF16), S)
    act, pa, ln, qb, rnk = _peer_route_act(ht, q_heads, peer_sub_keys, peer_u.astype(BF16), S)
    vt = _transpose_blocks(peer_v, PEER_EXPERT_BLOCK)
    out = _peer_expert(act, vt, pa, ln, qb, rnk, x1, g2, S)
    return out.reshape(B, S, D)


def kernel(x, c, w_ada, b_ada, norm_mix_w, w_in, fox_q_norm_w, fox_k_norm_w, fox_f_bias, rwkv_mu, rwkv_w0,
           rwkv_w_up, rwkv_a0, rwkv_a_up, rwkv_g_up, rwkv_k_k, rwkv_k_a, rwkv_r_k, rwkv_ln_w, rwkv_ln_b,
           w_out, norm_ffn_w, peer_w_query, peer_sub_keys, peer_u, peer_v):
    params = (w_ada, b_ada, norm_mix_w, w_in, fox_q_norm_w, fox_k_norm_w, fox_f_bias, rwkv_mu, rwkv_w0,
              rwkv_w_up, rwkv_a0, rwkv_a_up, rwkv_g_up, rwkv_k_k, rwkv_k_a, rwkv_r_k, rwkv_ln_w, rwkv_ln_b,
              w_out, norm_ffn_w, peer_w_query, peer_sub_keys, peer_u, peer_v)
    for l in range(w_ada.shape[0]):
        x = _layer(x, c, *[p[l] for p in params])
    return x
```

```python
import functools

import jax
import jax.numpy as jnp
from jax import lax
from jax.experimental import pallas as pl
from jax.experimental.pallas import tpu as pltpu

F32 = jnp.float32
BF16 = jnp.bfloat16
HIGHEST = lax.Precision.HIGHEST

LANES = 128
BF16_ROWS = 16
NORM_EPS = 1e-6
GN_EPS = 64e-5
CHUNK = 64
FOX_HEADS = 8
FOX_HEAD_DIM = 128
RWKV_HEADS = 16
RWKV_HEAD_DIM = 64
PEER_HEADS = 8
PEER_NKEYS = 128
PEER_TOPK = 16
PEER_EXPERT_BLOCK = 2048
PEER_SLABS = PEER_EXPERT_BLOCK // PEER_NKEYS
ROUTE_SUB_TOKENS = 256
NEG_BIG = -1e30
LOG2_E = 1.4426950408889634
VMEM_LIMIT = 56 * 1024 * 1024


def _cparams(sem):
    return pltpu.CompilerParams(dimension_semantics=sem, vmem_limit_bytes=VMEM_LIMIT)


def _dot(a, b, precision=None):
    return jnp.dot(a, b, preferred_element_type=F32, precision=precision)


def _dot_nt(a, b, precision=None):
    return lax.dot_general(a, b, (((1,), (1,)), ((), ())), preferred_element_type=F32,
                           precision=precision)


def _split_bf16(x):
    hi = x.astype(BF16)
    return hi, (x - hi.astype(F32)).astype(BF16)


def _dot_bf16x3(a, b):
    ah, al = _split_bf16(a)
    bh, bl = _split_bf16(b)
    return _dot(ah, bh) + (_dot(ah, bl) + _dot(al, bh))


def _dot_tn(a, b, precision=None):
    return lax.dot_general(a, b, (((0,), (0,)), ((), ())), preferred_element_type=F32,
                           precision=precision)


def _ada_kernel(c_ref, w_ref, b_ref, o_ref):
    c = c_ref[...]
    s = c * jax.nn.sigmoid(c)
    o_ref[...] = _dot(s, w_ref[...], HIGHEST) + b_ref[...]


def _ada_mod(c_pad, w_ada, b_ada):
    rows, d = c_pad.shape
    n = w_ada.shape[1]
    bn = 1024
    return pl.pallas_call(
        _ada_kernel,
        grid=(n // bn,),
        in_specs=[pl.BlockSpec((rows, d), lambda j: (0, 0)),
                  pl.BlockSpec((d, bn), lambda j: (0, j)),
                  pl.BlockSpec((1, bn), lambda j: (0, j))],
        out_specs=pl.BlockSpec((rows, bn), lambda j: (0, j)),
        out_shape=jax.ShapeDtypeStruct((rows, n), F32),
        compiler_params=_cparams(("arbitrary",)),
        name="ada_mod",
    )(c_pad, w_ada, b_ada.reshape(1, n))


def _modulated_norm(x, nw, sc, sh):
    y = x * lax.rsqrt(jnp.mean(x * x, axis=-1, keepdims=True) + NORM_EPS)
    return y * nw * (1.0 + sc) + sh


def _norm_mod_kernel(x_ref, nw_ref, sc_ref, sh_ref, o_ref):
    o_ref[...] = _modulated_norm(x_ref[...], nw_ref[...], sc_ref[0], sh_ref[0]).astype(BF16)


def _norm_mod(x2, norm_w, sc, sh, tokens_per_batch):
    t, d = x2.shape
    tm = min(512, tokens_per_batch)
    tpb = tokens_per_batch // tm
    return pl.pallas_call(
        _norm_mod_kernel,
        grid=(t // tm,),
        in_specs=[pl.BlockSpec((tm, d), lambda i: (i, 0)),
                  pl.BlockSpec((1, d), lambda i: (0, 0)),
                  pl.BlockSpec((1, 1, d), lambda i: (i // tpb, 0, 0)),
                  pl.BlockSpec((1, 1, d), lambda i: (i // tpb, 0, 0))],
        out_specs=pl.BlockSpec((tm, d), lambda i: (i, 0)),
        out_shape=jax.ShapeDtypeStruct((t, d), BF16),
        compiler_params=_cparams(("parallel",)),
        name="norm_mod",
    )(x2, norm_w, sc, sh)


def _in_proj_kernel(h_ref, w_ref, hw_ref, o_ref, *, n_qk_blocks):
    j = pl.program_id(1)
    acc = _dot(h_ref[...], w_ref[...])

    @pl.when(j < n_qk_blocks)
    def _():
        for hh in range(acc.shape[1] // FOX_HEAD_DIM):
            sl = slice(hh * FOX_HEAD_DIM, (hh + 1) * FOX_HEAD_DIM)
            a = acc[:, sl]
            rs = lax.rsqrt(jnp.mean(a * a, axis=-1, keepdims=True) + NORM_EPS)
            o_ref[:, sl] = a * rs * hw_ref[:, sl]

    @pl.when(j >= n_qk_blocks)
    def _():
        o_ref[...] = acc


def _in_proj(h_bf, w_bf, head_w, n_qk_cols):
    t, d = h_bf.shape
    n = w_bf.shape[1]
    tm = min(2048, t)
    bn = 512
    kern = functools.partial(_in_proj_kernel, n_qk_blocks=n_qk_cols // bn)
    return pl.pallas_call(
        kern,
        grid=(t // tm, n // bn),
        in_specs=[pl.BlockSpec((tm, d), lambda i, j: (i, 0)),
                  pl.BlockSpec((d, bn), lambda i, j: (0, j)),
                  pl.BlockSpec((1, bn), lambda i, j: (0, j))],
        out_specs=pl.BlockSpec((tm, bn), lambda i, j: (i, j)),
        out_shape=jax.ShapeDtypeStruct((t, n), F32),
        compiler_params=_cparams(("parallel", "arbitrary")),
        name="in_proj",
    )(h_bf, w_bf, head_w)


def _log_sigmoid(x):
    return jnp.minimum(x, 0.0) - jnp.log(1.0 + jnp.exp(-jnp.abs(x)))


def _fox_cum_kernel(f_ref, b_ref, o_ref, *, blk):
    s = f_ref.shape[0]
    row = lax.broadcasted_iota(jnp.int32, (blk, blk), 0)
    col = lax.broadcasted_iota(jnp.int32, (blk, blk), 1)
    tri = (row >= col).astype(F32)
    carry = jnp.zeros((1, f_ref.shape[1]), F32)
    for i in range(s // blk):
        lf = _log_sigmoid(f_ref[i * blk:(i + 1) * blk, :] + b_ref[...])
        cs = _dot(tri, lf, HIGHEST) + carry
        o_ref[i * blk:(i + 1) * blk, :] = cs * LOG2_E
        carry = cs[blk - 1:blk, :]


def _fox_cum(proj, f_bias_row, batch, seq, col_block):
    blk = min(256, seq)
    return pl.pallas_call(
        functools.partial(_fox_cum_kernel, blk=blk),
        grid=(batch,),
        in_specs=[pl.BlockSpec((seq, LANES), lambda b: (b, col_block)),
                  pl.BlockSpec((1, LANES), lambda b: (0, 0))],
        out_specs=pl.BlockSpec((seq, LANES), lambda b: (b, 0)),
        out_shape=jax.ShapeDtypeStruct((batch * seq, LANES), F32),
        compiler_params=_cparams(("parallel",)),
        name="fox_cum",
    )(proj, f_bias_row)


def _fox_attn_kernel(qt_ref, kt_ref, q_ref, k_ref, v_ref, c_ref, o_ref, m_scr, l_scr, acc_scr):
    t = pl.program_id(2)
    qi = qt_ref[t]
    ki = kt_ref[t]
    hd = FOX_HEAD_DIM
    n_heads = q_ref.shape[1] // hd

    @pl.when(ki == 0)
    def _():
        m_scr[...] = jnp.full(m_scr.shape, NEG_BIG, F32)
        l_scr[...] = jnp.zeros(l_scr.shape, F32)
        acc_scr[...] = jnp.zeros(acc_scr.shape, F32)

    def step(masked):
        hs = range(n_heads)
        sl = [slice(h * hd, (h + 1) * hd) for h in hs]
        s = [_dot_nt(q_ref[:, sl[h]].astype(BF16), k_ref[:, sl[h]].astype(BF16)) - c_ref[0, h] for h in hs]
        if masked:
            row = lax.broadcasted_iota(jnp.int32, s[0].shape, 0)
            col = lax.broadcasted_iota(jnp.int32, s[0].shape, 1)
            s = [jnp.where(row >= col, x, NEG_BIG) for x in s]
        m_prev = [m_scr[h] for h in hs]
        m_new = [jnp.maximum(m_prev[h], jnp.max(s[h], axis=-1, keepdims=True)) for h in hs]
        n_rep = s[0].shape[1] // hd
        p = [jnp.exp2(s[h] - jnp.concatenate([m_new[h]] * n_rep, axis=1)) for h in hs]
        alpha = [jnp.exp2(m_prev[h] - m_new[h]) for h in hs]
        ones = jnp.ones((k_ref.shape[0], hd), BF16)
        pv = [_dot(p[h].astype(BF16), jnp.concatenate([v_ref[:, sl[h]].astype(BF16), ones], axis=1))
              for h in hs]
        for h in hs:
            l_scr[h] = alpha[h] * l_scr[h] + pv[h][:, hd:]
            acc_scr[:, sl[h]] = alpha[h] * acc_scr[:, sl[h]] + pv[h][:, :hd]
            m_scr[h] = m_new[h]

    @pl.when(ki < qi)
    def _():
        step(False)

    @pl.when(ki == qi)
    def _():
        step(True)
        for h in range(n_heads):
            sl = slice(h * hd, (h + 1) * hd)
            o_ref[:, sl] = acc_scr[:, sl] / l_scr[h]


def _fox_attn(proj, cum_rows, batch, seq):
    tq = min(512, seq)
    nq = seq // tq
    hps = 8
    w = hps * FOX_HEAD_DIM
    kcol = FOX_HEADS // hps
    tri = [(q, k) for q in range(nq) for k in range(q + 1)]
    qt = jnp.asarray([q for q, _ in tri], jnp.int32)
    kt = jnp.asarray([k for _, k in tri], jnp.int32)
    grid_spec = pltpu.PrefetchScalarGridSpec(
        num_scalar_prefetch=2,
        grid=(batch, kcol, len(tri)),
        in_specs=[
            pl.BlockSpec((tq, w), lambda b, h, t, qt, kt: (b * nq + qt[t], h)),
            pl.BlockSpec((tq, w), lambda b, h, t, qt, kt: (b * nq + kt[t], kcol + h)),
            pl.BlockSpec((tq, w), lambda b, h, t, qt, kt: (b * nq + kt[t], 2 * kcol + h)),
            pl.BlockSpec((1, hps, 1, tq), lambda b, h, t, qt, kt: (b, h, 0, kt[t])),
        ],
        out_specs=pl.BlockSpec((tq, w), lambda b, h, t, qt, kt: (b * nq + qt[t], h)),
        scratch_shapes=[pltpu.VMEM((hps, tq, FOX_HEAD_DIM), F32), pltpu.VMEM((hps, tq, FOX_HEAD_DIM), F32),
                        pltpu.VMEM((tq, w), F32)],
    )
    return pl.pallas_call(
        _fox_attn_kernel,
        grid_spec=grid_spec,
        out_shape=jax.ShapeDtypeStruct((batch * seq, FOX_HEADS * FOX_HEAD_DIM), F32),
        compiler_params=_cparams(("parallel", "parallel", "arbitrary")),
        name="fox_attn",
    )(qt, kt, proj, proj, proj, cum_rows)


def _head_sum(x, bd):
    hi, lo = _split_bf16(x)
    parts = [_dot(hi[:, j * LANES:(j + 1) * LANES], bd) + _dot(lo[:, j * LANES:(j + 1) * LANES], bd)
             for j in range(x.shape[1] // LANES)]
    return jnp.concatenate(parts, axis=1)


def _head_block_diag():
    r = lax.broadcasted_iota(jnp.int32, (LANES, LANES), 0) // RWKV_HEAD_DIM
    c = lax.broadcasted_iota(jnp.int32, (LANES, LANES), 1) // RWKV_HEAD_DIM
    return (r == c).astype(BF16)


def _rwkv_prep_kernel(r_ref, k_ref, v_ref, gd_ref, lo_ref, pr_ref, pk_ref, pv_ref, pgd_ref, plo_ref,
                      mu_r, mu_k, mu_v, mu_gd, mu_lo, w0_ref, wup_ref, a0_ref, aup_ref, gup_ref,
                      kk_ref, ka_ref, rk_ref,
                      or_ref, olw_ref, ok_ref, ov_ref, oa_ref, ob_ref, og_ref, obonus_ref, *, tiles_per_batch):
    i = pl.program_id(0)
    first = (i % tiles_per_batch) == 0

    def shifted(cur_ref, prv_ref, mu_ref):
        cur = cur_ref[...]
        last = jnp.where(first, 0.0, prv_ref[7:8, :])
        row = lax.broadcasted_iota(jnp.int32, cur.shape, 0)
        prev = jnp.where(row == 0, last, pltpu.roll(cur, 1, 0))
        return cur + (prev - cur) * mu_ref[...]

    r = shifted(r_ref, pr_ref, mu_r)
    k = shifted(k_ref, pk_ref, mu_k)
    v = shifted(v_ref, pv_ref, mu_v)
    gd = shifted(gd_ref, pgd_ref, mu_gd)
    lo = shifted(lo_ref, plo_ref, mu_lo)
    wd = lo[:, :LANES]
    ad = lo[:, LANES:]

    w_pre = w0_ref[...] + _dot_bf16x3(jnp.tanh(wd), wup_ref[...])
    w_raw = _log_sigmoid(w_pre) - 0.5
    log_decay = -jnp.exp(w_raw)
    a = jax.nn.sigmoid(a0_ref[...] + _dot_bf16x3(ad, aup_ref[...]))
    g = _dot_bf16x3(jax.nn.sigmoid(gd), gup_ref[...])

    bd = _head_block_diag()
    kk = k * kk_ref[...]
    nrm = jnp.maximum(jnp.sqrt(_head_sum(kk * kk, bd)), 1e-12)
    kk = kk / nrm
    k_mod = k * (1.0 + (a - 1.0) * ka_ref[...])
    bonus = _head_sum(r * k_mod * rk_ref[...], bd) * v

    or_ref[...] = r.astype(BF16)
    olw_ref[...] = log_decay
    ok_ref[...] = k_mod.astype(BF16)
    ov_ref[...] = v.astype(BF16)
    oa_ref[...] = (-kk).astype(BF16)
    ob_ref[...] = (kk * a).astype(BF16)
    og_ref[...] = g.astype(BF16)
    obonus_ref[...] = bonus.astype(BF16)


def _rwkv_prep(proj, cols, mus, w0, w_up, a0, a_up, g_up, k_k, k_a, r_k, tokens_per_batch):
    t = proj.shape[0]
    w = RWKV_HEADS * RWKV_HEAD_DIM
    tm = min(512, tokens_per_batch)
    tpb = tokens_per_batch // tm
    widths = [w, w, w, 256, 256]
    offs = [cols["rr"], cols["rk"], cols["rv"], cols["gd"], cols["lo"]]
    cur_specs = [pl.BlockSpec((tm, wd), functools.partial(lambda i, cb: (i, cb), cb=o // wd))
                 for wd, o in zip(widths, offs)]
    prv_specs = [pl.BlockSpec((8, wd), functools.partial(
        lambda i, cb: (jnp.maximum(i * (tm // 8) - 1, 0), cb), cb=o // wd))
        for wd, o in zip(widths, offs)]
    full = lambda a: pl.BlockSpec(a.shape, lambda i: (0,) * a.ndim)
    params = list(mus) + [w0, w_up, a0, a_up, g_up, k_k, k_a, r_k]
    out_spec = pl.BlockSpec((tm, w), lambda i: (i, 0))
    return pl.pallas_call(
        functools.partial(_rwkv_prep_kernel, tiles_per_batch=tpb),
        grid=(t // tm,),
        in_specs=cur_specs + prv_specs + [full(p) for p in params],
        out_specs=[out_spec] * 8,
        out_shape=[jax.ShapeDtypeStruct((t, w), dt) for dt in (BF16, F32, BF16, BF16, BF16, BF16, BF16, BF16)],
        compiler_params=_cparams(("parallel",)),
        name="rwkv_prep",
    )(*([proj] * 10), *params)


def _rwkv_scan_kernel(r_ref, lw_ref, k_ref, v_ref, a_ref, b_ref, o_ref, h_scr):
    c = pl.program_id(1)

    @pl.when(c == 0)
    def _():
        h_scr[...] = jnp.zeros(h_scr.shape, F32)

    C = r_ref.shape[0]
    n_pairs = r_ref.shape[1] // LANES
    P = HIGHEST

    row = lax.broadcasted_iota(jnp.int32, (C, C), 0)
    col = lax.broadcasted_iota(jnp.int32, (C, C), 1)
    tri = (row >= col).astype(F32)
    lw = lw_ref[...]
    cw = _dot(tri, lw, P)
    cw_end = cw[C - 1:C, :]
    e_pos = jnp.exp(cw)
    e_prev = jnp.exp(cw - lw)
    e_neg = jnp.exp(-cw)
    e_end = jnp.exp(cw_end - cw)
    w_end = jnp.exp(cw_end)

    a = a_ref[...].astype(F32)
    b = b_ref[...].astype(F32)
    k = k_ref[...].astype(F32)
    r = r_ref[...].astype(F32)
    at = a * e_prev
    bt = b * e_neg
    kt = k * e_neg
    rt = r * e_pos
    bh = b * e_end
    kh = k * e_end
    v = v_ref[...].astype(F32)

    lane = lax.broadcasted_iota(jnp.int32, (C, LANES), 1)
    head0 = lane < RWKV_HEAD_DIM
    r2 = lax.broadcasted_iota(jnp.int32, (2 * C, 2 * C), 0)
    c2 = lax.broadcasted_iota(jnp.int32, (2 * C, 2 * C), 1)
    same = (r2 // C) == (c2 // C)
    strict = same & (r2 > c2)
    incl = same & (r2 >= c2)
    eye = (r2 == c2).astype(F32)

    def two(x, p):
        xp = x[:, p * LANES:(p + 1) * LANES]
        return jnp.concatenate([jnp.where(head0, xp, 0.0), jnp.where(head0, 0.0, xp)], axis=0)

    def mm(x, y):
        return _dot(x.astype(BF16), y.astype(BF16))

    def mm3(x, y):
        xh, xl = _split_bf16(x)
        yh, yl = _split_bf16(y)
        n = y.shape[1]
        wide = _dot(xh, jnp.concatenate([yh, yl], axis=1))
        return wide[:, :n] + (wide[:, n:] + _dot(xl, yh))

    G = 2 * C
    pairs = range(n_pairs)
    at2 = [two(at, p) for p in pairs]
    rt2 = [two(rt, p) for p in pairs]
    v2 = [two(v, p).astype(BF16) for p in pairs]
    gram = [_dot_nt(jnp.concatenate([at2[p], rt2[p]], axis=0).astype(BF16),
                    jnp.concatenate([two(bt, p), two(kt, p)], axis=0).astype(BF16)) for p in pairs]
    a_ab = [jnp.where(strict, gram[p][:G, :G], 0.0) for p in pairs]
    nmat = list(a_ab)
    pw = list(a_ab)
    for _ in range(max(1, (C - 1).bit_length() - 1)):
        pw = [mm3(pw[p], pw[p]) for p in pairs]
        nmat = [nmat[p] + pw[p] + mm3(nmat[p], pw[p]) for p in pairs]
    akv = [mm(jnp.where(strict, gram[p][:G, G:], 0.0), v2[p]) for p in pairs]
    rhs = [jnp.concatenate([at2[p], akv[p]], axis=1) for p in pairs]
    pq = [(rhs[p] + mm(nmat[p], rhs[p])).astype(BF16) for p in pairs]
    ry = [mm(jnp.where(incl, gram[p][G:, :G], 0.0), pq[p]) for p in pairs]
    mv = [mm(jnp.where(incl, gram[p][G:, G:], 0.0), v2[p]) for p in pairs]
    ge = [_dot_tn(two(bh, p).astype(BF16), pq[p]) for p in pairs]
    kv = [_dot_tn(two(kh, p).astype(BF16), v2[p]) for p in pairs]
    for p in pairs:
        rr = rt2[p] + ry[p][:, :LANES]
        gm = eye * w_end[:, p * LANES:(p + 1) * LANES] + ge[p][:, :LANES]
        yh = mm(jnp.concatenate([rr, gm], axis=0), h_scr[p])
        y2 = yh[:G] + ry[p][:, LANES:] + mv[p]
        h_scr[p] = yh[G:] + ge[p][:, LANES:] + kv[p]
        o_ref[:, p * LANES:(p + 1) * LANES] = y2[:C, :] + y2[C:, :]


def _rwkv_scan(r, lw, k, v, a, b, batch, seq):
    w = r.shape[1]
    nc = seq // CHUNK
    spec = pl.BlockSpec((CHUNK, w), lambda bi, ci: (bi * nc + ci, 0))
    return pl.pallas_call(
        _rwkv_scan_kernel,
        grid=(batch, nc),
        in_specs=[spec] * 6,
        out_specs=spec,
        out_shape=jax.ShapeDtypeStruct((batch * seq, w), F32),
        scratch_shapes=[pltpu.VMEM((w // LANES, LANES, LANES), F32)],
        compiler_params=_cparams(("parallel", "arbitrary")),
        name="rwkv_scan",
    )(r, lw, k, v, a, b)


def _mix_out_kernel(o_ref, gate_ref, y_ref, bonus_ref, g_ref, x_ref, g1_ref, lnw_ref, lnb_ref, w_ref, out_ref):
    fox = o_ref[...] * jax.nn.sigmoid(gate_ref[...])
    bd = _head_block_diag()
    y = y_ref[...]
    inv_n = 1.0 / RWKV_HEAD_DIM
    mean = _head_sum(y, bd) * inv_n
    d = y - mean
    var = _head_sum(d * d, bd) * inv_n
    yn = d * lax.rsqrt(var + GN_EPS) * lnw_ref[...] + lnb_ref[...]
    rw = (yn + bonus_ref[...].astype(F32)) * g_ref[...].astype(F32)
    wf = fox.shape[1]
    mix = _dot(fox.astype(BF16), w_ref[:wf, :]) + _dot(rw.astype(BF16), w_ref[wf:, :])
    out_ref[...] = x_ref[...] + g1_ref[0] * mix


def _mix_out(o_fox, proj, gate_col, y, bonus, g, x2, g1, ln_w, ln_b, w_out_bf, tokens_per_batch):
    t, d = x2.shape
    wf = o_fox.shape[1]
    wr = y.shape[1]
    tm = min(512, tokens_per_batch)
    tpb = tokens_per_batch // tm
    return pl.pallas_call(
        _mix_out_kernel,
        grid=(t // tm,),
        in_specs=[pl.BlockSpec((tm, wf), lambda i: (i, 0)),
                  pl.BlockSpec((tm, wf), lambda i: (i, gate_col // wf)),
                  pl.BlockSpec((tm, wr), lambda i: (i, 0)),
                  pl.BlockSpec((tm, wr), lambda i: (i, 0)),
                  pl.BlockSpec((tm, wr), lambda i: (i, 0)),
                  pl.BlockSpec((tm, d), lambda i: (i, 0)),
                  pl.BlockSpec((1, 1, d), lambda i: (i // tpb, 0, 0)),
                  pl.BlockSpec((1, wr), lambda i: (0, 0)),
                  pl.BlockSpec((1, wr), lambda i: (0, 0)),
                  pl.BlockSpec(w_out_bf.shape, lambda i: (0, 0))],
        out_specs=pl.BlockSpec((tm, d), lambda i: (i, 0)),
        out_shape=jax.ShapeDtypeStruct((t, d), F32),
        compiler_params=_cparams(("parallel",)),
        name="mix_out",
    )(o_fox, proj, y, bonus, g, x2, g1, ln_w, ln_b, w_out_bf)


def _top_k_mask_rows(s, k, tie_safe, want_rank=True):
    n = s.shape[0]
    s0 = s
    row = lax.broadcasted_iota(jnp.int32, s.shape, 0).astype(F32)
    rank = jnp.full(s.shape, float(k), F32) if want_rank else None
    vals = []
    for r in range(k):
        m = jnp.max(s, axis=0, keepdims=True)
        hit = s == m
        if tie_safe:
            pos = jnp.min(jnp.where(hit, row, float(n)), axis=0, keepdims=True)
            hit = row == pos
        vals.append(m)
        if want_rank:
            rank = jnp.where(hit, float(r), rank)
        s = jnp.where(hit, -jnp.inf, s)
    picked = (s == -jnp.inf) & (s0 != -jnp.inf)
    n_picked = jnp.sum(jnp.where(picked, 1.0, 0.0), axis=0, keepdims=True)
    return jnp.concatenate(vals, axis=0), rank, picked, jnp.where(n_picked == float(k), 1.0, 0.0)


def _candidate_rows():
    k = PEER_TOPK
    groups, valid = [], []
    for r0 in range(k // 2):
        n1 = k // (r0 + 1)
        for g in range(-(-n1 // 8)):
            groups.append((r0, g * 8))
            valid.append([g * 8 + i < n1 for i in range(8)])
    groups.append((None, k // 2))
    valid.append([True] * 8)
    return groups, valid


def _route_head(sa, sb, tie_safe):
    k = PEER_TOPK
    tm = sa.shape[1]
    groups, valid = _candidate_rows()
    sub8 = lax.broadcasted_iota(jnp.int32, (8, tm), 0)
    top_a, rank_a, _, clean_a = _top_k_mask_rows(sa, k, tie_safe)
    top_b, rank_b, _, clean_b = _top_k_mask_rows(sb, k, tie_safe)
    parts = []
    for (r0, off), ok in zip(groups, valid):
        if r0 is None:
            part = top_a[off:off + 8, :] + top_b[0:1, :]
        else:
            part = top_a[r0:r0 + 1, :] + top_b[off:off + 8, :]
        if not all(ok):
            part = jnp.where(sub8 < sum(ok), part, -jnp.inf)
        parts.append(part)
    cand = jnp.concatenate(parts, axis=0)
    _, _, sel, clean_c = _top_k_mask_rows(cand, k, tie_safe, want_rank=False)
    z = jnp.sum(jnp.where(sel, jnp.exp(cand - cand[0:1, :]), 0.0), axis=0, keepdims=True)
    self32 = jnp.where(sel, 1.0, 0.0)
    n_keys = sa.shape[0]
    rank3 = rank_a.astype(BF16).reshape(n_keys // BF16_ROWS, BF16_ROWS, tm)
    length3 = jnp.zeros(rank3.shape, BF16)

    def add_count(r0, count_row):
        tile = jnp.broadcast_to(count_row, (BF16_ROWS, tm)).astype(BF16)
        return length3 + jnp.where(rank3 == float(r0), tile[None], 0.0).astype(BF16)

    tail = None
    per_rank = {}
    for gi, (r0, off) in enumerate(groups):
        cnt = self32[gi * 8:(gi + 1) * 8, :]
        if r0 is None:
            tail = cnt
        else:
            c = jnp.sum(cnt, axis=0, keepdims=True)
            per_rank[r0] = c if r0 not in per_rank else per_rank[r0] + c
    for r in range(8):
        per_rank[k // 2 + r] = tail[r:r + 1, :]
    for r0, c in per_rank.items():
        length3 = add_count(r0, c)
    length = length3.reshape(n_keys, tm).astype(F32)
    pa = jnp.exp(sa - top_a[0:1, :])
    qb = jnp.exp(sb - top_b[0:1, :]) / z
    return pa, length, qb, rank_b, clean_a * clean_b * clean_c


def _peer_query_kernel(x_ref, nw_ref, sc_ref, sh_ref, wq_ref, ht_ref, q_ref):
    h = _modulated_norm(x_ref[...], nw_ref[...], sc_ref[0], sh_ref[0])
    ht_ref[...] = h.T.astype(BF16)
    q = _dot(h.astype(BF16), wq_ref[...])
    dk = q.shape[1] // PEER_HEADS
    for hd in range(PEER_HEADS):
        q_ref[hd] = q[:, hd * dk:(hd + 1) * dk]


def _peer_query(x1, norm_w, sc, sh, wq_bf, tokens_per_batch):
    t, d = x1.shape
    dk = wq_bf.shape[1] // PEER_HEADS
    tm = min(512, tokens_per_batch)
    tpb = tokens_per_batch // tm
    return pl.pallas_call(
        _peer_query_kernel,
        grid=(t // tm,),
        in_specs=[pl.BlockSpec((tm, d), lambda i: (i, 0)),
                  pl.BlockSpec((1, d), lambda i: (0, 0)),
                  pl.BlockSpec((1, 1, d), lambda i: (i // tpb, 0, 0)),
                  pl.BlockSpec((1, 1, d), lambda i: (i // tpb, 0, 0)),
                  pl.BlockSpec(wq_bf.shape, lambda i: (0, 0))],
        out_specs=[pl.BlockSpec((d, tm), lambda i: (0, i)),
                   pl.BlockSpec((PEER_HEADS, tm, dk), lambda i: (0, i, 0))],
        out_shape=[jax.ShapeDtypeStruct((d, t), BF16),
                   jax.ShapeDtypeStruct((PEER_HEADS, t, dk), F32)],
        compiler_params=_cparams(("parallel",)),
        name="peer_query",
    )(x1, norm_w, sc, sh, wq_bf)


def _gelu_exact(x):
    return 0.5 * x * (1.0 + lax.erf(x * 0.7071067811865476))


def _peer_route_act_kernel(ht_ref, q_ref, keys_ref, u_ref,
                           act_ref, pa_ref, len_ref, qb_ref, rnk_ref, code_scr):
    hd = pl.program_id(1)
    tm = ht_ref.shape[1]
    half = keys_ref.shape[-1]
    sub = min(ROUTE_SUB_TOKENS, tm)
    blocked = (PEER_NKEYS // PEER_SLABS, PEER_SLABS, sub)

    @pl.when(hd == 0)
    def _():
        code_scr[...] = jnp.zeros(code_scr.shape, F32)

    def scores(h, t0):
        q = q_ref[h, t0:t0 + sub, :]
        return (_dot_nt(keys_ref[h, 0], q[:, :half], HIGHEST),
                _dot_nt(keys_ref[h, 1], q[:, half:], HIGHEST))

    def route(h, t0, sa, sb, tie_safe):
        pa, length, qbt, rank_b, clean = _route_head(sa, sb, tie_safe)
        pa_ref[h, :, :, t0:t0 + sub] = pa.reshape(blocked)
        len_ref[h, :, :, t0:t0 + sub] = length.reshape(blocked)
        qb_ref[h, :, t0:t0 + sub] = qbt.astype(BF16)
        rnk_ref[h, :, t0:t0 + sub] = rank_b.astype(BF16)
        return clean

    starts = range(0, tm, sub)
    score_tiles = [scores(hd, t0) for t0 in starts]
    rows_per_dot = 512
    for r0 in range(0, u_ref.shape[0], rows_per_dot):
        act_ref[r0:r0 + rows_per_dot, :] = _dot(u_ref[r0:r0 + rows_per_dot, :], ht_ref[...]).astype(BF16)

    dirty = jnp.zeros((1, 1), F32)
    for t0, (sa, sb) in zip(starts, score_tiles):
        dirty = jnp.maximum(dirty, 1.0 - jnp.min(route(hd, t0, sa, sb, False), axis=1, keepdims=True))
    code_scr[0:1, 0:1] += dirty * lax.shift_left(1, hd).astype(F32)

    @pl.when(hd == pl.num_programs(1) - 1)
    def _():
        tie_bits = code_scr[0, 0].astype(jnp.int32)

        def redo(h, carry):
            @pl.when(((tie_bits >> h) & 1) == 1)
            def _():
                for t0 in starts:
                    route(h, t0, *scores(h, t0), True)
            return carry

        lax.fori_loop(0, PEER_HEADS, redo, 0)


def _peer_route_act(ht_bf, q_heads, sub_keys, u_bf, tokens_per_batch):
    d, t = ht_bf.shape
    ne = u_bf.shape[0]
    bu = ne // PEER_HEADS
    dk = q_heads.shape[-1]
    tm = min(512, tokens_per_batch)
    n_blocks = PEER_NKEYS // PEER_SLABS
    dense = jax.ShapeDtypeStruct((PEER_HEADS, PEER_NKEYS, t), BF16)
    dense_spec = pl.BlockSpec((PEER_HEADS, PEER_NKEYS, tm), lambda i, j: (0, 0, i))
    blocked = jax.ShapeDtypeStruct((PEER_HEADS, n_blocks, PEER_SLABS, t), F32)
    blocked_spec = pl.BlockSpec((PEER_HEADS, n_blocks, PEER_SLABS, tm), lambda i, j: (0, 0, 0, i))
    return pl.pallas_call(
        _peer_route_act_kernel,
        grid=(t // tm, PEER_HEADS),
        in_specs=[pl.BlockSpec((d, tm), lambda i, j: (0, i)),
                  pl.BlockSpec((PEER_HEADS, tm, dk), lambda i, j: (0, i, 0)),
                  pl.BlockSpec(sub_keys.shape, lambda i, j: (0, 0, 0, 0)),
                  pl.BlockSpec((bu, d), lambda i, j: (j, 0))],
        out_specs=[pl.BlockSpec((bu, tm), lambda i, j: (j, i)),
                   blocked_spec, blocked_spec, dense_spec, dense_spec],
        out_shape=[jax.ShapeDtypeStruct((ne, t), BF16), blocked, blocked, dense, dense],
        scratch_shapes=[pltpu.VMEM((8, LANES), F32)],
        compiler_params=_cparams(("parallel", "arbitrary")),
        name="peer_route_act",
    )(ht_bf, q_heads, sub_keys, u_bf)


def _transpose_blocks_kernel(x_ref, o_ref):
    o_ref[0] = x_ref[...].T.astype(BF16)


def _transpose_blocks(w, block):
    n, d = w.shape
    return pl.pallas_call(
        _transpose_blocks_kernel,
        grid=(n // block,),
        in_specs=[pl.BlockSpec((block, d), lambda j: (j, 0))],
        out_specs=pl.BlockSpec((1, d, block), lambda j: (j, 0, 0)),
        out_shape=jax.ShapeDtypeStruct((n // block, d, block), BF16),
        compiler_params=_cparams(("parallel",)),
        name="transpose_blocks",
    )(w)


def _peer_expert_kernel(act_ref, vt_ref, pa_ref, len_ref, qb_ref, rnk_ref, x_ref, g2_ref, o_ref, acc_scr):
    j = pl.program_id(1)
    be, tm = act_ref.shape

    @pl.when(j == 0)
    def _():
        acc_scr[...] = jnp.zeros(acc_scr.shape, F32)

    n_slabs = be // PEER_NKEYS
    rows = BF16_ROWS
    n_groups = PEER_NKEYS // rows

    def row_tile(ref, hd, s):
        return jnp.broadcast_to(ref[hd, 0, s:s + 1, :], (rows, tm)).astype(BF16)

    zero = jnp.zeros((rows, tm), BF16)
    parts = [[None] * n_groups for _ in range(n_slabs)]
    slab_group = 8
    for s0 in range(0, n_slabs, slab_group):
        ss = range(s0, min(s0 + slab_group, n_slabs))
        ln = {(hd, s): row_tile(len_ref, hd, s) for hd in range(PEER_HEADS) for s in ss}
        pa = {(hd, s): row_tile(pa_ref, hd, s) for hd in range(PEER_HEADS) for s in ss}
        for g in range(n_groups):
            sl = slice(g * rows, (g + 1) * rows)
            acc = {}
            for hd in range(PEER_HEADS):
                rk = rnk_ref[hd, sl, :]
                qv = qb_ref[hd, sl, :]
                for s in ss:
                    term = jnp.where(rk < ln[hd, s], qv, zero) * pa[hd, s]
                    acc[s] = term if s not in acc else acc[s] + term
            for s in ss:
                lo = s * PEER_NKEYS + g * rows
                parts[s][g] = acc[s] * _gelu_exact(act_ref[lo:lo + rows, :].astype(F32)).astype(BF16)
    p = jnp.concatenate([parts[s][g] for s in range(n_slabs) for g in range(n_groups)], axis=0)
    acc_scr[...] += _dot(vt_ref[0], p)

    @pl.when(j == pl.num_programs(1) - 1)
    def _():
        o_ref[...] = x_ref[...] + g2_ref[0] * acc_scr[...].T


def _peer_expert(act, vt_bf, pa, ln, qb, rnk, x1, g2, tokens_per_batch):
    t, d = x1.shape
    ne = act.shape[0]
    tm = min(512, tokens_per_batch)
    tpb = tokens_per_batch // tm
    be = PEER_EXPERT_BLOCK
    dense_spec = pl.BlockSpec((PEER_HEADS, PEER_NKEYS, tm), lambda i, j: (0, 0, i))
    blocked_spec = pl.BlockSpec((PEER_HEADS, 1, PEER_SLABS, tm), lambda i, j: (0, j, 0, i))
    return pl.pallas_call(
        _peer_expert_kernel,
        grid=(t // tm, ne // be),
        in_specs=[pl.BlockSpec((be, tm), lambda i, j: (j, i)),
                  pl.BlockSpec((1, d, be), lambda i, j: (j, 0, 0)),
                  blocked_spec, blocked_spec, dense_spec, dense_spec,
                  pl.BlockSpec((tm, d), lambda i, j: (i, 0)),
                  pl.BlockSpec((1, 1, d), lambda i, j: (i // tpb, 0, 0))],
        out_specs=pl.BlockSpec((tm, d), lambda i, j: (i, 0)),
        out_shape=jax.ShapeDtypeStruct((t, d), F32),
        scratch_shapes=[pltpu.VMEM((d, tm), F32)],
        compiler_params=_cparams(("parallel", "arbitrary")),
        name="peer_expert",
    )(act, vt_bf, pa, ln, qb, rnk, x1, g2)


def _pad_cols(a, n):
    return jnp.pad(a, ((0, 0), (0, n - a.shape[1])))


def _pad_rows(a, n):
    return jnp.pad(a, ((0, n - a.shape[0]), (0, 0)))


def _layer(x, c, w_ada, b_ada, norm_mix_w, w_in, fox_q_norm_w, fox_k_norm_w, fox_f_bias,
           rwkv_mu, rwkv_w0, rwkv_w_up, rwkv_a0, rwkv_a_up, rwkv_g_up, rwkv_k_k, rwkv_k_a,
           rwkv_r_k, rwkv_ln_w, rwkv_ln_b, w_out, norm_ffn_w, peer_w_query, peer_sub_keys,
           peer_u, peer_v):
    B, S, D = x.shape
    T = B * S
    fw = FOX_HEADS * FOX_HEAD_DIM
    rw = RWKV_HEADS * RWKV_HEAD_DIM
    w_lora = rwkv_w_up.shape[0]
    a_lora = rwkv_a_up.shape[0]
    g_lora = rwkv_g_up.shape[0]
    assert w_lora <= LANES and a_lora + FOX_HEADS <= LANES and g_lora == 256

    c_pad = _pad_rows(c, 8)
    mod = _ada_mod(c_pad, w_ada, b_ada)[:B]
    sh1, sc1, g1, sh2, sc2, g2 = [m.reshape(B, 1, D) for m in jnp.split(mod, 6, axis=-1)]

    fox_cols = 4 * fw + FOX_HEADS
    wi_fox, wi_rw = w_in[:, :fox_cols], w_in[:, fox_cols:]
    mu = rwkv_mu.reshape(1, -1)
    seg = lambda a, lo, n: a[:, lo:lo + n]
    w_perm = jnp.concatenate([
        seg(wi_fox, 0, 4 * fw),
        seg(wi_rw, 0, 3 * rw),
        seg(wi_rw, 3 * rw + w_lora + a_lora, g_lora),
        _pad_cols(seg(wi_rw, 3 * rw, w_lora), LANES),
        _pad_cols(jnp.concatenate([seg(wi_rw, 3 * rw + w_lora, a_lora), seg(wi_fox, 4 * fw, FOX_HEADS)], 1), LANES),
    ], axis=1).astype(BF16)
    cols = {"gate": 3 * fw, "rr": 4 * fw, "rk": 4 * fw + rw, "rv": 4 * fw + 2 * rw,
            "gd": 4 * fw + 3 * rw, "lo": 4 * fw + 3 * rw + g_lora}
    f_lane = a_lora
    f_block = (cols["lo"] + LANES) // LANES
    mus = [seg(mu, 0, rw), seg(mu, rw, rw), seg(mu, 2 * rw, rw),
           seg(mu, 3 * rw + w_lora + a_lora, g_lora),
           jnp.concatenate([_pad_cols(seg(mu, 3 * rw, w_lora), LANES),
                            _pad_cols(seg(mu, 3 * rw + w_lora, a_lora), LANES)], 1)]
    scale = FOX_HEAD_DIM ** -0.5 * LOG2_E
    head_w = _pad_cols(jnp.concatenate([jnp.tile(fox_q_norm_w * scale, FOX_HEADS),
                                        jnp.tile(fox_k_norm_w, FOX_HEADS)]).reshape(1, -1), w_perm.shape[1])

    x2 = x.reshape(T, D)
    proj = _in_proj(_norm_mod(x2, norm_mix_w.reshape(1, D), sc1, sh1, S), w_perm, head_w, 2 * fw)

    f_bias_row = jnp.zeros((1, LANES), F32).at[0, f_lane:f_lane + FOX_HEADS].set(fox_f_bias)
    cum = _fox_cum(proj, f_bias_row, B, S, f_block)
    cum_rows = cum.reshape(B, S, LANES)[:, :, f_lane:f_lane + FOX_HEADS].transpose(0, 2, 1).reshape(B, FOX_HEADS, 1, S)
    o_fox = _fox_attn(proj, cum_rows, B, S)

    row = lambda a: a.reshape(1, -1)
    r, lw, k, v, a_vec, b_vec, g, bonus = _rwkv_prep(
        proj, cols, mus, row(rwkv_w0), _pad_rows(rwkv_w_up, LANES), row(rwkv_a0), _pad_rows(rwkv_a_up, LANES),
        rwkv_g_up, row(rwkv_k_k), row(rwkv_k_a), row(rwkv_r_k), S)
    y = _rwkv_scan(r, lw, k, v, a_vec, b_vec, B, S)

    x1 = _mix_out(o_fox, proj, cols["gate"], y, bonus, g, x2, g1, row(rwkv_ln_w), row(rwkv_ln_b),
                  w_out.astype(BF16), S)

    ht, q_heads = _peer_query(x1, norm_ffn_w.reshape(1, D), sc2, sh2, peer_w_query.astype(BF16), S)
    act, pa, ln, qb, rnk = _peer_route_act(ht, q_heads, peer_sub_keys, peer_u.astype(BF16), S)
    vt = _transpose_blocks(peer_v, PEER_EXPERT_BLOCK)
    out = _peer_expert(act, vt, pa, ln, qb, rnk, x1, g2, S)
    return out.reshape(B, S, D)


def kernel(x, c, w_ada, b_ada, norm_mix_w, w_in, fox_q_norm_w, fox_k_norm_w, fox_f_bias, rwkv_mu, rwkv_w0,
           rwkv_w_up, rwkv_a0, rwkv_a_up, rwkv_g_up, rwkv_k_k, rwkv_k_a, rwkv_r_k, rwkv_ln_w, rwkv_ln_b,
           w_out, norm_ffn_w, peer_w_query, peer_sub_keys, peer_u, peer_v):
    params = (w_ada, b_ada, norm_mix_w, w_in, fox_q_norm_w, fox_k_norm_w, fox_f_bias, rwkv_mu, rwkv_w0,
              rwkv_w_up, rwkv_a0, rwkv_a_up, rwkv_g_up, rwkv_k_k, rwkv_k_a, rwkv_r_k, rwkv_ln_w, rwkv_ln_b,
              w_out, norm_ffn_w, peer_w_query, peer_sub_keys, peer_u, peer_v)
    for l in range(w_ada.shape[0]):
        x = _layer(x, c, *[p[l] for p in params])
    return x
```
